```python
import jax, jax.numpy as jnp
from jax import lax
import numpy as np

D_MODEL = 1024
BATCH = 8
SEQ = 4096
DEPTH = 4

D_FF = 2816
FFN_RESID = 0.5
EPS = 1e-6
GM_WIDTH = 512
GM_GROUPS = 4
GM_GROUP_DIM = GM_WIDTH // GM_GROUPS
GM_CHUNK = 128
MLA_HEADS = 8
MLA_Q_RANK = 384
MLA_KV_RANK = 256
MLA_NOPE = 64
MLA_ROPE = 32
MLA_QK_DIM = MLA_NOPE + MLA_ROPE
MLA_V = 64
MLA_WIDTH = MLA_HEADS * MLA_V
ROPE_THETA = 10000.0
Q_BLOCK = 128
SSD_HEADS = 8
SSD_HEAD_DIM = 64
SSD_INNER = SSD_HEADS * SSD_HEAD_DIM
SSD_GROUPS = 2
SSD_STATE = 128
SSD_CONV = 4
SSD_CHUNK = 128
SSD_CONV_DIM = SSD_INNER + 2 * SSD_GROUPS * SSD_STATE
N_BRANCH = 3
BRANCH_WIDTH = 512
IN_WIDTHS = (2 * GM_WIDTH, MLA_Q_RANK, MLA_KV_RANK, MLA_ROPE, SSD_INNER, SSD_CONV_DIM, SSD_HEADS, N_BRANCH * D_MODEL)
IN_COLS = sum(IN_WIDTHS)
IN_OFFSETS = tuple(int(v) for v in np.cumsum(IN_WIDTHS)[:-1])

kernel_name = 'hybrid_gmlp_mla_ssd_macaron'


def rms_norm(x, gain):
    xf = x.astype(jnp.float32)
    y = xf * lax.rsqrt(jnp.mean(xf * xf, axis=-1, keepdims=True) + EPS)
    return (y * gain.astype(jnp.float32)).astype(x.dtype)


def swiglu_ffn(h, w_in, w_out):
    gate, up = jnp.split(h @ w_in, 2, axis=-1)
    return (jax.nn.silu(gate) * up) @ w_out


def apply_rope(x, cos, sin):
    x1, x2 = jnp.split(x, 2, axis=-1)
    return jnp.concatenate([x1 * cos - x2 * sin, x2 * cos + x1 * sin], axis=-1)


def gmlp_mixer(uv, v_gain, w_s, b_s):
    bsz, s, _ = uv.shape
    u, v = jnp.split(jax.nn.gelu(uv, approximate=False), 2, axis=-1)
    v = rms_norm(v, v_gain).reshape(bsz, s // GM_CHUNK, GM_CHUNK, GM_GROUPS, GM_GROUP_DIM)
    causal = jnp.tril(jnp.ones((GM_CHUNK, GM_CHUNK), dtype=bool))
    w = jnp.where(causal[None], w_s, 0.0).astype(v.dtype)
    sp = jnp.einsum('gts,bcsgd->bctgd', w, v) + b_s.T[:, :, None].astype(v.dtype)
    return u * sp.reshape(bsz, s, GM_WIDTH)


def blocked_causal_attention(q, k, v):
    bsz, s, nh, dqk = q.shape
    dv = v.shape[-1]
    nb = s // Q_BLOCK
    scale = dqk ** -0.5
    qb = q.reshape(bsz, nb, Q_BLOCK, nh, dqk).transpose(1, 0, 3, 2, 4)
    kt = k.transpose(0, 2, 1, 3)
    vt = v.transpose(0, 2, 1, 3)
    kpos = jnp.arange(s)

    def one_block(args):
        qi, i = args
        sc = jnp.einsum('bhqd,bhkd->bhqk', qi, kt, preferred_element_type=jnp.float32) * scale
        qpos = i * Q_BLOCK + jnp.arange(Q_BLOCK)
        sc = jnp.where(kpos[None, :] <= qpos[:, None], sc, -jnp.inf)
        p = jax.nn.softmax(sc, axis=-1).astype(vt.dtype)
        return jnp.einsum('bhqk,bhkd->bhqd', p, vt)

    out = lax.map(one_block, (qb, jnp.arange(nb)))
    return out.transpose(1, 0, 3, 2, 4).reshape(bsz, s, nh * dv)


def mla_mixer(c_q, c_kv, k_rope, cos, sin, q_norm, kv_norm, w_uq, w_ukv, q_gain, k_gain):
    bsz, s, _ = c_q.shape
    q = (rms_norm(c_q, q_norm) @ w_uq).reshape(bsz, s, MLA_HEADS, MLA_QK_DIM)
    kv = (rms_norm(c_kv, kv_norm) @ w_ukv).reshape(bsz, s, MLA_HEADS, MLA_NOPE + MLA_V)
    k_nope, v = jnp.split(kv, [MLA_NOPE], axis=-1)
    k_pe = jnp.broadcast_to(k_rope[:, :, None, :], (bsz, s, MLA_HEADS, MLA_ROPE))
    k = jnp.concatenate([k_nope, k_pe], axis=-1)
    q = rms_norm(q, q_gain)
    k = rms_norm(k, k_gain)
    q = jnp.concatenate([q[..., :MLA_NOPE], apply_rope(q[..., MLA_NOPE:], cos, sin)], axis=-1)
    k = jnp.concatenate([k[..., :MLA_NOPE], apply_rope(k[..., MLA_NOPE:], cos, sin)], axis=-1)
    return blocked_causal_attention(q, k, v)


def segsum(a):
    t = a.shape[-1]
    idx = jnp.arange(t)
    ax = jnp.where(idx[:, None] > idx[None, :], a[..., :, None], 0.0)
    ss = jnp.cumsum(ax, axis=-2)
    return jnp.where(idx[:, None] >= idx[None, :], ss, -jnp.inf)


def ssd_scan(xs, dt, a_log, b_in, c_in):
    bsz, s, nh, hp = xs.shape
    ng, ns = b_in.shape[2], b_in.shape[3]
    nr = nh // ng
    nc = s // SSD_CHUNK
    a = -jnp.exp(a_log.astype(jnp.float32))
    da = (dt * a).reshape(bsz, nc, SSD_CHUNK, ng, nr).transpose(0, 3, 4, 1, 2)
    xdt = (xs * dt[..., None].astype(xs.dtype)).reshape(bsz, nc, SSD_CHUNK, ng, nr, hp)
    bc = b_in.reshape(bsz, nc, SSD_CHUNK, ng, ns)
    cc = c_in.reshape(bsz, nc, SSD_CHUNK, ng, ns)
    cs = jnp.cumsum(da, axis=-1)
    dt_ = xs.dtype
    lmat = jnp.exp(segsum(da)).astype(dt_)
    cb = jnp.einsum('bclgn,bcsgn->bgcls', cc, bc)
    y_diag = jnp.einsum('bgrcls,bcsgrp->bclgrp', cb[:, :, None] * lmat, xdt)
    decay_states = jnp.exp(cs[..., -1:] - cs).astype(dt_)
    states = jnp.einsum('bclgn,bgrcl,bclgrp->bcgrpn', bc, decay_states, xdt)
    chunk_tot = jnp.pad(cs[..., -1], ((0, 0), (0, 0), (0, 0), (1, 0)))
    decay_chunk = jnp.exp(segsum(chunk_tot)).astype(dt_)
    states0 = jnp.concatenate([jnp.zeros_like(states[:, :1]), states], axis=1)
    new_states = jnp.einsum('bgrzc,bcgrpn->bzgrpn', decay_chunk, states0)
    states_in = new_states[:, :-1]
    y_off = jnp.einsum('bclgn,bcgrpn,bgrcl->bclgrp', cc, states_in, jnp.exp(cs).astype(dt_))
    return (y_diag + y_off).reshape(bsz, s, nh, hp)


def ssd_mixer(z, xbc, dt_raw, conv_w, conv_b, dt_bias, a_log, d_skip, norm_gain):
    bsz, s, _ = xbc.shape
    xbc = lax.conv_general_dilated(xbc, conv_w[:, None, :].astype(xbc.dtype), (1,), [(SSD_CONV - 1, 0)],
                                   dimension_numbers=('NWC', 'WIO', 'NWC'), feature_group_count=SSD_CONV_DIM)
    xbc = jax.nn.silu(xbc + conv_b.astype(xbc.dtype))
    xs, b_in, c_in = jnp.split(xbc, [SSD_INNER, SSD_INNER + SSD_GROUPS * SSD_STATE], axis=-1)
    xs = xs.reshape(bsz, s, SSD_HEADS, SSD_HEAD_DIM)
    b_in = b_in.reshape(bsz, s, SSD_GROUPS, SSD_STATE)
    c_in = c_in.reshape(bsz, s, SSD_GROUPS, SSD_STATE)
    dt = jax.nn.softplus(dt_raw.astype(jnp.float32) + dt_bias.astype(jnp.float32))
    y = ssd_scan(xs, dt, a_log, b_in, c_in) + xs * d_skip[:, None].astype(xs.dtype)
    y = y.reshape(bsz, s, SSD_INNER) * jax.nn.silu(z)
    y = rms_norm(y.reshape(bsz, s, SSD_GROUPS, SSD_INNER // SSD_GROUPS), norm_gain.reshape(SSD_GROUPS, -1))
    return y.reshape(bsz, s, SSD_INNER)


def _fwd_setup_inputs(seed: int = 0) -> dict:
    key = jax.random.key(seed)
    ks = jax.random.split(key, 32)
    f32 = jnp.float32

    def nrm(k, shape, scale):
        return jax.random.normal(k, shape, f32) * scale

    def gain(k, shape):
        return 1.0 + 0.05 * jax.random.normal(k, shape, f32)

    L = DEPTH
    x = jax.random.normal(ks[0], (BATCH, SEQ, D_MODEL), f32)
    offsets = jax.random.randint(ks[1], (BATCH, 1), 0, SEQ, dtype=jnp.int32)
    positions = offsets + jnp.arange(SEQ, dtype=jnp.int32)[None, :]
    dt0 = jnp.exp(jax.random.uniform(ks[2], (L, SSD_HEADS), f32, np.log(1e-3), np.log(1e-1)))
    return {
        'x': x,
        'positions': positions,
        'ffn1_norm': gain(ks[3], (L, D_MODEL)),
        'ffn1_w_in': nrm(ks[4], (L, D_MODEL, 2 * D_FF), D_MODEL ** -0.5),
        'ffn1_w_out': nrm(ks[5], (L, D_FF, D_MODEL), D_FF ** -0.5),
        'mix_norm': gain(ks[6], (L, D_MODEL)),
        'w_in': nrm(ks[7], (L, D_MODEL, IN_COLS), D_MODEL ** -0.5),
        'gm_v_norm': gain(ks[8], (L, GM_WIDTH)),
        'gm_w_s': nrm(ks[9], (L, GM_GROUPS, GM_CHUNK, GM_CHUNK), 0.5 * GM_CHUNK ** -0.5),
        'gm_b_s': 1.0 + 0.1 * jax.random.normal(ks[10], (L, GM_GROUPS, GM_CHUNK), f32),
        'mla_q_norm': gain(ks[11], (L, MLA_Q_RANK)),
        'mla_kv_norm': gain(ks[12], (L, MLA_KV_RANK)),
        'mla_w_uq': nrm(ks[13], (L, MLA_Q_RANK, MLA_HEADS * MLA_QK_DIM), MLA_Q_RANK ** -0.5),
        'mla_w_ukv': nrm(ks[14], (L, MLA_KV_RANK, MLA_HEADS * (MLA_NOPE + MLA_V)), MLA_KV_RANK ** -0.5),
        'mla_q_gain': gain(ks[15], (L, MLA_QK_DIM)),
        'mla_k_gain': gain(ks[16], (L, MLA_QK_DIM)),
        'ssd_conv_w': nrm(ks[17], (L, SSD_CONV, SSD_CONV_DIM), SSD_CONV ** -0.5),
        'ssd_conv_b': nrm(ks[18], (L, SSD_CONV_DIM), 0.02),
        'ssd_dt_bias': dt0 + jnp.log(-jnp.expm1(-dt0)),
        'ssd_a_log': jnp.log(jax.random.uniform(ks[19], (L, SSD_HEADS), f32, 1.0, 16.0)),
        'ssd_d': 1.0 + 0.1 * jax.random.normal(ks[20], (L, SSD_HEADS), f32),
        'ssd_norm': gain(ks[21], (L, SSD_INNER)),
        'w_branch': nrm(ks[22], (L, N_BRANCH, BRANCH_WIDTH, D_MODEL), BRANCH_WIDTH ** -0.5),
        'w_out': nrm(ks[23], (L, D_MODEL, D_MODEL), D_MODEL ** -0.5),
        'ffn2_norm': gain(ks[24], (L, D_MODEL)),
        'ffn2_w_in': nrm(ks[25], (L, D_MODEL, 2 * D_FF), D_MODEL ** -0.5),
        'ffn2_w_out': nrm(ks[26], (L, D_FF, D_MODEL), D_FF ** -0.5),
    }


def _fwd_reference(x, positions, ffn1_norm, ffn1_w_in, ffn1_w_out, mix_norm, w_in, gm_v_norm, gm_w_s, gm_b_s,
              mla_q_norm, mla_kv_norm, mla_w_uq, mla_w_ukv, mla_q_gain, mla_k_gain,
              ssd_conv_w, ssd_conv_b, ssd_dt_bias, ssd_a_log, ssd_d, ssd_norm,
              w_branch, w_out, ffn2_norm, ffn2_w_in, ffn2_w_out):
    bsz, s, _ = x.shape
    inv_freq = 1.0 / (ROPE_THETA ** (jnp.arange(0, MLA_ROPE, 2, dtype=jnp.float32) / MLA_ROPE))
    ang = positions.astype(jnp.float32)[..., None] * inv_freq
    cos = jnp.cos(ang)[:, :, None, :].astype(x.dtype)
    sin = jnp.sin(ang)[:, :, None, :].astype(x.dtype)
    for l in range(DEPTH):
        x = x + FFN_RESID * swiglu_ffn(rms_norm(x, ffn1_norm[l]), ffn1_w_in[l], ffn1_w_out[l])
        h = rms_norm(x, mix_norm[l])
        uv, c_q, c_kv, k_rope, z, xbc, dt_raw, gates = jnp.split(h @ w_in[l], IN_OFFSETS, axis=-1)
        y_a = gmlp_mixer(uv, gm_v_norm[l], gm_w_s[l], gm_b_s[l])
        y_b = mla_mixer(c_q, c_kv, k_rope, cos, sin, mla_q_norm[l], mla_kv_norm[l], mla_w_uq[l], mla_w_ukv[l],
                        mla_q_gain[l], mla_k_gain[l])
        y_c = ssd_mixer(z, xbc, dt_raw, ssd_conv_w[l], ssd_conv_b[l], ssd_dt_bias[l], ssd_a_log[l], ssd_d[l],
                        ssd_norm[l])
        g = jax.nn.sigmoid(gates).reshape(bsz, s, N_BRANCH, D_MODEL)
        merged = (g[:, :, 0] * (y_a @ w_branch[l, 0])
                  + g[:, :, 1] * (y_b @ w_branch[l, 1])
                  + g[:, :, 2] * (y_c @ w_branch[l, 2]))
        x = x + merged @ w_out[l]
        x = x + FFN_RESID * swiglu_ffn(rms_norm(x, ffn2_norm[l]), ffn2_w_in[l], ffn2_w_out[l])
    return x


import jax as _jax
import jax.numpy as _jnp

TWIN_FORMAT = 'train_step'
FWD_PARAMS = ['x', 'positions', 'ffn1_norm', 'ffn1_w_in', 'ffn1_w_out', 'mix_norm', 'w_in', 'gm_v_norm', 'gm_w_s', 'gm_b_s', 'mla_q_norm', 'mla_kv_norm', 'mla_w_uq', 'mla_w_ukv', 'mla_q_gain', 'mla_k_gain', 'ssd_conv_w', 'ssd_conv_b', 'ssd_dt_bias', 'ssd_a_log', 'ssd_d', 'ssd_norm', 'w_branch', 'w_out', 'ffn2_norm', 'ffn2_w_in', 'ffn2_w_out']
TWIN_WEIGHTS = ['ffn1_norm', 'ffn1_w_in', 'ffn1_w_out', 'mix_norm', 'w_in', 'gm_v_norm', 'gm_w_s', 'gm_b_s', 'mla_q_norm', 'mla_kv_norm', 'mla_w_uq', 'mla_w_ukv', 'mla_q_gain', 'mla_k_gain', 'ssd_conv_w', 'ssd_conv_b', 'ssd_dt_bias', 'ssd_a_log', 'ssd_d', 'ssd_norm', 'w_branch', 'w_out', 'ffn2_norm', 'ffn2_w_in', 'ffn2_w_out']
TWIN_DIFF_INPUT = 'x'
TWIN_INPUTS = ['x', 'positions', 'ffn1_norm', 'ffn1_w_in', 'ffn1_w_out', 'mix_norm', 'w_in', 'gm_v_norm', 'gm_w_s', 'gm_b_s', 'mla_q_norm', 'mla_kv_norm', 'mla_w_uq', 'mla_w_ukv', 'mla_q_gain', 'mla_k_gain', 'ssd_conv_w', 'ssd_conv_b', 'ssd_dt_bias', 'ssd_a_log', 'ssd_d', 'ssd_norm', 'w_branch', 'w_out', 'ffn2_norm', 'ffn2_w_in', 'ffn2_w_out', 'loss_target', 'm_ffn1_norm', 'm_ffn1_w_in', 'm_ffn1_w_out', 'm_mix_norm', 'm_w_in', 'm_gm_v_norm', 'm_gm_w_s', 'm_gm_b_s', 'm_mla_q_norm', 'm_mla_kv_norm', 'm_mla_w_uq', 'm_mla_w_ukv', 'm_mla_q_gain', 'm_mla_k_gain', 'm_ssd_conv_w', 'm_ssd_conv_b', 'm_ssd_dt_bias', 'm_ssd_a_log', 'm_ssd_d', 'm_ssd_norm', 'm_w_branch', 'm_w_out', 'm_ffn2_norm', 'm_ffn2_w_in', 'm_ffn2_w_out', 'v_ffn1_norm', 'v_ffn1_w_in', 'v_ffn1_w_out', 'v_mix_norm', 'v_w_in', 'v_gm_v_norm', 'v_gm_w_s', 'v_gm_b_s', 'v_mla_q_norm', 'v_mla_kv_norm', 'v_mla_w_uq', 'v_mla_w_ukv', 'v_mla_q_gain', 'v_mla_k_gain', 'v_ssd_conv_w', 'v_ssd_conv_b', 'v_ssd_dt_bias', 'v_ssd_a_log', 'v_ssd_d', 'v_ssd_norm', 'v_w_branch', 'v_w_out', 'v_ffn2_norm', 'v_ffn2_w_in', 'v_ffn2_w_out']
TWIN_OUTPUTS = ['loss', 'grad_x', 'grad_ffn1_norm', 'grad_ffn1_w_in', 'grad_ffn1_w_out', 'grad_mix_norm', 'grad_w_in', 'grad_gm_v_norm', 'grad_gm_w_s', 'grad_gm_b_s', 'grad_mla_q_norm', 'grad_mla_kv_norm', 'grad_mla_w_uq', 'grad_mla_w_ukv', 'grad_mla_q_gain', 'grad_mla_k_gain', 'grad_ssd_conv_w', 'grad_ssd_conv_b', 'grad_ssd_dt_bias', 'grad_ssd_a_log', 'grad_ssd_d', 'grad_ssd_norm', 'grad_w_branch', 'grad_w_out', 'grad_ffn2_norm', 'grad_ffn2_w_in', 'grad_ffn2_w_out', 'delta_ffn1_norm', 'delta_ffn1_w_in', 'delta_ffn1_w_out', 'delta_mix_norm', 'delta_w_in', 'delta_gm_v_norm', 'delta_gm_w_s', 'delta_gm_b_s', 'delta_mla_q_norm', 'delta_mla_kv_norm', 'delta_mla_w_uq', 'delta_mla_w_ukv', 'delta_mla_q_gain', 'delta_mla_k_gain', 'delta_ssd_conv_w', 'delta_ssd_conv_b', 'delta_ssd_dt_bias', 'delta_ssd_a_log', 'delta_ssd_d', 'delta_ssd_norm', 'delta_w_branch', 'delta_w_out', 'delta_ffn2_norm', 'delta_ffn2_w_in', 'delta_ffn2_w_out', 'new_m_ffn1_norm', 'new_m_ffn1_w_in', 'new_m_ffn1_w_out', 'new_m_mix_norm', 'new_m_w_in', 'new_m_gm_v_norm', 'new_m_gm_w_s', 'new_m_gm_b_s', 'new_m_mla_q_norm', 'new_m_mla_kv_norm', 'new_m_mla_w_uq', 'new_m_mla_w_ukv', 'new_m_mla_q_gain', 'new_m_mla_k_gain', 'new_m_ssd_conv_w', 'new_m_ssd_conv_b', 'new_m_ssd_dt_bias', 'new_m_ssd_a_log', 'new_m_ssd_d', 'new_m_ssd_norm', 'new_m_w_branch', 'new_m_w_out', 'new_m_ffn2_norm', 'new_m_ffn2_w_in', 'new_m_ffn2_w_out', 'new_v_ffn1_norm', 'new_v_ffn1_w_in', 'new_v_ffn1_w_out', 'new_v_mix_norm', 'new_v_w_in', 'new_v_gm_v_norm', 'new_v_gm_w_s', 'new_v_gm_b_s', 'new_v_mla_q_norm', 'new_v_mla_kv_norm', 'new_v_mla_w_uq', 'new_v_mla_w_ukv', 'new_v_mla_q_gain', 'new_v_mla_k_gain', 'new_v_ssd_conv_w', 'new_v_ssd_conv_b', 'new_v_ssd_dt_bias', 'new_v_ssd_a_log', 'new_v_ssd_d', 'new_v_ssd_norm', 'new_v_w_branch', 'new_v_w_out', 'new_v_ffn2_norm', 'new_v_ffn2_w_in', 'new_v_ffn2_w_out']
TWIN_LEAF_KINDS = {'loss': 'loss', 'grad_x': 'grad_x', 'grad_ffn1_norm': 'grad_w', 'grad_ffn1_w_in': 'grad_w', 'grad_ffn1_w_out': 'grad_w', 'grad_mix_norm': 'grad_w', 'grad_w_in': 'grad_w', 'grad_gm_v_norm': 'grad_w', 'grad_gm_w_s': 'grad_w', 'grad_gm_b_s': 'grad_w', 'grad_mla_q_norm': 'grad_w', 'grad_mla_kv_norm': 'grad_w', 'grad_mla_w_uq': 'grad_w', 'grad_mla_w_ukv': 'grad_w', 'grad_mla_q_gain': 'grad_w', 'grad_mla_k_gain': 'grad_w', 'grad_ssd_conv_w': 'grad_w', 'grad_ssd_conv_b': 'grad_w', 'grad_ssd_dt_bias': 'grad_w', 'grad_ssd_a_log': 'grad_w', 'grad_ssd_d': 'grad_w', 'grad_ssd_norm': 'grad_w', 'grad_w_branch': 'grad_w', 'grad_w_out': 'grad_w', 'grad_ffn2_norm': 'grad_w', 'grad_ffn2_w_in': 'grad_w', 'grad_ffn2_w_out': 'grad_w', 'delta_ffn1_norm': 'delta_w', 'delta_ffn1_w_in': 'delta_w', 'delta_ffn1_w_out': 'delta_w', 'delta_mix_norm': 'delta_w', 'delta_w_in': 'delta_w', 'delta_gm_v_norm': 'delta_w', 'delta_gm_w_s': 'delta_w', 'delta_gm_b_s': 'delta_w', 'delta_mla_q_norm': 'delta_w', 'delta_mla_kv_norm': 'delta_w', 'delta_mla_w_uq': 'delta_w', 'delta_mla_w_ukv': 'delta_w', 'delta_mla_q_gain': 'delta_w', 'delta_mla_k_gain': 'delta_w', 'delta_ssd_conv_w': 'delta_w', 'delta_ssd_conv_b': 'delta_w', 'delta_ssd_dt_bias': 'delta_w', 'delta_ssd_a_log': 'delta_w', 'delta_ssd_d': 'delta_w', 'delta_ssd_norm': 'delta_w', 'delta_w_branch': 'delta_w', 'delta_w_out': 'delta_w', 'delta_ffn2_norm': 'delta_w', 'delta_ffn2_w_in': 'delta_w', 'delta_ffn2_w_out': 'delta_w', 'new_m_ffn1_norm': 'new_m', 'new_m_ffn1_w_in': 'new_m', 'new_m_ffn1_w_out': 'new_m', 'new_m_mix_norm': 'new_m', 'new_m_w_in': 'new_m', 'new_m_gm_v_norm': 'new_m', 'new_m_gm_w_s': 'new_m', 'new_m_gm_b_s': 'new_m', 'new_m_mla_q_norm': 'new_m', 'new_m_mla_kv_norm': 'new_m', 'new_m_mla_w_uq': 'new_m', 'new_m_mla_w_ukv': 'new_m', 'new_m_mla_q_gain': 'new_m', 'new_m_mla_k_gain': 'new_m', 'new_m_ssd_conv_w': 'new_m', 'new_m_ssd_conv_b': 'new_m', 'new_m_ssd_dt_bias': 'new_m', 'new_m_ssd_a_log': 'new_m', 'new_m_ssd_d': 'new_m', 'new_m_ssd_norm': 'new_m', 'new_m_w_branch': 'new_m', 'new_m_w_out': 'new_m', 'new_m_ffn2_norm': 'new_m', 'new_m_ffn2_w_in': 'new_m', 'new_m_ffn2_w_out': 'new_m', 'new_v_ffn1_norm': 'new_v', 'new_v_ffn1_w_in': 'new_v', 'new_v_ffn1_w_out': 'new_v', 'new_v_mix_norm': 'new_v', 'new_v_w_in': 'new_v', 'new_v_gm_v_norm': 'new_v', 'new_v_gm_w_s': 'new_v', 'new_v_gm_b_s': 'new_v', 'new_v_mla_q_norm': 'new_v', 'new_v_mla_kv_norm': 'new_v', 'new_v_mla_w_uq': 'new_v', 'new_v_mla_w_ukv': 'new_v', 'new_v_mla_q_gain': 'new_v', 'new_v_mla_k_gain': 'new_v', 'new_v_ssd_conv_w': 'new_v', 'new_v_ssd_conv_b': 'new_v', 'new_v_ssd_dt_bias': 'new_v', 'new_v_ssd_a_log': 'new_v', 'new_v_ssd_d': 'new_v', 'new_v_ssd_norm': 'new_v', 'new_v_w_branch': 'new_v', 'new_v_w_out': 'new_v', 'new_v_ffn2_norm': 'new_v', 'new_v_ffn2_w_in': 'new_v', 'new_v_ffn2_w_out': 'new_v'}


def _forward(args):
    return _fwd_reference(*[args[k] for k in FWD_PARAMS])


def _output_shape():
    out = _jax.eval_shape(lambda: _forward(_fwd_setup_inputs(0)))
    return out.shape, out.dtype

N_MICROBATCH = 1
ADAM_LR = 0.001
ADAM_B1 = 0.9
ADAM_B2 = 0.999
ADAM_EPS = 1e-08
ADAM_WD = 0.01
ADAM_STEP = 10
PER_EXAMPLE_BATCH_AXIS = {'x': 0, 'positions': 0, 'loss_target': 0}
SHARED_INPUTS = []
_WEIGHT_DTYPES = {'ffn1_norm': _jnp.float32, 'ffn1_w_in': _jnp.float32, 'ffn1_w_out': _jnp.float32, 'mix_norm': _jnp.float32, 'w_in': _jnp.float32, 'gm_v_norm': _jnp.float32, 'gm_w_s': _jnp.float32, 'gm_b_s': _jnp.float32, 'mla_q_norm': _jnp.float32, 'mla_kv_norm': _jnp.float32, 'mla_w_uq': _jnp.float32, 'mla_w_ukv': _jnp.float32, 'mla_q_gain': _jnp.float32, 'mla_k_gain': _jnp.float32, 'ssd_conv_w': _jnp.float32, 'ssd_conv_b': _jnp.float32, 'ssd_dt_bias': _jnp.float32, 'ssd_a_log': _jnp.float32, 'ssd_d': _jnp.float32, 'ssd_norm': _jnp.float32, 'w_branch': _jnp.float32, 'w_out': _jnp.float32, 'ffn2_norm': _jnp.float32, 'ffn2_w_in': _jnp.float32, 'ffn2_w_out': _jnp.float32}
MOMENT_SCALE = {'ffn1_norm': 5.871945e+00, 'ffn1_w_in': 1.956599e-01, 'ffn1_w_out': 3.398099e-01, 'mix_norm': 6.794620e+00, 'w_in': 6.220787e-01, 'gm_v_norm': 9.448952e-01, 'gm_w_s': 2.418447e+00, 'gm_b_s': 8.352885e+00, 'mla_q_norm': 1.115466e-01, 'mla_kv_norm': 2.313895e+00, 'mla_w_uq': 7.978172e-02, 'mla_w_ukv': 1.125529e+00, 'mla_q_gain': 5.162966e-01, 'mla_k_gain': 5.177247e-01, 'ssd_conv_w': 1.325423e+00, 'ssd_conv_b': 4.214931e+00, 'ssd_dt_bias': 3.080401e+00, 'ssd_a_log': 9.435152e+00, 'ssd_d': 1.202027e+01, 'ssd_norm': 2.678647e+01, 'w_branch': 1.988423e+00, 'w_out': 3.104206e+00, 'ffn2_norm': 6.136680e+00, 'ffn2_w_in': 1.875907e-01, 'ffn2_w_out': 3.372428e-01}


def _to_microbatches(a, axis):
    t = _jnp.moveaxis(a, axis, 0)
    t = t.reshape((N_MICROBATCH, t.shape[0] // N_MICROBATCH) + t.shape[1:])
    return _jnp.moveaxis(t, 1, axis + 1)


def setup_inputs(seed: int = 0) -> dict:
    inp = _fwd_setup_inputs(seed)
    key = _jax.random.fold_in(_jax.random.key(seed), 7919)
    shape, _ = _output_shape()
    out = dict(inp)
    out["loss_target"] = _jax.random.normal(_jax.random.fold_in(key, 0), shape, _jnp.float32)
    for i, name in enumerate(TWIN_WEIGHTS):
        w = inp[name].astype(_jnp.float32)
        if MOMENT_SCALE is None:
            s = _jnp.sqrt(_jnp.mean(_jnp.square(w)) + 1e-30)
        else:
            s = MOMENT_SCALE[name]
        km, kv = _jax.random.split(_jax.random.fold_in(key, i + 1))
        out[name] = w
        out["m_" + name] = s * _jax.random.normal(km, w.shape, _jnp.float32)
        out["v_" + name] = (s * s) * _jax.random.uniform(kv, w.shape, _jnp.float32, 0.5, 1.5)
    if N_MICROBATCH > 1:
        for name, axis in PER_EXAMPLE_BATCH_AXIS.items():
            out[name] = _to_microbatches(out[name], axis)
    return {'x': out['x'], 'positions': out['positions'], 'ffn1_norm': out['ffn1_norm'], 'ffn1_w_in': out['ffn1_w_in'], 'ffn1_w_out': out['ffn1_w_out'], 'mix_norm': out['mix_norm'], 'w_in': out['w_in'], 'gm_v_norm': out['gm_v_norm'], 'gm_w_s': out['gm_w_s'], 'gm_b_s': out['gm_b_s'], 'mla_q_norm': out['mla_q_norm'], 'mla_kv_norm': out['mla_kv_norm'], 'mla_w_uq': out['mla_w_uq'], 'mla_w_ukv': out['mla_w_ukv'], 'mla_q_gain': out['mla_q_gain'], 'mla_k_gain': out['mla_k_gain'], 'ssd_conv_w': out['ssd_conv_w'], 'ssd_conv_b': out['ssd_conv_b'], 'ssd_dt_bias': out['ssd_dt_bias'], 'ssd_a_log': out['ssd_a_log'], 'ssd_d': out['ssd_d'], 'ssd_norm': out['ssd_norm'], 'w_branch': out['w_branch'], 'w_out': out['w_out'], 'ffn2_norm': out['ffn2_norm'], 'ffn2_w_in': out['ffn2_w_in'], 'ffn2_w_out': out['ffn2_w_out'], 'loss_target': out['loss_target'], 'm_ffn1_norm': out['m_ffn1_norm'], 'm_ffn1_w_in': out['m_ffn1_w_in'], 'm_ffn1_w_out': out['m_ffn1_w_out'], 'm_mix_norm': out['m_mix_norm'], 'm_w_in': out['m_w_in'], 'm_gm_v_norm': out['m_gm_v_norm'], 'm_gm_w_s': out['m_gm_w_s'], 'm_gm_b_s': out['m_gm_b_s'], 'm_mla_q_norm': out['m_mla_q_norm'], 'm_mla_kv_norm': out['m_mla_kv_norm'], 'm_mla_w_uq': out['m_mla_w_uq'], 'm_mla_w_ukv': out['m_mla_w_ukv'], 'm_mla_q_gain': out['m_mla_q_gain'], 'm_mla_k_gain': out['m_mla_k_gain'], 'm_ssd_conv_w': out['m_ssd_conv_w'], 'm_ssd_conv_b': out['m_ssd_conv_b'], 'm_ssd_dt_bias': out['m_ssd_dt_bias'], 'm_ssd_a_log': out['m_ssd_a_log'], 'm_ssd_d': out['m_ssd_d'], 'm_ssd_norm': out['m_ssd_norm'], 'm_w_branch': out['m_w_branch'], 'm_w_out': out['m_w_out'], 'm_ffn2_norm': out['m_ffn2_norm'], 'm_ffn2_w_in': out['m_ffn2_w_in'], 'm_ffn2_w_out': out['m_ffn2_w_out'], 'v_ffn1_norm': out['v_ffn1_norm'], 'v_ffn1_w_in': out['v_ffn1_w_in'], 'v_ffn1_w_out': out['v_ffn1_w_out'], 'v_mix_norm': out['v_mix_norm'], 'v_w_in': out['v_w_in'], 'v_gm_v_norm': out['v_gm_v_norm'], 'v_gm_w_s': out['v_gm_w_s'], 'v_gm_b_s': out['v_gm_b_s'], 'v_mla_q_norm': out['v_mla_q_norm'], 'v_mla_kv_norm': out['v_mla_kv_norm'], 'v_mla_w_uq': out['v_mla_w_uq'], 'v_mla_w_ukv': out['v_mla_w_ukv'], 'v_mla_q_gain': out['v_mla_q_gain'], 'v_mla_k_gain': out['v_mla_k_gain'], 'v_ssd_conv_w': out['v_ssd_conv_w'], 'v_ssd_conv_b': out['v_ssd_conv_b'], 'v_ssd_dt_bias': out['v_ssd_dt_bias'], 'v_ssd_a_log': out['v_ssd_a_log'], 'v_ssd_d': out['v_ssd_d'], 'v_ssd_norm': out['v_ssd_norm'], 'v_w_branch': out['v_w_branch'], 'v_w_out': out['v_w_out'], 'v_ffn2_norm': out['v_ffn2_norm'], 'v_ffn2_w_in': out['v_ffn2_w_in'], 'v_ffn2_w_out': out['v_ffn2_w_out']}


def _loss(weights, diff, rest, loss_target):
    with _jax.named_scope("forward"):
        args = {**rest, TWIN_DIFF_INPUT: diff, **{k: w.astype(_WEIGHT_DTYPES[k]) for k, w in weights.items()}}
        y = _forward(args)
    with _jax.named_scope("loss_head"):
        err = _jnp.square(y.astype(_jnp.float32) - loss_target)
        return 0.5 * _jnp.sum(_jnp.mean(err, axis=-1)) if err.ndim else 0.5 * err


def _adamw(w, g, m, v):
    m = ADAM_B1 * m + (1.0 - ADAM_B1) * g
    v = ADAM_B2 * v + (1.0 - ADAM_B2) * _jnp.square(g)
    m_hat = m / (1.0 - ADAM_B1 ** ADAM_STEP)
    v_hat = v / (1.0 - ADAM_B2 ** ADAM_STEP)
    delta = -ADAM_LR * (m_hat / (_jnp.sqrt(v_hat) + ADAM_EPS) + ADAM_WD * w)
    return delta, m, v


def reference(x, positions, ffn1_norm, ffn1_w_in, ffn1_w_out, mix_norm, w_in, gm_v_norm, gm_w_s, gm_b_s, mla_q_norm, mla_kv_norm, mla_w_uq, mla_w_ukv, mla_q_gain, mla_k_gain, ssd_conv_w, ssd_conv_b, ssd_dt_bias, ssd_a_log, ssd_d, ssd_norm, w_branch, w_out, ffn2_norm, ffn2_w_in, ffn2_w_out, loss_target, m_ffn1_norm, m_ffn1_w_in, m_ffn1_w_out, m_mix_norm, m_w_in, m_gm_v_norm, m_gm_w_s, m_gm_b_s, m_mla_q_norm, m_mla_kv_norm, m_mla_w_uq, m_mla_w_ukv, m_mla_q_gain, m_mla_k_gain, m_ssd_conv_w, m_ssd_conv_b, m_ssd_dt_bias, m_ssd_a_log, m_ssd_d, m_ssd_norm, m_w_branch, m_w_out, m_ffn2_norm, m_ffn2_w_in, m_ffn2_w_out, v_ffn1_norm, v_ffn1_w_in, v_ffn1_w_out, v_mix_norm, v_w_in, v_gm_v_norm, v_gm_w_s, v_gm_b_s, v_mla_q_norm, v_mla_kv_norm, v_mla_w_uq, v_mla_w_ukv, v_mla_q_gain, v_mla_k_gain, v_ssd_conv_w, v_ssd_conv_b, v_ssd_dt_bias, v_ssd_a_log, v_ssd_d, v_ssd_norm, v_w_branch, v_w_out, v_ffn2_norm, v_ffn2_w_in, v_ffn2_w_out):
    given = dict(x=x, positions=positions, ffn1_norm=ffn1_norm, ffn1_w_in=ffn1_w_in, ffn1_w_out=ffn1_w_out, mix_norm=mix_norm, w_in=w_in, gm_v_norm=gm_v_norm, gm_w_s=gm_w_s, gm_b_s=gm_b_s, mla_q_norm=mla_q_norm, mla_kv_norm=mla_kv_norm, mla_w_uq=mla_w_uq, mla_w_ukv=mla_w_ukv, mla_q_gain=mla_q_gain, mla_k_gain=mla_k_gain, ssd_conv_w=ssd_conv_w, ssd_conv_b=ssd_conv_b, ssd_dt_bias=ssd_dt_bias, ssd_a_log=ssd_a_log, ssd_d=ssd_d, ssd_norm=ssd_norm, w_branch=w_branch, w_out=w_out, ffn2_norm=ffn2_norm, ffn2_w_in=ffn2_w_in, ffn2_w_out=ffn2_w_out, loss_target=loss_target, m_ffn1_norm=m_ffn1_norm, m_ffn1_w_in=m_ffn1_w_in, m_ffn1_w_out=m_ffn1_w_out, m_mix_norm=m_mix_norm, m_w_in=m_w_in, m_gm_v_norm=m_gm_v_norm, m_gm_w_s=m_gm_w_s, m_gm_b_s=m_gm_b_s, m_mla_q_norm=m_mla_q_norm, m_mla_kv_norm=m_mla_kv_norm, m_mla_w_uq=m_mla_w_uq, m_mla_w_ukv=m_mla_w_ukv, m_mla_q_gain=m_mla_q_gain, m_mla_k_gain=m_mla_k_gain, m_ssd_conv_w=m_ssd_conv_w, m_ssd_conv_b=m_ssd_conv_b, m_ssd_dt_bias=m_ssd_dt_bias, m_ssd_a_log=m_ssd_a_log, m_ssd_d=m_ssd_d, m_ssd_norm=m_ssd_norm, m_w_branch=m_w_branch, m_w_out=m_w_out, m_ffn2_norm=m_ffn2_norm, m_ffn2_w_in=m_ffn2_w_in, m_ffn2_w_out=m_ffn2_w_out, v_ffn1_norm=v_ffn1_norm, v_ffn1_w_in=v_ffn1_w_in, v_ffn1_w_out=v_ffn1_w_out, v_mix_norm=v_mix_norm, v_w_in=v_w_in, v_gm_v_norm=v_gm_v_norm, v_gm_w_s=v_gm_w_s, v_gm_b_s=v_gm_b_s, v_mla_q_norm=v_mla_q_norm, v_mla_kv_norm=v_mla_kv_norm, v_mla_w_uq=v_mla_w_uq, v_mla_w_ukv=v_mla_w_ukv, v_mla_q_gain=v_mla_q_gain, v_mla_k_gain=v_mla_k_gain, v_ssd_conv_w=v_ssd_conv_w, v_ssd_conv_b=v_ssd_conv_b, v_ssd_dt_bias=v_ssd_dt_bias, v_ssd_a_log=v_ssd_a_log, v_ssd_d=v_ssd_d, v_ssd_norm=v_ssd_norm, v_w_branch=v_w_branch, v_w_out=v_w_out, v_ffn2_norm=v_ffn2_norm, v_ffn2_w_in=v_ffn2_w_in, v_ffn2_w_out=v_ffn2_w_out)
    weights = {n: given[n] for n in TWIN_WEIGHTS}
    shared = {n: given[n] for n in SHARED_INPUTS}
    per_example = {n: given[n] for n in ['x', 'positions']}
    grad_fn = _jax.value_and_grad(_loss, argnums=(0, 1))

    def one_microbatch(ex, loss_target):
        ex = dict(ex)
        diff = ex.pop(TWIN_DIFF_INPUT)
        return grad_fn(weights, diff, {**shared, **ex}, loss_target)

    if N_MICROBATCH == 1:
        loss, (grad_w, grad_x) = one_microbatch(per_example, given["loss_target"])
    else:
        def body(carry, xs):
            loss_sum, grad_sum = carry
            l_k, (gw_k, gx_k) = one_microbatch(xs[0], xs[1])
            with _jax.named_scope("update"):
                return (loss_sum + l_k, _jax.tree.map(_jnp.add, grad_sum, gw_k)), gx_k

        init = (_jnp.zeros((), _jnp.float32), _jax.tree.map(_jnp.zeros_like, weights))
        (loss, grad_w), grad_x = _jax.lax.scan(body, init, (per_example, given["loss_target"]))
    with _jax.named_scope("update"):
        delta_w, new_m, new_v = {}, {}, {}
        for n in TWIN_WEIGHTS:
            delta_w[n], new_m[n], new_v[n] = _adamw(weights[n], grad_w[n], given["m_" + n], given["v_" + n])
    return (loss, grad_x, *[grad_w[n] for n in TWIN_WEIGHTS], *[delta_w[n] for n in TWIN_WEIGHTS],
            *[new_m[n] for n in TWIN_WEIGHTS], *[new_v[n] for n in TWIN_WEIGHTS])
```

```python
import functools
import math

import numpy as np
import jax
import jax.numpy as jnp
from jax import lax
from jax.experimental import pallas as pl
from jax.experimental.pallas import tpu as pltpu

F32, BF16 = jnp.float32, jnp.bfloat16
MESH = pl.DeviceIdType.MESH

D_MODEL, DEPTH, D_FF, EPS = 1024, 4, 2816, 1e-6
GM_WIDTH, GM_GROUPS, CHUNK = 512, 4, 128
MLA_HEADS, MLA_Q_RANK, MLA_KV_RANK, MLA_NOPE, MLA_ROPE, MLA_V = 8, 384, 256, 64, 32, 64
MLA_QK = MLA_NOPE + MLA_ROPE
ROPE_THETA = 10000.0
SSD_HEADS, SSD_HEAD_DIM, SSD_GROUPS, SSD_STATE, SSD_CONV = 8, 64, 2, 128, 4
SSD_INNER = SSD_HEADS * SSD_HEAD_DIM
IN_COLS = 6312
LANES = 128
ADAM_LR, ADAM_B1, ADAM_B2, ADAM_EPS, ADAM_WD, ADAM_STEP = 0.001, 0.9, 0.999, 1e-08, 0.01, 10

C_UV, C_Z, C_XS, C_G, C_BC, C_CKV, C_CQ, C_KR, C_DT, PW = 0, 1024, 2048, 3072, 6144, 6656, 6912, 7296, 7424, 7680
HP = MLA_HEADS * LANES

WEIGHTS = ['ffn1_norm', 'ffn1_w_in', 'ffn1_w_out', 'mix_norm', 'w_in', 'gm_v_norm', 'gm_w_s', 'gm_b_s', 'mla_q_norm',
           'mla_kv_norm', 'mla_w_uq', 'mla_w_ukv', 'mla_q_gain', 'mla_k_gain', 'ssd_conv_w', 'ssd_conv_b', 'ssd_dt_bias',
           'ssd_a_log', 'ssd_d', 'ssd_norm', 'w_branch', 'w_out', 'ffn2_norm', 'ffn2_w_in', 'ffn2_w_out']
SHARDED = {'ffn1_w_in': ((1024, 5632), 1), 'ffn1_w_out': ((2816, 1024), 0), 'w_in': ((1024, 6312), 1),
           'mla_w_uq': ((384, 768), 1), 'mla_w_ukv': ((256, 1024), 1), 'ssd_conv_w': ((4, 1024), 1),
           'w_branch': ((3, 512, 1024), 2), 'w_out': ((1024, 1024), 0), 'ffn2_w_in': ((1024, 5632), 1),
           'ffn2_w_out': ((2816, 1024), 0)}
SHARDED_ORDER = [n for n in WEIGHTS if n in SHARDED]
SMALL_ORDER = [n for n in WEIGHTS if n not in SHARDED]
N_CHIPS = 4
PACK_COLS = 1024
ROW_ALIGN = 512


def _shard_shape(name):
    shape, ax = SHARDED[name]
    return tuple(d // N_CHIPS if i == ax else d for i, d in enumerate(shape))


def _layer_rows():
    n = sum(int(np.prod(_shard_shape(w))) for w in SHARDED_ORDER)
    rows = -(-n // PACK_COLS)
    return -(-rows // ROW_ALIGN) * ROW_ALIGN


RL = _layer_rows()
ROWS = DEPTH * RL
HALF = ROWS // 2


def _pick(dim, target):
    if dim <= target:
        return dim
    t = (target // LANES) * LANES
    while t >= LANES:
        if dim % t == 0:
            return t
        t -= LANES
    return dim


def _sigmoid(x):
    return 1.0 / (1.0 + jnp.exp(-x))


def _params(*sem):
    return pltpu.CompilerParams(dimension_semantics=sem, vmem_limit_bytes=56 * 1024 * 1024)


def _matmul(a, b, *, ta=False, tb=False, out_dtype=F32, scale=1.0, res=None, name):
    if ta:
        k_dim, m_dim = a.shape
    else:
        m_dim, k_dim = a.shape
    if tb:
        n_dim, k2 = b.shape
    else:
        k2, n_dim = b.shape
    assert k_dim == k2, (a.shape, b.shape, ta, tb)
    tm, tn, tk = _pick(m_dim, 512), _pick(n_dim, 512), _pick(k_dim, 512)
    nk = k_dim // tk
    dn = (((0 if ta else 1,), (1 if tb else 0,)), ((), ()))

    def body(*refs):
        if res is not None:
            a_ref, b_ref, r_ref, o_ref, acc = refs
        else:
            a_ref, b_ref, o_ref, acc = refs
        k = pl.program_id(2)

        @pl.when(k == 0)
        def _():
            acc[...] = jnp.zeros_like(acc)

        acc[...] += lax.dot_general(a_ref[...].astype(BF16), b_ref[...].astype(BF16), dn, preferred_element_type=F32)

        @pl.when(k == nk - 1)
        def _():
            r = acc[...]
            if scale != 1.0:
                r = r * scale
            if res is not None:
                r = r + r_ref[...]
            o_ref[...] = r.astype(out_dtype)

    a_spec = pl.BlockSpec((tk, tm), lambda i, j, k: (k, i)) if ta else pl.BlockSpec((tm, tk), lambda i, j, k: (i, k))
    b_spec = pl.BlockSpec((tn, tk), lambda i, j, k: (j, k)) if tb else pl.BlockSpec((tk, tn), lambda i, j, k: (k, j))
    in_specs = [a_spec, b_spec]
    args = [a, b]
    if res is not None:
        in_specs.append(pl.BlockSpec((tm, tn), lambda i, j, k: (i, j)))
        args.append(res)
    return pl.pallas_call(
        body, name=name, grid=(m_dim // tm, n_dim // tn, nk), in_specs=in_specs,
        out_specs=pl.BlockSpec((tm, tn), lambda i, j, k: (i, j)),
        out_shape=jax.ShapeDtypeStruct((m_dim, n_dim), out_dtype),
        scratch_shapes=[pltpu.VMEM((tm, tn), F32)],
        compiler_params=_params("parallel", "parallel", "arbitrary"))(*args)


def _rmsnorm_fwd(x, gain, name):
    s, d = x.shape
    tm = _pick(s, 512)

    def body(x_ref, g_ref, o_ref):
        xv = x_ref[...]
        r = lax.rsqrt(jnp.mean(xv * xv, axis=-1, keepdims=True) + EPS)
        o_ref[...] = (xv * r * g_ref[...]).astype(BF16)

    return pl.pallas_call(
        body, name=name, grid=(s // tm,),
        in_specs=[pl.BlockSpec((tm, d), lambda i: (i, 0)), pl.BlockSpec((1, d), lambda i: (0, 0))],
        out_specs=pl.BlockSpec((tm, d), lambda i: (i, 0)),
        out_shape=jax.ShapeDtypeStruct((s, d), BF16), compiler_params=_params("parallel"))(x, gain)


def _rmsnorm_bwd(x, gain, dh, dres, name):
    s, d = x.shape
    tm = _pick(s, 512)

    def body(x_ref, g_ref, dh_ref, dr_ref, dx_ref, dg_ref):
        @pl.when(pl.program_id(0) == 0)
        def _():
            dg_ref[...] = jnp.zeros_like(dg_ref)

        xv, dhv = x_ref[...], dh_ref[...]
        r = lax.rsqrt(jnp.mean(xv * xv, axis=-1, keepdims=True) + EPS)
        u = dhv * g_ref[...]
        dx_ref[...] = dr_ref[...] + r * u - xv * (r * r * r) * jnp.mean(xv * u, axis=-1, keepdims=True)
        dg_ref[...] += jnp.sum(dhv * xv * r, axis=0, keepdims=True)

    row = pl.BlockSpec((tm, d), lambda i: (i, 0))
    vec = pl.BlockSpec((1, d), lambda i: (0, 0))
    return pl.pallas_call(
        body, name=name, grid=(s // tm,), in_specs=[row, vec, row, row], out_specs=[row, vec],
        out_shape=[jax.ShapeDtypeStruct((s, d), F32), jax.ShapeDtypeStruct((1, d), F32)],
        compiler_params=_params("arbitrary"))(x, gain, dh, dres)


def _swiglu_fwd(a, name):
    s = a.shape[0]
    tm, tc = _pick(s, 512), _pick(D_FF, 1408)
    nj = D_FF // tc

    def body(g_ref, u_ref, o_ref):
        g, u = g_ref[...].astype(F32), u_ref[...].astype(F32)
        o_ref[...] = (g * _sigmoid(g) * u).astype(BF16)

    return pl.pallas_call(
        body, name=name, grid=(s // tm, nj),
        in_specs=[pl.BlockSpec((tm, tc), lambda i, j: (i, j)), pl.BlockSpec((tm, tc), lambda i, j: (i, j + nj))],
        out_specs=pl.BlockSpec((tm, tc), lambda i, j: (i, j)),
        out_shape=jax.ShapeDtypeStruct((s, D_FF), BF16), compiler_params=_params("parallel", "parallel"))(a, a)


def _swiglu_bwd(a, dact, name):
    s = a.shape[0]
    tm, tc = _pick(s, 512), _pick(D_FF, 1408)
    nj = D_FF // tc

    def body(g_ref, u_ref, d_ref, o_ref):
        g, u, d = g_ref[...].astype(F32), u_ref[...].astype(F32), d_ref[...].astype(F32)
        sg = _sigmoid(g)
        dgate = d * u * (sg * (1.0 + g * (1.0 - sg)))
        dup = d * g * sg
        o_ref[...] = jnp.where(pl.program_id(1) == 0, dgate, dup).astype(BF16)

    return pl.pallas_call(
        body, name=name, grid=(s // tm, 2, nj),
        in_specs=[pl.BlockSpec((tm, tc), lambda i, h, j: (i, j)), pl.BlockSpec((tm, tc), lambda i, h, j: (i, j + nj)),
                  pl.BlockSpec((tm, tc), lambda i, h, j: (i, j))],
        out_specs=pl.BlockSpec((tm, tc), lambda i, h, j: (i, h * nj + j)),
        out_shape=jax.ShapeDtypeStruct((s, 2 * D_FF), BF16),
        compiler_params=_params("parallel", "parallel", "parallel"))(a, a, dact)


_INV_SQRT2 = 0.7071067811865476
_INV_SQRT2PI = 0.3989422804014327


def _gelu(x):
    return 0.5 * x * (1.0 + lax.erf(x * _INV_SQRT2))


def _gelu_grad(x):
    return 0.5 * (1.0 + lax.erf(x * _INV_SQRT2)) + x * jnp.exp(-0.5 * x * x) * _INV_SQRT2PI


def _tril_mask():
    r = lax.broadcasted_iota(jnp.int32, (CHUNK, CHUNK), 0)
    c = lax.broadcasted_iota(jnp.int32, (CHUNK, CHUNK), 1)
    return r >= c


def _gmlp_fwd(p, v_gain, w_s, b_full, name):
    s = p.shape[0]
    tm = _pick(s, 512)
    nch = tm // CHUNK

    def body(uv_ref, g_ref, w_ref, b_ref, o_ref):
        gel = _gelu(uv_ref[...])
        u, v = gel[:, :GM_WIDTH], gel[:, GM_WIDTH:]
        r = lax.rsqrt(jnp.mean(v * v, axis=-1, keepdims=True) + EPS)
        vn = (v * r * g_ref[...]).astype(BF16)
        mask = _tril_mask()
        for g in range(GM_GROUPS):
            wm = jnp.where(mask, w_ref[g], 0.0).astype(BF16)
            for c in range(nch):
                rs, cs = slice(c * CHUNK, (c + 1) * CHUNK), slice(g * LANES, (g + 1) * LANES)
                sp = jnp.dot(wm, vn[rs, cs], preferred_element_type=F32) + b_ref[g]
                o_ref[rs, cs] = (u[rs, cs] * sp).astype(BF16)

    full3 = pl.BlockSpec((GM_GROUPS, CHUNK, CHUNK), lambda i: (0, 0, 0))
    return pl.pallas_call(
        body, name=name, grid=(s // tm,),
        in_specs=[pl.BlockSpec((tm, 2 * GM_WIDTH), lambda i: (i, C_UV // (2 * GM_WIDTH))),
                  pl.BlockSpec((1, GM_WIDTH), lambda i: (0, 0)), full3, full3],
        out_specs=pl.BlockSpec((tm, GM_WIDTH), lambda i: (i, 0)),
        out_shape=jax.ShapeDtypeStruct((s, GM_WIDTH), BF16), compiler_params=_params("parallel"))(p, v_gain, w_s, b_full)


def _gmlp_bwd(p, v_gain, w_s, b_full, dy, name):
    s = p.shape[0]
    tm = _pick(s, 512)
    nch = tm // CHUNK
    nsteps = s // tm

    def body(uv_ref, g_ref, w_ref, b_ref, dy_ref, duv_ref, dg_ref, dw_ref, db_ref, dvn_s, dbacc):
        step = pl.program_id(0)

        @pl.when(step == 0)
        def _():
            dg_ref[...] = jnp.zeros_like(dg_ref)
            dw_ref[...] = jnp.zeros_like(dw_ref)
            dbacc[...] = jnp.zeros_like(dbacc)

        uv = uv_ref[...]
        gel = _gelu(uv)
        u, v = gel[:, :GM_WIDTH], gel[:, GM_WIDTH:]
        r = lax.rsqrt(jnp.mean(v * v, axis=-1, keepdims=True) + EPS)
        gain = g_ref[...]
        vn32 = v * r * gain
        vn = vn32.astype(BF16)
        dy = dy_ref[...]
        mask = _tril_mask()
        for g in range(GM_GROUPS):
            wm = jnp.where(mask, w_ref[g], 0.0).astype(BF16)
            dwg = jnp.zeros((CHUNK, CHUNK), F32)
            dbg = jnp.zeros((CHUNK, LANES), F32)
            for c in range(nch):
                rs, cs = slice(c * CHUNK, (c + 1) * CHUNK), slice(g * LANES, (g + 1) * LANES)
                sp = jnp.dot(wm, vn[rs, cs], preferred_element_type=F32) + b_ref[g]
                dyc = dy[rs, cs]
                dsp = dyc * u[rs, cs]
                dsp16 = dsp.astype(BF16)
                duv_ref[rs, cs] = (dyc * sp * _gelu_grad(uv[rs, cs])).astype(BF16)
                dvn_s[rs, cs] = lax.dot_general(wm, dsp16, (((0,), (0,)), ((), ())), preferred_element_type=F32)
                dwg = dwg + lax.dot_general(dsp16, vn[rs, cs], (((1,), (1,)), ((), ())), preferred_element_type=F32)
                dbg = dbg + dsp
            dw_ref[g] += jnp.where(mask, dwg, 0.0)
            dbacc[:, g * LANES:(g + 1) * LANES] += dbg
        dvn = dvn_s[...]
        uu = dvn * gain
        dv = r * uu - v * (r * r * r) * jnp.mean(v * uu, axis=-1, keepdims=True)
        duv_ref[:, GM_WIDTH:] = (dv * _gelu_grad(uv[:, GM_WIDTH:])).astype(BF16)
        dg_ref[...] += jnp.sum(dvn * v * r, axis=0, keepdims=True)

        @pl.when(step == nsteps - 1)
        def _():
            for g in range(GM_GROUPS):
                db_ref[:, g:g + 1] = jnp.sum(dbacc[:, g * LANES:(g + 1) * LANES], axis=1, keepdims=True)

    full3 = pl.BlockSpec((GM_GROUPS, CHUNK, CHUNK), lambda i: (0, 0, 0))
    return pl.pallas_call(
        body, name=name, grid=(nsteps,),
        in_specs=[pl.BlockSpec((tm, 2 * GM_WIDTH), lambda i: (i, C_UV // (2 * GM_WIDTH))),
                  pl.BlockSpec((1, GM_WIDTH), lambda i: (0, 0)), full3, full3,
                  pl.BlockSpec((tm, GM_WIDTH), lambda i: (i, 0))],
        out_specs=[pl.BlockSpec((tm, 2 * GM_WIDTH), lambda i: (i, 0)), pl.BlockSpec((1, GM_WIDTH), lambda i: (0, 0)),
                   full3, pl.BlockSpec((CHUNK, GM_GROUPS), lambda i: (0, 0))],
        out_shape=[jax.ShapeDtypeStruct((s, 2 * GM_WIDTH), BF16), jax.ShapeDtypeStruct((1, GM_WIDTH), F32),
                   jax.ShapeDtypeStruct((GM_GROUPS, CHUNK, CHUNK), F32), jax.ShapeDtypeStruct((CHUNK, GM_GROUPS), F32)],
        scratch_shapes=[pltpu.VMEM((tm, GM_WIDTH), F32), pltpu.VMEM((CHUNK, GM_WIDTH), F32)],
        compiler_params=_params("arbitrary"))(p, v_gain, w_s, b_full, dy)


def _rope(x, ct, s1, s2):
    return x * ct + pltpu.roll(x, LANES - MLA_ROPE // 2, 1) * s1 + pltpu.roll(x, MLA_ROPE // 2, 1) * s2


def _rope_bwd(d, ct, s1, s2):
    return d * ct + pltpu.roll(d * s1, MLA_ROPE // 2, 1) + pltpu.roll(d * s2, LANES - MLA_ROPE // 2, 1)


def _head_norm(x, gain):
    r = lax.rsqrt(jnp.sum(x * x, axis=-1, keepdims=True) * (1.0 / MLA_QK) + EPS)
    return x * r * gain, r


def _head_norm_bwd(x, r, gain, d):
    u = d * gain
    return r * u - x * (r * r * r) * (jnp.sum(x * u, axis=-1, keepdims=True) * (1.0 / MLA_QK))


def _mla_specs(tm):
    cq = pl.BlockSpec((tm, MLA_Q_RANK), lambda i: (i, C_CQ // MLA_Q_RANK))
    ckv = pl.BlockSpec((tm, MLA_KV_RANK), lambda i: (i, C_CKV // MLA_KV_RANK))
    kr = pl.BlockSpec((tm, LANES), lambda i: (i, C_KR // LANES))
    tab = pl.BlockSpec((tm, LANES), lambda i: (i, 0))
    return cq, ckv, kr, tab


def _const(shape):
    return pl.BlockSpec(shape, lambda i: tuple(0 for _ in shape))


def _mla_pre_fwd(p, tabs, qn_g, kvn_g, wuq, wkv, gq, gk, name):
    s = p.shape[0]
    tm = _pick(s, 256)
    ct, s1, s2 = tabs

    def body(cq_ref, ckv_ref, kr_ref, ct_ref, s1_ref, s2_ref, qg_ref, kvg_ref, wuq_ref, wkv_ref, gq_ref, gk_ref,
             q_ref, k_ref, v_ref):
        cq, ckv, kr = cq_ref[...], ckv_ref[...], kr_ref[...]
        ctv, s1v, s2v = ct_ref[...], s1_ref[...], s2_ref[...]
        rq = lax.rsqrt(jnp.mean(cq * cq, axis=-1, keepdims=True) + EPS)
        q = jnp.dot((cq * rq * qg_ref[...]).astype(BF16), wuq_ref[...], preferred_element_type=F32)
        rk = lax.rsqrt(jnp.mean(ckv * ckv, axis=-1, keepdims=True) + EPS)
        kv = jnp.dot((ckv * rk * kvg_ref[...]).astype(BF16), wkv_ref[...], preferred_element_type=F32)
        v_ref[...] = kv[:, HP:].astype(BF16)
        for h in range(MLA_HEADS):
            hs = slice(h * LANES, (h + 1) * LANES)
            qh, _ = _head_norm(q[:, hs], gq_ref[...])
            q_ref[:, hs] = _rope(qh, ctv, s1v, s2v).astype(BF16)
            kh, _ = _head_norm(kv[:, hs] + kr, gk_ref[...])
            k_ref[:, hs] = _rope(kh, ctv, s1v, s2v).astype(BF16)

    cq_s, ckv_s, kr_s, tab_s = _mla_specs(tm)
    out = pl.BlockSpec((tm, HP), lambda i: (i, 0))
    return pl.pallas_call(
        body, name=name, grid=(s // tm,),
        in_specs=[cq_s, ckv_s, kr_s, tab_s, tab_s, tab_s, _const((1, MLA_Q_RANK)), _const((1, MLA_KV_RANK)),
                  _const((MLA_Q_RANK, HP)), _const((MLA_KV_RANK, 2 * HP)), _const((1, LANES)), _const((1, LANES))],
        out_specs=[out, out, out], out_shape=[jax.ShapeDtypeStruct((s, HP), BF16)] * 3,
        compiler_params=_params("parallel"))(p, p, p, ct, s1, s2, qn_g, kvn_g, wuq, wkv, gq, gk)


def _mla_pre_bwd(p, tabs, qn_g, kvn_g, wuq, wkv, gq, gk, dq, dk, dv, name):
    s = p.shape[0]
    tm = _pick(s, 256)
    ct, s1, s2 = tabs

    def body(cq_ref, ckv_ref, kr_ref, ct_ref, s1_ref, s2_ref, qg_ref, kvg_ref, wuq_ref, wkv_ref, gq_ref, gk_ref,
             dq_ref, dk_ref, dv_ref, dcq_ref, dckv_ref, dkr_ref, dwuq_ref, dwkv_ref, dqg_ref, dkvg_ref, dgq_ref, dgk_ref,
             dqp, dkvp):
        @pl.when(pl.program_id(0) == 0)
        def _():
            for ref in (dwuq_ref, dwkv_ref, dqg_ref, dkvg_ref, dgq_ref, dgk_ref):
                ref[...] = jnp.zeros_like(ref)

        cq, ckv, kr = cq_ref[...], ckv_ref[...], kr_ref[...]
        ctv, s1v, s2v = ct_ref[...], s1_ref[...], s2_ref[...]
        rq = lax.rsqrt(jnp.mean(cq * cq, axis=-1, keepdims=True) + EPS)
        qn = (cq * rq * qg_ref[...]).astype(BF16)
        q = jnp.dot(qn, wuq_ref[...], preferred_element_type=F32)
        rk = lax.rsqrt(jnp.mean(ckv * ckv, axis=-1, keepdims=True) + EPS)
        kvn = (ckv * rk * kvg_ref[...]).astype(BF16)
        kv = jnp.dot(kvn, wkv_ref[...], preferred_element_type=F32)
        gqv, gkv = gq_ref[...], gk_ref[...]
        dgq = jnp.zeros((1, LANES), F32)
        dgk = jnp.zeros((1, LANES), F32)
        dkr = jnp.zeros((tm, LANES), F32)
        for h in range(MLA_HEADS):
            hs = slice(h * LANES, (h + 1) * LANES)
            xq = q[:, hs]
            _, r = _head_norm(xq, gqv)
            d = _rope_bwd(dq_ref[:, hs], ctv, s1v, s2v)
            dgq = dgq + jnp.sum(d * xq * r, axis=0, keepdims=True)
            dqp[:, hs] = _head_norm_bwd(xq, r, gqv, d)
            xk = kv[:, hs] + kr
            _, r = _head_norm(xk, gkv)
            d = _rope_bwd(dk_ref[:, hs], ctv, s1v, s2v)
            dgk = dgk + jnp.sum(d * xk * r, axis=0, keepdims=True)
            dxk = _head_norm_bwd(xk, r, gkv, d)
            dkvp[:, hs] = dxk
            dkr = dkr + dxk
        dkvp[:, HP:] = dv_ref[...]
        dgq_ref[...] += dgq
        dgk_ref[...] += dgk
        dkr_ref[...] = dkr.astype(BF16)
        tn = (((0,), (0,)), ((), ()))
        nt = (((1,), (1,)), ((), ()))
        dq16 = dqp[...].astype(BF16)
        dwuq_ref[...] += lax.dot_general(qn, dq16, tn, preferred_element_type=F32)
        dqn = lax.dot_general(dq16, wuq_ref[...], nt, preferred_element_type=F32)
        dqg_ref[...] += jnp.sum(dqn * cq * rq, axis=0, keepdims=True)
        u = dqn * qg_ref[...]
        dcq_ref[...] = (rq * u - cq * (rq * rq * rq) * jnp.mean(cq * u, axis=-1, keepdims=True)).astype(BF16)
        dkv16 = dkvp[...].astype(BF16)
        dwkv_ref[...] += lax.dot_general(kvn, dkv16, tn, preferred_element_type=F32)
        dkvn = lax.dot_general(dkv16, wkv_ref[...], nt, preferred_element_type=F32)
        dkvg_ref[...] += jnp.sum(dkvn * ckv * rk, axis=0, keepdims=True)
        u = dkvn * kvg_ref[...]
        dckv_ref[...] = (rk * u - ckv * (rk * rk * rk) * jnp.mean(ckv * u, axis=-1, keepdims=True)).astype(BF16)

    cq_s, ckv_s, kr_s, tab_s = _mla_specs(tm)
    hd = pl.BlockSpec((tm, HP), lambda i: (i, 0))
    return pl.pallas_call(
        body, name=name, grid=(s // tm,),
        in_specs=[cq_s, ckv_s, kr_s, tab_s, tab_s, tab_s, _const((1, MLA_Q_RANK)), _const((1, MLA_KV_RANK)),
                  _const((MLA_Q_RANK, HP)), _const((MLA_KV_RANK, 2 * HP)), _const((1, LANES)), _const((1, LANES)),
                  hd, hd, hd],
        out_specs=[pl.BlockSpec((tm, MLA_Q_RANK), lambda i: (i, 0)), pl.BlockSpec((tm, MLA_KV_RANK), lambda i: (i, 0)),
                   pl.BlockSpec((tm, LANES), lambda i: (i, 0)), _const((MLA_Q_RANK, HP)), _const((MLA_KV_RANK, 2 * HP)),
                   _const((1, MLA_Q_RANK)), _const((1, MLA_KV_RANK)), _const((1, LANES)), _const((1, LANES))],
        out_shape=[jax.ShapeDtypeStruct((s, MLA_Q_RANK), BF16), jax.ShapeDtypeStruct((s, MLA_KV_RANK), BF16),
                   jax.ShapeDtypeStruct((s, LANES), BF16), jax.ShapeDtypeStruct((MLA_Q_RANK, HP), F32),
                   jax.ShapeDtypeStruct((MLA_KV_RANK, 2 * HP), F32), jax.ShapeDtypeStruct((1, MLA_Q_RANK), F32),
                   jax.ShapeDtypeStruct((1, MLA_KV_RANK), F32), jax.ShapeDtypeStruct((1, LANES), F32),
                   jax.ShapeDtypeStruct((1, LANES), F32)],
        scratch_shapes=[pltpu.VMEM((tm, HP), F32), pltpu.VMEM((tm, 2 * HP), F32)],
        compiler_params=_params("arbitrary"))(p, p, p, ct, s1, s2, qn_g, kvn_g, wuq, wkv, gq, gk, dq, dk, dv)


_ATT_SCALE = MLA_QK ** -0.5
_NEG = -1e30
_NT = (((1,), (1,)), ((), ()))
_TN = (((0,), (0,)), ((), ()))


def _causal(i, j, t):
    r = lax.broadcasted_iota(jnp.int32, (t, t), 0) + i * t
    c = lax.broadcasted_iota(jnp.int32, (t, t), 1) + j * t
    return r >= c


def _attn_fwd(q, k, v, name):
    s = q.shape[0]
    t = _pick(s, 512)
    n = s // t

    def body(q_ref, k_ref, v_ref, o_ref, lse_ref, m_s, l_s, acc):
        i, j = pl.program_id(1), pl.program_id(2)

        @pl.when(j == 0)
        def _():
            m_s[...] = jnp.full_like(m_s, _NEG)
            l_s[...] = jnp.zeros_like(l_s)
            acc[...] = jnp.zeros_like(acc)

        @pl.when(j <= i)
        def _():
            sc = lax.dot_general(q_ref[...], k_ref[...], _NT, preferred_element_type=F32) * _ATT_SCALE
            sc = jnp.where(_causal(i, j, t), sc, _NEG)
            m_new = jnp.maximum(m_s[...], jnp.max(sc, axis=-1, keepdims=True))
            alpha = jnp.exp(m_s[...] - m_new)
            pr = jnp.exp(sc - m_new)
            l_s[...] = alpha * l_s[...] + jnp.sum(pr, axis=-1, keepdims=True)
            acc[...] = alpha * acc[...] + jnp.dot(pr.astype(BF16), v_ref[...], preferred_element_type=F32)
            m_s[...] = m_new

        @pl.when(j == i)
        def _():
            o_ref[...] = acc[...] / l_s[...]
            lse_ref[...] = m_s[...] + jnp.log(l_s[...])

    qs = pl.BlockSpec((t, LANES), lambda h, i, j: (i, h))
    ks = pl.BlockSpec((t, LANES), lambda h, i, j: (jnp.minimum(j, i), h))
    return pl.pallas_call(
        body, name=name, grid=(MLA_HEADS, n, n), in_specs=[qs, ks, ks],
        out_specs=[qs, pl.BlockSpec((None, t, 1), lambda h, i, j: (h, i, 0))],
        out_shape=[jax.ShapeDtypeStruct((s, HP), F32), jax.ShapeDtypeStruct((MLA_HEADS, s, 1), F32)],
        scratch_shapes=[pltpu.VMEM((t, 1), F32), pltpu.VMEM((t, 1), F32), pltpu.VMEM((t, LANES), F32)],
        compiler_params=_params("parallel", "parallel", "arbitrary"))(q, k, v)


def _attn_bwd_dq(q, k, v, o, lse, do, name):
    s = q.shape[0]
    t = _pick(s, 512)
    n = s // t

    def body(q_ref, k_ref, v_ref, o_ref, lse_ref, do_ref, dq_ref, acc):
        i, j = pl.program_id(1), pl.program_id(2)

        @pl.when(j == 0)
        def _():
            acc[...] = jnp.zeros_like(acc)

        @pl.when(j <= i)
        def _():
            dov = do_ref[...]
            delta = jnp.sum(dov * o_ref[...], axis=-1, keepdims=True)
            sc = lax.dot_general(q_ref[...], k_ref[...], _NT, preferred_element_type=F32) * _ATT_SCALE
            sc = jnp.where(_causal(i, j, t), sc, _NEG)
            pr = jnp.exp(sc - lse_ref[...])
            dp = lax.dot_general(dov.astype(BF16), v_ref[...], _NT, preferred_element_type=F32)
            ds = (pr * (dp - delta) * _ATT_SCALE).astype(BF16)
            acc[...] += jnp.dot(ds, k_ref[...], preferred_element_type=F32)

        @pl.when(j == i)
        def _():
            dq_ref[...] = acc[...]

    qs = pl.BlockSpec((t, LANES), lambda h, i, j: (i, h))
    ks = pl.BlockSpec((t, LANES), lambda h, i, j: (jnp.minimum(j, i), h))
    ls = pl.BlockSpec((None, t, 1), lambda h, i, j: (h, i, 0))
    return pl.pallas_call(
        body, name=name, grid=(MLA_HEADS, n, n), in_specs=[qs, ks, ks, qs, ls, qs], out_specs=qs,
        out_shape=jax.ShapeDtypeStruct((s, HP), F32), scratch_shapes=[pltpu.VMEM((t, LANES), F32)],
        compiler_params=_params("parallel", "parallel", "arbitrary"))(q, k, v, o, lse, do)


def _attn_bwd_dkv(q, k, v, o, lse, do, name):
    s = q.shape[0]
    t = _pick(s, 512)
    n = s // t

    def body(q_ref, k_ref, v_ref, o_ref, lse_ref, do_ref, dk_ref, dv_ref, dk_acc, dv_acc):
        j, i = pl.program_id(1), pl.program_id(2)

        @pl.when(i == 0)
        def _():
            dk_acc[...] = jnp.zeros_like(dk_acc)
            dv_acc[...] = jnp.zeros_like(dv_acc)

        @pl.when(i >= j)
        def _():
            dov = do_ref[...]
            delta = jnp.sum(dov * o_ref[...], axis=-1, keepdims=True)
            sc = lax.dot_general(q_ref[...], k_ref[...], _NT, preferred_element_type=F32) * _ATT_SCALE
            sc = jnp.where(_causal(i, j, t), sc, _NEG)
            pr = jnp.exp(sc - lse_ref[...])
            do16 = dov.astype(BF16)
            dv_acc[...] += lax.dot_general(pr.astype(BF16), do16, _TN, preferred_element_type=F32)
            dp = lax.dot_general(do16, v_ref[...], _NT, preferred_element_type=F32)
            ds = (pr * (dp - delta) * _ATT_SCALE).astype(BF16)
            dk_acc[...] += lax.dot_general(ds, q_ref[...], _TN, preferred_element_type=F32)

        @pl.when(i == n - 1)
        def _():
            dk_ref[...] = dk_acc[...]
            dv_ref[...] = dv_acc[...]

    qs = pl.BlockSpec((t, LANES), lambda h, j, i: (jnp.maximum(i, j), h))
    ks = pl.BlockSpec((t, LANES), lambda h, j, i: (j, h))
    ls = pl.BlockSpec((None, t, 1), lambda h, j, i: (h, jnp.maximum(i, j), 0))
    return pl.pallas_call(
        body, name=name, grid=(MLA_HEADS, n, n), in_specs=[qs, ks, ks, qs, ls, qs], out_specs=[ks, ks],
        out_shape=[jax.ShapeDtypeStruct((s, HP), F32)] * 2,
        scratch_shapes=[pltpu.VMEM((t, LANES), F32), pltpu.VMEM((t, LANES), F32)],
        compiler_params=_params("parallel", "parallel", "arbitrary"))(q, k, v, o, lse, do)


XBC = HP + 2 * SSD_GROUPS * SSD_STATE
N_XBLK = XBC // LANES


def _conv_col(j):
    return jnp.where(j < HP // LANES, C_XS // LANES + j, C_BC // LANES + j - HP // LANES)


def _conv_fwd(p, conv_w, conv_b, name):
    s = p.shape[0]

    def body(x_ref, w_ref, b_ref, o_ref, pad):
        pad[0:8, :] = jnp.zeros((8, LANES), F32)
        pad[8:s + 8, :] = x_ref[...]
        acc = jnp.broadcast_to(b_ref[...], (s, LANES))
        for t in range(SSD_CONV):
            acc = acc + pad[pl.ds(8 - (SSD_CONV - 1) + t, s), :] * w_ref[t:t + 1, :]
        o_ref[...] = acc * _sigmoid(acc)

    return pl.pallas_call(
        body, name=name, grid=(N_XBLK,),
        in_specs=[pl.BlockSpec((s, LANES), lambda j: (0, _conv_col(j))), pl.BlockSpec((SSD_CONV, LANES), lambda j: (0, j)),
                  pl.BlockSpec((1, LANES), lambda j: (0, j))],
        out_specs=pl.BlockSpec((s, LANES), lambda j: (0, j)), out_shape=jax.ShapeDtypeStruct((s, XBC), F32),
        scratch_shapes=[pltpu.VMEM((s + 8, LANES), F32)], compiler_params=_params("parallel"))(p, conv_w, conv_b)


def _conv_bwd(p, conv_w, conv_b, dact, name):
    s = p.shape[0]

    def body(x_ref, w_ref, b_ref, d_ref, dx_ref, dw_ref, db_ref, pad, padd):
        pad[0:8, :] = jnp.zeros((8, LANES), F32)
        pad[8:s + 8, :] = x_ref[...]
        acc = jnp.broadcast_to(b_ref[...], (s, LANES))
        for t in range(SSD_CONV):
            acc = acc + pad[pl.ds(8 - (SSD_CONV - 1) + t, s), :] * w_ref[t:t + 1, :]
        sg = _sigmoid(acc)
        dpre = d_ref[...] * (sg * (1.0 + acc * (1.0 - sg)))
        padd[0:s, :] = dpre
        padd[s:s + 8, :] = jnp.zeros((8, LANES), F32)
        dx = jnp.zeros((s, LANES), F32)
        for t in range(SSD_CONV):
            dx = dx + padd[pl.ds(SSD_CONV - 1 - t, s), :] * w_ref[t:t + 1, :]
            dw_ref[t:t + 1, :] = jnp.sum(dpre * pad[pl.ds(8 - (SSD_CONV - 1) + t, s), :], axis=0, keepdims=True)
        dx_ref[...] = dx.astype(BF16)
        db_ref[...] = jnp.sum(dpre, axis=0, keepdims=True)

    blk = pl.BlockSpec((s, LANES), lambda j: (0, j))
    return pl.pallas_call(
        body, name=name, grid=(N_XBLK,),
        in_specs=[pl.BlockSpec((s, LANES), lambda j: (0, _conv_col(j))), pl.BlockSpec((SSD_CONV, LANES), lambda j: (0, j)),
                  pl.BlockSpec((1, LANES), lambda j: (0, j)), blk],
        out_specs=[blk, pl.BlockSpec((SSD_CONV, LANES), lambda j: (0, j)), pl.BlockSpec((1, LANES), lambda j: (0, j))],
        out_shape=[jax.ShapeDtypeStruct((s, XBC), BF16), jax.ShapeDtypeStruct((SSD_CONV, XBC), F32),
                   jax.ShapeDtypeStruct((1, XBC), F32)],
        scratch_shapes=[pltpu.VMEM((s + 8, LANES), F32), pltpu.VMEM((s + 8, LANES), F32)],
        compiler_params=_params("parallel"))(p, conv_w, conv_b, dact)


def _softplus(x):
    return jnp.maximum(x, 0.0) + jnp.log(1.0 + jnp.exp(-jnp.abs(x)))


def _dt_fwd(p, dt_bias, a_log, name):
    s = p.shape[0]
    tm = _pick(s, 512)

    def body(x_ref, b_ref, a_ref, dt_ref, da_ref):
        dtv = _softplus(x_ref[...] + b_ref[...])
        dav = dtv * (-jnp.exp(a_ref[...]))
        for h in range(SSD_HEADS):
            hs = slice(h * LANES, (h + 1) * LANES)
            dt_ref[:, hs] = jnp.broadcast_to(dtv[:, h:h + 1], (tm, LANES))
            da_ref[:, hs] = jnp.broadcast_to(dav[:, h:h + 1], (tm, LANES))

    out = pl.BlockSpec((tm, HP), lambda i: (i, 0))
    return pl.pallas_call(
        body, name=name, grid=(s // tm,),
        in_specs=[pl.BlockSpec((tm, LANES), lambda i: (i, C_DT // LANES)), _const((1, LANES)), _const((1, LANES))],
        out_specs=[out, out], out_shape=[jax.ShapeDtypeStruct((s, HP), F32)] * 2,
        compiler_params=_params("parallel"))(p, dt_bias, a_log)


def _dt_bwd(p, dt_bias, a_log, dda, ddtx, name):
    s = p.shape[0]
    tm = _pick(s, 512)

    def body(x_ref, b_ref, a_ref, dda_ref, ddtx_ref, dx_ref, db_ref, dal_ref):
        @pl.when(pl.program_id(0) == 0)
        def _():
            db_ref[...] = jnp.zeros_like(db_ref)
            dal_ref[...] = jnp.zeros_like(dal_ref)

        x = x_ref[...] + b_ref[...]
        dtv = _softplus(x)
        av = -jnp.exp(a_ref[...])
        lane = lax.broadcasted_iota(jnp.int32, (tm, LANES), 1)
        pa = jnp.zeros((tm, LANES), F32)
        px = jnp.zeros((tm, LANES), F32)
        for h in range(SSD_HEADS):
            pa = jnp.where(lane == h, dda_ref[:, h * LANES:(h + 1) * LANES], pa)
            px = jnp.where(lane == h, ddtx_ref[:, h * LANES:(h + 1) * LANES], px)
        draw = (pa * av + px) * _sigmoid(x)
        dx_ref[...] = draw.astype(BF16)
        db_ref[...] += jnp.sum(draw, axis=0, keepdims=True)
        dal_ref[...] += jnp.sum(pa * dtv, axis=0, keepdims=True) * av

    hd = pl.BlockSpec((tm, HP), lambda i: (i, 0))
    return pl.pallas_call(
        body, name=name, grid=(s // tm,),
        in_specs=[pl.BlockSpec((tm, LANES), lambda i: (i, C_DT // LANES)), _const((1, LANES)), _const((1, LANES)), hd, hd],
        out_specs=[pl.BlockSpec((tm, LANES), lambda i: (i, 0)), _const((1, LANES)), _const((1, LANES))],
        out_shape=[jax.ShapeDtypeStruct((s, LANES), BF16), jax.ShapeDtypeStruct((1, LANES), F32),
                   jax.ShapeDtypeStruct((1, LANES), F32)],
        compiler_params=_params("arbitrary"))(p, dt_bias, a_log, dda, ddtx)


def _cumsum_rows(x):
    row = lax.broadcasted_iota(jnp.int32, x.shape, 0)
    k = 1
    while k < x.shape[0]:
        x = x + jnp.where(row >= k, pltpu.roll(x, k, 0), 0.0)
        k *= 2
    return x


def _rev_cumsum_rows(x):
    n = x.shape[0]
    row = lax.broadcasted_iota(jnp.int32, x.shape, 0)
    k = 1
    while k < n:
        x = x + jnp.where(row < n - k, pltpu.roll(x, n - k, 0), 0.0)
        k *= 2
    return x


HPG = SSD_HEADS // SSD_GROUPS


def _chunk_terms(da, b_mat, c_mat):
    cs = _cumsum_rows(da)
    mask = _tril_mask()
    lm = jnp.exp(jnp.where(mask, cs - cs.T, _NEG))
    g = lax.dot_general(c_mat, b_mat, _NT, preferred_element_type=F32)
    cl = cs[CHUNK - 1:CHUNK, :]
    return cs, lm, g, cl


def _scan_fwd(xbc, dtb, dab, name):
    s = xbc.shape[0]
    nc = s // CHUNK

    def body(x_ref, b_ref, c_ref, dt_ref, da_ref, y_ref, sin_ref, state):
        c, hh = pl.program_id(1), pl.program_id(2)

        @pl.when(c == 0)
        def _():
            state[hh] = jnp.zeros((SSD_STATE, LANES), F32)

        st = state[hh]
        sin_ref[...] = st
        b16, c16 = b_ref[...].astype(BF16), c_ref[...].astype(BF16)
        cs, lm, g, cl = _chunk_terms(da_ref[...], b16, c16)
        xd = (x_ref[...] * dt_ref[...]).astype(BF16)
        y = jnp.dot((g * lm).astype(BF16), xd, preferred_element_type=F32)
        y = y + jnp.dot(c16, st.astype(BF16), preferred_element_type=F32) * jnp.exp(cs)
        y_ref[...] = y
        bd = (b_ref[...] * jnp.exp(cl - cs)).astype(BF16)
        state[hh] = jnp.exp(cl) * st + lax.dot_general(bd, xd, _TN, preferred_element_type=F32)

    hd = pl.BlockSpec((CHUNK, LANES), lambda g, c, hh: (c, g * HPG + hh))
    return pl.pallas_call(
        body, name=name, grid=(SSD_GROUPS, nc, HPG),
        in_specs=[hd, pl.BlockSpec((CHUNK, LANES), lambda g, c, hh: (c, HP // LANES + g)),
                  pl.BlockSpec((CHUNK, LANES), lambda g, c, hh: (c, HP // LANES + SSD_GROUPS + g)), hd, hd],
        out_specs=[hd, pl.BlockSpec((None, None, SSD_STATE, LANES), lambda g, c, hh: (g * HPG + hh, c, 0, 0))],
        out_shape=[jax.ShapeDtypeStruct((s, HP), F32), jax.ShapeDtypeStruct((SSD_HEADS, nc, SSD_STATE, LANES), F32)],
        scratch_shapes=[pltpu.VMEM((HPG, SSD_STATE, LANES), F32)],
        compiler_params=_params("parallel", "arbitrary", "arbitrary"))(xbc, xbc, xbc, dtb, dab)


def _scan_bwd(xbc, dtb, dab, s_in, dy, d_vec, name):
    s = xbc.shape[0]
    nc = s // CHUNK

    def body(x_ref, b_ref, c_ref, dt_ref, da_ref, sin_ref, dy_ref, dv_ref, dx_ref, db_ref, dc_ref, dda_ref, ddtx_ref, dstate):
        c, hh = pl.program_id(1), pl.program_id(2)

        @pl.when(c == 0)
        def _():
            dstate[hh] = jnp.zeros((SSD_STATE, LANES), F32)

        st, ds = sin_ref[...], dstate[hh]
        st16, ds16 = st.astype(BF16), ds.astype(BF16)
        xv, bv, dtv, dyv = x_ref[...], b_ref[...], dt_ref[...], dy_ref[...]
        b16, c16 = bv.astype(BF16), c_ref[...].astype(BF16)
        cs, lm, g, cl = _chunk_terms(da_ref[...], b16, c16)
        ecs, ecl = jnp.exp(cs), jnp.exp(cl)
        decay = jnp.exp(cl - cs)
        xd32 = xv * dtv
        xd = xd32.astype(BF16)
        dy16 = dyv.astype(BF16)
        dye = (dyv * ecs).astype(BF16)
        yoff = jnp.dot(c16, st16, preferred_element_type=F32) * ecs
        dcs = jnp.sum(dyv * yoff, axis=-1, keepdims=True)
        dcm = lax.dot_general(dye, st16, _NT, preferred_element_type=F32)
        ds_in = ecl * ds + lax.dot_general(c16, dye, _TN, preferred_element_type=F32)
        dcl = jnp.sum(jnp.sum(ds * st, axis=0, keepdims=True), axis=1, keepdims=True) * ecl[:, 0:1]
        bd32 = bv * decay
        bd = bd32.astype(BF16)
        qm = lax.dot_general(xd, ds16, _NT, preferred_element_type=F32)
        dbm = qm * decay
        w = jnp.sum(bd32 * qm, axis=-1, keepdims=True)
        dcs = dcs - w
        dcl = dcl + jnp.sum(w, axis=0, keepdims=True)
        dxd = jnp.dot(bd, ds16, preferred_element_type=F32)
        m16 = (g * lm).astype(BF16)
        dm = lax.dot_general(dy16, xd, _NT, preferred_element_type=F32)
        dxd = dxd + lax.dot_general(m16, dy16, _TN, preferred_element_type=F32)
        dg = dm * lm
        dg16 = dg.astype(BF16)
        tt = dg * g
        dcm = dcm + jnp.dot(dg16, b16, preferred_element_type=F32)
        dbm = dbm + lax.dot_general(dg16, c16, _TN, preferred_element_type=F32)
        dcs = dcs + jnp.sum(tt, axis=-1, keepdims=True) - jnp.sum(tt.T, axis=-1, keepdims=True)
        row = lax.broadcasted_iota(jnp.int32, (CHUNK, 1), 0)
        dcs = dcs + jnp.where(row == CHUNK - 1, dcl, 0.0)
        dda_ref[...] = _rev_cumsum_rows(jnp.broadcast_to(dcs, (CHUNK, LANES)))
        ddtx_ref[...] = jnp.broadcast_to(jnp.sum(dxd * xv, axis=-1, keepdims=True), (CHUNK, LANES))
        dx_ref[...] = dxd * dtv + dyv * dv_ref[...]
        dstate[hh] = ds_in

        @pl.when(hh == 0)
        def _():
            db_ref[...] = dbm
            dc_ref[...] = dcm

        @pl.when(hh != 0)
        def _():
            db_ref[...] += dbm
            dc_ref[...] += dcm

    hd = pl.BlockSpec((CHUNK, LANES), lambda g, c, hh: (nc - 1 - c, g * HPG + hh))
    gp = pl.BlockSpec((CHUNK, LANES), lambda g, c, hh: (nc - 1 - c, g))
    return pl.pallas_call(
        body, name=name, grid=(SSD_GROUPS, nc, HPG),
        in_specs=[hd, pl.BlockSpec((CHUNK, LANES), lambda g, c, hh: (nc - 1 - c, HP // LANES + g)),
                  pl.BlockSpec((CHUNK, LANES), lambda g, c, hh: (nc - 1 - c, HP // LANES + SSD_GROUPS + g)), hd, hd,
                  pl.BlockSpec((None, None, SSD_STATE, LANES), lambda g, c, hh: (g * HPG + hh, nc - 1 - c, 0, 0)), hd,
                  pl.BlockSpec((1, LANES), lambda g, c, hh: (0, g * HPG + hh))],
        out_specs=[hd, gp, gp, hd, hd],
        out_shape=[jax.ShapeDtypeStruct((s, HP), F32), jax.ShapeDtypeStruct((s, SSD_GROUPS * SSD_STATE), F32),
                   jax.ShapeDtypeStruct((s, SSD_GROUPS * SSD_STATE), F32), jax.ShapeDtypeStruct((s, HP), F32),
                   jax.ShapeDtypeStruct((s, HP), F32)],
        scratch_shapes=[pltpu.VMEM((HPG, SSD_STATE, LANES), F32)],
        compiler_params=_params("parallel", "arbitrary", "arbitrary"))(xbc, xbc, xbc, dtb, dab, s_in, dy, d_vec)


_GN = SSD_INNER // SSD_GROUPS
_GW = HP // SSD_GROUPS


def _ssd_post_fwd(y, xbc, p, d_vec, gain, name):
    s = y.shape[0]
    tm = _pick(s, 512)

    def body(y_ref, x_ref, z_ref, d_ref, g_ref, o_ref):
        z = z_ref[...]
        y2 = (y_ref[...] + x_ref[...] * d_ref[...]) * (z * _sigmoid(z))
        for g in range(SSD_GROUPS):
            gs = slice(g * _GW, (g + 1) * _GW)
            yg = y2[:, gs]
            r = lax.rsqrt(jnp.sum(yg * yg, axis=-1, keepdims=True) * (1.0 / _GN) + EPS)
            o_ref[:, gs] = (yg * r * g_ref[:, gs]).astype(BF16)

    hd = pl.BlockSpec((tm, HP), lambda i: (i, 0))
    return pl.pallas_call(
        body, name=name, grid=(s // tm,),
        in_specs=[hd, hd, pl.BlockSpec((tm, HP), lambda i: (i, C_Z // HP)), _const((1, HP)), _const((1, HP))],
        out_specs=hd, out_shape=jax.ShapeDtypeStruct((s, HP), BF16), compiler_params=_params("parallel"))(y, xbc, p, d_vec, gain)


def _ssd_post_bwd(y, xbc, p, d_vec, gain, dyn, name):
    s = y.shape[0]
    tm = _pick(s, 512)

    def body(y_ref, x_ref, z_ref, d_ref, g_ref, dn_ref, dy_ref, dz_ref, dg_ref, dd_ref):
        @pl.when(pl.program_id(0) == 0)
        def _():
            dg_ref[...] = jnp.zeros_like(dg_ref)
            dd_ref[...] = jnp.zeros_like(dd_ref)

        z, xv = z_ref[...], x_ref[...]
        sg = _sigmoid(z)
        sz = z * sg
        yt = y_ref[...] + xv * d_ref[...]
        y2 = yt * sz
        for g in range(SSD_GROUPS):
            gs = slice(g * _GW, (g + 1) * _GW)
            yg, dn = y2[:, gs], dn_ref[:, gs]
            r = lax.rsqrt(jnp.sum(yg * yg, axis=-1, keepdims=True) * (1.0 / _GN) + EPS)
            u = dn * g_ref[:, gs]
            dy2 = r * u - yg * (r * r * r) * (jnp.sum(yg * u, axis=-1, keepdims=True) * (1.0 / _GN))
            dg_ref[:, gs] += jnp.sum(dn * yg * r, axis=0, keepdims=True)
            dyt = dy2 * sz[:, gs]
            dy_ref[:, gs] = dyt
            dz_ref[:, gs] = (dy2 * yt[:, gs] * (sg[:, gs] * (1.0 + z[:, gs] * (1.0 - sg[:, gs])))).astype(BF16)
            dd_ref[:, gs] += jnp.sum(dyt * xv[:, gs], axis=0, keepdims=True)

    hd = pl.BlockSpec((tm, HP), lambda i: (i, 0))
    return pl.pallas_call(
        body, name=name, grid=(s // tm,),
        in_specs=[hd, hd, pl.BlockSpec((tm, HP), lambda i: (i, C_Z // HP)), _const((1, HP)), _const((1, HP)), hd],
        out_specs=[hd, hd, _const((1, HP)), _const((1, HP))],
        out_shape=[jax.ShapeDtypeStruct((s, HP), F32), jax.ShapeDtypeStruct((s, HP), BF16),
                   jax.ShapeDtypeStruct((1, HP), F32), jax.ShapeDtypeStruct((1, HP), F32)],
        compiler_params=_params("arbitrary"))(y, xbc, p, d_vec, gain, dyn)


def _merge_fwd(p, t0, t1, t2, name):
    s = p.shape[0]
    tm = _pick(s, 256)

    def body(g_ref, t0_ref, t1_ref, t2_ref, o_ref):
        acc = jnp.zeros((tm, D_MODEL), F32)
        for i, t_ref in enumerate((t0_ref, t1_ref, t2_ref)):
            acc = acc + _sigmoid(g_ref[:, i * D_MODEL:(i + 1) * D_MODEL]) * t_ref[...]
        o_ref[...] = acc.astype(BF16)

    row = pl.BlockSpec((tm, D_MODEL), lambda i: (i, 0))
    return pl.pallas_call(
        body, name=name, grid=(s // tm,),
        in_specs=[pl.BlockSpec((tm, 3 * D_MODEL), lambda i: (i, C_G // (3 * D_MODEL))), row, row, row],
        out_specs=row, out_shape=jax.ShapeDtypeStruct((s, D_MODEL), BF16), compiler_params=_params("parallel"))(p, t0, t1, t2)


def _merge_bwd(p, t0, t1, t2, dm, name):
    s = p.shape[0]
    tm = _pick(s, 256)

    def body(g_ref, t0_ref, t1_ref, t2_ref, dm_ref, d0_ref, d1_ref, d2_ref, dg_ref):
        dmv = dm_ref[...]
        for i, (t_ref, d_ref) in enumerate(((t0_ref, d0_ref), (t1_ref, d1_ref), (t2_ref, d2_ref))):
            cs = slice(i * D_MODEL, (i + 1) * D_MODEL)
            sg = _sigmoid(g_ref[:, cs])
            d_ref[...] = (dmv * sg).astype(BF16)
            dg_ref[:, cs] = (dmv * t_ref[...] * sg * (1.0 - sg)).astype(BF16)

    row = pl.BlockSpec((tm, D_MODEL), lambda i: (i, 0))
    wide = pl.BlockSpec((tm, 3 * D_MODEL), lambda i: (i, 0))
    return pl.pallas_call(
        body, name=name, grid=(s // tm,),
        in_specs=[pl.BlockSpec((tm, 3 * D_MODEL), lambda i: (i, C_G // (3 * D_MODEL))), row, row, row, row],
        out_specs=[row, row, row, wide],
        out_shape=[jax.ShapeDtypeStruct((s, D_MODEL), BF16)] * 3 + [jax.ShapeDtypeStruct((s, 3 * D_MODEL), BF16)],
        compiler_params=_params("parallel"))(p, t0, t1, t2, dm)


def _loss_head(y, target, name):
    s, d = y.shape
    tm = _pick(s, 512)

    def body(y_ref, t_ref, dy_ref, sq_ref):
        @pl.when(pl.program_id(0) == 0)
        def _():
            sq_ref[...] = jnp.zeros_like(sq_ref)

        e = y_ref[...] - t_ref[...]
        dy_ref[...] = e * (1.0 / d)
        sq_ref[...] += jnp.sum(e * e, axis=0, keepdims=True)

    row = pl.BlockSpec((tm, d), lambda i: (i, 0))
    return pl.pallas_call(
        body, name=name, grid=(s // tm,), in_specs=[row, row], out_specs=[row, _const((1, d))],
        out_shape=[jax.ShapeDtypeStruct((s, d), F32), jax.ShapeDtypeStruct((1, d), F32)],
        compiler_params=_params("arbitrary"))(y, target)


def _adamw(w, g, m, v, name):
    rows, cols = w.shape
    tr = rows
    for cand in (512, 256, 128, 64, 32, 16, 8):
        if rows % cand == 0 and cand * cols * 4 <= 3 * 1024 * 1024:
            tr = cand
            break
    c1 = 1.0 - ADAM_B1 ** ADAM_STEP
    c2 = 1.0 - ADAM_B2 ** ADAM_STEP

    def body(w_ref, g_ref, m_ref, v_ref, d_ref, nm_ref, nv_ref):
        gv = g_ref[...]
        nm = ADAM_B1 * m_ref[...] + (1.0 - ADAM_B1) * gv
        nv = ADAM_B2 * v_ref[...] + (1.0 - ADAM_B2) * (gv * gv)
        nm_ref[...] = nm
        nv_ref[...] = nv
        d_ref[...] = -ADAM_LR * ((nm / c1) / (jnp.sqrt(nv / c2) + ADAM_EPS) + ADAM_WD * w_ref[...])

    blk = pl.BlockSpec((tr, cols), lambda i: (i, 0))
    return pl.pallas_call(
        body, name=name, grid=(rows // tr,), in_specs=[blk] * 4, out_specs=[blk] * 3,
        out_shape=[jax.ShapeDtypeStruct((rows, cols), F32)] * 3, compiler_params=_params("parallel"))(w, g, m, v)


ANY = pl.BlockSpec(memory_space=pl.ANY)


def _me():
    return lax.axis_index("x"), lax.axis_index("y"), lax.axis_index("c")


def _other_chips(x, y):
    return [(1 - x, y), (x, 1 - y), (1 - x, 1 - y)]


def _chip_index(cx, cy):
    return 2 * cx + cy


def _gather_shards(mine):
    rows, cols = mine.shape
    half = rows // 2

    def body(in_ref, out_ref, send_sems, recv_sems, local_sem):
        x, y, c = _me()
        sib = (x, y, 1 - c)
        chips = _other_chips(x, y)
        mychip = _chip_index(x, y)

        def part(chip, hc):
            return out_ref.at[chip, pl.ds(hc * half, half), :]

        def copy(k, chip, hc, to, src=None):
            dst = part(chip, hc)
            return pltpu.make_async_remote_copy(src_ref=dst if src is None else src, dst_ref=dst, send_sem=send_sems.at[k],
                                                recv_sem=recv_sems.at[k], device_id=to, device_id_type=MESH)

        local = pltpu.make_async_copy(in_ref, out_ref.at[mychip], local_sem)
        local.start()
        mine_half = in_ref.at[pl.ds(c * half, half), :]
        first = [copy(j, mychip, c, (*chip, c), src=mine_half) for j, chip in enumerate(chips)]
        for cp in first:
            cp.start()
        passed = [copy(3 + j, _chip_index(*chip), c, sib) for j, chip in enumerate(chips)]
        for j, chip in enumerate(chips):
            copy(j, _chip_index(*chip), c, (x, y, c)).wait_recv()
            passed[j].start()
        for j, chip in enumerate(chips):
            copy(3 + j, _chip_index(*chip), 1 - c, (x, y, c)).wait_recv()
        for cp in first + passed:
            cp.wait_send()
        local.wait()

    return pl.pallas_call(
        body, name="gather_shards", in_specs=[ANY], out_specs=ANY,
        out_shape=jax.ShapeDtypeStruct((N_CHIPS, rows, cols), mine.dtype),
        scratch_shapes=[pltpu.SemaphoreType.DMA((6,)), pltpu.SemaphoreType.DMA((6,)), pltpu.SemaphoreType.DMA(())])(mine)


def _pair_exchange(g):
    _, rows, cols = g.shape
    half = rows // 2

    def body(g_ref, out_ref, send_sem, recv_sem):
        x, y, c = _me()
        cp = pltpu.make_async_remote_copy(src_ref=g_ref.at[:, pl.ds((1 - c) * half, half), :], dst_ref=out_ref,
                                          send_sem=send_sem, recv_sem=recv_sem, device_id=(x, y, 1 - c), device_id_type=MESH)
        cp.start()
        cp.wait()

    return pl.pallas_call(
        body, name="pair_exchange", in_specs=[ANY], out_specs=ANY,
        out_shape=jax.ShapeDtypeStruct((N_CHIPS, half, cols), g.dtype),
        scratch_shapes=[pltpu.SemaphoreType.DMA(()), pltpu.SemaphoreType.DMA(())])(g)


def _pair_add(g, got, idx):
    _, rows, cols = g.shape
    half = rows // 2
    tr = ROW_ALIGN
    nb = half // tr

    def body(idx_ref, g_ref, r_ref, o16_ref, own_ref):
        tot = g_ref[...] + r_ref[...]
        o16_ref[...] = tot.astype(BF16)

        @pl.when(pl.program_id(1) == idx_ref[1])
        def _():
            own_ref[...] = tot

    return pl.pallas_call(
        body, name="pair_add",
        grid_spec=pltpu.PrefetchScalarGridSpec(
            num_scalar_prefetch=1, grid=(nb, N_CHIPS),
            in_specs=[pl.BlockSpec((None, tr, cols), lambda i, k, idx: (k, idx[0] * nb + i, 0)),
                      pl.BlockSpec((None, tr, cols), lambda i, k, idx: (k, i, 0))],
            out_specs=[pl.BlockSpec((None, tr, cols), lambda i, k, idx: (k, i, 0)),
                       pl.BlockSpec((tr, cols), lambda i, k, idx: (i, 0))]),
        out_shape=[jax.ShapeDtypeStruct((N_CHIPS, half, cols), BF16), jax.ShapeDtypeStruct((half, cols), F32)],
        compiler_params=_params("parallel", "arbitrary"))(idx, g, got)


def _chip_exchange(part):
    n, half, cols = part.shape

    def body(in_ref, out_ref, send_sems, recv_sems, local_sem):
        x, y, c = _me()
        chips = _other_chips(x, y)
        mychip = _chip_index(x, y)
        local = pltpu.make_async_copy(in_ref.at[mychip], out_ref.at[mychip], local_sem)
        local.start()

        def copy(j, chip):
            return pltpu.make_async_remote_copy(src_ref=in_ref.at[_chip_index(*chip)], dst_ref=out_ref.at[mychip],
                                                send_sem=send_sems.at[j], recv_sem=recv_sems.at[j],
                                                device_id=(*chip, c), device_id_type=MESH)

        cps = [copy(j, chip) for j, chip in enumerate(chips)]
        for cp in cps:
            cp.start()
        for j, chip in enumerate(chips):
            pltpu.make_async_remote_copy(src_ref=in_ref.at[mychip], dst_ref=out_ref.at[_chip_index(*chip)],
                                         send_sem=send_sems.at[j], recv_sem=recv_sems.at[j],
                                         device_id=(*chip, c), device_id_type=MESH).wait_recv()
        for cp in cps:
            cp.wait_send()
        local.wait()

    return pl.pallas_call(
        body, name="chip_exchange", in_specs=[ANY], out_specs=ANY, out_shape=jax.ShapeDtypeStruct((n, half, cols), part.dtype),
        scratch_shapes=[pltpu.SemaphoreType.DMA((3,)), pltpu.SemaphoreType.DMA((3,)), pltpu.SemaphoreType.DMA(())])(part)


def _chip_add(own, got, idx):
    half, cols = own.shape
    tr = ROW_ALIGN

    def body(idx_ref, own_ref, got_ref, o_ref):
        acc = jnp.zeros((tr, cols), F32)
        for k in range(N_CHIPS):
            acc = acc + jnp.where(idx_ref[1] == k, own_ref[...], got_ref[k].astype(F32))
        o_ref[...] = acc

    return pl.pallas_call(
        body, name="chip_add",
        grid_spec=pltpu.PrefetchScalarGridSpec(
            num_scalar_prefetch=1, grid=(half // tr,),
            in_specs=[pl.BlockSpec((tr, cols), lambda i, idx: (i, 0)), pl.BlockSpec((N_CHIPS, tr, cols), lambda i, idx: (0, i, 0))],
            out_specs=pl.BlockSpec((tr, cols), lambda i, idx: (i, 0))),
        out_shape=jax.ShapeDtypeStruct((half, cols), F32), compiler_params=_params("parallel"))(idx, own, got)


def _pair_share(mine):
    half, cols = mine.shape

    def body(in_ref, out_ref, send_sem, recv_sem, local_sem):
        x, y, c = _me()
        dst = out_ref.at[pl.ds(c * half, half), :]
        local = pltpu.make_async_copy(in_ref, dst, local_sem)
        local.start()
        cp = pltpu.make_async_remote_copy(src_ref=in_ref, dst_ref=dst, send_sem=send_sem, recv_sem=recv_sem,
                                          device_id=(x, y, 1 - c), device_id_type=MESH)
        cp.start()
        pltpu.make_async_remote_copy(src_ref=in_ref, dst_ref=out_ref.at[pl.ds((1 - c) * half, half), :], send_sem=send_sem,
                                     recv_sem=recv_sem, device_id=(x, y, 1 - c), device_id_type=MESH).wait_recv()
        cp.wait_send()
        local.wait()

    return pl.pallas_call(
        body, name="pair_share", in_specs=[ANY], out_specs=ANY, out_shape=jax.ShapeDtypeStruct((2 * half, cols), mine.dtype),
        scratch_shapes=[pltpu.SemaphoreType.DMA(()), pltpu.SemaphoreType.DMA(()), pltpu.SemaphoreType.DMA(())])(mine)


N_DEV = 8


def _all_exchange(v):
    r, cols = v.shape

    def body(in_ref, out_ref, send_sems, recv_sems, local_sem):
        x, y, c = _me()
        me = 4 * x + 2 * y + c
        local = pltpu.make_async_copy(in_ref, out_ref.at[me], local_sem)
        local.start()
        flip = lambda v, f: 1 - v if f else v
        peers = [(flip(x, fx), flip(y, fy), flip(c, fc)) for fx in (0, 1) for fy in (0, 1) for fc in (0, 1)][1:]
        cps = [pltpu.make_async_remote_copy(src_ref=in_ref, dst_ref=out_ref.at[me], send_sem=send_sems.at[j],
                                            recv_sem=recv_sems.at[j], device_id=peer, device_id_type=MESH)
               for j, peer in enumerate(peers)]
        for cp in cps:
            cp.start()
        for j, (px, py, pc) in enumerate(peers):
            pltpu.make_async_remote_copy(src_ref=in_ref, dst_ref=out_ref.at[4 * px + 2 * py + pc], send_sem=send_sems.at[j],
                                         recv_sem=recv_sems.at[j], device_id=(px, py, pc), device_id_type=MESH).wait_recv()
        for cp in cps:
            cp.wait_send()
        local.wait()

    return pl.pallas_call(
        body, name="all_exchange", in_specs=[ANY], out_specs=ANY, out_shape=jax.ShapeDtypeStruct((N_DEV, r, cols), v.dtype),
        scratch_shapes=[pltpu.SemaphoreType.DMA((7,)), pltpu.SemaphoreType.DMA((7,)), pltpu.SemaphoreType.DMA(())])(v)


def _sum_slots(a, name):
    n, r, cols = a.shape
    tr = _pick(r, 512) if r % 8 == 0 else r
    for cand in (512, 256, 128, 64, 32, 16, 8):
        if r % cand == 0:
            tr = cand
            break

    def body(a_ref, o_ref):
        acc = a_ref[0]
        for k in range(1, n):
            acc = acc + a_ref[k]
        o_ref[...] = acc

    return pl.pallas_call(
        body, name=name, grid=(r // tr,), in_specs=[pl.BlockSpec((n, tr, cols), lambda i: (0, i, 0))],
        out_specs=pl.BlockSpec((tr, cols), lambda i: (i, 0)), out_shape=jax.ShapeDtypeStruct((r, cols), F32),
        compiler_params=_params("parallel"))(a)


def _pack_shards(per_layer, dtype):
    rows = []
    for lw in per_layer:
        flat = jnp.concatenate([lw[n].reshape(-1).astype(dtype) for n in SHARDED_ORDER])
        flat = jnp.pad(flat, (0, RL * PACK_COLS - flat.shape[0]))
        rows.append(flat.reshape(RL, PACK_COLS))
    return jnp.concatenate(rows, axis=0)


def _unpack_shards(buf):
    lead = buf.shape[:-2]
    flat = buf.reshape(lead + (DEPTH, RL * PACK_COLS))
    out = []
    for l in range(DEPTH):
        off, lw = 0, {}
        for n in SHARDED_ORDER:
            shp = _shard_shape(n)
            size = int(np.prod(shp))
            lw[n] = flat[..., l, off:off + size].reshape(lead + shp)
            off += size
        out.append(lw)
    return out


def _join(name, stacked):
    ax = SHARDED[name][1]
    return jnp.concatenate([stacked[k] for k in range(N_CHIPS)], axis=ax)


def _split(name, full):
    ax = SHARDED[name][1]
    return jnp.stack(jnp.split(full, N_CHIPS, axis=ax))


def _heads_pad(a, real, axis):
    shp = a.shape
    a = a.reshape(shp[:axis] + (MLA_HEADS, real) + shp[axis + 1:])
    pad = [(0, 0)] * a.ndim
    pad[axis + 1] = (0, LANES - real)
    a = jnp.pad(a, pad)
    return a.reshape(shp[:axis] + (HP,) + shp[axis + 1:])


def _heads_unpad(a, real, axis):
    shp = a.shape
    a = a.reshape(shp[:axis] + (MLA_HEADS, LANES) + shp[axis + 1:])
    a = lax.slice_in_dim(a, 0, real, axis=axis + 1)
    return a.reshape(shp[:axis] + (MLA_HEADS * real,) + shp[axis + 1:])


def _lane_place(a, start):
    n = a.shape[-1]
    pad = [(0, 0)] * (a.ndim - 1) + [(start, LANES - start - n)]
    return jnp.pad(a, pad)


_O_UV, _O_CQ, _O_CKV, _O_KR, _O_Z, _O_XBC, _O_DT, _O_G = 0, 1024, 1408, 1664, 1696, 2208, 3232, 3240


def _w_in_pad(w):
    sl = lambda a, b: w[:, a:b]
    xs = _heads_pad(sl(_O_XBC, _O_XBC + SSD_INNER), SSD_HEAD_DIM, 1)
    bc = sl(_O_XBC + SSD_INNER, _O_DT)
    parts = [sl(_O_UV, _O_CQ), _heads_pad(sl(_O_Z, _O_XBC), SSD_HEAD_DIM, 1), xs, sl(_O_G, IN_COLS), bc, sl(_O_CKV, _O_KR),
             sl(_O_CQ, _O_CKV), _lane_place(sl(_O_KR, _O_Z), MLA_NOPE), _lane_place(sl(_O_DT, _O_G), 0),
             jnp.zeros((w.shape[0], PW - C_DT - LANES), w.dtype)]
    return jnp.concatenate(parts, axis=1)


def _w_in_unpad(g):
    sl = lambda a, n: g[:, a:a + n]
    parts = [sl(C_UV, 1024), sl(C_CQ, MLA_Q_RANK), sl(C_CKV, MLA_KV_RANK), sl(C_KR + MLA_NOPE, MLA_ROPE),
             _heads_unpad(sl(C_Z, HP), SSD_HEAD_DIM, 1), _heads_unpad(sl(C_XS, HP), SSD_HEAD_DIM, 1), sl(C_BC, 512),
             sl(C_DT, SSD_HEADS), sl(C_G, 3 * D_MODEL)]
    return jnp.concatenate(parts, axis=1)


def _xbc_pad(a):
    return jnp.concatenate([_heads_pad(a[..., :SSD_INNER], SSD_HEAD_DIM, a.ndim - 1), a[..., SSD_INNER:]], axis=-1)


def _xbc_unpad(a):
    return jnp.concatenate([_heads_unpad(a[..., :HP], SSD_HEAD_DIM, a.ndim - 1), a[..., HP:]], axis=-1)


def _rope_tables(positions):
    inv_freq = 1.0 / (ROPE_THETA ** (jnp.arange(0, MLA_ROPE, 2, dtype=F32) / MLA_ROPE))
    ang = positions.astype(F32)[:, None] * inv_freq
    cos, sin = jnp.cos(ang), jnp.sin(ang)
    s = positions.shape[0]
    half = MLA_ROPE // 2
    z = lambda n: jnp.zeros((s, n), F32)
    ct = jnp.concatenate([jnp.ones((s, MLA_NOPE), F32), cos, cos, z(LANES - MLA_QK)], axis=1)
    s1 = jnp.concatenate([z(MLA_NOPE), -sin, z(half), z(LANES - MLA_QK)], axis=1)
    s2 = jnp.concatenate([z(MLA_NOPE), z(half), sin, z(LANES - MLA_QK)], axis=1)
    return ct, s1, s2


def _layer_weights(full, small, l):
    w = {}
    for n in ('ffn1_w_in', 'ffn1_w_out', 'ffn2_w_in', 'ffn2_w_out', 'w_out'):
        w[n] = full[l][n]
    w['w_in'] = _w_in_pad(full[l]['w_in'])
    w['wuq'] = _heads_pad(full[l]['mla_w_uq'], MLA_QK, 1)
    ukv = full[l]['mla_w_ukv'].reshape(MLA_KV_RANK, MLA_HEADS, MLA_NOPE + MLA_V)
    zero = jnp.zeros((MLA_KV_RANK, MLA_HEADS, LANES - MLA_NOPE), ukv.dtype)
    wk = jnp.concatenate([ukv[:, :, :MLA_NOPE], zero], axis=2).reshape(MLA_KV_RANK, HP)
    wv = jnp.concatenate([ukv[:, :, MLA_NOPE:], zero], axis=2).reshape(MLA_KV_RANK, HP)
    w['wkv'] = jnp.concatenate([wk, wv], axis=1)
    wb = full[l]['w_branch']
    w['wb0'] = wb[0]
    w['wb1'] = _heads_pad(wb[1], MLA_V, 0)
    w['wb2'] = _heads_pad(wb[2], SSD_HEAD_DIM, 0)
    w['conv_w'] = _xbc_pad(full[l]['ssd_conv_w'].astype(F32))
    row = lambda n: small[n][l][None, :]
    for n in ('ffn1_norm', 'mix_norm', 'gm_v_norm', 'mla_q_norm', 'mla_kv_norm', 'ffn2_norm'):
        w[n] = row(n)
    w['gm_w_s'] = small['gm_w_s'][l]
    w['gm_b_full'] = jnp.broadcast_to(small['gm_b_s'][l][:, :, None], (GM_GROUPS, CHUNK, LANES))
    w['gq'] = _lane_place(row('mla_q_gain'), 0)
    w['gk'] = _lane_place(row('mla_k_gain'), 0)
    w['conv_b'] = _xbc_pad(row('ssd_conv_b'))
    w['dt_bias'] = _lane_place(row('ssd_dt_bias'), 0)
    w['a_log'] = _lane_place(row('ssd_a_log'), 0)
    w['d_vec'] = jnp.repeat(small['ssd_d'][l], LANES)[None, :]
    w['ssd_norm'] = _heads_pad(row('ssd_norm'), SSD_HEAD_DIM, 1)
    return w


def _ffn_fwd(x, norm, w_in, w_out, tag):
    h = _rmsnorm_fwd(x, norm, f"{tag}_norm")
    a = _matmul(h, w_in, out_dtype=BF16, name=f"{tag}_in")
    act = _swiglu_fwd(a, f"{tag}_act")
    y = _matmul(act, w_out, scale=0.5, res=x, name=f"{tag}_out")
    return y, (x, h, a, act)


def _ffn_bwd(dy, saved, norm, w_in, w_out, tag):
    x, h, a, act = saved
    dw_out = _matmul(act, dy, ta=True, scale=0.5, name=f"{tag}_dwout")
    dact = _matmul(dy, w_out, tb=True, scale=0.5, out_dtype=BF16, name=f"{tag}_dact")
    da = _swiglu_bwd(a, dact, f"{tag}_dswiglu")
    dw_in = _matmul(h, da, ta=True, name=f"{tag}_dwin")
    dh = _matmul(da, w_in, tb=True, name=f"{tag}_dh")
    dx, dnorm = _rmsnorm_bwd(x, norm, dh, dy, f"{tag}_dnorm")
    return dx, dnorm, dw_in, dw_out


def _mixer_fwd(x, w, tabs, tag):
    h = _rmsnorm_fwd(x, w['mix_norm'], f"{tag}_norm")
    p = _matmul(h, w['w_in'], name=f"{tag}_proj")
    ya = _gmlp_fwd(p, w['gm_v_norm'], w['gm_w_s'], w['gm_b_full'], f"{tag}_gmlp")
    q, k, v = _mla_pre_fwd(p, tabs, w['mla_q_norm'], w['mla_kv_norm'], w['wuq'], w['wkv'], w['gq'], w['gk'], f"{tag}_mla_pre")
    o, lse = _attn_fwd(q, k, v, f"{tag}_attn")
    xbc = _conv_fwd(p, w['conv_w'], w['conv_b'], f"{tag}_conv")
    dtb, dab = _dt_fwd(p, w['dt_bias'], w['a_log'], f"{tag}_dt")
    ys, s_in = _scan_fwd(xbc, dtb, dab, f"{tag}_scan")
    yc = _ssd_post_fwd(ys, xbc, p, w['d_vec'], w['ssd_norm'], f"{tag}_ssd_post")
    t0 = _matmul(ya, w['wb0'], name=f"{tag}_br0")
    t1 = _matmul(o, w['wb1'], name=f"{tag}_br1")
    t2 = _matmul(yc, w['wb2'], name=f"{tag}_br2")
    mg = _merge_fwd(p, t0, t1, t2, f"{tag}_merge")
    y = _matmul(mg, w['w_out'], res=x, name=f"{tag}_wout")
    return y, (x, h, p, ya, q, k, v, o, lse, xbc, dtb, dab, ys, s_in, yc, t0, t1, t2, mg)


def _mixer_bwd(dy, saved, w, tabs, tag):
    x, h, p, ya, q, k, v, o, lse, xbc, dtb, dab, ys, s_in, yc, t0, t1, t2, mg = saved
    g = {}
    g['w_out'] = _matmul(mg, dy, ta=True, name=f"{tag}_dwout")
    dm = _matmul(dy, w['w_out'], tb=True, name=f"{tag}_dmerge")
    d0, d1, d2, dgates = _merge_bwd(p, t0, t1, t2, dm, f"{tag}_dmerge_ew")
    dwb0 = _matmul(ya, d0, ta=True, name=f"{tag}_dwb0")
    dwb1 = _matmul(o, d1, ta=True, name=f"{tag}_dwb1")
    dwb2 = _matmul(yc, d2, ta=True, name=f"{tag}_dwb2")
    g['w_branch'] = jnp.stack([dwb0, _heads_unpad(dwb1, MLA_V, 0), _heads_unpad(dwb2, SSD_HEAD_DIM, 0)])
    dya = _matmul(d0, w['wb0'], tb=True, name=f"{tag}_dya")
    do = _matmul(d1, w['wb1'], tb=True, name=f"{tag}_do")
    dyc = _matmul(d2, w['wb2'], tb=True, name=f"{tag}_dyc")
    duv, g['gm_v_norm'], g['gm_w_s'], db = _gmlp_bwd(p, w['gm_v_norm'], w['gm_w_s'], w['gm_b_full'], dya, f"{tag}_dgmlp")
    g['gm_b_s'] = db.T
    dq = _attn_bwd_dq(q, k, v, o, lse, do, f"{tag}_dattn_q")
    dk, dv = _attn_bwd_dkv(q, k, v, o, lse, do, f"{tag}_dattn_kv")
    dcq, dckv, dkr, dwuq, dwkv, g['mla_q_norm'], g['mla_kv_norm'], dgq, dgk = _mla_pre_bwd(
        p, tabs, w['mla_q_norm'], w['mla_kv_norm'], w['wuq'], w['wkv'], w['gq'], w['gk'], dq, dk, dv, f"{tag}_dmla_pre")
    g['mla_w_uq'] = _heads_unpad(dwuq, MLA_QK, 1)
    dwk = dwkv[:, :HP].reshape(MLA_KV_RANK, MLA_HEADS, LANES)[:, :, :MLA_NOPE]
    dwv = dwkv[:, HP:].reshape(MLA_KV_RANK, MLA_HEADS, LANES)[:, :, :MLA_V]
    g['mla_w_ukv'] = jnp.concatenate([dwk, dwv], axis=2).reshape(MLA_KV_RANK, MLA_HEADS * (MLA_NOPE + MLA_V))
    g['mla_q_gain'], g['mla_k_gain'] = dgq[:, :MLA_QK], dgk[:, :MLA_QK]
    dys, dz, dssd_norm, dd = _ssd_post_bwd(ys, xbc, p, w['d_vec'], w['ssd_norm'], dyc, f"{tag}_dssd_post")
    g['ssd_norm'] = _heads_unpad(dssd_norm, SSD_HEAD_DIM, 1)
    g['ssd_d'] = jnp.sum(dd.reshape(SSD_HEADS, LANES), axis=1)[None, :]
    dxs, dbm, dcm, dda, ddtx = _scan_bwd(xbc, dtb, dab, s_in, dys, w['d_vec'], f"{tag}_dscan")
    dxbc_act = jnp.concatenate([dxs, dbm, dcm], axis=1)
    dxbc, dconv_w, dconv_b = _conv_bwd(p, w['conv_w'], w['conv_b'], dxbc_act, f"{tag}_dconv")
    g['ssd_conv_w'] = _xbc_unpad(dconv_w)
    g['ssd_conv_b'] = _xbc_unpad(dconv_b)
    ddt, dbias, dalog = _dt_bwd(p, w['dt_bias'], w['a_log'], dda, ddtx, f"{tag}_ddt")
    g['ssd_dt_bias'], g['ssd_a_log'] = dbias[:, :SSD_HEADS], dalog[:, :SSD_HEADS]
    s = x.shape[0]
    dp = jnp.concatenate([duv, dz, dxbc[:, :HP], dgates, dxbc[:, HP:], dckv, dcq, dkr, ddt,
                          jnp.zeros((s, PW - C_DT - LANES), BF16)], axis=1)
    g['w_in'] = _w_in_unpad(_matmul(h, dp, ta=True, name=f"{tag}_dwin"))
    dh = _matmul(dp, w['w_in'], tb=True, name=f"{tag}_dh")
    dx, g['mix_norm'] = _rmsnorm_bwd(x, w['mix_norm'], dh, dy, f"{tag}_dnorm")
    return dx, g


def _local_step(x, positions, target, full, small):
    tabs = _rope_tables(positions)
    ws = [_layer_weights(full, small, l) for l in range(DEPTH)]
    saved = []
    for l, w in enumerate(ws):
        x, s1 = _ffn_fwd(x, w['ffn1_norm'], w['ffn1_w_in'], w['ffn1_w_out'], "ffn1")
        x, s2 = _mixer_fwd(x, w, tabs, "mix")
        x, s3 = _ffn_fwd(x, w['ffn2_norm'], w['ffn2_w_in'], w['ffn2_w_out'], "ffn2")
        saved.append((s1, s2, s3))
    dy, sq = _loss_head(x, target, "loss_head")
    loss = 0.5 * jnp.sum(sq) / D_MODEL
    grads = [None] * DEPTH
    for l in reversed(range(DEPTH)):
        w = ws[l]
        s1, s2, s3 = saved[l]
        dy, dn2, dwi2, dwo2 = _ffn_bwd(dy, s3, w['ffn2_norm'], w['ffn2_w_in'], w['ffn2_w_out'], "ffn2")
        dy, g = _mixer_bwd(dy, s2, w, tabs, "mix")
        dy, dn1, dwi1, dwo1 = _ffn_bwd(dy, s1, w['ffn1_norm'], w['ffn1_w_in'], w['ffn1_w_out'], "ffn1")
        g.update(ffn1_norm=dn1, ffn1_w_in=dwi1, ffn1_w_out=dwo1, ffn2_norm=dn2, ffn2_w_in=dwi2, ffn2_w_out=dwo2)
        grads[l] = g
    return loss, dy, grads


_SMALL_ROWS = None


def _pack_small(per_layer_rows):
    flat = jnp.concatenate([per_layer_rows[l][n].reshape(-1).astype(F32) for l in range(DEPTH) for n in SMALL_ORDER])
    rows = -(-flat.shape[0] // LANES)
    rows = -(-rows // 8) * 8
    return jnp.pad(flat, (0, rows * LANES - flat.shape[0])).reshape(rows, LANES)


def _unpack_small(buf, shapes):
    flat = buf.reshape(-1)
    off = 0
    out = {n: [] for n in SMALL_ORDER}
    for l in range(DEPTH):
        for n in SMALL_ORDER:
            size = int(np.prod(shapes[n]))
            out[n].append(flat[off:off + size].reshape(shapes[n]))
            off += size
    return {n: jnp.stack(v) for n, v in out.items()}


def kernel(x, positions, ffn1_norm, ffn1_w_in, ffn1_w_out, mix_norm, w_in, gm_v_norm, gm_w_s, gm_b_s, mla_q_norm, mla_kv_norm, mla_w_uq, mla_w_ukv, mla_q_gain, mla_k_gain, ssd_conv_w, ssd_conv_b, ssd_dt_bias, ssd_a_log, ssd_d, ssd_norm, w_branch, w_out, ffn2_norm, ffn2_w_in, ffn2_w_out, loss_target, m_ffn1_norm, m_ffn1_w_in, m_ffn1_w_out, m_mix_norm, m_w_in, m_gm_v_norm, m_gm_w_s, m_gm_b_s, m_mla_q_norm, m_mla_kv_norm, m_mla_w_uq, m_mla_w_ukv, m_mla_q_gain, m_mla_k_gain, m_ssd_conv_w, m_ssd_conv_b, m_ssd_dt_bias, m_ssd_a_log, m_ssd_d, m_ssd_norm, m_w_branch, m_w_out, m_ffn2_norm, m_ffn2_w_in, m_ffn2_w_out, v_ffn1_norm, v_ffn1_w_in, v_ffn1_w_out, v_mix_norm, v_w_in, v_gm_v_norm, v_gm_w_s, v_gm_b_s, v_mla_q_norm, v_mla_kv_norm, v_mla_w_uq, v_mla_w_ukv, v_mla_q_gain, v_mla_k_gain, v_ssd_conv_w, v_ssd_conv_b, v_ssd_dt_bias, v_ssd_a_log, v_ssd_d, v_ssd_norm, v_w_branch, v_w_out, v_ffn2_norm, v_ffn2_w_in, v_ffn2_w_out):
    wts = dict(zip(WEIGHTS, (ffn1_norm, ffn1_w_in, ffn1_w_out, mix_norm, w_in, gm_v_norm, gm_w_s, gm_b_s, mla_q_norm, mla_kv_norm,
                             mla_w_uq, mla_w_ukv, mla_q_gain, mla_k_gain, ssd_conv_w, ssd_conv_b, ssd_dt_bias, ssd_a_log, ssd_d,
                             ssd_norm, w_branch, w_out, ffn2_norm, ffn2_w_in, ffn2_w_out)))
    mom = dict(zip(WEIGHTS, (m_ffn1_norm, m_ffn1_w_in, m_ffn1_w_out, m_mix_norm, m_w_in, m_gm_v_norm, m_gm_w_s, m_gm_b_s, m_mla_q_norm,
                             m_mla_kv_norm, m_mla_w_uq, m_mla_w_ukv, m_mla_q_gain, m_mla_k_gain, m_ssd_conv_w, m_ssd_conv_b,
                             m_ssd_dt_bias, m_ssd_a_log, m_ssd_d, m_ssd_norm, m_w_branch, m_w_out, m_ffn2_norm, m_ffn2_w_in,
                             m_ffn2_w_out)))
    var = dict(zip(WEIGHTS, (v_ffn1_norm, v_ffn1_w_in, v_ffn1_w_out, v_mix_norm, v_w_in, v_gm_v_norm, v_gm_w_s, v_gm_b_s, v_mla_q_norm,
                             v_mla_kv_norm, v_mla_w_uq, v_mla_w_ukv, v_mla_q_gain, v_mla_k_gain, v_ssd_conv_w, v_ssd_conv_b,
                             v_ssd_dt_bias, v_ssd_a_log, v_ssd_d, v_ssd_norm, v_w_branch, v_w_out, v_ffn2_norm, v_ffn2_w_in,
                             v_ffn2_w_out)))
    cx, cy, cc = _me()
    idx = jnp.stack([cc, _chip_index(cx, cy)]).astype(jnp.int32)

    mine = _pack_shards([{n: wts[n][l] for n in SHARDED_ORDER} for l in range(DEPTH)], BF16)
    gathered = _unpack_shards(_gather_shards(mine))
    full = [{n: _join(n, gathered[l][n]) for n in SHARDED_ORDER} for l in range(DEPTH)]
    small = {n: wts[n] for n in SMALL_ORDER}

    loss_part, dx, grads = _local_step(x[0], positions[0], loss_target[0], full, small)
    loss = lax.psum(loss_part, ("x", "y", "c"))

    parts = [{n: _split(n, grads[l][n]) for n in SHARDED_ORDER} for l in range(DEPTH)]
    gbuf = jnp.stack([_pack_shards([{n: parts[l][n][k] for n in SHARDED_ORDER} for l in range(DEPTH)], F32)
                      for k in range(N_CHIPS)])
    chip16, own32 = _pair_add(gbuf, _pair_exchange(gbuf), idx)
    reduced_half = _chip_add(own32, _chip_exchange(chip16), idx)
    reduced = _unpack_shards(_pair_share(reduced_half))
    shapes = {n: wts[n].shape[1:] for n in SMALL_ORDER}
    small_g = _unpack_small(_sum_slots(_all_exchange(_pack_small(grads)), "small_sum"), shapes)

    grad, delta, new_m, new_v = {}, {}, {}, {}
    for n in SHARDED_ORDER:
        grad[n] = jnp.stack([reduced[l][n] for l in range(DEPTH)])
        shp = grad[n].shape
        two = (int(np.prod(shp[:-1])), shp[-1])
        d, nm, nv = _adamw(wts[n].reshape(two), grad[n].reshape(two), mom[n].reshape(two), var[n].reshape(two), f"adamw_{n}")
        delta[n], new_m[n], new_v[n] = d.reshape(shp), nm.reshape(shp), nv.reshape(shp)
    per_layer = lambda t: [{n: t[n][l] for n in SMALL_ORDER} for l in range(DEPTH)]
    d, nm, nv = _adamw(_pack_small(per_layer(wts)), _pack_small(per_layer(small_g)), _pack_small(per_layer(mom)),
                       _pack_small(per_layer(var)), "adamw_small")
    sd, snm, snv = _unpack_small(d, shapes), _unpack_small(nm, shapes), _unpack_small(nv, shapes)
    for n in SMALL_ORDER:
        grad[n], delta[n], new_m[n], new_v[n] = small_g[n], sd[n], snm[n], snv[n]
    return (loss, dx[None], *[grad[n] for n in WEIGHTS], *[delta[n] for n in WEIGHTS], *[new_m[n] for n in WEIGHTS],
            *[new_v[n] for n in WEIGHTS])
```

```python
import functools
import math

import numpy as np
import jax
import jax.numpy as jnp
from jax import lax
from jax.experimental import pallas as pl
from jax.experimental.pallas import tpu as pltpu

F32, BF16 = jnp.float32, jnp.bfloat16
MESH = pl.DeviceIdType.MESH

D_MODEL, DEPTH, D_FF, EPS = 1024, 4, 2816, 1e-6
GM_WIDTH, GM_GROUPS, CHUNK = 512, 4, 128
MLA_HEADS, MLA_Q_RANK, MLA_KV_RANK, MLA_NOPE, MLA_ROPE, MLA_V = 8, 384, 256, 64, 32, 64
MLA_QK = MLA_NOPE + MLA_ROPE
ROPE_THETA = 10000.0
SSD_HEADS, SSD_HEAD_DIM, SSD_GROUPS, SSD_STATE, SSD_CONV = 8, 64, 2, 128, 4
SSD_INNER = SSD_HEADS * SSD_HEAD_DIM
IN_COLS = 6312
LANES = 128
ADAM_LR, ADAM_B1, ADAM_B2, ADAM_EPS, ADAM_WD, ADAM_STEP = 0.001, 0.9, 0.999, 1e-08, 0.01, 10

C_UV, C_Z, C_XS, C_G, PW_MAIN = 0, 1024, 2048, 3072, 6144
T_BC, T_CKV, T_CQ, T_KR, T_DT, PW_TAIL = 0, 512, 768, 1152, 1280, 1536
HP = MLA_HEADS * LANES
FC = 2 * D_FF // 4

WEIGHTS = ['ffn1_norm', 'ffn1_w_in', 'ffn1_w_out', 'mix_norm', 'w_in', 'gm_v_norm', 'gm_w_s', 'gm_b_s', 'mla_q_norm',
           'mla_kv_norm', 'mla_w_uq', 'mla_w_ukv', 'mla_q_gain', 'mla_k_gain', 'ssd_conv_w', 'ssd_conv_b', 'ssd_dt_bias',
           'ssd_a_log', 'ssd_d', 'ssd_norm', 'w_branch', 'w_out', 'ffn2_norm', 'ffn2_w_in', 'ffn2_w_out']
SHARDED = {'ffn1_w_in': ((1024, 5632), 1), 'ffn1_w_out': ((2816, 1024), 0), 'w_in': ((1024, 6312), 1),
           'mla_w_uq': ((384, 768), 1), 'mla_w_ukv': ((256, 1024), 1), 'ssd_conv_w': ((4, 1024), 1),
           'w_branch': ((3, 512, 1024), 2), 'w_out': ((1024, 1024), 0), 'ffn2_w_in': ((1024, 5632), 1),
           'ffn2_w_out': ((2816, 1024), 0)}
SHARDED_ORDER = [n for n in WEIGHTS if n in SHARDED]
SMALL_ORDER = [n for n in WEIGHTS if n not in SHARDED]
REDUCED = [n for n in SHARDED_ORDER if n != 'ssd_conv_w']
N_CHIPS = 4
HALF_L = DEPTH // 2


def _shard_shape(name):
    shape, ax = SHARDED[name]
    return tuple(d // N_CHIPS if i == ax else d for i, d in enumerate(shape))


def _pick(dim, target):
    if dim <= target:
        return dim
    t = (target // LANES) * LANES
    while t >= LANES:
        if dim % t == 0:
            return t
        t -= LANES
    return dim


def _sigmoid(x):
    return 1.0 / (1.0 + jnp.exp(-x))


def _params(*sem):
    return pltpu.CompilerParams(dimension_semantics=sem, vmem_limit_bytes=56 * 1024 * 1024)


def _matmul(a, b, *, ta=False, tb=False, out_dtype=F32, scale=1.0, res=None, name):
    if ta:
        k_dim, m_dim = a.shape
    else:
        m_dim, k_dim = a.shape
    if tb:
        n_dim, k2 = b.shape
    else:
        k2, n_dim = b.shape
    assert k_dim == k2, (a.shape, b.shape, ta, tb)
    tm, tn, tk = _pick(m_dim, 1024), _pick(n_dim, 1024), _pick(k_dim, 1024)
    nk = k_dim // tk
    dn = (((0 if ta else 1,), (1 if tb else 0,)), ((), ()))

    def body(*refs):
        if res is not None:
            a_ref, b_ref, r_ref, o_ref, acc = refs
        else:
            a_ref, b_ref, o_ref, acc = refs
        k = pl.program_id(2)

        @pl.when(k == 0)
        def _():
            acc[...] = jnp.zeros_like(acc)

        acc[...] += lax.dot_general(a_ref[...].astype(BF16), b_ref[...].astype(BF16), dn, preferred_element_type=F32)

        @pl.when(k == nk - 1)
        def _():
            r = acc[...]
            if scale != 1.0:
                r = r * scale
            if res is not None:
                r = r + r_ref[...]
            o_ref[...] = r.astype(out_dtype)

    a_spec = pl.BlockSpec((tk, tm), lambda j, i, k: (k, i)) if ta else pl.BlockSpec((tm, tk), lambda j, i, k: (i, k))
    b_spec = pl.BlockSpec((tn, tk), lambda j, i, k: (j, k)) if tb else pl.BlockSpec((tk, tn), lambda j, i, k: (k, j))
    in_specs = [a_spec, b_spec]
    args = [a, b]
    if res is not None:
        in_specs.append(pl.BlockSpec((tm, tn), lambda j, i, k: (i, j)))
        args.append(res)
    return pl.pallas_call(
        body, name=name, grid=(n_dim // tn, m_dim // tm, nk), in_specs=in_specs,
        out_specs=pl.BlockSpec((tm, tn), lambda j, i, k: (i, j)),
        out_shape=jax.ShapeDtypeStruct((m_dim, n_dim), out_dtype),
        scratch_shapes=[pltpu.VMEM((tm, tn), F32)],
        compiler_params=_params("parallel", "parallel", "arbitrary"))(*args)


def _rmsnorm_fwd(x, gain, name):
    s, d = x.shape
    tm = _pick(s, 512)

    def body(x_ref, g_ref, o_ref):
        xv = x_ref[...]
        r = lax.rsqrt(jnp.mean(xv * xv, axis=-1, keepdims=True) + EPS)
        o_ref[...] = (xv * r * g_ref[...]).astype(BF16)

    return pl.pallas_call(
        body, name=name, grid=(s // tm,),
        in_specs=[pl.BlockSpec((tm, d), lambda i: (i, 0)), pl.BlockSpec((1, d), lambda i: (0, 0))],
        out_specs=pl.BlockSpec((tm, d), lambda i: (i, 0)),
        out_shape=jax.ShapeDtypeStruct((s, d), BF16), compiler_params=_params("parallel"))(x, gain)


def _rmsnorm_bwd(x, gain, dh, dres, name):
    s, d = x.shape
    tm = _pick(s, 512)

    def body(x_ref, g_ref, dh_ref, dr_ref, dx_ref, dg_ref):
        @pl.when(pl.program_id(0) == 0)
        def _():
            dg_ref[...] = jnp.zeros_like(dg_ref)

        xv, dhv = x_ref[...], dh_ref[...]
        r = lax.rsqrt(jnp.mean(xv * xv, axis=-1, keepdims=True) + EPS)
        u = dhv * g_ref[...]
        dx_ref[...] = dr_ref[...] + r * u - xv * (r * r * r) * jnp.mean(xv * u, axis=-1, keepdims=True)
        dg_ref[...] += jnp.sum(dhv * xv * r, axis=0, keepdims=True)

    row = pl.BlockSpec((tm, d), lambda i: (i, 0))
    vec = pl.BlockSpec((1, d), lambda i: (0, 0))
    return pl.pallas_call(
        body, name=name, grid=(s // tm,), in_specs=[row, vec, row, row], out_specs=[row, vec],
        out_shape=[jax.ShapeDtypeStruct((s, d), F32), jax.ShapeDtypeStruct((1, d), F32)],
        compiler_params=_params("arbitrary"))(x, gain, dh, dres)


_NT = (((1,), (1,)), ((), ()))
_TN = (((0,), (0,)), ((), ()))


def _resident(shape):
    return pl.BlockSpec(shape, lambda *_: tuple(0 for _ in shape), pipeline_mode=pl.Buffered(1))


def _ffn_in(x, gain, w4, name):
    s, d = x.shape
    tm = _pick(s, 512)

    def body(x_ref, g_ref, w_ref, h_ref, gate_ref, up_ref, act_ref):
        xv = x_ref[...]
        r = lax.rsqrt(jnp.mean(xv * xv, axis=-1, keepdims=True) + EPS)
        h = (xv * r * g_ref[...]).astype(BF16)
        h_ref[...] = h
        for j in range(2):
            g16 = jnp.dot(h, w_ref[j], preferred_element_type=F32).astype(BF16)
            u16 = jnp.dot(h, w_ref[j + 2], preferred_element_type=F32).astype(BF16)
            gate_ref[j] = g16
            up_ref[j] = u16
            gf, uf = g16.astype(F32), u16.astype(F32)
            act_ref[j] = (gf * _sigmoid(gf) * uf).astype(BF16)

    half = pl.BlockSpec((2, tm, FC), lambda i: (0, i, 0))
    return pl.pallas_call(
        body, name=name, grid=(s // tm,),
        in_specs=[pl.BlockSpec((tm, d), lambda i: (i, 0)), pl.BlockSpec((1, d), lambda i: (0, 0)), _resident((4, d, FC))],
        out_specs=[pl.BlockSpec((tm, d), lambda i: (i, 0)), half, half, half],
        out_shape=[jax.ShapeDtypeStruct((s, d), BF16)] + [jax.ShapeDtypeStruct((2, s, FC), BF16)] * 3,
        compiler_params=_params("parallel"))(x, gain, w4)


def _ffn_out(act, w_out, x, name):
    s, d = x.shape
    tm = _pick(s, 512)

    def body(a_ref, w_ref, x_ref, o_ref):
        acc = jnp.dot(a_ref[0], w_ref[0:FC, :], preferred_element_type=F32)
        acc = acc + jnp.dot(a_ref[1], w_ref[FC:2 * FC, :], preferred_element_type=F32)
        o_ref[...] = x_ref[...] + 0.5 * acc

    row = pl.BlockSpec((tm, d), lambda i: (i, 0))
    return pl.pallas_call(
        body, name=name, grid=(s // tm,),
        in_specs=[pl.BlockSpec((2, tm, FC), lambda i: (0, i, 0)), _resident((2 * FC, d)), row], out_specs=row,
        out_shape=jax.ShapeDtypeStruct((s, d), F32), compiler_params=_params("parallel"))(act, w_out, x)


def _ffn_dact(dy, w_out, gate, up, name):
    s, d = dy.shape
    tm = _pick(s, 512)

    def body(dy_ref, w_ref, g_ref, u_ref, o_ref):
        dy16 = dy_ref[...].astype(BF16)
        for j in range(2):
            dact = 0.5 * lax.dot_general(dy16, w_ref[j * FC:(j + 1) * FC, :], _NT, preferred_element_type=F32)
            g, u = g_ref[j].astype(F32), u_ref[j].astype(F32)
            sg = _sigmoid(g)
            o_ref[j] = (dact * u * (sg * (1.0 + g * (1.0 - sg)))).astype(BF16)
            o_ref[j + 2] = (dact * g * sg).astype(BF16)

    half = pl.BlockSpec((2, tm, FC), lambda i: (0, i, 0))
    return pl.pallas_call(
        body, name=name, grid=(s // tm,),
        in_specs=[pl.BlockSpec((tm, d), lambda i: (i, 0)), _resident((2 * FC, d)), half, half],
        out_specs=pl.BlockSpec((4, tm, FC), lambda i: (0, i, 0)),
        out_shape=jax.ShapeDtypeStruct((4, s, FC), BF16), compiler_params=_params("parallel"))(dy, w_out, gate, up)


def _ffn_dwout(act, dy, name):
    s, d = dy.shape
    tk = _pick(s, 1024)
    nk = s // tk

    def body(a_ref, dy_ref, o_ref):
        k = pl.program_id(1)

        @pl.when(k == 0)
        def _():
            o_ref[...] = jnp.zeros_like(o_ref)

        o_ref[...] += lax.dot_general(a_ref[...], dy_ref[...].astype(BF16), _TN, preferred_element_type=F32)

        @pl.when(k == nk - 1)
        def _():
            o_ref[...] = 0.5 * o_ref[...]

    return pl.pallas_call(
        body, name=name, grid=(2, nk),
        in_specs=[pl.BlockSpec((None, tk, FC), lambda j, k: (j, k, 0)), pl.BlockSpec((tk, d), lambda j, k: (k, 0))],
        out_specs=pl.BlockSpec((FC, d), lambda j, k: (j, 0)), out_shape=jax.ShapeDtypeStruct((2 * FC, d), F32),
        compiler_params=_params("parallel", "arbitrary"))(act, dy)


def _ffn_dwin(h, da, name):
    s, d = h.shape
    tk = _pick(s, 1024)

    def body(h_ref, da_ref, o_ref):
        @pl.when(pl.program_id(1) == 0)
        def _():
            o_ref[...] = jnp.zeros_like(o_ref)

        o_ref[...] += lax.dot_general(h_ref[...], da_ref[...], _TN, preferred_element_type=F32)

    return pl.pallas_call(
        body, name=name, grid=(4, s // tk),
        in_specs=[pl.BlockSpec((tk, d), lambda j, k: (k, 0)), pl.BlockSpec((None, tk, FC), lambda j, k: (j, k, 0))],
        out_specs=pl.BlockSpec((None, d, FC), lambda j, k: (j, 0, 0)), out_shape=jax.ShapeDtypeStruct((4, d, FC), F32),
        compiler_params=_params("parallel", "arbitrary"))(h, da)


def _ffn_dx(da, w4, x, gain, dy, name):
    s, d = x.shape
    tm = _pick(s, 512)

    def body(da_ref, w_ref, x_ref, g_ref, dy_ref, dx_ref, dg_ref):
        @pl.when(pl.program_id(0) == 0)
        def _():
            dg_ref[...] = jnp.zeros_like(dg_ref)

        dh = jnp.zeros((tm, d), F32)
        for j in range(4):
            dh = dh + lax.dot_general(da_ref[j], w_ref[j], _NT, preferred_element_type=F32)
        xv = x_ref[...]
        r = lax.rsqrt(jnp.mean(xv * xv, axis=-1, keepdims=True) + EPS)
        u = dh * g_ref[...]
        dx_ref[...] = dy_ref[...] + r * u - xv * (r * r * r) * jnp.mean(xv * u, axis=-1, keepdims=True)
        dg_ref[...] += jnp.sum(dh * xv * r, axis=0, keepdims=True)

    row = pl.BlockSpec((tm, d), lambda i: (i, 0))
    vec = pl.BlockSpec((1, d), lambda i: (0, 0))
    return pl.pallas_call(
        body, name=name, grid=(s // tm,),
        in_specs=[pl.BlockSpec((4, tm, FC), lambda i: (0, i, 0)), _resident((4, d, FC)), row, vec, row],
        out_specs=[row, vec], out_shape=[jax.ShapeDtypeStruct((s, d), F32), jax.ShapeDtypeStruct((1, d), F32)],
        compiler_params=_params("arbitrary"))(da, w4, x, gain, dy)


_INV_SQRT2 = 0.7071067811865476
_INV_SQRT2PI = 0.3989422804014327


def _gelu(x):
    return 0.5 * x * (1.0 + lax.erf(x * _INV_SQRT2))


def _gelu_grad(x):
    return 0.5 * (1.0 + lax.erf(x * _INV_SQRT2)) + x * jnp.exp(-0.5 * x * x) * _INV_SQRT2PI


def _tril_mask():
    r = lax.broadcasted_iota(jnp.int32, (CHUNK, CHUNK), 0)
    c = lax.broadcasted_iota(jnp.int32, (CHUNK, CHUNK), 1)
    return r >= c


def _gmlp_fwd(p, v_gain, w_s, b_full, name):
    s = p.shape[0]
    tm = _pick(s, 512)
    nch = tm // CHUNK

    def body(uv_ref, g_ref, w_ref, b_ref, o_ref):
        gel = _gelu(uv_ref[...].astype(F32))
        u, v = gel[:, :GM_WIDTH], gel[:, GM_WIDTH:]
        r = lax.rsqrt(jnp.mean(v * v, axis=-1, keepdims=True) + EPS)
        vn = (v * r * g_ref[...]).astype(BF16)
        mask = _tril_mask()
        for g in range(GM_GROUPS):
            wm = jnp.where(mask, w_ref[g], 0.0).astype(BF16)
            for c in range(nch):
                rs, cs = slice(c * CHUNK, (c + 1) * CHUNK), slice(g * LANES, (g + 1) * LANES)
                sp = jnp.dot(wm, vn[rs, cs], preferred_element_type=F32) + b_ref[g]
                o_ref[rs, cs] = (u[rs, cs] * sp).astype(BF16)

    full3 = pl.BlockSpec((GM_GROUPS, CHUNK, CHUNK), lambda i: (0, 0, 0))
    return pl.pallas_call(
        body, name=name, grid=(s // tm,),
        in_specs=[pl.BlockSpec((tm, 2 * GM_WIDTH), lambda i: (i, C_UV // (2 * GM_WIDTH))),
                  pl.BlockSpec((1, GM_WIDTH), lambda i: (0, 0)), full3, full3],
        out_specs=pl.BlockSpec((tm, GM_WIDTH), lambda i: (i, 0)),
        out_shape=jax.ShapeDtypeStruct((s, GM_WIDTH), BF16), compiler_params=_params("parallel"))(p, v_gain, w_s, b_full)


def _gmlp_bwd(p, v_gain, w_s, b_full, dy, name):
    s = p.shape[0]
    tm = _pick(s, 512)
    nch = tm // CHUNK
    nsteps = s // tm

    def body(uv_ref, g_ref, w_ref, b_ref, dy_ref, duv_ref, dg_ref, dw_ref, db_ref, dvn_s, dbacc):
        step = pl.program_id(0)

        @pl.when(step == 0)
        def _():
            dg_ref[...] = jnp.zeros_like(dg_ref)
            dw_ref[...] = jnp.zeros_like(dw_ref)
            dbacc[...] = jnp.zeros_like(dbacc)

        uv = uv_ref[...].astype(F32)
        gel = _gelu(uv)
        u, v = gel[:, :GM_WIDTH], gel[:, GM_WIDTH:]
        r = lax.rsqrt(jnp.mean(v * v, axis=-1, keepdims=True) + EPS)
        gain = g_ref[...]
        vn32 = v * r * gain
        vn = vn32.astype(BF16)
        dy = dy_ref[...]
        mask = _tril_mask()
        for g in range(GM_GROUPS):
            wm = jnp.where(mask, w_ref[g], 0.0).astype(BF16)
            dwg = jnp.zeros((CHUNK, CHUNK), F32)
            dbg = jnp.zeros((CHUNK, LANES), F32)
            for c in range(nch):
                rs, cs = slice(c * CHUNK, (c + 1) * CHUNK), slice(g * LANES, (g + 1) * LANES)
                sp = jnp.dot(wm, vn[rs, cs], preferred_element_type=F32) + b_ref[g]
                dyc = dy[rs, cs]
                dsp = dyc * u[rs, cs]
                dsp16 = dsp.astype(BF16)
                duv_ref[rs, cs] = (dyc * sp * _gelu_grad(uv[rs, cs])).astype(BF16)
                dvn_s[rs, cs] = lax.dot_general(wm, dsp16, (((0,), (0,)), ((), ())), preferred_element_type=F32)
                dwg = dwg + lax.dot_general(dsp16, vn[rs, cs], (((1,), (1,)), ((), ())), preferred_element_type=F32)
                dbg = dbg + dsp
            dw_ref[g] += jnp.where(mask, dwg, 0.0)
            dbacc[:, g * LANES:(g + 1) * LANES] += dbg
        dvn = dvn_s[...]
        uu = dvn * gain
        dv = r * uu - v * (r * r * r) * jnp.mean(v * uu, axis=-1, keepdims=True)
        duv_ref[:, GM_WIDTH:] = (dv * _gelu_grad(uv[:, GM_WIDTH:])).astype(BF16)
        dg_ref[...] += jnp.sum(dvn * v * r, axis=0, keepdims=True)

        @pl.when(step == nsteps - 1)
        def _():
            for g in range(GM_GROUPS):
                db_ref[:, g:g + 1] = jnp.sum(dbacc[:, g * LANES:(g + 1) * LANES], axis=1, keepdims=True)

    full3 = pl.BlockSpec((GM_GROUPS, CHUNK, CHUNK), lambda i: (0, 0, 0))
    return pl.pallas_call(
        body, name=name, grid=(nsteps,),
        in_specs=[pl.BlockSpec((tm, 2 * GM_WIDTH), lambda i: (i, C_UV // (2 * GM_WIDTH))),
                  pl.BlockSpec((1, GM_WIDTH), lambda i: (0, 0)), full3, full3,
                  pl.BlockSpec((tm, GM_WIDTH), lambda i: (i, 0))],
        out_specs=[pl.BlockSpec((tm, 2 * GM_WIDTH), lambda i: (i, 0)), pl.BlockSpec((1, GM_WIDTH), lambda i: (0, 0)),
                   full3, pl.BlockSpec((CHUNK, GM_GROUPS), lambda i: (0, 0))],
        out_shape=[jax.ShapeDtypeStruct((s, 2 * GM_WIDTH), BF16), jax.ShapeDtypeStruct((1, GM_WIDTH), F32),
                   jax.ShapeDtypeStruct((GM_GROUPS, CHUNK, CHUNK), F32), jax.ShapeDtypeStruct((CHUNK, GM_GROUPS), F32)],
        scratch_shapes=[pltpu.VMEM((tm, GM_WIDTH), F32), pltpu.VMEM((CHUNK, GM_WIDTH), F32)],
        compiler_params=_params("arbitrary"))(p, v_gain, w_s, b_full, dy)


def _rope(x, ct, s1, s2):
    return x * ct + pltpu.roll(x, LANES - MLA_ROPE // 2, 1) * s1 + pltpu.roll(x, MLA_ROPE // 2, 1) * s2


def _rope_bwd(d, ct, s1, s2):
    return d * ct + pltpu.roll(d * s1, MLA_ROPE // 2, 1) + pltpu.roll(d * s2, LANES - MLA_ROPE // 2, 1)


def _head_norm(x, gain):
    r = lax.rsqrt(jnp.sum(x * x, axis=-1, keepdims=True) * (1.0 / MLA_QK) + EPS)
    return x * r * gain, r


def _head_norm_bwd(x, r, gain, d):
    u = d * gain
    return r * u - x * (r * r * r) * (jnp.sum(x * u, axis=-1, keepdims=True) * (1.0 / MLA_QK))


def _mla_specs(tm):
    cq = pl.BlockSpec((tm, MLA_Q_RANK), lambda i: (i, T_CQ // MLA_Q_RANK))
    ckv = pl.BlockSpec((tm, MLA_KV_RANK), lambda i: (i, T_CKV // MLA_KV_RANK))
    kr = pl.BlockSpec((tm, LANES), lambda i: (i, T_KR // LANES))
    tab = pl.BlockSpec((tm, LANES), lambda i: (i, 0))
    return cq, ckv, kr, tab


def _const(shape):
    return pl.BlockSpec(shape, lambda i: tuple(0 for _ in shape))


def _mla_pre_fwd(p, tabs, qn_g, kvn_g, wuq, wkv, gq, gk, name):
    s = p.shape[0]
    tm = _pick(s, 256)
    ct, s1, s2 = tabs

    def body(cq_ref, ckv_ref, kr_ref, ct_ref, s1_ref, s2_ref, qg_ref, kvg_ref, wuq_ref, wkv_ref, gq_ref, gk_ref,
             q_ref, k_ref, v_ref):
        cq, ckv, kr = cq_ref[...], ckv_ref[...], kr_ref[...]
        ctv, s1v, s2v = ct_ref[...], s1_ref[...], s2_ref[...]
        rq = lax.rsqrt(jnp.mean(cq * cq, axis=-1, keepdims=True) + EPS)
        q = jnp.dot((cq * rq * qg_ref[...]).astype(BF16), wuq_ref[...], preferred_element_type=F32)
        rk = lax.rsqrt(jnp.mean(ckv * ckv, axis=-1, keepdims=True) + EPS)
        kv = jnp.dot((ckv * rk * kvg_ref[...]).astype(BF16), wkv_ref[...], preferred_element_type=F32)
        v_ref[...] = kv[:, HP:].astype(BF16)
        for h in range(MLA_HEADS):
            hs = slice(h * LANES, (h + 1) * LANES)
            qh, _ = _head_norm(q[:, hs], gq_ref[...])
            q_ref[:, hs] = _rope(qh, ctv, s1v, s2v).astype(BF16)
            kh, _ = _head_norm(kv[:, hs] + kr, gk_ref[...])
            k_ref[:, hs] = _rope(kh, ctv, s1v, s2v).astype(BF16)

    cq_s, ckv_s, kr_s, tab_s = _mla_specs(tm)
    out = pl.BlockSpec((tm, HP), lambda i: (i, 0))
    return pl.pallas_call(
        body, name=name, grid=(s // tm,),
        in_specs=[cq_s, ckv_s, kr_s, tab_s, tab_s, tab_s, _const((1, MLA_Q_RANK)), _const((1, MLA_KV_RANK)),
                  _const((MLA_Q_RANK, HP)), _const((MLA_KV_RANK, 2 * HP)), _const((1, LANES)), _const((1, LANES))],
        out_specs=[out, out, out], out_shape=[jax.ShapeDtypeStruct((s, HP), BF16)] * 3,
        compiler_params=_params("parallel"))(p, p, p, ct, s1, s2, qn_g, kvn_g, wuq, wkv, gq, gk)


def _mla_pre_bwd(p, tabs, qn_g, kvn_g, wuq, wkv, gq, gk, dq, dk, dv, name):
    s = p.shape[0]
    tm = _pick(s, 256)
    ct, s1, s2 = tabs

    def body(cq_ref, ckv_ref, kr_ref, ct_ref, s1_ref, s2_ref, qg_ref, kvg_ref, wuq_ref, wkv_ref, gq_ref, gk_ref,
             dq_ref, dk_ref, dv_ref, dcq_ref, dckv_ref, dkr_ref, dwuq_ref, dwkv_ref, dqg_ref, dkvg_ref, dgq_ref, dgk_ref,
             dqp, dkvp):
        @pl.when(pl.program_id(0) == 0)
        def _():
            for ref in (dwuq_ref, dwkv_ref, dqg_ref, dkvg_ref, dgq_ref, dgk_ref):
                ref[...] = jnp.zeros_like(ref)

        cq, ckv, kr = cq_ref[...], ckv_ref[...], kr_ref[...]
        ctv, s1v, s2v = ct_ref[...], s1_ref[...], s2_ref[...]
        rq = lax.rsqrt(jnp.mean(cq * cq, axis=-1, keepdims=True) + EPS)
        qn = (cq * rq * qg_ref[...]).astype(BF16)
        q = jnp.dot(qn, wuq_ref[...], preferred_element_type=F32)
        rk = lax.rsqrt(jnp.mean(ckv * ckv, axis=-1, keepdims=True) + EPS)
        kvn = (ckv * rk * kvg_ref[...]).astype(BF16)
        kv = jnp.dot(kvn, wkv_ref[...], preferred_element_type=F32)
        gqv, gkv = gq_ref[...], gk_ref[...]
        dgq = jnp.zeros((1, LANES), F32)
        dgk = jnp.zeros((1, LANES), F32)
        dkr = jnp.zeros((tm, LANES), F32)
        for h in range(MLA_HEADS):
            hs = slice(h * LANES, (h + 1) * LANES)
            xq = q[:, hs]
            _, r = _head_norm(xq, gqv)
            d = _rope_bwd(dq_ref[:, hs], ctv, s1v, s2v)
            dgq = dgq + jnp.sum(d * xq * r, axis=0, keepdims=True)
            dqp[:, hs] = _head_norm_bwd(xq, r, gqv, d)
            xk = kv[:, hs] + kr
            _, r = _head_norm(xk, gkv)
            d = _rope_bwd(dk_ref[:, hs], ctv, s1v, s2v)
            dgk = dgk + jnp.sum(d * xk * r, axis=0, keepdims=True)
            dxk = _head_norm_bwd(xk, r, gkv, d)
            dkvp[:, hs] = dxk
            dkr = dkr + dxk
        dkvp[:, HP:] = dv_ref[...]
        dgq_ref[...] += dgq
        dgk_ref[...] += dgk
        dkr_ref[...] = dkr.astype(BF16)
        tn = (((0,), (0,)), ((), ()))
        nt = (((1,), (1,)), ((), ()))
        dq16 = dqp[...].astype(BF16)
        dwuq_ref[...] += lax.dot_general(qn, dq16, tn, preferred_element_type=F32)
        dqn = lax.dot_general(dq16, wuq_ref[...], nt, preferred_element_type=F32)
        dqg_ref[...] += jnp.sum(dqn * cq * rq, axis=0, keepdims=True)
        u = dqn * qg_ref[...]
        dcq_ref[...] = (rq * u - cq * (rq * rq * rq) * jnp.mean(cq * u, axis=-1, keepdims=True)).astype(BF16)
        dkv16 = dkvp[...].astype(BF16)
        dwkv_ref[...] += lax.dot_general(kvn, dkv16, tn, preferred_element_type=F32)
        dkvn = lax.dot_general(dkv16, wkv_ref[...], nt, preferred_element_type=F32)
        dkvg_ref[...] += jnp.sum(dkvn * ckv * rk, axis=0, keepdims=True)
        u = dkvn * kvg_ref[...]
        dckv_ref[...] = (rk * u - ckv * (rk * rk * rk) * jnp.mean(ckv * u, axis=-1, keepdims=True)).astype(BF16)

    cq_s, ckv_s, kr_s, tab_s = _mla_specs(tm)
    hd = pl.BlockSpec((tm, HP), lambda i: (i, 0))
    return pl.pallas_call(
        body, name=name, grid=(s // tm,),
        in_specs=[cq_s, ckv_s, kr_s, tab_s, tab_s, tab_s, _const((1, MLA_Q_RANK)), _const((1, MLA_KV_RANK)),
                  _const((MLA_Q_RANK, HP)), _const((MLA_KV_RANK, 2 * HP)), _const((1, LANES)), _const((1, LANES)),
                  hd, hd, hd],
        out_specs=[pl.BlockSpec((tm, MLA_Q_RANK), lambda i: (i, 0)), pl.BlockSpec((tm, MLA_KV_RANK), lambda i: (i, 0)),
                   pl.BlockSpec((tm, LANES), lambda i: (i, 0)), _const((MLA_Q_RANK, HP)), _const((MLA_KV_RANK, 2 * HP)),
                   _const((1, MLA_Q_RANK)), _const((1, MLA_KV_RANK)), _const((1, LANES)), _const((1, LANES))],
        out_shape=[jax.ShapeDtypeStruct((s, MLA_Q_RANK), BF16), jax.ShapeDtypeStruct((s, MLA_KV_RANK), BF16),
                   jax.ShapeDtypeStruct((s, LANES), BF16), jax.ShapeDtypeStruct((MLA_Q_RANK, HP), F32),
                   jax.ShapeDtypeStruct((MLA_KV_RANK, 2 * HP), F32), jax.ShapeDtypeStruct((1, MLA_Q_RANK), F32),
                   jax.ShapeDtypeStruct((1, MLA_KV_RANK), F32), jax.ShapeDtypeStruct((1, LANES), F32),
                   jax.ShapeDtypeStruct((1, LANES), F32)],
        scratch_shapes=[pltpu.VMEM((tm, HP), F32), pltpu.VMEM((tm, 2 * HP), F32)],
        compiler_params=_params("arbitrary"))(p, p, p, ct, s1, s2, qn_g, kvn_g, wuq, wkv, gq, gk, dq, dk, dv)


_ATT_SCALE = MLA_QK ** -0.5
_NEG = -1e30
_NT = (((1,), (1,)), ((), ()))
_TN = (((0,), (0,)), ((), ()))


def _causal(i, j, t):
    r = lax.broadcasted_iota(jnp.int32, (t, t), 0) + i * t
    c = lax.broadcasted_iota(jnp.int32, (t, t), 1) + j * t
    return r >= c


def _attn_fwd(q, k, v, name):
    s = q.shape[0]
    t = _pick(s, 512)
    n = s // t

    def body(q_ref, k_ref, v_ref, o_ref, lse_ref, m_s, l_s, acc):
        i, j = pl.program_id(1), pl.program_id(2)

        @pl.when(j == 0)
        def _():
            m_s[...] = jnp.full_like(m_s, _NEG)
            l_s[...] = jnp.zeros_like(l_s)
            acc[...] = jnp.zeros_like(acc)

        @pl.when(j <= i)
        def _():
            sc = lax.dot_general(q_ref[...], k_ref[...], _NT, preferred_element_type=F32) * _ATT_SCALE
            sc = jnp.where(_causal(i, j, t), sc, _NEG)
            m_new = jnp.maximum(m_s[...], jnp.max(sc, axis=-1, keepdims=True))
            alpha = jnp.exp(m_s[...] - m_new)
            pr = jnp.exp(sc - m_new)
            l_s[...] = alpha * l_s[...] + jnp.sum(pr, axis=-1, keepdims=True)
            acc[...] = alpha * acc[...] + jnp.dot(pr.astype(BF16), v_ref[...], preferred_element_type=F32)
            m_s[...] = m_new

        @pl.when(j == i)
        def _():
            o_ref[...] = acc[...] / l_s[...]
            lse_ref[...] = m_s[...] + jnp.log(l_s[...])

    qs = pl.BlockSpec((t, LANES), lambda h, i, j: (i, h))
    ks = pl.BlockSpec((t, LANES), lambda h, i, j: (jnp.minimum(j, i), h))
    return pl.pallas_call(
        body, name=name, grid=(MLA_HEADS, n, n), in_specs=[qs, ks, ks],
        out_specs=[qs, pl.BlockSpec((None, t, 1), lambda h, i, j: (h, i, 0))],
        out_shape=[jax.ShapeDtypeStruct((s, HP), F32), jax.ShapeDtypeStruct((MLA_HEADS, s, 1), F32)],
        scratch_shapes=[pltpu.VMEM((t, 1), F32), pltpu.VMEM((t, 1), F32), pltpu.VMEM((t, LANES), F32)],
        compiler_params=_params("parallel", "parallel", "arbitrary"))(q, k, v)


def _attn_bwd_dq(q, k, v, o, lse, do, name):
    s = q.shape[0]
    t = _pick(s, 512)
    n = s // t

    def body(q_ref, k_ref, v_ref, o_ref, lse_ref, do_ref, dq_ref, acc):
        i, j = pl.program_id(1), pl.program_id(2)

        @pl.when(j == 0)
        def _():
            acc[...] = jnp.zeros_like(acc)

        @pl.when(j <= i)
        def _():
            dov = do_ref[...]
            delta = jnp.sum(dov * o_ref[...], axis=-1, keepdims=True)
            sc = lax.dot_general(q_ref[...], k_ref[...], _NT, preferred_element_type=F32) * _ATT_SCALE
            sc = jnp.where(_causal(i, j, t), sc, _NEG)
            pr = jnp.exp(sc - lse_ref[...])
            dp = lax.dot_general(dov.astype(BF16), v_ref[...], _NT, preferred_element_type=F32)
            ds = (pr * (dp - delta) * _ATT_SCALE).astype(BF16)
            acc[...] += jnp.dot(ds, k_ref[...], preferred_element_type=F32)

        @pl.when(j == i)
        def _():
            dq_ref[...] = acc[...]

    qs = pl.BlockSpec((t, LANES), lambda h, i, j: (i, h))
    ks = pl.BlockSpec((t, LANES), lambda h, i, j: (jnp.minimum(j, i), h))
    ls = pl.BlockSpec((None, t, 1), lambda h, i, j: (h, i, 0))
    return pl.pallas_call(
        body, name=name, grid=(MLA_HEADS, n, n), in_specs=[qs, ks, ks, qs, ls, qs], out_specs=qs,
        out_shape=jax.ShapeDtypeStruct((s, HP), F32), scratch_shapes=[pltpu.VMEM((t, LANES), F32)],
        compiler_params=_params("parallel", "parallel", "arbitrary"))(q, k, v, o, lse, do)


def _attn_bwd_dkv(q, k, v, o, lse, do, name):
    s = q.shape[0]
    t = _pick(s, 512)
    n = s // t

    def body(q_ref, k_ref, v_ref, o_ref, lse_ref, do_ref, dk_ref, dv_ref, dk_acc, dv_acc):
        j, i = pl.program_id(1), pl.program_id(2)

        @pl.when(i == 0)
        def _():
            dk_acc[...] = jnp.zeros_like(dk_acc)
            dv_acc[...] = jnp.zeros_like(dv_acc)

        @pl.when(i >= j)
        def _():
            dov = do_ref[...]
            delta = jnp.sum(dov * o_ref[...], axis=-1, keepdims=True)
            sc = lax.dot_general(q_ref[...], k_ref[...], _NT, preferred_element_type=F32) * _ATT_SCALE
            sc = jnp.where(_causal(i, j, t), sc, _NEG)
            pr = jnp.exp(sc - lse_ref[...])
            do16 = dov.astype(BF16)
            dv_acc[...] += lax.dot_general(pr.astype(BF16), do16, _TN, preferred_element_type=F32)
            dp = lax.dot_general(do16, v_ref[...], _NT, preferred_element_type=F32)
            ds = (pr * (dp - delta) * _ATT_SCALE).astype(BF16)
            dk_acc[...] += lax.dot_general(ds, q_ref[...], _TN, preferred_element_type=F32)

        @pl.when(i == n - 1)
        def _():
            dk_ref[...] = dk_acc[...]
            dv_ref[...] = dv_acc[...]

    qs = pl.BlockSpec((t, LANES), lambda h, j, i: (jnp.maximum(i, j), h))
    ks = pl.BlockSpec((t, LANES), lambda h, j, i: (j, h))
    ls = pl.BlockSpec((None, t, 1), lambda h, j, i: (h, jnp.maximum(i, j), 0))
    return pl.pallas_call(
        body, name=name, grid=(MLA_HEADS, n, n), in_specs=[qs, ks, ks, qs, ls, qs], out_specs=[ks, ks],
        out_shape=[jax.ShapeDtypeStruct((s, HP), F32)] * 2,
        scratch_shapes=[pltpu.VMEM((t, LANES), F32), pltpu.VMEM((t, LANES), F32)],
        compiler_params=_params("parallel", "parallel", "arbitrary"))(q, k, v, o, lse, do)


XBC = HP + 2 * SSD_GROUPS * SSD_STATE
BCW = 2 * SSD_GROUPS * SSD_STATE


def _conv_fwd(p, col0, width, conv_w, conv_b, name):
    s = p.shape[0]
    c0, nblk = col0 // LANES, width // LANES

    def body(x_ref, w_ref, b_ref, o_ref, pad):
        pad[0:8, :] = jnp.zeros((8, LANES), F32)
        pad[8:s + 8, :] = x_ref[...].astype(F32)
        acc = jnp.broadcast_to(b_ref[...], (s, LANES))
        for t in range(SSD_CONV):
            acc = acc + pad[pl.ds(8 - (SSD_CONV - 1) + t, s), :] * w_ref[t:t + 1, :]
        o_ref[...] = acc * _sigmoid(acc)

    return pl.pallas_call(
        body, name=name, grid=(nblk,),
        in_specs=[pl.BlockSpec((s, LANES), lambda j: (0, c0 + j)), pl.BlockSpec((SSD_CONV, LANES), lambda j: (0, j)),
                  pl.BlockSpec((1, LANES), lambda j: (0, j))],
        out_specs=pl.BlockSpec((s, LANES), lambda j: (0, j)), out_shape=jax.ShapeDtypeStruct((s, width), F32),
        scratch_shapes=[pltpu.VMEM((s + 8, LANES), F32)], compiler_params=_params("parallel"))(p, conv_w, conv_b)


def _conv_bwd(p, col0, width, conv_w, conv_b, dact, name):
    s = p.shape[0]
    c0, nblk = col0 // LANES, width // LANES

    def body(x_ref, w_ref, b_ref, d_ref, dx_ref, dw_ref, db_ref, pad, padd):
        pad[0:8, :] = jnp.zeros((8, LANES), F32)
        pad[8:s + 8, :] = x_ref[...].astype(F32)
        acc = jnp.broadcast_to(b_ref[...], (s, LANES))
        for t in range(SSD_CONV):
            acc = acc + pad[pl.ds(8 - (SSD_CONV - 1) + t, s), :] * w_ref[t:t + 1, :]
        sg = _sigmoid(acc)
        dpre = d_ref[...] * (sg * (1.0 + acc * (1.0 - sg)))
        padd[0:s, :] = dpre
        padd[s:s + 8, :] = jnp.zeros((8, LANES), F32)
        dx = jnp.zeros((s, LANES), F32)
        for t in range(SSD_CONV):
            dx = dx + padd[pl.ds(SSD_CONV - 1 - t, s), :] * w_ref[t:t + 1, :]
            dw_ref[t:t + 1, :] = jnp.sum(dpre * pad[pl.ds(8 - (SSD_CONV - 1) + t, s), :], axis=0, keepdims=True)
        dx_ref[...] = dx.astype(BF16)
        db_ref[...] = jnp.sum(dpre, axis=0, keepdims=True)

    blk = pl.BlockSpec((s, LANES), lambda j: (0, j))
    return pl.pallas_call(
        body, name=name, grid=(nblk,),
        in_specs=[pl.BlockSpec((s, LANES), lambda j: (0, c0 + j)), pl.BlockSpec((SSD_CONV, LANES), lambda j: (0, j)),
                  pl.BlockSpec((1, LANES), lambda j: (0, j)), blk],
        out_specs=[blk, pl.BlockSpec((SSD_CONV, LANES), lambda j: (0, j)), pl.BlockSpec((1, LANES), lambda j: (0, j))],
        out_shape=[jax.ShapeDtypeStruct((s, width), BF16), jax.ShapeDtypeStruct((SSD_CONV, width), F32),
                   jax.ShapeDtypeStruct((1, width), F32)],
        scratch_shapes=[pltpu.VMEM((s + 8, LANES), F32), pltpu.VMEM((s + 8, LANES), F32)],
        compiler_params=_params("parallel"))(p, conv_w, conv_b, dact)


def _softplus(x):
    return jnp.maximum(x, 0.0) + jnp.log(1.0 + jnp.exp(-jnp.abs(x)))


def _dt_fwd(p, dt_bias, a_log, name):
    s = p.shape[0]
    tm = _pick(s, 512)

    def body(x_ref, b_ref, a_ref, dt_ref, da_ref):
        dtv = _softplus(x_ref[...] + b_ref[...])
        dav = dtv * (-jnp.exp(a_ref[...]))
        for h in range(SSD_HEADS):
            hs = slice(h * LANES, (h + 1) * LANES)
            dt_ref[:, hs] = jnp.broadcast_to(dtv[:, h:h + 1], (tm, LANES))
            da_ref[:, hs] = jnp.broadcast_to(dav[:, h:h + 1], (tm, LANES))

    out = pl.BlockSpec((tm, HP), lambda i: (i, 0))
    return pl.pallas_call(
        body, name=name, grid=(s // tm,),
        in_specs=[pl.BlockSpec((tm, LANES), lambda i: (i, T_DT // LANES)), _const((1, LANES)), _const((1, LANES))],
        out_specs=[out, out], out_shape=[jax.ShapeDtypeStruct((s, HP), F32)] * 2,
        compiler_params=_params("parallel"))(p, dt_bias, a_log)


def _dt_bwd(p, dt_bias, a_log, dda, ddtx, name):
    s = p.shape[0]
    tm = _pick(s, 512)

    def body(x_ref, b_ref, a_ref, dda_ref, ddtx_ref, dx_ref, db_ref, dal_ref):
        @pl.when(pl.program_id(0) == 0)
        def _():
            db_ref[...] = jnp.zeros_like(db_ref)
            dal_ref[...] = jnp.zeros_like(dal_ref)

        x = x_ref[...] + b_ref[...]
        dtv = _softplus(x)
        av = -jnp.exp(a_ref[...])
        lane = lax.broadcasted_iota(jnp.int32, (tm, LANES), 1)
        pa = jnp.zeros((tm, LANES), F32)
        px = jnp.zeros((tm, LANES), F32)
        for h in range(SSD_HEADS):
            pa = jnp.where(lane == h, dda_ref[:, h * LANES:(h + 1) * LANES], pa)
            px = jnp.where(lane == h, ddtx_ref[:, h * LANES:(h + 1) * LANES], px)
        draw = (pa * av + px) * _sigmoid(x)
        dx_ref[...] = draw.astype(BF16)
        db_ref[...] += jnp.sum(draw, axis=0, keepdims=True)
        dal_ref[...] += jnp.sum(pa * dtv, axis=0, keepdims=True) * av

    hd = pl.BlockSpec((tm, HP), lambda i: (i, 0))
    return pl.pallas_call(
        body, name=name, grid=(s // tm,),
        in_specs=[pl.BlockSpec((tm, LANES), lambda i: (i, T_DT // LANES)), _const((1, LANES)), _const((1, LANES)), hd, hd],
        out_specs=[pl.BlockSpec((tm, LANES), lambda i: (i, 0)), _const((1, LANES)), _const((1, LANES))],
        out_shape=[jax.ShapeDtypeStruct((s, LANES), BF16), jax.ShapeDtypeStruct((1, LANES), F32),
                   jax.ShapeDtypeStruct((1, LANES), F32)],
        compiler_params=_params("arbitrary"))(p, dt_bias, a_log, dda, ddtx)


def _cumsum_rows(x):
    row = lax.broadcasted_iota(jnp.int32, x.shape, 0)
    k = 1
    while k < x.shape[0]:
        x = x + jnp.where(row >= k, pltpu.roll(x, k, 0), 0.0)
        k *= 2
    return x


def _rev_cumsum_rows(x):
    n = x.shape[0]
    row = lax.broadcasted_iota(jnp.int32, x.shape, 0)
    k = 1
    while k < n:
        x = x + jnp.where(row < n - k, pltpu.roll(x, n - k, 0), 0.0)
        k *= 2
    return x


HPG = SSD_HEADS // SSD_GROUPS


def _chunk_terms(da, b_mat, c_mat):
    cs = _cumsum_rows(da)
    mask = _tril_mask()
    lm = jnp.exp(jnp.where(mask, cs - cs.T, _NEG))
    g = lax.dot_general(c_mat, b_mat, _NT, preferred_element_type=F32)
    cl = cs[CHUNK - 1:CHUNK, :]
    return cs, lm, g, cl


def _scan_fwd(xs, bc, dtb, dab, name):
    s = xs.shape[0]
    nc = s // CHUNK

    def body(x_ref, b_ref, c_ref, dt_ref, da_ref, y_ref, sin_ref, state):
        c, hh = pl.program_id(1), pl.program_id(2)

        @pl.when(c == 0)
        def _():
            state[hh] = jnp.zeros((SSD_STATE, LANES), F32)

        st = state[hh]
        sin_ref[...] = st
        b16, c16 = b_ref[...].astype(BF16), c_ref[...].astype(BF16)
        cs, lm, g, cl = _chunk_terms(da_ref[...], b16, c16)
        xd = (x_ref[...] * dt_ref[...]).astype(BF16)
        y = jnp.dot((g * lm).astype(BF16), xd, preferred_element_type=F32)
        y = y + jnp.dot(c16, st.astype(BF16), preferred_element_type=F32) * jnp.exp(cs)
        y_ref[...] = y
        bd = (b_ref[...] * jnp.exp(cl - cs)).astype(BF16)
        state[hh] = jnp.exp(cl) * st + lax.dot_general(bd, xd, _TN, preferred_element_type=F32)

    hd = pl.BlockSpec((CHUNK, LANES), lambda g, c, hh: (c, g * HPG + hh))
    return pl.pallas_call(
        body, name=name, grid=(SSD_GROUPS, nc, HPG),
        in_specs=[hd, pl.BlockSpec((CHUNK, LANES), lambda g, c, hh: (c, g)),
                  pl.BlockSpec((CHUNK, LANES), lambda g, c, hh: (c, SSD_GROUPS + g)), hd, hd],
        out_specs=[hd, pl.BlockSpec((None, None, SSD_STATE, LANES), lambda g, c, hh: (g * HPG + hh, c, 0, 0))],
        out_shape=[jax.ShapeDtypeStruct((s, HP), F32), jax.ShapeDtypeStruct((SSD_HEADS, nc, SSD_STATE, LANES), F32)],
        scratch_shapes=[pltpu.VMEM((HPG, SSD_STATE, LANES), F32)],
        compiler_params=_params("parallel", "arbitrary", "arbitrary"))(xs, bc, bc, dtb, dab)


def _scan_bwd(xs, bc, dtb, dab, s_in, dy, d_vec, name):
    s = xs.shape[0]
    nc = s // CHUNK

    def body(x_ref, b_ref, c_ref, dt_ref, da_ref, sin_ref, dy_ref, dv_ref, dx_ref, db_ref, dc_ref, dda_ref, ddtx_ref, dstate):
        c, hh = pl.program_id(1), pl.program_id(2)

        @pl.when(c == 0)
        def _():
            dstate[hh] = jnp.zeros((SSD_STATE, LANES), F32)

        st, ds = sin_ref[...], dstate[hh]
        st16, ds16 = st.astype(BF16), ds.astype(BF16)
        xv, bv, dtv, dyv = x_ref[...], b_ref[...], dt_ref[...], dy_ref[...]
        b16, c16 = bv.astype(BF16), c_ref[...].astype(BF16)
        cs, lm, g, cl = _chunk_terms(da_ref[...], b16, c16)
        ecs, ecl = jnp.exp(cs), jnp.exp(cl)
        decay = jnp.exp(cl - cs)
        xd32 = xv * dtv
        xd = xd32.astype(BF16)
        dy16 = dyv.astype(BF16)
        dye = (dyv * ecs).astype(BF16)
        yoff = jnp.dot(c16, st16, preferred_element_type=F32) * ecs
        dcs = jnp.sum(dyv * yoff, axis=-1, keepdims=True)
        dcm = lax.dot_general(dye, st16, _NT, preferred_element_type=F32)
        ds_in = ecl * ds + lax.dot_general(c16, dye, _TN, preferred_element_type=F32)
        dcl = jnp.sum(jnp.sum(ds * st, axis=0, keepdims=True), axis=1, keepdims=True) * ecl[:, 0:1]
        bd32 = bv * decay
        bd = bd32.astype(BF16)
        qm = lax.dot_general(xd, ds16, _NT, preferred_element_type=F32)
        dbm = qm * decay
        w = jnp.sum(bd32 * qm, axis=-1, keepdims=True)
        dcs = dcs - w
        dcl = dcl + jnp.sum(w, axis=0, keepdims=True)
        dxd = jnp.dot(bd, ds16, preferred_element_type=F32)
        m16 = (g * lm).astype(BF16)
        dm = lax.dot_general(dy16, xd, _NT, preferred_element_type=F32)
        dxd = dxd + lax.dot_general(m16, dy16, _TN, preferred_element_type=F32)
        dg = dm * lm
        dg16 = dg.astype(BF16)
        tt = dg * g
        dcm = dcm + jnp.dot(dg16, b16, preferred_element_type=F32)
        dbm = dbm + lax.dot_general(dg16, c16, _TN, preferred_element_type=F32)
        dcs = dcs + jnp.sum(tt, axis=-1, keepdims=True) - jnp.sum(tt.T, axis=-1, keepdims=True)
        row = lax.broadcasted_iota(jnp.int32, (CHUNK, 1), 0)
        dcs = dcs + jnp.where(row == CHUNK - 1, dcl, 0.0)
        dda_ref[...] = _rev_cumsum_rows(jnp.broadcast_to(dcs, (CHUNK, LANES)))
        ddtx_ref[...] = jnp.broadcast_to(jnp.sum(dxd * xv, axis=-1, keepdims=True), (CHUNK, LANES))
        dx_ref[...] = dxd * dtv + dyv * dv_ref[...]
        dstate[hh] = ds_in

        @pl.when(hh == 0)
        def _():
            db_ref[...] = dbm
            dc_ref[...] = dcm

        @pl.when(hh != 0)
        def _():
            db_ref[...] += dbm
            dc_ref[...] += dcm

    hd = pl.BlockSpec((CHUNK, LANES), lambda g, c, hh: (nc - 1 - c, g * HPG + hh))
    gp = pl.BlockSpec((CHUNK, LANES), lambda g, c, hh: (nc - 1 - c, g))
    return pl.pallas_call(
        body, name=name, grid=(SSD_GROUPS, nc, HPG),
        in_specs=[hd, pl.BlockSpec((CHUNK, LANES), lambda g, c, hh: (nc - 1 - c, g)),
                  pl.BlockSpec((CHUNK, LANES), lambda g, c, hh: (nc - 1 - c, SSD_GROUPS + g)), hd, hd,
                  pl.BlockSpec((None, None, SSD_STATE, LANES), lambda g, c, hh: (g * HPG + hh, nc - 1 - c, 0, 0)), hd,
                  pl.BlockSpec((1, LANES), lambda g, c, hh: (0, g * HPG + hh))],
        out_specs=[hd, gp, gp, hd, hd],
        out_shape=[jax.ShapeDtypeStruct((s, HP), F32), jax.ShapeDtypeStruct((s, SSD_GROUPS * SSD_STATE), F32),
                   jax.ShapeDtypeStruct((s, SSD_GROUPS * SSD_STATE), F32), jax.ShapeDtypeStruct((s, HP), F32),
                   jax.ShapeDtypeStruct((s, HP), F32)],
        scratch_shapes=[pltpu.VMEM((HPG, SSD_STATE, LANES), F32)],
        compiler_params=_params("parallel", "arbitrary", "arbitrary"))(xs, bc, bc, dtb, dab, s_in, dy, d_vec)


_GN = SSD_INNER // SSD_GROUPS
_GW = HP // SSD_GROUPS


def _ssd_post_fwd(y, xbc, p, d_vec, gain, name):
    s = y.shape[0]
    tm = _pick(s, 512)

    def body(y_ref, x_ref, z_ref, d_ref, g_ref, o_ref):
        z = z_ref[...].astype(F32)
        y2 = (y_ref[...] + x_ref[...] * d_ref[...]) * (z * _sigmoid(z))
        for g in range(SSD_GROUPS):
            gs = slice(g * _GW, (g + 1) * _GW)
            yg = y2[:, gs]
            r = lax.rsqrt(jnp.sum(yg * yg, axis=-1, keepdims=True) * (1.0 / _GN) + EPS)
            o_ref[:, gs] = (yg * r * g_ref[:, gs]).astype(BF16)

    hd = pl.BlockSpec((tm, HP), lambda i: (i, 0))
    return pl.pallas_call(
        body, name=name, grid=(s // tm,),
        in_specs=[hd, hd, pl.BlockSpec((tm, HP), lambda i: (i, C_Z // HP)), _const((1, HP)), _const((1, HP))],
        out_specs=hd, out_shape=jax.ShapeDtypeStruct((s, HP), BF16), compiler_params=_params("parallel"))(y, xbc, p, d_vec, gain)


def _ssd_post_bwd(y, xbc, p, d_vec, gain, dyn, name):
    s = y.shape[0]
    tm = _pick(s, 512)

    def body(y_ref, x_ref, z_ref, d_ref, g_ref, dn_ref, dy_ref, dz_ref, dg_ref, dd_ref):
        @pl.when(pl.program_id(0) == 0)
        def _():
            dg_ref[...] = jnp.zeros_like(dg_ref)
            dd_ref[...] = jnp.zeros_like(dd_ref)

        z, xv = z_ref[...].astype(F32), x_ref[...]
        sg = _sigmoid(z)
        sz = z * sg
        yt = y_ref[...] + xv * d_ref[...]
        y2 = yt * sz
        for g in range(SSD_GROUPS):
            gs = slice(g * _GW, (g + 1) * _GW)
            yg, dn = y2[:, gs], dn_ref[:, gs]
            r = lax.rsqrt(jnp.sum(yg * yg, axis=-1, keepdims=True) * (1.0 / _GN) + EPS)
            u = dn * g_ref[:, gs]
            dy2 = r * u - yg * (r * r * r) * (jnp.sum(yg * u, axis=-1, keepdims=True) * (1.0 / _GN))
            dg_ref[:, gs] += jnp.sum(dn * yg * r, axis=0, keepdims=True)
            dyt = dy2 * sz[:, gs]
            dy_ref[:, gs] = dyt
            dz_ref[:, gs] = (dy2 * yt[:, gs] * (sg[:, gs] * (1.0 + z[:, gs] * (1.0 - sg[:, gs])))).astype(BF16)
            dd_ref[:, gs] += jnp.sum(dyt * xv[:, gs], axis=0, keepdims=True)

    hd = pl.BlockSpec((tm, HP), lambda i: (i, 0))
    return pl.pallas_call(
        body, name=name, grid=(s // tm,),
        in_specs=[hd, hd, pl.BlockSpec((tm, HP), lambda i: (i, C_Z // HP)), _const((1, HP)), _const((1, HP)), hd],
        out_specs=[hd, hd, _const((1, HP)), _const((1, HP))],
        out_shape=[jax.ShapeDtypeStruct((s, HP), F32), jax.ShapeDtypeStruct((s, HP), BF16),
                   jax.ShapeDtypeStruct((1, HP), F32), jax.ShapeDtypeStruct((1, HP), F32)],
        compiler_params=_params("arbitrary"))(y, xbc, p, d_vec, gain, dyn)


def _merge_fwd(p, ya, o, yc, wb0, wb1, wb2, w_out, x, name):
    s = p.shape[0]
    tm = _pick(s, 512)

    def body(g_ref, ya_ref, o_ref, yc_ref, w0_ref, w1_ref, w2_ref, wo_ref, x_ref, mg_ref, y_ref):
        acc = jnp.zeros((tm, D_MODEL), F32)
        for i, (b_ref, w_ref) in enumerate(((ya_ref, w0_ref), (o_ref, w1_ref), (yc_ref, w2_ref))):
            t = jnp.dot(b_ref[...].astype(BF16), w_ref[...], preferred_element_type=F32)
            acc = acc + _sigmoid(g_ref[:, i * D_MODEL:(i + 1) * D_MODEL].astype(F32)) * t
        mg = acc.astype(BF16)
        mg_ref[...] = mg
        y_ref[...] = x_ref[...] + jnp.dot(mg, wo_ref[...], preferred_element_type=F32)

    row = pl.BlockSpec((tm, D_MODEL), lambda i: (i, 0))
    return pl.pallas_call(
        body, name=name, grid=(s // tm,),
        in_specs=[pl.BlockSpec((tm, 3 * D_MODEL), lambda i: (i, C_G // (3 * D_MODEL))),
                  pl.BlockSpec((tm, GM_WIDTH), lambda i: (i, 0)), row, row,
                  _resident((GM_WIDTH, D_MODEL)), _resident((HP, D_MODEL)), _resident((HP, D_MODEL)),
                  _resident((D_MODEL, D_MODEL)), row],
        out_specs=[row, row],
        out_shape=[jax.ShapeDtypeStruct((s, D_MODEL), BF16), jax.ShapeDtypeStruct((s, D_MODEL), F32)],
        compiler_params=_params("parallel"))(p, ya, o, yc, wb0, wb1, wb2, w_out, x)


def _merge_bwd(p, ya, o, yc, wb0, wb1, wb2, w_out, dy, name):
    s = p.shape[0]
    tm = _pick(s, 512)

    def body(g_ref, ya_ref, o_ref, yc_ref, w0_ref, w1_ref, w2_ref, wo_ref, dy_ref,
             d0_ref, d1_ref, d2_ref, dg_ref, dya_ref, do_ref, dyc_ref):
        dm = lax.dot_general(dy_ref[...].astype(BF16), wo_ref[...], _NT, preferred_element_type=F32)
        for i, (b_ref, w_ref, d_ref, db_ref) in enumerate(((ya_ref, w0_ref, d0_ref, dya_ref), (o_ref, w1_ref, d1_ref, do_ref),
                                                            (yc_ref, w2_ref, d2_ref, dyc_ref))):
            cs = slice(i * D_MODEL, (i + 1) * D_MODEL)
            t = jnp.dot(b_ref[...].astype(BF16), w_ref[...], preferred_element_type=F32)
            sg = _sigmoid(g_ref[:, cs].astype(F32))
            dt16 = (dm * sg).astype(BF16)
            d_ref[...] = dt16
            dg_ref[:, cs] = (dm * t * sg * (1.0 - sg)).astype(BF16)
            db_ref[...] = lax.dot_general(dt16, w_ref[...], _NT, preferred_element_type=F32)

    row = pl.BlockSpec((tm, D_MODEL), lambda i: (i, 0))
    nar = pl.BlockSpec((tm, GM_WIDTH), lambda i: (i, 0))
    wide = pl.BlockSpec((tm, 3 * D_MODEL), lambda i: (i, 0))
    return pl.pallas_call(
        body, name=name, grid=(s // tm,),
        in_specs=[pl.BlockSpec((tm, 3 * D_MODEL), lambda i: (i, C_G // (3 * D_MODEL))), nar, row, row,
                  _resident((GM_WIDTH, D_MODEL)), _resident((HP, D_MODEL)), _resident((HP, D_MODEL)),
                  _resident((D_MODEL, D_MODEL)), row],
        out_specs=[row, row, row, wide, nar, row, row],
        out_shape=[jax.ShapeDtypeStruct((s, D_MODEL), BF16)] * 3 + [jax.ShapeDtypeStruct((s, 3 * D_MODEL), BF16),
                   jax.ShapeDtypeStruct((s, GM_WIDTH), F32), jax.ShapeDtypeStruct((s, D_MODEL), F32),
                   jax.ShapeDtypeStruct((s, D_MODEL), F32)],
        compiler_params=_params("parallel"))(p, ya, o, yc, wb0, wb1, wb2, w_out, dy)


def _loss_head(y, target, name):
    s, d = y.shape
    tm = _pick(s, 512)

    def body(y_ref, t_ref, dy_ref, sq_ref):
        @pl.when(pl.program_id(0) == 0)
        def _():
            sq_ref[...] = jnp.zeros_like(sq_ref)

        e = y_ref[...] - t_ref[...]
        dy_ref[...] = e * (1.0 / d)
        sq_ref[...] += jnp.sum(e * e, axis=0, keepdims=True)

    row = pl.BlockSpec((tm, d), lambda i: (i, 0))
    return pl.pallas_call(
        body, name=name, grid=(s // tm,), in_specs=[row, row], out_specs=[row, _const((1, d))],
        out_shape=[jax.ShapeDtypeStruct((s, d), F32), jax.ShapeDtypeStruct((1, d), F32)],
        compiler_params=_params("arbitrary"))(y, target)


def _adamw(w, g, m, v, name):
    rows, cols = w.shape
    tr = rows
    for cand in (512, 256, 128, 64, 32, 16, 8):
        if rows % cand == 0 and cand * cols * 4 <= 3 * 1024 * 1024:
            tr = cand
            break
    c1 = 1.0 - ADAM_B1 ** ADAM_STEP
    c2 = 1.0 - ADAM_B2 ** ADAM_STEP

    def body(w_ref, g_ref, m_ref, v_ref, d_ref, nm_ref, nv_ref):
        gv = g_ref[...]
        nm = ADAM_B1 * m_ref[...] + (1.0 - ADAM_B1) * gv
        nv = ADAM_B2 * v_ref[...] + (1.0 - ADAM_B2) * (gv * gv)
        nm_ref[...] = nm
        nv_ref[...] = nv
        d_ref[...] = -ADAM_LR * ((nm / c1) / (jnp.sqrt(nv / c2) + ADAM_EPS) + ADAM_WD * w_ref[...])

    blk = pl.BlockSpec((tr, cols), lambda i: (i, 0))
    return pl.pallas_call(
        body, name=name, grid=(rows // tr,), in_specs=[blk] * 4, out_specs=[blk] * 3,
        out_shape=[jax.ShapeDtypeStruct((rows, cols), F32)] * 3, compiler_params=_params("parallel"))(w, g, m, v)


ANY = pl.BlockSpec(memory_space=pl.ANY)


def _me():
    return lax.axis_index("x"), lax.axis_index("y"), lax.axis_index("c")


def _other_chips(x, y):
    return [(1 - x, y), (x, 1 - y), (1 - x, 1 - y)]


def _chip_index(cx, cy):
    return 2 * cx + cy


def _gather_shards(shards):
    n = len(shards)

    def body(*refs):
        in_refs, out_refs = refs[:n], refs[n:2 * n]
        send_sems, recv_sems, local_sems = refs[2 * n:]
        x, y, c = _me()
        sib = (x, y, 1 - c)
        chips = _other_chips(x, y)
        mychip = _chip_index(x, y)

        def part(t, chip, hc):
            return out_refs[t].at[pl.ds(hc * HALF_L, HALF_L), chip]

        def copy(t, k, chip, hc, to, src=None):
            dst = part(t, chip, hc)
            return pltpu.make_async_remote_copy(src_ref=dst if src is None else src, dst_ref=dst, send_sem=send_sems.at[6 * t + k],
                                                recv_sem=recv_sems.at[6 * t + k], device_id=to, device_id_type=MESH)

        local = [pltpu.make_async_copy(in_refs[t], out_refs[t].at[pl.ds(0, DEPTH), mychip], local_sems.at[t]) for t in range(n)]
        for cp in local:
            cp.start()
        first = [copy(t, j, mychip, c, (*chip, c), src=in_refs[t].at[pl.ds(c * HALF_L, HALF_L)])
                 for j, chip in enumerate(chips) for t in range(n)]
        for cp in first:
            cp.start()
        passed = []
        for j, chip in enumerate(chips):
            for t in range(n):
                copy(t, j, _chip_index(*chip), c, (x, y, c)).wait_recv()
                cp = copy(t, 3 + j, _chip_index(*chip), c, sib)
                cp.start()
                passed.append(cp)
        for j, chip in enumerate(chips):
            for t in range(n):
                copy(t, 3 + j, _chip_index(*chip), 1 - c, (x, y, c)).wait_recv()
        for cp in first + passed:
            cp.wait_send()
        for cp in local:
            cp.wait()

    return pl.pallas_call(
        body, name="gather_shards", in_specs=[ANY] * n, out_specs=[ANY] * n,
        out_shape=[jax.ShapeDtypeStruct((DEPTH, N_CHIPS) + a.shape[1:], a.dtype) for a in shards],
        scratch_shapes=[pltpu.SemaphoreType.DMA((6 * n,)), pltpu.SemaphoreType.DMA((6 * n,)), pltpu.SemaphoreType.DMA((n,))])(*shards)


def _pair_exchange(gs):
    n = len(gs)

    def body(*refs):
        in_refs, out_refs = refs[:n], refs[n:2 * n]
        send_sems, recv_sems = refs[2 * n:]
        x, y, c = _me()
        cps = [pltpu.make_async_remote_copy(src_ref=in_refs[t].at[pl.ds((1 - c) * HALF_L, HALF_L)], dst_ref=out_refs[t],
                                            send_sem=send_sems.at[t], recv_sem=recv_sems.at[t], device_id=(x, y, 1 - c),
                                            device_id_type=MESH) for t in range(n)]
        for cp in cps:
            cp.start()
        for cp in cps:
            cp.wait()

    return pl.pallas_call(
        body, name="pair_exchange", in_specs=[ANY] * n, out_specs=[ANY] * n,
        out_shape=[jax.ShapeDtypeStruct((HALF_L,) + a.shape[1:], a.dtype) for a in gs],
        scratch_shapes=[pltpu.SemaphoreType.DMA((n,)), pltpu.SemaphoreType.DMA((n,))])(*gs)


def _row_tile(rows, cols, budget=2 * 1024 * 1024):
    best = None
    for t in range(8, rows + 1, 8):
        if rows % t == 0 and t * cols * 4 <= budget:
            best = t
    return best or rows


def _pair_add(g, got, name):
    _, _, rows, cols = g.shape
    tr = _row_tile(rows, cols)

    def body(lo_ref, hi_ref, r_ref, o16_ref, own_ref):
        x, y, c = _me()
        tot = jnp.where(c == 0, lo_ref[...], hi_ref[...]) + r_ref[...]
        o16_ref[...] = tot.astype(BF16)

        @pl.when(pl.program_id(2) == _chip_index(x, y))
        def _():
            own_ref[...] = tot

    blk = (None, None, tr, cols)
    return pl.pallas_call(
        body, name=name, grid=(HALF_L, rows // tr, N_CHIPS),
        in_specs=[pl.BlockSpec(blk, lambda a, i, k: (a, k, i, 0)), pl.BlockSpec(blk, lambda a, i, k: (a + HALF_L, k, i, 0)),
                  pl.BlockSpec(blk, lambda a, i, k: (a, k, i, 0))],
        out_specs=[pl.BlockSpec(blk, lambda a, i, k: (a, k, i, 0)), pl.BlockSpec((None, tr, cols), lambda a, i, k: (a, i, 0))],
        out_shape=[jax.ShapeDtypeStruct((HALF_L, N_CHIPS, rows, cols), BF16), jax.ShapeDtypeStruct((HALF_L, rows, cols), F32)],
        compiler_params=_params("parallel", "parallel", "arbitrary"))(g, g, got)


def _chip_exchange(parts):
    n = len(parts)

    def body(*refs):
        in_refs, out_refs = refs[:n], refs[n:2 * n]
        send_sems, recv_sems, local_sems = refs[2 * n:]
        x, y, c = _me()
        chips = _other_chips(x, y)
        mychip = _chip_index(x, y)
        rows = pl.ds(0, HALF_L)
        local = [pltpu.make_async_copy(in_refs[t].at[rows, mychip], out_refs[t].at[mychip], local_sems.at[t]) for t in range(n)]
        for cp in local:
            cp.start()
        cps = [pltpu.make_async_remote_copy(src_ref=in_refs[t].at[rows, _chip_index(*chip)], dst_ref=out_refs[t].at[mychip],
                                            send_sem=send_sems.at[3 * t + j], recv_sem=recv_sems.at[3 * t + j],
                                            device_id=(*chip, c), device_id_type=MESH)
               for j, chip in enumerate(chips) for t in range(n)]
        for cp in cps:
            cp.start()
        for j, chip in enumerate(chips):
            for t in range(n):
                pltpu.make_async_remote_copy(src_ref=in_refs[t].at[rows, mychip], dst_ref=out_refs[t].at[_chip_index(*chip)],
                                             send_sem=send_sems.at[3 * t + j], recv_sem=recv_sems.at[3 * t + j],
                                             device_id=(*chip, c), device_id_type=MESH).wait_recv()
        for cp in cps:
            cp.wait_send()
        for cp in local:
            cp.wait()

    return pl.pallas_call(
        body, name="chip_exchange", in_specs=[ANY] * n, out_specs=[ANY] * n,
        out_shape=[jax.ShapeDtypeStruct((N_CHIPS, HALF_L) + a.shape[2:], a.dtype) for a in parts],
        scratch_shapes=[pltpu.SemaphoreType.DMA((3 * n,)), pltpu.SemaphoreType.DMA((3 * n,)), pltpu.SemaphoreType.DMA((n,))])(*parts)


def _chip_add(own, got, name):
    _, rows, cols = own.shape
    tr = _row_tile(rows, cols, 1024 * 1024)

    def body(own_ref, got_ref, o_ref):
        x, y, _ = _me()
        mychip = _chip_index(x, y)
        acc = jnp.zeros((tr, cols), F32)
        for k in range(N_CHIPS):
            acc = acc + jnp.where(mychip == k, own_ref[...], got_ref[k].astype(F32))
        o_ref[...] = acc

    return pl.pallas_call(
        body, name=name, grid=(HALF_L, rows // tr),
        in_specs=[pl.BlockSpec((None, tr, cols), lambda a, i: (a, i, 0)),
                  pl.BlockSpec((N_CHIPS, None, tr, cols), lambda a, i: (0, a, i, 0))],
        out_specs=pl.BlockSpec((None, tr, cols), lambda a, i: (a, i, 0)),
        out_shape=jax.ShapeDtypeStruct((HALF_L, rows, cols), F32), compiler_params=_params("parallel", "parallel"))(own, got)


def _pair_share(halves):
    n = len(halves)

    def body(*refs):
        in_refs, out_refs = refs[:n], refs[n:2 * n]
        send_sems, recv_sems, local_sems = refs[2 * n:]
        x, y, c = _me()
        sib = (x, y, 1 - c)
        mine, theirs = pl.ds(c * HALF_L, HALF_L), pl.ds((1 - c) * HALF_L, HALF_L)
        local = [pltpu.make_async_copy(in_refs[t], out_refs[t].at[mine], local_sems.at[t]) for t in range(n)]
        for cp in local:
            cp.start()
        cps = [pltpu.make_async_remote_copy(src_ref=in_refs[t], dst_ref=out_refs[t].at[mine], send_sem=send_sems.at[t],
                                            recv_sem=recv_sems.at[t], device_id=sib, device_id_type=MESH) for t in range(n)]
        for cp in cps:
            cp.start()
        for t in range(n):
            pltpu.make_async_remote_copy(src_ref=in_refs[t], dst_ref=out_refs[t].at[theirs], send_sem=send_sems.at[t],
                                         recv_sem=recv_sems.at[t], device_id=sib, device_id_type=MESH).wait_recv()
        for cp in cps:
            cp.wait_send()
        for cp in local:
            cp.wait()

    return pl.pallas_call(
        body, name="pair_share", in_specs=[ANY] * n, out_specs=[ANY] * n,
        out_shape=[jax.ShapeDtypeStruct((DEPTH,) + a.shape[1:], a.dtype) for a in halves],
        scratch_shapes=[pltpu.SemaphoreType.DMA((n,)), pltpu.SemaphoreType.DMA((n,)), pltpu.SemaphoreType.DMA((n,))])(*halves)


N_DEV = 8


def _all_exchange(v):
    r, cols = v.shape

    def body(in_ref, out_ref, send_sems, recv_sems, local_sem):
        x, y, c = _me()
        me = 4 * x + 2 * y + c
        local = pltpu.make_async_copy(in_ref, out_ref.at[me], local_sem)
        local.start()
        flip = lambda v, f: 1 - v if f else v
        peers = [(flip(x, fx), flip(y, fy), flip(c, fc)) for fx in (0, 1) for fy in (0, 1) for fc in (0, 1)][1:]
        cps = [pltpu.make_async_remote_copy(src_ref=in_ref, dst_ref=out_ref.at[me], send_sem=send_sems.at[j],
                                            recv_sem=recv_sems.at[j], device_id=peer, device_id_type=MESH)
               for j, peer in enumerate(peers)]
        for cp in cps:
            cp.start()
        for j, (px, py, pc) in enumerate(peers):
            pltpu.make_async_remote_copy(src_ref=in_ref, dst_ref=out_ref.at[4 * px + 2 * py + pc], send_sem=send_sems.at[j],
                                         recv_sem=recv_sems.at[j], device_id=(px, py, pc), device_id_type=MESH).wait_recv()
        for cp in cps:
            cp.wait_send()
        local.wait()

    return pl.pallas_call(
        body, name="all_exchange", in_specs=[ANY], out_specs=ANY, out_shape=jax.ShapeDtypeStruct((N_DEV, r, cols), v.dtype),
        scratch_shapes=[pltpu.SemaphoreType.DMA((7,)), pltpu.SemaphoreType.DMA((7,)), pltpu.SemaphoreType.DMA(())])(v)


def _sum_slots(a, name):
    n, r, cols = a.shape
    tr = _pick(r, 512) if r % 8 == 0 else r
    for cand in (512, 256, 128, 64, 32, 16, 8):
        if r % cand == 0:
            tr = cand
            break

    def body(a_ref, o_ref):
        acc = a_ref[0]
        for k in range(1, n):
            acc = acc + a_ref[k]
        o_ref[...] = acc

    return pl.pallas_call(
        body, name=name, grid=(r // tr,), in_specs=[pl.BlockSpec((n, tr, cols), lambda i: (0, i, 0))],
        out_specs=pl.BlockSpec((tr, cols), lambda i: (i, 0)), out_shape=jax.ShapeDtypeStruct((r, cols), F32),
        compiler_params=_params("parallel"))(a)


def _join(name, stacked):
    ax = SHARDED[name][1]
    return jnp.concatenate([stacked[k] for k in range(N_CHIPS)], axis=ax)


def _split(name, full):
    ax = SHARDED[name][1]
    return jnp.stack(jnp.split(full, N_CHIPS, axis=ax))


def _heads_pad(a, real, axis):
    shp = a.shape
    a = a.reshape(shp[:axis] + (MLA_HEADS, real) + shp[axis + 1:])
    pad = [(0, 0)] * a.ndim
    pad[axis + 1] = (0, LANES - real)
    a = jnp.pad(a, pad)
    return a.reshape(shp[:axis] + (HP,) + shp[axis + 1:])


def _heads_unpad(a, real, axis):
    shp = a.shape
    a = a.reshape(shp[:axis] + (MLA_HEADS, LANES) + shp[axis + 1:])
    a = lax.slice_in_dim(a, 0, real, axis=axis + 1)
    return a.reshape(shp[:axis] + (MLA_HEADS * real,) + shp[axis + 1:])


def _lane_place(a, start):
    n = a.shape[-1]
    pad = [(0, 0)] * (a.ndim - 1) + [(start, LANES - start - n)]
    return jnp.pad(a, pad)


_O_UV, _O_CQ, _O_CKV, _O_KR, _O_Z, _O_XBC, _O_DT, _O_G = 0, 1024, 1408, 1664, 1696, 2208, 3232, 3240


def _w_in_pad(w):
    sl = lambda a, b: w[:, a:b]
    xs = _heads_pad(sl(_O_XBC, _O_XBC + SSD_INNER), SSD_HEAD_DIM, 1)
    bc = sl(_O_XBC + SSD_INNER, _O_DT)
    main = jnp.concatenate([sl(_O_UV, _O_CQ), _heads_pad(sl(_O_Z, _O_XBC), SSD_HEAD_DIM, 1), xs, sl(_O_G, IN_COLS)], axis=1)
    tail = jnp.concatenate([bc, sl(_O_CKV, _O_KR), sl(_O_CQ, _O_CKV), _lane_place(sl(_O_KR, _O_Z), MLA_NOPE),
                            _lane_place(sl(_O_DT, _O_G), 0), jnp.zeros((w.shape[0], PW_TAIL - T_DT - LANES), w.dtype)], axis=1)
    return main, tail


def _w_in_unpad(gm, gt):
    m = lambda a, n: gm[:, a:a + n]
    t = lambda a, n: gt[:, a:a + n]
    parts = [m(C_UV, 1024), t(T_CQ, MLA_Q_RANK), t(T_CKV, MLA_KV_RANK), t(T_KR + MLA_NOPE, MLA_ROPE),
             _heads_unpad(m(C_Z, HP), SSD_HEAD_DIM, 1), _heads_unpad(m(C_XS, HP), SSD_HEAD_DIM, 1), t(T_BC, BCW),
             t(T_DT, SSD_HEADS), m(C_G, 3 * D_MODEL)]
    return jnp.concatenate(parts, axis=1)


def _xbc_pad(a):
    return jnp.concatenate([_heads_pad(a[..., :SSD_INNER], SSD_HEAD_DIM, a.ndim - 1), a[..., SSD_INNER:]], axis=-1)


def _xbc_unpad(a):
    return jnp.concatenate([_heads_unpad(a[..., :HP], SSD_HEAD_DIM, a.ndim - 1), a[..., HP:]], axis=-1)


def _rope_tables(positions):
    inv_freq = 1.0 / (ROPE_THETA ** (jnp.arange(0, MLA_ROPE, 2, dtype=F32) / MLA_ROPE))
    ang = positions.astype(F32)[:, None] * inv_freq
    cos, sin = jnp.cos(ang), jnp.sin(ang)
    s = positions.shape[0]
    half = MLA_ROPE // 2
    z = lambda n: jnp.zeros((s, n), F32)
    ct = jnp.concatenate([jnp.ones((s, MLA_NOPE), F32), cos, cos, z(LANES - MLA_QK)], axis=1)
    s1 = jnp.concatenate([z(MLA_NOPE), -sin, z(half), z(LANES - MLA_QK)], axis=1)
    s2 = jnp.concatenate([z(MLA_NOPE), z(half), sin, z(LANES - MLA_QK)], axis=1)
    return ct, s1, s2


def _layer_weights(full, small, l):
    w = {}
    for n in ('ffn1_w_in', 'ffn2_w_in'):
        w[n] = full[n][l]
    for n in ('ffn1_w_out', 'ffn2_w_out', 'w_out'):
        g = full[n][l]
        w[n] = g.reshape((N_CHIPS * g.shape[1], g.shape[2]))
    fl = {n: _join(n, full[n][l]) for n in ('w_in', 'mla_w_uq', 'mla_w_ukv', 'w_branch', 'ssd_conv_w')}
    w['w_in_main'], w['w_in_tail'] = _w_in_pad(fl['w_in'])
    w['wuq'] = _heads_pad(fl['mla_w_uq'], MLA_QK, 1)
    ukv = fl['mla_w_ukv'].reshape(MLA_KV_RANK, MLA_HEADS, MLA_NOPE + MLA_V)
    zero = jnp.zeros((MLA_KV_RANK, MLA_HEADS, LANES - MLA_NOPE), ukv.dtype)
    wk = jnp.concatenate([ukv[:, :, :MLA_NOPE], zero], axis=2).reshape(MLA_KV_RANK, HP)
    wv = jnp.concatenate([ukv[:, :, MLA_NOPE:], zero], axis=2).reshape(MLA_KV_RANK, HP)
    w['wkv'] = jnp.concatenate([wk, wv], axis=1)
    wb = fl['w_branch']
    w['wb0'] = wb[0]
    w['wb1'] = _heads_pad(wb[1], MLA_V, 0)
    w['wb2'] = _heads_pad(wb[2], SSD_HEAD_DIM, 0)
    w['conv_w'] = _xbc_pad(fl['ssd_conv_w'].astype(F32))
    row = lambda n: small[n][l][None, :]
    for n in ('ffn1_norm', 'mix_norm', 'gm_v_norm', 'mla_q_norm', 'mla_kv_norm', 'ffn2_norm'):
        w[n] = row(n)
    w['gm_w_s'] = small['gm_w_s'][l]
    w['gm_b_full'] = jnp.broadcast_to(small['gm_b_s'][l][:, :, None], (GM_GROUPS, CHUNK, LANES))
    w['gq'] = _lane_place(row('mla_q_gain'), 0)
    w['gk'] = _lane_place(row('mla_k_gain'), 0)
    w['conv_b'] = _xbc_pad(row('ssd_conv_b'))
    w['dt_bias'] = _lane_place(row('ssd_dt_bias'), 0)
    w['a_log'] = _lane_place(row('ssd_a_log'), 0)
    w['d_vec'] = jnp.repeat(small['ssd_d'][l], LANES)[None, :]
    w['ssd_norm'] = _heads_pad(row('ssd_norm'), SSD_HEAD_DIM, 1)
    return w


def _ffn_fwd(x, norm, w4, w_out, tag):
    h, gate, up, act = _ffn_in(x, norm, w4, f"{tag}_in")
    y = _ffn_out(act, w_out, x, f"{tag}_out")
    return y, (x, h, gate, up, act)


def _ffn_bwd(dy, saved, norm, w4, w_out, tag):
    x, h, gate, up, act = saved
    dw_out = _ffn_dwout(act, dy, f"{tag}_dwout")
    da = _ffn_dact(dy, w_out, gate, up, f"{tag}_dact")
    dw_in = _ffn_dwin(h, da, f"{tag}_dwin")
    dx, dnorm = _ffn_dx(da, w4, x, norm, dy, f"{tag}_dx")
    return dx, dnorm, dw_in, dw_out.reshape((N_CHIPS, 2 * FC // N_CHIPS, D_MODEL))


def _mixer_fwd(x, w, tabs, tag):
    h = _rmsnorm_fwd(x, w['mix_norm'], f"{tag}_norm")
    pm = _matmul(h, w['w_in_main'], out_dtype=BF16, name=f"{tag}_proj_main")
    pt = _matmul(h, w['w_in_tail'], name=f"{tag}_proj_tail")
    ya = _gmlp_fwd(pm, w['gm_v_norm'], w['gm_w_s'], w['gm_b_full'], f"{tag}_gmlp")
    q, k, v = _mla_pre_fwd(pt, tabs, w['mla_q_norm'], w['mla_kv_norm'], w['wuq'], w['wkv'], w['gq'], w['gk'], f"{tag}_mla_pre")
    o, lse = _attn_fwd(q, k, v, f"{tag}_attn")
    xs = _conv_fwd(pm, C_XS, HP, w['conv_w'][:, :HP], w['conv_b'][:, :HP], f"{tag}_conv_x")
    bc = _conv_fwd(pt, T_BC, BCW, w['conv_w'][:, HP:], w['conv_b'][:, HP:], f"{tag}_conv_bc")
    dtb, dab = _dt_fwd(pt, w['dt_bias'], w['a_log'], f"{tag}_dt")
    ys, s_in = _scan_fwd(xs, bc, dtb, dab, f"{tag}_scan")
    yc = _ssd_post_fwd(ys, xs, pm, w['d_vec'], w['ssd_norm'], f"{tag}_ssd_post")
    mg, y = _merge_fwd(pm, ya, o, yc, w['wb0'], w['wb1'], w['wb2'], w['w_out'], x, f"{tag}_merge")
    return y, (x, h, pm, pt, ya, q, k, v, o, lse, xs, bc, dtb, dab, ys, s_in, yc, mg)


def _mixer_bwd(dy, saved, w, tabs, tag):
    x, h, pm, pt, ya, q, k, v, o, lse, xs, bc, dtb, dab, ys, s_in, yc, mg = saved
    g = {}
    g['w_out'] = _matmul(mg, dy, ta=True, name=f"{tag}_dwout").reshape((N_CHIPS, D_MODEL // N_CHIPS, D_MODEL))
    d0, d1, d2, dgates, dya, do, dyc = _merge_bwd(pm, ya, o, yc, w['wb0'], w['wb1'], w['wb2'], w['w_out'], dy, f"{tag}_dmerge")
    dwb0 = _matmul(ya, d0, ta=True, name=f"{tag}_dwb0")
    dwb1 = _matmul(o, d1, ta=True, name=f"{tag}_dwb1")
    dwb2 = _matmul(yc, d2, ta=True, name=f"{tag}_dwb2")
    g['w_branch'] = _split('w_branch', jnp.stack([dwb0, _heads_unpad(dwb1, MLA_V, 0), _heads_unpad(dwb2, SSD_HEAD_DIM, 0)]))
    duv, g['gm_v_norm'], g['gm_w_s'], db = _gmlp_bwd(pm, w['gm_v_norm'], w['gm_w_s'], w['gm_b_full'], dya, f"{tag}_dgmlp")
    g['gm_b_s'] = db.T
    dq = _attn_bwd_dq(q, k, v, o, lse, do, f"{tag}_dattn_q")
    dk, dv = _attn_bwd_dkv(q, k, v, o, lse, do, f"{tag}_dattn_kv")
    dcq, dckv, dkr, dwuq, dwkv, g['mla_q_norm'], g['mla_kv_norm'], dgq, dgk = _mla_pre_bwd(
        pt, tabs, w['mla_q_norm'], w['mla_kv_norm'], w['wuq'], w['wkv'], w['gq'], w['gk'], dq, dk, dv, f"{tag}_dmla_pre")
    g['mla_w_uq'] = _split('mla_w_uq', _heads_unpad(dwuq, MLA_QK, 1))
    dwk = dwkv[:, :HP].reshape(MLA_KV_RANK, MLA_HEADS, LANES)[:, :, :MLA_NOPE]
    dwv = dwkv[:, HP:].reshape(MLA_KV_RANK, MLA_HEADS, LANES)[:, :, :MLA_V]
    g['mla_w_ukv'] = _split('mla_w_ukv', jnp.concatenate([dwk, dwv], axis=2).reshape(MLA_KV_RANK, MLA_HEADS * (MLA_NOPE + MLA_V)))
    g['mla_q_gain'], g['mla_k_gain'] = dgq[:, :MLA_QK], dgk[:, :MLA_QK]
    dys, dz, dssd_norm, dd = _ssd_post_bwd(ys, xs, pm, w['d_vec'], w['ssd_norm'], dyc, f"{tag}_dssd_post")
    g['ssd_norm'] = _heads_unpad(dssd_norm, SSD_HEAD_DIM, 1)
    g['ssd_d'] = jnp.sum(dd.reshape(SSD_HEADS, LANES), axis=1)[None, :]
    dxs, dbm, dcm, dda, ddtx = _scan_bwd(xs, bc, dtb, dab, s_in, dys, w['d_vec'], f"{tag}_dscan")
    dxs16, dcw_x, dcb_x = _conv_bwd(pm, C_XS, HP, w['conv_w'][:, :HP], w['conv_b'][:, :HP], dxs, f"{tag}_dconv_x")
    dbc16, dcw_bc, dcb_bc = _conv_bwd(pt, T_BC, BCW, w['conv_w'][:, HP:], w['conv_b'][:, HP:],
                                      jnp.concatenate([dbm, dcm], axis=1), f"{tag}_dconv_bc")
    g['ssd_conv_w'] = _xbc_unpad(jnp.concatenate([dcw_x, dcw_bc], axis=1))
    g['ssd_conv_b'] = _xbc_unpad(jnp.concatenate([dcb_x, dcb_bc], axis=1))
    ddt, dbias, dalog = _dt_bwd(pt, w['dt_bias'], w['a_log'], dda, ddtx, f"{tag}_ddt")
    g['ssd_dt_bias'], g['ssd_a_log'] = dbias[:, :SSD_HEADS], dalog[:, :SSD_HEADS]
    s = x.shape[0]
    dpm = jnp.concatenate([duv, dz, dxs16, dgates], axis=1)
    dpt = jnp.concatenate([dbc16, dckv, dcq, dkr, ddt, jnp.zeros((s, PW_TAIL - T_DT - LANES), BF16)], axis=1)
    g['w_in'] = _split('w_in', _w_in_unpad(_matmul(h, dpm, ta=True, name=f"{tag}_dwin_main"),
                                           _matmul(h, dpt, ta=True, name=f"{tag}_dwin_tail")))
    dh = _matmul(dpt, w['w_in_tail'], tb=True, name=f"{tag}_dh_tail")
    dh = _matmul(dpm, w['w_in_main'], tb=True, res=dh, name=f"{tag}_dh_main")
    dx, g['mix_norm'] = _rmsnorm_bwd(x, w['mix_norm'], dh, dy, f"{tag}_dnorm")
    return dx, g


def _local_step(x, positions, target, full, small):
    tabs = _rope_tables(positions)
    ws = [_layer_weights(full, small, l) for l in range(DEPTH)]
    saved = []
    for l, w in enumerate(ws):
        x, s1 = _ffn_fwd(x, w['ffn1_norm'], w['ffn1_w_in'], w['ffn1_w_out'], "ffn1")
        x, s2 = _mixer_fwd(x, w, tabs, "mix")
        x, s3 = _ffn_fwd(x, w['ffn2_norm'], w['ffn2_w_in'], w['ffn2_w_out'], "ffn2")
        saved.append((s1, s2, s3))
    dy, sq = _loss_head(x, target, "loss_head")
    loss = 0.5 * jnp.sum(sq) / D_MODEL
    grads = [None] * DEPTH
    for l in reversed(range(DEPTH)):
        w = ws[l]
        s1, s2, s3 = saved[l]
        dy, dn2, dwi2, dwo2 = _ffn_bwd(dy, s3, w['ffn2_norm'], w['ffn2_w_in'], w['ffn2_w_out'], "ffn2")
        dy, g = _mixer_bwd(dy, s2, w, tabs, "mix")
        dy, dn1, dwi1, dwo1 = _ffn_bwd(dy, s1, w['ffn1_norm'], w['ffn1_w_in'], w['ffn1_w_out'], "ffn1")
        g.update(ffn1_norm=dn1, ffn1_w_in=dwi1, ffn1_w_out=dwo1, ffn2_norm=dn2, ffn2_w_in=dwi2, ffn2_w_out=dwo2)
        grads[l] = g
    return loss, dy, grads


SMALL_PACK = SMALL_ORDER + ['ssd_conv_w']


def _pack_small(per_layer_rows):
    flat = jnp.concatenate([per_layer_rows[l][n].reshape(-1).astype(F32) for l in range(DEPTH) for n in SMALL_PACK])
    rows = -(-flat.shape[0] // LANES)
    rows = -(-rows // 8) * 8
    return jnp.pad(flat, (0, rows * LANES - flat.shape[0])).reshape(rows, LANES)


def _unpack_small(buf, shapes):
    flat = buf.reshape(-1)
    off = 0
    out = {n: [] for n in SMALL_PACK}
    for l in range(DEPTH):
        for n in SMALL_PACK:
            size = int(np.prod(shapes[n]))
            out[n].append(flat[off:off + size].reshape(shapes[n]))
            off += size
    return {n: jnp.stack(v) for n, v in out.items()}


def kernel(x, positions, ffn1_norm, ffn1_w_in, ffn1_w_out, mix_norm, w_in, gm_v_norm, gm_w_s, gm_b_s, mla_q_norm, mla_kv_norm, mla_w_uq, mla_w_ukv, mla_q_gain, mla_k_gain, ssd_conv_w, ssd_conv_b, ssd_dt_bias, ssd_a_log, ssd_d, ssd_norm, w_branch, w_out, ffn2_norm, ffn2_w_in, ffn2_w_out, loss_target, m_ffn1_norm, m_ffn1_w_in, m_ffn1_w_out, m_mix_norm, m_w_in, m_gm_v_norm, m_gm_w_s, m_gm_b_s, m_mla_q_norm, m_mla_kv_norm, m_mla_w_uq, m_mla_w_ukv, m_mla_q_gain, m_mla_k_gain, m_ssd_conv_w, m_ssd_conv_b, m_ssd_dt_bias, m_ssd_a_log, m_ssd_d, m_ssd_norm, m_w_branch, m_w_out, m_ffn2_norm, m_ffn2_w_in, m_ffn2_w_out, v_ffn1_norm, v_ffn1_w_in, v_ffn1_w_out, v_mix_norm, v_w_in, v_gm_v_norm, v_gm_w_s, v_gm_b_s, v_mla_q_norm, v_mla_kv_norm, v_mla_w_uq, v_mla_w_ukv, v_mla_q_gain, v_mla_k_gain, v_ssd_conv_w, v_ssd_conv_b, v_ssd_dt_bias, v_ssd_a_log, v_ssd_d, v_ssd_norm, v_w_branch, v_w_out, v_ffn2_norm, v_ffn2_w_in, v_ffn2_w_out):
    wts = dict(zip(WEIGHTS, (ffn1_norm, ffn1_w_in, ffn1_w_out, mix_norm, w_in, gm_v_norm, gm_w_s, gm_b_s, mla_q_norm, mla_kv_norm,
                             mla_w_uq, mla_w_ukv, mla_q_gain, mla_k_gain, ssd_conv_w, ssd_conv_b, ssd_dt_bias, ssd_a_log, ssd_d,
                             ssd_norm, w_branch, w_out, ffn2_norm, ffn2_w_in, ffn2_w_out)))
    mom = dict(zip(WEIGHTS, (m_ffn1_norm, m_ffn1_w_in, m_ffn1_w_out, m_mix_norm, m_w_in, m_gm_v_norm, m_gm_w_s, m_gm_b_s, m_mla_q_norm,
                             m_mla_kv_norm, m_mla_w_uq, m_mla_w_ukv, m_mla_q_gain, m_mla_k_gain, m_ssd_conv_w, m_ssd_conv_b,
                             m_ssd_dt_bias, m_ssd_a_log, m_ssd_d, m_ssd_norm, m_w_branch, m_w_out, m_ffn2_norm, m_ffn2_w_in,
                             m_ffn2_w_out)))
    var = dict(zip(WEIGHTS, (v_ffn1_norm, v_ffn1_w_in, v_ffn1_w_out, v_mix_norm, v_w_in, v_gm_v_norm, v_gm_w_s, v_gm_b_s, v_mla_q_norm,
                             v_mla_kv_norm, v_mla_w_uq, v_mla_w_ukv, v_mla_q_gain, v_mla_k_gain, v_ssd_conv_w, v_ssd_conv_b,
                             v_ssd_dt_bias, v_ssd_a_log, v_ssd_d, v_ssd_norm, v_w_branch, v_w_out, v_ffn2_norm, v_ffn2_w_in,
                             v_ffn2_w_out)))
    cx, cy, _ = _me()
    mychip = _chip_index(cx, cy)

    gathered = _gather_shards([wts[n].astype(BF16) for n in SHARDED_ORDER])
    full = dict(zip(SHARDED_ORDER, gathered))
    small = {n: wts[n] for n in SMALL_ORDER}

    loss_part, dx, grads = _local_step(x[0], positions[0], loss_target[0], full, small)
    loss = lax.psum(loss_part, ("x", "y", "c"))

    def rows_cols(a, lead):
        return a.reshape(a.shape[:lead] + (int(np.prod(a.shape[lead:-1])), a.shape[-1]))

    gs = [rows_cols(jnp.stack([grads[l][n] for l in range(DEPTH)]), 2) for n in REDUCED]
    got = _pair_exchange(gs)
    sums = [_pair_add(g, r, f"pair_add_{n}") for g, r, n in zip(gs, got, REDUCED)]
    arrived = _chip_exchange([s16 for s16, _ in sums])
    halves = [_chip_add(own, a, f"chip_add_{n}") for (_, own), a, n in zip(sums, arrived, REDUCED)]
    reduced = dict(zip(REDUCED, _pair_share(halves)))
    shapes = {n: wts[n].shape[1:] for n in SMALL_ORDER}
    shapes['ssd_conv_w'] = SHARDED['ssd_conv_w'][0]
    small_g = _unpack_small(_sum_slots(_all_exchange(_pack_small(grads)), "small_sum"), shapes)
    conv_full = small_g.pop('ssd_conv_w')
    shard_cols = _shard_shape('ssd_conv_w')[1]
    small_g['ssd_conv_w'] = lax.dynamic_slice_in_dim(conv_full, mychip * shard_cols, shard_cols, axis=2)
    shapes['ssd_conv_w'] = _shard_shape('ssd_conv_w')

    grad, delta, new_m, new_v = {}, {}, {}, {}
    for n in REDUCED:
        shp = wts[n].shape
        grad[n] = reduced[n].reshape(shp)
        two = (int(np.prod(shp[:-1])), shp[-1])
        d, nm, nv = _adamw(wts[n].reshape(two), reduced[n].reshape(two), mom[n].reshape(two), var[n].reshape(two), f"adamw_{n}")
        delta[n], new_m[n], new_v[n] = d.reshape(shp), nm.reshape(shp), nv.reshape(shp)
    per_layer = lambda t: [{n: t[n][l] for n in SMALL_PACK} for l in range(DEPTH)]
    d, nm, nv = _adamw(_pack_small(per_layer(wts)), _pack_small(per_layer(small_g)), _pack_small(per_layer(mom)),
                       _pack_small(per_layer(var)), "adamw_small")
    sd, snm, snv = _unpack_small(d, shapes), _unpack_small(nm, shapes), _unpack_small(nv, shapes)
    for n in SMALL_PACK:
        grad[n], delta[n], new_m[n], new_v[n] = small_g[n], sd[n], snm[n], snv[n]
    return (loss, dx[None], *[grad[n] for n in WEIGHTS], *[delta[n] for n in WEIGHTS], *[new_m[n] for n in WEIGHTS],
            *[new_v[n] for n in WEIGHTS])
```

```python
import functools
import math

import numpy as np
import jax
import jax.numpy as jnp
from jax import lax
from jax.experimental import pallas as pl
from jax.experimental.pallas import tpu as pltpu

F32, BF16 = jnp.float32, jnp.bfloat16
MESH = pl.DeviceIdType.MESH

D_MODEL, DEPTH, D_FF, EPS = 1024, 4, 2816, 1e-6
GM_WIDTH, GM_GROUPS, CHUNK = 512, 4, 128
MLA_HEADS, MLA_Q_RANK, MLA_KV_RANK, MLA_NOPE, MLA_ROPE, MLA_V = 8, 384, 256, 64, 32, 64
MLA_QK = MLA_NOPE + MLA_ROPE
ROPE_THETA = 10000.0
SSD_HEADS, SSD_HEAD_DIM, SSD_GROUPS, SSD_STATE, SSD_CONV = 8, 64, 2, 128, 4
SSD_INNER = SSD_HEADS * SSD_HEAD_DIM
IN_COLS = 6312
LANES = 128
ADAM_LR, ADAM_B1, ADAM_B2, ADAM_EPS, ADAM_WD, ADAM_STEP = 0.001, 0.9, 0.999, 1e-08, 0.01, 10

C_UV, C_Z, C_XS, C_G, PW_MAIN = 0, 1024, 2048, 3072, 6144
T_BC, T_CKV, T_CQ, T_KR, T_DT, PW_TAIL = 0, 512, 768, 1152, 1280, 1536
HP = MLA_HEADS * LANES
FC = 2 * D_FF // 4

WEIGHTS = ['ffn1_norm', 'ffn1_w_in', 'ffn1_w_out', 'mix_norm', 'w_in', 'gm_v_norm', 'gm_w_s', 'gm_b_s', 'mla_q_norm',
           'mla_kv_norm', 'mla_w_uq', 'mla_w_ukv', 'mla_q_gain', 'mla_k_gain', 'ssd_conv_w', 'ssd_conv_b', 'ssd_dt_bias',
           'ssd_a_log', 'ssd_d', 'ssd_norm', 'w_branch', 'w_out', 'ffn2_norm', 'ffn2_w_in', 'ffn2_w_out']
SHARDED = {'ffn1_w_in': ((1024, 5632), 1), 'ffn1_w_out': ((2816, 1024), 0), 'w_in': ((1024, 6312), 1),
           'mla_w_uq': ((384, 768), 1), 'mla_w_ukv': ((256, 1024), 1), 'ssd_conv_w': ((4, 1024), 1),
           'w_branch': ((3, 512, 1024), 2), 'w_out': ((1024, 1024), 0), 'ffn2_w_in': ((1024, 5632), 1),
           'ffn2_w_out': ((2816, 1024), 0)}
SHARDED_ORDER = [n for n in WEIGHTS if n in SHARDED]
SMALL_ORDER = [n for n in WEIGHTS if n not in SHARDED]
REDUCED = [n for n in SHARDED_ORDER if n != 'ssd_conv_w']
N_CHIPS = 4
HALF_L = DEPTH // 2


def _shard_shape(name):
    shape, ax = SHARDED[name]
    return tuple(d // N_CHIPS if i == ax else d for i, d in enumerate(shape))


def _pick(dim, target):
    if dim <= target:
        return dim
    t = (target // LANES) * LANES
    while t >= LANES:
        if dim % t == 0:
            return t
        t -= LANES
    return dim


def _sigmoid(x):
    return 1.0 / (1.0 + jnp.exp(-x))


def _params(*sem):
    return pltpu.CompilerParams(dimension_semantics=sem, vmem_limit_bytes=56 * 1024 * 1024)


def _matmul(a, b, *, ta=False, tb=False, out_dtype=F32, scale=1.0, res=None, name):
    if ta:
        k_dim, m_dim = a.shape
    else:
        m_dim, k_dim = a.shape
    if tb:
        n_dim, k2 = b.shape
    else:
        k2, n_dim = b.shape
    assert k_dim == k2, (a.shape, b.shape, ta, tb)
    tm, tn, tk = _pick(m_dim, 1024), _pick(n_dim, 1024), _pick(k_dim, 1024)
    nk = k_dim // tk
    dn = (((0 if ta else 1,), (1 if tb else 0,)), ((), ()))

    def body(*refs):
        if res is not None:
            a_ref, b_ref, r_ref, o_ref, acc = refs
        else:
            a_ref, b_ref, o_ref, acc = refs
        k = pl.program_id(2)

        @pl.when(k == 0)
        def _():
            acc[...] = jnp.zeros_like(acc)

        acc[...] += lax.dot_general(a_ref[...].astype(BF16), b_ref[...].astype(BF16), dn, preferred_element_type=F32)

        @pl.when(k == nk - 1)
        def _():
            r = acc[...]
            if scale != 1.0:
                r = r * scale
            if res is not None:
                r = r + r_ref[...]
            o_ref[...] = r.astype(out_dtype)

    a_spec = pl.BlockSpec((tk, tm), lambda j, i, k: (k, i)) if ta else pl.BlockSpec((tm, tk), lambda j, i, k: (i, k))
    b_spec = pl.BlockSpec((tn, tk), lambda j, i, k: (j, k)) if tb else pl.BlockSpec((tk, tn), lambda j, i, k: (k, j))
    in_specs = [a_spec, b_spec]
    args = [a, b]
    if res is not None:
        in_specs.append(pl.BlockSpec((tm, tn), lambda j, i, k: (i, j)))
        args.append(res)
    return pl.pallas_call(
        body, name=name, grid=(n_dim // tn, m_dim // tm, nk), in_specs=in_specs,
        out_specs=pl.BlockSpec((tm, tn), lambda j, i, k: (i, j)),
        out_shape=jax.ShapeDtypeStruct((m_dim, n_dim), out_dtype),
        scratch_shapes=[pltpu.VMEM((tm, tn), F32)],
        compiler_params=_params("parallel", "parallel", "arbitrary"))(*args)


def _rmsnorm_fwd(x, gain, name):
    s, d = x.shape
    tm = _pick(s, 512)

    def body(x_ref, g_ref, o_ref):
        xv = x_ref[...]
        r = lax.rsqrt(jnp.mean(xv * xv, axis=-1, keepdims=True) + EPS)
        o_ref[...] = (xv * r * g_ref[...]).astype(BF16)

    return pl.pallas_call(
        body, name=name, grid=(s // tm,),
        in_specs=[pl.BlockSpec((tm, d), lambda i: (i, 0)), pl.BlockSpec((1, d), lambda i: (0, 0))],
        out_specs=pl.BlockSpec((tm, d), lambda i: (i, 0)),
        out_shape=jax.ShapeDtypeStruct((s, d), BF16), compiler_params=_params("parallel"))(x, gain)


def _rmsnorm_bwd(x, gain, dh, dres, name):
    s, d = x.shape
    tm = _pick(s, 512)

    def body(x_ref, g_ref, dh_ref, dr_ref, dx_ref, dg_ref):
        @pl.when(pl.program_id(0) == 0)
        def _():
            dg_ref[...] = jnp.zeros_like(dg_ref)

        xv, dhv = x_ref[...], dh_ref[...]
        r = lax.rsqrt(jnp.mean(xv * xv, axis=-1, keepdims=True) + EPS)
        u = dhv * g_ref[...]
        dx_ref[...] = dr_ref[...] + r * u - xv * (r * r * r) * jnp.mean(xv * u, axis=-1, keepdims=True)
        dg_ref[...] += jnp.sum(dhv * xv * r, axis=0, keepdims=True)

    row = pl.BlockSpec((tm, d), lambda i: (i, 0))
    vec = pl.BlockSpec((1, d), lambda i: (0, 0))
    return pl.pallas_call(
        body, name=name, grid=(s // tm,), in_specs=[row, vec, row, row], out_specs=[row, vec],
        out_shape=[jax.ShapeDtypeStruct((s, d), F32), jax.ShapeDtypeStruct((1, d), F32)],
        compiler_params=_params("arbitrary"))(x, gain, dh, dres)


_NT = (((1,), (1,)), ((), ()))
_TN = (((0,), (0,)), ((), ()))


def _resident(shape):
    return pl.BlockSpec(shape, lambda *_: tuple(0 for _ in shape), pipeline_mode=pl.Buffered(1))


def _ffn_in(x, gain, w4, name):
    s, d = x.shape
    tm = _pick(s, 512)

    def body(x_ref, g_ref, w_ref, h_ref, gate_ref, up_ref, act_ref):
        xv = x_ref[...]
        r = lax.rsqrt(jnp.mean(xv * xv, axis=-1, keepdims=True) + EPS)
        h = (xv * r * g_ref[...]).astype(BF16)
        h_ref[...] = h
        for j in range(2):
            g16 = jnp.dot(h, w_ref[j], preferred_element_type=F32).astype(BF16)
            u16 = jnp.dot(h, w_ref[j + 2], preferred_element_type=F32).astype(BF16)
            gate_ref[j] = g16
            up_ref[j] = u16
            gf, uf = g16.astype(F32), u16.astype(F32)
            act_ref[j] = (gf * _sigmoid(gf) * uf).astype(BF16)

    half = pl.BlockSpec((2, tm, FC), lambda i: (0, i, 0))
    return pl.pallas_call(
        body, name=name, grid=(s // tm,),
        in_specs=[pl.BlockSpec((tm, d), lambda i: (i, 0)), pl.BlockSpec((1, d), lambda i: (0, 0)), _resident((4, d, FC))],
        out_specs=[pl.BlockSpec((tm, d), lambda i: (i, 0)), half, half, half],
        out_shape=[jax.ShapeDtypeStruct((s, d), BF16)] + [jax.ShapeDtypeStruct((2, s, FC), BF16)] * 3,
        compiler_params=_params("parallel"))(x, gain, w4)


def _ffn_out(act, w_out, x, name):
    s, d = x.shape
    tm = _pick(s, 512)

    def body(a_ref, w_ref, x_ref, o_ref):
        acc = jnp.dot(a_ref[0], w_ref[0:FC, :], preferred_element_type=F32)
        acc = acc + jnp.dot(a_ref[1], w_ref[FC:2 * FC, :], preferred_element_type=F32)
        o_ref[...] = x_ref[...] + 0.5 * acc

    row = pl.BlockSpec((tm, d), lambda i: (i, 0))
    return pl.pallas_call(
        body, name=name, grid=(s // tm,),
        in_specs=[pl.BlockSpec((2, tm, FC), lambda i: (0, i, 0)), _resident((2 * FC, d)), row], out_specs=row,
        out_shape=jax.ShapeDtypeStruct((s, d), F32), compiler_params=_params("parallel"))(act, w_out, x)


def _ffn_dact(dy, w_out, gate, up, name):
    s, d = dy.shape
    tm = _pick(s, 512)

    def body(dy_ref, w_ref, g_ref, u_ref, o_ref):
        dy16 = dy_ref[...].astype(BF16)
        for j in range(2):
            dact = 0.5 * lax.dot_general(dy16, w_ref[j * FC:(j + 1) * FC, :], _NT, preferred_element_type=F32)
            g, u = g_ref[j].astype(F32), u_ref[j].astype(F32)
            sg = _sigmoid(g)
            o_ref[j] = (dact * u * (sg * (1.0 + g * (1.0 - sg)))).astype(BF16)
            o_ref[j + 2] = (dact * g * sg).astype(BF16)

    half = pl.BlockSpec((2, tm, FC), lambda i: (0, i, 0))
    return pl.pallas_call(
        body, name=name, grid=(s // tm,),
        in_specs=[pl.BlockSpec((tm, d), lambda i: (i, 0)), _resident((2 * FC, d)), half, half],
        out_specs=pl.BlockSpec((4, tm, FC), lambda i: (0, i, 0)),
        out_shape=jax.ShapeDtypeStruct((4, s, FC), BF16), compiler_params=_params("parallel"))(dy, w_out, gate, up)


def _ffn_dwout(act, dy, name):
    s, d = dy.shape
    tk = _pick(s, 1024)
    nk = s // tk

    def body(a_ref, dy_ref, o_ref):
        k = pl.program_id(1)

        @pl.when(k == 0)
        def _():
            o_ref[...] = jnp.zeros_like(o_ref)

        o_ref[...] += lax.dot_general(a_ref[...], dy_ref[...].astype(BF16), _TN, preferred_element_type=F32)

        @pl.when(k == nk - 1)
        def _():
            o_ref[...] = 0.5 * o_ref[...]

    return pl.pallas_call(
        body, name=name, grid=(2, nk),
        in_specs=[pl.BlockSpec((None, tk, FC), lambda j, k: (j, k, 0)), pl.BlockSpec((tk, d), lambda j, k: (k, 0))],
        out_specs=pl.BlockSpec((FC, d), lambda j, k: (j, 0)), out_shape=jax.ShapeDtypeStruct((2 * FC, d), F32),
        compiler_params=_params("parallel", "arbitrary"))(act, dy)


def _ffn_dwin(h, da, name):
    s, d = h.shape
    tk = _pick(s, 1024)

    def body(h_ref, da_ref, o_ref):
        @pl.when(pl.program_id(1) == 0)
        def _():
            o_ref[...] = jnp.zeros_like(o_ref)

        o_ref[...] += lax.dot_general(h_ref[...], da_ref[...], _TN, preferred_element_type=F32)

    return pl.pallas_call(
        body, name=name, grid=(4, s // tk),
        in_specs=[pl.BlockSpec((tk, d), lambda j, k: (k, 0)), pl.BlockSpec((None, tk, FC), lambda j, k: (j, k, 0))],
        out_specs=pl.BlockSpec((None, d, FC), lambda j, k: (j, 0, 0)), out_shape=jax.ShapeDtypeStruct((4, d, FC), F32),
        compiler_params=_params("parallel", "arbitrary"))(h, da)


def _ffn_dx(da, w4, x, gain, dy, name):
    s, d = x.shape
    tm = _pick(s, 512)

    def body(da_ref, w_ref, x_ref, g_ref, dy_ref, dx_ref, dg_ref):
        @pl.when(pl.program_id(0) == 0)
        def _():
            dg_ref[...] = jnp.zeros_like(dg_ref)

        dh = jnp.zeros((tm, d), F32)
        for j in range(4):
            dh = dh + lax.dot_general(da_ref[j], w_ref[j], _NT, preferred_element_type=F32)
        xv = x_ref[...]
        r = lax.rsqrt(jnp.mean(xv * xv, axis=-1, keepdims=True) + EPS)
        u = dh * g_ref[...]
        dx_ref[...] = dy_ref[...] + r * u - xv * (r * r * r) * jnp.mean(xv * u, axis=-1, keepdims=True)
        dg_ref[...] += jnp.sum(dh * xv * r, axis=0, keepdims=True)

    row = pl.BlockSpec((tm, d), lambda i: (i, 0))
    vec = pl.BlockSpec((1, d), lambda i: (0, 0))
    return pl.pallas_call(
        body, name=name, grid=(s // tm,),
        in_specs=[pl.BlockSpec((4, tm, FC), lambda i: (0, i, 0)), _resident((4, d, FC)), row, vec, row],
        out_specs=[row, vec], out_shape=[jax.ShapeDtypeStruct((s, d), F32), jax.ShapeDtypeStruct((1, d), F32)],
        compiler_params=_params("arbitrary"))(da, w4, x, gain, dy)


_INV_SQRT2 = 0.7071067811865476
_INV_SQRT2PI = 0.3989422804014327


def _gelu(x):
    return 0.5 * x * (1.0 + lax.erf(x * _INV_SQRT2))


def _gelu_grad(x):
    return 0.5 * (1.0 + lax.erf(x * _INV_SQRT2)) + x * jnp.exp(-0.5 * x * x) * _INV_SQRT2PI


def _tril_mask():
    r = lax.broadcasted_iota(jnp.int32, (CHUNK, CHUNK), 0)
    c = lax.broadcasted_iota(jnp.int32, (CHUNK, CHUNK), 1)
    return r >= c


def _gmlp_fwd(p, v_gain, w_s, b_full, name):
    s = p.shape[0]
    tm = _pick(s, 512)
    nch = tm // CHUNK

    def body(uv_ref, g_ref, w_ref, b_ref, o_ref):
        gel = _gelu(uv_ref[...].astype(F32))
        u, v = gel[:, :GM_WIDTH], gel[:, GM_WIDTH:]
        r = lax.rsqrt(jnp.mean(v * v, axis=-1, keepdims=True) + EPS)
        vn = (v * r * g_ref[...]).astype(BF16)
        mask = _tril_mask()
        for g in range(GM_GROUPS):
            wm = jnp.where(mask, w_ref[g], 0.0).astype(BF16)
            for c in range(nch):
                rs, cs = slice(c * CHUNK, (c + 1) * CHUNK), slice(g * LANES, (g + 1) * LANES)
                sp = jnp.dot(wm, vn[rs, cs], preferred_element_type=F32) + b_ref[g]
                o_ref[rs, cs] = (u[rs, cs] * sp).astype(BF16)

    full3 = pl.BlockSpec((GM_GROUPS, CHUNK, CHUNK), lambda i: (0, 0, 0))
    return pl.pallas_call(
        body, name=name, grid=(s // tm,),
        in_specs=[pl.BlockSpec((tm, 2 * GM_WIDTH), lambda i: (i, C_UV // (2 * GM_WIDTH))),
                  pl.BlockSpec((1, GM_WIDTH), lambda i: (0, 0)), full3, full3],
        out_specs=pl.BlockSpec((tm, GM_WIDTH), lambda i: (i, 0)),
        out_shape=jax.ShapeDtypeStruct((s, GM_WIDTH), BF16), compiler_params=_params("parallel"))(p, v_gain, w_s, b_full)


def _gmlp_bwd(p, v_gain, w_s, b_full, dy, name):
    s = p.shape[0]
    tm = _pick(s, 512)
    nch = tm // CHUNK
    nsteps = s // tm

    def body(uv_ref, g_ref, w_ref, b_ref, dy_ref, duv_ref, dg_ref, dw_ref, db_ref, dvn_s, dbacc):
        step = pl.program_id(0)

        @pl.when(step == 0)
        def _():
            dg_ref[...] = jnp.zeros_like(dg_ref)
            dw_ref[...] = jnp.zeros_like(dw_ref)
            dbacc[...] = jnp.zeros_like(dbacc)

        uv = uv_ref[...].astype(F32)
        gel = _gelu(uv)
        u, v = gel[:, :GM_WIDTH], gel[:, GM_WIDTH:]
        r = lax.rsqrt(jnp.mean(v * v, axis=-1, keepdims=True) + EPS)
        gain = g_ref[...]
        vn32 = v * r * gain
        vn = vn32.astype(BF16)
        dy = dy_ref[...]
        mask = _tril_mask()
        for g in range(GM_GROUPS):
            wm = jnp.where(mask, w_ref[g], 0.0).astype(BF16)
            dwg = jnp.zeros((CHUNK, CHUNK), F32)
            dbg = jnp.zeros((CHUNK, LANES), F32)
            for c in range(nch):
                rs, cs = slice(c * CHUNK, (c + 1) * CHUNK), slice(g * LANES, (g + 1) * LANES)
                sp = jnp.dot(wm, vn[rs, cs], preferred_element_type=F32) + b_ref[g]
                dyc = dy[rs, cs]
                dsp = dyc * u[rs, cs]
                dsp16 = dsp.astype(BF16)
                duv_ref[rs, cs] = (dyc * sp * _gelu_grad(uv[rs, cs])).astype(BF16)
                dvn_s[rs, cs] = lax.dot_general(wm, dsp16, (((0,), (0,)), ((), ())), preferred_element_type=F32)
                dwg = dwg + lax.dot_general(dsp16, vn[rs, cs], (((1,), (1,)), ((), ())), preferred_element_type=F32)
                dbg = dbg + dsp
            dw_ref[g] += jnp.where(mask, dwg, 0.0)
            dbacc[:, g * LANES:(g + 1) * LANES] += dbg
        dvn = dvn_s[...]
        uu = dvn * gain
        dv = r * uu - v * (r * r * r) * jnp.mean(v * uu, axis=-1, keepdims=True)
        duv_ref[:, GM_WIDTH:] = (dv * _gelu_grad(uv[:, GM_WIDTH:])).astype(BF16)
        dg_ref[...] += jnp.sum(dvn * v * r, axis=0, keepdims=True)

        @pl.when(step == nsteps - 1)
        def _():
            for g in range(GM_GROUPS):
                db_ref[:, g:g + 1] = jnp.sum(dbacc[:, g * LANES:(g + 1) * LANES], axis=1, keepdims=True)

    full3 = pl.BlockSpec((GM_GROUPS, CHUNK, CHUNK), lambda i: (0, 0, 0))
    return pl.pallas_call(
        body, name=name, grid=(nsteps,),
        in_specs=[pl.BlockSpec((tm, 2 * GM_WIDTH), lambda i: (i, C_UV // (2 * GM_WIDTH))),
                  pl.BlockSpec((1, GM_WIDTH), lambda i: (0, 0)), full3, full3,
                  pl.BlockSpec((tm, GM_WIDTH), lambda i: (i, 0))],
        out_specs=[pl.BlockSpec((tm, 2 * GM_WIDTH), lambda i: (i, 0)), pl.BlockSpec((1, GM_WIDTH), lambda i: (0, 0)),
                   full3, pl.BlockSpec((CHUNK, GM_GROUPS), lambda i: (0, 0))],
        out_shape=[jax.ShapeDtypeStruct((s, 2 * GM_WIDTH), BF16), jax.ShapeDtypeStruct((1, GM_WIDTH), F32),
                   jax.ShapeDtypeStruct((GM_GROUPS, CHUNK, CHUNK), F32), jax.ShapeDtypeStruct((CHUNK, GM_GROUPS), F32)],
        scratch_shapes=[pltpu.VMEM((tm, GM_WIDTH), F32), pltpu.VMEM((CHUNK, GM_WIDTH), F32)],
        compiler_params=_params("arbitrary"))(p, v_gain, w_s, b_full, dy)


def _rope(x, ct, s1, s2):
    return x * ct + pltpu.roll(x, LANES - MLA_ROPE // 2, 1) * s1 + pltpu.roll(x, MLA_ROPE // 2, 1) * s2


def _rope_bwd(d, ct, s1, s2):
    return d * ct + pltpu.roll(d * s1, MLA_ROPE // 2, 1) + pltpu.roll(d * s2, LANES - MLA_ROPE // 2, 1)


def _head_norm(x, gain):
    r = lax.rsqrt(jnp.sum(x * x, axis=-1, keepdims=True) * (1.0 / MLA_QK) + EPS)
    return x * r * gain, r


def _head_norm_bwd(x, r, gain, d):
    u = d * gain
    return r * u - x * (r * r * r) * (jnp.sum(x * u, axis=-1, keepdims=True) * (1.0 / MLA_QK))


def _mla_specs(tm):
    cq = pl.BlockSpec((tm, MLA_Q_RANK), lambda i: (i, T_CQ // MLA_Q_RANK))
    ckv = pl.BlockSpec((tm, MLA_KV_RANK), lambda i: (i, T_CKV // MLA_KV_RANK))
    kr = pl.BlockSpec((tm, LANES), lambda i: (i, T_KR // LANES))
    tab = pl.BlockSpec((tm, LANES), lambda i: (i, 0))
    return cq, ckv, kr, tab


def _const(shape):
    return pl.BlockSpec(shape, lambda i: tuple(0 for _ in shape))


def _mla_pre_fwd(p, tabs, qn_g, kvn_g, wuq, wkv, gq, gk, name):
    s = p.shape[0]
    tm = _pick(s, 256)
    ct, s1, s2 = tabs

    def body(cq_ref, ckv_ref, kr_ref, ct_ref, s1_ref, s2_ref, qg_ref, kvg_ref, wuq_ref, wkv_ref, gq_ref, gk_ref,
             q_ref, k_ref, v_ref):
        cq, ckv, kr = cq_ref[...], ckv_ref[...], kr_ref[...]
        ctv, s1v, s2v = ct_ref[...], s1_ref[...], s2_ref[...]
        rq = lax.rsqrt(jnp.mean(cq * cq, axis=-1, keepdims=True) + EPS)
        q = jnp.dot((cq * rq * qg_ref[...]).astype(BF16), wuq_ref[...], preferred_element_type=F32)
        rk = lax.rsqrt(jnp.mean(ckv * ckv, axis=-1, keepdims=True) + EPS)
        kv = jnp.dot((ckv * rk * kvg_ref[...]).astype(BF16), wkv_ref[...], preferred_element_type=F32)
        v_ref[...] = kv[:, HP:].astype(BF16)
        for h in range(MLA_HEADS):
            hs = slice(h * LANES, (h + 1) * LANES)
            qh, _ = _head_norm(q[:, hs], gq_ref[...])
            q_ref[:, hs] = (_rope(qh, ctv, s1v, s2v) * _ATT_SCALE).astype(BF16)
            kh, _ = _head_norm(kv[:, hs] + kr, gk_ref[...])
            k_ref[:, hs] = _rope(kh, ctv, s1v, s2v).astype(BF16)

    cq_s, ckv_s, kr_s, tab_s = _mla_specs(tm)
    out = pl.BlockSpec((tm, HP), lambda i: (i, 0))
    return pl.pallas_call(
        body, name=name, grid=(s // tm,),
        in_specs=[cq_s, ckv_s, kr_s, tab_s, tab_s, tab_s, _const((1, MLA_Q_RANK)), _const((1, MLA_KV_RANK)),
                  _const((MLA_Q_RANK, HP)), _const((MLA_KV_RANK, 2 * HP)), _const((1, LANES)), _const((1, LANES))],
        out_specs=[out, out, out], out_shape=[jax.ShapeDtypeStruct((s, HP), BF16)] * 3,
        compiler_params=_params("parallel"))(p, p, p, ct, s1, s2, qn_g, kvn_g, wuq, wkv, gq, gk)


def _mla_pre_bwd(p, tabs, qn_g, kvn_g, wuq, wkv, gq, gk, dq, dk, dv, name):
    s = p.shape[0]
    tm = _pick(s, 256)
    ct, s1, s2 = tabs

    def body(cq_ref, ckv_ref, kr_ref, ct_ref, s1_ref, s2_ref, qg_ref, kvg_ref, wuq_ref, wkv_ref, gq_ref, gk_ref,
             dq_ref, dk_ref, dv_ref, dcq_ref, dckv_ref, dkr_ref, dwuq_ref, dwkv_ref, dqg_ref, dkvg_ref, dgq_ref, dgk_ref,
             dqp, dkvp):
        @pl.when(pl.program_id(0) == 0)
        def _():
            for ref in (dwuq_ref, dwkv_ref, dqg_ref, dkvg_ref, dgq_ref, dgk_ref):
                ref[...] = jnp.zeros_like(ref)

        cq, ckv, kr = cq_ref[...], ckv_ref[...], kr_ref[...]
        ctv, s1v, s2v = ct_ref[...], s1_ref[...], s2_ref[...]
        rq = lax.rsqrt(jnp.mean(cq * cq, axis=-1, keepdims=True) + EPS)
        qn = (cq * rq * qg_ref[...]).astype(BF16)
        q = jnp.dot(qn, wuq_ref[...], preferred_element_type=F32)
        rk = lax.rsqrt(jnp.mean(ckv * ckv, axis=-1, keepdims=True) + EPS)
        kvn = (ckv * rk * kvg_ref[...]).astype(BF16)
        kv = jnp.dot(kvn, wkv_ref[...], preferred_element_type=F32)
        gqv, gkv = gq_ref[...], gk_ref[...]
        dgq = jnp.zeros((1, LANES), F32)
        dgk = jnp.zeros((1, LANES), F32)
        dkr = jnp.zeros((tm, LANES), F32)
        for h in range(MLA_HEADS):
            hs = slice(h * LANES, (h + 1) * LANES)
            xq = q[:, hs]
            _, r = _head_norm(xq, gqv)
            d = _rope_bwd(dq_ref[:, hs], ctv, s1v, s2v)
            dgq = dgq + jnp.sum(d * xq * r, axis=0, keepdims=True)
            dqp[:, hs] = _head_norm_bwd(xq, r, gqv, d)
            xk = kv[:, hs] + kr
            _, r = _head_norm(xk, gkv)
            d = _rope_bwd(dk_ref[:, hs], ctv, s1v, s2v)
            dgk = dgk + jnp.sum(d * xk * r, axis=0, keepdims=True)
            dxk = _head_norm_bwd(xk, r, gkv, d)
            dkvp[:, hs] = dxk
            dkr = dkr + dxk
        dkvp[:, HP:] = dv_ref[...]
        dgq_ref[...] += dgq
        dgk_ref[...] += dgk
        dkr_ref[...] = dkr.astype(BF16)
        tn = (((0,), (0,)), ((), ()))
        nt = (((1,), (1,)), ((), ()))
        dq16 = dqp[...].astype(BF16)
        dwuq_ref[...] += lax.dot_general(qn, dq16, tn, preferred_element_type=F32)
        dqn = lax.dot_general(dq16, wuq_ref[...], nt, preferred_element_type=F32)
        dqg_ref[...] += jnp.sum(dqn * cq * rq, axis=0, keepdims=True)
        u = dqn * qg_ref[...]
        dcq_ref[...] = (rq * u - cq * (rq * rq * rq) * jnp.mean(cq * u, axis=-1, keepdims=True)).astype(BF16)
        dkv16 = dkvp[...].astype(BF16)
        dwkv_ref[...] += lax.dot_general(kvn, dkv16, tn, preferred_element_type=F32)
        dkvn = lax.dot_general(dkv16, wkv_ref[...], nt, preferred_element_type=F32)
        dkvg_ref[...] += jnp.sum(dkvn * ckv * rk, axis=0, keepdims=True)
        u = dkvn * kvg_ref[...]
        dckv_ref[...] = (rk * u - ckv * (rk * rk * rk) * jnp.mean(ckv * u, axis=-1, keepdims=True)).astype(BF16)

    cq_s, ckv_s, kr_s, tab_s = _mla_specs(tm)
    hd = pl.BlockSpec((tm, HP), lambda i: (i, 0))
    return pl.pallas_call(
        body, name=name, grid=(s // tm,),
        in_specs=[cq_s, ckv_s, kr_s, tab_s, tab_s, tab_s, _const((1, MLA_Q_RANK)), _const((1, MLA_KV_RANK)),
                  _const((MLA_Q_RANK, HP)), _const((MLA_KV_RANK, 2 * HP)), _const((1, LANES)), _const((1, LANES)),
                  hd, hd, hd],
        out_specs=[pl.BlockSpec((tm, MLA_Q_RANK), lambda i: (i, 0)), pl.BlockSpec((tm, MLA_KV_RANK), lambda i: (i, 0)),
                   pl.BlockSpec((tm, LANES), lambda i: (i, 0)), _const((MLA_Q_RANK, HP)), _const((MLA_KV_RANK, 2 * HP)),
                   _const((1, MLA_Q_RANK)), _const((1, MLA_KV_RANK)), _const((1, LANES)), _const((1, LANES))],
        out_shape=[jax.ShapeDtypeStruct((s, MLA_Q_RANK), BF16), jax.ShapeDtypeStruct((s, MLA_KV_RANK), BF16),
                   jax.ShapeDtypeStruct((s, LANES), BF16), jax.ShapeDtypeStruct((MLA_Q_RANK, HP), F32),
                   jax.ShapeDtypeStruct((MLA_KV_RANK, 2 * HP), F32), jax.ShapeDtypeStruct((1, MLA_Q_RANK), F32),
                   jax.ShapeDtypeStruct((1, MLA_KV_RANK), F32), jax.ShapeDtypeStruct((1, LANES), F32),
                   jax.ShapeDtypeStruct((1, LANES), F32)],
        scratch_shapes=[pltpu.VMEM((tm, HP), F32), pltpu.VMEM((tm, 2 * HP), F32)],
        compiler_params=_params("arbitrary"))(p, p, p, ct, s1, s2, qn_g, kvn_g, wuq, wkv, gq, gk, dq, dk, dv)


_ATT_SCALE = MLA_QK ** -0.5
_NEG = -1e30
_NT = (((1,), (1,)), ((), ()))
_TN = (((0,), (0,)), ((), ()))


def _tri_rows(step, n):
    i = step * 0
    for m in range(1, n):
        i = i + (step >= m * (m + 1) // 2).astype(jnp.int32)
    return i, step - i * (i + 1) // 2


def _tri_cols(step, n):
    j = step * 0
    for m in range(1, n):
        j = j + (step >= m * n - m * (m - 1) // 2).astype(jnp.int32)
    return j, j + step - (j * n - j * (j - 1) // 2)


def _diag_mask(t):
    return lax.broadcasted_iota(jnp.int32, (t, t), 0) >= lax.broadcasted_iota(jnp.int32, (t, t), 1)


def _attn_fwd(q, k, v, name):
    s = q.shape[0]
    t = _pick(s, 512)
    n = s // t

    def body(q_ref, k_ref, v_ref, o_ref, lse_ref, m_s, l_s, acc):
        i, j = _tri_rows(pl.program_id(1), n)

        @pl.when(j == 0)
        def _():
            m_s[...] = jnp.full_like(m_s, _NEG)
            l_s[...] = jnp.zeros_like(l_s)
            acc[...] = jnp.zeros_like(acc)

        def step(diagonal):
            sc = lax.dot_general(q_ref[...], k_ref[...], _NT, preferred_element_type=F32)
            if diagonal:
                sc = jnp.where(_diag_mask(t), sc, _NEG)
            m_new = jnp.maximum(m_s[...], jnp.max(sc, axis=-1, keepdims=True))
            alpha = jnp.exp(m_s[...] - m_new)
            pr = jnp.exp(sc - m_new)
            l_s[...] = alpha * l_s[...] + jnp.sum(pr, axis=-1, keepdims=True)
            acc[...] = alpha * acc[...] + jnp.dot(pr.astype(BF16), v_ref[...], preferred_element_type=F32)
            m_s[...] = m_new

        @pl.when(j < i)
        def _():
            step(False)

        @pl.when(j == i)
        def _():
            step(True)
            o_ref[...] = acc[...] / l_s[...]
            lse_ref[...] = m_s[...] + jnp.log(l_s[...])

    qs = pl.BlockSpec((t, LANES), lambda h, p: (_tri_rows(p, n)[0], h))
    ks = pl.BlockSpec((t, LANES), lambda h, p: (_tri_rows(p, n)[1], h))
    return pl.pallas_call(
        body, name=name, grid=(MLA_HEADS, n * (n + 1) // 2), in_specs=[qs, ks, ks],
        out_specs=[qs, pl.BlockSpec((None, t, 1), lambda h, p: (h, _tri_rows(p, n)[0], 0))],
        out_shape=[jax.ShapeDtypeStruct((s, HP), F32), jax.ShapeDtypeStruct((MLA_HEADS, s, 1), F32)],
        scratch_shapes=[pltpu.VMEM((t, 1), F32), pltpu.VMEM((t, 1), F32), pltpu.VMEM((t, LANES), F32)],
        compiler_params=_params("parallel", "arbitrary"))(q, k, v)


def _attn_bwd_dq(q, k, v, o, lse, do, name):
    s = q.shape[0]
    t = _pick(s, 512)
    n = s // t

    def body(q_ref, k_ref, v_ref, o_ref, lse_ref, do_ref, dq_ref, dl_ref, acc, dl_s):
        i, j = _tri_rows(pl.program_id(1), n)

        @pl.when(j == 0)
        def _():
            acc[...] = jnp.zeros_like(acc)
            dl_s[...] = jnp.sum(do_ref[...] * o_ref[...], axis=-1, keepdims=True)

        def step(diagonal):
            sc = lax.dot_general(q_ref[...], k_ref[...], _NT, preferred_element_type=F32)
            if diagonal:
                sc = jnp.where(_diag_mask(t), sc, _NEG)
            pr = jnp.exp(sc - lse_ref[...])
            dp = lax.dot_general(do_ref[...].astype(BF16), v_ref[...], _NT, preferred_element_type=F32)
            ds = (pr * (dp - dl_s[...])).astype(BF16)
            acc[...] += jnp.dot(ds, k_ref[...], preferred_element_type=F32)

        @pl.when(j < i)
        def _():
            step(False)

        @pl.when(j == i)
        def _():
            step(True)
            dq_ref[...] = acc[...] * _ATT_SCALE
            dl_ref[...] = dl_s[...]

    qs = pl.BlockSpec((t, LANES), lambda h, p: (_tri_rows(p, n)[0], h))
    ks = pl.BlockSpec((t, LANES), lambda h, p: (_tri_rows(p, n)[1], h))
    ls = pl.BlockSpec((None, t, 1), lambda h, p: (h, _tri_rows(p, n)[0], 0))
    return pl.pallas_call(
        body, name=name, grid=(MLA_HEADS, n * (n + 1) // 2), in_specs=[qs, ks, ks, qs, ls, qs], out_specs=[qs, ls],
        out_shape=[jax.ShapeDtypeStruct((s, HP), F32), jax.ShapeDtypeStruct((MLA_HEADS, s, 1), F32)],
        scratch_shapes=[pltpu.VMEM((t, LANES), F32), pltpu.VMEM((t, 1), F32)],
        compiler_params=_params("parallel", "arbitrary"))(q, k, v, o, lse, do)


def _attn_bwd_dkv(q, k, v, lse, delta, do, name):
    s = q.shape[0]
    t = _pick(s, 512)
    n = s // t

    def body(q_ref, k_ref, v_ref, lse_ref, dl_ref, do_ref, dk_ref, dv_ref, dk_acc, dv_acc):
        j, i = _tri_cols(pl.program_id(1), n)

        def step(diagonal):
            sc = lax.dot_general(q_ref[...], k_ref[...], _NT, preferred_element_type=F32)
            if diagonal:
                sc = jnp.where(_diag_mask(t), sc, _NEG)
            pr = jnp.exp(sc - lse_ref[...])
            do16 = do_ref[...].astype(BF16)
            dv_acc[...] += lax.dot_general(pr.astype(BF16), do16, _TN, preferred_element_type=F32)
            dp = lax.dot_general(do16, v_ref[...], _NT, preferred_element_type=F32)
            ds = (pr * (dp - dl_ref[...])).astype(BF16)
            dk_acc[...] += lax.dot_general(ds, q_ref[...], _TN, preferred_element_type=F32)

        @pl.when(i == j)
        def _():
            dk_acc[...] = jnp.zeros_like(dk_acc)
            dv_acc[...] = jnp.zeros_like(dv_acc)
            step(True)

        @pl.when(i > j)
        def _():
            step(False)

        @pl.when(i == n - 1)
        def _():
            dk_ref[...] = dk_acc[...]
            dv_ref[...] = dv_acc[...]

    qs = pl.BlockSpec((t, LANES), lambda h, p: (_tri_cols(p, n)[1], h))
    ks = pl.BlockSpec((t, LANES), lambda h, p: (_tri_cols(p, n)[0], h))
    ls = pl.BlockSpec((None, t, 1), lambda h, p: (h, _tri_cols(p, n)[1], 0))
    return pl.pallas_call(
        body, name=name, grid=(MLA_HEADS, n * (n + 1) // 2), in_specs=[qs, ks, ks, ls, ls, qs], out_specs=[ks, ks],
        out_shape=[jax.ShapeDtypeStruct((s, HP), F32)] * 2,
        scratch_shapes=[pltpu.VMEM((t, LANES), F32), pltpu.VMEM((t, LANES), F32)],
        compiler_params=_params("parallel", "arbitrary"))(q, k, v, lse, delta, do)


XBC = HP + 2 * SSD_GROUPS * SSD_STATE
BCW = 2 * SSD_GROUPS * SSD_STATE


def _conv_fwd(p, col0, width, conv_w, conv_b, name):
    s = p.shape[0]
    c0, nblk = col0 // LANES, width // LANES

    def body(x_ref, w_ref, b_ref, o_ref, pad):
        pad[0:8, :] = jnp.zeros((8, LANES), F32)
        pad[8:s + 8, :] = x_ref[...].astype(F32)
        acc = jnp.broadcast_to(b_ref[...], (s, LANES))
        for t in range(SSD_CONV):
            acc = acc + pad[pl.ds(8 - (SSD_CONV - 1) + t, s), :] * w_ref[t:t + 1, :]
        o_ref[...] = acc * _sigmoid(acc)

    return pl.pallas_call(
        body, name=name, grid=(nblk,),
        in_specs=[pl.BlockSpec((s, LANES), lambda j: (0, c0 + j)), pl.BlockSpec((SSD_CONV, LANES), lambda j: (0, j)),
                  pl.BlockSpec((1, LANES), lambda j: (0, j))],
        out_specs=pl.BlockSpec((s, LANES), lambda j: (0, j)), out_shape=jax.ShapeDtypeStruct((s, width), F32),
        scratch_shapes=[pltpu.VMEM((s + 8, LANES), F32)], compiler_params=_params("parallel"))(p, conv_w, conv_b)


def _conv_bwd(p, col0, width, conv_w, conv_b, dact, name):
    s = p.shape[0]
    c0, nblk = col0 // LANES, width // LANES

    def body(x_ref, w_ref, b_ref, d_ref, dx_ref, dw_ref, db_ref, pad, padd):
        pad[0:8, :] = jnp.zeros((8, LANES), F32)
        pad[8:s + 8, :] = x_ref[...].astype(F32)
        acc = jnp.broadcast_to(b_ref[...], (s, LANES))
        for t in range(SSD_CONV):
            acc = acc + pad[pl.ds(8 - (SSD_CONV - 1) + t, s), :] * w_ref[t:t + 1, :]
        sg = _sigmoid(acc)
        dpre = d_ref[...] * (sg * (1.0 + acc * (1.0 - sg)))
        padd[0:s, :] = dpre
        padd[s:s + 8, :] = jnp.zeros((8, LANES), F32)
        dx = jnp.zeros((s, LANES), F32)
        for t in range(SSD_CONV):
            dx = dx + padd[pl.ds(SSD_CONV - 1 - t, s), :] * w_ref[t:t + 1, :]
            dw_ref[t:t + 1, :] = jnp.sum(dpre * pad[pl.ds(8 - (SSD_CONV - 1) + t, s), :], axis=0, keepdims=True)
        dx_ref[...] = dx.astype(BF16)
        db_ref[...] = jnp.sum(dpre, axis=0, keepdims=True)

    blk = pl.BlockSpec((s, LANES), lambda j: (0, j))
    return pl.pallas_call(
        body, name=name, grid=(nblk,),
        in_specs=[pl.BlockSpec((s, LANES), lambda j: (0, c0 + j)), pl.BlockSpec((SSD_CONV, LANES), lambda j: (0, j)),
                  pl.BlockSpec((1, LANES), lambda j: (0, j)), blk],
        out_specs=[blk, pl.BlockSpec((SSD_CONV, LANES), lambda j: (0, j)), pl.BlockSpec((1, LANES), lambda j: (0, j))],
        out_shape=[jax.ShapeDtypeStruct((s, width), BF16), jax.ShapeDtypeStruct((SSD_CONV, width), F32),
                   jax.ShapeDtypeStruct((1, width), F32)],
        scratch_shapes=[pltpu.VMEM((s + 8, LANES), F32), pltpu.VMEM((s + 8, LANES), F32)],
        compiler_params=_params("parallel"))(p, conv_w, conv_b, dact)


def _softplus(x):
    return jnp.maximum(x, 0.0) + jnp.log(1.0 + jnp.exp(-jnp.abs(x)))


def _dt_fwd(p, dt_bias, a_log, name):
    s = p.shape[0]
    tm = _pick(s, 512)

    def body(x_ref, b_ref, a_ref, dt_ref, da_ref):
        dtv = _softplus(x_ref[...] + b_ref[...])
        dav = dtv * (-jnp.exp(a_ref[...]))
        for h in range(SSD_HEADS):
            hs = slice(h * LANES, (h + 1) * LANES)
            dt_ref[:, hs] = jnp.broadcast_to(dtv[:, h:h + 1], (tm, LANES))
            da_ref[:, hs] = jnp.broadcast_to(dav[:, h:h + 1], (tm, LANES))

    out = pl.BlockSpec((tm, HP), lambda i: (i, 0))
    return pl.pallas_call(
        body, name=name, grid=(s // tm,),
        in_specs=[pl.BlockSpec((tm, LANES), lambda i: (i, T_DT // LANES)), _const((1, LANES)), _const((1, LANES))],
        out_specs=[out, out], out_shape=[jax.ShapeDtypeStruct((s, HP), F32)] * 2,
        compiler_params=_params("parallel"))(p, dt_bias, a_log)


def _dt_bwd(p, dt_bias, a_log, dda, ddtx, name):
    s = p.shape[0]
    tm = _pick(s, 512)

    def body(x_ref, b_ref, a_ref, dda_ref, ddtx_ref, dx_ref, db_ref, dal_ref):
        @pl.when(pl.program_id(0) == 0)
        def _():
            db_ref[...] = jnp.zeros_like(db_ref)
            dal_ref[...] = jnp.zeros_like(dal_ref)

        x = x_ref[...] + b_ref[...]
        dtv = _softplus(x)
        av = -jnp.exp(a_ref[...])
        lane = lax.broadcasted_iota(jnp.int32, (tm, LANES), 1)
        pa = jnp.zeros((tm, LANES), F32)
        px = jnp.zeros((tm, LANES), F32)
        for h in range(SSD_HEADS):
            pa = jnp.where(lane == h, dda_ref[:, h * LANES:(h + 1) * LANES], pa)
            px = jnp.where(lane == h, ddtx_ref[:, h * LANES:(h + 1) * LANES], px)
        draw = (pa * av + px) * _sigmoid(x)
        dx_ref[...] = draw.astype(BF16)
        db_ref[...] += jnp.sum(draw, axis=0, keepdims=True)
        dal_ref[...] += jnp.sum(pa * dtv, axis=0, keepdims=True) * av

    hd = pl.BlockSpec((tm, HP), lambda i: (i, 0))
    return pl.pallas_call(
        body, name=name, grid=(s // tm,),
        in_specs=[pl.BlockSpec((tm, LANES), lambda i: (i, T_DT // LANES)), _const((1, LANES)), _const((1, LANES)), hd, hd],
        out_specs=[pl.BlockSpec((tm, LANES), lambda i: (i, 0)), _const((1, LANES)), _const((1, LANES))],
        out_shape=[jax.ShapeDtypeStruct((s, LANES), BF16), jax.ShapeDtypeStruct((1, LANES), F32),
                   jax.ShapeDtypeStruct((1, LANES), F32)],
        compiler_params=_params("arbitrary"))(p, dt_bias, a_log, dda, ddtx)


def _cumsum_rows(x):
    row = lax.broadcasted_iota(jnp.int32, x.shape, 0)
    k = 1
    while k < x.shape[0]:
        x = x + jnp.where(row >= k, pltpu.roll(x, k, 0), 0.0)
        k *= 2
    return x


def _rev_cumsum_rows(x):
    n = x.shape[0]
    row = lax.broadcasted_iota(jnp.int32, x.shape, 0)
    k = 1
    while k < n:
        x = x + jnp.where(row < n - k, pltpu.roll(x, n - k, 0), 0.0)
        k *= 2
    return x


HPG = SSD_HEADS // SSD_GROUPS


def _chunk_terms(da, b_mat, c_mat):
    cs = _cumsum_rows(da)
    mask = _tril_mask()
    lm = jnp.exp(jnp.where(mask, cs - cs.T, _NEG))
    g = lax.dot_general(c_mat, b_mat, _NT, preferred_element_type=F32)
    cl = cs[CHUNK - 1:CHUNK, :]
    return cs, lm, g, cl


def _scan_fwd(xs, bc, dtb, dab, name):
    s = xs.shape[0]
    nc = s // CHUNK

    def body(x_ref, b_ref, c_ref, dt_ref, da_ref, y_ref, sin_ref, state):
        c, hh = pl.program_id(1), pl.program_id(2)

        @pl.when(c == 0)
        def _():
            state[hh] = jnp.zeros((SSD_STATE, LANES), F32)

        st = state[hh]
        sin_ref[...] = st
        b16, c16 = b_ref[...].astype(BF16), c_ref[...].astype(BF16)
        cs, lm, g, cl = _chunk_terms(da_ref[...], b16, c16)
        xd = (x_ref[...] * dt_ref[...]).astype(BF16)
        y = jnp.dot((g * lm).astype(BF16), xd, preferred_element_type=F32)
        y = y + jnp.dot(c16, st.astype(BF16), preferred_element_type=F32) * jnp.exp(cs)
        y_ref[...] = y
        bd = (b_ref[...] * jnp.exp(cl - cs)).astype(BF16)
        state[hh] = jnp.exp(cl) * st + lax.dot_general(bd, xd, _TN, preferred_element_type=F32)

    hd = pl.BlockSpec((CHUNK, LANES), lambda g, c, hh: (c, g * HPG + hh))
    return pl.pallas_call(
        body, name=name, grid=(SSD_GROUPS, nc, HPG),
        in_specs=[hd, pl.BlockSpec((CHUNK, LANES), lambda g, c, hh: (c, g)),
                  pl.BlockSpec((CHUNK, LANES), lambda g, c, hh: (c, SSD_GROUPS + g)), hd, hd],
        out_specs=[hd, pl.BlockSpec((None, None, SSD_STATE, LANES), lambda g, c, hh: (g * HPG + hh, c, 0, 0))],
        out_shape=[jax.ShapeDtypeStruct((s, HP), F32), jax.ShapeDtypeStruct((SSD_HEADS, nc, SSD_STATE, LANES), F32)],
        scratch_shapes=[pltpu.VMEM((HPG, SSD_STATE, LANES), F32)],
        compiler_params=_params("parallel", "arbitrary", "arbitrary"))(xs, bc, bc, dtb, dab)


def _scan_bwd(xs, bc, dtb, dab, s_in, dy, d_vec, name):
    s = xs.shape[0]
    nc = s // CHUNK

    def body(x_ref, b_ref, c_ref, dt_ref, da_ref, sin_ref, dy_ref, dv_ref, dx_ref, db_ref, dc_ref, dda_ref, ddtx_ref, dstate):
        c, hh = pl.program_id(1), pl.program_id(2)

        @pl.when(c == 0)
        def _():
            dstate[hh] = jnp.zeros((SSD_STATE, LANES), F32)

        st, ds = sin_ref[...], dstate[hh]
        st16, ds16 = st.astype(BF16), ds.astype(BF16)
        xv, bv, dtv, dyv = x_ref[...], b_ref[...], dt_ref[...], dy_ref[...]
        b16, c16 = bv.astype(BF16), c_ref[...].astype(BF16)
        cs, lm, g, cl = _chunk_terms(da_ref[...], b16, c16)
        ecs, ecl = jnp.exp(cs), jnp.exp(cl)
        decay = jnp.exp(cl - cs)
        xd32 = xv * dtv
        xd = xd32.astype(BF16)
        dy16 = dyv.astype(BF16)
        dye = (dyv * ecs).astype(BF16)
        yoff = jnp.dot(c16, st16, preferred_element_type=F32) * ecs
        dcs = jnp.sum(dyv * yoff, axis=-1, keepdims=True)
        dcm = lax.dot_general(dye, st16, _NT, preferred_element_type=F32)
        ds_in = ecl * ds + lax.dot_general(c16, dye, _TN, preferred_element_type=F32)
        dcl = jnp.sum(jnp.sum(ds * st, axis=0, keepdims=True), axis=1, keepdims=True) * ecl[:, 0:1]
        bd32 = bv * decay
        bd = bd32.astype(BF16)
        qm = lax.dot_general(xd, ds16, _NT, preferred_element_type=F32)
        dbm = qm * decay
        w = jnp.sum(bd32 * qm, axis=-1, keepdims=True)
        dcs = dcs - w
        dcl = dcl + jnp.sum(w, axis=0, keepdims=True)
        dxd = jnp.dot(bd, ds16, preferred_element_type=F32)
        m16 = (g * lm).astype(BF16)
        dm = lax.dot_general(dy16, xd, _NT, preferred_element_type=F32)
        dxd = dxd + lax.dot_general(m16, dy16, _TN, preferred_element_type=F32)
        dg = dm * lm
        dg16 = dg.astype(BF16)
        tt = dg * g
        dcm = dcm + jnp.dot(dg16, b16, preferred_element_type=F32)
        dbm = dbm + lax.dot_general(dg16, c16, _TN, preferred_element_type=F32)
        dcs = dcs + jnp.sum(tt, axis=-1, keepdims=True) - jnp.sum(tt.T, axis=-1, keepdims=True)
        row = lax.broadcasted_iota(jnp.int32, (CHUNK, 1), 0)
        dcs = dcs + jnp.where(row == CHUNK - 1, dcl, 0.0)
        dda_ref[...] = _rev_cumsum_rows(jnp.broadcast_to(dcs, (CHUNK, LANES)))
        ddtx_ref[...] = jnp.broadcast_to(jnp.sum(dxd * xv, axis=-1, keepdims=True), (CHUNK, LANES))
        dx_ref[...] = dxd * dtv + dyv * dv_ref[...]
        dstate[hh] = ds_in

        @pl.when(hh == 0)
        def _():
            db_ref[...] = dbm
            dc_ref[...] = dcm

        @pl.when(hh != 0)
        def _():
            db_ref[...] += dbm
            dc_ref[...] += dcm

    hd = pl.BlockSpec((CHUNK, LANES), lambda g, c, hh: (nc - 1 - c, g * HPG + hh))
    gp = pl.BlockSpec((CHUNK, LANES), lambda g, c, hh: (nc - 1 - c, g))
    return pl.pallas_call(
        body, name=name, grid=(SSD_GROUPS, nc, HPG),
        in_specs=[hd, pl.BlockSpec((CHUNK, LANES), lambda g, c, hh: (nc - 1 - c, g)),
                  pl.BlockSpec((CHUNK, LANES), lambda g, c, hh: (nc - 1 - c, SSD_GROUPS + g)), hd, hd,
                  pl.BlockSpec((None, None, SSD_STATE, LANES), lambda g, c, hh: (g * HPG + hh, nc - 1 - c, 0, 0)), hd,
                  pl.BlockSpec((1, LANES), lambda g, c, hh: (0, g * HPG + hh))],
        out_specs=[hd, gp, gp, hd, hd],
        out_shape=[jax.ShapeDtypeStruct((s, HP), F32), jax.ShapeDtypeStruct((s, SSD_GROUPS * SSD_STATE), F32),
                   jax.ShapeDtypeStruct((s, SSD_GROUPS * SSD_STATE), F32), jax.ShapeDtypeStruct((s, HP), F32),
                   jax.ShapeDtypeStruct((s, HP), F32)],
        scratch_shapes=[pltpu.VMEM((HPG, SSD_STATE, LANES), F32)],
        compiler_params=_params("parallel", "arbitrary", "arbitrary"))(xs, bc, bc, dtb, dab, s_in, dy, d_vec)


_GN = SSD_INNER // SSD_GROUPS
_GW = HP // SSD_GROUPS


def _ssd_post_fwd(y, xbc, p, d_vec, gain, name):
    s = y.shape[0]
    tm = _pick(s, 512)

    def body(y_ref, x_ref, z_ref, d_ref, g_ref, o_ref):
        z = z_ref[...].astype(F32)
        y2 = (y_ref[...] + x_ref[...] * d_ref[...]) * (z * _sigmoid(z))
        for g in range(SSD_GROUPS):
            gs = slice(g * _GW, (g + 1) * _GW)
            yg = y2[:, gs]
            r = lax.rsqrt(jnp.sum(yg * yg, axis=-1, keepdims=True) * (1.0 / _GN) + EPS)
            o_ref[:, gs] = (yg * r * g_ref[:, gs]).astype(BF16)

    hd = pl.BlockSpec((tm, HP), lambda i: (i, 0))
    return pl.pallas_call(
        body, name=name, grid=(s // tm,),
        in_specs=[hd, hd, pl.BlockSpec((tm, HP), lambda i: (i, C_Z // HP)), _const((1, HP)), _const((1, HP))],
        out_specs=hd, out_shape=jax.ShapeDtypeStruct((s, HP), BF16), compiler_params=_params("parallel"))(y, xbc, p, d_vec, gain)


def _ssd_post_bwd(y, xbc, p, d_vec, gain, dyn, name):
    s = y.shape[0]
    tm = _pick(s, 512)

    def body(y_ref, x_ref, z_ref, d_ref, g_ref, dn_ref, dy_ref, dz_ref, dg_ref, dd_ref):
        @pl.when(pl.program_id(0) == 0)
        def _():
            dg_ref[...] = jnp.zeros_like(dg_ref)
            dd_ref[...] = jnp.zeros_like(dd_ref)

        z, xv = z_ref[...].astype(F32), x_ref[...]
        sg = _sigmoid(z)
        sz = z * sg
        yt = y_ref[...] + xv * d_ref[...]
        y2 = yt * sz
        for g in range(SSD_GROUPS):
            gs = slice(g * _GW, (g + 1) * _GW)
            yg, dn = y2[:, gs], dn_ref[:, gs]
            r = lax.rsqrt(jnp.sum(yg * yg, axis=-1, keepdims=True) * (1.0 / _GN) + EPS)
            u = dn * g_ref[:, gs]
            dy2 = r * u - yg * (r * r * r) * (jnp.sum(yg * u, axis=-1, keepdims=True) * (1.0 / _GN))
            dg_ref[:, gs] += jnp.sum(dn * yg * r, axis=0, keepdims=True)
            dyt = dy2 * sz[:, gs]
            dy_ref[:, gs] = dyt
            dz_ref[:, gs] = (dy2 * yt[:, gs] * (sg[:, gs] * (1.0 + z[:, gs] * (1.0 - sg[:, gs])))).astype(BF16)
            dd_ref[:, gs] += jnp.sum(dyt * xv[:, gs], axis=0, keepdims=True)

    hd = pl.BlockSpec((tm, HP), lambda i: (i, 0))
    return pl.pallas_call(
        body, name=name, grid=(s // tm,),
        in_specs=[hd, hd, pl.BlockSpec((tm, HP), lambda i: (i, C_Z // HP)), _const((1, HP)), _const((1, HP)), hd],
        out_specs=[hd, hd, _const((1, HP)), _const((1, HP))],
        out_shape=[jax.ShapeDtypeStruct((s, HP), F32), jax.ShapeDtypeStruct((s, HP), BF16),
                   jax.ShapeDtypeStruct((1, HP), F32), jax.ShapeDtypeStruct((1, HP), F32)],
        compiler_params=_params("arbitrary"))(y, xbc, p, d_vec, gain, dyn)


def _merge_fwd(p, ya, o, yc, wb0, wb1, wb2, w_out, x, name):
    s = p.shape[0]
    tm = _pick(s, 512)

    def body(g_ref, ya_ref, o_ref, yc_ref, w0_ref, w1_ref, w2_ref, wo_ref, x_ref, mg_ref, y_ref):
        acc = jnp.zeros((tm, D_MODEL), F32)
        for i, (b_ref, w_ref) in enumerate(((ya_ref, w0_ref), (o_ref, w1_ref), (yc_ref, w2_ref))):
            t = jnp.dot(b_ref[...].astype(BF16), w_ref[...], preferred_element_type=F32)
            acc = acc + _sigmoid(g_ref[:, i * D_MODEL:(i + 1) * D_MODEL].astype(F32)) * t
        mg = acc.astype(BF16)
        mg_ref[...] = mg
        y_ref[...] = x_ref[...] + jnp.dot(mg, wo_ref[...], preferred_element_type=F32)

    row = pl.BlockSpec((tm, D_MODEL), lambda i: (i, 0))
    return pl.pallas_call(
        body, name=name, grid=(s // tm,),
        in_specs=[pl.BlockSpec((tm, 3 * D_MODEL), lambda i: (i, C_G // (3 * D_MODEL))),
                  pl.BlockSpec((tm, GM_WIDTH), lambda i: (i, 0)), row, row,
                  _resident((GM_WIDTH, D_MODEL)), _resident((HP, D_MODEL)), _resident((HP, D_MODEL)),
                  _resident((D_MODEL, D_MODEL)), row],
        out_specs=[row, row],
        out_shape=[jax.ShapeDtypeStruct((s, D_MODEL), BF16), jax.ShapeDtypeStruct((s, D_MODEL), F32)],
        compiler_params=_params("parallel"))(p, ya, o, yc, wb0, wb1, wb2, w_out, x)


def _merge_bwd(p, ya, o, yc, wb0, wb1, wb2, w_out, dy, name):
    s = p.shape[0]
    tm = _pick(s, 512)

    def body(g_ref, ya_ref, o_ref, yc_ref, w0_ref, w1_ref, w2_ref, wo_ref, dy_ref,
             d0_ref, d1_ref, d2_ref, dg_ref, dya_ref, do_ref, dyc_ref):
        dm = lax.dot_general(dy_ref[...].astype(BF16), wo_ref[...], _NT, preferred_element_type=F32)
        for i, (b_ref, w_ref, d_ref, db_ref) in enumerate(((ya_ref, w0_ref, d0_ref, dya_ref), (o_ref, w1_ref, d1_ref, do_ref),
                                                            (yc_ref, w2_ref, d2_ref, dyc_ref))):
            cs = slice(i * D_MODEL, (i + 1) * D_MODEL)
            t = jnp.dot(b_ref[...].astype(BF16), w_ref[...], preferred_element_type=F32)
            sg = _sigmoid(g_ref[:, cs].astype(F32))
            dt16 = (dm * sg).astype(BF16)
            d_ref[...] = dt16
            dg_ref[:, cs] = (dm * t * sg * (1.0 - sg)).astype(BF16)
            db_ref[...] = lax.dot_general(dt16, w_ref[...], _NT, preferred_element_type=F32)

    row = pl.BlockSpec((tm, D_MODEL), lambda i: (i, 0))
    nar = pl.BlockSpec((tm, GM_WIDTH), lambda i: (i, 0))
    wide = pl.BlockSpec((tm, 3 * D_MODEL), lambda i: (i, 0))
    return pl.pallas_call(
        body, name=name, grid=(s // tm,),
        in_specs=[pl.BlockSpec((tm, 3 * D_MODEL), lambda i: (i, C_G // (3 * D_MODEL))), nar, row, row,
                  _resident((GM_WIDTH, D_MODEL)), _resident((HP, D_MODEL)), _resident((HP, D_MODEL)),
                  _resident((D_MODEL, D_MODEL)), row],
        out_specs=[row, row, row, wide, nar, row, row],
        out_shape=[jax.ShapeDtypeStruct((s, D_MODEL), BF16)] * 3 + [jax.ShapeDtypeStruct((s, 3 * D_MODEL), BF16),
                   jax.ShapeDtypeStruct((s, GM_WIDTH), F32), jax.ShapeDtypeStruct((s, D_MODEL), F32),
                   jax.ShapeDtypeStruct((s, D_MODEL), F32)],
        compiler_params=_params("parallel"))(p, ya, o, yc, wb0, wb1, wb2, w_out, dy)


def _loss_head(y, target, name):
    s, d = y.shape
    tm = _pick(s, 512)

    def body(y_ref, t_ref, dy_ref, sq_ref):
        @pl.when(pl.program_id(0) == 0)
        def _():
            sq_ref[...] = jnp.zeros_like(sq_ref)

        e = y_ref[...] - t_ref[...]
        dy_ref[...] = e * (1.0 / d)
        sq_ref[...] += jnp.sum(e * e, axis=0, keepdims=True)

    row = pl.BlockSpec((tm, d), lambda i: (i, 0))
    return pl.pallas_call(
        body, name=name, grid=(s // tm,), in_specs=[row, row], out_specs=[row, _const((1, d))],
        out_shape=[jax.ShapeDtypeStruct((s, d), F32), jax.ShapeDtypeStruct((1, d), F32)],
        compiler_params=_params("arbitrary"))(y, target)


def _adamw(w, g, m, v, name):
    rows, cols = w.shape
    tr = rows
    for cand in (512, 256, 128, 64, 32, 16, 8):
        if rows % cand == 0 and cand * cols * 4 <= 3 * 1024 * 1024:
            tr = cand
            break

    def body(w_ref, g_ref, m_ref, v_ref, d_ref, nm_ref, nv_ref):
        d_ref[...], nm_ref[...], nv_ref[...] = _adam_update(w_ref[...], g_ref[...], m_ref[...], v_ref[...])

    blk = pl.BlockSpec((tr, cols), lambda i: (i, 0))
    return pl.pallas_call(
        body, name=name, grid=(rows // tr,), in_specs=[blk] * 4, out_specs=[blk] * 3,
        out_shape=[jax.ShapeDtypeStruct((rows, cols), F32)] * 3, compiler_params=_params("parallel"))(w, g, m, v)


def _adam_update(w, g, m, v):
    nm = ADAM_B1 * m + (1.0 - ADAM_B1) * g
    nv = ADAM_B2 * v + (1.0 - ADAM_B2) * (g * g)
    c1 = 1.0 - ADAM_B1 ** ADAM_STEP
    c2 = 1.0 - ADAM_B2 ** ADAM_STEP
    return -ADAM_LR * ((nm / c1) / (jnp.sqrt(nv / c2) + ADAM_EPS) + ADAM_WD * w), nm, nv


def _adamw_sharded(w, m, v, mine, theirs, name):
    depth, rows, cols = w.shape
    tr = _row_tile(rows, cols, 1024 * 1024)

    def body(w_ref, m_ref, v_ref, a_ref, b_ref, g_ref, d_ref, nm_ref, nv_ref):
        c = lax.axis_index("c")
        g = jnp.where(pl.program_id(0) // HALF_L == c, a_ref[...], b_ref[...])
        g_ref[...] = g
        d_ref[...], nm_ref[...], nv_ref[...] = _adam_update(w_ref[...], g, m_ref[...], v_ref[...])

    blk = pl.BlockSpec((None, tr, cols), lambda l, i: (l, i, 0))
    half = pl.BlockSpec((None, tr, cols), lambda l, i: (l % HALF_L, i, 0))
    return pl.pallas_call(
        body, name=name, grid=(depth, rows // tr), in_specs=[blk, blk, blk, half, half], out_specs=[blk] * 4,
        out_shape=[jax.ShapeDtypeStruct((depth, rows, cols), F32)] * 4,
        compiler_params=_params("parallel", "parallel"))(w, m, v, mine, theirs)


ANY = pl.BlockSpec(memory_space=pl.ANY)


def _me():
    return lax.axis_index("x"), lax.axis_index("y"), lax.axis_index("c")


def _other_chips(x, y):
    return [(1 - x, y), (x, 1 - y), (1 - x, 1 - y)]


def _chip_index(cx, cy):
    return 2 * cx + cy


def _gather_shards(shards):
    n = len(shards)

    def body(*refs):
        in_refs, out_refs = refs[:n], refs[n:2 * n]
        send_sems, recv_sems = refs[2 * n:]
        x, y, c = _me()
        sib = (x, y, 1 - c)
        chips = _other_chips(x, y)
        mychip = _chip_index(x, y)

        def part(t, chip, hc):
            return out_refs[t].at[pl.ds(hc * HALF_L, HALF_L), chip]

        def copy(t, k, chip, hc, to, src=None):
            dst = part(t, chip, hc)
            return pltpu.make_async_remote_copy(src_ref=dst if src is None else src, dst_ref=dst, send_sem=send_sems.at[6 * t + k],
                                                recv_sem=recv_sems.at[6 * t + k], device_id=to, device_id_type=MESH)

        first = [copy(t, j, mychip, c, (*chip, c), src=in_refs[t].at[pl.ds(c * HALF_L, HALF_L)])
                 for j, chip in enumerate(chips) for t in range(n)]
        for cp in first:
            cp.start()
        passed = []
        for j, chip in enumerate(chips):
            for t in range(n):
                copy(t, j, _chip_index(*chip), c, (x, y, c)).wait_recv()
                cp = copy(t, 3 + j, _chip_index(*chip), c, sib)
                cp.start()
                passed.append(cp)
        for j, chip in enumerate(chips):
            for t in range(n):
                copy(t, 3 + j, _chip_index(*chip), 1 - c, (x, y, c)).wait_recv()
        for cp in first + passed:
            cp.wait_send()

    got = pl.pallas_call(
        body, name="gather_shards", in_specs=[ANY] * n, out_specs=[ANY] * n,
        out_shape=[jax.ShapeDtypeStruct((DEPTH, N_CHIPS) + a.shape[1:], a.dtype) for a in shards],
        scratch_shapes=[pltpu.SemaphoreType.DMA((6 * n,)), pltpu.SemaphoreType.DMA((6 * n,))])(*shards)
    cx, cy, _ = _me()
    return [lax.dynamic_update_index_in_dim(g, a, _chip_index(cx, cy), 1) for g, a in zip(got, shards)]


def _pair_exchange(gs):
    n = len(gs)

    def body(*refs):
        in_refs, out_refs = refs[:n], refs[n:2 * n]
        send_sems, recv_sems = refs[2 * n:]
        x, y, c = _me()
        cps = [pltpu.make_async_remote_copy(src_ref=in_refs[t].at[pl.ds((1 - c) * HALF_L, HALF_L)], dst_ref=out_refs[t],
                                            send_sem=send_sems.at[t], recv_sem=recv_sems.at[t], device_id=(x, y, 1 - c),
                                            device_id_type=MESH) for t in range(n)]
        for cp in cps:
            cp.start()
        for cp in cps:
            cp.wait()

    return pl.pallas_call(
        body, name="pair_exchange", in_specs=[ANY] * n, out_specs=[ANY] * n,
        out_shape=[jax.ShapeDtypeStruct((HALF_L,) + a.shape[1:], a.dtype) for a in gs],
        scratch_shapes=[pltpu.SemaphoreType.DMA((n,)), pltpu.SemaphoreType.DMA((n,))])(*gs)


def _row_tile(rows, cols, budget=2 * 1024 * 1024):
    best = None
    for t in range(8, rows + 1, 8):
        if rows % t == 0 and t * cols * 4 <= budget:
            best = t
    return best or rows


def _pair_add(g, got, name):
    _, _, rows, cols = g.shape
    tr = _row_tile(rows, cols)

    def body(lo_ref, hi_ref, r_ref, o16_ref, own_ref):
        x, y, c = _me()
        tot = jnp.where(c == 0, lo_ref[...], hi_ref[...]) + r_ref[...]
        o16_ref[...] = tot.astype(BF16)

        @pl.when(pl.program_id(2) == _chip_index(x, y))
        def _():
            own_ref[...] = tot

    blk = (None, None, tr, cols)
    return pl.pallas_call(
        body, name=name, grid=(HALF_L, rows // tr, N_CHIPS),
        in_specs=[pl.BlockSpec(blk, lambda a, i, k: (a, k, i, 0)), pl.BlockSpec(blk, lambda a, i, k: (a + HALF_L, k, i, 0)),
                  pl.BlockSpec(blk, lambda a, i, k: (a, k, i, 0))],
        out_specs=[pl.BlockSpec(blk, lambda a, i, k: (a, k, i, 0)), pl.BlockSpec((None, tr, cols), lambda a, i, k: (a, i, 0))],
        out_shape=[jax.ShapeDtypeStruct((HALF_L, N_CHIPS, rows, cols), BF16), jax.ShapeDtypeStruct((HALF_L, rows, cols), F32)],
        compiler_params=_params("parallel", "parallel", "arbitrary"))(g, g, got)


def _chip_exchange(parts):
    n = len(parts)

    def body(*refs):
        in_refs, out_refs = refs[:n], refs[n:2 * n]
        send_sems, recv_sems = refs[2 * n:]
        x, y, c = _me()
        chips = _other_chips(x, y)
        rows = pl.ds(0, HALF_L)
        cps = [pltpu.make_async_remote_copy(src_ref=in_refs[t].at[rows, _chip_index(*chip)], dst_ref=out_refs[t].at[j],
                                            send_sem=send_sems.at[3 * t + j], recv_sem=recv_sems.at[3 * t + j],
                                            device_id=(*chip, c), device_id_type=MESH)
               for j, chip in enumerate(chips) for t in range(n)]
        for cp in cps:
            cp.start()
        for cp in cps:
            cp.wait()

    return pl.pallas_call(
        body, name="chip_exchange", in_specs=[ANY] * n, out_specs=[ANY] * n,
        out_shape=[jax.ShapeDtypeStruct((3, HALF_L) + a.shape[2:], a.dtype) for a in parts],
        scratch_shapes=[pltpu.SemaphoreType.DMA((3 * n,)), pltpu.SemaphoreType.DMA((3 * n,))])(*parts)


def _chip_add(own, got, name):
    _, rows, cols = own.shape
    tr = _row_tile(rows, cols, 1024 * 1024)

    def body(own_ref, got_ref, o_ref):
        acc = own_ref[...]
        for j in range(3):
            acc = acc + got_ref[j].astype(F32)
        o_ref[...] = acc

    return pl.pallas_call(
        body, name=name, grid=(HALF_L, rows // tr),
        in_specs=[pl.BlockSpec((None, tr, cols), lambda a, i: (a, i, 0)),
                  pl.BlockSpec((3, None, tr, cols), lambda a, i: (0, a, i, 0))],
        out_specs=pl.BlockSpec((None, tr, cols), lambda a, i: (a, i, 0)),
        out_shape=jax.ShapeDtypeStruct((HALF_L, rows, cols), F32), compiler_params=_params("parallel", "parallel"))(own, got)


def _pair_share(halves):
    n = len(halves)

    def body(*refs):
        in_refs, out_refs = refs[:n], refs[n:2 * n]
        send_sems, recv_sems = refs[2 * n:]
        x, y, c = _me()
        cps = [pltpu.make_async_remote_copy(src_ref=in_refs[t], dst_ref=out_refs[t], send_sem=send_sems.at[t],
                                            recv_sem=recv_sems.at[t], device_id=(x, y, 1 - c), device_id_type=MESH)
               for t in range(n)]
        for cp in cps:
            cp.start()
        for cp in cps:
            cp.wait()

    return pl.pallas_call(
        body, name="pair_share", in_specs=[ANY] * n, out_specs=[ANY] * n,
        out_shape=[jax.ShapeDtypeStruct(a.shape, a.dtype) for a in halves],
        scratch_shapes=[pltpu.SemaphoreType.DMA((n,)), pltpu.SemaphoreType.DMA((n,))])(*halves)


N_DEV = 8


def _all_exchange(v):
    r, cols = v.shape

    def body(in_ref, out_ref, send_sems, recv_sems, local_sem):
        x, y, c = _me()
        me = 4 * x + 2 * y + c
        local = pltpu.make_async_copy(in_ref, out_ref.at[me], local_sem)
        local.start()
        flip = lambda v, f: 1 - v if f else v
        peers = [(flip(x, fx), flip(y, fy), flip(c, fc)) for fx in (0, 1) for fy in (0, 1) for fc in (0, 1)][1:]
        cps = [pltpu.make_async_remote_copy(src_ref=in_ref, dst_ref=out_ref.at[me], send_sem=send_sems.at[j],
                                            recv_sem=recv_sems.at[j], device_id=peer, device_id_type=MESH)
               for j, peer in enumerate(peers)]
        for cp in cps:
            cp.start()
        for j, (px, py, pc) in enumerate(peers):
            pltpu.make_async_remote_copy(src_ref=in_ref, dst_ref=out_ref.at[4 * px + 2 * py + pc], send_sem=send_sems.at[j],
                                         recv_sem=recv_sems.at[j], device_id=(px, py, pc), device_id_type=MESH).wait_recv()
        for cp in cps:
            cp.wait_send()
        local.wait()

    return pl.pallas_call(
        body, name="all_exchange", in_specs=[ANY], out_specs=ANY, out_shape=jax.ShapeDtypeStruct((N_DEV, r, cols), v.dtype),
        scratch_shapes=[pltpu.SemaphoreType.DMA((7,)), pltpu.SemaphoreType.DMA((7,)), pltpu.SemaphoreType.DMA(())])(v)


def _sum_slots(a, name):
    n, r, cols = a.shape
    tr = _pick(r, 512) if r % 8 == 0 else r
    for cand in (512, 256, 128, 64, 32, 16, 8):
        if r % cand == 0:
            tr = cand
            break

    def body(a_ref, o_ref):
        acc = a_ref[0]
        for k in range(1, n):
            acc = acc + a_ref[k]
        o_ref[...] = acc

    return pl.pallas_call(
        body, name=name, grid=(r // tr,), in_specs=[pl.BlockSpec((n, tr, cols), lambda i: (0, i, 0))],
        out_specs=pl.BlockSpec((tr, cols), lambda i: (i, 0)), out_shape=jax.ShapeDtypeStruct((r, cols), F32),
        compiler_params=_params("parallel"))(a)


def _join(name, stacked):
    ax = SHARDED[name][1]
    return jnp.concatenate([stacked[k] for k in range(N_CHIPS)], axis=ax)


def _split(name, full):
    ax = SHARDED[name][1]
    return jnp.stack(jnp.split(full, N_CHIPS, axis=ax))


def _heads_pad(a, real, axis):
    shp = a.shape
    a = a.reshape(shp[:axis] + (MLA_HEADS, real) + shp[axis + 1:])
    pad = [(0, 0)] * a.ndim
    pad[axis + 1] = (0, LANES - real)
    a = jnp.pad(a, pad)
    return a.reshape(shp[:axis] + (HP,) + shp[axis + 1:])


def _heads_unpad(a, real, axis):
    shp = a.shape
    a = a.reshape(shp[:axis] + (MLA_HEADS, LANES) + shp[axis + 1:])
    a = lax.slice_in_dim(a, 0, real, axis=axis + 1)
    return a.reshape(shp[:axis] + (MLA_HEADS * real,) + shp[axis + 1:])


def _lane_place(a, start):
    n = a.shape[-1]
    pad = [(0, 0)] * (a.ndim - 1) + [(start, LANES - start - n)]
    return jnp.pad(a, pad)


_O_UV, _O_CQ, _O_CKV, _O_KR, _O_Z, _O_XBC, _O_DT, _O_G = 0, 1024, 1408, 1664, 1696, 2208, 3232, 3240


def _w_in_pad(w):
    sl = lambda a, b: w[:, a:b]
    xs = _heads_pad(sl(_O_XBC, _O_XBC + SSD_INNER), SSD_HEAD_DIM, 1)
    bc = sl(_O_XBC + SSD_INNER, _O_DT)
    main = jnp.concatenate([sl(_O_UV, _O_CQ), _heads_pad(sl(_O_Z, _O_XBC), SSD_HEAD_DIM, 1), xs, sl(_O_G, IN_COLS)], axis=1)
    tail = jnp.concatenate([bc, sl(_O_CKV, _O_KR), sl(_O_CQ, _O_CKV), _lane_place(sl(_O_KR, _O_Z), MLA_NOPE),
                            _lane_place(sl(_O_DT, _O_G), 0), jnp.zeros((w.shape[0], PW_TAIL - T_DT - LANES), w.dtype)], axis=1)
    return main, tail


def _w_in_unpad(gm, gt):
    m = lambda a, n: gm[:, a:a + n]
    t = lambda a, n: gt[:, a:a + n]
    parts = [m(C_UV, 1024), t(T_CQ, MLA_Q_RANK), t(T_CKV, MLA_KV_RANK), t(T_KR + MLA_NOPE, MLA_ROPE),
             _heads_unpad(m(C_Z, HP), SSD_HEAD_DIM, 1), _heads_unpad(m(C_XS, HP), SSD_HEAD_DIM, 1), t(T_BC, BCW),
             t(T_DT, SSD_HEADS), m(C_G, 3 * D_MODEL)]
    return jnp.concatenate(parts, axis=1)


def _xbc_pad(a):
    return jnp.concatenate([_heads_pad(a[..., :SSD_INNER], SSD_HEAD_DIM, a.ndim - 1), a[..., SSD_INNER:]], axis=-1)


def _xbc_unpad(a):
    return jnp.concatenate([_heads_unpad(a[..., :HP], SSD_HEAD_DIM, a.ndim - 1), a[..., HP:]], axis=-1)


def _rope_tables(positions):
    inv_freq = 1.0 / (ROPE_THETA ** (jnp.arange(0, MLA_ROPE, 2, dtype=F32) / MLA_ROPE))
    ang = positions.astype(F32)[:, None] * inv_freq
    cos, sin = jnp.cos(ang), jnp.sin(ang)
    s = positions.shape[0]
    half = MLA_ROPE // 2
    z = lambda n: jnp.zeros((s, n), F32)
    ct = jnp.concatenate([jnp.ones((s, MLA_NOPE), F32), cos, cos, z(LANES - MLA_QK)], axis=1)
    s1 = jnp.concatenate([z(MLA_NOPE), -sin, z(half), z(LANES - MLA_QK)], axis=1)
    s2 = jnp.concatenate([z(MLA_NOPE), z(half), sin, z(LANES - MLA_QK)], axis=1)
    return ct, s1, s2


def _layer_weights(full, small, l):
    w = {}
    for n in ('ffn1_w_in', 'ffn2_w_in'):
        w[n] = full[n][l]
    for n in ('ffn1_w_out', 'ffn2_w_out', 'w_out'):
        g = full[n][l]
        w[n] = g.reshape((N_CHIPS * g.shape[1], g.shape[2]))
    fl = {n: _join(n, full[n][l]) for n in ('w_in', 'mla_w_uq', 'mla_w_ukv', 'w_branch', 'ssd_conv_w')}
    w['w_in_main'], w['w_in_tail'] = _w_in_pad(fl['w_in'])
    w['wuq'] = _heads_pad(fl['mla_w_uq'], MLA_QK, 1)
    ukv = fl['mla_w_ukv'].reshape(MLA_KV_RANK, MLA_HEADS, MLA_NOPE + MLA_V)
    zero = jnp.zeros((MLA_KV_RANK, MLA_HEADS, LANES - MLA_NOPE), ukv.dtype)
    wk = jnp.concatenate([ukv[:, :, :MLA_NOPE], zero], axis=2).reshape(MLA_KV_RANK, HP)
    wv = jnp.concatenate([ukv[:, :, MLA_NOPE:], zero], axis=2).reshape(MLA_KV_RANK, HP)
    w['wkv'] = jnp.concatenate([wk, wv], axis=1)
    wb = fl['w_branch']
    w['wb0'] = wb[0]
    w['wb1'] = _heads_pad(wb[1], MLA_V, 0)
    w['wb2'] = _heads_pad(wb[2], SSD_HEAD_DIM, 0)
    w['conv_w'] = _xbc_pad(fl['ssd_conv_w'].astype(F32))
    row = lambda n: small[n][l][None, :]
    for n in ('ffn1_norm', 'mix_norm', 'gm_v_norm', 'mla_q_norm', 'mla_kv_norm', 'ffn2_norm'):
        w[n] = row(n)
    w['gm_w_s'] = small['gm_w_s'][l]
    w['gm_b_full'] = jnp.broadcast_to(small['gm_b_s'][l][:, :, None], (GM_GROUPS, CHUNK, LANES))
    w['gq'] = _lane_place(row('mla_q_gain'), 0)
    w['gk'] = _lane_place(row('mla_k_gain'), 0)
    w['conv_b'] = _xbc_pad(row('ssd_conv_b'))
    w['dt_bias'] = _lane_place(row('ssd_dt_bias'), 0)
    w['a_log'] = _lane_place(row('ssd_a_log'), 0)
    w['d_vec'] = jnp.repeat(small['ssd_d'][l], LANES)[None, :]
    w['ssd_norm'] = _heads_pad(row('ssd_norm'), SSD_HEAD_DIM, 1)
    return w


def _ffn_fwd(x, norm, w4, w_out, tag):
    h, gate, up, act = _ffn_in(x, norm, w4, f"{tag}_in")
    y = _ffn_out(act, w_out, x, f"{tag}_out")
    return y, (x, h, gate, up, act)


def _ffn_bwd(dy, saved, norm, w4, w_out, tag):
    x, h, gate, up, act = saved
    dw_out = _ffn_dwout(act, dy, f"{tag}_dwout")
    da = _ffn_dact(dy, w_out, gate, up, f"{tag}_dact")
    dw_in = _ffn_dwin(h, da, f"{tag}_dwin")
    dx, dnorm = _ffn_dx(da, w4, x, norm, dy, f"{tag}_dx")
    return dx, dnorm, dw_in, dw_out.reshape((N_CHIPS, 2 * FC // N_CHIPS, D_MODEL))


def _mixer_fwd(x, w, tabs, tag):
    h = _rmsnorm_fwd(x, w['mix_norm'], f"{tag}_norm")
    pm = _matmul(h, w['w_in_main'], out_dtype=BF16, name=f"{tag}_proj_main")
    pt = _matmul(h, w['w_in_tail'], name=f"{tag}_proj_tail")
    ya = _gmlp_fwd(pm, w['gm_v_norm'], w['gm_w_s'], w['gm_b_full'], f"{tag}_gmlp")
    q, k, v = _mla_pre_fwd(pt, tabs, w['mla_q_norm'], w['mla_kv_norm'], w['wuq'], w['wkv'], w['gq'], w['gk'], f"{tag}_mla_pre")
    o, lse = _attn_fwd(q, k, v, f"{tag}_attn")
    xs = _conv_fwd(pm, C_XS, HP, w['conv_w'][:, :HP], w['conv_b'][:, :HP], f"{tag}_conv_x")
    bc = _conv_fwd(pt, T_BC, BCW, w['conv_w'][:, HP:], w['conv_b'][:, HP:], f"{tag}_conv_bc")
    dtb, dab = _dt_fwd(pt, w['dt_bias'], w['a_log'], f"{tag}_dt")
    ys, s_in = _scan_fwd(xs, bc, dtb, dab, f"{tag}_scan")
    yc = _ssd_post_fwd(ys, xs, pm, w['d_vec'], w['ssd_norm'], f"{tag}_ssd_post")
    mg, y = _merge_fwd(pm, ya, o, yc, w['wb0'], w['wb1'], w['wb2'], w['w_out'], x, f"{tag}_merge")
    return y, (x, h, pm, pt, ya, q, k, v, o, lse, xs, bc, dtb, dab, ys, s_in, yc, mg)


def _mixer_bwd(dy, saved, w, tabs, tag):
    x, h, pm, pt, ya, q, k, v, o, lse, xs, bc, dtb, dab, ys, s_in, yc, mg = saved
    g = {}
    g['w_out'] = _matmul(mg, dy, ta=True, name=f"{tag}_dwout").reshape((N_CHIPS, D_MODEL // N_CHIPS, D_MODEL))
    d0, d1, d2, dgates, dya, do, dyc = _merge_bwd(pm, ya, o, yc, w['wb0'], w['wb1'], w['wb2'], w['w_out'], dy, f"{tag}_dmerge")
    dwb0 = _matmul(ya, d0, ta=True, name=f"{tag}_dwb0")
    dwb1 = _matmul(o, d1, ta=True, name=f"{tag}_dwb1")
    dwb2 = _matmul(yc, d2, ta=True, name=f"{tag}_dwb2")
    g['w_branch'] = _split('w_branch', jnp.stack([dwb0, _heads_unpad(dwb1, MLA_V, 0), _heads_unpad(dwb2, SSD_HEAD_DIM, 0)]))
    duv, g['gm_v_norm'], g['gm_w_s'], db = _gmlp_bwd(pm, w['gm_v_norm'], w['gm_w_s'], w['gm_b_full'], dya, f"{tag}_dgmlp")
    g['gm_b_s'] = db.T
    dq, delta = _attn_bwd_dq(q, k, v, o, lse, do, f"{tag}_dattn_q")
    dk, dv = _attn_bwd_dkv(q, k, v, lse, delta, do, f"{tag}_dattn_kv")
    dcq, dckv, dkr, dwuq, dwkv, g['mla_q_norm'], g['mla_kv_norm'], dgq, dgk = _mla_pre_bwd(
        pt, tabs, w['mla_q_norm'], w['mla_kv_norm'], w['wuq'], w['wkv'], w['gq'], w['gk'], dq, dk, dv, f"{tag}_dmla_pre")
    g['mla_w_uq'] = _split('mla_w_uq', _heads_unpad(dwuq, MLA_QK, 1))
    dwk = dwkv[:, :HP].reshape(MLA_KV_RANK, MLA_HEADS, LANES)[:, :, :MLA_NOPE]
    dwv = dwkv[:, HP:].reshape(MLA_KV_RANK, MLA_HEADS, LANES)[:, :, :MLA_V]
    g['mla_w_ukv'] = _split('mla_w_ukv', jnp.concatenate([dwk, dwv], axis=2).reshape(MLA_KV_RANK, MLA_HEADS * (MLA_NOPE + MLA_V)))
    g['mla_q_gain'], g['mla_k_gain'] = dgq[:, :MLA_QK], dgk[:, :MLA_QK]
    dys, dz, dssd_norm, dd = _ssd_post_bwd(ys, xs, pm, w['d_vec'], w['ssd_norm'], dyc, f"{tag}_dssd_post")
    g['ssd_norm'] = _heads_unpad(dssd_norm, SSD_HEAD_DIM, 1)
    g['ssd_d'] = jnp.sum(dd.reshape(SSD_HEADS, LANES), axis=1)[None, :]
    dxs, dbm, dcm, dda, ddtx = _scan_bwd(xs, bc, dtb, dab, s_in, dys, w['d_vec'], f"{tag}_dscan")
    dxs16, dcw_x, dcb_x = _conv_bwd(pm, C_XS, HP, w['conv_w'][:, :HP], w['conv_b'][:, :HP], dxs, f"{tag}_dconv_x")
    dbc16, dcw_bc, dcb_bc = _conv_bwd(pt, T_BC, BCW, w['conv_w'][:, HP:], w['conv_b'][:, HP:],
                                      jnp.concatenate([dbm, dcm], axis=1), f"{tag}_dconv_bc")
    g['ssd_conv_w'] = _xbc_unpad(jnp.concatenate([dcw_x, dcw_bc], axis=1))
    g['ssd_conv_b'] = _xbc_unpad(jnp.concatenate([dcb_x, dcb_bc], axis=1))
    ddt, dbias, dalog = _dt_bwd(pt, w['dt_bias'], w['a_log'], dda, ddtx, f"{tag}_ddt")
    g['ssd_dt_bias'], g['ssd_a_log'] = dbias[:, :SSD_HEADS], dalog[:, :SSD_HEADS]
    s = x.shape[0]
    dpm = jnp.concatenate([duv, dz, dxs16, dgates], axis=1)
    dpt = jnp.concatenate([dbc16, dckv, dcq, dkr, ddt, jnp.zeros((s, PW_TAIL - T_DT - LANES), BF16)], axis=1)
    g['w_in'] = _split('w_in', _w_in_unpad(_matmul(h, dpm, ta=True, name=f"{tag}_dwin_main"),
                                           _matmul(h, dpt, ta=True, name=f"{tag}_dwin_tail")))
    dh = _matmul(dpt, w['w_in_tail'], tb=True, name=f"{tag}_dh_tail")
    dh = _matmul(dpm, w['w_in_main'], tb=True, res=dh, name=f"{tag}_dh_main")
    dx, g['mix_norm'] = _rmsnorm_bwd(x, w['mix_norm'], dh, dy, f"{tag}_dnorm")
    return dx, g


def _local_step(x, positions, target, full, small):
    tabs = _rope_tables(positions)
    ws = [_layer_weights(full, small, l) for l in range(DEPTH)]
    saved = []
    for l, w in enumerate(ws):
        x, s1 = _ffn_fwd(x, w['ffn1_norm'], w['ffn1_w_in'], w['ffn1_w_out'], "ffn1")
        x, s2 = _mixer_fwd(x, w, tabs, "mix")
        x, s3 = _ffn_fwd(x, w['ffn2_norm'], w['ffn2_w_in'], w['ffn2_w_out'], "ffn2")
        saved.append((s1, s2, s3))
    dy, sq = _loss_head(x, target, "loss_head")
    loss = 0.5 * jnp.sum(sq) / D_MODEL
    grads = [None] * DEPTH
    for l in reversed(range(DEPTH)):
        w = ws[l]
        s1, s2, s3 = saved[l]
        dy, dn2, dwi2, dwo2 = _ffn_bwd(dy, s3, w['ffn2_norm'], w['ffn2_w_in'], w['ffn2_w_out'], "ffn2")
        dy, g = _mixer_bwd(dy, s2, w, tabs, "mix")
        dy, dn1, dwi1, dwo1 = _ffn_bwd(dy, s1, w['ffn1_norm'], w['ffn1_w_in'], w['ffn1_w_out'], "ffn1")
        g.update(ffn1_norm=dn1, ffn1_w_in=dwi1, ffn1_w_out=dwo1, ffn2_norm=dn2, ffn2_w_in=dwi2, ffn2_w_out=dwo2)
        grads[l] = g
    return loss, dy, grads


SMALL_PACK = SMALL_ORDER + ['ssd_conv_w']


def _pack_small(per_layer_rows):
    flat = jnp.concatenate([per_layer_rows[l][n].reshape(-1).astype(F32) for l in range(DEPTH) for n in SMALL_PACK])
    rows = -(-flat.shape[0] // LANES)
    rows = -(-rows // 8) * 8
    return jnp.pad(flat, (0, rows * LANES - flat.shape[0])).reshape(rows, LANES)


def _unpack_small(buf, shapes):
    flat = buf.reshape(-1)
    off = 0
    out = {n: [] for n in SMALL_PACK}
    for l in range(DEPTH):
        for n in SMALL_PACK:
            size = int(np.prod(shapes[n]))
            out[n].append(flat[off:off + size].reshape(shapes[n]))
            off += size
    return {n: jnp.stack(v) for n, v in out.items()}


def kernel(x, positions, ffn1_norm, ffn1_w_in, ffn1_w_out, mix_norm, w_in, gm_v_norm, gm_w_s, gm_b_s, mla_q_norm, mla_kv_norm, mla_w_uq, mla_w_ukv, mla_q_gain, mla_k_gain, ssd_conv_w, ssd_conv_b, ssd_dt_bias, ssd_a_log, ssd_d, ssd_norm, w_branch, w_out, ffn2_norm, ffn2_w_in, ffn2_w_out, loss_target, m_ffn1_norm, m_ffn1_w_in, m_ffn1_w_out, m_mix_norm, m_w_in, m_gm_v_norm, m_gm_w_s, m_gm_b_s, m_mla_q_norm, m_mla_kv_norm, m_mla_w_uq, m_mla_w_ukv, m_mla_q_gain, m_mla_k_gain, m_ssd_conv_w, m_ssd_conv_b, m_ssd_dt_bias, m_ssd_a_log, m_ssd_d, m_ssd_norm, m_w_branch, m_w_out, m_ffn2_norm, m_ffn2_w_in, m_ffn2_w_out, v_ffn1_norm, v_ffn1_w_in, v_ffn1_w_out, v_mix_norm, v_w_in, v_gm_v_norm, v_gm_w_s, v_gm_b_s, v_mla_q_norm, v_mla_kv_norm, v_mla_w_uq, v_mla_w_ukv, v_mla_q_gain, v_mla_k_gain, v_ssd_conv_w, v_ssd_conv_b, v_ssd_dt_bias, v_ssd_a_log, v_ssd_d, v_ssd_norm, v_w_branch, v_w_out, v_ffn2_norm, v_ffn2_w_in, v_ffn2_w_out):
    wts = dict(zip(WEIGHTS, (ffn1_norm, ffn1_w_in, ffn1_w_out, mix_norm, w_in, gm_v_norm, gm_w_s, gm_b_s, mla_q_norm, mla_kv_norm,
                             mla_w_uq, mla_w_ukv, mla_q_gain, mla_k_gain, ssd_conv_w, ssd_conv_b, ssd_dt_bias, ssd_a_log, ssd_d,
                             ssd_norm, w_branch, w_out, ffn2_norm, ffn2_w_in, ffn2_w_out)))
    mom = dict(zip(WEIGHTS, (m_ffn1_norm, m_ffn1_w_in, m_ffn1_w_out, m_mix_norm, m_w_in, m_gm_v_norm, m_gm_w_s, m_gm_b_s, m_mla_q_norm,
                             m_mla_kv_norm, m_mla_w_uq, m_mla_w_ukv, m_mla_q_gain, m_mla_k_gain, m_ssd_conv_w, m_ssd_conv_b,
                             m_ssd_dt_bias, m_ssd_a_log, m_ssd_d, m_ssd_norm, m_w_branch, m_w_out, m_ffn2_norm, m_ffn2_w_in,
                             m_ffn2_w_out)))
    var = dict(zip(WEIGHTS, (v_ffn1_norm, v_ffn1_w_in, v_ffn1_w_out, v_mix_norm, v_w_in, v_gm_v_norm, v_gm_w_s, v_gm_b_s, v_mla_q_norm,
                             v_mla_kv_norm, v_mla_w_uq, v_mla_w_ukv, v_mla_q_gain, v_mla_k_gain, v_ssd_conv_w, v_ssd_conv_b,
                             v_ssd_dt_bias, v_ssd_a_log, v_ssd_d, v_ssd_norm, v_w_branch, v_w_out, v_ffn2_norm, v_ffn2_w_in,
                             v_ffn2_w_out)))
    cx, cy, _ = _me()
    mychip = _chip_index(cx, cy)

    gathered = _gather_shards([wts[n].astype(BF16) for n in SHARDED_ORDER])
    full = dict(zip(SHARDED_ORDER, gathered))
    small = {n: wts[n] for n in SMALL_ORDER}

    loss_part, dx, grads = _local_step(x[0], positions[0], loss_target[0], full, small)
    loss = lax.psum(loss_part, ("x", "y", "c"))

    def rows_cols(a, lead):
        return a.reshape(a.shape[:lead] + (int(np.prod(a.shape[lead:-1])), a.shape[-1]))

    gs = [rows_cols(jnp.stack([grads[l][n] for l in range(DEPTH)]), 2) for n in REDUCED]
    got = _pair_exchange(gs)
    sums = [_pair_add(g, r, f"pair_add_{n}") for g, r, n in zip(gs, got, REDUCED)]
    arrived = _chip_exchange([s16 for s16, _ in sums])
    halves = [_chip_add(own, a, f"chip_add_{n}") for (_, own), a, n in zip(sums, arrived, REDUCED)]
    theirs = _pair_share(halves)
    shapes = {n: wts[n].shape[1:] for n in SMALL_ORDER}
    shapes['ssd_conv_w'] = SHARDED['ssd_conv_w'][0]
    small_g = _unpack_small(_sum_slots(_all_exchange(_pack_small(grads)), "small_sum"), shapes)
    conv_full = small_g.pop('ssd_conv_w')
    shard_cols = _shard_shape('ssd_conv_w')[1]
    small_g['ssd_conv_w'] = lax.dynamic_slice_in_dim(conv_full, mychip * shard_cols, shard_cols, axis=2)
    shapes['ssd_conv_w'] = _shard_shape('ssd_conv_w')

    grad, delta, new_m, new_v = {}, {}, {}, {}
    for n, a, b in zip(REDUCED, halves, theirs):
        shp = wts[n].shape
        outs = _adamw_sharded(rows_cols(wts[n], 1), rows_cols(mom[n], 1), rows_cols(var[n], 1), a, b, f"adamw_{n}")
        grad[n], delta[n], new_m[n], new_v[n] = [o.reshape(shp) for o in outs]
    per_layer = lambda t: [{n: t[n][l] for n in SMALL_PACK} for l in range(DEPTH)]
    d, nm, nv = _adamw(_pack_small(per_layer(wts)), _pack_small(per_layer(small_g)), _pack_small(per_layer(mom)),
                       _pack_small(per_layer(var)), "adamw_small")
    sd, snm, snv = _unpack_small(d, shapes), _unpack_small(nm, shapes), _unpack_small(nv, shapes)
    for n in SMALL_PACK:
        grad[n], delta[n], new_m[n], new_v[n] = small_g[n], sd[n], snm[n], snv[n]
    return (loss, dx[None], *[grad[n] for n in WEIGHTS], *[delta[n] for n in WEIGHTS], *[new_m[n] for n in WEIGHTS],
            *[new_v[n] for n in WEIGHTS])
```

```python
import functools
import math

import numpy as np
import jax
import jax.numpy as jnp
from jax import lax
from jax.experimental import pallas as pl
from jax.experimental.pallas import tpu as pltpu

F32, BF16 = jnp.float32, jnp.bfloat16
MESH = pl.DeviceIdType.MESH

D_MODEL, DEPTH, D_FF, EPS = 1024, 4, 2816, 1e-6
GM_WIDTH, GM_GROUPS, CHUNK = 512, 4, 128
MLA_HEADS, MLA_Q_RANK, MLA_KV_RANK, MLA_NOPE, MLA_ROPE, MLA_V = 8, 384, 256, 64, 32, 64
MLA_QK = MLA_NOPE + MLA_ROPE
ROPE_THETA = 10000.0
SSD_HEADS, SSD_HEAD_DIM, SSD_GROUPS, SSD_STATE, SSD_CONV = 8, 64, 2, 128, 4
SSD_INNER = SSD_HEADS * SSD_HEAD_DIM
IN_COLS = 6312
LANES = 128
ADAM_LR, ADAM_B1, ADAM_B2, ADAM_EPS, ADAM_WD, ADAM_STEP = 0.001, 0.9, 0.999, 1e-08, 0.01, 10

C_UV, C_Z, C_XS, C_G, PW_MAIN = 0, 1024, 2048, 3072, 6144
T_BC, T_CKV, T_CQ, T_KR, T_DT, PW_TAIL = 0, 512, 768, 1152, 1280, 1536
HP = MLA_HEADS * LANES
FC = 2 * D_FF // 4

WEIGHTS = ['ffn1_norm', 'ffn1_w_in', 'ffn1_w_out', 'mix_norm', 'w_in', 'gm_v_norm', 'gm_w_s', 'gm_b_s', 'mla_q_norm',
           'mla_kv_norm', 'mla_w_uq', 'mla_w_ukv', 'mla_q_gain', 'mla_k_gain', 'ssd_conv_w', 'ssd_conv_b', 'ssd_dt_bias',
           'ssd_a_log', 'ssd_d', 'ssd_norm', 'w_branch', 'w_out', 'ffn2_norm', 'ffn2_w_in', 'ffn2_w_out']
SHARDED = {'ffn1_w_in': ((1024, 5632), 1), 'ffn1_w_out': ((2816, 1024), 0), 'w_in': ((1024, 6312), 1),
           'mla_w_uq': ((384, 768), 1), 'mla_w_ukv': ((256, 1024), 1), 'ssd_conv_w': ((4, 1024), 1),
           'w_branch': ((3, 512, 1024), 2), 'w_out': ((1024, 1024), 0), 'ffn2_w_in': ((1024, 5632), 1),
           'ffn2_w_out': ((2816, 1024), 0)}
SHARDED_ORDER = [n for n in WEIGHTS if n in SHARDED]
SMALL_ORDER = [n for n in WEIGHTS if n not in SHARDED]
REDUCED = [n for n in SHARDED_ORDER if n != 'ssd_conv_w']
N_CHIPS = 4
HALF_L = DEPTH // 2


def _shard_shape(name):
    shape, ax = SHARDED[name]
    return tuple(d // N_CHIPS if i == ax else d for i, d in enumerate(shape))


def _pick(dim, target):
    if dim <= target:
        return dim
    t = (target // LANES) * LANES
    while t >= LANES:
        if dim % t == 0:
            return t
        t -= LANES
    return dim


def _sigmoid(x):
    return 1.0 / (1.0 + jnp.exp(-x))


def _params(*sem):
    return pltpu.CompilerParams(dimension_semantics=sem, vmem_limit_bytes=56 * 1024 * 1024)


def _matmul(a, b, *, ta=False, tb=False, out_dtype=F32, scale=1.0, res=None, name):
    if ta:
        k_dim, m_dim = a.shape
    else:
        m_dim, k_dim = a.shape
    if tb:
        n_dim, k2 = b.shape
    else:
        k2, n_dim = b.shape
    assert k_dim == k2, (a.shape, b.shape, ta, tb)
    tm, tn, tk = _pick(m_dim, 1024), _pick(n_dim, 1024), _pick(k_dim, 1024)
    nk = k_dim // tk
    dn = (((0 if ta else 1,), (1 if tb else 0,)), ((), ()))

    def body(*refs):
        if res is not None:
            a_ref, b_ref, r_ref, o_ref, acc = refs
        else:
            a_ref, b_ref, o_ref, acc = refs
        k = pl.program_id(2)

        @pl.when(k == 0)
        def _():
            acc[...] = jnp.zeros_like(acc)

        acc[...] += lax.dot_general(a_ref[...].astype(BF16), b_ref[...].astype(BF16), dn, preferred_element_type=F32)

        @pl.when(k == nk - 1)
        def _():
            r = acc[...]
            if scale != 1.0:
                r = r * scale
            if res is not None:
                r = r + r_ref[...]
            o_ref[...] = r.astype(out_dtype)

    a_spec = pl.BlockSpec((tk, tm), lambda j, i, k: (k, i)) if ta else pl.BlockSpec((tm, tk), lambda j, i, k: (i, k))
    b_spec = pl.BlockSpec((tn, tk), lambda j, i, k: (j, k)) if tb else pl.BlockSpec((tk, tn), lambda j, i, k: (k, j))
    in_specs = [a_spec, b_spec]
    args = [a, b]
    if res is not None:
        in_specs.append(pl.BlockSpec((tm, tn), lambda j, i, k: (i, j)))
        args.append(res)
    return pl.pallas_call(
        body, name=name, grid=(n_dim // tn, m_dim // tm, nk), in_specs=in_specs,
        out_specs=pl.BlockSpec((tm, tn), lambda j, i, k: (i, j)),
        out_shape=jax.ShapeDtypeStruct((m_dim, n_dim), out_dtype),
        scratch_shapes=[pltpu.VMEM((tm, tn), F32)],
        compiler_params=_params("parallel", "parallel", "arbitrary"))(*args)


def _rmsnorm_fwd(x, gain, name):
    s, d = x.shape
    tm = _pick(s, 512)

    def body(x_ref, g_ref, o_ref):
        xv = x_ref[...]
        r = lax.rsqrt(jnp.mean(xv * xv, axis=-1, keepdims=True) + EPS)
        o_ref[...] = (xv * r * g_ref[...]).astype(BF16)

    return pl.pallas_call(
        body, name=name, grid=(s // tm,),
        in_specs=[pl.BlockSpec((tm, d), lambda i: (i, 0)), pl.BlockSpec((1, d), lambda i: (0, 0))],
        out_specs=pl.BlockSpec((tm, d), lambda i: (i, 0)),
        out_shape=jax.ShapeDtypeStruct((s, d), BF16), compiler_params=_params("parallel"))(x, gain)


def _rmsnorm_bwd(x, gain, dh, dres, name):
    s, d = x.shape
    tm = _pick(s, 512)

    def body(x_ref, g_ref, dh_ref, dr_ref, dx_ref, dg_ref):
        @pl.when(pl.program_id(0) == 0)
        def _():
            dg_ref[...] = jnp.zeros_like(dg_ref)

        xv, dhv = x_ref[...], dh_ref[...]
        r = lax.rsqrt(jnp.mean(xv * xv, axis=-1, keepdims=True) + EPS)
        u = dhv * g_ref[...]
        dx_ref[...] = dr_ref[...] + r * u - xv * (r * r * r) * jnp.mean(xv * u, axis=-1, keepdims=True)
        dg_ref[...] += jnp.sum(dhv * xv * r, axis=0, keepdims=True)

    row = pl.BlockSpec((tm, d), lambda i: (i, 0))
    vec = pl.BlockSpec((1, d), lambda i: (0, 0))
    return pl.pallas_call(
        body, name=name, grid=(s // tm,), in_specs=[row, vec, row, row], out_specs=[row, vec],
        out_shape=[jax.ShapeDtypeStruct((s, d), F32), jax.ShapeDtypeStruct((1, d), F32)],
        compiler_params=_params("arbitrary"))(x, gain, dh, dres)


_NT = (((1,), (1,)), ((), ()))
_TN = (((0,), (0,)), ((), ()))


def _resident(shape):
    return pl.BlockSpec(shape, lambda *_: tuple(0 for _ in shape), pipeline_mode=pl.Buffered(1))


def _ffn_in(x, gain, w4, name):
    s, d = x.shape
    tm = _pick(s, 512)

    def body(x_ref, g_ref, w_ref, h_ref, gate_ref, up_ref, act_ref):
        xv = x_ref[...]
        r = lax.rsqrt(jnp.mean(xv * xv, axis=-1, keepdims=True) + EPS)
        h = (xv * r * g_ref[...]).astype(BF16)
        h_ref[...] = h
        for j in range(2):
            g16 = jnp.dot(h, w_ref[j], preferred_element_type=F32).astype(BF16)
            u16 = jnp.dot(h, w_ref[j + 2], preferred_element_type=F32).astype(BF16)
            gate_ref[j] = g16
            up_ref[j] = u16
            gf, uf = g16.astype(F32), u16.astype(F32)
            act_ref[j] = (gf * _sigmoid(gf) * uf).astype(BF16)

    half = pl.BlockSpec((2, tm, FC), lambda i: (0, i, 0))
    return pl.pallas_call(
        body, name=name, grid=(s // tm,),
        in_specs=[pl.BlockSpec((tm, d), lambda i: (i, 0)), pl.BlockSpec((1, d), lambda i: (0, 0)), _resident((4, d, FC))],
        out_specs=[pl.BlockSpec((tm, d), lambda i: (i, 0)), half, half, half],
        out_shape=[jax.ShapeDtypeStruct((s, d), BF16)] + [jax.ShapeDtypeStruct((2, s, FC), BF16)] * 3,
        compiler_params=_params("parallel"))(x, gain, w4)


def _ffn_out(act, w_out, x, name):
    s, d = x.shape
    tm = _pick(s, 512)

    def body(a_ref, w_ref, x_ref, o_ref):
        acc = jnp.dot(a_ref[0], w_ref[0:FC, :], preferred_element_type=F32)
        acc = acc + jnp.dot(a_ref[1], w_ref[FC:2 * FC, :], preferred_element_type=F32)
        o_ref[...] = x_ref[...] + 0.5 * acc

    row = pl.BlockSpec((tm, d), lambda i: (i, 0))
    return pl.pallas_call(
        body, name=name, grid=(s // tm,),
        in_specs=[pl.BlockSpec((2, tm, FC), lambda i: (0, i, 0)), _resident((2 * FC, d)), row], out_specs=row,
        out_shape=jax.ShapeDtypeStruct((s, d), F32), compiler_params=_params("parallel"))(act, w_out, x)


def _ffn_dact(dy, w_out, gate, up, name):
    s, d = dy.shape
    tm = _pick(s, 512)

    def body(dy_ref, w_ref, g_ref, u_ref, o_ref):
        dy16 = dy_ref[...].astype(BF16)
        for j in range(2):
            dact = 0.5 * lax.dot_general(dy16, w_ref[j * FC:(j + 1) * FC, :], _NT, preferred_element_type=F32)
            g, u = g_ref[j].astype(F32), u_ref[j].astype(F32)
            sg = _sigmoid(g)
            o_ref[j] = (dact * u * (sg * (1.0 + g * (1.0 - sg)))).astype(BF16)
            o_ref[j + 2] = (dact * g * sg).astype(BF16)

    half = pl.BlockSpec((2, tm, FC), lambda i: (0, i, 0))
    return pl.pallas_call(
        body, name=name, grid=(s // tm,),
        in_specs=[pl.BlockSpec((tm, d), lambda i: (i, 0)), _resident((2 * FC, d)), half, half],
        out_specs=pl.BlockSpec((4, tm, FC), lambda i: (0, i, 0)),
        out_shape=jax.ShapeDtypeStruct((4, s, FC), BF16), compiler_params=_params("parallel"))(dy, w_out, gate, up)


def _ffn_dwout(act, dy, name):
    s, d = dy.shape
    tk = _pick(s, 1024)
    nk = s // tk

    def body(a_ref, dy_ref, o_ref):
        k = pl.program_id(1)

        @pl.when(k == 0)
        def _():
            o_ref[...] = jnp.zeros_like(o_ref)

        o_ref[...] += lax.dot_general(a_ref[...], dy_ref[...].astype(BF16), _TN, preferred_element_type=F32)

        @pl.when(k == nk - 1)
        def _():
            o_ref[...] = 0.5 * o_ref[...]

    return pl.pallas_call(
        body, name=name, grid=(2, nk),
        in_specs=[pl.BlockSpec((None, tk, FC), lambda j, k: (j, k, 0)), pl.BlockSpec((tk, d), lambda j, k: (k, 0))],
        out_specs=pl.BlockSpec((FC, d), lambda j, k: (j, 0)), out_shape=jax.ShapeDtypeStruct((2 * FC, d), F32),
        compiler_params=_params("parallel", "arbitrary"))(act, dy)


def _ffn_dwin(h, da, name):
    s, d = h.shape
    tk = _pick(s, 1024)

    def body(h_ref, da_ref, o_ref):
        @pl.when(pl.program_id(1) == 0)
        def _():
            o_ref[...] = jnp.zeros_like(o_ref)

        o_ref[...] += lax.dot_general(h_ref[...], da_ref[...], _TN, preferred_element_type=F32)

    return pl.pallas_call(
        body, name=name, grid=(4, s // tk),
        in_specs=[pl.BlockSpec((tk, d), lambda j, k: (k, 0)), pl.BlockSpec((None, tk, FC), lambda j, k: (j, k, 0))],
        out_specs=pl.BlockSpec((None, d, FC), lambda j, k: (j, 0, 0)), out_shape=jax.ShapeDtypeStruct((4, d, FC), F32),
        compiler_params=_params("parallel", "arbitrary"))(h, da)


def _ffn_dx(da, w4, x, gain, dy, name):
    s, d = x.shape
    tm = _pick(s, 512)

    def body(da_ref, w_ref, x_ref, g_ref, dy_ref, dx_ref, dg_ref):
        @pl.when(pl.program_id(0) == 0)
        def _():
            dg_ref[...] = jnp.zeros_like(dg_ref)

        dh = jnp.zeros((tm, d), F32)
        for j in range(4):
            dh = dh + lax.dot_general(da_ref[j], w_ref[j], _NT, preferred_element_type=F32)
        xv = x_ref[...]
        r = lax.rsqrt(jnp.mean(xv * xv, axis=-1, keepdims=True) + EPS)
        u = dh * g_ref[...]
        dx_ref[...] = dy_ref[...] + r * u - xv * (r * r * r) * jnp.mean(xv * u, axis=-1, keepdims=True)
        dg_ref[...] += jnp.sum(dh * xv * r, axis=0, keepdims=True)

    row = pl.BlockSpec((tm, d), lambda i: (i, 0))
    vec = pl.BlockSpec((1, d), lambda i: (0, 0))
    return pl.pallas_call(
        body, name=name, grid=(s // tm,),
        in_specs=[pl.BlockSpec((4, tm, FC), lambda i: (0, i, 0)), _resident((4, d, FC)), row, vec, row],
        out_specs=[row, vec], out_shape=[jax.ShapeDtypeStruct((s, d), F32), jax.ShapeDtypeStruct((1, d), F32)],
        compiler_params=_params("arbitrary"))(da, w4, x, gain, dy)


_INV_SQRT2 = 0.7071067811865476
_INV_SQRT2PI = 0.3989422804014327


def _gelu(x):
    return 0.5 * x * (1.0 + lax.erf(x * _INV_SQRT2))


def _gelu_grad(x):
    return 0.5 * (1.0 + lax.erf(x * _INV_SQRT2)) + x * jnp.exp(-0.5 * x * x) * _INV_SQRT2PI


def _tril_mask():
    r = lax.broadcasted_iota(jnp.int32, (CHUNK, CHUNK), 0)
    c = lax.broadcasted_iota(jnp.int32, (CHUNK, CHUNK), 1)
    return r >= c


def _gmlp_fwd(p, v_gain, w_s, b_full, name):
    s = p.shape[0]
    tm = _pick(s, 512)
    nch = tm // CHUNK

    def body(uv_ref, g_ref, w_ref, b_ref, o_ref):
        gel = _gelu(uv_ref[...].astype(F32))
        u, v = gel[:, :GM_WIDTH], gel[:, GM_WIDTH:]
        r = lax.rsqrt(jnp.mean(v * v, axis=-1, keepdims=True) + EPS)
        vn = (v * r * g_ref[...]).astype(BF16)
        mask = _tril_mask()
        for g in range(GM_GROUPS):
            wm = jnp.where(mask, w_ref[g], 0.0).astype(BF16)
            for c in range(nch):
                rs, cs = slice(c * CHUNK, (c + 1) * CHUNK), slice(g * LANES, (g + 1) * LANES)
                sp = jnp.dot(wm, vn[rs, cs], preferred_element_type=F32) + b_ref[g]
                o_ref[rs, cs] = (u[rs, cs] * sp).astype(BF16)

    full3 = pl.BlockSpec((GM_GROUPS, CHUNK, CHUNK), lambda i: (0, 0, 0))
    return pl.pallas_call(
        body, name=name, grid=(s // tm,),
        in_specs=[pl.BlockSpec((tm, 2 * GM_WIDTH), lambda i: (i, C_UV // (2 * GM_WIDTH))),
                  pl.BlockSpec((1, GM_WIDTH), lambda i: (0, 0)), full3, full3],
        out_specs=pl.BlockSpec((tm, GM_WIDTH), lambda i: (i, 0)),
        out_shape=jax.ShapeDtypeStruct((s, GM_WIDTH), BF16), compiler_params=_params("parallel"))(p, v_gain, w_s, b_full)


def _gmlp_bwd(p, v_gain, w_s, b_full, dy, name):
    s = p.shape[0]
    tm = _pick(s, 512)
    nch = tm // CHUNK
    nsteps = s // tm

    def body(uv_ref, g_ref, w_ref, b_ref, dy_ref, duv_ref, dg_ref, dw_ref, db_ref, dvn_s, dbacc):
        step = pl.program_id(0)

        @pl.when(step == 0)
        def _():
            dg_ref[...] = jnp.zeros_like(dg_ref)
            dw_ref[...] = jnp.zeros_like(dw_ref)
            dbacc[...] = jnp.zeros_like(dbacc)

        uv = uv_ref[...].astype(F32)
        gel = _gelu(uv)
        u, v = gel[:, :GM_WIDTH], gel[:, GM_WIDTH:]
        r = lax.rsqrt(jnp.mean(v * v, axis=-1, keepdims=True) + EPS)
        gain = g_ref[...]
        vn32 = v * r * gain
        vn = vn32.astype(BF16)
        dy = dy_ref[...]
        mask = _tril_mask()
        for g in range(GM_GROUPS):
            wm = jnp.where(mask, w_ref[g], 0.0).astype(BF16)
            dwg = jnp.zeros((CHUNK, CHUNK), F32)
            dbg = jnp.zeros((CHUNK, LANES), F32)
            for c in range(nch):
                rs, cs = slice(c * CHUNK, (c + 1) * CHUNK), slice(g * LANES, (g + 1) * LANES)
                sp = jnp.dot(wm, vn[rs, cs], preferred_element_type=F32) + b_ref[g]
                dyc = dy[rs, cs]
                dsp = dyc * u[rs, cs]
                dsp16 = dsp.astype(BF16)
                duv_ref[rs, cs] = (dyc * sp * _gelu_grad(uv[rs, cs])).astype(BF16)
                dvn_s[rs, cs] = lax.dot_general(wm, dsp16, (((0,), (0,)), ((), ())), preferred_element_type=F32)
                dwg = dwg + lax.dot_general(dsp16, vn[rs, cs], (((1,), (1,)), ((), ())), preferred_element_type=F32)
                dbg = dbg + dsp
            dw_ref[g] += jnp.where(mask, dwg, 0.0)
            dbacc[:, g * LANES:(g + 1) * LANES] += dbg
        dvn = dvn_s[...]
        uu = dvn * gain
        dv = r * uu - v * (r * r * r) * jnp.mean(v * uu, axis=-1, keepdims=True)
        duv_ref[:, GM_WIDTH:] = (dv * _gelu_grad(uv[:, GM_WIDTH:])).astype(BF16)
        dg_ref[...] += jnp.sum(dvn * v * r, axis=0, keepdims=True)

        @pl.when(step == nsteps - 1)
        def _():
            for g in range(GM_GROUPS):
                db_ref[:, g:g + 1] = jnp.sum(dbacc[:, g * LANES:(g + 1) * LANES], axis=1, keepdims=True)

    full3 = pl.BlockSpec((GM_GROUPS, CHUNK, CHUNK), lambda i: (0, 0, 0))
    return pl.pallas_call(
        body, name=name, grid=(nsteps,),
        in_specs=[pl.BlockSpec((tm, 2 * GM_WIDTH), lambda i: (i, C_UV // (2 * GM_WIDTH))),
                  pl.BlockSpec((1, GM_WIDTH), lambda i: (0, 0)), full3, full3,
                  pl.BlockSpec((tm, GM_WIDTH), lambda i: (i, 0))],
        out_specs=[pl.BlockSpec((tm, 2 * GM_WIDTH), lambda i: (i, 0)), pl.BlockSpec((1, GM_WIDTH), lambda i: (0, 0)),
                   full3, pl.BlockSpec((CHUNK, GM_GROUPS), lambda i: (0, 0))],
        out_shape=[jax.ShapeDtypeStruct((s, 2 * GM_WIDTH), BF16), jax.ShapeDtypeStruct((1, GM_WIDTH), F32),
                   jax.ShapeDtypeStruct((GM_GROUPS, CHUNK, CHUNK), F32), jax.ShapeDtypeStruct((CHUNK, GM_GROUPS), F32)],
        scratch_shapes=[pltpu.VMEM((tm, GM_WIDTH), F32), pltpu.VMEM((CHUNK, GM_WIDTH), F32)],
        compiler_params=_params("arbitrary"))(p, v_gain, w_s, b_full, dy)


def _rope(x, ct, s1, s2):
    return x * ct + pltpu.roll(x, LANES - MLA_ROPE // 2, 1) * s1 + pltpu.roll(x, MLA_ROPE // 2, 1) * s2


def _rope_bwd(d, ct, s1, s2):
    return d * ct + pltpu.roll(d * s1, MLA_ROPE // 2, 1) + pltpu.roll(d * s2, LANES - MLA_ROPE // 2, 1)


def _head_norm(x, gain):
    r = lax.rsqrt(jnp.sum(x * x, axis=-1, keepdims=True) * (1.0 / MLA_QK) + EPS)
    return x * r * gain, r


def _head_norm_bwd(x, r, gain, d):
    u = d * gain
    return r * u - x * (r * r * r) * (jnp.sum(x * u, axis=-1, keepdims=True) * (1.0 / MLA_QK))


def _mla_specs(tm):
    cq = pl.BlockSpec((tm, MLA_Q_RANK), lambda i: (i, T_CQ // MLA_Q_RANK))
    ckv = pl.BlockSpec((tm, MLA_KV_RANK), lambda i: (i, T_CKV // MLA_KV_RANK))
    kr = pl.BlockSpec((tm, LANES), lambda i: (i, T_KR // LANES))
    tab = pl.BlockSpec((tm, LANES), lambda i: (i, 0))
    return cq, ckv, kr, tab


def _const(shape):
    return pl.BlockSpec(shape, lambda i: tuple(0 for _ in shape))


def _mla_pre_fwd(p, tabs, qn_g, kvn_g, wuq, wkv, gq, gk, name):
    s = p.shape[0]
    tm = _pick(s, 256)
    ct, s1, s2 = tabs

    def body(cq_ref, ckv_ref, kr_ref, ct_ref, s1_ref, s2_ref, qg_ref, kvg_ref, wuq_ref, wkv_ref, gq_ref, gk_ref,
             q_ref, k_ref, v_ref):
        cq, ckv, kr = cq_ref[...], ckv_ref[...], kr_ref[...]
        ctv, s1v, s2v = ct_ref[...], s1_ref[...], s2_ref[...]
        rq = lax.rsqrt(jnp.mean(cq * cq, axis=-1, keepdims=True) + EPS)
        q = jnp.dot((cq * rq * qg_ref[...]).astype(BF16), wuq_ref[...], preferred_element_type=F32)
        rk = lax.rsqrt(jnp.mean(ckv * ckv, axis=-1, keepdims=True) + EPS)
        kv = jnp.dot((ckv * rk * kvg_ref[...]).astype(BF16), wkv_ref[...], preferred_element_type=F32)
        v_ref[...] = kv[:, HP:].astype(BF16)
        for h in range(MLA_HEADS):
            hs = slice(h * LANES, (h + 1) * LANES)
            qh, _ = _head_norm(q[:, hs], gq_ref[...])
            q_ref[:, hs] = (_rope(qh, ctv, s1v, s2v) * _ATT_SCALE).astype(BF16)
            kh, _ = _head_norm(kv[:, hs] + kr, gk_ref[...])
            k_ref[:, hs] = _rope(kh, ctv, s1v, s2v).astype(BF16)

    cq_s, ckv_s, kr_s, tab_s = _mla_specs(tm)
    out = pl.BlockSpec((tm, HP), lambda i: (i, 0))
    return pl.pallas_call(
        body, name=name, grid=(s // tm,),
        in_specs=[cq_s, ckv_s, kr_s, tab_s, tab_s, tab_s, _const((1, MLA_Q_RANK)), _const((1, MLA_KV_RANK)),
                  _const((MLA_Q_RANK, HP)), _const((MLA_KV_RANK, 2 * HP)), _const((1, LANES)), _const((1, LANES))],
        out_specs=[out, out, out], out_shape=[jax.ShapeDtypeStruct((s, HP), BF16)] * 3,
        compiler_params=_params("parallel"))(p, p, p, ct, s1, s2, qn_g, kvn_g, wuq, wkv, gq, gk)


def _mla_pre_bwd(p, tabs, qn_g, kvn_g, wuq, wkv, gq, gk, dq, dk, dv, name):
    s = p.shape[0]
    tm = _pick(s, 256)
    ct, s1, s2 = tabs

    def body(cq_ref, ckv_ref, kr_ref, ct_ref, s1_ref, s2_ref, qg_ref, kvg_ref, wuq_ref, wkv_ref, gq_ref, gk_ref,
             dq_ref, dk_ref, dv_ref, dcq_ref, dckv_ref, dkr_ref, dwuq_ref, dwkv_ref, dqg_ref, dkvg_ref, dgq_ref, dgk_ref,
             dqp, dkvp):
        @pl.when(pl.program_id(0) == 0)
        def _():
            for ref in (dwuq_ref, dwkv_ref, dqg_ref, dkvg_ref, dgq_ref, dgk_ref):
                ref[...] = jnp.zeros_like(ref)

        cq, ckv, kr = cq_ref[...], ckv_ref[...], kr_ref[...]
        ctv, s1v, s2v = ct_ref[...], s1_ref[...], s2_ref[...]
        rq = lax.rsqrt(jnp.mean(cq * cq, axis=-1, keepdims=True) + EPS)
        qn = (cq * rq * qg_ref[...]).astype(BF16)
        q = jnp.dot(qn, wuq_ref[...], preferred_element_type=F32)
        rk = lax.rsqrt(jnp.mean(ckv * ckv, axis=-1, keepdims=True) + EPS)
        kvn = (ckv * rk * kvg_ref[...]).astype(BF16)
        kv = jnp.dot(kvn, wkv_ref[...], preferred_element_type=F32)
        gqv, gkv = gq_ref[...], gk_ref[...]
        dgq = jnp.zeros((1, LANES), F32)
        dgk = jnp.zeros((1, LANES), F32)
        dkr = jnp.zeros((tm, LANES), F32)
        for h in range(MLA_HEADS):
            hs = slice(h * LANES, (h + 1) * LANES)
            xq = q[:, hs]
            _, r = _head_norm(xq, gqv)
            d = _rope_bwd(dq_ref[:, hs], ctv, s1v, s2v)
            dgq = dgq + jnp.sum(d * xq * r, axis=0, keepdims=True)
            dqp[:, hs] = _head_norm_bwd(xq, r, gqv, d)
            xk = kv[:, hs] + kr
            _, r = _head_norm(xk, gkv)
            d = _rope_bwd(dk_ref[:, hs], ctv, s1v, s2v)
            dgk = dgk + jnp.sum(d * xk * r, axis=0, keepdims=True)
            dxk = _head_norm_bwd(xk, r, gkv, d)
            dkvp[:, hs] = dxk
            dkr = dkr + dxk
        dkvp[:, HP:] = dv_ref[...]
        dgq_ref[...] += dgq
        dgk_ref[...] += dgk
        dkr_ref[...] = dkr.astype(BF16)
        tn = (((0,), (0,)), ((), ()))
        nt = (((1,), (1,)), ((), ()))
        dq16 = dqp[...].astype(BF16)
        dwuq_ref[...] += lax.dot_general(qn, dq16, tn, preferred_element_type=F32)
        dqn = lax.dot_general(dq16, wuq_ref[...], nt, preferred_element_type=F32)
        dqg_ref[...] += jnp.sum(dqn * cq * rq, axis=0, keepdims=True)
        u = dqn * qg_ref[...]
        dcq_ref[...] = (rq * u - cq * (rq * rq * rq) * jnp.mean(cq * u, axis=-1, keepdims=True)).astype(BF16)
        dkv16 = dkvp[...].astype(BF16)
        dwkv_ref[...] += lax.dot_general(kvn, dkv16, tn, preferred_element_type=F32)
        dkvn = lax.dot_general(dkv16, wkv_ref[...], nt, preferred_element_type=F32)
        dkvg_ref[...] += jnp.sum(dkvn * ckv * rk, axis=0, keepdims=True)
        u = dkvn * kvg_ref[...]
        dckv_ref[...] = (rk * u - ckv * (rk * rk * rk) * jnp.mean(ckv * u, axis=-1, keepdims=True)).astype(BF16)

    cq_s, ckv_s, kr_s, tab_s = _mla_specs(tm)
    hd = pl.BlockSpec((tm, HP), lambda i: (i, 0))
    return pl.pallas_call(
        body, name=name, grid=(s // tm,),
        in_specs=[cq_s, ckv_s, kr_s, tab_s, tab_s, tab_s, _const((1, MLA_Q_RANK)), _const((1, MLA_KV_RANK)),
                  _const((MLA_Q_RANK, HP)), _const((MLA_KV_RANK, 2 * HP)), _const((1, LANES)), _const((1, LANES)),
                  hd, hd, hd],
        out_specs=[pl.BlockSpec((tm, MLA_Q_RANK), lambda i: (i, 0)), pl.BlockSpec((tm, MLA_KV_RANK), lambda i: (i, 0)),
                   pl.BlockSpec((tm, LANES), lambda i: (i, 0)), _const((MLA_Q_RANK, HP)), _const((MLA_KV_RANK, 2 * HP)),
                   _const((1, MLA_Q_RANK)), _const((1, MLA_KV_RANK)), _const((1, LANES)), _const((1, LANES))],
        out_shape=[jax.ShapeDtypeStruct((s, MLA_Q_RANK), BF16), jax.ShapeDtypeStruct((s, MLA_KV_RANK), BF16),
                   jax.ShapeDtypeStruct((s, LANES), BF16), jax.ShapeDtypeStruct((MLA_Q_RANK, HP), F32),
                   jax.ShapeDtypeStruct((MLA_KV_RANK, 2 * HP), F32), jax.ShapeDtypeStruct((1, MLA_Q_RANK), F32),
                   jax.ShapeDtypeStruct((1, MLA_KV_RANK), F32), jax.ShapeDtypeStruct((1, LANES), F32),
                   jax.ShapeDtypeStruct((1, LANES), F32)],
        scratch_shapes=[pltpu.VMEM((tm, HP), F32), pltpu.VMEM((tm, 2 * HP), F32)],
        compiler_params=_params("arbitrary"))(p, p, p, ct, s1, s2, qn_g, kvn_g, wuq, wkv, gq, gk, dq, dk, dv)


_ATT_SCALE = MLA_QK ** -0.5
_NEG = -1e30
_NT = (((1,), (1,)), ((), ()))
_TN = (((0,), (0,)), ((), ()))


def _tri_rows(step, n):
    i = step * 0
    for m in range(1, n):
        i = i + (step >= m * (m + 1) // 2).astype(jnp.int32)
    return i, step - i * (i + 1) // 2


def _tri_cols(step, n):
    j = step * 0
    for m in range(1, n):
        j = j + (step >= m * n - m * (m - 1) // 2).astype(jnp.int32)
    return j, j + step - (j * n - j * (j - 1) // 2)


def _diag_mask(t):
    return lax.broadcasted_iota(jnp.int32, (t, t), 0) <= lax.broadcasted_iota(jnp.int32, (t, t), 1)


def _attn_fwd(q, k, v, name):
    s = q.shape[0]
    t = _pick(s, 512)
    n = s // t

    def body(q_ref, k_ref, v_ref, o_ref, lse_ref, m_s, l_s, acc):
        i, j = _tri_rows(pl.program_id(1), n)

        @pl.when(j == 0)
        def _():
            m_s[...] = jnp.full_like(m_s, _NEG)
            l_s[...] = jnp.zeros_like(l_s)
            acc[...] = jnp.zeros_like(acc)

        def step(diagonal):
            sc = lax.dot_general(k_ref[...], q_ref[...], _NT, preferred_element_type=F32)
            if diagonal:
                sc = jnp.where(_diag_mask(t), sc, _NEG)
            m_new = jnp.maximum(m_s[...], jnp.max(sc, axis=0, keepdims=True))
            alpha = jnp.exp(m_s[...] - m_new)
            pr = jnp.exp(sc - m_new)
            l_s[...] = alpha * l_s[...] + jnp.sum(pr, axis=0, keepdims=True)
            acc[...] = alpha * acc[...] + lax.dot_general(v_ref[...], pr.astype(BF16), _TN, preferred_element_type=F32)
            m_s[...] = m_new

        @pl.when(j < i)
        def _():
            step(False)

        @pl.when(j == i)
        def _():
            step(True)
            o_ref[...] = (acc[...] / l_s[...]).T
            lse_ref[...] = m_s[...] + jnp.log(l_s[...])

    qs = pl.BlockSpec((t, LANES), lambda h, p: (_tri_rows(p, n)[0], h))
    ks = pl.BlockSpec((t, LANES), lambda h, p: (_tri_rows(p, n)[1], h))
    return pl.pallas_call(
        body, name=name, grid=(MLA_HEADS, n * (n + 1) // 2), in_specs=[qs, ks, ks],
        out_specs=[qs, pl.BlockSpec((None, 1, t), lambda h, p: (h, 0, _tri_rows(p, n)[0]))],
        out_shape=[jax.ShapeDtypeStruct((s, HP), F32), jax.ShapeDtypeStruct((MLA_HEADS, 1, s), F32)],
        scratch_shapes=[pltpu.VMEM((1, t), F32), pltpu.VMEM((1, t), F32), pltpu.VMEM((LANES, t), F32)],
        compiler_params=_params("parallel", "arbitrary"))(q, k, v)


def _attn_bwd_dq(q, k, v, o, lse, do, name):
    s = q.shape[0]
    t = _pick(s, 512)
    n = s // t

    def body(q_ref, k_ref, v_ref, o_ref, lse_ref, do_ref, dq_ref, dl_ref, acc, dl_s):
        i, j = _tri_rows(pl.program_id(1), n)

        @pl.when(j == 0)
        def _():
            acc[...] = jnp.zeros_like(acc)
            dl_s[...] = jnp.sum((do_ref[...] * o_ref[...]).T, axis=0, keepdims=True)

        def step(diagonal):
            sc = lax.dot_general(k_ref[...], q_ref[...], _NT, preferred_element_type=F32)
            if diagonal:
                sc = jnp.where(_diag_mask(t), sc, _NEG)
            pr = jnp.exp(sc - lse_ref[...])
            dp = lax.dot_general(v_ref[...], do_ref[...].astype(BF16), _NT, preferred_element_type=F32)
            ds = (pr * (dp - dl_s[...])).astype(BF16)
            acc[...] += lax.dot_general(ds, k_ref[...], _TN, preferred_element_type=F32)

        @pl.when(j < i)
        def _():
            step(False)

        @pl.when(j == i)
        def _():
            step(True)
            dq_ref[...] = acc[...] * _ATT_SCALE
            dl_ref[...] = dl_s[...]

    qs = pl.BlockSpec((t, LANES), lambda h, p: (_tri_rows(p, n)[0], h))
    ks = pl.BlockSpec((t, LANES), lambda h, p: (_tri_rows(p, n)[1], h))
    ls = pl.BlockSpec((None, 1, t), lambda h, p: (h, 0, _tri_rows(p, n)[0]))
    return pl.pallas_call(
        body, name=name, grid=(MLA_HEADS, n * (n + 1) // 2), in_specs=[qs, ks, ks, qs, ls, qs], out_specs=[qs, ls],
        out_shape=[jax.ShapeDtypeStruct((s, HP), F32), jax.ShapeDtypeStruct((MLA_HEADS, 1, s), F32)],
        scratch_shapes=[pltpu.VMEM((t, LANES), F32), pltpu.VMEM((1, t), F32)],
        compiler_params=_params("parallel", "arbitrary"))(q, k, v, o, lse, do)


def _attn_bwd_dkv(q, k, v, lse, delta, do, name):
    s = q.shape[0]
    t = _pick(s, 512)
    n = s // t

    def body(q_ref, k_ref, v_ref, lse_ref, dl_ref, do_ref, dk_ref, dv_ref, dk_acc, dv_acc):
        j, i = _tri_cols(pl.program_id(1), n)

        def step(diagonal):
            sc = lax.dot_general(k_ref[...], q_ref[...], _NT, preferred_element_type=F32)
            if diagonal:
                sc = jnp.where(_diag_mask(t), sc, _NEG)
            pr = jnp.exp(sc - lse_ref[...])
            do16 = do_ref[...].astype(BF16)
            dv_acc[...] += jnp.dot(pr.astype(BF16), do16, preferred_element_type=F32)
            dp = lax.dot_general(v_ref[...], do16, _NT, preferred_element_type=F32)
            ds = (pr * (dp - dl_ref[...])).astype(BF16)
            dk_acc[...] += jnp.dot(ds, q_ref[...], preferred_element_type=F32)

        @pl.when(i == j)
        def _():
            dk_acc[...] = jnp.zeros_like(dk_acc)
            dv_acc[...] = jnp.zeros_like(dv_acc)
            step(True)

        @pl.when(i > j)
        def _():
            step(False)

        @pl.when(i == n - 1)
        def _():
            dk_ref[...] = dk_acc[...]
            dv_ref[...] = dv_acc[...]

    qs = pl.BlockSpec((t, LANES), lambda h, p: (_tri_cols(p, n)[1], h))
    ks = pl.BlockSpec((t, LANES), lambda h, p: (_tri_cols(p, n)[0], h))
    ls = pl.BlockSpec((None, 1, t), lambda h, p: (h, 0, _tri_cols(p, n)[1]))
    return pl.pallas_call(
        body, name=name, grid=(MLA_HEADS, n * (n + 1) // 2), in_specs=[qs, ks, ks, ls, ls, qs], out_specs=[ks, ks],
        out_shape=[jax.ShapeDtypeStruct((s, HP), F32)] * 2,
        scratch_shapes=[pltpu.VMEM((t, LANES), F32), pltpu.VMEM((t, LANES), F32)],
        compiler_params=_params("parallel", "arbitrary"))(q, k, v, lse, delta, do)


XBC = HP + 2 * SSD_GROUPS * SSD_STATE
BCW = 2 * SSD_GROUPS * SSD_STATE


def _conv_fwd(p, col0, width, conv_w, conv_b, name):
    s = p.shape[0]
    c0, nblk = col0 // LANES, width // LANES

    def body(x_ref, w_ref, b_ref, o_ref, pad):
        pad[0:8, :] = jnp.zeros((8, LANES), F32)
        pad[8:s + 8, :] = x_ref[...].astype(F32)
        acc = jnp.broadcast_to(b_ref[...], (s, LANES))
        for t in range(SSD_CONV):
            acc = acc + pad[pl.ds(8 - (SSD_CONV - 1) + t, s), :] * w_ref[t:t + 1, :]
        o_ref[...] = acc * _sigmoid(acc)

    return pl.pallas_call(
        body, name=name, grid=(nblk,),
        in_specs=[pl.BlockSpec((s, LANES), lambda j: (0, c0 + j)), pl.BlockSpec((SSD_CONV, LANES), lambda j: (0, j)),
                  pl.BlockSpec((1, LANES), lambda j: (0, j))],
        out_specs=pl.BlockSpec((s, LANES), lambda j: (0, j)), out_shape=jax.ShapeDtypeStruct((s, width), F32),
        scratch_shapes=[pltpu.VMEM((s + 8, LANES), F32)], compiler_params=_params("parallel"))(p, conv_w, conv_b)


def _conv_bwd(p, col0, width, conv_w, conv_b, dact, name):
    s = p.shape[0]
    c0, nblk = col0 // LANES, width // LANES

    def body(x_ref, w_ref, b_ref, d_ref, dx_ref, dw_ref, db_ref, pad, padd):
        pad[0:8, :] = jnp.zeros((8, LANES), F32)
        pad[8:s + 8, :] = x_ref[...].astype(F32)
        acc = jnp.broadcast_to(b_ref[...], (s, LANES))
        for t in range(SSD_CONV):
            acc = acc + pad[pl.ds(8 - (SSD_CONV - 1) + t, s), :] * w_ref[t:t + 1, :]
        sg = _sigmoid(acc)
        dpre = d_ref[...] * (sg * (1.0 + acc * (1.0 - sg)))
        padd[0:s, :] = dpre
        padd[s:s + 8, :] = jnp.zeros((8, LANES), F32)
        dx = jnp.zeros((s, LANES), F32)
        for t in range(SSD_CONV):
            dx = dx + padd[pl.ds(SSD_CONV - 1 - t, s), :] * w_ref[t:t + 1, :]
            dw_ref[t:t + 1, :] = jnp.sum(dpre * pad[pl.ds(8 - (SSD_CONV - 1) + t, s), :], axis=0, keepdims=True)
        dx_ref[...] = dx.astype(BF16)
        db_ref[...] = jnp.sum(dpre, axis=0, keepdims=True)

    blk = pl.BlockSpec((s, LANES), lambda j: (0, j))
    return pl.pallas_call(
        body, name=name, grid=(nblk,),
        in_specs=[pl.BlockSpec((s, LANES), lambda j: (0, c0 + j)), pl.BlockSpec((SSD_CONV, LANES), lambda j: (0, j)),
                  pl.BlockSpec((1, LANES), lambda j: (0, j)), blk],
        out_specs=[blk, pl.BlockSpec((SSD_CONV, LANES), lambda j: (0, j)), pl.BlockSpec((1, LANES), lambda j: (0, j))],
        out_shape=[jax.ShapeDtypeStruct((s, width), BF16), jax.ShapeDtypeStruct((SSD_CONV, width), F32),
                   jax.ShapeDtypeStruct((1, width), F32)],
        scratch_shapes=[pltpu.VMEM((s + 8, LANES), F32), pltpu.VMEM((s + 8, LANES), F32)],
        compiler_params=_params("parallel"))(p, conv_w, conv_b, dact)


def _softplus(x):
    return jnp.maximum(x, 0.0) + jnp.log(1.0 + jnp.exp(-jnp.abs(x)))


def _dt_fwd(p, dt_bias, a_log, name):
    s = p.shape[0]
    tm = _pick(s, 512)

    def body(x_ref, b_ref, a_ref, dt_ref, da_ref):
        dtv = _softplus(x_ref[...] + b_ref[...])
        dav = dtv * (-jnp.exp(a_ref[...]))
        for h in range(SSD_HEADS):
            hs = slice(h * LANES, (h + 1) * LANES)
            dt_ref[:, hs] = jnp.broadcast_to(dtv[:, h:h + 1], (tm, LANES))
            da_ref[:, hs] = jnp.broadcast_to(dav[:, h:h + 1], (tm, LANES))

    out = pl.BlockSpec((tm, HP), lambda i: (i, 0))
    return pl.pallas_call(
        body, name=name, grid=(s // tm,),
        in_specs=[pl.BlockSpec((tm, LANES), lambda i: (i, T_DT // LANES)), _const((1, LANES)), _const((1, LANES))],
        out_specs=[out, out], out_shape=[jax.ShapeDtypeStruct((s, HP), F32)] * 2,
        compiler_params=_params("parallel"))(p, dt_bias, a_log)


def _dt_bwd(p, dt_bias, a_log, dda, ddtx, name):
    s = p.shape[0]
    tm = _pick(s, 512)

    def body(x_ref, b_ref, a_ref, dda_ref, ddtx_ref, dx_ref, db_ref, dal_ref):
        @pl.when(pl.program_id(0) == 0)
        def _():
            db_ref[...] = jnp.zeros_like(db_ref)
            dal_ref[...] = jnp.zeros_like(dal_ref)

        x = x_ref[...] + b_ref[...]
        dtv = _softplus(x)
        av = -jnp.exp(a_ref[...])
        lane = lax.broadcasted_iota(jnp.int32, (tm, LANES), 1)
        pa = jnp.zeros((tm, LANES), F32)
        px = jnp.zeros((tm, LANES), F32)
        for h in range(SSD_HEADS):
            pa = jnp.where(lane == h, dda_ref[:, h * LANES:(h + 1) * LANES], pa)
            px = jnp.where(lane == h, ddtx_ref[:, h * LANES:(h + 1) * LANES], px)
        draw = (pa * av + px) * _sigmoid(x)
        dx_ref[...] = draw.astype(BF16)
        db_ref[...] += jnp.sum(draw, axis=0, keepdims=True)
        dal_ref[...] += jnp.sum(pa * dtv, axis=0, keepdims=True) * av

    hd = pl.BlockSpec((tm, HP), lambda i: (i, 0))
    return pl.pallas_call(
        body, name=name, grid=(s // tm,),
        in_specs=[pl.BlockSpec((tm, LANES), lambda i: (i, T_DT // LANES)), _const((1, LANES)), _const((1, LANES)), hd, hd],
        out_specs=[pl.BlockSpec((tm, LANES), lambda i: (i, 0)), _const((1, LANES)), _const((1, LANES))],
        out_shape=[jax.ShapeDtypeStruct((s, LANES), BF16), jax.ShapeDtypeStruct((1, LANES), F32),
                   jax.ShapeDtypeStruct((1, LANES), F32)],
        compiler_params=_params("arbitrary"))(p, dt_bias, a_log, dda, ddtx)


def _cumsum_rows(x):
    row = lax.broadcasted_iota(jnp.int32, x.shape, 0)
    k = 1
    while k < x.shape[0]:
        x = x + jnp.where(row >= k, pltpu.roll(x, k, 0), 0.0)
        k *= 2
    return x


def _rev_cumsum_rows(x):
    n = x.shape[0]
    row = lax.broadcasted_iota(jnp.int32, x.shape, 0)
    k = 1
    while k < n:
        x = x + jnp.where(row < n - k, pltpu.roll(x, n - k, 0), 0.0)
        k *= 2
    return x


HPG = SSD_HEADS // SSD_GROUPS


def _chunk_decay(da):
    cs = _cumsum_rows(da)
    lm = jnp.exp(jnp.where(_tril_mask(), cs - cs.T, _NEG))
    return cs, lm, cs[CHUNK - 1:CHUNK, :]


def _scan_fwd(xs, bc, dtb, dab, name):
    s = xs.shape[0]
    nc = s // CHUNK

    def body(x_ref, b_ref, c_ref, dt_ref, da_ref, y_ref, sin_ref, state):
        @pl.when(pl.program_id(1) == 0)
        def _():
            state[...] = jnp.zeros_like(state)

        bv = b_ref[...]
        b16, c16 = bv.astype(BF16), c_ref[...].astype(BF16)
        g = lax.dot_general(c16, b16, _NT, preferred_element_type=F32)
        for hh in range(HPG):
            hs = slice(hh * LANES, (hh + 1) * LANES)
            st = state[hh]
            sin_ref[hh] = st
            cs, lm, cl = _chunk_decay(da_ref[:, hs])
            xd = (x_ref[:, hs] * dt_ref[:, hs]).astype(BF16)
            y = jnp.dot((g * lm).astype(BF16), xd, preferred_element_type=F32)
            y_ref[:, hs] = y + jnp.dot(c16, st.astype(BF16), preferred_element_type=F32) * jnp.exp(cs)
            bd = (bv * jnp.exp(cl - cs)).astype(BF16)
            state[hh] = jnp.exp(cl) * st + lax.dot_general(bd, xd, _TN, preferred_element_type=F32)

    gw = HPG * LANES
    hd = pl.BlockSpec((CHUNK, gw), lambda g, c: (c, g))
    return pl.pallas_call(
        body, name=name, grid=(SSD_GROUPS, nc),
        in_specs=[hd, pl.BlockSpec((CHUNK, LANES), lambda g, c: (c, g)),
                  pl.BlockSpec((CHUNK, LANES), lambda g, c: (c, SSD_GROUPS + g)), hd, hd],
        out_specs=[hd, pl.BlockSpec((HPG, None, SSD_STATE, LANES), lambda g, c: (g, c, 0, 0))],
        out_shape=[jax.ShapeDtypeStruct((s, HP), F32), jax.ShapeDtypeStruct((SSD_HEADS, nc, SSD_STATE, LANES), F32)],
        scratch_shapes=[pltpu.VMEM((HPG, SSD_STATE, LANES), F32)],
        compiler_params=_params("parallel", "arbitrary"))(xs, bc, bc, dtb, dab)


def _scan_bwd(xs, bc, dtb, dab, s_in, dy, d_vec, name):
    s = xs.shape[0]
    nc = s // CHUNK

    def body(x_ref, b_ref, c_ref, dt_ref, da_ref, sin_ref, dy_ref, dv_ref, dx_ref, db_ref, dc_ref, dda_ref, ddtx_ref, dstate):
        @pl.when(pl.program_id(1) == 0)
        def _():
            dstate[...] = jnp.zeros_like(dstate)

        bv = b_ref[...]
        b16, c16 = bv.astype(BF16), c_ref[...].astype(BF16)
        g = lax.dot_general(c16, b16, _NT, preferred_element_type=F32)
        row = lax.broadcasted_iota(jnp.int32, (CHUNK, 1), 0)
        dbm = jnp.zeros((CHUNK, SSD_STATE), F32)
        dcm = jnp.zeros((CHUNK, SSD_STATE), F32)
        for hh in range(HPG):
            hs = slice(hh * LANES, (hh + 1) * LANES)
            st, ds = sin_ref[hh], dstate[hh]
            st16, ds16 = st.astype(BF16), ds.astype(BF16)
            xv, dtv, dyv = x_ref[:, hs], dt_ref[:, hs], dy_ref[:, hs]
            cs, lm, cl = _chunk_decay(da_ref[:, hs])
            ecs, ecl = jnp.exp(cs), jnp.exp(cl)
            decay = jnp.exp(cl - cs)
            xd = (xv * dtv).astype(BF16)
            dy16 = dyv.astype(BF16)
            dye = (dyv * ecs).astype(BF16)
            yoff = jnp.dot(c16, st16, preferred_element_type=F32) * ecs
            dcs = jnp.sum(dyv * yoff, axis=-1, keepdims=True)
            dcm = dcm + lax.dot_general(dye, st16, _NT, preferred_element_type=F32)
            dstate[hh] = ecl * ds + lax.dot_general(c16, dye, _TN, preferred_element_type=F32)
            dcl = jnp.sum(jnp.sum(ds * st, axis=0, keepdims=True), axis=1, keepdims=True) * ecl[:, 0:1]
            bd32 = bv * decay
            qm = lax.dot_general(xd, ds16, _NT, preferred_element_type=F32)
            dbm = dbm + qm * decay
            w = jnp.sum(bd32 * qm, axis=-1, keepdims=True)
            dcs = dcs - w
            dcl = dcl + jnp.sum(w, axis=0, keepdims=True)
            dxd = jnp.dot(bd32.astype(BF16), ds16, preferred_element_type=F32)
            m16 = (g * lm).astype(BF16)
            dm = lax.dot_general(dy16, xd, _NT, preferred_element_type=F32)
            dxd = dxd + lax.dot_general(m16, dy16, _TN, preferred_element_type=F32)
            dg = dm * lm
            dg16 = dg.astype(BF16)
            tt = dg * g
            dcm = dcm + jnp.dot(dg16, b16, preferred_element_type=F32)
            dbm = dbm + lax.dot_general(dg16, c16, _TN, preferred_element_type=F32)
            dcs = dcs + jnp.sum(tt, axis=-1, keepdims=True) - jnp.sum(tt.T, axis=-1, keepdims=True)
            dcs = dcs + jnp.where(row == CHUNK - 1, dcl, 0.0)
            dda_ref[:, hs] = _rev_cumsum_rows(jnp.broadcast_to(dcs, (CHUNK, LANES)))
            ddtx_ref[:, hs] = jnp.broadcast_to(jnp.sum(dxd * xv, axis=-1, keepdims=True), (CHUNK, LANES))
            dx_ref[:, hs] = dxd * dtv + dyv * dv_ref[:, hs]
        db_ref[...] = dbm
        dc_ref[...] = dcm

    gw = HPG * LANES
    hd = pl.BlockSpec((CHUNK, gw), lambda g, c: (nc - 1 - c, g))
    gp = pl.BlockSpec((CHUNK, LANES), lambda g, c: (nc - 1 - c, g))
    return pl.pallas_call(
        body, name=name, grid=(SSD_GROUPS, nc),
        in_specs=[hd, gp, pl.BlockSpec((CHUNK, LANES), lambda g, c: (nc - 1 - c, SSD_GROUPS + g)), hd, hd,
                  pl.BlockSpec((HPG, None, SSD_STATE, LANES), lambda g, c: (g, nc - 1 - c, 0, 0)), hd,
                  pl.BlockSpec((1, gw), lambda g, c: (0, g))],
        out_specs=[hd, gp, gp, hd, hd],
        out_shape=[jax.ShapeDtypeStruct((s, HP), F32), jax.ShapeDtypeStruct((s, SSD_GROUPS * SSD_STATE), F32),
                   jax.ShapeDtypeStruct((s, SSD_GROUPS * SSD_STATE), F32), jax.ShapeDtypeStruct((s, HP), F32),
                   jax.ShapeDtypeStruct((s, HP), F32)],
        scratch_shapes=[pltpu.VMEM((HPG, SSD_STATE, LANES), F32)],
        compiler_params=_params("parallel", "arbitrary"))(xs, bc, bc, dtb, dab, s_in, dy, d_vec)


_GN = SSD_INNER // SSD_GROUPS
_GW = HP // SSD_GROUPS


def _ssd_post_fwd(y, xbc, p, d_vec, gain, name):
    s = y.shape[0]
    tm = _pick(s, 512)

    def body(y_ref, x_ref, z_ref, d_ref, g_ref, o_ref):
        z = z_ref[...].astype(F32)
        y2 = (y_ref[...] + x_ref[...] * d_ref[...]) * (z * _sigmoid(z))
        for g in range(SSD_GROUPS):
            gs = slice(g * _GW, (g + 1) * _GW)
            yg = y2[:, gs]
            r = lax.rsqrt(jnp.sum(yg * yg, axis=-1, keepdims=True) * (1.0 / _GN) + EPS)
            o_ref[:, gs] = (yg * r * g_ref[:, gs]).astype(BF16)

    hd = pl.BlockSpec((tm, HP), lambda i: (i, 0))
    return pl.pallas_call(
        body, name=name, grid=(s // tm,),
        in_specs=[hd, hd, pl.BlockSpec((tm, HP), lambda i: (i, C_Z // HP)), _const((1, HP)), _const((1, HP))],
        out_specs=hd, out_shape=jax.ShapeDtypeStruct((s, HP), BF16), compiler_params=_params("parallel"))(y, xbc, p, d_vec, gain)


def _ssd_post_bwd(y, xbc, p, d_vec, gain, dyn, name):
    s = y.shape[0]
    tm = _pick(s, 512)

    def body(y_ref, x_ref, z_ref, d_ref, g_ref, dn_ref, dy_ref, dz_ref, dg_ref, dd_ref):
        @pl.when(pl.program_id(0) == 0)
        def _():
            dg_ref[...] = jnp.zeros_like(dg_ref)
            dd_ref[...] = jnp.zeros_like(dd_ref)

        z, xv = z_ref[...].astype(F32), x_ref[...]
        sg = _sigmoid(z)
        sz = z * sg
        yt = y_ref[...] + xv * d_ref[...]
        y2 = yt * sz
        for g in range(SSD_GROUPS):
            gs = slice(g * _GW, (g + 1) * _GW)
            yg, dn = y2[:, gs], dn_ref[:, gs]
            r = lax.rsqrt(jnp.sum(yg * yg, axis=-1, keepdims=True) * (1.0 / _GN) + EPS)
            u = dn * g_ref[:, gs]
            dy2 = r * u - yg * (r * r * r) * (jnp.sum(yg * u, axis=-1, keepdims=True) * (1.0 / _GN))
            dg_ref[:, gs] += jnp.sum(dn * yg * r, axis=0, keepdims=True)
            dyt = dy2 * sz[:, gs]
            dy_ref[:, gs] = dyt
            dz_ref[:, gs] = (dy2 * yt[:, gs] * (sg[:, gs] * (1.0 + z[:, gs] * (1.0 - sg[:, gs])))).astype(BF16)
            dd_ref[:, gs] += jnp.sum(dyt * xv[:, gs], axis=0, keepdims=True)

    hd = pl.BlockSpec((tm, HP), lambda i: (i, 0))
    return pl.pallas_call(
        body, name=name, grid=(s // tm,),
        in_specs=[hd, hd, pl.BlockSpec((tm, HP), lambda i: (i, C_Z // HP)), _const((1, HP)), _const((1, HP)), hd],
        out_specs=[hd, hd, _const((1, HP)), _const((1, HP))],
        out_shape=[jax.ShapeDtypeStruct((s, HP), F32), jax.ShapeDtypeStruct((s, HP), BF16),
                   jax.ShapeDtypeStruct((1, HP), F32), jax.ShapeDtypeStruct((1, HP), F32)],
        compiler_params=_params("arbitrary"))(y, xbc, p, d_vec, gain, dyn)


def _merge_fwd(p, ya, o, yc, wb0, wb1, wb2, w_out, x, name):
    s = p.shape[0]
    tm = _pick(s, 512)

    def body(g_ref, ya_ref, o_ref, yc_ref, w0_ref, w1_ref, w2_ref, wo_ref, x_ref, mg_ref, y_ref):
        acc = jnp.zeros((tm, D_MODEL), F32)
        for i, (b_ref, w_ref) in enumerate(((ya_ref, w0_ref), (o_ref, w1_ref), (yc_ref, w2_ref))):
            t = jnp.dot(b_ref[...].astype(BF16), w_ref[...], preferred_element_type=F32)
            acc = acc + _sigmoid(g_ref[:, i * D_MODEL:(i + 1) * D_MODEL].astype(F32)) * t
        mg = acc.astype(BF16)
        mg_ref[...] = mg
        y_ref[...] = x_ref[...] + jnp.dot(mg, wo_ref[...], preferred_element_type=F32)

    row = pl.BlockSpec((tm, D_MODEL), lambda i: (i, 0))
    return pl.pallas_call(
        body, name=name, grid=(s // tm,),
        in_specs=[pl.BlockSpec((tm, 3 * D_MODEL), lambda i: (i, C_G // (3 * D_MODEL))),
                  pl.BlockSpec((tm, GM_WIDTH), lambda i: (i, 0)), row, row,
                  _resident((GM_WIDTH, D_MODEL)), _resident((HP, D_MODEL)), _resident((HP, D_MODEL)),
                  _resident((D_MODEL, D_MODEL)), row],
        out_specs=[row, row],
        out_shape=[jax.ShapeDtypeStruct((s, D_MODEL), BF16), jax.ShapeDtypeStruct((s, D_MODEL), F32)],
        compiler_params=_params("parallel"))(p, ya, o, yc, wb0, wb1, wb2, w_out, x)


def _merge_bwd(p, ya, o, yc, wb0, wb1, wb2, w_out, dy, name):
    s = p.shape[0]
    tm = _pick(s, 512)

    def body(g_ref, ya_ref, o_ref, yc_ref, w0_ref, w1_ref, w2_ref, wo_ref, dy_ref,
             d0_ref, d1_ref, d2_ref, dg_ref, dya_ref, do_ref, dyc_ref):
        dm = lax.dot_general(dy_ref[...].astype(BF16), wo_ref[...], _NT, preferred_element_type=F32)
        for i, (b_ref, w_ref, d_ref, db_ref) in enumerate(((ya_ref, w0_ref, d0_ref, dya_ref), (o_ref, w1_ref, d1_ref, do_ref),
                                                            (yc_ref, w2_ref, d2_ref, dyc_ref))):
            cs = slice(i * D_MODEL, (i + 1) * D_MODEL)
            t = jnp.dot(b_ref[...].astype(BF16), w_ref[...], preferred_element_type=F32)
            sg = _sigmoid(g_ref[:, cs].astype(F32))
            dt16 = (dm * sg).astype(BF16)
            d_ref[...] = dt16
            dg_ref[:, cs] = (dm * t * sg * (1.0 - sg)).astype(BF16)
            db_ref[...] = lax.dot_general(dt16, w_ref[...], _NT, preferred_element_type=F32)

    row = pl.BlockSpec((tm, D_MODEL), lambda i: (i, 0))
    nar = pl.BlockSpec((tm, GM_WIDTH), lambda i: (i, 0))
    wide = pl.BlockSpec((tm, 3 * D_MODEL), lambda i: (i, 0))
    return pl.pallas_call(
        body, name=name, grid=(s // tm,),
        in_specs=[pl.BlockSpec((tm, 3 * D_MODEL), lambda i: (i, C_G // (3 * D_MODEL))), nar, row, row,
                  _resident((GM_WIDTH, D_MODEL)), _resident((HP, D_MODEL)), _resident((HP, D_MODEL)),
                  _resident((D_MODEL, D_MODEL)), row],
        out_specs=[row, row, row, wide, nar, row, row],
        out_shape=[jax.ShapeDtypeStruct((s, D_MODEL), BF16)] * 3 + [jax.ShapeDtypeStruct((s, 3 * D_MODEL), BF16),
                   jax.ShapeDtypeStruct((s, GM_WIDTH), F32), jax.ShapeDtypeStruct((s, D_MODEL), F32),
                   jax.ShapeDtypeStruct((s, D_MODEL), F32)],
        compiler_params=_params("parallel"))(p, ya, o, yc, wb0, wb1, wb2, w_out, dy)


def _loss_head(y, target, name):
    s, d = y.shape
    tm = _pick(s, 512)

    def body(y_ref, t_ref, dy_ref, sq_ref):
        @pl.when(pl.program_id(0) == 0)
        def _():
            sq_ref[...] = jnp.zeros_like(sq_ref)

        e = y_ref[...] - t_ref[...]
        dy_ref[...] = e * (1.0 / d)
        sq_ref[...] += jnp.sum(e * e, axis=0, keepdims=True)

    row = pl.BlockSpec((tm, d), lambda i: (i, 0))
    return pl.pallas_call(
        body, name=name, grid=(s // tm,), in_specs=[row, row], out_specs=[row, _const((1, d))],
        out_shape=[jax.ShapeDtypeStruct((s, d), F32), jax.ShapeDtypeStruct((1, d), F32)],
        compiler_params=_params("arbitrary"))(y, target)


def _adamw(w, g, m, v, name):
    rows, cols = w.shape
    tr = rows
    for cand in (512, 256, 128, 64, 32, 16, 8):
        if rows % cand == 0 and cand * cols * 4 <= 3 * 1024 * 1024:
            tr = cand
            break

    def body(w_ref, g_ref, m_ref, v_ref, d_ref, nm_ref, nv_ref):
        d_ref[...], nm_ref[...], nv_ref[...] = _adam_update(w_ref[...], g_ref[...], m_ref[...], v_ref[...])

    blk = pl.BlockSpec((tr, cols), lambda i: (i, 0))
    return pl.pallas_call(
        body, name=name, grid=(rows // tr,), in_specs=[blk] * 4, out_specs=[blk] * 3,
        out_shape=[jax.ShapeDtypeStruct((rows, cols), F32)] * 3, compiler_params=_params("parallel"))(w, g, m, v)


def _adam_update(w, g, m, v):
    nm = ADAM_B1 * m + (1.0 - ADAM_B1) * g
    nv = ADAM_B2 * v + (1.0 - ADAM_B2) * (g * g)
    c1 = 1.0 - ADAM_B1 ** ADAM_STEP
    c2 = 1.0 - ADAM_B2 ** ADAM_STEP
    return -ADAM_LR * ((nm / c1) / (jnp.sqrt(nv / c2) + ADAM_EPS) + ADAM_WD * w), nm, nv


def _adamw_sharded(w, m, v, mine, theirs, name):
    depth, rows, cols = w.shape
    tr = _row_tile(rows, cols, 1024 * 1024)

    def body(w_ref, m_ref, v_ref, a_ref, b_ref, g_ref, d_ref, nm_ref, nv_ref):
        c = lax.axis_index("c")
        g = jnp.where(pl.program_id(0) // HALF_L == c, a_ref[...], b_ref[...])
        g_ref[...] = g
        d_ref[...], nm_ref[...], nv_ref[...] = _adam_update(w_ref[...], g, m_ref[...], v_ref[...])

    blk = pl.BlockSpec((None, tr, cols), lambda l, i: (l, i, 0))
    half = pl.BlockSpec((None, tr, cols), lambda l, i: (l % HALF_L, i, 0))
    return pl.pallas_call(
        body, name=name, grid=(depth, rows // tr), in_specs=[blk, blk, blk, half, half], out_specs=[blk] * 4,
        out_shape=[jax.ShapeDtypeStruct((depth, rows, cols), F32)] * 4,
        compiler_params=_params("parallel", "parallel"))(w, m, v, mine, theirs)


ANY = pl.BlockSpec(memory_space=pl.ANY)


def _me():
    return lax.axis_index("x"), lax.axis_index("y"), lax.axis_index("c")


def _other_chips(x, y):
    return [(1 - x, y), (x, 1 - y), (1 - x, 1 - y)]


def _chip_index(cx, cy):
    return 2 * cx + cy


def _gather_shards(shards):
    n = len(shards)

    def body(*refs):
        in_refs, out_refs = refs[:n], refs[n:2 * n]
        send_sems, recv_sems = refs[2 * n:]
        x, y, c = _me()
        sib = (x, y, 1 - c)
        chips = _other_chips(x, y)
        mychip = _chip_index(x, y)

        def part(t, chip, hc):
            return out_refs[t].at[pl.ds(hc * HALF_L, HALF_L), chip]

        def copy(t, k, chip, hc, to, src=None):
            dst = part(t, chip, hc)
            return pltpu.make_async_remote_copy(src_ref=dst if src is None else src, dst_ref=dst, send_sem=send_sems.at[6 * t + k],
                                                recv_sem=recv_sems.at[6 * t + k], device_id=to, device_id_type=MESH)

        first = [copy(t, j, mychip, c, (*chip, c), src=in_refs[t].at[pl.ds(c * HALF_L, HALF_L)])
                 for j, chip in enumerate(chips) for t in range(n)]
        for cp in first:
            cp.start()
        passed = []
        for j, chip in enumerate(chips):
            for t in range(n):
                copy(t, j, _chip_index(*chip), c, (x, y, c)).wait_recv()
                cp = copy(t, 3 + j, _chip_index(*chip), c, sib)
                cp.start()
                passed.append(cp)
        for j, chip in enumerate(chips):
            for t in range(n):
                copy(t, 3 + j, _chip_index(*chip), 1 - c, (x, y, c)).wait_recv()
        for cp in first + passed:
            cp.wait_send()

    got = pl.pallas_call(
        body, name="gather_shards", in_specs=[ANY] * n, out_specs=[ANY] * n,
        out_shape=[jax.ShapeDtypeStruct((DEPTH, N_CHIPS) + a.shape[1:], a.dtype) for a in shards],
        scratch_shapes=[pltpu.SemaphoreType.DMA((6 * n,)), pltpu.SemaphoreType.DMA((6 * n,))])(*shards)
    cx, cy, _ = _me()
    return [lax.dynamic_update_index_in_dim(g, a, _chip_index(cx, cy), 1) for g, a in zip(got, shards)]


def _pair_exchange(gs):
    n = len(gs)

    def body(*refs):
        in_refs, out_refs = refs[:n], refs[n:2 * n]
        send_sems, recv_sems = refs[2 * n:]
        x, y, c = _me()
        cps = [pltpu.make_async_remote_copy(src_ref=in_refs[t].at[pl.ds((1 - c) * HALF_L, HALF_L)], dst_ref=out_refs[t],
                                            send_sem=send_sems.at[t], recv_sem=recv_sems.at[t], device_id=(x, y, 1 - c),
                                            device_id_type=MESH) for t in range(n)]
        for cp in cps:
            cp.start()
        for cp in cps:
            cp.wait()

    return pl.pallas_call(
        body, name="pair_exchange", in_specs=[ANY] * n, out_specs=[ANY] * n,
        out_shape=[jax.ShapeDtypeStruct((HALF_L,) + a.shape[1:], a.dtype) for a in gs],
        scratch_shapes=[pltpu.SemaphoreType.DMA((n,)), pltpu.SemaphoreType.DMA((n,))])(*gs)


def _row_tile(rows, cols, budget=2 * 1024 * 1024):
    best = None
    for t in range(8, rows + 1, 8):
        if rows % t == 0 and t * cols * 4 <= budget:
            best = t
    return best or rows


def _pair_add(g, got, name):
    _, _, rows, cols = g.shape
    tr = _row_tile(rows, cols)

    def body(lo_ref, hi_ref, r_ref, o16_ref, own_ref):
        x, y, c = _me()
        tot = jnp.where(c == 0, lo_ref[...], hi_ref[...]) + r_ref[...]
        o16_ref[...] = tot.astype(BF16)

        @pl.when(pl.program_id(2) == _chip_index(x, y))
        def _():
            own_ref[...] = tot

    blk = (None, None, tr, cols)
    return pl.pallas_call(
        body, name=name, grid=(HALF_L, rows // tr, N_CHIPS),
        in_specs=[pl.BlockSpec(blk, lambda a, i, k: (a, k, i, 0)), pl.BlockSpec(blk, lambda a, i, k: (a + HALF_L, k, i, 0)),
                  pl.BlockSpec(blk, lambda a, i, k: (a, k, i, 0))],
        out_specs=[pl.BlockSpec(blk, lambda a, i, k: (a, k, i, 0)), pl.BlockSpec((None, tr, cols), lambda a, i, k: (a, i, 0))],
        out_shape=[jax.ShapeDtypeStruct((HALF_L, N_CHIPS, rows, cols), BF16), jax.ShapeDtypeStruct((HALF_L, rows, cols), F32)],
        compiler_params=_params("parallel", "parallel", "arbitrary"))(g, g, got)


def _chip_exchange(parts):
    n = len(parts)

    def body(*refs):
        in_refs, out_refs = refs[:n], refs[n:2 * n]
        send_sems, recv_sems = refs[2 * n:]
        x, y, c = _me()
        chips = _other_chips(x, y)
        rows = pl.ds(0, HALF_L)
        cps = [pltpu.make_async_remote_copy(src_ref=in_refs[t].at[rows, _chip_index(*chip)], dst_ref=out_refs[t].at[j],
                                            send_sem=send_sems.at[3 * t + j], recv_sem=recv_sems.at[3 * t + j],
                                            device_id=(*chip, c), device_id_type=MESH)
               for j, chip in enumerate(chips) for t in range(n)]
        for cp in cps:
            cp.start()
        for cp in cps:
            cp.wait()

    return pl.pallas_call(
        body, name="chip_exchange", in_specs=[ANY] * n, out_specs=[ANY] * n,
        out_shape=[jax.ShapeDtypeStruct((3, HALF_L) + a.shape[2:], a.dtype) for a in parts],
        scratch_shapes=[pltpu.SemaphoreType.DMA((3 * n,)), pltpu.SemaphoreType.DMA((3 * n,))])(*parts)


def _chip_add(own, got, name):
    _, rows, cols = own.shape
    tr = _row_tile(rows, cols, 1024 * 1024)

    def body(own_ref, got_ref, o_ref):
        acc = own_ref[...]
        for j in range(3):
            acc = acc + got_ref[j].astype(F32)
        o_ref[...] = acc

    return pl.pallas_call(
        body, name=name, grid=(HALF_L, rows // tr),
        in_specs=[pl.BlockSpec((None, tr, cols), lambda a, i: (a, i, 0)),
                  pl.BlockSpec((3, None, tr, cols), lambda a, i: (0, a, i, 0))],
        out_specs=pl.BlockSpec((None, tr, cols), lambda a, i: (a, i, 0)),
        out_shape=jax.ShapeDtypeStruct((HALF_L, rows, cols), F32), compiler_params=_params("parallel", "parallel"))(own, got)


def _pair_share(halves):
    n = len(halves)

    def body(*refs):
        in_refs, out_refs = refs[:n], refs[n:2 * n]
        send_sems, recv_sems = refs[2 * n:]
        x, y, c = _me()
        cps = [pltpu.make_async_remote_copy(src_ref=in_refs[t], dst_ref=out_refs[t], send_sem=send_sems.at[t],
                                            recv_sem=recv_sems.at[t], device_id=(x, y, 1 - c), device_id_type=MESH)
               for t in range(n)]
        for cp in cps:
            cp.start()
        for cp in cps:
            cp.wait()

    return pl.pallas_call(
        body, name="pair_share", in_specs=[ANY] * n, out_specs=[ANY] * n,
        out_shape=[jax.ShapeDtypeStruct(a.shape, a.dtype) for a in halves],
        scratch_shapes=[pltpu.SemaphoreType.DMA((n,)), pltpu.SemaphoreType.DMA((n,))])(*halves)


N_DEV = 8


def _all_exchange(v):
    r, cols = v.shape

    def body(in_ref, out_ref, send_sems, recv_sems, local_sem):
        x, y, c = _me()
        me = 4 * x + 2 * y + c
        local = pltpu.make_async_copy(in_ref, out_ref.at[me], local_sem)
        local.start()
        flip = lambda v, f: 1 - v if f else v
        peers = [(flip(x, fx), flip(y, fy), flip(c, fc)) for fx in (0, 1) for fy in (0, 1) for fc in (0, 1)][1:]
        cps = [pltpu.make_async_remote_copy(src_ref=in_ref, dst_ref=out_ref.at[me], send_sem=send_sems.at[j],
                                            recv_sem=recv_sems.at[j], device_id=peer, device_id_type=MESH)
               for j, peer in enumerate(peers)]
        for cp in cps:
            cp.start()
        for j, (px, py, pc) in enumerate(peers):
            pltpu.make_async_remote_copy(src_ref=in_ref, dst_ref=out_ref.at[4 * px + 2 * py + pc], send_sem=send_sems.at[j],
                                         recv_sem=recv_sems.at[j], device_id=(px, py, pc), device_id_type=MESH).wait_recv()
        for cp in cps:
            cp.wait_send()
        local.wait()

    return pl.pallas_call(
        body, name="all_exchange", in_specs=[ANY], out_specs=ANY, out_shape=jax.ShapeDtypeStruct((N_DEV, r, cols), v.dtype),
        scratch_shapes=[pltpu.SemaphoreType.DMA((7,)), pltpu.SemaphoreType.DMA((7,)), pltpu.SemaphoreType.DMA(())])(v)


def _sum_slots(a, name):
    n, r, cols = a.shape
    tr = _pick(r, 512) if r % 8 == 0 else r
    for cand in (512, 256, 128, 64, 32, 16, 8):
        if r % cand == 0:
            tr = cand
            break

    def body(a_ref, o_ref):
        acc = a_ref[0]
        for k in range(1, n):
            acc = acc + a_ref[k]
        o_ref[...] = acc

    return pl.pallas_call(
        body, name=name, grid=(r // tr,), in_specs=[pl.BlockSpec((n, tr, cols), lambda i: (0, i, 0))],
        out_specs=pl.BlockSpec((tr, cols), lambda i: (i, 0)), out_shape=jax.ShapeDtypeStruct((r, cols), F32),
        compiler_params=_params("parallel"))(a)


def _join(name, stacked):
    ax = SHARDED[name][1]
    return jnp.concatenate([stacked[k] for k in range(N_CHIPS)], axis=ax)


def _split(name, full):
    ax = SHARDED[name][1]
    return jnp.stack(jnp.split(full, N_CHIPS, axis=ax))


def _heads_pad(a, real, axis):
    shp = a.shape
    a = a.reshape(shp[:axis] + (MLA_HEADS, real) + shp[axis + 1:])
    pad = [(0, 0)] * a.ndim
    pad[axis + 1] = (0, LANES - real)
    a = jnp.pad(a, pad)
    return a.reshape(shp[:axis] + (HP,) + shp[axis + 1:])


def _heads_unpad(a, real, axis):
    shp = a.shape
    a = a.reshape(shp[:axis] + (MLA_HEADS, LANES) + shp[axis + 1:])
    a = lax.slice_in_dim(a, 0, real, axis=axis + 1)
    return a.reshape(shp[:axis] + (MLA_HEADS * real,) + shp[axis + 1:])


def _lane_place(a, start):
    n = a.shape[-1]
    pad = [(0, 0)] * (a.ndim - 1) + [(start, LANES - start - n)]
    return jnp.pad(a, pad)


_O_UV, _O_CQ, _O_CKV, _O_KR, _O_Z, _O_XBC, _O_DT, _O_G = 0, 1024, 1408, 1664, 1696, 2208, 3232, 3240


def _w_in_pad(w):
    sl = lambda a, b: w[:, a:b]
    xs = _heads_pad(sl(_O_XBC, _O_XBC + SSD_INNER), SSD_HEAD_DIM, 1)
    bc = sl(_O_XBC + SSD_INNER, _O_DT)
    main = jnp.concatenate([sl(_O_UV, _O_CQ), _heads_pad(sl(_O_Z, _O_XBC), SSD_HEAD_DIM, 1), xs, sl(_O_G, IN_COLS)], axis=1)
    tail = jnp.concatenate([bc, sl(_O_CKV, _O_KR), sl(_O_CQ, _O_CKV), _lane_place(sl(_O_KR, _O_Z), MLA_NOPE),
                            _lane_place(sl(_O_DT, _O_G), 0), jnp.zeros((w.shape[0], PW_TAIL - T_DT - LANES), w.dtype)], axis=1)
    return main, tail


def _w_in_unpad(gm, gt):
    m = lambda a, n: gm[:, a:a + n]
    t = lambda a, n: gt[:, a:a + n]
    parts = [m(C_UV, 1024), t(T_CQ, MLA_Q_RANK), t(T_CKV, MLA_KV_RANK), t(T_KR + MLA_NOPE, MLA_ROPE),
             _heads_unpad(m(C_Z, HP), SSD_HEAD_DIM, 1), _heads_unpad(m(C_XS, HP), SSD_HEAD_DIM, 1), t(T_BC, BCW),
             t(T_DT, SSD_HEADS), m(C_G, 3 * D_MODEL)]
    return jnp.concatenate(parts, axis=1)


def _xbc_pad(a):
    return jnp.concatenate([_heads_pad(a[..., :SSD_INNER], SSD_HEAD_DIM, a.ndim - 1), a[..., SSD_INNER:]], axis=-1)


def _xbc_unpad(a):
    return jnp.concatenate([_heads_unpad(a[..., :HP], SSD_HEAD_DIM, a.ndim - 1), a[..., HP:]], axis=-1)


def _rope_tables(positions):
    inv_freq = 1.0 / (ROPE_THETA ** (jnp.arange(0, MLA_ROPE, 2, dtype=F32) / MLA_ROPE))
    ang = positions.astype(F32)[:, None] * inv_freq
    cos, sin = jnp.cos(ang), jnp.sin(ang)
    s = positions.shape[0]
    half = MLA_ROPE // 2
    z = lambda n: jnp.zeros((s, n), F32)
    ct = jnp.concatenate([jnp.ones((s, MLA_NOPE), F32), cos, cos, z(LANES - MLA_QK)], axis=1)
    s1 = jnp.concatenate([z(MLA_NOPE), -sin, z(half), z(LANES - MLA_QK)], axis=1)
    s2 = jnp.concatenate([z(MLA_NOPE), z(half), sin, z(LANES - MLA_QK)], axis=1)
    return ct, s1, s2


def _layer_weights(full, small, l):
    w = {}
    for n in ('ffn1_w_in', 'ffn2_w_in'):
        w[n] = full[n][l]
    for n in ('ffn1_w_out', 'ffn2_w_out', 'w_out'):
        g = full[n][l]
        w[n] = g.reshape((N_CHIPS * g.shape[1], g.shape[2]))
    fl = {n: _join(n, full[n][l]) for n in ('w_in', 'mla_w_uq', 'mla_w_ukv', 'w_branch', 'ssd_conv_w')}
    w['w_in_main'], w['w_in_tail'] = _w_in_pad(fl['w_in'])
    w['wuq'] = _heads_pad(fl['mla_w_uq'], MLA_QK, 1)
    ukv = fl['mla_w_ukv'].reshape(MLA_KV_RANK, MLA_HEADS, MLA_NOPE + MLA_V)
    zero = jnp.zeros((MLA_KV_RANK, MLA_HEADS, LANES - MLA_NOPE), ukv.dtype)
    wk = jnp.concatenate([ukv[:, :, :MLA_NOPE], zero], axis=2).reshape(MLA_KV_RANK, HP)
    wv = jnp.concatenate([ukv[:, :, MLA_NOPE:], zero], axis=2).reshape(MLA_KV_RANK, HP)
    w['wkv'] = jnp.concatenate([wk, wv], axis=1)
    wb = fl['w_branch']
    w['wb0'] = wb[0]
    w['wb1'] = _heads_pad(wb[1], MLA_V, 0)
    w['wb2'] = _heads_pad(wb[2], SSD_HEAD_DIM, 0)
    w['conv_w'] = _xbc_pad(fl['ssd_conv_w'].astype(F32))
    row = lambda n: small[n][l][None, :]
    for n in ('ffn1_norm', 'mix_norm', 'gm_v_norm', 'mla_q_norm', 'mla_kv_norm', 'ffn2_norm'):
        w[n] = row(n)
    w['gm_w_s'] = small['gm_w_s'][l]
    w['gm_b_full'] = jnp.broadcast_to(small['gm_b_s'][l][:, :, None], (GM_GROUPS, CHUNK, LANES))
    w['gq'] = _lane_place(row('mla_q_gain'), 0)
    w['gk'] = _lane_place(row('mla_k_gain'), 0)
    w['conv_b'] = _xbc_pad(row('ssd_conv_b'))
    w['dt_bias'] = _lane_place(row('ssd_dt_bias'), 0)
    w['a_log'] = _lane_place(row('ssd_a_log'), 0)
    w['d_vec'] = jnp.repeat(small['ssd_d'][l], LANES)[None, :]
    w['ssd_norm'] = _heads_pad(row('ssd_norm'), SSD_HEAD_DIM, 1)
    return w


def _ffn_fwd(x, norm, w4, w_out, tag):
    h, gate, up, act = _ffn_in(x, norm, w4, f"{tag}_in")
    y = _ffn_out(act, w_out, x, f"{tag}_out")
    return y, (x, h, gate, up, act)


def _ffn_bwd(dy, saved, norm, w4, w_out, tag):
    x, h, gate, up, act = saved
    dw_out = _ffn_dwout(act, dy, f"{tag}_dwout")
    da = _ffn_dact(dy, w_out, gate, up, f"{tag}_dact")
    dw_in = _ffn_dwin(h, da, f"{tag}_dwin")
    dx, dnorm = _ffn_dx(da, w4, x, norm, dy, f"{tag}_dx")
    return dx, dnorm, dw_in, dw_out.reshape((N_CHIPS, 2 * FC // N_CHIPS, D_MODEL))


def _mixer_fwd(x, w, tabs, tag):
    h = _rmsnorm_fwd(x, w['mix_norm'], f"{tag}_norm")
    pm = _matmul(h, w['w_in_main'], out_dtype=BF16, name=f"{tag}_proj_main")
    pt = _matmul(h, w['w_in_tail'], name=f"{tag}_proj_tail")
    ya = _gmlp_fwd(pm, w['gm_v_norm'], w['gm_w_s'], w['gm_b_full'], f"{tag}_gmlp")
    q, k, v = _mla_pre_fwd(pt, tabs, w['mla_q_norm'], w['mla_kv_norm'], w['wuq'], w['wkv'], w['gq'], w['gk'], f"{tag}_mla_pre")
    o, lse = _attn_fwd(q, k, v, f"{tag}_attn")
    xs = _conv_fwd(pm, C_XS, HP, w['conv_w'][:, :HP], w['conv_b'][:, :HP], f"{tag}_conv_x")
    bc = _conv_fwd(pt, T_BC, BCW, w['conv_w'][:, HP:], w['conv_b'][:, HP:], f"{tag}_conv_bc")
    dtb, dab = _dt_fwd(pt, w['dt_bias'], w['a_log'], f"{tag}_dt")
    ys, s_in = _scan_fwd(xs, bc, dtb, dab, f"{tag}_scan")
    yc = _ssd_post_fwd(ys, xs, pm, w['d_vec'], w['ssd_norm'], f"{tag}_ssd_post")
    mg, y = _merge_fwd(pm, ya, o, yc, w['wb0'], w['wb1'], w['wb2'], w['w_out'], x, f"{tag}_merge")
    return y, (x, h, pm, pt, ya, q, k, v, o, lse, xs, bc, dtb, dab, ys, s_in, yc, mg)


def _mixer_bwd(dy, saved, w, tabs, tag):
    x, h, pm, pt, ya, q, k, v, o, lse, xs, bc, dtb, dab, ys, s_in, yc, mg = saved
    g = {}
    g['w_out'] = _matmul(mg, dy, ta=True, name=f"{tag}_dwout").reshape((N_CHIPS, D_MODEL // N_CHIPS, D_MODEL))
    d0, d1, d2, dgates, dya, do, dyc = _merge_bwd(pm, ya, o, yc, w['wb0'], w['wb1'], w['wb2'], w['w_out'], dy, f"{tag}_dmerge")
    dwb0 = _matmul(ya, d0, ta=True, name=f"{tag}_dwb0")
    dwb1 = _matmul(o, d1, ta=True, name=f"{tag}_dwb1")
    dwb2 = _matmul(yc, d2, ta=True, name=f"{tag}_dwb2")
    g['w_branch'] = _split('w_branch', jnp.stack([dwb0, _heads_unpad(dwb1, MLA_V, 0), _heads_unpad(dwb2, SSD_HEAD_DIM, 0)]))
    duv, g['gm_v_norm'], g['gm_w_s'], db = _gmlp_bwd(pm, w['gm_v_norm'], w['gm_w_s'], w['gm_b_full'], dya, f"{tag}_dgmlp")
    g['gm_b_s'] = db.T
    dq, delta = _attn_bwd_dq(q, k, v, o, lse, do, f"{tag}_dattn_q")
    dk, dv = _attn_bwd_dkv(q, k, v, lse, delta, do, f"{tag}_dattn_kv")
    dcq, dckv, dkr, dwuq, dwkv, g['mla_q_norm'], g['mla_kv_norm'], dgq, dgk = _mla_pre_bwd(
        pt, tabs, w['mla_q_norm'], w['mla_kv_norm'], w['wuq'], w['wkv'], w['gq'], w['gk'], dq, dk, dv, f"{tag}_dmla_pre")
    g['mla_w_uq'] = _split('mla_w_uq', _heads_unpad(dwuq, MLA_QK, 1))
    dwk = dwkv[:, :HP].reshape(MLA_KV_RANK, MLA_HEADS, LANES)[:, :, :MLA_NOPE]
    dwv = dwkv[:, HP:].reshape(MLA_KV_RANK, MLA_HEADS, LANES)[:, :, :MLA_V]
    g['mla_w_ukv'] = _split('mla_w_ukv', jnp.concatenate([dwk, dwv], axis=2).reshape(MLA_KV_RANK, MLA_HEADS * (MLA_NOPE + MLA_V)))
    g['mla_q_gain'], g['mla_k_gain'] = dgq[:, :MLA_QK], dgk[:, :MLA_QK]
    dys, dz, dssd_norm, dd = _ssd_post_bwd(ys, xs, pm, w['d_vec'], w['ssd_norm'], dyc, f"{tag}_dssd_post")
    g['ssd_norm'] = _heads_unpad(dssd_norm, SSD_HEAD_DIM, 1)
    g['ssd_d'] = jnp.sum(dd.reshape(SSD_HEADS, LANES), axis=1)[None, :]
    dxs, dbm, dcm, dda, ddtx = _scan_bwd(xs, bc, dtb, dab, s_in, dys, w['d_vec'], f"{tag}_dscan")
    dxs16, dcw_x, dcb_x = _conv_bwd(pm, C_XS, HP, w['conv_w'][:, :HP], w['conv_b'][:, :HP], dxs, f"{tag}_dconv_x")
    dbc16, dcw_bc, dcb_bc = _conv_bwd(pt, T_BC, BCW, w['conv_w'][:, HP:], w['conv_b'][:, HP:],
                                      jnp.concatenate([dbm, dcm], axis=1), f"{tag}_dconv_bc")
    g['ssd_conv_w'] = _xbc_unpad(jnp.concatenate([dcw_x, dcw_bc], axis=1))
    g['ssd_conv_b'] = _xbc_unpad(jnp.concatenate([dcb_x, dcb_bc], axis=1))
    ddt, dbias, dalog = _dt_bwd(pt, w['dt_bias'], w['a_log'], dda, ddtx, f"{tag}_ddt")
    g['ssd_dt_bias'], g['ssd_a_log'] = dbias[:, :SSD_HEADS], dalog[:, :SSD_HEADS]
    s = x.shape[0]
    dpm = jnp.concatenate([duv, dz, dxs16, dgates], axis=1)
    dpt = jnp.concatenate([dbc16, dckv, dcq, dkr, ddt, jnp.zeros((s, PW_TAIL - T_DT - LANES), BF16)], axis=1)
    g['w_in'] = _split('w_in', _w_in_unpad(_matmul(h, dpm, ta=True, name=f"{tag}_dwin_main"),
                                           _matmul(h, dpt, ta=True, name=f"{tag}_dwin_tail")))
    dh = _matmul(dpt, w['w_in_tail'], tb=True, name=f"{tag}_dh_tail")
    dh = _matmul(dpm, w['w_in_main'], tb=True, res=dh, name=f"{tag}_dh_main")
    dx, g['mix_norm'] = _rmsnorm_bwd(x, w['mix_norm'], dh, dy, f"{tag}_dnorm")
    return dx, g


def _local_step(x, positions, target, full, small):
    tabs = _rope_tables(positions)
    ws = [_layer_weights(full, small, l) for l in range(DEPTH)]
    saved = []
    for l, w in enumerate(ws):
        x, s1 = _ffn_fwd(x, w['ffn1_norm'], w['ffn1_w_in'], w['ffn1_w_out'], "ffn1")
        x, s2 = _mixer_fwd(x, w, tabs, "mix")
        x, s3 = _ffn_fwd(x, w['ffn2_norm'], w['ffn2_w_in'], w['ffn2_w_out'], "ffn2")
        saved.append((s1, s2, s3))
    dy, sq = _loss_head(x, target, "loss_head")
    loss = 0.5 * jnp.sum(sq) / D_MODEL
    grads = [None] * DEPTH
    for l in reversed(range(DEPTH)):
        w = ws[l]
        s1, s2, s3 = saved[l]
        dy, dn2, dwi2, dwo2 = _ffn_bwd(dy, s3, w['ffn2_norm'], w['ffn2_w_in'], w['ffn2_w_out'], "ffn2")
        dy, g = _mixer_bwd(dy, s2, w, tabs, "mix")
        dy, dn1, dwi1, dwo1 = _ffn_bwd(dy, s1, w['ffn1_norm'], w['ffn1_w_in'], w['ffn1_w_out'], "ffn1")
        g.update(ffn1_norm=dn1, ffn1_w_in=dwi1, ffn1_w_out=dwo1, ffn2_norm=dn2, ffn2_w_in=dwi2, ffn2_w_out=dwo2)
        grads[l] = g
    return loss, dy, grads


SMALL_PACK = SMALL_ORDER + ['ssd_conv_w']


def _pack_small(per_layer_rows):
    flat = jnp.concatenate([per_layer_rows[l][n].reshape(-1).astype(F32) for l in range(DEPTH) for n in SMALL_PACK])
    rows = -(-flat.shape[0] // LANES)
    rows = -(-rows // 8) * 8
    return jnp.pad(flat, (0, rows * LANES - flat.shape[0])).reshape(rows, LANES)


def _unpack_small(buf, shapes):
    flat = buf.reshape(-1)
    off = 0
    out = {n: [] for n in SMALL_PACK}
    for l in range(DEPTH):
        for n in SMALL_PACK:
            size = int(np.prod(shapes[n]))
            out[n].append(flat[off:off + size].reshape(shapes[n]))
            off += size
    return {n: jnp.stack(v) for n, v in out.items()}


def kernel(x, positions, ffn1_norm, ffn1_w_in, ffn1_w_out, mix_norm, w_in, gm_v_norm, gm_w_s, gm_b_s, mla_q_norm, mla_kv_norm, mla_w_uq, mla_w_ukv, mla_q_gain, mla_k_gain, ssd_conv_w, ssd_conv_b, ssd_dt_bias, ssd_a_log, ssd_d, ssd_norm, w_branch, w_out, ffn2_norm, ffn2_w_in, ffn2_w_out, loss_target, m_ffn1_norm, m_ffn1_w_in, m_ffn1_w_out, m_mix_norm, m_w_in, m_gm_v_norm, m_gm_w_s, m_gm_b_s, m_mla_q_norm, m_mla_kv_norm, m_mla_w_uq, m_mla_w_ukv, m_mla_q_gain, m_mla_k_gain, m_ssd_conv_w, m_ssd_conv_b, m_ssd_dt_bias, m_ssd_a_log, m_ssd_d, m_ssd_norm, m_w_branch, m_w_out, m_ffn2_norm, m_ffn2_w_in, m_ffn2_w_out, v_ffn1_norm, v_ffn1_w_in, v_ffn1_w_out, v_mix_norm, v_w_in, v_gm_v_norm, v_gm_w_s, v_gm_b_s, v_mla_q_norm, v_mla_kv_norm, v_mla_w_uq, v_mla_w_ukv, v_mla_q_gain, v_mla_k_gain, v_ssd_conv_w, v_ssd_conv_b, v_ssd_dt_bias, v_ssd_a_log, v_ssd_d, v_ssd_norm, v_w_branch, v_w_out, v_ffn2_norm, v_ffn2_w_in, v_ffn2_w_out):
    wts = dict(zip(WEIGHTS, (ffn1_norm, ffn1_w_in, ffn1_w_out, mix_norm, w_in, gm_v_norm, gm_w_s, gm_b_s, mla_q_norm, mla_kv_norm,
                             mla_w_uq, mla_w_ukv, mla_q_gain, mla_k_gain, ssd_conv_w, ssd_conv_b, ssd_dt_bias, ssd_a_log, ssd_d,
                             ssd_norm, w_branch, w_out, ffn2_norm, ffn2_w_in, ffn2_w_out)))
    mom = dict(zip(WEIGHTS, (m_ffn1_norm, m_ffn1_w_in, m_ffn1_w_out, m_mix_norm, m_w_in, m_gm_v_norm, m_gm_w_s, m_gm_b_s, m_mla_q_norm,
                             m_mla_kv_norm, m_mla_w_uq, m_mla_w_ukv, m_mla_q_gain, m_mla_k_gain, m_ssd_conv_w, m_ssd_conv_b,
                             m_ssd_dt_bias, m_ssd_a_log, m_ssd_d, m_ssd_norm, m_w_branch, m_w_out, m_ffn2_norm, m_ffn2_w_in,
                             m_ffn2_w_out)))
    var = dict(zip(WEIGHTS, (v_ffn1_norm, v_ffn1_w_in, v_ffn1_w_out, v_mix_norm, v_w_in, v_gm_v_norm, v_gm_w_s, v_gm_b_s, v_mla_q_norm,
                             v_mla_kv_norm, v_mla_w_uq, v_mla_w_ukv, v_mla_q_gain, v_mla_k_gain, v_ssd_conv_w, v_ssd_conv_b,
                             v_ssd_dt_bias, v_ssd_a_log, v_ssd_d, v_ssd_norm, v_w_branch, v_w_out, v_ffn2_norm, v_ffn2_w_in,
                             v_ffn2_w_out)))
    cx, cy, _ = _me()
    mychip = _chip_index(cx, cy)

    gathered = _gather_shards([wts[n].astype(BF16) for n in SHARDED_ORDER])
    full = dict(zip(SHARDED_ORDER, gathered))
    small = {n: wts[n] for n in SMALL_ORDER}

    loss_part, dx, grads = _local_step(x[0], positions[0], loss_target[0], full, small)
    loss = lax.psum(loss_part, ("x", "y", "c"))

    def rows_cols(a, lead):
        return a.reshape(a.shape[:lead] + (int(np.prod(a.shape[lead:-1])), a.shape[-1]))

    gs = [rows_cols(jnp.stack([grads[l][n] for l in range(DEPTH)]), 2) for n in REDUCED]
    got = _pair_exchange(gs)
    sums = [_pair_add(g, r, f"pair_add_{n}") for g, r, n in zip(gs, got, REDUCED)]
    arrived = _chip_exchange([s16 for s16, _ in sums])
    halves = [_chip_add(own, a, f"chip_add_{n}") for (_, own), a, n in zip(sums, arrived, REDUCED)]
    theirs = _pair_share(halves)
    shapes = {n: wts[n].shape[1:] for n in SMALL_ORDER}
    shapes['ssd_conv_w'] = SHARDED['ssd_conv_w'][0]
    small_g = _unpack_small(_sum_slots(_all_exchange(_pack_small(grads)), "small_sum"), shapes)
    conv_full = small_g.pop('ssd_conv_w')
    shard_cols = _shard_shape('ssd_conv_w')[1]
    small_g['ssd_conv_w'] = lax.dynamic_slice_in_dim(conv_full, mychip * shard_cols, shard_cols, axis=2)
    shapes['ssd_conv_w'] = _shard_shape('ssd_conv_w')

    grad, delta, new_m, new_v = {}, {}, {}, {}
    for n, a, b in zip(REDUCED, halves, theirs):
        shp = wts[n].shape
        outs = _adamw_sharded(rows_cols(wts[n], 1), rows_cols(mom[n], 1), rows_cols(var[n], 1), a, b, f"adamw_{n}")
        grad[n], delta[n], new_m[n], new_v[n] = [o.reshape(shp) for o in outs]
    per_layer = lambda t: [{n: t[n][l] for n in SMALL_PACK} for l in range(DEPTH)]
    d, nm, nv = _adamw(_pack_small(per_layer(wts)), _pack_small(per_layer(small_g)), _pack_small(per_layer(mom)),
                       _pack_small(per_layer(var)), "adamw_small")
    sd, snm, snv = _unpack_small(d, shapes), _unpack_small(nm, shapes), _unpack_small(nv, shapes)
    for n in SMALL_PACK:
        grad[n], delta[n], new_m[n], new_v[n] = small_g[n], sd[n], snm[n], snv[n]
    return (loss, dx[None], *[grad[n] for n in WEIGHTS], *[delta[n] for n in WEIGHTS], *[new_m[n] for n in WEIGHTS],
            *[new_v[n] for n in WEIGHTS])
```

```python
import functools
import math

import numpy as np
import jax
import jax.numpy as jnp
from jax import lax
from jax.experimental import pallas as pl
from jax.experimental.pallas import tpu as pltpu

F32, BF16 = jnp.float32, jnp.bfloat16
MESH = pl.DeviceIdType.MESH

D_MODEL, DEPTH, D_FF, EPS = 1024, 4, 2816, 1e-6
GM_WIDTH, GM_GROUPS, CHUNK = 512, 4, 128
MLA_HEADS, MLA_Q_RANK, MLA_KV_RANK, MLA_NOPE, MLA_ROPE, MLA_V = 8, 384, 256, 64, 32, 64
MLA_QK = MLA_NOPE + MLA_ROPE
ROPE_THETA = 10000.0
SSD_HEADS, SSD_HEAD_DIM, SSD_GROUPS, SSD_STATE, SSD_CONV = 8, 64, 2, 128, 4
SSD_INNER = SSD_HEADS * SSD_HEAD_DIM
IN_COLS = 6312
LANES = 128
ADAM_LR, ADAM_B1, ADAM_B2, ADAM_EPS, ADAM_WD, ADAM_STEP = 0.001, 0.9, 0.999, 1e-08, 0.01, 10

C_UV, C_Z, C_XS, C_G, PW_MAIN = 0, 1024, 2048, 3072, 6144
T_BC, T_CKV, T_CQ, T_KR, T_DT, PW_TAIL = 0, 512, 768, 1152, 1280, 1536
HP = MLA_HEADS * LANES
FC = 2 * D_FF // 4

WEIGHTS = ['ffn1_norm', 'ffn1_w_in', 'ffn1_w_out', 'mix_norm', 'w_in', 'gm_v_norm', 'gm_w_s', 'gm_b_s', 'mla_q_norm',
           'mla_kv_norm', 'mla_w_uq', 'mla_w_ukv', 'mla_q_gain', 'mla_k_gain', 'ssd_conv_w', 'ssd_conv_b', 'ssd_dt_bias',
           'ssd_a_log', 'ssd_d', 'ssd_norm', 'w_branch', 'w_out', 'ffn2_norm', 'ffn2_w_in', 'ffn2_w_out']
SHARDED = {'ffn1_w_in': ((1024, 5632), 1), 'ffn1_w_out': ((2816, 1024), 0), 'w_in': ((1024, 6312), 1),
           'mla_w_uq': ((384, 768), 1), 'mla_w_ukv': ((256, 1024), 1), 'ssd_conv_w': ((4, 1024), 1),
           'w_branch': ((3, 512, 1024), 2), 'w_out': ((1024, 1024), 0), 'ffn2_w_in': ((1024, 5632), 1),
           'ffn2_w_out': ((2816, 1024), 0)}
SHARDED_ORDER = [n for n in WEIGHTS if n in SHARDED]
SMALL_ORDER = [n for n in WEIGHTS if n not in SHARDED]
REDUCED = [n for n in SHARDED_ORDER if n != 'ssd_conv_w']
N_CHIPS = 4
HALF_L = DEPTH // 2


def _shard_shape(name):
    shape, ax = SHARDED[name]
    return tuple(d // N_CHIPS if i == ax else d for i, d in enumerate(shape))


def _pick(dim, target):
    if dim <= target:
        return dim
    t = (target // LANES) * LANES
    while t >= LANES:
        if dim % t == 0:
            return t
        t -= LANES
    return dim


def _sigmoid(x):
    return 1.0 / (1.0 + jnp.exp(-x))


def _params(*sem):
    return pltpu.CompilerParams(dimension_semantics=sem, vmem_limit_bytes=56 * 1024 * 1024)


def _matmul(a, b, *, ta=False, tb=False, out_dtype=F32, scale=1.0, res=None, name):
    if ta:
        k_dim, m_dim = a.shape
    else:
        m_dim, k_dim = a.shape
    if tb:
        n_dim, k2 = b.shape
    else:
        k2, n_dim = b.shape
    assert k_dim == k2, (a.shape, b.shape, ta, tb)
    tm, tn, tk = _pick(m_dim, 1024), _pick(n_dim, 1024), _pick(k_dim, 1024)
    nk = k_dim // tk
    dn = (((0 if ta else 1,), (1 if tb else 0,)), ((), ()))

    def body(*refs):
        if res is not None:
            a_ref, b_ref, r_ref, o_ref, acc = refs
        else:
            a_ref, b_ref, o_ref, acc = refs
        k = pl.program_id(2)

        @pl.when(k == 0)
        def _():
            acc[...] = jnp.zeros_like(acc)

        acc[...] += lax.dot_general(a_ref[...].astype(BF16), b_ref[...].astype(BF16), dn, preferred_element_type=F32)

        @pl.when(k == nk - 1)
        def _():
            r = acc[...]
            if scale != 1.0:
                r = r * scale
            if res is not None:
                r = r + r_ref[...]
            o_ref[...] = r.astype(out_dtype)

    a_spec = pl.BlockSpec((tk, tm), lambda j, i, k: (k, i)) if ta else pl.BlockSpec((tm, tk), lambda j, i, k: (i, k))
    b_spec = pl.BlockSpec((tn, tk), lambda j, i, k: (j, k)) if tb else pl.BlockSpec((tk, tn), lambda j, i, k: (k, j))
    in_specs = [a_spec, b_spec]
    args = [a, b]
    if res is not None:
        in_specs.append(pl.BlockSpec((tm, tn), lambda j, i, k: (i, j)))
        args.append(res)
    return pl.pallas_call(
        body, name=name, grid=(n_dim // tn, m_dim // tm, nk), in_specs=in_specs,
        out_specs=pl.BlockSpec((tm, tn), lambda j, i, k: (i, j)),
        out_shape=jax.ShapeDtypeStruct((m_dim, n_dim), out_dtype),
        scratch_shapes=[pltpu.VMEM((tm, tn), F32)],
        compiler_params=_params("parallel", "parallel", "arbitrary"))(*args)


def _rmsnorm_fwd(x, gain, name):
    s, d = x.shape
    tm = _pick(s, 512)

    def body(x_ref, g_ref, o_ref):
        xv = x_ref[...]
        r = lax.rsqrt(jnp.mean(xv * xv, axis=-1, keepdims=True) + EPS)
        o_ref[...] = (xv * r * g_ref[...]).astype(BF16)

    return pl.pallas_call(
        body, name=name, grid=(s // tm,),
        in_specs=[pl.BlockSpec((tm, d), lambda i: (i, 0)), pl.BlockSpec((1, d), lambda i: (0, 0))],
        out_specs=pl.BlockSpec((tm, d), lambda i: (i, 0)),
        out_shape=jax.ShapeDtypeStruct((s, d), BF16), compiler_params=_params("parallel"))(x, gain)


def _rmsnorm_bwd(x, gain, dh, dres, name):
    s, d = x.shape
    tm = _pick(s, 512)

    def body(x_ref, g_ref, dh_ref, dr_ref, dx_ref, dg_ref):
        @pl.when(pl.program_id(0) == 0)
        def _():
            dg_ref[...] = jnp.zeros_like(dg_ref)

        xv, dhv = x_ref[...], dh_ref[...]
        r = lax.rsqrt(jnp.mean(xv * xv, axis=-1, keepdims=True) + EPS)
        u = dhv * g_ref[...]
        dx_ref[...] = dr_ref[...] + r * u - xv * (r * r * r) * jnp.mean(xv * u, axis=-1, keepdims=True)
        dg_ref[...] += jnp.sum(dhv * xv * r, axis=0, keepdims=True)

    row = pl.BlockSpec((tm, d), lambda i: (i, 0))
    vec = pl.BlockSpec((1, d), lambda i: (0, 0))
    return pl.pallas_call(
        body, name=name, grid=(s // tm,), in_specs=[row, vec, row, row], out_specs=[row, vec],
        out_shape=[jax.ShapeDtypeStruct((s, d), F32), jax.ShapeDtypeStruct((1, d), F32)],
        compiler_params=_params("arbitrary"))(x, gain, dh, dres)


_NT = (((1,), (1,)), ((), ()))
_TN = (((0,), (0,)), ((), ()))


def _resident(shape):
    return pl.BlockSpec(shape, lambda *_: tuple(0 for _ in shape), pipeline_mode=pl.Buffered(1))


def _ffn_in(x, gain, w4, name):
    s, d = x.shape
    tm = _pick(s, 512)

    def body(x_ref, g_ref, w_ref, h_ref, gate_ref, up_ref, act_ref):
        xv = x_ref[...]
        r = lax.rsqrt(jnp.mean(xv * xv, axis=-1, keepdims=True) + EPS)
        h = (xv * r * g_ref[...]).astype(BF16)
        h_ref[...] = h
        for j in range(2):
            g16 = jnp.dot(h, w_ref[j], preferred_element_type=F32).astype(BF16)
            u16 = jnp.dot(h, w_ref[j + 2], preferred_element_type=F32).astype(BF16)
            gate_ref[j] = g16
            up_ref[j] = u16
            gf, uf = g16.astype(F32), u16.astype(F32)
            act_ref[j] = (gf * _sigmoid(gf) * uf).astype(BF16)

    half = pl.BlockSpec((2, tm, FC), lambda i: (0, i, 0))
    return pl.pallas_call(
        body, name=name, grid=(s // tm,),
        in_specs=[pl.BlockSpec((tm, d), lambda i: (i, 0)), pl.BlockSpec((1, d), lambda i: (0, 0)), _resident((4, d, FC))],
        out_specs=[pl.BlockSpec((tm, d), lambda i: (i, 0)), half, half, half],
        out_shape=[jax.ShapeDtypeStruct((s, d), BF16)] + [jax.ShapeDtypeStruct((2, s, FC), BF16)] * 3,
        compiler_params=_params("parallel"))(x, gain, w4)


def _ffn_out(act, w_out, x, name):
    s, d = x.shape
    tm = _pick(s, 512)

    def body(a_ref, w_ref, x_ref, o_ref):
        acc = jnp.dot(a_ref[0], w_ref[0:FC, :], preferred_element_type=F32)
        acc = acc + jnp.dot(a_ref[1], w_ref[FC:2 * FC, :], preferred_element_type=F32)
        o_ref[...] = x_ref[...] + 0.5 * acc

    row = pl.BlockSpec((tm, d), lambda i: (i, 0))
    return pl.pallas_call(
        body, name=name, grid=(s // tm,),
        in_specs=[pl.BlockSpec((2, tm, FC), lambda i: (0, i, 0)), _resident((2 * FC, d)), row], out_specs=row,
        out_shape=jax.ShapeDtypeStruct((s, d), F32), compiler_params=_params("parallel"))(act, w_out, x)


def _ffn_dact(dy, w_out, gate, up, name):
    s, d = dy.shape
    tm = _pick(s, 512)

    def body(dy_ref, w_ref, g_ref, u_ref, o_ref):
        dy16 = dy_ref[...].astype(BF16)
        for j in range(2):
            dact = 0.5 * lax.dot_general(dy16, w_ref[j * FC:(j + 1) * FC, :], _NT, preferred_element_type=F32)
            g, u = g_ref[j].astype(F32), u_ref[j].astype(F32)
            sg = _sigmoid(g)
            o_ref[j] = (dact * u * (sg * (1.0 + g * (1.0 - sg)))).astype(BF16)
            o_ref[j + 2] = (dact * g * sg).astype(BF16)

    half = pl.BlockSpec((2, tm, FC), lambda i: (0, i, 0))
    return pl.pallas_call(
        body, name=name, grid=(s // tm,),
        in_specs=[pl.BlockSpec((tm, d), lambda i: (i, 0)), _resident((2 * FC, d)), half, half],
        out_specs=pl.BlockSpec((4, tm, FC), lambda i: (0, i, 0)),
        out_shape=jax.ShapeDtypeStruct((4, s, FC), BF16), compiler_params=_params("parallel"))(dy, w_out, gate, up)


def _ffn_dwout(act, dy, name):
    s, d = dy.shape
    tk = _pick(s, 1024)
    nk = s // tk

    def body(a_ref, dy_ref, o_ref):
        k = pl.program_id(1)

        @pl.when(k == 0)
        def _():
            o_ref[...] = jnp.zeros_like(o_ref)

        o_ref[...] += lax.dot_general(a_ref[...], dy_ref[...].astype(BF16), _TN, preferred_element_type=F32)

        @pl.when(k == nk - 1)
        def _():
            o_ref[...] = 0.5 * o_ref[...]

    return pl.pallas_call(
        body, name=name, grid=(2, nk),
        in_specs=[pl.BlockSpec((None, tk, FC), lambda j, k: (j, k, 0)), pl.BlockSpec((tk, d), lambda j, k: (k, 0))],
        out_specs=pl.BlockSpec((FC, d), lambda j, k: (j, 0)), out_shape=jax.ShapeDtypeStruct((2 * FC, d), F32),
        compiler_params=_params("parallel", "arbitrary"))(act, dy)


def _ffn_dwin(h, da, name):
    s, d = h.shape
    tk = _pick(s, 1024)

    def body(h_ref, da_ref, o_ref):
        @pl.when(pl.program_id(1) == 0)
        def _():
            o_ref[...] = jnp.zeros_like(o_ref)

        o_ref[...] += lax.dot_general(h_ref[...], da_ref[...], _TN, preferred_element_type=F32)

    return pl.pallas_call(
        body, name=name, grid=(4, s // tk),
        in_specs=[pl.BlockSpec((tk, d), lambda j, k: (k, 0)), pl.BlockSpec((None, tk, FC), lambda j, k: (j, k, 0))],
        out_specs=pl.BlockSpec((None, d, FC), lambda j, k: (j, 0, 0)), out_shape=jax.ShapeDtypeStruct((4, d, FC), F32),
        compiler_params=_params("parallel", "arbitrary"))(h, da)


def _ffn_dx(da, w4, x, gain, dy, name):
    s, d = x.shape
    tm = _pick(s, 512)

    def body(da_ref, w_ref, x_ref, g_ref, dy_ref, dx_ref, dg_ref):
        @pl.when(pl.program_id(0) == 0)
        def _():
            dg_ref[...] = jnp.zeros_like(dg_ref)

        dh = jnp.zeros((tm, d), F32)
        for j in range(4):
            dh = dh + lax.dot_general(da_ref[j], w_ref[j], _NT, preferred_element_type=F32)
        xv = x_ref[...]
        r = lax.rsqrt(jnp.mean(xv * xv, axis=-1, keepdims=True) + EPS)
        u = dh * g_ref[...]
        dx_ref[...] = dy_ref[...] + r * u - xv * (r * r * r) * jnp.mean(xv * u, axis=-1, keepdims=True)
        dg_ref[...] += jnp.sum(dh * xv * r, axis=0, keepdims=True)

    row = pl.BlockSpec((tm, d), lambda i: (i, 0))
    vec = pl.BlockSpec((1, d), lambda i: (0, 0))
    return pl.pallas_call(
        body, name=name, grid=(s // tm,),
        in_specs=[pl.BlockSpec((4, tm, FC), lambda i: (0, i, 0)), _resident((4, d, FC)), row, vec, row],
        out_specs=[row, vec], out_shape=[jax.ShapeDtypeStruct((s, d), F32), jax.ShapeDtypeStruct((1, d), F32)],
        compiler_params=_params("arbitrary"))(da, w4, x, gain, dy)


_INV_SQRT2 = 0.7071067811865476
_INV_SQRT2PI = 0.3989422804014327


def _gelu(x):
    return 0.5 * x * (1.0 + lax.erf(x * _INV_SQRT2))


def _gelu_grad(x):
    return 0.5 * (1.0 + lax.erf(x * _INV_SQRT2)) + x * jnp.exp(-0.5 * x * x) * _INV_SQRT2PI


def _tril_mask():
    r = lax.broadcasted_iota(jnp.int32, (CHUNK, CHUNK), 0)
    c = lax.broadcasted_iota(jnp.int32, (CHUNK, CHUNK), 1)
    return r >= c


def _gmlp_fwd(p, v_gain, w_s, b_full, name):
    s = p.shape[0]
    tm = _pick(s, 512)
    nch = tm // CHUNK

    def body(uv_ref, g_ref, w_ref, b_ref, o_ref):
        gel = _gelu(uv_ref[...].astype(F32))
        u, v = gel[:, :GM_WIDTH], gel[:, GM_WIDTH:]
        r = lax.rsqrt(jnp.mean(v * v, axis=-1, keepdims=True) + EPS)
        vn = (v * r * g_ref[...]).astype(BF16)
        mask = _tril_mask()
        for g in range(GM_GROUPS):
            wm = jnp.where(mask, w_ref[g], 0.0).astype(BF16)
            for c in range(nch):
                rs, cs = slice(c * CHUNK, (c + 1) * CHUNK), slice(g * LANES, (g + 1) * LANES)
                sp = jnp.dot(wm, vn[rs, cs], preferred_element_type=F32) + b_ref[g]
                o_ref[rs, cs] = (u[rs, cs] * sp).astype(BF16)

    full3 = pl.BlockSpec((GM_GROUPS, CHUNK, CHUNK), lambda i: (0, 0, 0))
    return pl.pallas_call(
        body, name=name, grid=(s // tm,),
        in_specs=[pl.BlockSpec((tm, 2 * GM_WIDTH), lambda i: (i, C_UV // (2 * GM_WIDTH))),
                  pl.BlockSpec((1, GM_WIDTH), lambda i: (0, 0)), full3, full3],
        out_specs=pl.BlockSpec((tm, GM_WIDTH), lambda i: (i, 0)),
        out_shape=jax.ShapeDtypeStruct((s, GM_WIDTH), BF16), compiler_params=_params("parallel"))(p, v_gain, w_s, b_full)


def _gmlp_bwd(p, v_gain, w_s, b_full, dy, name):
    s = p.shape[0]
    tm = _pick(s, 512)
    nch = tm // CHUNK
    nsteps = s // tm

    def body(uv_ref, g_ref, w_ref, b_ref, dy_ref, duv_ref, dg_ref, dw_ref, db_ref, dvn_s, dbacc):
        step = pl.program_id(0)

        @pl.when(step == 0)
        def _():
            dg_ref[...] = jnp.zeros_like(dg_ref)
            dw_ref[...] = jnp.zeros_like(dw_ref)
            dbacc[...] = jnp.zeros_like(dbacc)

        uv = uv_ref[...].astype(F32)
        gel = _gelu(uv)
        u, v = gel[:, :GM_WIDTH], gel[:, GM_WIDTH:]
        r = lax.rsqrt(jnp.mean(v * v, axis=-1, keepdims=True) + EPS)
        gain = g_ref[...]
        vn32 = v * r * gain
        vn = vn32.astype(BF16)
        dy = dy_ref[...]
        mask = _tril_mask()
        for g in range(GM_GROUPS):
            wm = jnp.where(mask, w_ref[g], 0.0).astype(BF16)
            dwg = jnp.zeros((CHUNK, CHUNK), F32)
            dbg = jnp.zeros((CHUNK, LANES), F32)
            for c in range(nch):
                rs, cs = slice(c * CHUNK, (c + 1) * CHUNK), slice(g * LANES, (g + 1) * LANES)
                sp = jnp.dot(wm, vn[rs, cs], preferred_element_type=F32) + b_ref[g]
                dyc = dy[rs, cs]
                dsp = dyc * u[rs, cs]
                dsp16 = dsp.astype(BF16)
                duv_ref[rs, cs] = (dyc * sp * _gelu_grad(uv[rs, cs])).astype(BF16)
                dvn_s[rs, cs] = lax.dot_general(wm, dsp16, (((0,), (0,)), ((), ())), preferred_element_type=F32)
                dwg = dwg + lax.dot_general(dsp16, vn[rs, cs], (((1,), (1,)), ((), ())), preferred_element_type=F32)
                dbg = dbg + dsp
            dw_ref[g] += jnp.where(mask, dwg, 0.0)
            dbacc[:, g * LANES:(g + 1) * LANES] += dbg
        dvn = dvn_s[...]
        uu = dvn * gain
        dv = r * uu - v * (r * r * r) * jnp.mean(v * uu, axis=-1, keepdims=True)
        duv_ref[:, GM_WIDTH:] = (dv * _gelu_grad(uv[:, GM_WIDTH:])).astype(BF16)
        dg_ref[...] += jnp.sum(dvn * v * r, axis=0, keepdims=True)

        @pl.when(step == nsteps - 1)
        def _():
            for g in range(GM_GROUPS):
                db_ref[:, g:g + 1] = jnp.sum(dbacc[:, g * LANES:(g + 1) * LANES], axis=1, keepdims=True)

    full3 = pl.BlockSpec((GM_GROUPS, CHUNK, CHUNK), lambda i: (0, 0, 0))
    return pl.pallas_call(
        body, name=name, grid=(nsteps,),
        in_specs=[pl.BlockSpec((tm, 2 * GM_WIDTH), lambda i: (i, C_UV // (2 * GM_WIDTH))),
                  pl.BlockSpec((1, GM_WIDTH), lambda i: (0, 0)), full3, full3,
                  pl.BlockSpec((tm, GM_WIDTH), lambda i: (i, 0))],
        out_specs=[pl.BlockSpec((tm, 2 * GM_WIDTH), lambda i: (i, 0)), pl.BlockSpec((1, GM_WIDTH), lambda i: (0, 0)),
                   full3, pl.BlockSpec((CHUNK, GM_GROUPS), lambda i: (0, 0))],
        out_shape=[jax.ShapeDtypeStruct((s, 2 * GM_WIDTH), BF16), jax.ShapeDtypeStruct((1, GM_WIDTH), F32),
                   jax.ShapeDtypeStruct((GM_GROUPS, CHUNK, CHUNK), F32), jax.ShapeDtypeStruct((CHUNK, GM_GROUPS), F32)],
        scratch_shapes=[pltpu.VMEM((tm, GM_WIDTH), F32), pltpu.VMEM((CHUNK, GM_WIDTH), F32)],
        compiler_params=_params("arbitrary"))(p, v_gain, w_s, b_full, dy)


def _rope(x, ct, s1, s2):
    return x * ct + pltpu.roll(x, LANES - MLA_ROPE // 2, 1) * s1 + pltpu.roll(x, MLA_ROPE // 2, 1) * s2


def _rope_bwd(d, ct, s1, s2):
    return d * ct + pltpu.roll(d * s1, MLA_ROPE // 2, 1) + pltpu.roll(d * s2, LANES - MLA_ROPE // 2, 1)


def _head_norm(x, gain):
    r = lax.rsqrt(jnp.sum(x * x, axis=-1, keepdims=True) * (1.0 / MLA_QK) + EPS)
    return x * r * gain, r


def _head_norm_bwd(x, r, gain, d):
    u = d * gain
    return r * u - x * (r * r * r) * (jnp.sum(x * u, axis=-1, keepdims=True) * (1.0 / MLA_QK))


def _mla_specs(tm):
    cq = pl.BlockSpec((tm, MLA_Q_RANK), lambda i: (i, T_CQ // MLA_Q_RANK))
    ckv = pl.BlockSpec((tm, MLA_KV_RANK), lambda i: (i, T_CKV // MLA_KV_RANK))
    kr = pl.BlockSpec((tm, LANES), lambda i: (i, T_KR // LANES))
    tab = pl.BlockSpec((tm, LANES), lambda i: (i, 0))
    return cq, ckv, kr, tab


def _const(shape):
    return pl.BlockSpec(shape, lambda i: tuple(0 for _ in shape))


def _mla_pre_fwd(p, tabs, qn_g, kvn_g, wuq, wkv, gq, gk, name):
    s = p.shape[0]
    tm = _pick(s, 256)
    ct, s1, s2 = tabs

    def body(cq_ref, ckv_ref, kr_ref, ct_ref, s1_ref, s2_ref, qg_ref, kvg_ref, wuq_ref, wkv_ref, gq_ref, gk_ref,
             q_ref, k_ref, v_ref):
        cq, ckv, kr = cq_ref[...], ckv_ref[...], kr_ref[...]
        ctv, s1v, s2v = ct_ref[...], s1_ref[...], s2_ref[...]
        rq = lax.rsqrt(jnp.mean(cq * cq, axis=-1, keepdims=True) + EPS)
        q = jnp.dot((cq * rq * qg_ref[...]).astype(BF16), wuq_ref[...], preferred_element_type=F32)
        rk = lax.rsqrt(jnp.mean(ckv * ckv, axis=-1, keepdims=True) + EPS)
        kv = jnp.dot((ckv * rk * kvg_ref[...]).astype(BF16), wkv_ref[...], preferred_element_type=F32)
        v_ref[...] = kv[:, HP:].astype(BF16)
        for h in range(MLA_HEADS):
            hs = slice(h * LANES, (h + 1) * LANES)
            qh, _ = _head_norm(q[:, hs], gq_ref[...])
            q_ref[:, hs] = (_rope(qh, ctv, s1v, s2v) * _ATT_SCALE).astype(BF16)
            kh, _ = _head_norm(kv[:, hs] + kr, gk_ref[...])
            k_ref[:, hs] = _rope(kh, ctv, s1v, s2v).astype(BF16)

    cq_s, ckv_s, kr_s, tab_s = _mla_specs(tm)
    out = pl.BlockSpec((tm, HP), lambda i: (i, 0))
    return pl.pallas_call(
        body, name=name, grid=(s // tm,),
        in_specs=[cq_s, ckv_s, kr_s, tab_s, tab_s, tab_s, _const((1, MLA_Q_RANK)), _const((1, MLA_KV_RANK)),
                  _const((MLA_Q_RANK, HP)), _const((MLA_KV_RANK, 2 * HP)), _const((1, LANES)), _const((1, LANES))],
        out_specs=[out, out, out], out_shape=[jax.ShapeDtypeStruct((s, HP), BF16)] * 3,
        compiler_params=_params("parallel"))(p, p, p, ct, s1, s2, qn_g, kvn_g, wuq, wkv, gq, gk)


def _mla_pre_bwd(p, tabs, qn_g, kvn_g, wuq, wkv, gq, gk, dq, dk, dv, name):
    s = p.shape[0]
    tm = _pick(s, 256)
    ct, s1, s2 = tabs

    def body(cq_ref, ckv_ref, kr_ref, ct_ref, s1_ref, s2_ref, qg_ref, kvg_ref, wuq_ref, wkv_ref, gq_ref, gk_ref,
             dq_ref, dk_ref, dv_ref, dcq_ref, dckv_ref, dkr_ref, dwuq_ref, dwkv_ref, dqg_ref, dkvg_ref, dgq_ref, dgk_ref,
             dqp, dkvp):
        @pl.when(pl.program_id(0) == 0)
        def _():
            for ref in (dwuq_ref, dwkv_ref, dqg_ref, dkvg_ref, dgq_ref, dgk_ref):
                ref[...] = jnp.zeros_like(ref)

        cq, ckv, kr = cq_ref[...], ckv_ref[...], kr_ref[...]
        ctv, s1v, s2v = ct_ref[...], s1_ref[...], s2_ref[...]
        rq = lax.rsqrt(jnp.mean(cq * cq, axis=-1, keepdims=True) + EPS)
        qn = (cq * rq * qg_ref[...]).astype(BF16)
        q = jnp.dot(qn, wuq_ref[...], preferred_element_type=F32)
        rk = lax.rsqrt(jnp.mean(ckv * ckv, axis=-1, keepdims=True) + EPS)
        kvn = (ckv * rk * kvg_ref[...]).astype(BF16)
        kv = jnp.dot(kvn, wkv_ref[...], preferred_element_type=F32)
        gqv, gkv = gq_ref[...], gk_ref[...]
        dgq = jnp.zeros((1, LANES), F32)
        dgk = jnp.zeros((1, LANES), F32)
        dkr = jnp.zeros((tm, LANES), F32)
        for h in range(MLA_HEADS):
            hs = slice(h * LANES, (h + 1) * LANES)
            xq = q[:, hs]
            _, r = _head_norm(xq, gqv)
            d = _rope_bwd(dq_ref[:, hs], ctv, s1v, s2v)
            dgq = dgq + jnp.sum(d * xq * r, axis=0, keepdims=True)
            dqp[:, hs] = _head_norm_bwd(xq, r, gqv, d)
            xk = kv[:, hs] + kr
            _, r = _head_norm(xk, gkv)
            d = _rope_bwd(dk_ref[:, hs], ctv, s1v, s2v)
            dgk = dgk + jnp.sum(d * xk * r, axis=0, keepdims=True)
            dxk = _head_norm_bwd(xk, r, gkv, d)
            dkvp[:, hs] = dxk
            dkr = dkr + dxk
        dkvp[:, HP:] = dv_ref[...]
        dgq_ref[...] += dgq
        dgk_ref[...] += dgk
        dkr_ref[...] = dkr.astype(BF16)
        tn = (((0,), (0,)), ((), ()))
        nt = (((1,), (1,)), ((), ()))
        dq16 = dqp[...].astype(BF16)
        dwuq_ref[...] += lax.dot_general(qn, dq16, tn, preferred_element_type=F32)
        dqn = lax.dot_general(dq16, wuq_ref[...], nt, preferred_element_type=F32)
        dqg_ref[...] += jnp.sum(dqn * cq * rq, axis=0, keepdims=True)
        u = dqn * qg_ref[...]
        dcq_ref[...] = (rq * u - cq * (rq * rq * rq) * jnp.mean(cq * u, axis=-1, keepdims=True)).astype(BF16)
        dkv16 = dkvp[...].astype(BF16)
        dwkv_ref[...] += lax.dot_general(kvn, dkv16, tn, preferred_element_type=F32)
        dkvn = lax.dot_general(dkv16, wkv_ref[...], nt, preferred_element_type=F32)
        dkvg_ref[...] += jnp.sum(dkvn * ckv * rk, axis=0, keepdims=True)
        u = dkvn * kvg_ref[...]
        dckv_ref[...] = (rk * u - ckv * (rk * rk * rk) * jnp.mean(ckv * u, axis=-1, keepdims=True)).astype(BF16)

    cq_s, ckv_s, kr_s, tab_s = _mla_specs(tm)
    hd = pl.BlockSpec((tm, HP), lambda i: (i, 0))
    return pl.pallas_call(
        body, name=name, grid=(s // tm,),
        in_specs=[cq_s, ckv_s, kr_s, tab_s, tab_s, tab_s, _const((1, MLA_Q_RANK)), _const((1, MLA_KV_RANK)),
                  _const((MLA_Q_RANK, HP)), _const((MLA_KV_RANK, 2 * HP)), _const((1, LANES)), _const((1, LANES)),
                  hd, hd, hd],
        out_specs=[pl.BlockSpec((tm, MLA_Q_RANK), lambda i: (i, 0)), pl.BlockSpec((tm, MLA_KV_RANK), lambda i: (i, 0)),
                   pl.BlockSpec((tm, LANES), lambda i: (i, 0)), _const((MLA_Q_RANK, HP)), _const((MLA_KV_RANK, 2 * HP)),
                   _const((1, MLA_Q_RANK)), _const((1, MLA_KV_RANK)), _const((1, LANES)), _const((1, LANES))],
        out_shape=[jax.ShapeDtypeStruct((s, MLA_Q_RANK), BF16), jax.ShapeDtypeStruct((s, MLA_KV_RANK), BF16),
                   jax.ShapeDtypeStruct((s, LANES), BF16), jax.ShapeDtypeStruct((MLA_Q_RANK, HP), F32),
                   jax.ShapeDtypeStruct((MLA_KV_RANK, 2 * HP), F32), jax.ShapeDtypeStruct((1, MLA_Q_RANK), F32),
                   jax.ShapeDtypeStruct((1, MLA_KV_RANK), F32), jax.ShapeDtypeStruct((1, LANES), F32),
                   jax.ShapeDtypeStruct((1, LANES), F32)],
        scratch_shapes=[pltpu.VMEM((tm, HP), F32), pltpu.VMEM((tm, 2 * HP), F32)],
        compiler_params=_params("arbitrary"))(p, p, p, ct, s1, s2, qn_g, kvn_g, wuq, wkv, gq, gk, dq, dk, dv)


_ATT_SCALE = MLA_QK ** -0.5
_NEG = -1e30
_NT = (((1,), (1,)), ((), ()))
_TN = (((0,), (0,)), ((), ()))


def _tri_rows(step, n):
    i = step * 0
    for m in range(1, n):
        i = i + (step >= m * (m + 1) // 2).astype(jnp.int32)
    return i, step - i * (i + 1) // 2


def _tri_cols(step, n):
    j = step * 0
    for m in range(1, n):
        j = j + (step >= m * n - m * (m - 1) // 2).astype(jnp.int32)
    return j, j + step - (j * n - j * (j - 1) // 2)


def _diag_mask(t):
    return lax.broadcasted_iota(jnp.int32, (t, t), 0) <= lax.broadcasted_iota(jnp.int32, (t, t), 1)


def _attn_fwd(q, k, v, name, side=None):
    s = q.shape[0]
    t = _pick(s, 512)
    n = s // t

    def body(q_ref, k_ref, v_ref, o_ref, lse_ref, m_s, l_s, acc):
        i, j = _tri_rows(pl.program_id(1), n)

        @pl.when(j == 0)
        def _():
            m_s[...] = jnp.full_like(m_s, _NEG)
            l_s[...] = jnp.zeros_like(l_s)
            acc[...] = jnp.zeros_like(acc)

        def step(diagonal):
            sc = lax.dot_general(k_ref[...], q_ref[...], _NT, preferred_element_type=F32)
            if diagonal:
                sc = jnp.where(_diag_mask(t), sc, _NEG)
            m_new = jnp.maximum(m_s[...], jnp.max(sc, axis=0, keepdims=True))
            alpha = jnp.exp(m_s[...] - m_new)
            pr = jnp.exp(sc - m_new)
            l_s[...] = alpha * l_s[...] + jnp.sum(pr, axis=0, keepdims=True)
            acc[...] = alpha * acc[...] + lax.dot_general(v_ref[...], pr.astype(BF16), _TN, preferred_element_type=F32)
            m_s[...] = m_new

        @pl.when(j < i)
        def _():
            step(False)

        @pl.when(j == i)
        def _():
            step(True)
            o_ref[...] = (acc[...] / l_s[...]).T
            lse_ref[...] = m_s[...] + jnp.log(l_s[...])

    qs = pl.BlockSpec((t, LANES), lambda h, p: (_tri_rows(p, n)[0], h))
    ks = pl.BlockSpec((t, LANES), lambda h, p: (_tri_rows(p, n)[1], h))
    return _call(
        body, name=name, grid=(MLA_HEADS, n * (n + 1) // 2), in_specs=[qs, ks, ks],
        out_specs=[qs, pl.BlockSpec((None, 1, t), lambda h, p: (h, 0, _tri_rows(p, n)[0]))],
        out_shape=[jax.ShapeDtypeStruct((s, HP), F32), jax.ShapeDtypeStruct((MLA_HEADS, 1, s), F32)],
        scratch_shapes=[pltpu.VMEM((1, t), F32), pltpu.VMEM((1, t), F32), pltpu.VMEM((LANES, t), F32)],
        args=(q, k, v), semantics=("parallel", "arbitrary"), side=side)


def _attn_bwd_dq(q, k, v, o, lse, do, name, side=None):
    s = q.shape[0]
    t = _pick(s, 512)
    n = s // t

    def body(q_ref, k_ref, v_ref, o_ref, lse_ref, do_ref, dq_ref, dl_ref, acc, dl_s):
        i, j = _tri_rows(pl.program_id(1), n)

        @pl.when(j == 0)
        def _():
            acc[...] = jnp.zeros_like(acc)
            dl_s[...] = jnp.sum((do_ref[...] * o_ref[...]).T, axis=0, keepdims=True)

        def step(diagonal):
            sc = lax.dot_general(k_ref[...], q_ref[...], _NT, preferred_element_type=F32)
            if diagonal:
                sc = jnp.where(_diag_mask(t), sc, _NEG)
            pr = jnp.exp(sc - lse_ref[...])
            dp = lax.dot_general(v_ref[...], do_ref[...].astype(BF16), _NT, preferred_element_type=F32)
            ds = (pr * (dp - dl_s[...])).astype(BF16)
            acc[...] += lax.dot_general(k_ref[...], ds, _TN, preferred_element_type=F32)

        @pl.when(j < i)
        def _():
            step(False)

        @pl.when(j == i)
        def _():
            step(True)
            dq_ref[...] = (acc[...] * _ATT_SCALE).T
            dl_ref[...] = dl_s[...]

    qs = pl.BlockSpec((t, LANES), lambda h, p: (_tri_rows(p, n)[0], h))
    ks = pl.BlockSpec((t, LANES), lambda h, p: (_tri_rows(p, n)[1], h))
    ls = pl.BlockSpec((None, 1, t), lambda h, p: (h, 0, _tri_rows(p, n)[0]))
    return _call(
        body, name=name, grid=(MLA_HEADS, n * (n + 1) // 2), in_specs=[qs, ks, ks, qs, ls, qs], out_specs=[qs, ls],
        out_shape=[jax.ShapeDtypeStruct((s, HP), F32), jax.ShapeDtypeStruct((MLA_HEADS, 1, s), F32)],
        scratch_shapes=[pltpu.VMEM((LANES, t), F32), pltpu.VMEM((1, t), F32)],
        args=(q, k, v, o, lse, do), semantics=("parallel", "arbitrary"), side=side)


def _attn_bwd_dkv(q, k, v, lse, delta, do, name, side=None):
    s = q.shape[0]
    t = _pick(s, 512)
    n = s // t

    def body(q_ref, k_ref, v_ref, lse_ref, dl_ref, do_ref, dk_ref, dv_ref, dk_acc, dv_acc):
        j, i = _tri_cols(pl.program_id(1), n)

        def step(diagonal):
            sc = lax.dot_general(k_ref[...], q_ref[...], _NT, preferred_element_type=F32)
            if diagonal:
                sc = jnp.where(_diag_mask(t), sc, _NEG)
            pr = jnp.exp(sc - lse_ref[...])
            do16 = do_ref[...].astype(BF16)
            dv_acc[...] += jnp.dot(pr.astype(BF16), do16, preferred_element_type=F32)
            dp = lax.dot_general(v_ref[...], do16, _NT, preferred_element_type=F32)
            ds = (pr * (dp - dl_ref[...])).astype(BF16)
            dk_acc[...] += jnp.dot(ds, q_ref[...], preferred_element_type=F32)

        @pl.when(i == j)
        def _():
            dk_acc[...] = jnp.zeros_like(dk_acc)
            dv_acc[...] = jnp.zeros_like(dv_acc)
            step(True)

        @pl.when(i > j)
        def _():
            step(False)

        @pl.when(i == n - 1)
        def _():
            dk_ref[...] = dk_acc[...]
            dv_ref[...] = dv_acc[...]

    qs = pl.BlockSpec((t, LANES), lambda h, p: (_tri_cols(p, n)[1], h))
    ks = pl.BlockSpec((t, LANES), lambda h, p: (_tri_cols(p, n)[0], h))
    ls = pl.BlockSpec((None, 1, t), lambda h, p: (h, 0, _tri_cols(p, n)[1]))
    return _call(
        body, name=name, grid=(MLA_HEADS, n * (n + 1) // 2), in_specs=[qs, ks, ks, ls, ls, qs], out_specs=[ks, ks],
        out_shape=[jax.ShapeDtypeStruct((s, HP), F32)] * 2,
        scratch_shapes=[pltpu.VMEM((t, LANES), F32), pltpu.VMEM((t, LANES), F32)],
        args=(q, k, v, lse, delta, do), semantics=("parallel", "arbitrary"), side=side)


XBC = HP + 2 * SSD_GROUPS * SSD_STATE
BCW = 2 * SSD_GROUPS * SSD_STATE


def _conv_fwd(p, col0, width, conv_w, conv_b, name):
    s = p.shape[0]
    c0, nblk = col0 // LANES, width // LANES

    def body(x_ref, w_ref, b_ref, o_ref, pad):
        pad[0:8, :] = jnp.zeros((8, LANES), F32)
        pad[8:s + 8, :] = x_ref[...].astype(F32)
        acc = jnp.broadcast_to(b_ref[...], (s, LANES))
        for t in range(SSD_CONV):
            acc = acc + pad[pl.ds(8 - (SSD_CONV - 1) + t, s), :] * w_ref[t:t + 1, :]
        o_ref[...] = acc * _sigmoid(acc)

    return pl.pallas_call(
        body, name=name, grid=(nblk,),
        in_specs=[pl.BlockSpec((s, LANES), lambda j: (0, c0 + j)), pl.BlockSpec((SSD_CONV, LANES), lambda j: (0, j)),
                  pl.BlockSpec((1, LANES), lambda j: (0, j))],
        out_specs=pl.BlockSpec((s, LANES), lambda j: (0, j)), out_shape=jax.ShapeDtypeStruct((s, width), F32),
        scratch_shapes=[pltpu.VMEM((s + 8, LANES), F32)], compiler_params=_params("parallel"))(p, conv_w, conv_b)


def _conv_bwd(p, col0, width, conv_w, conv_b, dact, name):
    s = p.shape[0]
    c0, nblk = col0 // LANES, width // LANES

    def body(x_ref, w_ref, b_ref, d_ref, dx_ref, dw_ref, db_ref, pad, padd):
        pad[0:8, :] = jnp.zeros((8, LANES), F32)
        pad[8:s + 8, :] = x_ref[...].astype(F32)
        acc = jnp.broadcast_to(b_ref[...], (s, LANES))
        for t in range(SSD_CONV):
            acc = acc + pad[pl.ds(8 - (SSD_CONV - 1) + t, s), :] * w_ref[t:t + 1, :]
        sg = _sigmoid(acc)
        dpre = d_ref[...] * (sg * (1.0 + acc * (1.0 - sg)))
        padd[0:s, :] = dpre
        padd[s:s + 8, :] = jnp.zeros((8, LANES), F32)
        dx = jnp.zeros((s, LANES), F32)
        for t in range(SSD_CONV):
            dx = dx + padd[pl.ds(SSD_CONV - 1 - t, s), :] * w_ref[t:t + 1, :]
            dw_ref[t:t + 1, :] = jnp.sum(dpre * pad[pl.ds(8 - (SSD_CONV - 1) + t, s), :], axis=0, keepdims=True)
        dx_ref[...] = dx.astype(BF16)
        db_ref[...] = jnp.sum(dpre, axis=0, keepdims=True)

    blk = pl.BlockSpec((s, LANES), lambda j: (0, j))
    return pl.pallas_call(
        body, name=name, grid=(nblk,),
        in_specs=[pl.BlockSpec((s, LANES), lambda j: (0, c0 + j)), pl.BlockSpec((SSD_CONV, LANES), lambda j: (0, j)),
                  pl.BlockSpec((1, LANES), lambda j: (0, j)), blk],
        out_specs=[blk, pl.BlockSpec((SSD_CONV, LANES), lambda j: (0, j)), pl.BlockSpec((1, LANES), lambda j: (0, j))],
        out_shape=[jax.ShapeDtypeStruct((s, width), BF16), jax.ShapeDtypeStruct((SSD_CONV, width), F32),
                   jax.ShapeDtypeStruct((1, width), F32)],
        scratch_shapes=[pltpu.VMEM((s + 8, LANES), F32), pltpu.VMEM((s + 8, LANES), F32)],
        compiler_params=_params("parallel"))(p, conv_w, conv_b, dact)


def _softplus(x):
    return jnp.maximum(x, 0.0) + jnp.log(1.0 + jnp.exp(-jnp.abs(x)))


def _dt_fwd(p, dt_bias, a_log, name):
    s = p.shape[0]
    tm = _pick(s, 512)

    def body(x_ref, b_ref, a_ref, dt_ref, da_ref):
        dtv = _softplus(x_ref[...] + b_ref[...])
        dav = dtv * (-jnp.exp(a_ref[...]))
        for h in range(SSD_HEADS):
            hs = slice(h * LANES, (h + 1) * LANES)
            dt_ref[:, hs] = jnp.broadcast_to(dtv[:, h:h + 1], (tm, LANES))
            da_ref[:, hs] = jnp.broadcast_to(dav[:, h:h + 1], (tm, LANES))

    out = pl.BlockSpec((tm, HP), lambda i: (i, 0))
    return pl.pallas_call(
        body, name=name, grid=(s // tm,),
        in_specs=[pl.BlockSpec((tm, LANES), lambda i: (i, T_DT // LANES)), _const((1, LANES)), _const((1, LANES))],
        out_specs=[out, out], out_shape=[jax.ShapeDtypeStruct((s, HP), F32)] * 2,
        compiler_params=_params("parallel"))(p, dt_bias, a_log)


def _dt_bwd(p, dt_bias, a_log, dda, ddtx, name):
    s = p.shape[0]
    tm = _pick(s, 512)

    def body(x_ref, b_ref, a_ref, dda_ref, ddtx_ref, dx_ref, db_ref, dal_ref):
        @pl.when(pl.program_id(0) == 0)
        def _():
            db_ref[...] = jnp.zeros_like(db_ref)
            dal_ref[...] = jnp.zeros_like(dal_ref)

        x = x_ref[...] + b_ref[...]
        dtv = _softplus(x)
        av = -jnp.exp(a_ref[...])
        lane = lax.broadcasted_iota(jnp.int32, (tm, LANES), 1)
        pa = jnp.zeros((tm, LANES), F32)
        px = jnp.zeros((tm, LANES), F32)
        for h in range(SSD_HEADS):
            pa = jnp.where(lane == h, dda_ref[:, h * LANES:(h + 1) * LANES], pa)
            px = jnp.where(lane == h, ddtx_ref[:, h * LANES:(h + 1) * LANES], px)
        draw = (pa * av + px) * _sigmoid(x)
        dx_ref[...] = draw.astype(BF16)
        db_ref[...] += jnp.sum(draw, axis=0, keepdims=True)
        dal_ref[...] += jnp.sum(pa * dtv, axis=0, keepdims=True) * av

    hd = pl.BlockSpec((tm, HP), lambda i: (i, 0))
    return pl.pallas_call(
        body, name=name, grid=(s // tm,),
        in_specs=[pl.BlockSpec((tm, LANES), lambda i: (i, T_DT // LANES)), _const((1, LANES)), _const((1, LANES)), hd, hd],
        out_specs=[pl.BlockSpec((tm, LANES), lambda i: (i, 0)), _const((1, LANES)), _const((1, LANES))],
        out_shape=[jax.ShapeDtypeStruct((s, LANES), BF16), jax.ShapeDtypeStruct((1, LANES), F32),
                   jax.ShapeDtypeStruct((1, LANES), F32)],
        compiler_params=_params("arbitrary"))(p, dt_bias, a_log, dda, ddtx)


def _cumsum_rows(x):
    row = lax.broadcasted_iota(jnp.int32, x.shape, 0)
    k = 1
    while k < x.shape[0]:
        x = x + jnp.where(row >= k, pltpu.roll(x, k, 0), 0.0)
        k *= 2
    return x


def _rev_cumsum_rows(x):
    n = x.shape[0]
    row = lax.broadcasted_iota(jnp.int32, x.shape, 0)
    k = 1
    while k < n:
        x = x + jnp.where(row < n - k, pltpu.roll(x, n - k, 0), 0.0)
        k *= 2
    return x


HPG = SSD_HEADS // SSD_GROUPS


def _chunk_decay(da):
    cs = _cumsum_rows(da)
    lm = jnp.exp(jnp.where(_tril_mask(), cs - cs.T, _NEG))
    return cs, lm, cs[CHUNK - 1:CHUNK, :]


def _scan_fwd(xs, bc, dtb, dab, name):
    s = xs.shape[0]
    nc = s // CHUNK

    def body(x_ref, b_ref, c_ref, dt_ref, da_ref, y_ref, sin_ref, state):
        @pl.when(pl.program_id(1) == 0)
        def _():
            state[...] = jnp.zeros_like(state)

        bv = b_ref[...]
        b16, c16 = bv.astype(BF16), c_ref[...].astype(BF16)
        g = lax.dot_general(c16, b16, _NT, preferred_element_type=F32)
        for hh in range(HPG):
            hs = slice(hh * LANES, (hh + 1) * LANES)
            st = state[hh]
            sin_ref[hh] = st
            cs, lm, cl = _chunk_decay(da_ref[:, hs])
            xd = (x_ref[:, hs] * dt_ref[:, hs]).astype(BF16)
            y = jnp.dot((g * lm).astype(BF16), xd, preferred_element_type=F32)
            y_ref[:, hs] = y + jnp.dot(c16, st.astype(BF16), preferred_element_type=F32) * jnp.exp(cs)
            bd = (bv * jnp.exp(cl - cs)).astype(BF16)
            state[hh] = jnp.exp(cl) * st + lax.dot_general(bd, xd, _TN, preferred_element_type=F32)

    gw = HPG * LANES
    hd = pl.BlockSpec((CHUNK, gw), lambda g, c: (c, g))
    return pl.pallas_call(
        body, name=name, grid=(SSD_GROUPS, nc),
        in_specs=[hd, pl.BlockSpec((CHUNK, LANES), lambda g, c: (c, g)),
                  pl.BlockSpec((CHUNK, LANES), lambda g, c: (c, SSD_GROUPS + g)), hd, hd],
        out_specs=[hd, pl.BlockSpec((HPG, None, SSD_STATE, LANES), lambda g, c: (g, c, 0, 0))],
        out_shape=[jax.ShapeDtypeStruct((s, HP), F32), jax.ShapeDtypeStruct((SSD_HEADS, nc, SSD_STATE, LANES), F32)],
        scratch_shapes=[pltpu.VMEM((HPG, SSD_STATE, LANES), F32)],
        compiler_params=_params("parallel", "arbitrary"))(xs, bc, bc, dtb, dab)


def _scan_bwd(xs, bc, dtb, dab, s_in, dy, d_vec, name):
    s = xs.shape[0]
    nc = s // CHUNK

    def body(x_ref, b_ref, c_ref, dt_ref, da_ref, sin_ref, dy_ref, dv_ref, dx_ref, db_ref, dc_ref, dda_ref, ddtx_ref, dstate):
        @pl.when(pl.program_id(1) == 0)
        def _():
            dstate[...] = jnp.zeros_like(dstate)

        bv = b_ref[...]
        b16, c16 = bv.astype(BF16), c_ref[...].astype(BF16)
        g = lax.dot_general(c16, b16, _NT, preferred_element_type=F32)
        row = lax.broadcasted_iota(jnp.int32, (CHUNK, 1), 0)
        dbm = jnp.zeros((CHUNK, SSD_STATE), F32)
        dcm = jnp.zeros((CHUNK, SSD_STATE), F32)
        for hh in range(HPG):
            hs = slice(hh * LANES, (hh + 1) * LANES)
            st, ds = sin_ref[hh], dstate[hh]
            st16, ds16 = st.astype(BF16), ds.astype(BF16)
            xv, dtv, dyv = x_ref[:, hs], dt_ref[:, hs], dy_ref[:, hs]
            cs, lm, cl = _chunk_decay(da_ref[:, hs])
            ecs, ecl = jnp.exp(cs), jnp.exp(cl)
            decay = jnp.exp(cl - cs)
            xd = (xv * dtv).astype(BF16)
            dy16 = dyv.astype(BF16)
            dye = (dyv * ecs).astype(BF16)
            yoff = jnp.dot(c16, st16, preferred_element_type=F32) * ecs
            dcs = jnp.sum(dyv * yoff, axis=-1, keepdims=True)
            dcm = dcm + lax.dot_general(dye, st16, _NT, preferred_element_type=F32)
            dstate[hh] = ecl * ds + lax.dot_general(c16, dye, _TN, preferred_element_type=F32)
            dcl = jnp.sum(jnp.sum(ds * st, axis=0, keepdims=True), axis=1, keepdims=True) * ecl[:, 0:1]
            bd32 = bv * decay
            qm = lax.dot_general(xd, ds16, _NT, preferred_element_type=F32)
            dbm = dbm + qm * decay
            w = jnp.sum(bd32 * qm, axis=-1, keepdims=True)
            dcs = dcs - w
            dcl = dcl + jnp.sum(w, axis=0, keepdims=True)
            dxd = jnp.dot(bd32.astype(BF16), ds16, preferred_element_type=F32)
            m16 = (g * lm).astype(BF16)
            dm = lax.dot_general(dy16, xd, _NT, preferred_element_type=F32)
            dxd = dxd + lax.dot_general(m16, dy16, _TN, preferred_element_type=F32)
            dg = dm * lm
            dg16 = dg.astype(BF16)
            tt = dg * g
            dcm = dcm + jnp.dot(dg16, b16, preferred_element_type=F32)
            dbm = dbm + lax.dot_general(dg16, c16, _TN, preferred_element_type=F32)
            dcs = dcs + jnp.sum(tt, axis=-1, keepdims=True) - jnp.sum(tt.T, axis=-1, keepdims=True)
            dcs = dcs + jnp.where(row == CHUNK - 1, dcl, 0.0)
            dda_ref[:, hs] = _rev_cumsum_rows(jnp.broadcast_to(dcs, (CHUNK, LANES)))
            ddtx_ref[:, hs] = jnp.broadcast_to(jnp.sum(dxd * xv, axis=-1, keepdims=True), (CHUNK, LANES))
            dx_ref[:, hs] = dxd * dtv + dyv * dv_ref[:, hs]
        db_ref[...] = dbm
        dc_ref[...] = dcm

    gw = HPG * LANES
    hd = pl.BlockSpec((CHUNK, gw), lambda g, c: (nc - 1 - c, g))
    gp = pl.BlockSpec((CHUNK, LANES), lambda g, c: (nc - 1 - c, g))
    return pl.pallas_call(
        body, name=name, grid=(SSD_GROUPS, nc),
        in_specs=[hd, gp, pl.BlockSpec((CHUNK, LANES), lambda g, c: (nc - 1 - c, SSD_GROUPS + g)), hd, hd,
                  pl.BlockSpec((HPG, None, SSD_STATE, LANES), lambda g, c: (g, nc - 1 - c, 0, 0)), hd,
                  pl.BlockSpec((1, gw), lambda g, c: (0, g))],
        out_specs=[hd, gp, gp, hd, hd],
        out_shape=[jax.ShapeDtypeStruct((s, HP), F32), jax.ShapeDtypeStruct((s, SSD_GROUPS * SSD_STATE), F32),
                   jax.ShapeDtypeStruct((s, SSD_GROUPS * SSD_STATE), F32), jax.ShapeDtypeStruct((s, HP), F32),
                   jax.ShapeDtypeStruct((s, HP), F32)],
        scratch_shapes=[pltpu.VMEM((HPG, SSD_STATE, LANES), F32)],
        compiler_params=_params("parallel", "arbitrary"))(xs, bc, bc, dtb, dab, s_in, dy, d_vec)


_GN = SSD_INNER // SSD_GROUPS
_GW = HP // SSD_GROUPS


def _ssd_post_fwd(y, xbc, p, d_vec, gain, name):
    s = y.shape[0]
    tm = _pick(s, 512)

    def body(y_ref, x_ref, z_ref, d_ref, g_ref, o_ref):
        z = z_ref[...].astype(F32)
        y2 = (y_ref[...] + x_ref[...] * d_ref[...]) * (z * _sigmoid(z))
        for g in range(SSD_GROUPS):
            gs = slice(g * _GW, (g + 1) * _GW)
            yg = y2[:, gs]
            r = lax.rsqrt(jnp.sum(yg * yg, axis=-1, keepdims=True) * (1.0 / _GN) + EPS)
            o_ref[:, gs] = (yg * r * g_ref[:, gs]).astype(BF16)

    hd = pl.BlockSpec((tm, HP), lambda i: (i, 0))
    return pl.pallas_call(
        body, name=name, grid=(s // tm,),
        in_specs=[hd, hd, pl.BlockSpec((tm, HP), lambda i: (i, C_Z // HP)), _const((1, HP)), _const((1, HP))],
        out_specs=hd, out_shape=jax.ShapeDtypeStruct((s, HP), BF16), compiler_params=_params("parallel"))(y, xbc, p, d_vec, gain)


def _ssd_post_bwd(y, xbc, p, d_vec, gain, dyn, name):
    s = y.shape[0]
    tm = _pick(s, 512)

    def body(y_ref, x_ref, z_ref, d_ref, g_ref, dn_ref, dy_ref, dz_ref, dg_ref, dd_ref):
        @pl.when(pl.program_id(0) == 0)
        def _():
            dg_ref[...] = jnp.zeros_like(dg_ref)
            dd_ref[...] = jnp.zeros_like(dd_ref)

        z, xv = z_ref[...].astype(F32), x_ref[...]
        sg = _sigmoid(z)
        sz = z * sg
        yt = y_ref[...] + xv * d_ref[...]
        y2 = yt * sz
        for g in range(SSD_GROUPS):
            gs = slice(g * _GW, (g + 1) * _GW)
            yg, dn = y2[:, gs], dn_ref[:, gs]
            r = lax.rsqrt(jnp.sum(yg * yg, axis=-1, keepdims=True) * (1.0 / _GN) + EPS)
            u = dn * g_ref[:, gs]
            dy2 = r * u - yg * (r * r * r) * (jnp.sum(yg * u, axis=-1, keepdims=True) * (1.0 / _GN))
            dg_ref[:, gs] += jnp.sum(dn * yg * r, axis=0, keepdims=True)
            dyt = dy2 * sz[:, gs]
            dy_ref[:, gs] = dyt
            dz_ref[:, gs] = (dy2 * yt[:, gs] * (sg[:, gs] * (1.0 + z[:, gs] * (1.0 - sg[:, gs])))).astype(BF16)
            dd_ref[:, gs] += jnp.sum(dyt * xv[:, gs], axis=0, keepdims=True)

    hd = pl.BlockSpec((tm, HP), lambda i: (i, 0))
    return pl.pallas_call(
        body, name=name, grid=(s // tm,),
        in_specs=[hd, hd, pl.BlockSpec((tm, HP), lambda i: (i, C_Z // HP)), _const((1, HP)), _const((1, HP)), hd],
        out_specs=[hd, hd, _const((1, HP)), _const((1, HP))],
        out_shape=[jax.ShapeDtypeStruct((s, HP), F32), jax.ShapeDtypeStruct((s, HP), BF16),
                   jax.ShapeDtypeStruct((1, HP), F32), jax.ShapeDtypeStruct((1, HP), F32)],
        compiler_params=_params("arbitrary"))(y, xbc, p, d_vec, gain, dyn)


def _merge_fwd(p, ya, o, yc, wb0, wb1, wb2, w_out, x, name):
    s = p.shape[0]
    tm = _pick(s, 512)

    def body(g_ref, ya_ref, o_ref, yc_ref, w0_ref, w1_ref, w2_ref, wo_ref, x_ref, mg_ref, y_ref):
        acc = jnp.zeros((tm, D_MODEL), F32)
        for i, (b_ref, w_ref) in enumerate(((ya_ref, w0_ref), (o_ref, w1_ref), (yc_ref, w2_ref))):
            t = jnp.dot(b_ref[...].astype(BF16), w_ref[...], preferred_element_type=F32)
            acc = acc + _sigmoid(g_ref[:, i * D_MODEL:(i + 1) * D_MODEL].astype(F32)) * t
        mg = acc.astype(BF16)
        mg_ref[...] = mg
        y_ref[...] = x_ref[...] + jnp.dot(mg, wo_ref[...], preferred_element_type=F32)

    row = pl.BlockSpec((tm, D_MODEL), lambda i: (i, 0))
    return pl.pallas_call(
        body, name=name, grid=(s // tm,),
        in_specs=[pl.BlockSpec((tm, 3 * D_MODEL), lambda i: (i, C_G // (3 * D_MODEL))),
                  pl.BlockSpec((tm, GM_WIDTH), lambda i: (i, 0)), row, row,
                  _resident((GM_WIDTH, D_MODEL)), _resident((HP, D_MODEL)), _resident((HP, D_MODEL)),
                  _resident((D_MODEL, D_MODEL)), row],
        out_specs=[row, row],
        out_shape=[jax.ShapeDtypeStruct((s, D_MODEL), BF16), jax.ShapeDtypeStruct((s, D_MODEL), F32)],
        compiler_params=_params("parallel"))(p, ya, o, yc, wb0, wb1, wb2, w_out, x)


def _merge_bwd(p, ya, o, yc, wb0, wb1, wb2, w_out, dy, name):
    s = p.shape[0]
    tm = _pick(s, 512)

    def body(g_ref, ya_ref, o_ref, yc_ref, w0_ref, w1_ref, w2_ref, wo_ref, dy_ref,
             d0_ref, d1_ref, d2_ref, dg_ref, dya_ref, do_ref, dyc_ref):
        dm = lax.dot_general(dy_ref[...].astype(BF16), wo_ref[...], _NT, preferred_element_type=F32)
        for i, (b_ref, w_ref, d_ref, db_ref) in enumerate(((ya_ref, w0_ref, d0_ref, dya_ref), (o_ref, w1_ref, d1_ref, do_ref),
                                                            (yc_ref, w2_ref, d2_ref, dyc_ref))):
            cs = slice(i * D_MODEL, (i + 1) * D_MODEL)
            t = jnp.dot(b_ref[...].astype(BF16), w_ref[...], preferred_element_type=F32)
            sg = _sigmoid(g_ref[:, cs].astype(F32))
            dt16 = (dm * sg).astype(BF16)
            d_ref[...] = dt16
            dg_ref[:, cs] = (dm * t * sg * (1.0 - sg)).astype(BF16)
            db_ref[...] = lax.dot_general(dt16, w_ref[...], _NT, preferred_element_type=F32)

    row = pl.BlockSpec((tm, D_MODEL), lambda i: (i, 0))
    nar = pl.BlockSpec((tm, GM_WIDTH), lambda i: (i, 0))
    wide = pl.BlockSpec((tm, 3 * D_MODEL), lambda i: (i, 0))
    return pl.pallas_call(
        body, name=name, grid=(s // tm,),
        in_specs=[pl.BlockSpec((tm, 3 * D_MODEL), lambda i: (i, C_G // (3 * D_MODEL))), nar, row, row,
                  _resident((GM_WIDTH, D_MODEL)), _resident((HP, D_MODEL)), _resident((HP, D_MODEL)),
                  _resident((D_MODEL, D_MODEL)), row],
        out_specs=[row, row, row, wide, nar, row, row],
        out_shape=[jax.ShapeDtypeStruct((s, D_MODEL), BF16)] * 3 + [jax.ShapeDtypeStruct((s, 3 * D_MODEL), BF16),
                   jax.ShapeDtypeStruct((s, GM_WIDTH), F32), jax.ShapeDtypeStruct((s, D_MODEL), F32),
                   jax.ShapeDtypeStruct((s, D_MODEL), F32)],
        compiler_params=_params("parallel"))(p, ya, o, yc, wb0, wb1, wb2, w_out, dy)


def _loss_head(y, target, name):
    s, d = y.shape
    tm = _pick(s, 512)

    def body(y_ref, t_ref, dy_ref, sq_ref):
        @pl.when(pl.program_id(0) == 0)
        def _():
            sq_ref[...] = jnp.zeros_like(sq_ref)

        e = y_ref[...] - t_ref[...]
        dy_ref[...] = e * (1.0 / d)
        sq_ref[...] += jnp.sum(e * e, axis=0, keepdims=True)

    row = pl.BlockSpec((tm, d), lambda i: (i, 0))
    return pl.pallas_call(
        body, name=name, grid=(s // tm,), in_specs=[row, row], out_specs=[row, _const((1, d))],
        out_shape=[jax.ShapeDtypeStruct((s, d), F32), jax.ShapeDtypeStruct((1, d), F32)],
        compiler_params=_params("arbitrary"))(y, target)


def _adamw(w, g, m, v, name):
    rows, cols = w.shape
    tr = rows
    for cand in (512, 256, 128, 64, 32, 16, 8):
        if rows % cand == 0 and cand * cols * 4 <= 3 * 1024 * 1024:
            tr = cand
            break

    def body(w_ref, g_ref, m_ref, v_ref, d_ref, nm_ref, nv_ref):
        d_ref[...], nm_ref[...], nv_ref[...] = _adam_update(w_ref[...], g_ref[...], m_ref[...], v_ref[...])

    blk = pl.BlockSpec((tr, cols), lambda i: (i, 0))
    return pl.pallas_call(
        body, name=name, grid=(rows // tr,), in_specs=[blk] * 4, out_specs=[blk] * 3,
        out_shape=[jax.ShapeDtypeStruct((rows, cols), F32)] * 3, compiler_params=_params("parallel"))(w, g, m, v)


def _adam_update(w, g, m, v):
    nm = ADAM_B1 * m + (1.0 - ADAM_B1) * g
    nv = ADAM_B2 * v + (1.0 - ADAM_B2) * (g * g)
    c1 = 1.0 - ADAM_B1 ** ADAM_STEP
    c2 = 1.0 - ADAM_B2 ** ADAM_STEP
    return -ADAM_LR * ((nm / c1) / (jnp.sqrt(nv / c2) + ADAM_EPS) + ADAM_WD * w), nm, nv


def _adamw_sharded(w, m, v, mine, theirs, name):
    depth, rows, cols = w.shape
    tr = _row_tile(rows // 2, cols, 1024 * 1024)
    nb = rows // 2 // tr

    def body(w_ref, m_ref, v_ref, a_ref, b_ref, g_ref, d_ref, nm_ref, nv_ref):
        c = lax.axis_index("c")
        g = jnp.where(pl.program_id(1) // nb == c, a_ref[...], b_ref[...])
        g_ref[...] = g
        d_ref[...], nm_ref[...], nv_ref[...] = _adam_update(w_ref[...], g, m_ref[...], v_ref[...])

    blk = pl.BlockSpec((None, tr, cols), lambda l, i: (l, i, 0))
    half = pl.BlockSpec((None, tr, cols), lambda l, i: (l, i % nb, 0))
    return pl.pallas_call(
        body, name=name, grid=(depth, rows // tr), in_specs=[blk, blk, blk, half, half], out_specs=[blk] * 4,
        out_shape=[jax.ShapeDtypeStruct((depth, rows, cols), F32)] * 4,
        compiler_params=_params("parallel", "parallel"))(w, m, v, mine, theirs)


ANY = pl.BlockSpec(memory_space=pl.ANY)


def _me():
    return lax.axis_index("x"), lax.axis_index("y"), lax.axis_index("c")


def _other_chips(x, y):
    return [(1 - x, y), (x, 1 - y), (1 - x, 1 - y)]


def _chip_index(cx, cy):
    return 2 * cx + cy


class _Exchange:
    def __init__(self, ins, out_shapes, n_sems, start, finish):
        self.ins, self.out_shapes, self.n_sems, self.start, self.finish = list(ins), list(out_shapes), n_sems, start, finish


def _sem_scratch(ex):
    return [pltpu.SemaphoreType.DMA((ex.n_sems,)), pltpu.SemaphoreType.DMA((ex.n_sems,))]


def _run_exchange(ex, name):
    n_in, n_out = len(ex.ins), len(ex.out_shapes)

    def body(*refs):
        in_refs, out_refs, (send, recv) = refs[:n_in], refs[n_in:n_in + n_out], refs[n_in + n_out:]
        ex.start(in_refs, out_refs, send, recv)
        ex.finish(in_refs, out_refs, send, recv)

    return pl.pallas_call(body, name=name, in_specs=[ANY] * n_in, out_specs=[ANY] * n_out, out_shape=ex.out_shapes,
                          scratch_shapes=_sem_scratch(ex))(*ex.ins)


def _call(body, *, name, grid, in_specs, out_specs, out_shape, scratch_shapes, args, semantics, side=None):
    if side is None:
        return pl.pallas_call(body, name=name, grid=grid, in_specs=in_specs, out_specs=out_specs, out_shape=out_shape,
                              scratch_shapes=scratch_shapes, compiler_params=_params(*semantics))(*args), []
    n_in, n_out, n_sc = len(in_specs), len(out_specs), len(scratch_shapes)
    s_in, s_out = len(side.ins), len(side.out_shapes)

    def hosted(*refs):
        pos = 0
        parts = []
        for size in (n_in, s_in, n_out, s_out, n_sc, 2):
            parts.append(refs[pos:pos + size])
            pos += size
        ins, sins, outs, souts, scratch, (send, recv) = parts
        ids = [pl.program_id(a) for a in range(len(grid))]
        first = functools.reduce(jnp.logical_and, [i == 0 for i in ids])
        last = functools.reduce(jnp.logical_and, [i == g - 1 for i, g in zip(ids, grid)])

        @pl.when(first)
        def _():
            side.start(sins, souts, send, recv)

        body(*ins, *outs, *scratch)

        @pl.when(last)
        def _():
            side.finish(sins, souts, send, recv)

    res = pl.pallas_call(
        hosted, name=name, grid=grid, in_specs=list(in_specs) + [ANY] * s_in, out_specs=list(out_specs) + [ANY] * s_out,
        out_shape=list(out_shape) + side.out_shapes, scratch_shapes=list(scratch_shapes) + _sem_scratch(side),
        compiler_params=_params(*["arbitrary"] * len(grid)))(*args, *side.ins)
    return res[:n_out], res[n_out:]


def _half(ref_rows, c):
    return pl.ds(c * (ref_rows // 2), ref_rows // 2)


def _gather_exchange(shards, layer):
    n = len(shards)
    rows = [a.shape[1] for a in shards]

    def copy(in_refs, out_refs, send, recv, t, k, chip, hc, to, from_input=False):
        dst = out_refs[t].at[chip, _half(rows[t], hc)]
        src = in_refs[t].at[layer, _half(rows[t], hc)] if from_input else dst
        return pltpu.make_async_remote_copy(src_ref=src, dst_ref=dst, send_sem=send.at[6 * t + k], recv_sem=recv.at[6 * t + k],
                                            device_id=to, device_id_type=MESH)

    def start(in_refs, out_refs, send, recv):
        x, y, c = _me()
        for j, chip in enumerate(_other_chips(x, y)):
            for t in range(n):
                copy(in_refs, out_refs, send, recv, t, j, _chip_index(x, y), c, (*chip, c), from_input=True).start()

    def finish(in_refs, out_refs, send, recv):
        x, y, c = _me()
        chips = _other_chips(x, y)
        passed = []
        for j, chip in enumerate(chips):
            for t in range(n):
                copy(in_refs, out_refs, send, recv, t, j, _chip_index(*chip), c, (x, y, c)).wait_recv()
                cp = copy(in_refs, out_refs, send, recv, t, 3 + j, _chip_index(*chip), c, (x, y, 1 - c))
                cp.start()
                passed.append(cp)
        for j, chip in enumerate(chips):
            for t in range(n):
                copy(in_refs, out_refs, send, recv, t, 3 + j, _chip_index(*chip), 1 - c, (x, y, c)).wait_recv()
                copy(in_refs, out_refs, send, recv, t, j, _chip_index(x, y), c, (*chip, c), from_input=True).wait_send()
        for cp in passed:
            cp.wait_send()

    return _Exchange(shards, [jax.ShapeDtypeStruct((N_CHIPS,) + a.shape[1:], a.dtype) for a in shards], 6 * n, start, finish)


def _place_own(gathered, shards, layer):
    cx, cy, _ = _me()
    return [lax.dynamic_update_index_in_dim(g, a[layer], _chip_index(cx, cy), 0) for g, a in zip(gathered, shards)]


def _pair_exchange(gs):
    n = len(gs)
    rows = [a.shape[1] for a in gs]

    def copies(in_refs, out_refs, send, recv):
        x, y, c = _me()
        return [pltpu.make_async_remote_copy(src_ref=in_refs[t].at[:, _half(rows[t], 1 - c)], dst_ref=out_refs[t],
                                             send_sem=send.at[t], recv_sem=recv.at[t], device_id=(x, y, 1 - c),
                                             device_id_type=MESH) for t in range(n)]

    def start(*refs):
        for cp in copies(*refs):
            cp.start()

    def finish(*refs):
        for cp in copies(*refs):
            cp.wait()

    return _Exchange(gs, [jax.ShapeDtypeStruct((N_CHIPS, a.shape[1] // 2, a.shape[2]), a.dtype) for a in gs], n, start, finish)


def _row_tile(rows, cols, budget=2 * 1024 * 1024):
    best = None
    for t in range(8, rows + 1, 8):
        if rows % t == 0 and t * cols * 4 <= budget:
            best = t
    return best or rows


def _pair_add(g, got, name):
    _, rows, cols = g.shape
    tr = _row_tile(rows // 2, cols)
    nb = rows // 2 // tr

    def body(lo_ref, hi_ref, r_ref, o16_ref, own_ref):
        x, y, c = _me()
        tot = jnp.where(c == 0, lo_ref[...], hi_ref[...]) + r_ref[...]
        o16_ref[...] = tot.astype(BF16)

        @pl.when(pl.program_id(1) == _chip_index(x, y))
        def _():
            own_ref[...] = tot

    blk = (None, tr, cols)
    return pl.pallas_call(
        body, name=name, grid=(nb, N_CHIPS),
        in_specs=[pl.BlockSpec(blk, lambda i, k: (k, i, 0)), pl.BlockSpec(blk, lambda i, k: (k, i + nb, 0)),
                  pl.BlockSpec(blk, lambda i, k: (k, i, 0))],
        out_specs=[pl.BlockSpec(blk, lambda i, k: (k, i, 0)), pl.BlockSpec((tr, cols), lambda i, k: (i, 0))],
        out_shape=[jax.ShapeDtypeStruct((N_CHIPS, rows // 2, cols), BF16), jax.ShapeDtypeStruct((rows // 2, cols), F32)],
        compiler_params=_params("parallel", "arbitrary"))(g, g, got)


def _chip_exchange(parts):
    n = len(parts)

    def copies(in_refs, out_refs, send, recv):
        x, y, c = _me()
        return [pltpu.make_async_remote_copy(src_ref=in_refs[t].at[_chip_index(*chip)], dst_ref=out_refs[t].at[j],
                                             send_sem=send.at[3 * t + j], recv_sem=recv.at[3 * t + j],
                                             device_id=(*chip, c), device_id_type=MESH)
                for j, chip in enumerate(_other_chips(x, y)) for t in range(n)]

    def start(*refs):
        for cp in copies(*refs):
            cp.start()

    def finish(*refs):
        for cp in copies(*refs):
            cp.wait()

    return _Exchange(parts, [jax.ShapeDtypeStruct((3,) + a.shape[1:], a.dtype) for a in parts], 3 * n, start, finish)


def _chip_add(own, got, name):
    rows, cols = own.shape
    tr = _row_tile(rows, cols, 1024 * 1024)

    def body(own_ref, got_ref, o_ref):
        acc = own_ref[...]
        for j in range(3):
            acc = acc + got_ref[j].astype(F32)
        o_ref[...] = acc

    return pl.pallas_call(
        body, name=name, grid=(rows // tr,),
        in_specs=[pl.BlockSpec((tr, cols), lambda i: (i, 0)), pl.BlockSpec((3, tr, cols), lambda i: (0, i, 0))],
        out_specs=pl.BlockSpec((tr, cols), lambda i: (i, 0)),
        out_shape=jax.ShapeDtypeStruct((rows, cols), F32), compiler_params=_params("parallel"))(own, got)


def _pair_share(halves):
    n = len(halves)

    def copies(in_refs, out_refs, send, recv):
        x, y, c = _me()
        return [pltpu.make_async_remote_copy(src_ref=in_refs[t], dst_ref=out_refs[t], send_sem=send.at[t],
                                             recv_sem=recv.at[t], device_id=(x, y, 1 - c), device_id_type=MESH)
                for t in range(n)]

    def start(*refs):
        for cp in copies(*refs):
            cp.start()

    def finish(*refs):
        for cp in copies(*refs):
            cp.wait()

    return _Exchange(halves, [jax.ShapeDtypeStruct(a.shape, a.dtype) for a in halves], n, start, finish)


N_DEV = 8


def _all_exchange(v):
    r, cols = v.shape

    def body(in_ref, out_ref, send_sems, recv_sems, local_sem):
        x, y, c = _me()
        me = 4 * x + 2 * y + c
        local = pltpu.make_async_copy(in_ref, out_ref.at[me], local_sem)
        local.start()
        flip = lambda v, f: 1 - v if f else v
        peers = [(flip(x, fx), flip(y, fy), flip(c, fc)) for fx in (0, 1) for fy in (0, 1) for fc in (0, 1)][1:]
        cps = [pltpu.make_async_remote_copy(src_ref=in_ref, dst_ref=out_ref.at[me], send_sem=send_sems.at[j],
                                            recv_sem=recv_sems.at[j], device_id=peer, device_id_type=MESH)
               for j, peer in enumerate(peers)]
        for cp in cps:
            cp.start()
        for j, (px, py, pc) in enumerate(peers):
            pltpu.make_async_remote_copy(src_ref=in_ref, dst_ref=out_ref.at[4 * px + 2 * py + pc], send_sem=send_sems.at[j],
                                         recv_sem=recv_sems.at[j], device_id=(px, py, pc), device_id_type=MESH).wait_recv()
        for cp in cps:
            cp.wait_send()
        local.wait()

    return pl.pallas_call(
        body, name="all_exchange", in_specs=[ANY], out_specs=ANY, out_shape=jax.ShapeDtypeStruct((N_DEV, r, cols), v.dtype),
        scratch_shapes=[pltpu.SemaphoreType.DMA((7,)), pltpu.SemaphoreType.DMA((7,)), pltpu.SemaphoreType.DMA(())])(v)


def _sum_slots(a, name):
    n, r, cols = a.shape
    tr = _pick(r, 512) if r % 8 == 0 else r
    for cand in (512, 256, 128, 64, 32, 16, 8):
        if r % cand == 0:
            tr = cand
            break

    def body(a_ref, o_ref):
        acc = a_ref[0]
        for k in range(1, n):
            acc = acc + a_ref[k]
        o_ref[...] = acc

    return pl.pallas_call(
        body, name=name, grid=(r // tr,), in_specs=[pl.BlockSpec((n, tr, cols), lambda i: (0, i, 0))],
        out_specs=pl.BlockSpec((tr, cols), lambda i: (i, 0)), out_shape=jax.ShapeDtypeStruct((r, cols), F32),
        compiler_params=_params("parallel"))(a)


def _join(name, stacked):
    ax = SHARDED[name][1]
    return jnp.concatenate([stacked[k] for k in range(N_CHIPS)], axis=ax)


def _split(name, full):
    ax = SHARDED[name][1]
    return jnp.stack(jnp.split(full, N_CHIPS, axis=ax))


def _heads_pad(a, real, axis):
    shp = a.shape
    a = a.reshape(shp[:axis] + (MLA_HEADS, real) + shp[axis + 1:])
    pad = [(0, 0)] * a.ndim
    pad[axis + 1] = (0, LANES - real)
    a = jnp.pad(a, pad)
    return a.reshape(shp[:axis] + (HP,) + shp[axis + 1:])


def _heads_unpad(a, real, axis):
    shp = a.shape
    a = a.reshape(shp[:axis] + (MLA_HEADS, LANES) + shp[axis + 1:])
    a = lax.slice_in_dim(a, 0, real, axis=axis + 1)
    return a.reshape(shp[:axis] + (MLA_HEADS * real,) + shp[axis + 1:])


def _lane_place(a, start):
    n = a.shape[-1]
    pad = [(0, 0)] * (a.ndim - 1) + [(start, LANES - start - n)]
    return jnp.pad(a, pad)


_O_UV, _O_CQ, _O_CKV, _O_KR, _O_Z, _O_XBC, _O_DT, _O_G = 0, 1024, 1408, 1664, 1696, 2208, 3232, 3240


def _w_in_pad(w):
    sl = lambda a, b: w[:, a:b]
    xs = _heads_pad(sl(_O_XBC, _O_XBC + SSD_INNER), SSD_HEAD_DIM, 1)
    bc = sl(_O_XBC + SSD_INNER, _O_DT)
    main = jnp.concatenate([sl(_O_UV, _O_CQ), _heads_pad(sl(_O_Z, _O_XBC), SSD_HEAD_DIM, 1), xs, sl(_O_G, IN_COLS)], axis=1)
    tail = jnp.concatenate([bc, sl(_O_CKV, _O_KR), sl(_O_CQ, _O_CKV), _lane_place(sl(_O_KR, _O_Z), MLA_NOPE),
                            _lane_place(sl(_O_DT, _O_G), 0), jnp.zeros((w.shape[0], PW_TAIL - T_DT - LANES), w.dtype)], axis=1)
    return main, tail


def _w_in_unpad(gm, gt):
    m = lambda a, n: gm[:, a:a + n]
    t = lambda a, n: gt[:, a:a + n]
    parts = [m(C_UV, 1024), t(T_CQ, MLA_Q_RANK), t(T_CKV, MLA_KV_RANK), t(T_KR + MLA_NOPE, MLA_ROPE),
             _heads_unpad(m(C_Z, HP), SSD_HEAD_DIM, 1), _heads_unpad(m(C_XS, HP), SSD_HEAD_DIM, 1), t(T_BC, BCW),
             t(T_DT, SSD_HEADS), m(C_G, 3 * D_MODEL)]
    return jnp.concatenate(parts, axis=1)


def _xbc_pad(a):
    return jnp.concatenate([_heads_pad(a[..., :SSD_INNER], SSD_HEAD_DIM, a.ndim - 1), a[..., SSD_INNER:]], axis=-1)


def _xbc_unpad(a):
    return jnp.concatenate([_heads_unpad(a[..., :HP], SSD_HEAD_DIM, a.ndim - 1), a[..., HP:]], axis=-1)


def _rope_tables(positions):
    inv_freq = 1.0 / (ROPE_THETA ** (jnp.arange(0, MLA_ROPE, 2, dtype=F32) / MLA_ROPE))
    ang = positions.astype(F32)[:, None] * inv_freq
    cos, sin = jnp.cos(ang), jnp.sin(ang)
    s = positions.shape[0]
    half = MLA_ROPE // 2
    z = lambda n: jnp.zeros((s, n), F32)
    ct = jnp.concatenate([jnp.ones((s, MLA_NOPE), F32), cos, cos, z(LANES - MLA_QK)], axis=1)
    s1 = jnp.concatenate([z(MLA_NOPE), -sin, z(half), z(LANES - MLA_QK)], axis=1)
    s2 = jnp.concatenate([z(MLA_NOPE), z(half), sin, z(LANES - MLA_QK)], axis=1)
    return ct, s1, s2


def _layer_weights(full, small, l):
    w = {}
    for n in ('ffn1_w_in', 'ffn2_w_in'):
        w[n] = full[n]
    for n in ('ffn1_w_out', 'ffn2_w_out', 'w_out'):
        g = full[n]
        w[n] = g.reshape((N_CHIPS * g.shape[1], g.shape[2]))
    fl = {n: _join(n, full[n]) for n in ('w_in', 'mla_w_uq', 'mla_w_ukv', 'w_branch', 'ssd_conv_w')}
    w['w_in_main'], w['w_in_tail'] = _w_in_pad(fl['w_in'])
    w['wuq'] = _heads_pad(fl['mla_w_uq'], MLA_QK, 1)
    ukv = fl['mla_w_ukv'].reshape(MLA_KV_RANK, MLA_HEADS, MLA_NOPE + MLA_V)
    zero = jnp.zeros((MLA_KV_RANK, MLA_HEADS, LANES - MLA_NOPE), ukv.dtype)
    wk = jnp.concatenate([ukv[:, :, :MLA_NOPE], zero], axis=2).reshape(MLA_KV_RANK, HP)
    wv = jnp.concatenate([ukv[:, :, MLA_NOPE:], zero], axis=2).reshape(MLA_KV_RANK, HP)
    w['wkv'] = jnp.concatenate([wk, wv], axis=1)
    wb = fl['w_branch']
    w['wb0'] = wb[0]
    w['wb1'] = _heads_pad(wb[1], MLA_V, 0)
    w['wb2'] = _heads_pad(wb[2], SSD_HEAD_DIM, 0)
    w['conv_w'] = _xbc_pad(fl['ssd_conv_w'].astype(F32))
    row = lambda n: small[n][l][None, :]
    for n in ('ffn1_norm', 'mix_norm', 'gm_v_norm', 'mla_q_norm', 'mla_kv_norm', 'ffn2_norm'):
        w[n] = row(n)
    w['gm_w_s'] = small['gm_w_s'][l]
    w['gm_b_full'] = jnp.broadcast_to(small['gm_b_s'][l][:, :, None], (GM_GROUPS, CHUNK, LANES))
    w['gq'] = _lane_place(row('mla_q_gain'), 0)
    w['gk'] = _lane_place(row('mla_k_gain'), 0)
    w['conv_b'] = _xbc_pad(row('ssd_conv_b'))
    w['dt_bias'] = _lane_place(row('ssd_dt_bias'), 0)
    w['a_log'] = _lane_place(row('ssd_a_log'), 0)
    w['d_vec'] = jnp.repeat(small['ssd_d'][l], LANES)[None, :]
    w['ssd_norm'] = _heads_pad(row('ssd_norm'), SSD_HEAD_DIM, 1)
    return w


def _ffn_fwd(x, norm, w4, w_out, tag):
    h, gate, up, act = _ffn_in(x, norm, w4, f"{tag}_in")
    y = _ffn_out(act, w_out, x, f"{tag}_out")
    return y, (x, h, gate, up, act)


def _ffn_bwd(dy, saved, norm, w4, w_out, tag):
    x, h, gate, up, act = saved
    dw_out = _ffn_dwout(act, dy, f"{tag}_dwout")
    da = _ffn_dact(dy, w_out, gate, up, f"{tag}_dact")
    dw_in = _ffn_dwin(h, da, f"{tag}_dwin")
    dx, dnorm = _ffn_dx(da, w4, x, norm, dy, f"{tag}_dx")
    return dx, dnorm, dw_in, dw_out.reshape((N_CHIPS, 2 * FC // N_CHIPS, D_MODEL))


def _mixer_fwd(x, w, tabs, tag, side=None):
    h = _rmsnorm_fwd(x, w['mix_norm'], f"{tag}_norm")
    pm = _matmul(h, w['w_in_main'], out_dtype=BF16, name=f"{tag}_proj_main")
    pt = _matmul(h, w['w_in_tail'], name=f"{tag}_proj_tail")
    ya = _gmlp_fwd(pm, w['gm_v_norm'], w['gm_w_s'], w['gm_b_full'], f"{tag}_gmlp")
    q, k, v = _mla_pre_fwd(pt, tabs, w['mla_q_norm'], w['mla_kv_norm'], w['wuq'], w['wkv'], w['gq'], w['gk'], f"{tag}_mla_pre")
    (o, lse), carried = _attn_fwd(q, k, v, f"{tag}_attn", side)
    xs = _conv_fwd(pm, C_XS, HP, w['conv_w'][:, :HP], w['conv_b'][:, :HP], f"{tag}_conv_x")
    bc = _conv_fwd(pt, T_BC, BCW, w['conv_w'][:, HP:], w['conv_b'][:, HP:], f"{tag}_conv_bc")
    dtb, dab = _dt_fwd(pt, w['dt_bias'], w['a_log'], f"{tag}_dt")
    ys, s_in = _scan_fwd(xs, bc, dtb, dab, f"{tag}_scan")
    yc = _ssd_post_fwd(ys, xs, pm, w['d_vec'], w['ssd_norm'], f"{tag}_ssd_post")
    mg, y = _merge_fwd(pm, ya, o, yc, w['wb0'], w['wb1'], w['wb2'], w['w_out'], x, f"{tag}_merge")
    return y, (x, h, pm, pt, ya, q, k, v, o, lse, xs, bc, dtb, dab, ys, s_in, yc, mg), carried


def _reduce_to_chip(pending, run_first, run_second):
    got = run_first(_pair_exchange(pending))
    sums = [_pair_add(g, r, f"pair_add_{n}") for g, r, n in zip(pending, got, REDUCED)]
    arrived = run_second(_chip_exchange([s16 for s16, _ in sums]))
    return [_chip_add(own, a, f"chip_add_{n}") for (_, own), a, n in zip(sums, arrived, REDUCED)]


def _mixer_bwd(dy, saved, w, tabs, tag, pending=None):
    x, h, pm, pt, ya, q, k, v, o, lse, xs, bc, dtb, dab, ys, s_in, yc, mg = saved
    g = {}
    g['w_out'] = _matmul(mg, dy, ta=True, name=f"{tag}_dwout").reshape((N_CHIPS, D_MODEL // N_CHIPS, D_MODEL))
    d0, d1, d2, dgates, dya, do, dyc = _merge_bwd(pm, ya, o, yc, w['wb0'], w['wb1'], w['wb2'], w['w_out'], dy, f"{tag}_dmerge")
    dwb0 = _matmul(ya, d0, ta=True, name=f"{tag}_dwb0")
    dwb1 = _matmul(o, d1, ta=True, name=f"{tag}_dwb1")
    dwb2 = _matmul(yc, d2, ta=True, name=f"{tag}_dwb2")
    g['w_branch'] = _split('w_branch', jnp.stack([dwb0, _heads_unpad(dwb1, MLA_V, 0), _heads_unpad(dwb2, SSD_HEAD_DIM, 0)]))
    duv, g['gm_v_norm'], g['gm_w_s'], db = _gmlp_bwd(pm, w['gm_v_norm'], w['gm_w_s'], w['gm_b_full'], dya, f"{tag}_dgmlp")
    g['gm_b_s'] = db.T
    res = {}

    def with_dq(side):
        (res['dq'], res['delta']), carried = _attn_bwd_dq(q, k, v, o, lse, do, f"{tag}_dattn_q", side)
        return carried

    def with_dkv(side):
        (res['dk'], res['dv']), carried = _attn_bwd_dkv(q, k, v, lse, res['delta'], do, f"{tag}_dattn_kv", side)
        return carried

    if pending is None:
        reduced = None
        with_dq(None)
        with_dkv(None)
    else:
        reduced = _reduce_to_chip(pending, with_dq, with_dkv)
    dq, dk, dv = res['dq'], res['dk'], res['dv']
    dcq, dckv, dkr, dwuq, dwkv, g['mla_q_norm'], g['mla_kv_norm'], dgq, dgk = _mla_pre_bwd(
        pt, tabs, w['mla_q_norm'], w['mla_kv_norm'], w['wuq'], w['wkv'], w['gq'], w['gk'], dq, dk, dv, f"{tag}_dmla_pre")
    g['mla_w_uq'] = _split('mla_w_uq', _heads_unpad(dwuq, MLA_QK, 1))
    dwk = dwkv[:, :HP].reshape(MLA_KV_RANK, MLA_HEADS, LANES)[:, :, :MLA_NOPE]
    dwv = dwkv[:, HP:].reshape(MLA_KV_RANK, MLA_HEADS, LANES)[:, :, :MLA_V]
    g['mla_w_ukv'] = _split('mla_w_ukv', jnp.concatenate([dwk, dwv], axis=2).reshape(MLA_KV_RANK, MLA_HEADS * (MLA_NOPE + MLA_V)))
    g['mla_q_gain'], g['mla_k_gain'] = dgq[:, :MLA_QK], dgk[:, :MLA_QK]
    dys, dz, dssd_norm, dd = _ssd_post_bwd(ys, xs, pm, w['d_vec'], w['ssd_norm'], dyc, f"{tag}_dssd_post")
    g['ssd_norm'] = _heads_unpad(dssd_norm, SSD_HEAD_DIM, 1)
    g['ssd_d'] = jnp.sum(dd.reshape(SSD_HEADS, LANES), axis=1)[None, :]
    dxs, dbm, dcm, dda, ddtx = _scan_bwd(xs, bc, dtb, dab, s_in, dys, w['d_vec'], f"{tag}_dscan")
    dxs16, dcw_x, dcb_x = _conv_bwd(pm, C_XS, HP, w['conv_w'][:, :HP], w['conv_b'][:, :HP], dxs, f"{tag}_dconv_x")
    dbc16, dcw_bc, dcb_bc = _conv_bwd(pt, T_BC, BCW, w['conv_w'][:, HP:], w['conv_b'][:, HP:],
                                      jnp.concatenate([dbm, dcm], axis=1), f"{tag}_dconv_bc")
    g['ssd_conv_w'] = _xbc_unpad(jnp.concatenate([dcw_x, dcw_bc], axis=1))
    g['ssd_conv_b'] = _xbc_unpad(jnp.concatenate([dcb_x, dcb_bc], axis=1))
    ddt, dbias, dalog = _dt_bwd(pt, w['dt_bias'], w['a_log'], dda, ddtx, f"{tag}_ddt")
    g['ssd_dt_bias'], g['ssd_a_log'] = dbias[:, :SSD_HEADS], dalog[:, :SSD_HEADS]
    s = x.shape[0]
    dpm = jnp.concatenate([duv, dz, dxs16, dgates], axis=1)
    dpt = jnp.concatenate([dbc16, dckv, dcq, dkr, ddt, jnp.zeros((s, PW_TAIL - T_DT - LANES), BF16)], axis=1)
    g['w_in'] = _split('w_in', _w_in_unpad(_matmul(h, dpm, ta=True, name=f"{tag}_dwin_main"),
                                           _matmul(h, dpt, ta=True, name=f"{tag}_dwin_tail")))
    dh = _matmul(dpt, w['w_in_tail'], tb=True, name=f"{tag}_dh_tail")
    dh = _matmul(dpm, w['w_in_main'], tb=True, res=dh, name=f"{tag}_dh_main")
    dx, g['mix_norm'] = _rmsnorm_bwd(x, w['mix_norm'], dh, dy, f"{tag}_dnorm")
    return dx, g, reduced


_CONV_ROWS = 32


def _rows_cols(a, lead):
    return a.reshape(a.shape[:lead] + (int(np.prod(a.shape[lead:-1])), a.shape[-1]))


def _shard_views(wts):
    views = []
    for n in SHARDED_ORDER:
        a = _rows_cols(wts[n].astype(BF16), 1)
        if n == 'ssd_conv_w':
            a = jnp.pad(a, ((0, 0), (0, _CONV_ROWS - a.shape[1]), (0, 0)))
        views.append(a)
    return views


def _gathered_layer(arrays):
    out = {}
    for n, a in zip(SHARDED_ORDER, arrays):
        shp = _shard_shape(n)
        if n == 'ssd_conv_w':
            a = a[:, :shp[0]]
        out[n] = a.reshape((N_CHIPS,) + shp)
    return out


def _local_step(x, positions, target, weights, small, distributed=True):
    tabs = _rope_tables(positions)
    if distributed:
        gathered = _place_own(_run_exchange(_gather_exchange(weights, 0), "gather_first"), weights, 0)
    ws, saved = [], []
    for l in range(DEPTH):
        w = _layer_weights(_gathered_layer(gathered) if distributed else weights[l], small, l)
        ws.append(w)
        side = _gather_exchange(weights, l + 1) if distributed and l + 1 < DEPTH else None
        x, s1 = _ffn_fwd(x, w['ffn1_norm'], w['ffn1_w_in'], w['ffn1_w_out'], "ffn1")
        x, s2, carried = _mixer_fwd(x, w, tabs, "mix", side)
        x, s3 = _ffn_fwd(x, w['ffn2_norm'], w['ffn2_w_in'], w['ffn2_w_out'], "ffn2")
        saved.append((s1, s2, s3))
        if side is not None:
            gathered = _place_own(carried, weights, l + 1)
    dy, sq = _loss_head(x, target, "loss_head")
    loss = 0.5 * jnp.sum(sq) / D_MODEL
    grads, reduced, pending = [None] * DEPTH, [None] * DEPTH, None
    for l in reversed(range(DEPTH)):
        w = ws[l]
        s1, s2, s3 = saved[l]
        dy, dn2, dwi2, dwo2 = _ffn_bwd(dy, s3, w['ffn2_norm'], w['ffn2_w_in'], w['ffn2_w_out'], "ffn2")
        dy, g, red = _mixer_bwd(dy, s2, w, tabs, "mix", pending)
        if pending is not None:
            reduced[l + 1] = red
        dy, dn1, dwi1, dwo1 = _ffn_bwd(dy, s1, w['ffn1_norm'], w['ffn1_w_in'], w['ffn1_w_out'], "ffn1")
        g.update(ffn1_norm=dn1, ffn1_w_in=dwi1, ffn1_w_out=dwo1, ffn2_norm=dn2, ffn2_w_in=dwi2, ffn2_w_out=dwo2)
        grads[l] = g
        if distributed:
            pending = [_rows_cols(g[n], 1) for n in REDUCED]
    if distributed:
        reduced[0] = _reduce_to_chip(pending, lambda ex: _run_exchange(ex, "pair_exchange"),
                                     lambda ex: _run_exchange(ex, "chip_exchange"))
    return loss, dy, grads, reduced


SMALL_PACK = SMALL_ORDER + ['ssd_conv_w']


def _pack_small(per_layer_rows, tail=None):
    parts = [per_layer_rows[l][n].reshape(-1).astype(F32) for l in range(DEPTH) for n in SMALL_PACK]
    if tail is not None:
        parts.append(tail.reshape(1))
    flat = jnp.concatenate(parts)
    rows = -(-flat.shape[0] // LANES)
    rows = -(-rows // 8) * 8
    return jnp.pad(flat, (0, rows * LANES - flat.shape[0])).reshape(rows, LANES)


def _unpack_small(buf, shapes):
    flat = buf.reshape(-1)
    off = 0
    out = {n: [] for n in SMALL_PACK}
    for l in range(DEPTH):
        for n in SMALL_PACK:
            size = int(np.prod(shapes[n]))
            out[n].append(flat[off:off + size].reshape(shapes[n]))
            off += size
    return {n: jnp.stack(v) for n, v in out.items()}


def kernel(x, positions, ffn1_norm, ffn1_w_in, ffn1_w_out, mix_norm, w_in, gm_v_norm, gm_w_s, gm_b_s, mla_q_norm, mla_kv_norm, mla_w_uq, mla_w_ukv, mla_q_gain, mla_k_gain, ssd_conv_w, ssd_conv_b, ssd_dt_bias, ssd_a_log, ssd_d, ssd_norm, w_branch, w_out, ffn2_norm, ffn2_w_in, ffn2_w_out, loss_target, m_ffn1_norm, m_ffn1_w_in, m_ffn1_w_out, m_mix_norm, m_w_in, m_gm_v_norm, m_gm_w_s, m_gm_b_s, m_mla_q_norm, m_mla_kv_norm, m_mla_w_uq, m_mla_w_ukv, m_mla_q_gain, m_mla_k_gain, m_ssd_conv_w, m_ssd_conv_b, m_ssd_dt_bias, m_ssd_a_log, m_ssd_d, m_ssd_norm, m_w_branch, m_w_out, m_ffn2_norm, m_ffn2_w_in, m_ffn2_w_out, v_ffn1_norm, v_ffn1_w_in, v_ffn1_w_out, v_mix_norm, v_w_in, v_gm_v_norm, v_gm_w_s, v_gm_b_s, v_mla_q_norm, v_mla_kv_norm, v_mla_w_uq, v_mla_w_ukv, v_mla_q_gain, v_mla_k_gain, v_ssd_conv_w, v_ssd_conv_b, v_ssd_dt_bias, v_ssd_a_log, v_ssd_d, v_ssd_norm, v_w_branch, v_w_out, v_ffn2_norm, v_ffn2_w_in, v_ffn2_w_out):
    wts = dict(zip(WEIGHTS, (ffn1_norm, ffn1_w_in, ffn1_w_out, mix_norm, w_in, gm_v_norm, gm_w_s, gm_b_s, mla_q_norm, mla_kv_norm,
                             mla_w_uq, mla_w_ukv, mla_q_gain, mla_k_gain, ssd_conv_w, ssd_conv_b, ssd_dt_bias, ssd_a_log, ssd_d,
                             ssd_norm, w_branch, w_out, ffn2_norm, ffn2_w_in, ffn2_w_out)))
    mom = dict(zip(WEIGHTS, (m_ffn1_norm, m_ffn1_w_in, m_ffn1_w_out, m_mix_norm, m_w_in, m_gm_v_norm, m_gm_w_s, m_gm_b_s, m_mla_q_norm,
                             m_mla_kv_norm, m_mla_w_uq, m_mla_w_ukv, m_mla_q_gain, m_mla_k_gain, m_ssd_conv_w, m_ssd_conv_b,
                             m_ssd_dt_bias, m_ssd_a_log, m_ssd_d, m_ssd_norm, m_w_branch, m_w_out, m_ffn2_norm, m_ffn2_w_in,
                             m_ffn2_w_out)))
    var = dict(zip(WEIGHTS, (v_ffn1_norm, v_ffn1_w_in, v_ffn1_w_out, v_mix_norm, v_w_in, v_gm_v_norm, v_gm_w_s, v_gm_b_s, v_mla_q_norm,
                             v_mla_kv_norm, v_mla_w_uq, v_mla_w_ukv, v_mla_q_gain, v_mla_k_gain, v_ssd_conv_w, v_ssd_conv_b,
                             v_ssd_dt_bias, v_ssd_a_log, v_ssd_d, v_ssd_norm, v_w_branch, v_w_out, v_ffn2_norm, v_ffn2_w_in,
                             v_ffn2_w_out)))
    cx, cy, _ = _me()
    mychip = _chip_index(cx, cy)

    small = {n: wts[n] for n in SMALL_ORDER}
    loss_part, dx, grads, reduced = _local_step(x[0], positions[0], loss_target[0], _shard_views(wts), small)
    rows_cols = _rows_cols
    halves = [jnp.stack([reduced[l][t] for l in range(DEPTH)]) for t in range(len(REDUCED))]
    theirs = _run_exchange(_pair_share(halves), "pair_share")
    shapes = {n: wts[n].shape[1:] for n in SMALL_ORDER}
    shapes['ssd_conv_w'] = SHARDED['ssd_conv_w'][0]
    summed = _sum_slots(_all_exchange(_pack_small(grads, tail=loss_part)), "small_sum")
    small_g = _unpack_small(summed, shapes)
    loss = summed.reshape(-1)[DEPTH * sum(int(np.prod(shapes[n])) for n in SMALL_PACK)]
    conv_full = small_g.pop('ssd_conv_w')
    shard_cols = _shard_shape('ssd_conv_w')[1]
    small_g['ssd_conv_w'] = lax.dynamic_slice_in_dim(conv_full, mychip * shard_cols, shard_cols, axis=2)
    shapes['ssd_conv_w'] = _shard_shape('ssd_conv_w')

    grad, delta, new_m, new_v = {}, {}, {}, {}
    for n, a, b in zip(REDUCED, halves, theirs):
        shp = wts[n].shape
        outs = _adamw_sharded(rows_cols(wts[n], 1), rows_cols(mom[n], 1), rows_cols(var[n], 1), a, b, f"adamw_{n}")
        grad[n], delta[n], new_m[n], new_v[n] = [o.reshape(shp) for o in outs]
    per_layer = lambda t: [{n: t[n][l] for n in SMALL_PACK} for l in range(DEPTH)]
    d, nm, nv = _adamw(_pack_small(per_layer(wts)), _pack_small(per_layer(small_g)), _pack_small(per_layer(mom)),
                       _pack_small(per_layer(var)), "adamw_small")
    sd, snm, snv = _unpack_small(d, shapes), _unpack_small(nm, shapes), _unpack_small(nv, shapes)
    for n in SMALL_PACK:
        grad[n], delta[n], new_m[n], new_v[n] = small_g[n], sd[n], snm[n], snv[n]
    return (loss, dx[None], *[grad[n] for n in WEIGHTS], *[delta[n] for n in WEIGHTS], *[new_m[n] for n in WEIGHTS],
            *[new_v[n] for n in WEIGHTS])
```

```python
import functools
import math

import numpy as np
import jax
import jax.numpy as jnp
from jax import lax
from jax.experimental import pallas as pl
from jax.experimental.pallas import tpu as pltpu

F32, BF16 = jnp.float32, jnp.bfloat16
MESH = pl.DeviceIdType.MESH

D_MODEL, DEPTH, D_FF, EPS = 1024, 4, 2816, 1e-6
GM_WIDTH, GM_GROUPS, CHUNK = 512, 4, 128
MLA_HEADS, MLA_Q_RANK, MLA_KV_RANK, MLA_NOPE, MLA_ROPE, MLA_V = 8, 384, 256, 64, 32, 64
MLA_QK = MLA_NOPE + MLA_ROPE
ROPE_THETA = 10000.0
SSD_HEADS, SSD_HEAD_DIM, SSD_GROUPS, SSD_STATE, SSD_CONV = 8, 64, 2, 128, 4
SSD_INNER = SSD_HEADS * SSD_HEAD_DIM
IN_COLS = 6312
LANES = 128
ADAM_LR, ADAM_B1, ADAM_B2, ADAM_EPS, ADAM_WD, ADAM_STEP = 0.001, 0.9, 0.999, 1e-08, 0.01, 10

C_UV, C_Z, C_XS, C_G, PW_MAIN = 0, 1024, 2048, 3072, 6144
T_BC, T_CKV, T_CQ, T_KR, T_DT, PW_TAIL = 0, 512, 768, 1152, 1280, 1536
HP = MLA_HEADS * LANES
FC = 2 * D_FF // 4

WEIGHTS = ['ffn1_norm', 'ffn1_w_in', 'ffn1_w_out', 'mix_norm', 'w_in', 'gm_v_norm', 'gm_w_s', 'gm_b_s', 'mla_q_norm',
           'mla_kv_norm', 'mla_w_uq', 'mla_w_ukv', 'mla_q_gain', 'mla_k_gain', 'ssd_conv_w', 'ssd_conv_b', 'ssd_dt_bias',
           'ssd_a_log', 'ssd_d', 'ssd_norm', 'w_branch', 'w_out', 'ffn2_norm', 'ffn2_w_in', 'ffn2_w_out']
SHARDED = {'ffn1_w_in': ((1024, 5632), 1), 'ffn1_w_out': ((2816, 1024), 0), 'w_in': ((1024, 6312), 1),
           'mla_w_uq': ((384, 768), 1), 'mla_w_ukv': ((256, 1024), 1), 'ssd_conv_w': ((4, 1024), 1),
           'w_branch': ((3, 512, 1024), 2), 'w_out': ((1024, 1024), 0), 'ffn2_w_in': ((1024, 5632), 1),
           'ffn2_w_out': ((2816, 1024), 0)}
SHARDED_ORDER = [n for n in WEIGHTS if n in SHARDED]
SMALL_ORDER = [n for n in WEIGHTS if n not in SHARDED]
REDUCED = [n for n in SHARDED_ORDER if n != 'ssd_conv_w']
N_CHIPS = 4
HALF_L = DEPTH // 2


def _shard_shape(name):
    shape, ax = SHARDED[name]
    return tuple(d // N_CHIPS if i == ax else d for i, d in enumerate(shape))


def _pick(dim, target):
    if dim <= target:
        return dim
    t = (target // LANES) * LANES
    while t >= LANES:
        if dim % t == 0:
            return t
        t -= LANES
    return dim


def _sigmoid(x):
    return 1.0 / (1.0 + jnp.exp(-x))


def _params(*sem):
    return pltpu.CompilerParams(dimension_semantics=sem, vmem_limit_bytes=56 * 1024 * 1024)


def _matmul(a, b, *, ta=False, tb=False, out_dtype=F32, scale=1.0, res=None, name):
    if ta:
        k_dim, m_dim = a.shape
    else:
        m_dim, k_dim = a.shape
    if tb:
        n_dim, k2 = b.shape
    else:
        k2, n_dim = b.shape
    assert k_dim == k2, (a.shape, b.shape, ta, tb)
    tm, tn, tk = _pick(m_dim, 1024), _pick(n_dim, 1024), _pick(k_dim, 1024)
    nk = k_dim // tk
    dn = (((0 if ta else 1,), (1 if tb else 0,)), ((), ()))

    def body(*refs):
        if res is not None:
            a_ref, b_ref, r_ref, o_ref, acc = refs
        else:
            a_ref, b_ref, o_ref, acc = refs
        k = pl.program_id(2)

        @pl.when(k == 0)
        def _():
            acc[...] = jnp.zeros_like(acc)

        acc[...] += lax.dot_general(a_ref[...].astype(BF16), b_ref[...].astype(BF16), dn, preferred_element_type=F32)

        @pl.when(k == nk - 1)
        def _():
            r = acc[...]
            if scale != 1.0:
                r = r * scale
            if res is not None:
                r = r + r_ref[...]
            o_ref[...] = r.astype(out_dtype)

    a_spec = pl.BlockSpec((tk, tm), lambda j, i, k: (k, i)) if ta else pl.BlockSpec((tm, tk), lambda j, i, k: (i, k))
    b_spec = pl.BlockSpec((tn, tk), lambda j, i, k: (j, k)) if tb else pl.BlockSpec((tk, tn), lambda j, i, k: (k, j))
    in_specs = [a_spec, b_spec]
    args = [a, b]
    if res is not None:
        in_specs.append(pl.BlockSpec((tm, tn), lambda j, i, k: (i, j)))
        args.append(res)
    return pl.pallas_call(
        body, name=name, grid=(n_dim // tn, m_dim // tm, nk), in_specs=in_specs,
        out_specs=pl.BlockSpec((tm, tn), lambda j, i, k: (i, j)),
        out_shape=jax.ShapeDtypeStruct((m_dim, n_dim), out_dtype),
        scratch_shapes=[pltpu.VMEM((tm, tn), F32)],
        compiler_params=_params("parallel", "parallel", "arbitrary"))(*args)


def _rmsnorm_fwd(x, gain, name):
    s, d = x.shape
    tm = _pick(s, 512)

    def body(x_ref, g_ref, o_ref):
        xv = x_ref[...]
        r = lax.rsqrt(jnp.mean(xv * xv, axis=-1, keepdims=True) + EPS)
        o_ref[...] = (xv * r * g_ref[...]).astype(BF16)

    return pl.pallas_call(
        body, name=name, grid=(s // tm,),
        in_specs=[pl.BlockSpec((tm, d), lambda i: (i, 0)), pl.BlockSpec((1, d), lambda i: (0, 0))],
        out_specs=pl.BlockSpec((tm, d), lambda i: (i, 0)),
        out_shape=jax.ShapeDtypeStruct((s, d), BF16), compiler_params=_params("parallel"))(x, gain)


def _rmsnorm_bwd(x, gain, dh, dres, name):
    s, d = x.shape
    tm = _pick(s, 512)

    def body(x_ref, g_ref, dh_ref, dr_ref, dx_ref, dg_ref):
        @pl.when(pl.program_id(0) == 0)
        def _():
            dg_ref[...] = jnp.zeros_like(dg_ref)

        xv, dhv = x_ref[...], dh_ref[...]
        r = lax.rsqrt(jnp.mean(xv * xv, axis=-1, keepdims=True) + EPS)
        u = dhv * g_ref[...]
        dx_ref[...] = dr_ref[...] + r * u - xv * (r * r * r) * jnp.mean(xv * u, axis=-1, keepdims=True)
        dg_ref[...] += jnp.sum(dhv * xv * r, axis=0, keepdims=True)

    row = pl.BlockSpec((tm, d), lambda i: (i, 0))
    vec = pl.BlockSpec((1, d), lambda i: (0, 0))
    return pl.pallas_call(
        body, name=name, grid=(s // tm,), in_specs=[row, vec, row, row], out_specs=[row, vec],
        out_shape=[jax.ShapeDtypeStruct((s, d), F32), jax.ShapeDtypeStruct((1, d), F32)],
        compiler_params=_params("arbitrary"))(x, gain, dh, dres)


_NT = (((1,), (1,)), ((), ()))
_TN = (((0,), (0,)), ((), ()))


def _resident(shape):
    return pl.BlockSpec(shape, lambda *_: tuple(0 for _ in shape), pipeline_mode=pl.Buffered(1))


def _ffn_in(x, gain, w4, name):
    s, d = x.shape
    tm = _pick(s, 512)

    def body(x_ref, g_ref, w_ref, h_ref, gate_ref, up_ref, act_ref):
        xv = x_ref[...]
        r = lax.rsqrt(jnp.mean(xv * xv, axis=-1, keepdims=True) + EPS)
        h = (xv * r * g_ref[...]).astype(BF16)
        h_ref[...] = h
        for j in range(2):
            g16 = jnp.dot(h, w_ref[j], preferred_element_type=F32).astype(BF16)
            u16 = jnp.dot(h, w_ref[j + 2], preferred_element_type=F32).astype(BF16)
            gate_ref[j] = g16
            up_ref[j] = u16
            gf, uf = g16.astype(F32), u16.astype(F32)
            act_ref[j] = (gf * _sigmoid(gf) * uf).astype(BF16)

    half = pl.BlockSpec((2, tm, FC), lambda i: (0, i, 0))
    return pl.pallas_call(
        body, name=name, grid=(s // tm,),
        in_specs=[pl.BlockSpec((tm, d), lambda i: (i, 0)), pl.BlockSpec((1, d), lambda i: (0, 0)), _resident((4, d, FC))],
        out_specs=[pl.BlockSpec((tm, d), lambda i: (i, 0)), half, half, half],
        out_shape=[jax.ShapeDtypeStruct((s, d), BF16)] + [jax.ShapeDtypeStruct((2, s, FC), BF16)] * 3,
        compiler_params=_params("parallel"))(x, gain, w4)


def _ffn_out(act, w_out, x, name):
    s, d = x.shape
    tm = _pick(s, 512)

    def body(a_ref, w_ref, x_ref, o_ref):
        acc = jnp.dot(a_ref[0], w_ref[0:FC, :], preferred_element_type=F32)
        acc = acc + jnp.dot(a_ref[1], w_ref[FC:2 * FC, :], preferred_element_type=F32)
        o_ref[...] = x_ref[...] + 0.5 * acc

    row = pl.BlockSpec((tm, d), lambda i: (i, 0))
    return pl.pallas_call(
        body, name=name, grid=(s // tm,),
        in_specs=[pl.BlockSpec((2, tm, FC), lambda i: (0, i, 0)), _resident((2 * FC, d)), row], out_specs=row,
        out_shape=jax.ShapeDtypeStruct((s, d), F32), compiler_params=_params("parallel"))(act, w_out, x)


def _ffn_dact(dy, w_out, gate, up, name):
    s, d = dy.shape
    tm = _pick(s, 512)

    def body(dy_ref, w_ref, g_ref, u_ref, o_ref):
        dy16 = dy_ref[...].astype(BF16)
        for j in range(2):
            dact = 0.5 * lax.dot_general(dy16, w_ref[j * FC:(j + 1) * FC, :], _NT, preferred_element_type=F32)
            g, u = g_ref[j].astype(F32), u_ref[j].astype(F32)
            sg = _sigmoid(g)
            o_ref[j] = (dact * u * (sg * (1.0 + g * (1.0 - sg)))).astype(BF16)
            o_ref[j + 2] = (dact * g * sg).astype(BF16)

    half = pl.BlockSpec((2, tm, FC), lambda i: (0, i, 0))
    return pl.pallas_call(
        body, name=name, grid=(s // tm,),
        in_specs=[pl.BlockSpec((tm, d), lambda i: (i, 0)), _resident((2 * FC, d)), half, half],
        out_specs=pl.BlockSpec((4, tm, FC), lambda i: (0, i, 0)),
        out_shape=jax.ShapeDtypeStruct((4, s, FC), BF16), compiler_params=_params("parallel"))(dy, w_out, gate, up)


def _ffn_dwout(act, dy, name):
    s, d = dy.shape
    tk = _pick(s, 1024)
    nk = s // tk

    def body(a_ref, dy_ref, o_ref):
        k = pl.program_id(1)

        @pl.when(k == 0)
        def _():
            o_ref[...] = jnp.zeros_like(o_ref)

        o_ref[...] += lax.dot_general(a_ref[...], dy_ref[...].astype(BF16), _TN, preferred_element_type=F32)

        @pl.when(k == nk - 1)
        def _():
            o_ref[...] = 0.5 * o_ref[...]

    return pl.pallas_call(
        body, name=name, grid=(2, nk),
        in_specs=[pl.BlockSpec((None, tk, FC), lambda j, k: (j, k, 0)), pl.BlockSpec((tk, d), lambda j, k: (k, 0))],
        out_specs=pl.BlockSpec((FC, d), lambda j, k: (j, 0)), out_shape=jax.ShapeDtypeStruct((2 * FC, d), F32),
        compiler_params=_params("parallel", "arbitrary"))(act, dy)


def _ffn_dwin(h, da, name):
    s, d = h.shape
    tk = _pick(s, 1024)

    def body(h_ref, da_ref, o_ref):
        @pl.when(pl.program_id(1) == 0)
        def _():
            o_ref[...] = jnp.zeros_like(o_ref)

        o_ref[...] += lax.dot_general(h_ref[...], da_ref[...], _TN, preferred_element_type=F32)

    return pl.pallas_call(
        body, name=name, grid=(4, s // tk),
        in_specs=[pl.BlockSpec((tk, d), lambda j, k: (k, 0)), pl.BlockSpec((None, tk, FC), lambda j, k: (j, k, 0))],
        out_specs=pl.BlockSpec((None, d, FC), lambda j, k: (j, 0, 0)), out_shape=jax.ShapeDtypeStruct((4, d, FC), F32),
        compiler_params=_params("parallel", "arbitrary"))(h, da)


def _ffn_dx(da, w4, x, gain, dy, name):
    s, d = x.shape
    tm = _pick(s, 512)

    def body(da_ref, w_ref, x_ref, g_ref, dy_ref, dx_ref, dg_ref):
        @pl.when(pl.program_id(0) == 0)
        def _():
            dg_ref[...] = jnp.zeros_like(dg_ref)

        dh = jnp.zeros((tm, d), F32)
        for j in range(4):
            dh = dh + lax.dot_general(da_ref[j], w_ref[j], _NT, preferred_element_type=F32)
        xv = x_ref[...]
        r = lax.rsqrt(jnp.mean(xv * xv, axis=-1, keepdims=True) + EPS)
        u = dh * g_ref[...]
        dx_ref[...] = dy_ref[...] + r * u - xv * (r * r * r) * jnp.mean(xv * u, axis=-1, keepdims=True)
        dg_ref[...] += jnp.sum(dh * xv * r, axis=0, keepdims=True)

    row = pl.BlockSpec((tm, d), lambda i: (i, 0))
    vec = pl.BlockSpec((1, d), lambda i: (0, 0))
    return pl.pallas_call(
        body, name=name, grid=(s // tm,),
        in_specs=[pl.BlockSpec((4, tm, FC), lambda i: (0, i, 0)), _resident((4, d, FC)), row, vec, row],
        out_specs=[row, vec], out_shape=[jax.ShapeDtypeStruct((s, d), F32), jax.ShapeDtypeStruct((1, d), F32)],
        compiler_params=_params("arbitrary"))(da, w4, x, gain, dy)


_INV_SQRT2 = 0.7071067811865476
_INV_SQRT2PI = 0.3989422804014327


def _gelu(x):
    return 0.5 * x * (1.0 + lax.erf(x * _INV_SQRT2))


def _gelu_grad(x):
    return 0.5 * (1.0 + lax.erf(x * _INV_SQRT2)) + x * jnp.exp(-0.5 * x * x) * _INV_SQRT2PI


def _tril_mask():
    r = lax.broadcasted_iota(jnp.int32, (CHUNK, CHUNK), 0)
    c = lax.broadcasted_iota(jnp.int32, (CHUNK, CHUNK), 1)
    return r >= c


def _gmlp_fwd(p, v_gain, w_s, b_full, name):
    s = p.shape[0]
    tm = _pick(s, 512)
    nch = tm // CHUNK

    def body(uv_ref, g_ref, w_ref, b_ref, o_ref):
        gel = _gelu(uv_ref[...].astype(F32))
        u, v = gel[:, :GM_WIDTH], gel[:, GM_WIDTH:]
        r = lax.rsqrt(jnp.mean(v * v, axis=-1, keepdims=True) + EPS)
        vn = (v * r * g_ref[...]).astype(BF16)
        mask = _tril_mask()
        for g in range(GM_GROUPS):
            wm = jnp.where(mask, w_ref[g], 0.0).astype(BF16)
            for c in range(nch):
                rs, cs = slice(c * CHUNK, (c + 1) * CHUNK), slice(g * LANES, (g + 1) * LANES)
                sp = jnp.dot(wm, vn[rs, cs], preferred_element_type=F32) + b_ref[g]
                o_ref[rs, cs] = (u[rs, cs] * sp).astype(BF16)

    full3 = pl.BlockSpec((GM_GROUPS, CHUNK, CHUNK), lambda i: (0, 0, 0))
    return pl.pallas_call(
        body, name=name, grid=(s // tm,),
        in_specs=[pl.BlockSpec((tm, 2 * GM_WIDTH), lambda i: (i, C_UV // (2 * GM_WIDTH))),
                  pl.BlockSpec((1, GM_WIDTH), lambda i: (0, 0)), full3, full3],
        out_specs=pl.BlockSpec((tm, GM_WIDTH), lambda i: (i, 0)),
        out_shape=jax.ShapeDtypeStruct((s, GM_WIDTH), BF16), compiler_params=_params("parallel"))(p, v_gain, w_s, b_full)


def _gmlp_bwd(p, v_gain, w_s, b_full, dy, name):
    s = p.shape[0]
    tm = _pick(s, 512)
    nch = tm // CHUNK
    nsteps = s // tm

    def body(uv_ref, g_ref, w_ref, b_ref, dy_ref, duv_ref, dg_ref, dw_ref, db_ref, dvn_s, dbacc):
        step = pl.program_id(0)

        @pl.when(step == 0)
        def _():
            dg_ref[...] = jnp.zeros_like(dg_ref)
            dw_ref[...] = jnp.zeros_like(dw_ref)
            dbacc[...] = jnp.zeros_like(dbacc)

        uv = uv_ref[...].astype(F32)
        gel = _gelu(uv)
        u, v = gel[:, :GM_WIDTH], gel[:, GM_WIDTH:]
        r = lax.rsqrt(jnp.mean(v * v, axis=-1, keepdims=True) + EPS)
        gain = g_ref[...]
        vn32 = v * r * gain
        vn = vn32.astype(BF16)
        dy = dy_ref[...]
        mask = _tril_mask()
        for g in range(GM_GROUPS):
            wm = jnp.where(mask, w_ref[g], 0.0).astype(BF16)
            dwg = jnp.zeros((CHUNK, CHUNK), F32)
            dbg = jnp.zeros((CHUNK, LANES), F32)
            for c in range(nch):
                rs, cs = slice(c * CHUNK, (c + 1) * CHUNK), slice(g * LANES, (g + 1) * LANES)
                sp = jnp.dot(wm, vn[rs, cs], preferred_element_type=F32) + b_ref[g]
                dyc = dy[rs, cs]
                dsp = dyc * u[rs, cs]
                dsp16 = dsp.astype(BF16)
                duv_ref[rs, cs] = (dyc * sp * _gelu_grad(uv[rs, cs])).astype(BF16)
                dvn_s[rs, cs] = lax.dot_general(wm, dsp16, (((0,), (0,)), ((), ())), preferred_element_type=F32)
                dwg = dwg + lax.dot_general(dsp16, vn[rs, cs], (((1,), (1,)), ((), ())), preferred_element_type=F32)
                dbg = dbg + dsp
            dw_ref[g] += jnp.where(mask, dwg, 0.0)
            dbacc[:, g * LANES:(g + 1) * LANES] += dbg
        dvn = dvn_s[...]
        uu = dvn * gain
        dv = r * uu - v * (r * r * r) * jnp.mean(v * uu, axis=-1, keepdims=True)
        duv_ref[:, GM_WIDTH:] = (dv * _gelu_grad(uv[:, GM_WIDTH:])).astype(BF16)
        dg_ref[...] += jnp.sum(dvn * v * r, axis=0, keepdims=True)

        @pl.when(step == nsteps - 1)
        def _():
            for g in range(GM_GROUPS):
                db_ref[:, g:g + 1] = jnp.sum(dbacc[:, g * LANES:(g + 1) * LANES], axis=1, keepdims=True)

    full3 = pl.BlockSpec((GM_GROUPS, CHUNK, CHUNK), lambda i: (0, 0, 0))
    return pl.pallas_call(
        body, name=name, grid=(nsteps,),
        in_specs=[pl.BlockSpec((tm, 2 * GM_WIDTH), lambda i: (i, C_UV // (2 * GM_WIDTH))),
                  pl.BlockSpec((1, GM_WIDTH), lambda i: (0, 0)), full3, full3,
                  pl.BlockSpec((tm, GM_WIDTH), lambda i: (i, 0))],
        out_specs=[pl.BlockSpec((tm, 2 * GM_WIDTH), lambda i: (i, 0)), pl.BlockSpec((1, GM_WIDTH), lambda i: (0, 0)),
                   full3, pl.BlockSpec((CHUNK, GM_GROUPS), lambda i: (0, 0))],
        out_shape=[jax.ShapeDtypeStruct((s, 2 * GM_WIDTH), BF16), jax.ShapeDtypeStruct((1, GM_WIDTH), F32),
                   jax.ShapeDtypeStruct((GM_GROUPS, CHUNK, CHUNK), F32), jax.ShapeDtypeStruct((CHUNK, GM_GROUPS), F32)],
        scratch_shapes=[pltpu.VMEM((tm, GM_WIDTH), F32), pltpu.VMEM((CHUNK, GM_WIDTH), F32)],
        compiler_params=_params("arbitrary"))(p, v_gain, w_s, b_full, dy)


def _rope(x, ct, s1, s2):
    return x * ct + pltpu.roll(x, LANES - MLA_ROPE // 2, 1) * s1 + pltpu.roll(x, MLA_ROPE // 2, 1) * s2


def _rope_bwd(d, ct, s1, s2):
    return d * ct + pltpu.roll(d * s1, MLA_ROPE // 2, 1) + pltpu.roll(d * s2, LANES - MLA_ROPE // 2, 1)


def _head_norm(x, gain):
    r = lax.rsqrt(jnp.sum(x * x, axis=-1, keepdims=True) * (1.0 / MLA_QK) + EPS)
    return x * r * gain, r


def _head_norm_bwd(x, r, gain, d):
    u = d * gain
    return r * u - x * (r * r * r) * (jnp.sum(x * u, axis=-1, keepdims=True) * (1.0 / MLA_QK))


def _mla_specs(tm):
    cq = pl.BlockSpec((tm, MLA_Q_RANK), lambda i: (i, T_CQ // MLA_Q_RANK))
    ckv = pl.BlockSpec((tm, MLA_KV_RANK), lambda i: (i, T_CKV // MLA_KV_RANK))
    kr = pl.BlockSpec((tm, LANES), lambda i: (i, T_KR // LANES))
    tab = pl.BlockSpec((tm, LANES), lambda i: (i, 0))
    return cq, ckv, kr, tab


def _const(shape):
    return pl.BlockSpec(shape, lambda i: tuple(0 for _ in shape))


def _mla_pre_fwd(p, tabs, qn_g, kvn_g, wuq, wkv, gq, gk, name):
    s = p.shape[0]
    tm = _pick(s, 256)
    ct, s1, s2 = tabs

    def body(cq_ref, ckv_ref, kr_ref, ct_ref, s1_ref, s2_ref, qg_ref, kvg_ref, wuq_ref, wkv_ref, gq_ref, gk_ref,
             q_ref, k_ref, v_ref):
        cq, ckv, kr = cq_ref[...], ckv_ref[...], kr_ref[...]
        ctv, s1v, s2v = ct_ref[...], s1_ref[...], s2_ref[...]
        rq = lax.rsqrt(jnp.mean(cq * cq, axis=-1, keepdims=True) + EPS)
        q = jnp.dot((cq * rq * qg_ref[...]).astype(BF16), wuq_ref[...], preferred_element_type=F32)
        rk = lax.rsqrt(jnp.mean(ckv * ckv, axis=-1, keepdims=True) + EPS)
        kv = jnp.dot((ckv * rk * kvg_ref[...]).astype(BF16), wkv_ref[...], preferred_element_type=F32)
        v_ref[...] = kv[:, HP:].astype(BF16)
        for h in range(MLA_HEADS):
            hs = slice(h * LANES, (h + 1) * LANES)
            qh, _ = _head_norm(q[:, hs], gq_ref[...])
            q_ref[:, hs] = (_rope(qh, ctv, s1v, s2v) * _ATT_SCALE).astype(BF16)
            kh, _ = _head_norm(kv[:, hs] + kr, gk_ref[...])
            k_ref[:, hs] = _rope(kh, ctv, s1v, s2v).astype(BF16)

    cq_s, ckv_s, kr_s, tab_s = _mla_specs(tm)
    out = pl.BlockSpec((tm, HP), lambda i: (i, 0))
    return pl.pallas_call(
        body, name=name, grid=(s // tm,),
        in_specs=[cq_s, ckv_s, kr_s, tab_s, tab_s, tab_s, _const((1, MLA_Q_RANK)), _const((1, MLA_KV_RANK)),
                  _const((MLA_Q_RANK, HP)), _const((MLA_KV_RANK, 2 * HP)), _const((1, LANES)), _const((1, LANES))],
        out_specs=[out, out, out], out_shape=[jax.ShapeDtypeStruct((s, HP), BF16)] * 3,
        compiler_params=_params("parallel"))(p, p, p, ct, s1, s2, qn_g, kvn_g, wuq, wkv, gq, gk)


def _mla_pre_bwd(p, tabs, qn_g, kvn_g, wuq, wkv, gq, gk, dq, dk, dv, name):
    s = p.shape[0]
    tm = _pick(s, 256)
    ct, s1, s2 = tabs

    def body(cq_ref, ckv_ref, kr_ref, ct_ref, s1_ref, s2_ref, qg_ref, kvg_ref, wuq_ref, wkv_ref, gq_ref, gk_ref,
             dq_ref, dk_ref, dv_ref, dcq_ref, dckv_ref, dkr_ref, dwuq_ref, dwkv_ref, dqg_ref, dkvg_ref, dgq_ref, dgk_ref,
             dqp, dkvp):
        @pl.when(pl.program_id(0) == 0)
        def _():
            for ref in (dwuq_ref, dwkv_ref, dqg_ref, dkvg_ref, dgq_ref, dgk_ref):
                ref[...] = jnp.zeros_like(ref)

        cq, ckv, kr = cq_ref[...], ckv_ref[...], kr_ref[...]
        ctv, s1v, s2v = ct_ref[...], s1_ref[...], s2_ref[...]
        rq = lax.rsqrt(jnp.mean(cq * cq, axis=-1, keepdims=True) + EPS)
        qn = (cq * rq * qg_ref[...]).astype(BF16)
        q = jnp.dot(qn, wuq_ref[...], preferred_element_type=F32)
        rk = lax.rsqrt(jnp.mean(ckv * ckv, axis=-1, keepdims=True) + EPS)
        kvn = (ckv * rk * kvg_ref[...]).astype(BF16)
        kv = jnp.dot(kvn, wkv_ref[...], preferred_element_type=F32)
        gqv, gkv = gq_ref[...], gk_ref[...]
        dgq = jnp.zeros((1, LANES), F32)
        dgk = jnp.zeros((1, LANES), F32)
        dkr = jnp.zeros((tm, LANES), F32)
        for h in range(MLA_HEADS):
            hs = slice(h * LANES, (h + 1) * LANES)
            xq = q[:, hs]
            _, r = _head_norm(xq, gqv)
            d = _rope_bwd(dq_ref[:, hs], ctv, s1v, s2v)
            dgq = dgq + jnp.sum(d * xq * r, axis=0, keepdims=True)
            dqp[:, hs] = _head_norm_bwd(xq, r, gqv, d)
            xk = kv[:, hs] + kr
            _, r = _head_norm(xk, gkv)
            d = _rope_bwd(dk_ref[:, hs], ctv, s1v, s2v)
            dgk = dgk + jnp.sum(d * xk * r, axis=0, keepdims=True)
            dxk = _head_norm_bwd(xk, r, gkv, d)
            dkvp[:, hs] = dxk
            dkr = dkr + dxk
        dkvp[:, HP:] = dv_ref[...]
        dgq_ref[...] += dgq
        dgk_ref[...] += dgk
        dkr_ref[...] = dkr.astype(BF16)
        tn = (((0,), (0,)), ((), ()))
        nt = (((1,), (1,)), ((), ()))
        dq16 = dqp[...].astype(BF16)
        dwuq_ref[...] += lax.dot_general(qn, dq16, tn, preferred_element_type=F32)
        dqn = lax.dot_general(dq16, wuq_ref[...], nt, preferred_element_type=F32)
        dqg_ref[...] += jnp.sum(dqn * cq * rq, axis=0, keepdims=True)
        u = dqn * qg_ref[...]
        dcq_ref[...] = (rq * u - cq * (rq * rq * rq) * jnp.mean(cq * u, axis=-1, keepdims=True)).astype(BF16)
        dkv16 = dkvp[...].astype(BF16)
        dwkv_ref[...] += lax.dot_general(kvn, dkv16, tn, preferred_element_type=F32)
        dkvn = lax.dot_general(dkv16, wkv_ref[...], nt, preferred_element_type=F32)
        dkvg_ref[...] += jnp.sum(dkvn * ckv * rk, axis=0, keepdims=True)
        u = dkvn * kvg_ref[...]
        dckv_ref[...] = (rk * u - ckv * (rk * rk * rk) * jnp.mean(ckv * u, axis=-1, keepdims=True)).astype(BF16)

    cq_s, ckv_s, kr_s, tab_s = _mla_specs(tm)
    hd = pl.BlockSpec((tm, HP), lambda i: (i, 0))
    return pl.pallas_call(
        body, name=name, grid=(s // tm,),
        in_specs=[cq_s, ckv_s, kr_s, tab_s, tab_s, tab_s, _const((1, MLA_Q_RANK)), _const((1, MLA_KV_RANK)),
                  _const((MLA_Q_RANK, HP)), _const((MLA_KV_RANK, 2 * HP)), _const((1, LANES)), _const((1, LANES)),
                  hd, hd, hd],
        out_specs=[pl.BlockSpec((tm, MLA_Q_RANK), lambda i: (i, 0)), pl.BlockSpec((tm, MLA_KV_RANK), lambda i: (i, 0)),
                   pl.BlockSpec((tm, LANES), lambda i: (i, 0)), _const((MLA_Q_RANK, HP)), _const((MLA_KV_RANK, 2 * HP)),
                   _const((1, MLA_Q_RANK)), _const((1, MLA_KV_RANK)), _const((1, LANES)), _const((1, LANES))],
        out_shape=[jax.ShapeDtypeStruct((s, MLA_Q_RANK), BF16), jax.ShapeDtypeStruct((s, MLA_KV_RANK), BF16),
                   jax.ShapeDtypeStruct((s, LANES), BF16), jax.ShapeDtypeStruct((MLA_Q_RANK, HP), F32),
                   jax.ShapeDtypeStruct((MLA_KV_RANK, 2 * HP), F32), jax.ShapeDtypeStruct((1, MLA_Q_RANK), F32),
                   jax.ShapeDtypeStruct((1, MLA_KV_RANK), F32), jax.ShapeDtypeStruct((1, LANES), F32),
                   jax.ShapeDtypeStruct((1, LANES), F32)],
        scratch_shapes=[pltpu.VMEM((tm, HP), F32), pltpu.VMEM((tm, 2 * HP), F32)],
        compiler_params=_params("arbitrary"))(p, p, p, ct, s1, s2, qn_g, kvn_g, wuq, wkv, gq, gk, dq, dk, dv)


_ATT_SCALE = MLA_QK ** -0.5
ATT_BLOCK = 1024
_NEG = -1e30
_NT = (((1,), (1,)), ((), ()))
_TN = (((0,), (0,)), ((), ()))


def _tri_rows(step, n):
    i = step * 0
    for m in range(1, n):
        i = i + (step >= m * (m + 1) // 2).astype(jnp.int32)
    return i, step - i * (i + 1) // 2


def _tri_cols(step, n):
    j = step * 0
    for m in range(1, n):
        j = j + (step >= m * n - m * (m - 1) // 2).astype(jnp.int32)
    return j, j + step - (j * n - j * (j - 1) // 2)


def _diag_mask(t):
    return lax.broadcasted_iota(jnp.int32, (t, t), 0) <= lax.broadcasted_iota(jnp.int32, (t, t), 1)


def _attn_fwd(q, k, v, name, side=None):
    s = q.shape[0]
    t = _pick(s, ATT_BLOCK)
    n = s // t

    def body(q_ref, k_ref, v_ref, o_ref, lse_ref, m_s, l_s, acc):
        i, j = _tri_rows(pl.program_id(1), n)

        @pl.when(j == 0)
        def _():
            m_s[...] = jnp.full_like(m_s, _NEG)
            l_s[...] = jnp.zeros_like(l_s)
            acc[...] = jnp.zeros_like(acc)

        def step(diagonal):
            sc = lax.dot_general(k_ref[...], q_ref[...], _NT, preferred_element_type=F32)
            if diagonal:
                sc = jnp.where(_diag_mask(t), sc, _NEG)
            m_new = jnp.maximum(m_s[...], jnp.max(sc, axis=0, keepdims=True))
            alpha = jnp.exp(m_s[...] - m_new)
            pr = jnp.exp(sc - m_new)
            l_s[...] = alpha * l_s[...] + jnp.sum(pr, axis=0, keepdims=True)
            acc[...] = alpha * acc[...] + lax.dot_general(v_ref[...], pr.astype(BF16), _TN, preferred_element_type=F32)
            m_s[...] = m_new

        @pl.when(j < i)
        def _():
            step(False)

        @pl.when(j == i)
        def _():
            step(True)
            o_ref[...] = (acc[...] / l_s[...]).T
            lse_ref[...] = m_s[...] + jnp.log(l_s[...])

    qs = pl.BlockSpec((t, LANES), lambda h, p: (_tri_rows(p, n)[0], h))
    ks = pl.BlockSpec((t, LANES), lambda h, p: (_tri_rows(p, n)[1], h))
    return _call(
        body, name=name, grid=(MLA_HEADS, n * (n + 1) // 2), in_specs=[qs, ks, ks],
        out_specs=[qs, pl.BlockSpec((None, 1, t), lambda h, p: (h, 0, _tri_rows(p, n)[0]))],
        out_shape=[jax.ShapeDtypeStruct((s, HP), F32), jax.ShapeDtypeStruct((MLA_HEADS, 1, s), F32)],
        scratch_shapes=[pltpu.VMEM((1, t), F32), pltpu.VMEM((1, t), F32), pltpu.VMEM((LANES, t), F32)],
        args=(q, k, v), semantics=("parallel", "arbitrary"), side=side)


def _attn_bwd_dq(q, k, v, o, lse, do, name, side=None):
    s = q.shape[0]
    t = _pick(s, ATT_BLOCK)
    n = s // t

    def body(q_ref, k_ref, v_ref, o_ref, lse_ref, do_ref, dq_ref, dl_ref, acc, dl_s):
        i, j = _tri_rows(pl.program_id(1), n)

        @pl.when(j == 0)
        def _():
            acc[...] = jnp.zeros_like(acc)
            dl_s[...] = jnp.sum((do_ref[...] * o_ref[...]).T, axis=0, keepdims=True)

        def step(diagonal):
            sc = lax.dot_general(k_ref[...], q_ref[...], _NT, preferred_element_type=F32)
            if diagonal:
                sc = jnp.where(_diag_mask(t), sc, _NEG)
            pr = jnp.exp(sc - lse_ref[...])
            dp = lax.dot_general(v_ref[...], do_ref[...].astype(BF16), _NT, preferred_element_type=F32)
            ds = (pr * (dp - dl_s[...])).astype(BF16)
            acc[...] += lax.dot_general(k_ref[...], ds, _TN, preferred_element_type=F32)

        @pl.when(j < i)
        def _():
            step(False)

        @pl.when(j == i)
        def _():
            step(True)
            dq_ref[...] = (acc[...] * _ATT_SCALE).T
            dl_ref[...] = dl_s[...]

    qs = pl.BlockSpec((t, LANES), lambda h, p: (_tri_rows(p, n)[0], h))
    ks = pl.BlockSpec((t, LANES), lambda h, p: (_tri_rows(p, n)[1], h))
    ls = pl.BlockSpec((None, 1, t), lambda h, p: (h, 0, _tri_rows(p, n)[0]))
    return _call(
        body, name=name, grid=(MLA_HEADS, n * (n + 1) // 2), in_specs=[qs, ks, ks, qs, ls, qs], out_specs=[qs, ls],
        out_shape=[jax.ShapeDtypeStruct((s, HP), F32), jax.ShapeDtypeStruct((MLA_HEADS, 1, s), F32)],
        scratch_shapes=[pltpu.VMEM((LANES, t), F32), pltpu.VMEM((1, t), F32)],
        args=(q, k, v, o, lse, do), semantics=("parallel", "arbitrary"), side=side)


def _attn_bwd_dkv(q, k, v, lse, delta, do, name, side=None):
    s = q.shape[0]
    t = _pick(s, ATT_BLOCK)
    n = s // t

    def body(q_ref, k_ref, v_ref, lse_ref, dl_ref, do_ref, dk_ref, dv_ref, dk_acc, dv_acc):
        j, i = _tri_cols(pl.program_id(1), n)

        def step(diagonal):
            sc = lax.dot_general(k_ref[...], q_ref[...], _NT, preferred_element_type=F32)
            if diagonal:
                sc = jnp.where(_diag_mask(t), sc, _NEG)
            pr = jnp.exp(sc - lse_ref[...])
            do16 = do_ref[...].astype(BF16)
            dv_acc[...] += jnp.dot(pr.astype(BF16), do16, preferred_element_type=F32)
            dp = lax.dot_general(v_ref[...], do16, _NT, preferred_element_type=F32)
            ds = (pr * (dp - dl_ref[...])).astype(BF16)
            dk_acc[...] += jnp.dot(ds, q_ref[...], preferred_element_type=F32)

        @pl.when(i == j)
        def _():
            dk_acc[...] = jnp.zeros_like(dk_acc)
            dv_acc[...] = jnp.zeros_like(dv_acc)
            step(True)

        @pl.when(i > j)
        def _():
            step(False)

        @pl.when(i == n - 1)
        def _():
            dk_ref[...] = dk_acc[...]
            dv_ref[...] = dv_acc[...]

    qs = pl.BlockSpec((t, LANES), lambda h, p: (_tri_cols(p, n)[1], h))
    ks = pl.BlockSpec((t, LANES), lambda h, p: (_tri_cols(p, n)[0], h))
    ls = pl.BlockSpec((None, 1, t), lambda h, p: (h, 0, _tri_cols(p, n)[1]))
    return _call(
        body, name=name, grid=(MLA_HEADS, n * (n + 1) // 2), in_specs=[qs, ks, ks, ls, ls, qs], out_specs=[ks, ks],
        out_shape=[jax.ShapeDtypeStruct((s, HP), F32)] * 2,
        scratch_shapes=[pltpu.VMEM((t, LANES), F32), pltpu.VMEM((t, LANES), F32)],
        args=(q, k, v, lse, delta, do), semantics=("parallel", "arbitrary"), side=side)


XBC = HP + 2 * SSD_GROUPS * SSD_STATE
BCW = 2 * SSD_GROUPS * SSD_STATE


def _conv_fwd(p, col0, width, conv_w, conv_b, name):
    s = p.shape[0]
    c0, nblk = col0 // LANES, width // LANES

    def body(x_ref, w_ref, b_ref, o_ref, pad):
        pad[0:8, :] = jnp.zeros((8, LANES), F32)
        pad[8:s + 8, :] = x_ref[...].astype(F32)
        acc = jnp.broadcast_to(b_ref[...], (s, LANES))
        for t in range(SSD_CONV):
            acc = acc + pad[pl.ds(8 - (SSD_CONV - 1) + t, s), :] * w_ref[t:t + 1, :]
        o_ref[...] = acc * _sigmoid(acc)

    return pl.pallas_call(
        body, name=name, grid=(nblk,),
        in_specs=[pl.BlockSpec((s, LANES), lambda j: (0, c0 + j)), pl.BlockSpec((SSD_CONV, LANES), lambda j: (0, j)),
                  pl.BlockSpec((1, LANES), lambda j: (0, j))],
        out_specs=pl.BlockSpec((s, LANES), lambda j: (0, j)), out_shape=jax.ShapeDtypeStruct((s, width), F32),
        scratch_shapes=[pltpu.VMEM((s + 8, LANES), F32)], compiler_params=_params("parallel"))(p, conv_w, conv_b)


def _conv_bwd(p, col0, width, conv_w, conv_b, dact, name):
    s = p.shape[0]
    c0, nblk = col0 // LANES, width // LANES

    def body(x_ref, w_ref, b_ref, d_ref, dx_ref, dw_ref, db_ref, pad, padd):
        pad[0:8, :] = jnp.zeros((8, LANES), F32)
        pad[8:s + 8, :] = x_ref[...].astype(F32)
        acc = jnp.broadcast_to(b_ref[...], (s, LANES))
        for t in range(SSD_CONV):
            acc = acc + pad[pl.ds(8 - (SSD_CONV - 1) + t, s), :] * w_ref[t:t + 1, :]
        sg = _sigmoid(acc)
        dpre = d_ref[...] * (sg * (1.0 + acc * (1.0 - sg)))
        padd[0:s, :] = dpre
        padd[s:s + 8, :] = jnp.zeros((8, LANES), F32)
        dx = jnp.zeros((s, LANES), F32)
        for t in range(SSD_CONV):
            dx = dx + padd[pl.ds(SSD_CONV - 1 - t, s), :] * w_ref[t:t + 1, :]
            dw_ref[t:t + 1, :] = jnp.sum(dpre * pad[pl.ds(8 - (SSD_CONV - 1) + t, s), :], axis=0, keepdims=True)
        dx_ref[...] = dx.astype(BF16)
        db_ref[...] = jnp.sum(dpre, axis=0, keepdims=True)

    blk = pl.BlockSpec((s, LANES), lambda j: (0, j))
    return pl.pallas_call(
        body, name=name, grid=(nblk,),
        in_specs=[pl.BlockSpec((s, LANES), lambda j: (0, c0 + j)), pl.BlockSpec((SSD_CONV, LANES), lambda j: (0, j)),
                  pl.BlockSpec((1, LANES), lambda j: (0, j)), blk],
        out_specs=[blk, pl.BlockSpec((SSD_CONV, LANES), lambda j: (0, j)), pl.BlockSpec((1, LANES), lambda j: (0, j))],
        out_shape=[jax.ShapeDtypeStruct((s, width), BF16), jax.ShapeDtypeStruct((SSD_CONV, width), F32),
                   jax.ShapeDtypeStruct((1, width), F32)],
        scratch_shapes=[pltpu.VMEM((s + 8, LANES), F32), pltpu.VMEM((s + 8, LANES), F32)],
        compiler_params=_params("parallel"))(p, conv_w, conv_b, dact)


def _softplus(x):
    return jnp.maximum(x, 0.0) + jnp.log(1.0 + jnp.exp(-jnp.abs(x)))


def _dt_fwd(p, dt_bias, a_log, name):
    s = p.shape[0]
    tm = _pick(s, 512)

    def body(x_ref, b_ref, a_ref, dt_ref, da_ref):
        dtv = _softplus(x_ref[...] + b_ref[...])
        dav = dtv * (-jnp.exp(a_ref[...]))
        for h in range(SSD_HEADS):
            hs = slice(h * LANES, (h + 1) * LANES)
            dt_ref[:, hs] = jnp.broadcast_to(dtv[:, h:h + 1], (tm, LANES))
            da_ref[:, hs] = jnp.broadcast_to(dav[:, h:h + 1], (tm, LANES))

    out = pl.BlockSpec((tm, HP), lambda i: (i, 0))
    return pl.pallas_call(
        body, name=name, grid=(s // tm,),
        in_specs=[pl.BlockSpec((tm, LANES), lambda i: (i, T_DT // LANES)), _const((1, LANES)), _const((1, LANES))],
        out_specs=[out, out], out_shape=[jax.ShapeDtypeStruct((s, HP), F32)] * 2,
        compiler_params=_params("parallel"))(p, dt_bias, a_log)


def _dt_bwd(p, dt_bias, a_log, dda, ddtx, name):
    s = p.shape[0]
    tm = _pick(s, 512)

    def body(x_ref, b_ref, a_ref, dda_ref, ddtx_ref, dx_ref, db_ref, dal_ref):
        @pl.when(pl.program_id(0) == 0)
        def _():
            db_ref[...] = jnp.zeros_like(db_ref)
            dal_ref[...] = jnp.zeros_like(dal_ref)

        x = x_ref[...] + b_ref[...]
        dtv = _softplus(x)
        av = -jnp.exp(a_ref[...])
        lane = lax.broadcasted_iota(jnp.int32, (tm, LANES), 1)
        pa = jnp.zeros((tm, LANES), F32)
        px = jnp.zeros((tm, LANES), F32)
        for h in range(SSD_HEADS):
            pa = jnp.where(lane == h, dda_ref[:, h * LANES:(h + 1) * LANES], pa)
            px = jnp.where(lane == h, ddtx_ref[:, h * LANES:(h + 1) * LANES], px)
        draw = (pa * av + px) * _sigmoid(x)
        dx_ref[...] = draw.astype(BF16)
        db_ref[...] += jnp.sum(draw, axis=0, keepdims=True)
        dal_ref[...] += jnp.sum(pa * dtv, axis=0, keepdims=True) * av

    hd = pl.BlockSpec((tm, HP), lambda i: (i, 0))
    return pl.pallas_call(
        body, name=name, grid=(s // tm,),
        in_specs=[pl.BlockSpec((tm, LANES), lambda i: (i, T_DT // LANES)), _const((1, LANES)), _const((1, LANES)), hd, hd],
        out_specs=[pl.BlockSpec((tm, LANES), lambda i: (i, 0)), _const((1, LANES)), _const((1, LANES))],
        out_shape=[jax.ShapeDtypeStruct((s, LANES), BF16), jax.ShapeDtypeStruct((1, LANES), F32),
                   jax.ShapeDtypeStruct((1, LANES), F32)],
        compiler_params=_params("arbitrary"))(p, dt_bias, a_log, dda, ddtx)


def _cumsum_rows(x):
    row = lax.broadcasted_iota(jnp.int32, x.shape, 0)
    k = 1
    while k < x.shape[0]:
        x = x + jnp.where(row >= k, pltpu.roll(x, k, 0), 0.0)
        k *= 2
    return x


def _rev_cumsum_rows(x):
    n = x.shape[0]
    row = lax.broadcasted_iota(jnp.int32, x.shape, 0)
    k = 1
    while k < n:
        x = x + jnp.where(row < n - k, pltpu.roll(x, n - k, 0), 0.0)
        k *= 2
    return x


HPG = SSD_HEADS // SSD_GROUPS


def _chunk_decay(da):
    cs = _cumsum_rows(da)
    lm = jnp.exp(jnp.where(_tril_mask(), cs - cs.T, _NEG))
    return cs, lm, cs[CHUNK - 1:CHUNK, :]


def _scan_fwd(xs, bc, dtb, dab, name):
    s = xs.shape[0]
    nc = s // CHUNK

    def body(x_ref, b_ref, c_ref, dt_ref, da_ref, y_ref, sin_ref, state):
        @pl.when(pl.program_id(1) == 0)
        def _():
            state[...] = jnp.zeros_like(state)

        bv = b_ref[...]
        b16, c16 = bv.astype(BF16), c_ref[...].astype(BF16)
        g = lax.dot_general(c16, b16, _NT, preferred_element_type=F32)
        for hh in range(HPG):
            hs = slice(hh * LANES, (hh + 1) * LANES)
            st = state[hh]
            sin_ref[hh] = st
            cs, lm, cl = _chunk_decay(da_ref[:, hs])
            xd = (x_ref[:, hs] * dt_ref[:, hs]).astype(BF16)
            y = jnp.dot((g * lm).astype(BF16), xd, preferred_element_type=F32)
            y_ref[:, hs] = y + jnp.dot(c16, st.astype(BF16), preferred_element_type=F32) * jnp.exp(cs)
            bd = (bv * jnp.exp(cl - cs)).astype(BF16)
            state[hh] = jnp.exp(cl) * st + lax.dot_general(bd, xd, _TN, preferred_element_type=F32)

    gw = HPG * LANES
    hd = pl.BlockSpec((CHUNK, gw), lambda g, c: (c, g))
    return pl.pallas_call(
        body, name=name, grid=(SSD_GROUPS, nc),
        in_specs=[hd, pl.BlockSpec((CHUNK, LANES), lambda g, c: (c, g)),
                  pl.BlockSpec((CHUNK, LANES), lambda g, c: (c, SSD_GROUPS + g)), hd, hd],
        out_specs=[hd, pl.BlockSpec((HPG, None, SSD_STATE, LANES), lambda g, c: (g, c, 0, 0))],
        out_shape=[jax.ShapeDtypeStruct((s, HP), F32), jax.ShapeDtypeStruct((SSD_HEADS, nc, SSD_STATE, LANES), F32)],
        scratch_shapes=[pltpu.VMEM((HPG, SSD_STATE, LANES), F32)],
        compiler_params=_params("parallel", "arbitrary"))(xs, bc, bc, dtb, dab)


def _scan_bwd(xs, bc, dtb, dab, s_in, dy, d_vec, name):
    s = xs.shape[0]
    nc = s // CHUNK

    def body(x_ref, b_ref, c_ref, dt_ref, da_ref, sin_ref, dy_ref, dv_ref, dx_ref, db_ref, dc_ref, dda_ref, ddtx_ref, dstate):
        @pl.when(pl.program_id(1) == 0)
        def _():
            dstate[...] = jnp.zeros_like(dstate)

        bv = b_ref[...]
        b16, c16 = bv.astype(BF16), c_ref[...].astype(BF16)
        g = lax.dot_general(c16, b16, _NT, preferred_element_type=F32)
        row = lax.broadcasted_iota(jnp.int32, (CHUNK, 1), 0)
        dbm = jnp.zeros((CHUNK, SSD_STATE), F32)
        dcm = jnp.zeros((CHUNK, SSD_STATE), F32)
        for hh in range(HPG):
            hs = slice(hh * LANES, (hh + 1) * LANES)
            st, ds = sin_ref[hh], dstate[hh]
            st16, ds16 = st.astype(BF16), ds.astype(BF16)
            xv, dtv, dyv = x_ref[:, hs], dt_ref[:, hs], dy_ref[:, hs]
            cs, lm, cl = _chunk_decay(da_ref[:, hs])
            ecs, ecl = jnp.exp(cs), jnp.exp(cl)
            decay = jnp.exp(cl - cs)
            xd = (xv * dtv).astype(BF16)
            dy16 = dyv.astype(BF16)
            dye = (dyv * ecs).astype(BF16)
            yoff = jnp.dot(c16, st16, preferred_element_type=F32) * ecs
            dcs = jnp.sum(dyv * yoff, axis=-1, keepdims=True)
            dcm = dcm + lax.dot_general(dye, st16, _NT, preferred_element_type=F32)
            dstate[hh] = ecl * ds + lax.dot_general(c16, dye, _TN, preferred_element_type=F32)
            dcl = jnp.sum(jnp.sum(ds * st, axis=0, keepdims=True), axis=1, keepdims=True) * ecl[:, 0:1]
            bd32 = bv * decay
            qm = lax.dot_general(xd, ds16, _NT, preferred_element_type=F32)
            dbm = dbm + qm * decay
            w = jnp.sum(bd32 * qm, axis=-1, keepdims=True)
            dcs = dcs - w
            dcl = dcl + jnp.sum(w, axis=0, keepdims=True)
            dxd = jnp.dot(bd32.astype(BF16), ds16, preferred_element_type=F32)
            m16 = (g * lm).astype(BF16)
            dm = lax.dot_general(dy16, xd, _NT, preferred_element_type=F32)
            dxd = dxd + lax.dot_general(m16, dy16, _TN, preferred_element_type=F32)
            dg = dm * lm
            dg16 = dg.astype(BF16)
            tt = dg * g
            dcm = dcm + jnp.dot(dg16, b16, preferred_element_type=F32)
            dbm = dbm + lax.dot_general(dg16, c16, _TN, preferred_element_type=F32)
            dcs = dcs + jnp.sum(tt, axis=-1, keepdims=True) - jnp.sum(tt.T, axis=-1, keepdims=True)
            dcs = dcs + jnp.where(row == CHUNK - 1, dcl, 0.0)
            dda_ref[:, hs] = _rev_cumsum_rows(jnp.broadcast_to(dcs, (CHUNK, LANES)))
            ddtx_ref[:, hs] = jnp.broadcast_to(jnp.sum(dxd * xv, axis=-1, keepdims=True), (CHUNK, LANES))
            dx_ref[:, hs] = dxd * dtv + dyv * dv_ref[:, hs]
        db_ref[...] = dbm
        dc_ref[...] = dcm

    gw = HPG * LANES
    hd = pl.BlockSpec((CHUNK, gw), lambda g, c: (nc - 1 - c, g))
    gp = pl.BlockSpec((CHUNK, LANES), lambda g, c: (nc - 1 - c, g))
    return pl.pallas_call(
        body, name=name, grid=(SSD_GROUPS, nc),
        in_specs=[hd, gp, pl.BlockSpec((CHUNK, LANES), lambda g, c: (nc - 1 - c, SSD_GROUPS + g)), hd, hd,
                  pl.BlockSpec((HPG, None, SSD_STATE, LANES), lambda g, c: (g, nc - 1 - c, 0, 0)), hd,
                  pl.BlockSpec((1, gw), lambda g, c: (0, g))],
        out_specs=[hd, gp, gp, hd, hd],
        out_shape=[jax.ShapeDtypeStruct((s, HP), F32), jax.ShapeDtypeStruct((s, SSD_GROUPS * SSD_STATE), F32),
                   jax.ShapeDtypeStruct((s, SSD_GROUPS * SSD_STATE), F32), jax.ShapeDtypeStruct((s, HP), F32),
                   jax.ShapeDtypeStruct((s, HP), F32)],
        scratch_shapes=[pltpu.VMEM((HPG, SSD_STATE, LANES), F32)],
        compiler_params=_params("parallel", "arbitrary"))(xs, bc, bc, dtb, dab, s_in, dy, d_vec)


_GN = SSD_INNER // SSD_GROUPS
_GW = HP // SSD_GROUPS


def _ssd_post_fwd(y, xbc, p, d_vec, gain, name):
    s = y.shape[0]
    tm = _pick(s, 512)

    def body(y_ref, x_ref, z_ref, d_ref, g_ref, o_ref):
        z = z_ref[...].astype(F32)
        y2 = (y_ref[...] + x_ref[...] * d_ref[...]) * (z * _sigmoid(z))
        for g in range(SSD_GROUPS):
            gs = slice(g * _GW, (g + 1) * _GW)
            yg = y2[:, gs]
            r = lax.rsqrt(jnp.sum(yg * yg, axis=-1, keepdims=True) * (1.0 / _GN) + EPS)
            o_ref[:, gs] = (yg * r * g_ref[:, gs]).astype(BF16)

    hd = pl.BlockSpec((tm, HP), lambda i: (i, 0))
    return pl.pallas_call(
        body, name=name, grid=(s // tm,),
        in_specs=[hd, hd, pl.BlockSpec((tm, HP), lambda i: (i, C_Z // HP)), _const((1, HP)), _const((1, HP))],
        out_specs=hd, out_shape=jax.ShapeDtypeStruct((s, HP), BF16), compiler_params=_params("parallel"))(y, xbc, p, d_vec, gain)


def _ssd_post_bwd(y, xbc, p, d_vec, gain, dyn, name):
    s = y.shape[0]
    tm = _pick(s, 512)

    def body(y_ref, x_ref, z_ref, d_ref, g_ref, dn_ref, dy_ref, dz_ref, dg_ref, dd_ref):
        @pl.when(pl.program_id(0) == 0)
        def _():
            dg_ref[...] = jnp.zeros_like(dg_ref)
            dd_ref[...] = jnp.zeros_like(dd_ref)

        z, xv = z_ref[...].astype(F32), x_ref[...]
        sg = _sigmoid(z)
        sz = z * sg
        yt = y_ref[...] + xv * d_ref[...]
        y2 = yt * sz
        for g in range(SSD_GROUPS):
            gs = slice(g * _GW, (g + 1) * _GW)
            yg, dn = y2[:, gs], dn_ref[:, gs]
            r = lax.rsqrt(jnp.sum(yg * yg, axis=-1, keepdims=True) * (1.0 / _GN) + EPS)
            u = dn * g_ref[:, gs]
            dy2 = r * u - yg * (r * r * r) * (jnp.sum(yg * u, axis=-1, keepdims=True) * (1.0 / _GN))
            dg_ref[:, gs] += jnp.sum(dn * yg * r, axis=0, keepdims=True)
            dyt = dy2 * sz[:, gs]
            dy_ref[:, gs] = dyt
            dz_ref[:, gs] = (dy2 * yt[:, gs] * (sg[:, gs] * (1.0 + z[:, gs] * (1.0 - sg[:, gs])))).astype(BF16)
            dd_ref[:, gs] += jnp.sum(dyt * xv[:, gs], axis=0, keepdims=True)

    hd = pl.BlockSpec((tm, HP), lambda i: (i, 0))
    return pl.pallas_call(
        body, name=name, grid=(s // tm,),
        in_specs=[hd, hd, pl.BlockSpec((tm, HP), lambda i: (i, C_Z // HP)), _const((1, HP)), _const((1, HP)), hd],
        out_specs=[hd, hd, _const((1, HP)), _const((1, HP))],
        out_shape=[jax.ShapeDtypeStruct((s, HP), F32), jax.ShapeDtypeStruct((s, HP), BF16),
                   jax.ShapeDtypeStruct((1, HP), F32), jax.ShapeDtypeStruct((1, HP), F32)],
        compiler_params=_params("arbitrary"))(y, xbc, p, d_vec, gain, dyn)


def _merge_fwd(p, ya, o, yc, wb0, wb1, wb2, w_out, x, name):
    s = p.shape[0]
    tm = _pick(s, 512)

    def body(g_ref, ya_ref, o_ref, yc_ref, w0_ref, w1_ref, w2_ref, wo_ref, x_ref, mg_ref, y_ref):
        acc = jnp.zeros((tm, D_MODEL), F32)
        for i, (b_ref, w_ref) in enumerate(((ya_ref, w0_ref), (o_ref, w1_ref), (yc_ref, w2_ref))):
            t = jnp.dot(b_ref[...].astype(BF16), w_ref[...], preferred_element_type=F32)
            acc = acc + _sigmoid(g_ref[:, i * D_MODEL:(i + 1) * D_MODEL].astype(F32)) * t
        mg = acc.astype(BF16)
        mg_ref[...] = mg
        y_ref[...] = x_ref[...] + jnp.dot(mg, wo_ref[...], preferred_element_type=F32)

    row = pl.BlockSpec((tm, D_MODEL), lambda i: (i, 0))
    return pl.pallas_call(
        body, name=name, grid=(s // tm,),
        in_specs=[pl.BlockSpec((tm, 3 * D_MODEL), lambda i: (i, C_G // (3 * D_MODEL))),
                  pl.BlockSpec((tm, GM_WIDTH), lambda i: (i, 0)), row, row,
                  _resident((GM_WIDTH, D_MODEL)), _resident((HP, D_MODEL)), _resident((HP, D_MODEL)),
                  _resident((D_MODEL, D_MODEL)), row],
        out_specs=[row, row],
        out_shape=[jax.ShapeDtypeStruct((s, D_MODEL), BF16), jax.ShapeDtypeStruct((s, D_MODEL), F32)],
        compiler_params=_params("parallel"))(p, ya, o, yc, wb0, wb1, wb2, w_out, x)


def _merge_bwd(p, ya, o, yc, wb0, wb1, wb2, w_out, dy, name):
    s = p.shape[0]
    tm = _pick(s, 512)

    def body(g_ref, ya_ref, o_ref, yc_ref, w0_ref, w1_ref, w2_ref, wo_ref, dy_ref,
             d0_ref, d1_ref, d2_ref, dg_ref, dya_ref, do_ref, dyc_ref):
        dm = lax.dot_general(dy_ref[...].astype(BF16), wo_ref[...], _NT, preferred_element_type=F32)
        for i, (b_ref, w_ref, d_ref, db_ref) in enumerate(((ya_ref, w0_ref, d0_ref, dya_ref), (o_ref, w1_ref, d1_ref, do_ref),
                                                            (yc_ref, w2_ref, d2_ref, dyc_ref))):
            cs = slice(i * D_MODEL, (i + 1) * D_MODEL)
            t = jnp.dot(b_ref[...].astype(BF16), w_ref[...], preferred_element_type=F32)
            sg = _sigmoid(g_ref[:, cs].astype(F32))
            dt16 = (dm * sg).astype(BF16)
            d_ref[...] = dt16
            dg_ref[:, cs] = (dm * t * sg * (1.0 - sg)).astype(BF16)
            db_ref[...] = lax.dot_general(dt16, w_ref[...], _NT, preferred_element_type=F32)

    row = pl.BlockSpec((tm, D_MODEL), lambda i: (i, 0))
    nar = pl.BlockSpec((tm, GM_WIDTH), lambda i: (i, 0))
    wide = pl.BlockSpec((tm, 3 * D_MODEL), lambda i: (i, 0))
    return pl.pallas_call(
        body, name=name, grid=(s // tm,),
        in_specs=[pl.BlockSpec((tm, 3 * D_MODEL), lambda i: (i, C_G // (3 * D_MODEL))), nar, row, row,
                  _resident((GM_WIDTH, D_MODEL)), _resident((HP, D_MODEL)), _resident((HP, D_MODEL)),
                  _resident((D_MODEL, D_MODEL)), row],
        out_specs=[row, row, row, wide, nar, row, row],
        out_shape=[jax.ShapeDtypeStruct((s, D_MODEL), BF16)] * 3 + [jax.ShapeDtypeStruct((s, 3 * D_MODEL), BF16),
                   jax.ShapeDtypeStruct((s, GM_WIDTH), F32), jax.ShapeDtypeStruct((s, D_MODEL), F32),
                   jax.ShapeDtypeStruct((s, D_MODEL), F32)],
        compiler_params=_params("parallel"))(p, ya, o, yc, wb0, wb1, wb2, w_out, dy)


def _loss_head(y, target, name):
    s, d = y.shape
    tm = _pick(s, 512)

    def body(y_ref, t_ref, dy_ref, sq_ref):
        @pl.when(pl.program_id(0) == 0)
        def _():
            sq_ref[...] = jnp.zeros_like(sq_ref)

        e = y_ref[...] - t_ref[...]
        dy_ref[...] = e * (1.0 / d)
        sq_ref[...] += jnp.sum(e * e, axis=0, keepdims=True)

    row = pl.BlockSpec((tm, d), lambda i: (i, 0))
    return pl.pallas_call(
        body, name=name, grid=(s // tm,), in_specs=[row, row], out_specs=[row, _const((1, d))],
        out_shape=[jax.ShapeDtypeStruct((s, d), F32), jax.ShapeDtypeStruct((1, d), F32)],
        compiler_params=_params("arbitrary"))(y, target)


def _adamw(w, g, m, v, name):
    rows, cols = w.shape
    tr = rows
    for cand in (512, 256, 128, 64, 32, 16, 8):
        if rows % cand == 0 and cand * cols * 4 <= 3 * 1024 * 1024:
            tr = cand
            break

    def body(w_ref, g_ref, m_ref, v_ref, d_ref, nm_ref, nv_ref):
        d_ref[...], nm_ref[...], nv_ref[...] = _adam_update(w_ref[...], g_ref[...], m_ref[...], v_ref[...])

    blk = pl.BlockSpec((tr, cols), lambda i: (i, 0))
    return pl.pallas_call(
        body, name=name, grid=(rows // tr,), in_specs=[blk] * 4, out_specs=[blk] * 3,
        out_shape=[jax.ShapeDtypeStruct((rows, cols), F32)] * 3, compiler_params=_params("parallel"))(w, g, m, v)


def _adam_update(w, g, m, v):
    nm = ADAM_B1 * m + (1.0 - ADAM_B1) * g
    nv = ADAM_B2 * v + (1.0 - ADAM_B2) * (g * g)
    c1 = 1.0 - ADAM_B1 ** ADAM_STEP
    c2 = 1.0 - ADAM_B2 ** ADAM_STEP
    return -ADAM_LR * ((nm / c1) / (jnp.sqrt(nv / c2) + ADAM_EPS) + ADAM_WD * w), nm, nv


def _adamw_sharded(w, m, v, mine, theirs, name):
    depth, rows, cols = w.shape
    tr = _row_tile(rows // 2, cols, 1024 * 1024)
    nb = rows // 2 // tr

    def body(w_ref, m_ref, v_ref, a_ref, b_ref, g_ref, d_ref, nm_ref, nv_ref):
        c = lax.axis_index("c")
        g = jnp.where(pl.program_id(1) // nb == c, a_ref[...], b_ref[...])
        g_ref[...] = g
        d_ref[...], nm_ref[...], nv_ref[...] = _adam_update(w_ref[...], g, m_ref[...], v_ref[...])

    blk = pl.BlockSpec((None, tr, cols), lambda l, i: (l, i, 0))
    half = pl.BlockSpec((None, tr, cols), lambda l, i: (l, i % nb, 0))
    return pl.pallas_call(
        body, name=name, grid=(depth, rows // tr), in_specs=[blk, blk, blk, half, half], out_specs=[blk] * 4,
        out_shape=[jax.ShapeDtypeStruct((depth, rows, cols), F32)] * 4,
        compiler_params=_params("parallel", "parallel"))(w, m, v, mine, theirs)


ANY = pl.BlockSpec(memory_space=pl.ANY)


def _me():
    return lax.axis_index("x"), lax.axis_index("y"), lax.axis_index("c")


def _other_chips(x, y):
    return [(1 - x, y), (x, 1 - y), (1 - x, 1 - y)]


def _chip_index(cx, cy):
    return 2 * cx + cy


class _Exchange:
    def __init__(self, ins, out_shapes, n_sems, start, finish):
        self.ins, self.out_shapes, self.n_sems, self.start, self.finish = list(ins), list(out_shapes), n_sems, start, finish


def _sem_scratch(ex):
    return [pltpu.SemaphoreType.DMA((ex.n_sems,)), pltpu.SemaphoreType.DMA((ex.n_sems,))]


def _run_exchange(ex, name):
    n_in, n_out = len(ex.ins), len(ex.out_shapes)

    def body(*refs):
        in_refs, out_refs, (send, recv) = refs[:n_in], refs[n_in:n_in + n_out], refs[n_in + n_out:]
        ex.start(in_refs, out_refs, send, recv)
        ex.finish(in_refs, out_refs, send, recv)

    return pl.pallas_call(body, name=name, in_specs=[ANY] * n_in, out_specs=[ANY] * n_out, out_shape=ex.out_shapes,
                          scratch_shapes=_sem_scratch(ex))(*ex.ins)


def _call(body, *, name, grid, in_specs, out_specs, out_shape, scratch_shapes, args, semantics, side=None):
    if side is None:
        return pl.pallas_call(body, name=name, grid=grid, in_specs=in_specs, out_specs=out_specs, out_shape=out_shape,
                              scratch_shapes=scratch_shapes, compiler_params=_params(*semantics))(*args), []
    n_in, n_out, n_sc = len(in_specs), len(out_specs), len(scratch_shapes)
    s_in, s_out = len(side.ins), len(side.out_shapes)

    def hosted(*refs):
        pos = 0
        parts = []
        for size in (n_in, s_in, n_out, s_out, n_sc, 2):
            parts.append(refs[pos:pos + size])
            pos += size
        ins, sins, outs, souts, scratch, (send, recv) = parts
        ids = [pl.program_id(a) for a in range(len(grid))]
        first = functools.reduce(jnp.logical_and, [i == 0 for i in ids])
        last = functools.reduce(jnp.logical_and, [i == g - 1 for i, g in zip(ids, grid)])

        @pl.when(first)
        def _():
            side.start(sins, souts, send, recv)

        body(*ins, *outs, *scratch)

        @pl.when(last)
        def _():
            side.finish(sins, souts, send, recv)

    res = pl.pallas_call(
        hosted, name=name, grid=grid, in_specs=list(in_specs) + [ANY] * s_in, out_specs=list(out_specs) + [ANY] * s_out,
        out_shape=list(out_shape) + side.out_shapes, scratch_shapes=list(scratch_shapes) + _sem_scratch(side),
        compiler_params=_params(*["arbitrary"] * len(grid)))(*args, *side.ins)
    return res[:n_out], res[n_out:]


def _half(ref_rows, c):
    return pl.ds(c * (ref_rows // 2), ref_rows // 2)


def _gather_exchange(shards, layer):
    n = len(shards)
    rows = [a.shape[1] for a in shards]

    def copy(in_refs, out_refs, send, recv, t, k, chip, hc, to, from_input=False):
        dst = out_refs[t].at[chip, _half(rows[t], hc)]
        src = in_refs[t].at[layer, _half(rows[t], hc)] if from_input else dst
        return pltpu.make_async_remote_copy(src_ref=src, dst_ref=dst, send_sem=send.at[6 * t + k], recv_sem=recv.at[6 * t + k],
                                            device_id=to, device_id_type=MESH)

    def start(in_refs, out_refs, send, recv):
        x, y, c = _me()
        for j, chip in enumerate(_other_chips(x, y)):
            for t in range(n):
                copy(in_refs, out_refs, send, recv, t, j, _chip_index(x, y), c, (*chip, c), from_input=True).start()

    def finish(in_refs, out_refs, send, recv):
        x, y, c = _me()
        chips = _other_chips(x, y)
        passed = []
        for j, chip in enumerate(chips):
            for t in range(n):
                copy(in_refs, out_refs, send, recv, t, j, _chip_index(*chip), c, (x, y, c)).wait_recv()
                cp = copy(in_refs, out_refs, send, recv, t, 3 + j, _chip_index(*chip), c, (x, y, 1 - c))
                cp.start()
                passed.append(cp)
        for j, chip in enumerate(chips):
            for t in range(n):
                copy(in_refs, out_refs, send, recv, t, 3 + j, _chip_index(*chip), 1 - c, (x, y, c)).wait_recv()
                copy(in_refs, out_refs, send, recv, t, j, _chip_index(x, y), c, (*chip, c), from_input=True).wait_send()
        for cp in passed:
            cp.wait_send()

    return _Exchange(shards, [jax.ShapeDtypeStruct((N_CHIPS,) + a.shape[1:], a.dtype) for a in shards], 6 * n, start, finish)


def _place_own(gathered, shards, layer):
    cx, cy, _ = _me()
    return [lax.dynamic_update_index_in_dim(g, a[layer], _chip_index(cx, cy), 0) for g, a in zip(gathered, shards)]


def _pair_exchange(gs):
    n = len(gs)
    rows = [a.shape[1] for a in gs]

    def copies(in_refs, out_refs, send, recv):
        x, y, c = _me()
        return [pltpu.make_async_remote_copy(src_ref=in_refs[t].at[:, _half(rows[t], 1 - c)], dst_ref=out_refs[t],
                                             send_sem=send.at[t], recv_sem=recv.at[t], device_id=(x, y, 1 - c),
                                             device_id_type=MESH) for t in range(n)]

    def start(*refs):
        for cp in copies(*refs):
            cp.start()

    def finish(*refs):
        for cp in copies(*refs):
            cp.wait()

    return _Exchange(gs, [jax.ShapeDtypeStruct((N_CHIPS, a.shape[1] // 2, a.shape[2]), a.dtype) for a in gs], n, start, finish)


def _row_tile(rows, cols, budget=2 * 1024 * 1024):
    best = None
    for t in range(8, rows + 1, 8):
        if rows % t == 0 and t * cols * 4 <= budget:
            best = t
    return best or rows


def _pair_add(g, got, name):
    _, rows, cols = g.shape
    tr = _row_tile(rows // 2, cols)
    nb = rows // 2 // tr

    def body(lo_ref, hi_ref, r_ref, o16_ref, own_ref):
        x, y, c = _me()
        tot = jnp.where(c == 0, lo_ref[...], hi_ref[...]) + r_ref[...]
        o16_ref[...] = tot.astype(BF16)

        @pl.when(pl.program_id(1) == _chip_index(x, y))
        def _():
            own_ref[...] = tot

    blk = (None, tr, cols)
    return pl.pallas_call(
        body, name=name, grid=(nb, N_CHIPS),
        in_specs=[pl.BlockSpec(blk, lambda i, k: (k, i, 0)), pl.BlockSpec(blk, lambda i, k: (k, i + nb, 0)),
                  pl.BlockSpec(blk, lambda i, k: (k, i, 0))],
        out_specs=[pl.BlockSpec(blk, lambda i, k: (k, i, 0)), pl.BlockSpec((tr, cols), lambda i, k: (i, 0))],
        out_shape=[jax.ShapeDtypeStruct((N_CHIPS, rows // 2, cols), BF16), jax.ShapeDtypeStruct((rows // 2, cols), F32)],
        compiler_params=_params("parallel", "arbitrary"))(g, g, got)


def _chip_exchange(parts):
    n = len(parts)

    def copies(in_refs, out_refs, send, recv):
        x, y, c = _me()
        return [pltpu.make_async_remote_copy(src_ref=in_refs[t].at[_chip_index(*chip)], dst_ref=out_refs[t].at[j],
                                             send_sem=send.at[3 * t + j], recv_sem=recv.at[3 * t + j],
                                             device_id=(*chip, c), device_id_type=MESH)
                for j, chip in enumerate(_other_chips(x, y)) for t in range(n)]

    def start(*refs):
        for cp in copies(*refs):
            cp.start()

    def finish(*refs):
        for cp in copies(*refs):
            cp.wait()

    return _Exchange(parts, [jax.ShapeDtypeStruct((3,) + a.shape[1:], a.dtype) for a in parts], 3 * n, start, finish)


def _chip_add(own, got, name):
    rows, cols = own.shape
    tr = _row_tile(rows, cols, 1024 * 1024)

    def body(own_ref, got_ref, o_ref):
        acc = own_ref[...]
        for j in range(3):
            acc = acc + got_ref[j].astype(F32)
        o_ref[...] = acc

    return pl.pallas_call(
        body, name=name, grid=(rows // tr,),
        in_specs=[pl.BlockSpec((tr, cols), lambda i: (i, 0)), pl.BlockSpec((3, tr, cols), lambda i: (0, i, 0))],
        out_specs=pl.BlockSpec((tr, cols), lambda i: (i, 0)),
        out_shape=jax.ShapeDtypeStruct((rows, cols), F32), compiler_params=_params("parallel"))(own, got)


def _pair_share(halves):
    n = len(halves)

    def copies(in_refs, out_refs, send, recv):
        x, y, c = _me()
        return [pltpu.make_async_remote_copy(src_ref=in_refs[t], dst_ref=out_refs[t], send_sem=send.at[t],
                                             recv_sem=recv.at[t], device_id=(x, y, 1 - c), device_id_type=MESH)
                for t in range(n)]

    def start(*refs):
        for cp in copies(*refs):
            cp.start()

    def finish(*refs):
        for cp in copies(*refs):
            cp.wait()

    return _Exchange(halves, [jax.ShapeDtypeStruct(a.shape, a.dtype) for a in halves], n, start, finish)


N_DEV = 8


def _all_exchange(v):
    r, cols = v.shape

    def body(in_ref, out_ref, send_sems, recv_sems, local_sem):
        x, y, c = _me()
        me = 4 * x + 2 * y + c
        local = pltpu.make_async_copy(in_ref, out_ref.at[me], local_sem)
        local.start()
        flip = lambda v, f: 1 - v if f else v
        peers = [(flip(x, fx), flip(y, fy), flip(c, fc)) for fx in (0, 1) for fy in (0, 1) for fc in (0, 1)][1:]
        cps = [pltpu.make_async_remote_copy(src_ref=in_ref, dst_ref=out_ref.at[me], send_sem=send_sems.at[j],
                                            recv_sem=recv_sems.at[j], device_id=peer, device_id_type=MESH)
               for j, peer in enumerate(peers)]
        for cp in cps:
            cp.start()
        for j, (px, py, pc) in enumerate(peers):
            pltpu.make_async_remote_copy(src_ref=in_ref, dst_ref=out_ref.at[4 * px + 2 * py + pc], send_sem=send_sems.at[j],
                                         recv_sem=recv_sems.at[j], device_id=(px, py, pc), device_id_type=MESH).wait_recv()
        for cp in cps:
            cp.wait_send()
        local.wait()

    return pl.pallas_call(
        body, name="all_exchange", in_specs=[ANY], out_specs=ANY, out_shape=jax.ShapeDtypeStruct((N_DEV, r, cols), v.dtype),
        scratch_shapes=[pltpu.SemaphoreType.DMA((7,)), pltpu.SemaphoreType.DMA((7,)), pltpu.SemaphoreType.DMA(())])(v)


def _sum_slots(a, name):
    n, r, cols = a.shape
    tr = _pick(r, 512) if r % 8 == 0 else r
    for cand in (512, 256, 128, 64, 32, 16, 8):
        if r % cand == 0:
            tr = cand
            break

    def body(a_ref, o_ref):
        acc = a_ref[0]
        for k in range(1, n):
            acc = acc + a_ref[k]
        o_ref[...] = acc

    return pl.pallas_call(
        body, name=name, grid=(r // tr,), in_specs=[pl.BlockSpec((n, tr, cols), lambda i: (0, i, 0))],
        out_specs=pl.BlockSpec((tr, cols), lambda i: (i, 0)), out_shape=jax.ShapeDtypeStruct((r, cols), F32),
        compiler_params=_params("parallel"))(a)


def _join(name, stacked):
    ax = SHARDED[name][1]
    return jnp.concatenate([stacked[k] for k in range(N_CHIPS)], axis=ax)


def _split(name, full):
    ax = SHARDED[name][1]
    return jnp.stack(jnp.split(full, N_CHIPS, axis=ax))


def _heads_pad(a, real, axis):
    shp = a.shape
    a = a.reshape(shp[:axis] + (MLA_HEADS, real) + shp[axis + 1:])
    pad = [(0, 0)] * a.ndim
    pad[axis + 1] = (0, LANES - real)
    a = jnp.pad(a, pad)
    return a.reshape(shp[:axis] + (HP,) + shp[axis + 1:])


def _heads_unpad(a, real, axis):
    shp = a.shape
    a = a.reshape(shp[:axis] + (MLA_HEADS, LANES) + shp[axis + 1:])
    a = lax.slice_in_dim(a, 0, real, axis=axis + 1)
    return a.reshape(shp[:axis] + (MLA_HEADS * real,) + shp[axis + 1:])


def _lane_place(a, start):
    n = a.shape[-1]
    pad = [(0, 0)] * (a.ndim - 1) + [(start, LANES - start - n)]
    return jnp.pad(a, pad)


_O_UV, _O_CQ, _O_CKV, _O_KR, _O_Z, _O_XBC, _O_DT, _O_G = 0, 1024, 1408, 1664, 1696, 2208, 3232, 3240


def _w_in_pad(w):
    sl = lambda a, b: w[:, a:b]
    xs = _heads_pad(sl(_O_XBC, _O_XBC + SSD_INNER), SSD_HEAD_DIM, 1)
    bc = sl(_O_XBC + SSD_INNER, _O_DT)
    main = jnp.concatenate([sl(_O_UV, _O_CQ), _heads_pad(sl(_O_Z, _O_XBC), SSD_HEAD_DIM, 1), xs, sl(_O_G, IN_COLS)], axis=1)
    tail = jnp.concatenate([bc, sl(_O_CKV, _O_KR), sl(_O_CQ, _O_CKV), _lane_place(sl(_O_KR, _O_Z), MLA_NOPE),
                            _lane_place(sl(_O_DT, _O_G), 0), jnp.zeros((w.shape[0], PW_TAIL - T_DT - LANES), w.dtype)], axis=1)
    return main, tail


def _w_in_unpad(gm, gt):
    m = lambda a, n: gm[:, a:a + n]
    t = lambda a, n: gt[:, a:a + n]
    parts = [m(C_UV, 1024), t(T_CQ, MLA_Q_RANK), t(T_CKV, MLA_KV_RANK), t(T_KR + MLA_NOPE, MLA_ROPE),
             _heads_unpad(m(C_Z, HP), SSD_HEAD_DIM, 1), _heads_unpad(m(C_XS, HP), SSD_HEAD_DIM, 1), t(T_BC, BCW),
             t(T_DT, SSD_HEADS), m(C_G, 3 * D_MODEL)]
    return jnp.concatenate(parts, axis=1)


def _xbc_pad(a):
    return jnp.concatenate([_heads_pad(a[..., :SSD_INNER], SSD_HEAD_DIM, a.ndim - 1), a[..., SSD_INNER:]], axis=-1)


def _xbc_unpad(a):
    return jnp.concatenate([_heads_unpad(a[..., :HP], SSD_HEAD_DIM, a.ndim - 1), a[..., HP:]], axis=-1)


def _rope_tables(positions):
    inv_freq = 1.0 / (ROPE_THETA ** (jnp.arange(0, MLA_ROPE, 2, dtype=F32) / MLA_ROPE))
    ang = positions.astype(F32)[:, None] * inv_freq
    cos, sin = jnp.cos(ang), jnp.sin(ang)
    s = positions.shape[0]
    half = MLA_ROPE // 2
    z = lambda n: jnp.zeros((s, n), F32)
    ct = jnp.concatenate([jnp.ones((s, MLA_NOPE), F32), cos, cos, z(LANES - MLA_QK)], axis=1)
    s1 = jnp.concatenate([z(MLA_NOPE), -sin, z(half), z(LANES - MLA_QK)], axis=1)
    s2 = jnp.concatenate([z(MLA_NOPE), z(half), sin, z(LANES - MLA_QK)], axis=1)
    return ct, s1, s2


def _layer_weights(full, small, l):
    w = {}
    for n in ('ffn1_w_in', 'ffn2_w_in'):
        w[n] = full[n]
    for n in ('ffn1_w_out', 'ffn2_w_out', 'w_out'):
        g = full[n]
        w[n] = g.reshape((N_CHIPS * g.shape[1], g.shape[2]))
    fl = {n: _join(n, full[n]) for n in ('w_in', 'mla_w_uq', 'mla_w_ukv', 'w_branch', 'ssd_conv_w')}
    w['w_in_main'], w['w_in_tail'] = _w_in_pad(fl['w_in'])
    w['wuq'] = _heads_pad(fl['mla_w_uq'], MLA_QK, 1)
    ukv = fl['mla_w_ukv'].reshape(MLA_KV_RANK, MLA_HEADS, MLA_NOPE + MLA_V)
    zero = jnp.zeros((MLA_KV_RANK, MLA_HEADS, LANES - MLA_NOPE), ukv.dtype)
    wk = jnp.concatenate([ukv[:, :, :MLA_NOPE], zero], axis=2).reshape(MLA_KV_RANK, HP)
    wv = jnp.concatenate([ukv[:, :, MLA_NOPE:], zero], axis=2).reshape(MLA_KV_RANK, HP)
    w['wkv'] = jnp.concatenate([wk, wv], axis=1)
    wb = fl['w_branch']
    w['wb0'] = wb[0]
    w['wb1'] = _heads_pad(wb[1], MLA_V, 0)
    w['wb2'] = _heads_pad(wb[2], SSD_HEAD_DIM, 0)
    w['conv_w'] = _xbc_pad(fl['ssd_conv_w'].astype(F32))
    row = lambda n: small[n][l][None, :]
    for n in ('ffn1_norm', 'mix_norm', 'gm_v_norm', 'mla_q_norm', 'mla_kv_norm', 'ffn2_norm'):
        w[n] = row(n)
    w['gm_w_s'] = small['gm_w_s'][l]
    w['gm_b_full'] = jnp.broadcast_to(small['gm_b_s'][l][:, :, None], (GM_GROUPS, CHUNK, LANES))
    w['gq'] = _lane_place(row('mla_q_gain'), 0)
    w['gk'] = _lane_place(row('mla_k_gain'), 0)
    w['conv_b'] = _xbc_pad(row('ssd_conv_b'))
    w['dt_bias'] = _lane_place(row('ssd_dt_bias'), 0)
    w['a_log'] = _lane_place(row('ssd_a_log'), 0)
    w['d_vec'] = jnp.repeat(small['ssd_d'][l], LANES)[None, :]
    w['ssd_norm'] = _heads_pad(row('ssd_norm'), SSD_HEAD_DIM, 1)
    return w


def _ffn_fwd(x, norm, w4, w_out, tag):
    h, gate, up, act = _ffn_in(x, norm, w4, f"{tag}_in")
    y = _ffn_out(act, w_out, x, f"{tag}_out")
    return y, (x, h, gate, up, act)


def _ffn_bwd(dy, saved, norm, w4, w_out, tag):
    x, h, gate, up, act = saved
    dw_out = _ffn_dwout(act, dy, f"{tag}_dwout")
    da = _ffn_dact(dy, w_out, gate, up, f"{tag}_dact")
    dw_in = _ffn_dwin(h, da, f"{tag}_dwin")
    dx, dnorm = _ffn_dx(da, w4, x, norm, dy, f"{tag}_dx")
    return dx, dnorm, dw_in, dw_out.reshape((N_CHIPS, 2 * FC // N_CHIPS, D_MODEL))


def _mixer_fwd(x, w, tabs, tag, side=None):
    h = _rmsnorm_fwd(x, w['mix_norm'], f"{tag}_norm")
    pm = _matmul(h, w['w_in_main'], out_dtype=BF16, name=f"{tag}_proj_main")
    pt = _matmul(h, w['w_in_tail'], name=f"{tag}_proj_tail")
    ya = _gmlp_fwd(pm, w['gm_v_norm'], w['gm_w_s'], w['gm_b_full'], f"{tag}_gmlp")
    q, k, v = _mla_pre_fwd(pt, tabs, w['mla_q_norm'], w['mla_kv_norm'], w['wuq'], w['wkv'], w['gq'], w['gk'], f"{tag}_mla_pre")
    (o, lse), carried = _attn_fwd(q, k, v, f"{tag}_attn", side)
    xs = _conv_fwd(pm, C_XS, HP, w['conv_w'][:, :HP], w['conv_b'][:, :HP], f"{tag}_conv_x")
    bc = _conv_fwd(pt, T_BC, BCW, w['conv_w'][:, HP:], w['conv_b'][:, HP:], f"{tag}_conv_bc")
    dtb, dab = _dt_fwd(pt, w['dt_bias'], w['a_log'], f"{tag}_dt")
    ys, s_in = _scan_fwd(xs, bc, dtb, dab, f"{tag}_scan")
    yc = _ssd_post_fwd(ys, xs, pm, w['d_vec'], w['ssd_norm'], f"{tag}_ssd_post")
    mg, y = _merge_fwd(pm, ya, o, yc, w['wb0'], w['wb1'], w['wb2'], w['w_out'], x, f"{tag}_merge")
    return y, (x, h, pm, pt, ya, q, k, v, o, lse, xs, bc, dtb, dab, ys, s_in, yc, mg), carried


def _reduce_to_chip(pending, run_first, run_second):
    got = run_first(_pair_exchange(pending))
    sums = [_pair_add(g, r, f"pair_add_{n}") for g, r, n in zip(pending, got, REDUCED)]
    arrived = run_second(_chip_exchange([s16 for s16, _ in sums]))
    return [_chip_add(own, a, f"chip_add_{n}") for (_, own), a, n in zip(sums, arrived, REDUCED)]


def _mixer_bwd(dy, saved, w, tabs, tag, pending=None):
    x, h, pm, pt, ya, q, k, v, o, lse, xs, bc, dtb, dab, ys, s_in, yc, mg = saved
    g = {}
    g['w_out'] = _matmul(mg, dy, ta=True, name=f"{tag}_dwout").reshape((N_CHIPS, D_MODEL // N_CHIPS, D_MODEL))
    d0, d1, d2, dgates, dya, do, dyc = _merge_bwd(pm, ya, o, yc, w['wb0'], w['wb1'], w['wb2'], w['w_out'], dy, f"{tag}_dmerge")
    dwb0 = _matmul(ya, d0, ta=True, name=f"{tag}_dwb0")
    dwb1 = _matmul(o, d1, ta=True, name=f"{tag}_dwb1")
    dwb2 = _matmul(yc, d2, ta=True, name=f"{tag}_dwb2")
    g['w_branch'] = _split('w_branch', jnp.stack([dwb0, _heads_unpad(dwb1, MLA_V, 0), _heads_unpad(dwb2, SSD_HEAD_DIM, 0)]))
    duv, g['gm_v_norm'], g['gm_w_s'], db = _gmlp_bwd(pm, w['gm_v_norm'], w['gm_w_s'], w['gm_b_full'], dya, f"{tag}_dgmlp")
    g['gm_b_s'] = db.T
    res = {}

    def with_dq(side):
        (res['dq'], res['delta']), carried = _attn_bwd_dq(q, k, v, o, lse, do, f"{tag}_dattn_q", side)
        return carried

    def with_dkv(side):
        (res['dk'], res['dv']), carried = _attn_bwd_dkv(q, k, v, lse, res['delta'], do, f"{tag}_dattn_kv", side)
        return carried

    if pending is None:
        reduced = None
        with_dq(None)
        with_dkv(None)
    else:
        reduced = _reduce_to_chip(pending, with_dq, with_dkv)
    dq, dk, dv = res['dq'], res['dk'], res['dv']
    dcq, dckv, dkr, dwuq, dwkv, g['mla_q_norm'], g['mla_kv_norm'], dgq, dgk = _mla_pre_bwd(
        pt, tabs, w['mla_q_norm'], w['mla_kv_norm'], w['wuq'], w['wkv'], w['gq'], w['gk'], dq, dk, dv, f"{tag}_dmla_pre")
    g['mla_w_uq'] = _split('mla_w_uq', _heads_unpad(dwuq, MLA_QK, 1))
    dwk = dwkv[:, :HP].reshape(MLA_KV_RANK, MLA_HEADS, LANES)[:, :, :MLA_NOPE]
    dwv = dwkv[:, HP:].reshape(MLA_KV_RANK, MLA_HEADS, LANES)[:, :, :MLA_V]
    g['mla_w_ukv'] = _split('mla_w_ukv', jnp.concatenate([dwk, dwv], axis=2).reshape(MLA_KV_RANK, MLA_HEADS * (MLA_NOPE + MLA_V)))
    g['mla_q_gain'], g['mla_k_gain'] = dgq[:, :MLA_QK], dgk[:, :MLA_QK]
    dys, dz, dssd_norm, dd = _ssd_post_bwd(ys, xs, pm, w['d_vec'], w['ssd_norm'], dyc, f"{tag}_dssd_post")
    g['ssd_norm'] = _heads_unpad(dssd_norm, SSD_HEAD_DIM, 1)
    g['ssd_d'] = jnp.sum(dd.reshape(SSD_HEADS, LANES), axis=1)[None, :]
    dxs, dbm, dcm, dda, ddtx = _scan_bwd(xs, bc, dtb, dab, s_in, dys, w['d_vec'], f"{tag}_dscan")
    dxs16, dcw_x, dcb_x = _conv_bwd(pm, C_XS, HP, w['conv_w'][:, :HP], w['conv_b'][:, :HP], dxs, f"{tag}_dconv_x")
    dbc16, dcw_bc, dcb_bc = _conv_bwd(pt, T_BC, BCW, w['conv_w'][:, HP:], w['conv_b'][:, HP:],
                                      jnp.concatenate([dbm, dcm], axis=1), f"{tag}_dconv_bc")
    g['ssd_conv_w'] = _xbc_unpad(jnp.concatenate([dcw_x, dcw_bc], axis=1))
    g['ssd_conv_b'] = _xbc_unpad(jnp.concatenate([dcb_x, dcb_bc], axis=1))
    ddt, dbias, dalog = _dt_bwd(pt, w['dt_bias'], w['a_log'], dda, ddtx, f"{tag}_ddt")
    g['ssd_dt_bias'], g['ssd_a_log'] = dbias[:, :SSD_HEADS], dalog[:, :SSD_HEADS]
    s = x.shape[0]
    dpm = jnp.concatenate([duv, dz, dxs16, dgates], axis=1)
    dpt = jnp.concatenate([dbc16, dckv, dcq, dkr, ddt, jnp.zeros((s, PW_TAIL - T_DT - LANES), BF16)], axis=1)
    g['w_in'] = _split('w_in', _w_in_unpad(_matmul(h, dpm, ta=True, name=f"{tag}_dwin_main"),
                                           _matmul(h, dpt, ta=True, name=f"{tag}_dwin_tail")))
    dh = _matmul(dpt, w['w_in_tail'], tb=True, name=f"{tag}_dh_tail")
    dh = _matmul(dpm, w['w_in_main'], tb=True, res=dh, name=f"{tag}_dh_main")
    dx, g['mix_norm'] = _rmsnorm_bwd(x, w['mix_norm'], dh, dy, f"{tag}_dnorm")
    return dx, g, reduced


_CONV_ROWS = 32


def _rows_cols(a, lead):
    return a.reshape(a.shape[:lead] + (int(np.prod(a.shape[lead:-1])), a.shape[-1]))


def _shard_views(wts):
    views = []
    for n in SHARDED_ORDER:
        a = _rows_cols(wts[n].astype(BF16), 1)
        if n == 'ssd_conv_w':
            a = jnp.pad(a, ((0, 0), (0, _CONV_ROWS - a.shape[1]), (0, 0)))
        views.append(a)
    return views


def _gathered_layer(arrays):
    out = {}
    for n, a in zip(SHARDED_ORDER, arrays):
        shp = _shard_shape(n)
        if n == 'ssd_conv_w':
            a = a[:, :shp[0]]
        out[n] = a.reshape((N_CHIPS,) + shp)
    return out


def _local_step(x, positions, target, weights, small, distributed=True):
    tabs = _rope_tables(positions)
    if distributed:
        gathered = _place_own(_run_exchange(_gather_exchange(weights, 0), "gather_first"), weights, 0)
    ws, saved = [], []
    for l in range(DEPTH):
        w = _layer_weights(_gathered_layer(gathered) if distributed else weights[l], small, l)
        ws.append(w)
        side = _gather_exchange(weights, l + 1) if distributed and l + 1 < DEPTH else None
        x, s1 = _ffn_fwd(x, w['ffn1_norm'], w['ffn1_w_in'], w['ffn1_w_out'], "ffn1")
        x, s2, carried = _mixer_fwd(x, w, tabs, "mix", side)
        x, s3 = _ffn_fwd(x, w['ffn2_norm'], w['ffn2_w_in'], w['ffn2_w_out'], "ffn2")
        saved.append((s1, s2, s3))
        if side is not None:
            gathered = _place_own(carried, weights, l + 1)
    dy, sq = _loss_head(x, target, "loss_head")
    loss = 0.5 * jnp.sum(sq) / D_MODEL
    grads, reduced, pending = [None] * DEPTH, [None] * DEPTH, None
    for l in reversed(range(DEPTH)):
        w = ws[l]
        s1, s2, s3 = saved[l]
        dy, dn2, dwi2, dwo2 = _ffn_bwd(dy, s3, w['ffn2_norm'], w['ffn2_w_in'], w['ffn2_w_out'], "ffn2")
        dy, g, red = _mixer_bwd(dy, s2, w, tabs, "mix", pending)
        if pending is not None:
            reduced[l + 1] = red
        dy, dn1, dwi1, dwo1 = _ffn_bwd(dy, s1, w['ffn1_norm'], w['ffn1_w_in'], w['ffn1_w_out'], "ffn1")
        g.update(ffn1_norm=dn1, ffn1_w_in=dwi1, ffn1_w_out=dwo1, ffn2_norm=dn2, ffn2_w_in=dwi2, ffn2_w_out=dwo2)
        grads[l] = g
        if distributed:
            pending = [_rows_cols(g[n], 1) for n in REDUCED]
    if distributed:
        reduced[0] = _reduce_to_chip(pending, lambda ex: _run_exchange(ex, "pair_exchange"),
                                     lambda ex: _run_exchange(ex, "chip_exchange"))
    return loss, dy, grads, reduced


SMALL_PACK = SMALL_ORDER + ['ssd_conv_w']


def _pack_small(per_layer_rows, tail=None):
    parts = [per_layer_rows[l][n].reshape(-1).astype(F32) for l in range(DEPTH) for n in SMALL_PACK]
    if tail is not None:
        parts.append(tail.reshape(1))
    flat = jnp.concatenate(parts)
    rows = -(-flat.shape[0] // LANES)
    rows = -(-rows // 8) * 8
    return jnp.pad(flat, (0, rows * LANES - flat.shape[0])).reshape(rows, LANES)


def _unpack_small(buf, shapes):
    flat = buf.reshape(-1)
    off = 0
    out = {n: [] for n in SMALL_PACK}
    for l in range(DEPTH):
        for n in SMALL_PACK:
            size = int(np.prod(shapes[n]))
            out[n].append(flat[off:off + size].reshape(shapes[n]))
            off += size
    return {n: jnp.stack(v) for n, v in out.items()}


def kernel(x, positions, ffn1_norm, ffn1_w_in, ffn1_w_out, mix_norm, w_in, gm_v_norm, gm_w_s, gm_b_s, mla_q_norm, mla_kv_norm, mla_w_uq, mla_w_ukv, mla_q_gain, mla_k_gain, ssd_conv_w, ssd_conv_b, ssd_dt_bias, ssd_a_log, ssd_d, ssd_norm, w_branch, w_out, ffn2_norm, ffn2_w_in, ffn2_w_out, loss_target, m_ffn1_norm, m_ffn1_w_in, m_ffn1_w_out, m_mix_norm, m_w_in, m_gm_v_norm, m_gm_w_s, m_gm_b_s, m_mla_q_norm, m_mla_kv_norm, m_mla_w_uq, m_mla_w_ukv, m_mla_q_gain, m_mla_k_gain, m_ssd_conv_w, m_ssd_conv_b, m_ssd_dt_bias, m_ssd_a_log, m_ssd_d, m_ssd_norm, m_w_branch, m_w_out, m_ffn2_norm, m_ffn2_w_in, m_ffn2_w_out, v_ffn1_norm, v_ffn1_w_in, v_ffn1_w_out, v_mix_norm, v_w_in, v_gm_v_norm, v_gm_w_s, v_gm_b_s, v_mla_q_norm, v_mla_kv_norm, v_mla_w_uq, v_mla_w_ukv, v_mla_q_gain, v_mla_k_gain, v_ssd_conv_w, v_ssd_conv_b, v_ssd_dt_bias, v_ssd_a_log, v_ssd_d, v_ssd_norm, v_w_branch, v_w_out, v_ffn2_norm, v_ffn2_w_in, v_ffn2_w_out):
    wts = dict(zip(WEIGHTS, (ffn1_norm, ffn1_w_in, ffn1_w_out, mix_norm, w_in, gm_v_norm, gm_w_s, gm_b_s, mla_q_norm, mla_kv_norm,
                             mla_w_uq, mla_w_ukv, mla_q_gain, mla_k_gain, ssd_conv_w, ssd_conv_b, ssd_dt_bias, ssd_a_log, ssd_d,
                             ssd_norm, w_branch, w_out, ffn2_norm, ffn2_w_in, ffn2_w_out)))
    mom = dict(zip(WEIGHTS, (m_ffn1_norm, m_ffn1_w_in, m_ffn1_w_out, m_mix_norm, m_w_in, m_gm_v_norm, m_gm_w_s, m_gm_b_s, m_mla_q_norm,
                             m_mla_kv_norm, m_mla_w_uq, m_mla_w_ukv, m_mla_q_gain, m_mla_k_gain, m_ssd_conv_w, m_ssd_conv_b,
                             m_ssd_dt_bias, m_ssd_a_log, m_ssd_d, m_ssd_norm, m_w_branch, m_w_out, m_ffn2_norm, m_ffn2_w_in,
                             m_ffn2_w_out)))
    var = dict(zip(WEIGHTS, (v_ffn1_norm, v_ffn1_w_in, v_ffn1_w_out, v_mix_norm, v_w_in, v_gm_v_norm, v_gm_w_s, v_gm_b_s, v_mla_q_norm,
                             v_mla_kv_norm, v_mla_w_uq, v_mla_w_ukv, v_mla_q_gain, v_mla_k_gain, v_ssd_conv_w, v_ssd_conv_b,
                             v_ssd_dt_bias, v_ssd_a_log, v_ssd_d, v_ssd_norm, v_w_branch, v_w_out, v_ffn2_norm, v_ffn2_w_in,
                             v_ffn2_w_out)))
    cx, cy, _ = _me()
    mychip = _chip_index(cx, cy)

    small = {n: wts[n] for n in SMALL_ORDER}
    loss_part, dx, grads, reduced = _local_step(x[0], positions[0], loss_target[0], _shard_views(wts), small)
    rows_cols = _rows_cols
    halves = [jnp.stack([reduced[l][t] for l in range(DEPTH)]) for t in range(len(REDUCED))]
    theirs = _run_exchange(_pair_share(halves), "pair_share")
    shapes = {n: wts[n].shape[1:] for n in SMALL_ORDER}
    shapes['ssd_conv_w'] = SHARDED['ssd_conv_w'][0]
    summed = _sum_slots(_all_exchange(_pack_small(grads, tail=loss_part)), "small_sum")
    small_g = _unpack_small(summed, shapes)
    loss = summed.reshape(-1)[DEPTH * sum(int(np.prod(shapes[n])) for n in SMALL_PACK)]
    conv_full = small_g.pop('ssd_conv_w')
    shard_cols = _shard_shape('ssd_conv_w')[1]
    small_g['ssd_conv_w'] = lax.dynamic_slice_in_dim(conv_full, mychip * shard_cols, shard_cols, axis=2)
    shapes['ssd_conv_w'] = _shard_shape('ssd_conv_w')

    grad, delta, new_m, new_v = {}, {}, {}, {}
    for n, a, b in zip(REDUCED, halves, theirs):
        shp = wts[n].shape
        outs = _adamw_sharded(rows_cols(wts[n], 1), rows_cols(mom[n], 1), rows_cols(var[n], 1), a, b, f"adamw_{n}")
        grad[n], delta[n], new_m[n], new_v[n] = [o.reshape(shp) for o in outs]
    per_layer = lambda t: [{n: t[n][l] for n in SMALL_PACK} for l in range(DEPTH)]
    d, nm, nv = _adamw(_pack_small(per_layer(wts)), _pack_small(per_layer(small_g)), _pack_small(per_layer(mom)),
                       _pack_small(per_layer(var)), "adamw_small")
    sd, snm, snv = _unpack_small(d, shapes), _unpack_small(nm, shapes), _unpack_small(nv, shapes)
    for n in SMALL_PACK:
        grad[n], delta[n], new_m[n], new_v[n] = small_g[n], sd[n], snm[n], snv[n]
    return (loss, dx[None], *[grad[n] for n in WEIGHTS], *[delta[n] for n in WEIGHTS], *[new_m[n] for n in WEIGHTS],
            *[new_v[n] for n in WEIGHTS])
```

```python
import functools
import math

import numpy as np
import jax
import jax.numpy as jnp
from jax import lax
from jax.experimental import pallas as pl
from jax.experimental.pallas import tpu as pltpu

F32, BF16 = jnp.float32, jnp.bfloat16
MESH = pl.DeviceIdType.MESH

D_MODEL, DEPTH, D_FF, EPS = 1024, 4, 2816, 1e-6
GM_WIDTH, GM_GROUPS, CHUNK = 512, 4, 128
MLA_HEADS, MLA_Q_RANK, MLA_KV_RANK, MLA_NOPE, MLA_ROPE, MLA_V = 8, 384, 256, 64, 32, 64
MLA_QK = MLA_NOPE + MLA_ROPE
ROPE_THETA = 10000.0
SSD_HEADS, SSD_HEAD_DIM, SSD_GROUPS, SSD_STATE, SSD_CONV = 8, 64, 2, 128, 4
SSD_INNER = SSD_HEADS * SSD_HEAD_DIM
IN_COLS = 6312
LANES = 128
ADAM_LR, ADAM_B1, ADAM_B2, ADAM_EPS, ADAM_WD, ADAM_STEP = 0.001, 0.9, 0.999, 1e-08, 0.01, 10

C_UV, C_Z, C_XS, C_G, PW_MAIN = 0, 1024, 2048, 3072, 6144
T_BC, T_CKV, T_CQ, T_KR, T_DT, PW_TAIL = 0, 512, 768, 1152, 1280, 1536
HP = MLA_HEADS * LANES
FC = 2 * D_FF // 4

WEIGHTS = ['ffn1_norm', 'ffn1_w_in', 'ffn1_w_out', 'mix_norm', 'w_in', 'gm_v_norm', 'gm_w_s', 'gm_b_s', 'mla_q_norm',
           'mla_kv_norm', 'mla_w_uq', 'mla_w_ukv', 'mla_q_gain', 'mla_k_gain', 'ssd_conv_w', 'ssd_conv_b', 'ssd_dt_bias',
           'ssd_a_log', 'ssd_d', 'ssd_norm', 'w_branch', 'w_out', 'ffn2_norm', 'ffn2_w_in', 'ffn2_w_out']
SHARDED = {'ffn1_w_in': ((1024, 5632), 1), 'ffn1_w_out': ((2816, 1024), 0), 'w_in': ((1024, 6312), 1),
           'mla_w_uq': ((384, 768), 1), 'mla_w_ukv': ((256, 1024), 1), 'ssd_conv_w': ((4, 1024), 1),
           'w_branch': ((3, 512, 1024), 2), 'w_out': ((1024, 1024), 0), 'ffn2_w_in': ((1024, 5632), 1),
           'ffn2_w_out': ((2816, 1024), 0)}
SHARDED_ORDER = [n for n in WEIGHTS if n in SHARDED]
SMALL_ORDER = [n for n in WEIGHTS if n not in SHARDED]
REDUCED = [n for n in SHARDED_ORDER if n != 'ssd_conv_w']
N_CHIPS = 4
HALF_L = DEPTH // 2


def _shard_shape(name):
    shape, ax = SHARDED[name]
    return tuple(d // N_CHIPS if i == ax else d for i, d in enumerate(shape))


def _pick(dim, target):
    if dim <= target:
        return dim
    t = (target // LANES) * LANES
    while t >= LANES:
        if dim % t == 0:
            return t
        t -= LANES
    return dim


def _sigmoid(x):
    return 1.0 / (1.0 + jnp.exp(-x))


def _params(*sem):
    return pltpu.CompilerParams(dimension_semantics=sem, vmem_limit_bytes=56 * 1024 * 1024)


def _matmul(a, b, *, ta=False, tb=False, out_dtype=F32, scale=1.0, res=None, name, side=None):
    if ta:
        k_dim, m_dim = a.shape
    else:
        m_dim, k_dim = a.shape
    if tb:
        n_dim, k2 = b.shape
    else:
        k2, n_dim = b.shape
    assert k_dim == k2, (a.shape, b.shape, ta, tb)
    tm, tn, tk = _pick(m_dim, 1024), _pick(n_dim, 1024), _pick(k_dim, 1024)
    nk = k_dim // tk
    dn = (((0 if ta else 1,), (1 if tb else 0,)), ((), ()))

    def body(*refs):
        if res is not None:
            a_ref, b_ref, r_ref, o_ref, acc = refs
        else:
            a_ref, b_ref, o_ref, acc = refs
        k = pl.program_id(2)

        @pl.when(k == 0)
        def _():
            acc[...] = jnp.zeros_like(acc)

        acc[...] += lax.dot_general(a_ref[...].astype(BF16), b_ref[...].astype(BF16), dn, preferred_element_type=F32)

        @pl.when(k == nk - 1)
        def _():
            r = acc[...]
            if scale != 1.0:
                r = r * scale
            if res is not None:
                r = r + r_ref[...]
            o_ref[...] = r.astype(out_dtype)

    a_spec = pl.BlockSpec((tk, tm), lambda j, i, k: (k, i)) if ta else pl.BlockSpec((tm, tk), lambda j, i, k: (i, k))
    b_spec = pl.BlockSpec((tn, tk), lambda j, i, k: (j, k)) if tb else pl.BlockSpec((tk, tn), lambda j, i, k: (k, j))
    in_specs = [a_spec, b_spec]
    args = [a, b]
    if res is not None:
        in_specs.append(pl.BlockSpec((tm, tn), lambda j, i, k: (i, j)))
        args.append(res)
    (out,), carried = _call(
        body, name=name, grid=(n_dim // tn, m_dim // tm, nk), in_specs=in_specs,
        out_specs=[pl.BlockSpec((tm, tn), lambda j, i, k: (i, j))],
        out_shape=[jax.ShapeDtypeStruct((m_dim, n_dim), out_dtype)],
        scratch_shapes=[pltpu.VMEM((tm, tn), F32)], args=args, semantics=("parallel", "parallel", "arbitrary"), side=side)
    return out if side is None else (out, carried)


def _rmsnorm_fwd(x, gain, name):
    s, d = x.shape
    tm = _pick(s, 512)

    def body(x_ref, g_ref, o_ref):
        xv = x_ref[...]
        r = lax.rsqrt(jnp.mean(xv * xv, axis=-1, keepdims=True) + EPS)
        o_ref[...] = (xv * r * g_ref[...]).astype(BF16)

    return pl.pallas_call(
        body, name=name, grid=(s // tm,),
        in_specs=[pl.BlockSpec((tm, d), lambda i: (i, 0)), pl.BlockSpec((1, d), lambda i: (0, 0))],
        out_specs=pl.BlockSpec((tm, d), lambda i: (i, 0)),
        out_shape=jax.ShapeDtypeStruct((s, d), BF16), compiler_params=_params("parallel"))(x, gain)


def _rmsnorm_bwd(x, gain, dh, dres, name):
    s, d = x.shape
    tm = _pick(s, 512)

    def body(x_ref, g_ref, dh_ref, dr_ref, dx_ref, dg_ref):
        @pl.when(pl.program_id(0) == 0)
        def _():
            dg_ref[...] = jnp.zeros_like(dg_ref)

        xv, dhv = x_ref[...], dh_ref[...]
        r = lax.rsqrt(jnp.mean(xv * xv, axis=-1, keepdims=True) + EPS)
        u = dhv * g_ref[...]
        dx_ref[...] = dr_ref[...] + r * u - xv * (r * r * r) * jnp.mean(xv * u, axis=-1, keepdims=True)
        dg_ref[...] += jnp.sum(dhv * xv * r, axis=0, keepdims=True)

    row = pl.BlockSpec((tm, d), lambda i: (i, 0))
    vec = pl.BlockSpec((1, d), lambda i: (0, 0))
    return pl.pallas_call(
        body, name=name, grid=(s // tm,), in_specs=[row, vec, row, row], out_specs=[row, vec],
        out_shape=[jax.ShapeDtypeStruct((s, d), F32), jax.ShapeDtypeStruct((1, d), F32)],
        compiler_params=_params("arbitrary"))(x, gain, dh, dres)


_NT = (((1,), (1,)), ((), ()))
_TN = (((0,), (0,)), ((), ()))


def _resident(shape):
    return pl.BlockSpec(shape, lambda *_: tuple(0 for _ in shape), pipeline_mode=pl.Buffered(1))


def _ffn_in(x, gain, w4, name, side=None):
    s, d = x.shape
    tm = _pick(s, 512)

    def body(x_ref, g_ref, w_ref, h_ref, gate_ref, up_ref, act_ref):
        xv = x_ref[...]
        r = lax.rsqrt(jnp.mean(xv * xv, axis=-1, keepdims=True) + EPS)
        h = (xv * r * g_ref[...]).astype(BF16)
        h_ref[...] = h
        for j in range(2):
            g16 = jnp.dot(h, w_ref[j], preferred_element_type=F32).astype(BF16)
            u16 = jnp.dot(h, w_ref[j + 2], preferred_element_type=F32).astype(BF16)
            gate_ref[j] = g16
            up_ref[j] = u16
            gf, uf = g16.astype(F32), u16.astype(F32)
            act_ref[j] = (gf * _sigmoid(gf) * uf).astype(BF16)

    half = pl.BlockSpec((2, tm, FC), lambda i: (0, i, 0))
    return _call(
        body, name=name, grid=(s // tm,),
        in_specs=[pl.BlockSpec((tm, d), lambda i: (i, 0)), pl.BlockSpec((1, d), lambda i: (0, 0)), _resident((4, d, FC))],
        out_specs=[pl.BlockSpec((tm, d), lambda i: (i, 0)), half, half, half],
        out_shape=[jax.ShapeDtypeStruct((s, d), BF16)] + [jax.ShapeDtypeStruct((2, s, FC), BF16)] * 3,
        scratch_shapes=[], args=(x, gain, w4), semantics=("parallel",), side=side)


def _ffn_out(act, w_out, x, name, side=None):
    s, d = x.shape
    tm = _pick(s, 512)

    def body(a_ref, w_ref, x_ref, o_ref):
        acc = jnp.dot(a_ref[0], w_ref[0:FC, :], preferred_element_type=F32)
        acc = acc + jnp.dot(a_ref[1], w_ref[FC:2 * FC, :], preferred_element_type=F32)
        o_ref[...] = x_ref[...] + 0.5 * acc

    row = pl.BlockSpec((tm, d), lambda i: (i, 0))
    (out,), carried = _call(
        body, name=name, grid=(s // tm,),
        in_specs=[pl.BlockSpec((2, tm, FC), lambda i: (0, i, 0)), _resident((2 * FC, d)), row], out_specs=[row],
        out_shape=[jax.ShapeDtypeStruct((s, d), F32)], scratch_shapes=[], args=(act, w_out, x), semantics=("parallel",),
        side=side)
    return out, carried


def _ffn_dact(dy, w_out, gate, up, name, side=None):
    s, d = dy.shape
    tm = _pick(s, 512)

    def body(dy_ref, w_ref, g_ref, u_ref, o_ref):
        dy16 = dy_ref[...].astype(BF16)
        for j in range(2):
            dact = 0.5 * lax.dot_general(dy16, w_ref[j * FC:(j + 1) * FC, :], _NT, preferred_element_type=F32)
            g, u = g_ref[j].astype(F32), u_ref[j].astype(F32)
            sg = _sigmoid(g)
            o_ref[j] = (dact * u * (sg * (1.0 + g * (1.0 - sg)))).astype(BF16)
            o_ref[j + 2] = (dact * g * sg).astype(BF16)

    half = pl.BlockSpec((2, tm, FC), lambda i: (0, i, 0))
    (out,), carried = _call(
        body, name=name, grid=(s // tm,),
        in_specs=[pl.BlockSpec((tm, d), lambda i: (i, 0)), _resident((2 * FC, d)), half, half],
        out_specs=[pl.BlockSpec((4, tm, FC), lambda i: (0, i, 0))],
        out_shape=[jax.ShapeDtypeStruct((4, s, FC), BF16)], scratch_shapes=[], args=(dy, w_out, gate, up),
        semantics=("parallel",), side=side)
    return out, carried


def _ffn_dwout(act, dy, name, side=None):
    s, d = dy.shape
    tk = _pick(s, 1024)
    nk = s // tk

    def body(a_ref, dy_ref, o_ref):
        k = pl.program_id(1)

        @pl.when(k == 0)
        def _():
            o_ref[...] = jnp.zeros_like(o_ref)

        o_ref[...] += lax.dot_general(a_ref[...], dy_ref[...].astype(BF16), _TN, preferred_element_type=F32)

        @pl.when(k == nk - 1)
        def _():
            o_ref[...] = 0.5 * o_ref[...]

    (out,), carried = _call(
        body, name=name, grid=(2, nk),
        in_specs=[pl.BlockSpec((None, tk, FC), lambda j, k: (j, k, 0)), pl.BlockSpec((tk, d), lambda j, k: (k, 0))],
        out_specs=[pl.BlockSpec((FC, d), lambda j, k: (j, 0))], out_shape=[jax.ShapeDtypeStruct((2 * FC, d), F32)],
        scratch_shapes=[], args=(act, dy), semantics=("parallel", "arbitrary"), side=side)
    return out, carried


def _ffn_dwin(h, da, name, side=None):
    s, d = h.shape
    tk = _pick(s, 1024)

    def body(h_ref, da_ref, o_ref):
        @pl.when(pl.program_id(1) == 0)
        def _():
            o_ref[...] = jnp.zeros_like(o_ref)

        o_ref[...] += lax.dot_general(h_ref[...], da_ref[...], _TN, preferred_element_type=F32)

    (out,), carried = _call(
        body, name=name, grid=(4, s // tk),
        in_specs=[pl.BlockSpec((tk, d), lambda j, k: (k, 0)), pl.BlockSpec((None, tk, FC), lambda j, k: (j, k, 0))],
        out_specs=[pl.BlockSpec((None, d, FC), lambda j, k: (j, 0, 0))], out_shape=[jax.ShapeDtypeStruct((4, d, FC), F32)],
        scratch_shapes=[], args=(h, da), semantics=("parallel", "arbitrary"), side=side)
    return out, carried


def _ffn_dx(da, w4, x, gain, dy, name, side=None):
    s, d = x.shape
    tm = _pick(s, 512)

    def body(da_ref, w_ref, x_ref, g_ref, dy_ref, dx_ref, dg_ref):
        @pl.when(pl.program_id(0) == 0)
        def _():
            dg_ref[...] = jnp.zeros_like(dg_ref)

        dh = jnp.zeros((tm, d), F32)
        for j in range(4):
            dh = dh + lax.dot_general(da_ref[j], w_ref[j], _NT, preferred_element_type=F32)
        xv = x_ref[...]
        r = lax.rsqrt(jnp.mean(xv * xv, axis=-1, keepdims=True) + EPS)
        u = dh * g_ref[...]
        dx_ref[...] = dy_ref[...] + r * u - xv * (r * r * r) * jnp.mean(xv * u, axis=-1, keepdims=True)
        dg_ref[...] += jnp.sum(dh * xv * r, axis=0, keepdims=True)

    row = pl.BlockSpec((tm, d), lambda i: (i, 0))
    vec = pl.BlockSpec((1, d), lambda i: (0, 0))
    return _call(
        body, name=name, grid=(s // tm,),
        in_specs=[pl.BlockSpec((4, tm, FC), lambda i: (0, i, 0)), _resident((4, d, FC)), row, vec, row],
        out_specs=[row, vec], out_shape=[jax.ShapeDtypeStruct((s, d), F32), jax.ShapeDtypeStruct((1, d), F32)],
        scratch_shapes=[], args=(da, w4, x, gain, dy), semantics=("arbitrary",), side=side)


_INV_SQRT2 = 0.7071067811865476
_INV_SQRT2PI = 0.3989422804014327


def _gelu(x):
    return 0.5 * x * (1.0 + lax.erf(x * _INV_SQRT2))


def _gelu_grad(x):
    return 0.5 * (1.0 + lax.erf(x * _INV_SQRT2)) + x * jnp.exp(-0.5 * x * x) * _INV_SQRT2PI


def _tril_mask():
    r = lax.broadcasted_iota(jnp.int32, (CHUNK, CHUNK), 0)
    c = lax.broadcasted_iota(jnp.int32, (CHUNK, CHUNK), 1)
    return r >= c


def _gmlp_fwd(p, v_gain, w_s, b_full, name):
    s = p.shape[0]
    tm = _pick(s, 512)
    nch = tm // CHUNK

    def body(uv_ref, g_ref, w_ref, b_ref, o_ref):
        gel = _gelu(uv_ref[...].astype(F32))
        u, v = gel[:, :GM_WIDTH], gel[:, GM_WIDTH:]
        r = lax.rsqrt(jnp.mean(v * v, axis=-1, keepdims=True) + EPS)
        vn = (v * r * g_ref[...]).astype(BF16)
        mask = _tril_mask()
        for g in range(GM_GROUPS):
            wm = jnp.where(mask, w_ref[g], 0.0).astype(BF16)
            for c in range(nch):
                rs, cs = slice(c * CHUNK, (c + 1) * CHUNK), slice(g * LANES, (g + 1) * LANES)
                sp = jnp.dot(wm, vn[rs, cs], preferred_element_type=F32) + b_ref[g]
                o_ref[rs, cs] = (u[rs, cs] * sp).astype(BF16)

    full3 = pl.BlockSpec((GM_GROUPS, CHUNK, CHUNK), lambda i: (0, 0, 0))
    return pl.pallas_call(
        body, name=name, grid=(s // tm,),
        in_specs=[pl.BlockSpec((tm, 2 * GM_WIDTH), lambda i: (i, C_UV // (2 * GM_WIDTH))),
                  pl.BlockSpec((1, GM_WIDTH), lambda i: (0, 0)), full3, full3],
        out_specs=pl.BlockSpec((tm, GM_WIDTH), lambda i: (i, 0)),
        out_shape=jax.ShapeDtypeStruct((s, GM_WIDTH), BF16), compiler_params=_params("parallel"))(p, v_gain, w_s, b_full)


def _gmlp_bwd(p, v_gain, w_s, b_full, dy, name):
    s = p.shape[0]
    tm = _pick(s, 512)
    nch = tm // CHUNK
    nsteps = s // tm

    def body(uv_ref, g_ref, w_ref, b_ref, dy_ref, duv_ref, dg_ref, dw_ref, db_ref, dvn_s, dbacc):
        step = pl.program_id(0)

        @pl.when(step == 0)
        def _():
            dg_ref[...] = jnp.zeros_like(dg_ref)
            dw_ref[...] = jnp.zeros_like(dw_ref)
            dbacc[...] = jnp.zeros_like(dbacc)

        uv = uv_ref[...].astype(F32)
        gel = _gelu(uv)
        u, v = gel[:, :GM_WIDTH], gel[:, GM_WIDTH:]
        r = lax.rsqrt(jnp.mean(v * v, axis=-1, keepdims=True) + EPS)
        gain = g_ref[...]
        vn32 = v * r * gain
        vn = vn32.astype(BF16)
        dy = dy_ref[...]
        mask = _tril_mask()
        for g in range(GM_GROUPS):
            wm = jnp.where(mask, w_ref[g], 0.0).astype(BF16)
            dwg = jnp.zeros((CHUNK, CHUNK), F32)
            dbg = jnp.zeros((CHUNK, LANES), F32)
            for c in range(nch):
                rs, cs = slice(c * CHUNK, (c + 1) * CHUNK), slice(g * LANES, (g + 1) * LANES)
                sp = jnp.dot(wm, vn[rs, cs], preferred_element_type=F32) + b_ref[g]
                dyc = dy[rs, cs]
                dsp = dyc * u[rs, cs]
                dsp16 = dsp.astype(BF16)
                duv_ref[rs, cs] = (dyc * sp * _gelu_grad(uv[rs, cs])).astype(BF16)
                dvn_s[rs, cs] = lax.dot_general(wm, dsp16, (((0,), (0,)), ((), ())), preferred_element_type=F32)
                dwg = dwg + lax.dot_general(dsp16, vn[rs, cs], (((1,), (1,)), ((), ())), preferred_element_type=F32)
                dbg = dbg + dsp
            dw_ref[g] += jnp.where(mask, dwg, 0.0)
            dbacc[:, g * LANES:(g + 1) * LANES] += dbg
        dvn = dvn_s[...]
        uu = dvn * gain
        dv = r * uu - v * (r * r * r) * jnp.mean(v * uu, axis=-1, keepdims=True)
        duv_ref[:, GM_WIDTH:] = (dv * _gelu_grad(uv[:, GM_WIDTH:])).astype(BF16)
        dg_ref[...] += jnp.sum(dvn * v * r, axis=0, keepdims=True)

        @pl.when(step == nsteps - 1)
        def _():
            for g in range(GM_GROUPS):
                db_ref[:, g:g + 1] = jnp.sum(dbacc[:, g * LANES:(g + 1) * LANES], axis=1, keepdims=True)

    full3 = pl.BlockSpec((GM_GROUPS, CHUNK, CHUNK), lambda i: (0, 0, 0))
    return pl.pallas_call(
        body, name=name, grid=(nsteps,),
        in_specs=[pl.BlockSpec((tm, 2 * GM_WIDTH), lambda i: (i, C_UV // (2 * GM_WIDTH))),
                  pl.BlockSpec((1, GM_WIDTH), lambda i: (0, 0)), full3, full3,
                  pl.BlockSpec((tm, GM_WIDTH), lambda i: (i, 0))],
        out_specs=[pl.BlockSpec((tm, 2 * GM_WIDTH), lambda i: (i, 0)), pl.BlockSpec((1, GM_WIDTH), lambda i: (0, 0)),
                   full3, pl.BlockSpec((CHUNK, GM_GROUPS), lambda i: (0, 0))],
        out_shape=[jax.ShapeDtypeStruct((s, 2 * GM_WIDTH), BF16), jax.ShapeDtypeStruct((1, GM_WIDTH), F32),
                   jax.ShapeDtypeStruct((GM_GROUPS, CHUNK, CHUNK), F32), jax.ShapeDtypeStruct((CHUNK, GM_GROUPS), F32)],
        scratch_shapes=[pltpu.VMEM((tm, GM_WIDTH), F32), pltpu.VMEM((CHUNK, GM_WIDTH), F32)],
        compiler_params=_params("arbitrary"))(p, v_gain, w_s, b_full, dy)


def _rope(x, ct, s1, s2):
    return x * ct + pltpu.roll(x, LANES - MLA_ROPE // 2, 1) * s1 + pltpu.roll(x, MLA_ROPE // 2, 1) * s2


def _rope_bwd(d, ct, s1, s2):
    return d * ct + pltpu.roll(d * s1, MLA_ROPE // 2, 1) + pltpu.roll(d * s2, LANES - MLA_ROPE // 2, 1)


def _head_norm(x, gain):
    r = lax.rsqrt(jnp.sum(x * x, axis=-1, keepdims=True) * (1.0 / MLA_QK) + EPS)
    return x * r * gain, r


def _head_norm_bwd(x, r, gain, d):
    u = d * gain
    return r * u - x * (r * r * r) * (jnp.sum(x * u, axis=-1, keepdims=True) * (1.0 / MLA_QK))


def _mla_specs(tm):
    cq = pl.BlockSpec((tm, MLA_Q_RANK), lambda i: (i, T_CQ // MLA_Q_RANK))
    ckv = pl.BlockSpec((tm, MLA_KV_RANK), lambda i: (i, T_CKV // MLA_KV_RANK))
    kr = pl.BlockSpec((tm, LANES), lambda i: (i, T_KR // LANES))
    tab = pl.BlockSpec((tm, LANES), lambda i: (i, 0))
    return cq, ckv, kr, tab


def _const(shape):
    return pl.BlockSpec(shape, lambda i: tuple(0 for _ in shape))


def _mla_pre_fwd(p, tabs, qn_g, kvn_g, wuq, wkv, gq, gk, name):
    s = p.shape[0]
    tm = _pick(s, 256)
    ct, s1, s2 = tabs

    def body(cq_ref, ckv_ref, kr_ref, ct_ref, s1_ref, s2_ref, qg_ref, kvg_ref, wuq_ref, wkv_ref, gq_ref, gk_ref,
             q_ref, k_ref, v_ref):
        cq, ckv, kr = cq_ref[...], ckv_ref[...], kr_ref[...]
        ctv, s1v, s2v = ct_ref[...], s1_ref[...], s2_ref[...]
        rq = lax.rsqrt(jnp.mean(cq * cq, axis=-1, keepdims=True) + EPS)
        q = jnp.dot((cq * rq * qg_ref[...]).astype(BF16), wuq_ref[...], preferred_element_type=F32)
        rk = lax.rsqrt(jnp.mean(ckv * ckv, axis=-1, keepdims=True) + EPS)
        kv = jnp.dot((ckv * rk * kvg_ref[...]).astype(BF16), wkv_ref[...], preferred_element_type=F32)
        v_ref[...] = kv[:, HP:].astype(BF16)
        for h in range(MLA_HEADS):
            hs = slice(h * LANES, (h + 1) * LANES)
            qh, _ = _head_norm(q[:, hs], gq_ref[...])
            q_ref[:, hs] = (_rope(qh, ctv, s1v, s2v) * _ATT_SCALE).astype(BF16)
            kh, _ = _head_norm(kv[:, hs] + kr, gk_ref[...])
            k_ref[:, hs] = _rope(kh, ctv, s1v, s2v).astype(BF16)

    cq_s, ckv_s, kr_s, tab_s = _mla_specs(tm)
    out = pl.BlockSpec((tm, HP), lambda i: (i, 0))
    return pl.pallas_call(
        body, name=name, grid=(s // tm,),
        in_specs=[cq_s, ckv_s, kr_s, tab_s, tab_s, tab_s, _const((1, MLA_Q_RANK)), _const((1, MLA_KV_RANK)),
                  _const((MLA_Q_RANK, HP)), _const((MLA_KV_RANK, 2 * HP)), _const((1, LANES)), _const((1, LANES))],
        out_specs=[out, out, out], out_shape=[jax.ShapeDtypeStruct((s, HP), BF16)] * 3,
        compiler_params=_params("parallel"))(p, p, p, ct, s1, s2, qn_g, kvn_g, wuq, wkv, gq, gk)


def _mla_pre_bwd(p, tabs, qn_g, kvn_g, wuq, wkv, gq, gk, dq, dk, dv, name, side=None):
    s = p.shape[0]
    tm = _pick(s, 256)
    ct, s1, s2 = tabs

    def body(cq_ref, ckv_ref, kr_ref, ct_ref, s1_ref, s2_ref, qg_ref, kvg_ref, wuq_ref, wkv_ref, gq_ref, gk_ref,
             dq_ref, dk_ref, dv_ref, dcq_ref, dckv_ref, dkr_ref, dwuq_ref, dwkv_ref, dqg_ref, dkvg_ref, dgq_ref, dgk_ref,
             dqp, dkvp):
        @pl.when(pl.program_id(0) == 0)
        def _():
            for ref in (dwuq_ref, dwkv_ref, dqg_ref, dkvg_ref, dgq_ref, dgk_ref):
                ref[...] = jnp.zeros_like(ref)

        cq, ckv, kr = cq_ref[...], ckv_ref[...], kr_ref[...]
        ctv, s1v, s2v = ct_ref[...], s1_ref[...], s2_ref[...]
        rq = lax.rsqrt(jnp.mean(cq * cq, axis=-1, keepdims=True) + EPS)
        qn = (cq * rq * qg_ref[...]).astype(BF16)
        q = jnp.dot(qn, wuq_ref[...], preferred_element_type=F32)
        rk = lax.rsqrt(jnp.mean(ckv * ckv, axis=-1, keepdims=True) + EPS)
        kvn = (ckv * rk * kvg_ref[...]).astype(BF16)
        kv = jnp.dot(kvn, wkv_ref[...], preferred_element_type=F32)
        gqv, gkv = gq_ref[...], gk_ref[...]
        dgq = jnp.zeros((1, LANES), F32)
        dgk = jnp.zeros((1, LANES), F32)
        dkr = jnp.zeros((tm, LANES), F32)
        for h in range(MLA_HEADS):
            hs = slice(h * LANES, (h + 1) * LANES)
            xq = q[:, hs]
            _, r = _head_norm(xq, gqv)
            d = _rope_bwd(dq_ref[:, hs], ctv, s1v, s2v)
            dgq = dgq + jnp.sum(d * xq * r, axis=0, keepdims=True)
            dqp[:, hs] = _head_norm_bwd(xq, r, gqv, d)
            xk = kv[:, hs] + kr
            _, r = _head_norm(xk, gkv)
            d = _rope_bwd(dk_ref[:, hs], ctv, s1v, s2v)
            dgk = dgk + jnp.sum(d * xk * r, axis=0, keepdims=True)
            dxk = _head_norm_bwd(xk, r, gkv, d)
            dkvp[:, hs] = dxk
            dkr = dkr + dxk
        dkvp[:, HP:] = dv_ref[...]
        dgq_ref[...] += dgq
        dgk_ref[...] += dgk
        dkr_ref[...] = dkr.astype(BF16)
        tn = (((0,), (0,)), ((), ()))
        nt = (((1,), (1,)), ((), ()))
        dq16 = dqp[...].astype(BF16)
        dwuq_ref[...] += lax.dot_general(qn, dq16, tn, preferred_element_type=F32)
        dqn = lax.dot_general(dq16, wuq_ref[...], nt, preferred_element_type=F32)
        dqg_ref[...] += jnp.sum(dqn * cq * rq, axis=0, keepdims=True)
        u = dqn * qg_ref[...]
        dcq_ref[...] = (rq * u - cq * (rq * rq * rq) * jnp.mean(cq * u, axis=-1, keepdims=True)).astype(BF16)
        dkv16 = dkvp[...].astype(BF16)
        dwkv_ref[...] += lax.dot_general(kvn, dkv16, tn, preferred_element_type=F32)
        dkvn = lax.dot_general(dkv16, wkv_ref[...], nt, preferred_element_type=F32)
        dkvg_ref[...] += jnp.sum(dkvn * ckv * rk, axis=0, keepdims=True)
        u = dkvn * kvg_ref[...]
        dckv_ref[...] = (rk * u - ckv * (rk * rk * rk) * jnp.mean(ckv * u, axis=-1, keepdims=True)).astype(BF16)

    cq_s, ckv_s, kr_s, tab_s = _mla_specs(tm)
    hd = pl.BlockSpec((tm, HP), lambda i: (i, 0))
    return _call(
        body, name=name, grid=(s // tm,),
        in_specs=[cq_s, ckv_s, kr_s, tab_s, tab_s, tab_s, _const((1, MLA_Q_RANK)), _const((1, MLA_KV_RANK)),
                  _const((MLA_Q_RANK, HP)), _const((MLA_KV_RANK, 2 * HP)), _const((1, LANES)), _const((1, LANES)),
                  hd, hd, hd],
        out_specs=[pl.BlockSpec((tm, MLA_Q_RANK), lambda i: (i, 0)), pl.BlockSpec((tm, MLA_KV_RANK), lambda i: (i, 0)),
                   pl.BlockSpec((tm, LANES), lambda i: (i, 0)), _const((MLA_Q_RANK, HP)), _const((MLA_KV_RANK, 2 * HP)),
                   _const((1, MLA_Q_RANK)), _const((1, MLA_KV_RANK)), _const((1, LANES)), _const((1, LANES))],
        out_shape=[jax.ShapeDtypeStruct((s, MLA_Q_RANK), BF16), jax.ShapeDtypeStruct((s, MLA_KV_RANK), BF16),
                   jax.ShapeDtypeStruct((s, LANES), BF16), jax.ShapeDtypeStruct((MLA_Q_RANK, HP), F32),
                   jax.ShapeDtypeStruct((MLA_KV_RANK, 2 * HP), F32), jax.ShapeDtypeStruct((1, MLA_Q_RANK), F32),
                   jax.ShapeDtypeStruct((1, MLA_KV_RANK), F32), jax.ShapeDtypeStruct((1, LANES), F32),
                   jax.ShapeDtypeStruct((1, LANES), F32)],
        scratch_shapes=[pltpu.VMEM((tm, HP), F32), pltpu.VMEM((tm, 2 * HP), F32)],
        args=(p, p, p, ct, s1, s2, qn_g, kvn_g, wuq, wkv, gq, gk, dq, dk, dv), semantics=("arbitrary",), side=side)


_ATT_SCALE = MLA_QK ** -0.5
ATT_BLOCK = 1024
_NEG = -1e30
_NT = (((1,), (1,)), ((), ()))
_TN = (((0,), (0,)), ((), ()))


def _tri_rows(step, n):
    i = step * 0
    for m in range(1, n):
        i = i + (step >= m * (m + 1) // 2).astype(jnp.int32)
    return i, step - i * (i + 1) // 2


def _tri_cols(step, n):
    j = step * 0
    for m in range(1, n):
        j = j + (step >= m * n - m * (m - 1) // 2).astype(jnp.int32)
    return j, j + step - (j * n - j * (j - 1) // 2)


def _diag_mask(t):
    return lax.broadcasted_iota(jnp.int32, (t, t), 0) <= lax.broadcasted_iota(jnp.int32, (t, t), 1)


def _attn_fwd(q, k, v, name, side=None):
    s = q.shape[0]
    t = _pick(s, ATT_BLOCK)
    n = s // t

    def body(q_ref, k_ref, v_ref, o_ref, lse_ref, m_s, l_s, acc):
        i, j = _tri_rows(pl.program_id(1), n)

        @pl.when(j == 0)
        def _():
            m_s[...] = jnp.full_like(m_s, _NEG)
            l_s[...] = jnp.zeros_like(l_s)
            acc[...] = jnp.zeros_like(acc)

        def step(diagonal):
            sc = lax.dot_general(k_ref[...], q_ref[...], _NT, preferred_element_type=F32)
            if diagonal:
                sc = jnp.where(_diag_mask(t), sc, _NEG)
            m_new = jnp.maximum(m_s[...], jnp.max(sc, axis=0, keepdims=True))
            alpha = jnp.exp(m_s[...] - m_new)
            pr = jnp.exp(sc - m_new)
            l_s[...] = alpha * l_s[...] + jnp.sum(pr, axis=0, keepdims=True)
            acc[...] = alpha * acc[...] + lax.dot_general(v_ref[...], pr.astype(BF16), _TN, preferred_element_type=F32)
            m_s[...] = m_new

        @pl.when(j < i)
        def _():
            step(False)

        @pl.when(j == i)
        def _():
            step(True)
            o_ref[...] = (acc[...] / l_s[...]).T
            lse_ref[...] = m_s[...] + jnp.log(l_s[...])

    qs = pl.BlockSpec((t, LANES), lambda h, p: (_tri_rows(p, n)[0], h))
    ks = pl.BlockSpec((t, LANES), lambda h, p: (_tri_rows(p, n)[1], h))
    return _call(
        body, name=name, grid=(MLA_HEADS, n * (n + 1) // 2), in_specs=[qs, ks, ks],
        out_specs=[qs, pl.BlockSpec((None, 1, t), lambda h, p: (h, 0, _tri_rows(p, n)[0]))],
        out_shape=[jax.ShapeDtypeStruct((s, HP), F32), jax.ShapeDtypeStruct((MLA_HEADS, 1, s), F32)],
        scratch_shapes=[pltpu.VMEM((1, t), F32), pltpu.VMEM((1, t), F32), pltpu.VMEM((LANES, t), F32)],
        args=(q, k, v), semantics=("parallel", "arbitrary"), side=side)


def _attn_bwd_dq(q, k, v, o, lse, do, name, side=None):
    s = q.shape[0]
    t = _pick(s, ATT_BLOCK)
    n = s // t

    def body(q_ref, k_ref, v_ref, o_ref, lse_ref, do_ref, dq_ref, dl_ref, acc, dl_s):
        i, j = _tri_rows(pl.program_id(1), n)

        @pl.when(j == 0)
        def _():
            acc[...] = jnp.zeros_like(acc)
            dl_s[...] = jnp.sum((do_ref[...] * o_ref[...]).T, axis=0, keepdims=True)

        def step(diagonal):
            sc = lax.dot_general(k_ref[...], q_ref[...], _NT, preferred_element_type=F32)
            if diagonal:
                sc = jnp.where(_diag_mask(t), sc, _NEG)
            pr = jnp.exp(sc - lse_ref[...])
            dp = lax.dot_general(v_ref[...], do_ref[...].astype(BF16), _NT, preferred_element_type=F32)
            ds = (pr * (dp - dl_s[...])).astype(BF16)
            acc[...] += lax.dot_general(k_ref[...], ds, _TN, preferred_element_type=F32)

        @pl.when(j < i)
        def _():
            step(False)

        @pl.when(j == i)
        def _():
            step(True)
            dq_ref[...] = (acc[...] * _ATT_SCALE).T
            dl_ref[...] = dl_s[...]

    qs = pl.BlockSpec((t, LANES), lambda h, p: (_tri_rows(p, n)[0], h))
    ks = pl.BlockSpec((t, LANES), lambda h, p: (_tri_rows(p, n)[1], h))
    ls = pl.BlockSpec((None, 1, t), lambda h, p: (h, 0, _tri_rows(p, n)[0]))
    return _call(
        body, name=name, grid=(MLA_HEADS, n * (n + 1) // 2), in_specs=[qs, ks, ks, qs, ls, qs], out_specs=[qs, ls],
        out_shape=[jax.ShapeDtypeStruct((s, HP), F32), jax.ShapeDtypeStruct((MLA_HEADS, 1, s), F32)],
        scratch_shapes=[pltpu.VMEM((LANES, t), F32), pltpu.VMEM((1, t), F32)],
        args=(q, k, v, o, lse, do), semantics=("parallel", "arbitrary"), side=side)


def _attn_bwd_dkv(q, k, v, lse, delta, do, name, side=None):
    s = q.shape[0]
    t = _pick(s, ATT_BLOCK)
    n = s // t

    def body(q_ref, k_ref, v_ref, lse_ref, dl_ref, do_ref, dk_ref, dv_ref, dk_acc, dv_acc):
        j, i = _tri_cols(pl.program_id(1), n)

        def step(diagonal):
            sc = lax.dot_general(k_ref[...], q_ref[...], _NT, preferred_element_type=F32)
            if diagonal:
                sc = jnp.where(_diag_mask(t), sc, _NEG)
            pr = jnp.exp(sc - lse_ref[...])
            do16 = do_ref[...].astype(BF16)
            dv_acc[...] += jnp.dot(pr.astype(BF16), do16, preferred_element_type=F32)
            dp = lax.dot_general(v_ref[...], do16, _NT, preferred_element_type=F32)
            ds = (pr * (dp - dl_ref[...])).astype(BF16)
            dk_acc[...] += jnp.dot(ds, q_ref[...], preferred_element_type=F32)

        @pl.when(i == j)
        def _():
            dk_acc[...] = jnp.zeros_like(dk_acc)
            dv_acc[...] = jnp.zeros_like(dv_acc)
            step(True)

        @pl.when(i > j)
        def _():
            step(False)

        @pl.when(i == n - 1)
        def _():
            dk_ref[...] = dk_acc[...]
            dv_ref[...] = dv_acc[...]

    qs = pl.BlockSpec((t, LANES), lambda h, p: (_tri_cols(p, n)[1], h))
    ks = pl.BlockSpec((t, LANES), lambda h, p: (_tri_cols(p, n)[0], h))
    ls = pl.BlockSpec((None, 1, t), lambda h, p: (h, 0, _tri_cols(p, n)[1]))
    return _call(
        body, name=name, grid=(MLA_HEADS, n * (n + 1) // 2), in_specs=[qs, ks, ks, ls, ls, qs], out_specs=[ks, ks],
        out_shape=[jax.ShapeDtypeStruct((s, HP), F32)] * 2,
        scratch_shapes=[pltpu.VMEM((t, LANES), F32), pltpu.VMEM((t, LANES), F32)],
        args=(q, k, v, lse, delta, do), semantics=("parallel", "arbitrary"), side=side)


XBC = HP + 2 * SSD_GROUPS * SSD_STATE
BCW = 2 * SSD_GROUPS * SSD_STATE


def _conv_fwd(p, col0, width, conv_w, conv_b, name):
    s = p.shape[0]
    c0, nblk = col0 // LANES, width // LANES

    def body(x_ref, w_ref, b_ref, o_ref, pad):
        pad[0:8, :] = jnp.zeros((8, LANES), F32)
        pad[8:s + 8, :] = x_ref[...].astype(F32)
        acc = jnp.broadcast_to(b_ref[...], (s, LANES))
        for t in range(SSD_CONV):
            acc = acc + pad[pl.ds(8 - (SSD_CONV - 1) + t, s), :] * w_ref[t:t + 1, :]
        o_ref[...] = acc * _sigmoid(acc)

    return pl.pallas_call(
        body, name=name, grid=(nblk,),
        in_specs=[pl.BlockSpec((s, LANES), lambda j: (0, c0 + j)), pl.BlockSpec((SSD_CONV, LANES), lambda j: (0, j)),
                  pl.BlockSpec((1, LANES), lambda j: (0, j))],
        out_specs=pl.BlockSpec((s, LANES), lambda j: (0, j)), out_shape=jax.ShapeDtypeStruct((s, width), F32),
        scratch_shapes=[pltpu.VMEM((s + 8, LANES), F32)], compiler_params=_params("parallel"))(p, conv_w, conv_b)


def _conv_bwd(p, col0, width, conv_w, conv_b, dact, name):
    s = p.shape[0]
    c0, nblk = col0 // LANES, width // LANES

    def body(x_ref, w_ref, b_ref, d_ref, dx_ref, dw_ref, db_ref, pad, padd):
        pad[0:8, :] = jnp.zeros((8, LANES), F32)
        pad[8:s + 8, :] = x_ref[...].astype(F32)
        acc = jnp.broadcast_to(b_ref[...], (s, LANES))
        for t in range(SSD_CONV):
            acc = acc + pad[pl.ds(8 - (SSD_CONV - 1) + t, s), :] * w_ref[t:t + 1, :]
        sg = _sigmoid(acc)
        dpre = d_ref[...] * (sg * (1.0 + acc * (1.0 - sg)))
        padd[0:s, :] = dpre
        padd[s:s + 8, :] = jnp.zeros((8, LANES), F32)
        dx = jnp.zeros((s, LANES), F32)
        for t in range(SSD_CONV):
            dx = dx + padd[pl.ds(SSD_CONV - 1 - t, s), :] * w_ref[t:t + 1, :]
            dw_ref[t:t + 1, :] = jnp.sum(dpre * pad[pl.ds(8 - (SSD_CONV - 1) + t, s), :], axis=0, keepdims=True)
        dx_ref[...] = dx.astype(BF16)
        db_ref[...] = jnp.sum(dpre, axis=0, keepdims=True)

    blk = pl.BlockSpec((s, LANES), lambda j: (0, j))
    return pl.pallas_call(
        body, name=name, grid=(nblk,),
        in_specs=[pl.BlockSpec((s, LANES), lambda j: (0, c0 + j)), pl.BlockSpec((SSD_CONV, LANES), lambda j: (0, j)),
                  pl.BlockSpec((1, LANES), lambda j: (0, j)), blk],
        out_specs=[blk, pl.BlockSpec((SSD_CONV, LANES), lambda j: (0, j)), pl.BlockSpec((1, LANES), lambda j: (0, j))],
        out_shape=[jax.ShapeDtypeStruct((s, width), BF16), jax.ShapeDtypeStruct((SSD_CONV, width), F32),
                   jax.ShapeDtypeStruct((1, width), F32)],
        scratch_shapes=[pltpu.VMEM((s + 8, LANES), F32), pltpu.VMEM((s + 8, LANES), F32)],
        compiler_params=_params("parallel"))(p, conv_w, conv_b, dact)


def _softplus(x):
    return jnp.maximum(x, 0.0) + jnp.log(1.0 + jnp.exp(-jnp.abs(x)))


def _dt_fwd(p, dt_bias, a_log, name):
    s = p.shape[0]
    tm = _pick(s, 512)

    def body(x_ref, b_ref, a_ref, dt_ref, da_ref):
        dtv = _softplus(x_ref[...] + b_ref[...])
        dav = dtv * (-jnp.exp(a_ref[...]))
        for h in range(SSD_HEADS):
            hs = slice(h * LANES, (h + 1) * LANES)
            dt_ref[:, hs] = jnp.broadcast_to(dtv[:, h:h + 1], (tm, LANES))
            da_ref[:, hs] = jnp.broadcast_to(dav[:, h:h + 1], (tm, LANES))

    out = pl.BlockSpec((tm, HP), lambda i: (i, 0))
    return pl.pallas_call(
        body, name=name, grid=(s // tm,),
        in_specs=[pl.BlockSpec((tm, LANES), lambda i: (i, T_DT // LANES)), _const((1, LANES)), _const((1, LANES))],
        out_specs=[out, out], out_shape=[jax.ShapeDtypeStruct((s, HP), F32)] * 2,
        compiler_params=_params("parallel"))(p, dt_bias, a_log)


def _dt_bwd(p, dt_bias, a_log, dda, ddtx, name):
    s = p.shape[0]
    tm = _pick(s, 512)

    def body(x_ref, b_ref, a_ref, dda_ref, ddtx_ref, dx_ref, db_ref, dal_ref):
        @pl.when(pl.program_id(0) == 0)
        def _():
            db_ref[...] = jnp.zeros_like(db_ref)
            dal_ref[...] = jnp.zeros_like(dal_ref)

        x = x_ref[...] + b_ref[...]
        dtv = _softplus(x)
        av = -jnp.exp(a_ref[...])
        lane = lax.broadcasted_iota(jnp.int32, (tm, LANES), 1)
        pa = jnp.zeros((tm, LANES), F32)
        px = jnp.zeros((tm, LANES), F32)
        for h in range(SSD_HEADS):
            pa = jnp.where(lane == h, dda_ref[:, h * LANES:(h + 1) * LANES], pa)
            px = jnp.where(lane == h, ddtx_ref[:, h * LANES:(h + 1) * LANES], px)
        draw = (pa * av + px) * _sigmoid(x)
        dx_ref[...] = draw.astype(BF16)
        db_ref[...] += jnp.sum(draw, axis=0, keepdims=True)
        dal_ref[...] += jnp.sum(pa * dtv, axis=0, keepdims=True) * av

    hd = pl.BlockSpec((tm, HP), lambda i: (i, 0))
    return pl.pallas_call(
        body, name=name, grid=(s // tm,),
        in_specs=[pl.BlockSpec((tm, LANES), lambda i: (i, T_DT // LANES)), _const((1, LANES)), _const((1, LANES)), hd, hd],
        out_specs=[pl.BlockSpec((tm, LANES), lambda i: (i, 0)), _const((1, LANES)), _const((1, LANES))],
        out_shape=[jax.ShapeDtypeStruct((s, LANES), BF16), jax.ShapeDtypeStruct((1, LANES), F32),
                   jax.ShapeDtypeStruct((1, LANES), F32)],
        compiler_params=_params("arbitrary"))(p, dt_bias, a_log, dda, ddtx)


def _cumsum_rows(x):
    row = lax.broadcasted_iota(jnp.int32, x.shape, 0)
    k = 1
    while k < x.shape[0]:
        x = x + jnp.where(row >= k, pltpu.roll(x, k, 0), 0.0)
        k *= 2
    return x


def _rev_cumsum_rows(x):
    n = x.shape[0]
    row = lax.broadcasted_iota(jnp.int32, x.shape, 0)
    k = 1
    while k < n:
        x = x + jnp.where(row < n - k, pltpu.roll(x, n - k, 0), 0.0)
        k *= 2
    return x


HPG = SSD_HEADS // SSD_GROUPS


def _chunk_decay(da):
    cs = _cumsum_rows(da)
    lm = jnp.exp(jnp.where(_tril_mask(), cs - cs.T, _NEG))
    return cs, lm, cs[CHUNK - 1:CHUNK, :]


def _scan_fwd(xs, bc, dtb, dab, name, side=None):
    s = xs.shape[0]
    nc = s // CHUNK

    def body(x_ref, b_ref, c_ref, dt_ref, da_ref, y_ref, sin_ref, state):
        @pl.when(pl.program_id(1) == 0)
        def _():
            state[...] = jnp.zeros_like(state)

        bv = b_ref[...]
        b16, c16 = bv.astype(BF16), c_ref[...].astype(BF16)
        g = lax.dot_general(c16, b16, _NT, preferred_element_type=F32)
        for hh in range(HPG):
            hs = slice(hh * LANES, (hh + 1) * LANES)
            st = state[hh]
            sin_ref[hh] = st
            cs, lm, cl = _chunk_decay(da_ref[:, hs])
            xd = (x_ref[:, hs] * dt_ref[:, hs]).astype(BF16)
            y = jnp.dot((g * lm).astype(BF16), xd, preferred_element_type=F32)
            y_ref[:, hs] = y + jnp.dot(c16, st.astype(BF16), preferred_element_type=F32) * jnp.exp(cs)
            bd = (bv * jnp.exp(cl - cs)).astype(BF16)
            state[hh] = jnp.exp(cl) * st + lax.dot_general(bd, xd, _TN, preferred_element_type=F32)

    gw = HPG * LANES
    hd = pl.BlockSpec((CHUNK, gw), lambda g, c: (c, g))
    return _call(
        body, name=name, grid=(SSD_GROUPS, nc),
        in_specs=[hd, pl.BlockSpec((CHUNK, LANES), lambda g, c: (c, g)),
                  pl.BlockSpec((CHUNK, LANES), lambda g, c: (c, SSD_GROUPS + g)), hd, hd],
        out_specs=[hd, pl.BlockSpec((HPG, None, SSD_STATE, LANES), lambda g, c: (g, c, 0, 0))],
        out_shape=[jax.ShapeDtypeStruct((s, HP), F32), jax.ShapeDtypeStruct((SSD_HEADS, nc, SSD_STATE, LANES), F32)],
        scratch_shapes=[pltpu.VMEM((HPG, SSD_STATE, LANES), F32)],
        args=(xs, bc, bc, dtb, dab), semantics=("parallel", "arbitrary"), side=side)


def _scan_bwd(xs, bc, dtb, dab, s_in, dy, d_vec, name):
    s = xs.shape[0]
    nc = s // CHUNK

    def body(x_ref, b_ref, c_ref, dt_ref, da_ref, sin_ref, dy_ref, dv_ref, dx_ref, db_ref, dc_ref, dda_ref, ddtx_ref, dstate):
        @pl.when(pl.program_id(1) == 0)
        def _():
            dstate[...] = jnp.zeros_like(dstate)

        bv = b_ref[...]
        b16, c16 = bv.astype(BF16), c_ref[...].astype(BF16)
        g = lax.dot_general(c16, b16, _NT, preferred_element_type=F32)
        row = lax.broadcasted_iota(jnp.int32, (CHUNK, 1), 0)
        dbm = jnp.zeros((CHUNK, SSD_STATE), F32)
        dcm = jnp.zeros((CHUNK, SSD_STATE), F32)
        for hh in range(HPG):
            hs = slice(hh * LANES, (hh + 1) * LANES)
            st, ds = sin_ref[hh], dstate[hh]
            st16, ds16 = st.astype(BF16), ds.astype(BF16)
            xv, dtv, dyv = x_ref[:, hs], dt_ref[:, hs], dy_ref[:, hs]
            cs, lm, cl = _chunk_decay(da_ref[:, hs])
            ecs, ecl = jnp.exp(cs), jnp.exp(cl)
            decay = jnp.exp(cl - cs)
            xd = (xv * dtv).astype(BF16)
            dy16 = dyv.astype(BF16)
            dye = (dyv * ecs).astype(BF16)
            yoff = jnp.dot(c16, st16, preferred_element_type=F32) * ecs
            dcs = jnp.sum(dyv * yoff, axis=-1, keepdims=True)
            dcm = dcm + lax.dot_general(dye, st16, _NT, preferred_element_type=F32)
            dstate[hh] = ecl * ds + lax.dot_general(c16, dye, _TN, preferred_element_type=F32)
            dcl = jnp.sum(jnp.sum(ds * st, axis=0, keepdims=True), axis=1, keepdims=True) * ecl[:, 0:1]
            bd32 = bv * decay
            qm = lax.dot_general(xd, ds16, _NT, preferred_element_type=F32)
            dbm = dbm + qm * decay
            w = jnp.sum(bd32 * qm, axis=-1, keepdims=True)
            dcs = dcs - w
            dcl = dcl + jnp.sum(w, axis=0, keepdims=True)
            dxd = jnp.dot(bd32.astype(BF16), ds16, preferred_element_type=F32)
            m16 = (g * lm).astype(BF16)
            dm = lax.dot_general(dy16, xd, _NT, preferred_element_type=F32)
            dxd = dxd + lax.dot_general(m16, dy16, _TN, preferred_element_type=F32)
            dg = dm * lm
            dg16 = dg.astype(BF16)
            tt = dg * g
            dcm = dcm + jnp.dot(dg16, b16, preferred_element_type=F32)
            dbm = dbm + lax.dot_general(dg16, c16, _TN, preferred_element_type=F32)
            dcs = dcs + jnp.sum(tt, axis=-1, keepdims=True) - jnp.sum(tt.T, axis=-1, keepdims=True)
            dcs = dcs + jnp.where(row == CHUNK - 1, dcl, 0.0)
            dda_ref[:, hs] = _rev_cumsum_rows(jnp.broadcast_to(dcs, (CHUNK, LANES)))
            ddtx_ref[:, hs] = jnp.broadcast_to(jnp.sum(dxd * xv, axis=-1, keepdims=True), (CHUNK, LANES))
            dx_ref[:, hs] = dxd * dtv + dyv * dv_ref[:, hs]
        db_ref[...] = dbm
        dc_ref[...] = dcm

    gw = HPG * LANES
    hd = pl.BlockSpec((CHUNK, gw), lambda g, c: (nc - 1 - c, g))
    gp = pl.BlockSpec((CHUNK, LANES), lambda g, c: (nc - 1 - c, g))
    return pl.pallas_call(
        body, name=name, grid=(SSD_GROUPS, nc),
        in_specs=[hd, gp, pl.BlockSpec((CHUNK, LANES), lambda g, c: (nc - 1 - c, SSD_GROUPS + g)), hd, hd,
                  pl.BlockSpec((HPG, None, SSD_STATE, LANES), lambda g, c: (g, nc - 1 - c, 0, 0)), hd,
                  pl.BlockSpec((1, gw), lambda g, c: (0, g))],
        out_specs=[hd, gp, gp, hd, hd],
        out_shape=[jax.ShapeDtypeStruct((s, HP), F32), jax.ShapeDtypeStruct((s, SSD_GROUPS * SSD_STATE), F32),
                   jax.ShapeDtypeStruct((s, SSD_GROUPS * SSD_STATE), F32), jax.ShapeDtypeStruct((s, HP), F32),
                   jax.ShapeDtypeStruct((s, HP), F32)],
        scratch_shapes=[pltpu.VMEM((HPG, SSD_STATE, LANES), F32)],
        compiler_params=_params("parallel", "arbitrary"))(xs, bc, bc, dtb, dab, s_in, dy, d_vec)


_GN = SSD_INNER // SSD_GROUPS
_GW = HP // SSD_GROUPS


def _ssd_post_fwd(y, xbc, p, d_vec, gain, name):
    s = y.shape[0]
    tm = _pick(s, 512)

    def body(y_ref, x_ref, z_ref, d_ref, g_ref, o_ref):
        z = z_ref[...].astype(F32)
        y2 = (y_ref[...] + x_ref[...] * d_ref[...]) * (z * _sigmoid(z))
        for g in range(SSD_GROUPS):
            gs = slice(g * _GW, (g + 1) * _GW)
            yg = y2[:, gs]
            r = lax.rsqrt(jnp.sum(yg * yg, axis=-1, keepdims=True) * (1.0 / _GN) + EPS)
            o_ref[:, gs] = (yg * r * g_ref[:, gs]).astype(BF16)

    hd = pl.BlockSpec((tm, HP), lambda i: (i, 0))
    return pl.pallas_call(
        body, name=name, grid=(s // tm,),
        in_specs=[hd, hd, pl.BlockSpec((tm, HP), lambda i: (i, C_Z // HP)), _const((1, HP)), _const((1, HP))],
        out_specs=hd, out_shape=jax.ShapeDtypeStruct((s, HP), BF16), compiler_params=_params("parallel"))(y, xbc, p, d_vec, gain)


def _ssd_post_bwd(y, xbc, p, d_vec, gain, dyn, name):
    s = y.shape[0]
    tm = _pick(s, 512)

    def body(y_ref, x_ref, z_ref, d_ref, g_ref, dn_ref, dy_ref, dz_ref, dg_ref, dd_ref):
        @pl.when(pl.program_id(0) == 0)
        def _():
            dg_ref[...] = jnp.zeros_like(dg_ref)
            dd_ref[...] = jnp.zeros_like(dd_ref)

        z, xv = z_ref[...].astype(F32), x_ref[...]
        sg = _sigmoid(z)
        sz = z * sg
        yt = y_ref[...] + xv * d_ref[...]
        y2 = yt * sz
        for g in range(SSD_GROUPS):
            gs = slice(g * _GW, (g + 1) * _GW)
            yg, dn = y2[:, gs], dn_ref[:, gs]
            r = lax.rsqrt(jnp.sum(yg * yg, axis=-1, keepdims=True) * (1.0 / _GN) + EPS)
            u = dn * g_ref[:, gs]
            dy2 = r * u - yg * (r * r * r) * (jnp.sum(yg * u, axis=-1, keepdims=True) * (1.0 / _GN))
            dg_ref[:, gs] += jnp.sum(dn * yg * r, axis=0, keepdims=True)
            dyt = dy2 * sz[:, gs]
            dy_ref[:, gs] = dyt
            dz_ref[:, gs] = (dy2 * yt[:, gs] * (sg[:, gs] * (1.0 + z[:, gs] * (1.0 - sg[:, gs])))).astype(BF16)
            dd_ref[:, gs] += jnp.sum(dyt * xv[:, gs], axis=0, keepdims=True)

    hd = pl.BlockSpec((tm, HP), lambda i: (i, 0))
    return pl.pallas_call(
        body, name=name, grid=(s // tm,),
        in_specs=[hd, hd, pl.BlockSpec((tm, HP), lambda i: (i, C_Z // HP)), _const((1, HP)), _const((1, HP)), hd],
        out_specs=[hd, hd, _const((1, HP)), _const((1, HP))],
        out_shape=[jax.ShapeDtypeStruct((s, HP), F32), jax.ShapeDtypeStruct((s, HP), BF16),
                   jax.ShapeDtypeStruct((1, HP), F32), jax.ShapeDtypeStruct((1, HP), F32)],
        compiler_params=_params("arbitrary"))(y, xbc, p, d_vec, gain, dyn)


def _merge_fwd(p, ya, o, yc, wb0, wb1, wb2, w_out, x, name, side=None):
    s = p.shape[0]
    tm = _pick(s, 512)

    def body(g_ref, ya_ref, o_ref, yc_ref, w0_ref, w1_ref, w2_ref, wo_ref, x_ref, mg_ref, y_ref):
        acc = jnp.zeros((tm, D_MODEL), F32)
        for i, (b_ref, w_ref) in enumerate(((ya_ref, w0_ref), (o_ref, w1_ref), (yc_ref, w2_ref))):
            t = jnp.dot(b_ref[...].astype(BF16), w_ref[...], preferred_element_type=F32)
            acc = acc + _sigmoid(g_ref[:, i * D_MODEL:(i + 1) * D_MODEL].astype(F32)) * t
        mg = acc.astype(BF16)
        mg_ref[...] = mg
        y_ref[...] = x_ref[...] + jnp.dot(mg, wo_ref[...], preferred_element_type=F32)

    row = pl.BlockSpec((tm, D_MODEL), lambda i: (i, 0))
    return _call(
        body, name=name, grid=(s // tm,),
        in_specs=[pl.BlockSpec((tm, 3 * D_MODEL), lambda i: (i, C_G // (3 * D_MODEL))),
                  pl.BlockSpec((tm, GM_WIDTH), lambda i: (i, 0)), row, row,
                  _resident((GM_WIDTH, D_MODEL)), _resident((HP, D_MODEL)), _resident((HP, D_MODEL)),
                  _resident((D_MODEL, D_MODEL)), row],
        out_specs=[row, row],
        out_shape=[jax.ShapeDtypeStruct((s, D_MODEL), BF16), jax.ShapeDtypeStruct((s, D_MODEL), F32)],
        scratch_shapes=[], args=(p, ya, o, yc, wb0, wb1, wb2, w_out, x), semantics=("parallel",), side=side)


def _merge_bwd(p, ya, o, yc, wb0, wb1, wb2, w_out, dy, name):
    s = p.shape[0]
    tm = _pick(s, 512)

    def body(g_ref, ya_ref, o_ref, yc_ref, w0_ref, w1_ref, w2_ref, wo_ref, dy_ref,
             d0_ref, d1_ref, d2_ref, dg_ref, dya_ref, do_ref, dyc_ref):
        dm = lax.dot_general(dy_ref[...].astype(BF16), wo_ref[...], _NT, preferred_element_type=F32)
        for i, (b_ref, w_ref, d_ref, db_ref) in enumerate(((ya_ref, w0_ref, d0_ref, dya_ref), (o_ref, w1_ref, d1_ref, do_ref),
                                                            (yc_ref, w2_ref, d2_ref, dyc_ref))):
            cs = slice(i * D_MODEL, (i + 1) * D_MODEL)
            t = jnp.dot(b_ref[...].astype(BF16), w_ref[...], preferred_element_type=F32)
            sg = _sigmoid(g_ref[:, cs].astype(F32))
            dt16 = (dm * sg).astype(BF16)
            d_ref[...] = dt16
            dg_ref[:, cs] = (dm * t * sg * (1.0 - sg)).astype(BF16)
            db_ref[...] = lax.dot_general(dt16, w_ref[...], _NT, preferred_element_type=F32)

    row = pl.BlockSpec((tm, D_MODEL), lambda i: (i, 0))
    nar = pl.BlockSpec((tm, GM_WIDTH), lambda i: (i, 0))
    wide = pl.BlockSpec((tm, 3 * D_MODEL), lambda i: (i, 0))
    return pl.pallas_call(
        body, name=name, grid=(s // tm,),
        in_specs=[pl.BlockSpec((tm, 3 * D_MODEL), lambda i: (i, C_G // (3 * D_MODEL))), nar, row, row,
                  _resident((GM_WIDTH, D_MODEL)), _resident((HP, D_MODEL)), _resident((HP, D_MODEL)),
                  _resident((D_MODEL, D_MODEL)), row],
        out_specs=[row, row, row, wide, nar, row, row],
        out_shape=[jax.ShapeDtypeStruct((s, D_MODEL), BF16)] * 3 + [jax.ShapeDtypeStruct((s, 3 * D_MODEL), BF16),
                   jax.ShapeDtypeStruct((s, GM_WIDTH), F32), jax.ShapeDtypeStruct((s, D_MODEL), F32),
                   jax.ShapeDtypeStruct((s, D_MODEL), F32)],
        compiler_params=_params("parallel"))(p, ya, o, yc, wb0, wb1, wb2, w_out, dy)


def _loss_head(y, target, name):
    s, d = y.shape
    tm = _pick(s, 512)

    def body(y_ref, t_ref, dy_ref, sq_ref):
        @pl.when(pl.program_id(0) == 0)
        def _():
            sq_ref[...] = jnp.zeros_like(sq_ref)

        e = y_ref[...] - t_ref[...]
        dy_ref[...] = e * (1.0 / d)
        sq_ref[...] += jnp.sum(e * e, axis=0, keepdims=True)

    row = pl.BlockSpec((tm, d), lambda i: (i, 0))
    return pl.pallas_call(
        body, name=name, grid=(s // tm,), in_specs=[row, row], out_specs=[row, _const((1, d))],
        out_shape=[jax.ShapeDtypeStruct((s, d), F32), jax.ShapeDtypeStruct((1, d), F32)],
        compiler_params=_params("arbitrary"))(y, target)


def _adamw(w, g, m, v, name):
    rows, cols = w.shape
    tr = rows
    for cand in (512, 256, 128, 64, 32, 16, 8):
        if rows % cand == 0 and cand * cols * 4 <= 3 * 1024 * 1024:
            tr = cand
            break

    def body(w_ref, g_ref, m_ref, v_ref, d_ref, nm_ref, nv_ref):
        d_ref[...], nm_ref[...], nv_ref[...] = _adam_update(w_ref[...], g_ref[...], m_ref[...], v_ref[...])

    blk = pl.BlockSpec((tr, cols), lambda i: (i, 0))
    return pl.pallas_call(
        body, name=name, grid=(rows // tr,), in_specs=[blk] * 4, out_specs=[blk] * 3,
        out_shape=[jax.ShapeDtypeStruct((rows, cols), F32)] * 3, compiler_params=_params("parallel"))(w, g, m, v)


def _adam_update(w, g, m, v):
    nm = ADAM_B1 * m + (1.0 - ADAM_B1) * g
    nv = ADAM_B2 * v + (1.0 - ADAM_B2) * (g * g)
    c1 = 1.0 - ADAM_B1 ** ADAM_STEP
    c2 = 1.0 - ADAM_B2 ** ADAM_STEP
    return -ADAM_LR * ((nm / c1) / (jnp.sqrt(nv / c2) + ADAM_EPS) + ADAM_WD * w), nm, nv


def _adamw_sharded(w, m, v, mine, theirs, name):
    depth, rows, cols = w.shape
    tr = _row_tile(rows // 2, cols, 1024 * 1024)
    nb = rows // 2 // tr

    def body(w_ref, m_ref, v_ref, a_ref, b_ref, g_ref, d_ref, nm_ref, nv_ref):
        c = lax.axis_index("c")
        g = jnp.where(pl.program_id(1) // nb == c, a_ref[...], b_ref[...])
        g_ref[...] = g
        d_ref[...], nm_ref[...], nv_ref[...] = _adam_update(w_ref[...], g, m_ref[...], v_ref[...])

    blk = pl.BlockSpec((None, tr, cols), lambda l, i: (l, i, 0))
    half = pl.BlockSpec((None, tr, cols), lambda l, i: (l, i % nb, 0))
    return pl.pallas_call(
        body, name=name, grid=(depth, rows // tr), in_specs=[blk, blk, blk, half, half], out_specs=[blk] * 4,
        out_shape=[jax.ShapeDtypeStruct((depth, rows, cols), F32)] * 4,
        compiler_params=_params("parallel", "parallel"))(w, m, v, mine, theirs)


ANY = pl.BlockSpec(memory_space=pl.ANY)


def _me():
    return lax.axis_index("x"), lax.axis_index("y"), lax.axis_index("c")


def _other_chips(x, y):
    return [(1 - x, y), (x, 1 - y), (1 - x, 1 - y)]


def _chip_index(cx, cy):
    return 2 * cx + cy


class _Exchange:
    def __init__(self, ins, out_shapes, n_sems, start, finish):
        self.ins, self.out_shapes, self.n_sems, self.start, self.finish = list(ins), list(out_shapes), n_sems, start, finish


def _sem_scratch(ex):
    return [pltpu.SemaphoreType.DMA((ex.n_sems,)), pltpu.SemaphoreType.DMA((ex.n_sems,))]


def _run_exchange(ex, name):
    n_in, n_out = len(ex.ins), len(ex.out_shapes)

    def body(*refs):
        in_refs, out_refs, (send, recv) = refs[:n_in], refs[n_in:n_in + n_out], refs[n_in + n_out:]
        ex.start(in_refs, out_refs, send, recv)
        ex.finish(in_refs, out_refs, send, recv)

    return pl.pallas_call(body, name=name, in_specs=[ANY] * n_in, out_specs=[ANY] * n_out, out_shape=ex.out_shapes,
                          scratch_shapes=_sem_scratch(ex))(*ex.ins)


def _call(body, *, name, grid, in_specs, out_specs, out_shape, scratch_shapes, args, semantics, side=None):
    if side is None:
        return pl.pallas_call(body, name=name, grid=grid, in_specs=in_specs, out_specs=out_specs, out_shape=out_shape,
                              scratch_shapes=scratch_shapes, compiler_params=_params(*semantics))(*args), []
    n_in, n_out, n_sc = len(in_specs), len(out_specs), len(scratch_shapes)
    s_in, s_out = len(side.ins), len(side.out_shapes)

    def hosted(*refs):
        pos = 0
        parts = []
        for size in (n_in, s_in, n_out, s_out, n_sc, 2):
            parts.append(refs[pos:pos + size])
            pos += size
        ins, sins, outs, souts, scratch, (send, recv) = parts
        ids = [pl.program_id(a) for a in range(len(grid))]
        first = functools.reduce(jnp.logical_and, [i == 0 for i in ids])
        last = functools.reduce(jnp.logical_and, [i == g - 1 for i, g in zip(ids, grid)])

        @pl.when(first)
        def _():
            side.start(sins, souts, send, recv)

        body(*ins, *outs, *scratch)

        @pl.when(last)
        def _():
            side.finish(sins, souts, send, recv)

    res = pl.pallas_call(
        hosted, name=name, grid=grid, in_specs=list(in_specs) + [ANY] * s_in, out_specs=list(out_specs) + [ANY] * s_out,
        out_shape=list(out_shape) + side.out_shapes, scratch_shapes=list(scratch_shapes) + _sem_scratch(side),
        compiler_params=_params(*["arbitrary"] * len(grid)))(*args, *side.ins)
    return res[:n_out], res[n_out:]


def _half(ref_rows, c):
    return pl.ds(c * (ref_rows // 2), ref_rows // 2)


def _gather_exchange(shards, layer):
    n = len(shards)
    rows = [a.shape[1] for a in shards]

    def copy(in_refs, out_refs, send, recv, t, k, chip, hc, to, from_input=False):
        dst = out_refs[t].at[chip, _half(rows[t], hc)]
        src = in_refs[t].at[layer, _half(rows[t], hc)] if from_input else dst
        return pltpu.make_async_remote_copy(src_ref=src, dst_ref=dst, send_sem=send.at[7 * t + k], recv_sem=recv.at[7 * t + k],
                                            device_id=to, device_id_type=MESH)

    def own(in_refs, out_refs, send, recv, t):
        x, y, c = _me()
        return pltpu.make_async_remote_copy(src_ref=in_refs[t].at[layer], dst_ref=out_refs[t].at[_chip_index(x, y)],
                                            send_sem=send.at[7 * t + 6], recv_sem=recv.at[7 * t + 6],
                                            device_id=(x, y, 1 - c), device_id_type=MESH)

    def start(in_refs, out_refs, send, recv):
        x, y, c = _me()
        for j, chip in enumerate(_other_chips(x, y)):
            for t in range(n):
                copy(in_refs, out_refs, send, recv, t, j, _chip_index(x, y), c, (*chip, c), from_input=True).start()
        for t in range(n):
            own(in_refs, out_refs, send, recv, t).start()

    def finish(in_refs, out_refs, send, recv):
        x, y, c = _me()
        chips = _other_chips(x, y)
        passed = []
        for t in range(n):
            own(in_refs, out_refs, send, recv, t).wait()
        for j, chip in enumerate(chips):
            for t in range(n):
                copy(in_refs, out_refs, send, recv, t, j, _chip_index(*chip), c, (x, y, c)).wait_recv()
                cp = copy(in_refs, out_refs, send, recv, t, 3 + j, _chip_index(*chip), c, (x, y, 1 - c))
                cp.start()
                passed.append(cp)
        for j, chip in enumerate(chips):
            for t in range(n):
                copy(in_refs, out_refs, send, recv, t, 3 + j, _chip_index(*chip), 1 - c, (x, y, c)).wait_recv()
                copy(in_refs, out_refs, send, recv, t, j, _chip_index(x, y), c, (*chip, c), from_input=True).wait_send()
        for cp in passed:
            cp.wait_send()

    return _Exchange(shards, [jax.ShapeDtypeStruct((N_CHIPS,) + a.shape[1:], a.dtype) for a in shards], 7 * n, start, finish)


def _pair_exchange(gs):
    n = len(gs)
    rows = [a.shape[1] for a in gs]

    def copies(in_refs, out_refs, send, recv):
        x, y, c = _me()
        return [pltpu.make_async_remote_copy(src_ref=in_refs[t].at[:, _half(rows[t], 1 - c)], dst_ref=out_refs[t],
                                             send_sem=send.at[t], recv_sem=recv.at[t], device_id=(x, y, 1 - c),
                                             device_id_type=MESH) for t in range(n)]

    def start(*refs):
        for cp in copies(*refs):
            cp.start()

    def finish(*refs):
        for cp in copies(*refs):
            cp.wait()

    return _Exchange(gs, [jax.ShapeDtypeStruct((N_CHIPS, a.shape[1] // 2, a.shape[2]), a.dtype) for a in gs], n, start, finish)


def _row_tile(rows, cols, budget=2 * 1024 * 1024):
    best = None
    for t in range(8, rows + 1, 8):
        if rows % t == 0 and t * cols * 4 <= budget:
            best = t
    return best or rows


def _pair_add(g, got, name):
    _, rows, cols = g.shape
    tr = _row_tile(rows // 2, cols)
    nb = rows // 2 // tr

    def body(lo_ref, hi_ref, r_ref, o16_ref, own_ref):
        x, y, c = _me()
        tot = jnp.where(c == 0, lo_ref[...], hi_ref[...]) + r_ref[...]
        o16_ref[...] = tot.astype(BF16)

        @pl.when(pl.program_id(1) == _chip_index(x, y))
        def _():
            own_ref[...] = tot

    blk = (None, tr, cols)
    return pl.pallas_call(
        body, name=name, grid=(nb, N_CHIPS),
        in_specs=[pl.BlockSpec(blk, lambda i, k: (k, i, 0)), pl.BlockSpec(blk, lambda i, k: (k, i + nb, 0)),
                  pl.BlockSpec(blk, lambda i, k: (k, i, 0))],
        out_specs=[pl.BlockSpec(blk, lambda i, k: (k, i, 0)), pl.BlockSpec((tr, cols), lambda i, k: (i, 0))],
        out_shape=[jax.ShapeDtypeStruct((N_CHIPS, rows // 2, cols), BF16), jax.ShapeDtypeStruct((rows // 2, cols), F32)],
        compiler_params=_params("parallel", "arbitrary"))(g, g, got)


def _chip_exchange(parts):
    n = len(parts)

    def copies(in_refs, out_refs, send, recv):
        x, y, c = _me()
        return [pltpu.make_async_remote_copy(src_ref=in_refs[t].at[_chip_index(*chip)], dst_ref=out_refs[t].at[j],
                                             send_sem=send.at[3 * t + j], recv_sem=recv.at[3 * t + j],
                                             device_id=(*chip, c), device_id_type=MESH)
                for j, chip in enumerate(_other_chips(x, y)) for t in range(n)]

    def start(*refs):
        for cp in copies(*refs):
            cp.start()

    def finish(*refs):
        for cp in copies(*refs):
            cp.wait()

    return _Exchange(parts, [jax.ShapeDtypeStruct((3,) + a.shape[1:], a.dtype) for a in parts], 3 * n, start, finish)


def _chip_add(own, got, name):
    rows, cols = own.shape
    tr = _row_tile(rows, cols, 1024 * 1024)

    def body(own_ref, got_ref, o_ref):
        acc = own_ref[...]
        for j in range(3):
            acc = acc + got_ref[j].astype(F32)
        o_ref[...] = acc

    return pl.pallas_call(
        body, name=name, grid=(rows // tr,),
        in_specs=[pl.BlockSpec((tr, cols), lambda i: (i, 0)), pl.BlockSpec((3, tr, cols), lambda i: (0, i, 0))],
        out_specs=pl.BlockSpec((tr, cols), lambda i: (i, 0)),
        out_shape=jax.ShapeDtypeStruct((rows, cols), F32), compiler_params=_params("parallel"))(own, got)


def _pair_share(halves):
    n = len(halves)

    def copies(in_refs, out_refs, send, recv):
        x, y, c = _me()
        return [pltpu.make_async_remote_copy(src_ref=in_refs[t], dst_ref=out_refs[t], send_sem=send.at[t],
                                             recv_sem=recv.at[t], device_id=(x, y, 1 - c), device_id_type=MESH)
                for t in range(n)]

    def start(*refs):
        for cp in copies(*refs):
            cp.start()

    def finish(*refs):
        for cp in copies(*refs):
            cp.wait()

    return _Exchange(halves, [jax.ShapeDtypeStruct(a.shape, a.dtype) for a in halves], n, start, finish)


N_DEV = 8


def _all_exchange(v):
    r, cols = v.shape

    def body(in_ref, out_ref, send_sems, recv_sems, local_sem):
        x, y, c = _me()
        me = 4 * x + 2 * y + c
        local = pltpu.make_async_copy(in_ref, out_ref.at[me], local_sem)
        local.start()
        flip = lambda v, f: 1 - v if f else v
        peers = [(flip(x, fx), flip(y, fy), flip(c, fc)) for fx in (0, 1) for fy in (0, 1) for fc in (0, 1)][1:]
        cps = [pltpu.make_async_remote_copy(src_ref=in_ref, dst_ref=out_ref.at[me], send_sem=send_sems.at[j],
                                            recv_sem=recv_sems.at[j], device_id=peer, device_id_type=MESH)
               for j, peer in enumerate(peers)]
        for cp in cps:
            cp.start()
        for j, (px, py, pc) in enumerate(peers):
            pltpu.make_async_remote_copy(src_ref=in_ref, dst_ref=out_ref.at[4 * px + 2 * py + pc], send_sem=send_sems.at[j],
                                         recv_sem=recv_sems.at[j], device_id=(px, py, pc), device_id_type=MESH).wait_recv()
        for cp in cps:
            cp.wait_send()
        local.wait()

    return pl.pallas_call(
        body, name="all_exchange", in_specs=[ANY], out_specs=ANY, out_shape=jax.ShapeDtypeStruct((N_DEV, r, cols), v.dtype),
        scratch_shapes=[pltpu.SemaphoreType.DMA((7,)), pltpu.SemaphoreType.DMA((7,)), pltpu.SemaphoreType.DMA(())])(v)


def _sum_slots(a, name):
    n, r, cols = a.shape
    tr = _pick(r, 512) if r % 8 == 0 else r
    for cand in (512, 256, 128, 64, 32, 16, 8):
        if r % cand == 0:
            tr = cand
            break

    def body(a_ref, o_ref):
        acc = a_ref[0]
        for k in range(1, n):
            acc = acc + a_ref[k]
        o_ref[...] = acc

    return pl.pallas_call(
        body, name=name, grid=(r // tr,), in_specs=[pl.BlockSpec((n, tr, cols), lambda i: (0, i, 0))],
        out_specs=pl.BlockSpec((tr, cols), lambda i: (i, 0)), out_shape=jax.ShapeDtypeStruct((r, cols), F32),
        compiler_params=_params("parallel"))(a)


def _join(name, stacked):
    ax = SHARDED[name][1]
    return jnp.concatenate([stacked[k] for k in range(N_CHIPS)], axis=ax)


def _split(name, full):
    ax = SHARDED[name][1]
    return jnp.stack(jnp.split(full, N_CHIPS, axis=ax))


def _heads_pad(a, real, axis):
    shp = a.shape
    a = a.reshape(shp[:axis] + (MLA_HEADS, real) + shp[axis + 1:])
    pad = [(0, 0)] * a.ndim
    pad[axis + 1] = (0, LANES - real)
    a = jnp.pad(a, pad)
    return a.reshape(shp[:axis] + (HP,) + shp[axis + 1:])


def _heads_unpad(a, real, axis):
    shp = a.shape
    a = a.reshape(shp[:axis] + (MLA_HEADS, LANES) + shp[axis + 1:])
    a = lax.slice_in_dim(a, 0, real, axis=axis + 1)
    return a.reshape(shp[:axis] + (MLA_HEADS * real,) + shp[axis + 1:])


def _lane_place(a, start):
    n = a.shape[-1]
    pad = [(0, 0)] * (a.ndim - 1) + [(start, LANES - start - n)]
    return jnp.pad(a, pad)


_O_UV, _O_CQ, _O_CKV, _O_KR, _O_Z, _O_XBC, _O_DT, _O_G = 0, 1024, 1408, 1664, 1696, 2208, 3232, 3240


def _w_in_pad(w):
    sl = lambda a, b: w[:, a:b]
    xs = _heads_pad(sl(_O_XBC, _O_XBC + SSD_INNER), SSD_HEAD_DIM, 1)
    bc = sl(_O_XBC + SSD_INNER, _O_DT)
    main = jnp.concatenate([sl(_O_UV, _O_CQ), _heads_pad(sl(_O_Z, _O_XBC), SSD_HEAD_DIM, 1), xs, sl(_O_G, IN_COLS)], axis=1)
    tail = jnp.concatenate([bc, sl(_O_CKV, _O_KR), sl(_O_CQ, _O_CKV), _lane_place(sl(_O_KR, _O_Z), MLA_NOPE),
                            _lane_place(sl(_O_DT, _O_G), 0), jnp.zeros((w.shape[0], PW_TAIL - T_DT - LANES), w.dtype)], axis=1)
    return main, tail


def _w_in_unpad(gm, gt):
    m = lambda a, n: gm[:, a:a + n]
    t = lambda a, n: gt[:, a:a + n]
    parts = [m(C_UV, 1024), t(T_CQ, MLA_Q_RANK), t(T_CKV, MLA_KV_RANK), t(T_KR + MLA_NOPE, MLA_ROPE),
             _heads_unpad(m(C_Z, HP), SSD_HEAD_DIM, 1), _heads_unpad(m(C_XS, HP), SSD_HEAD_DIM, 1), t(T_BC, BCW),
             t(T_DT, SSD_HEADS), m(C_G, 3 * D_MODEL)]
    return jnp.concatenate(parts, axis=1)


def _xbc_pad(a):
    return jnp.concatenate([_heads_pad(a[..., :SSD_INNER], SSD_HEAD_DIM, a.ndim - 1), a[..., SSD_INNER:]], axis=-1)


def _xbc_unpad(a):
    return jnp.concatenate([_heads_unpad(a[..., :HP], SSD_HEAD_DIM, a.ndim - 1), a[..., HP:]], axis=-1)


def _rope_tables(positions):
    inv_freq = 1.0 / (ROPE_THETA ** (jnp.arange(0, MLA_ROPE, 2, dtype=F32) / MLA_ROPE))
    ang = positions.astype(F32)[:, None] * inv_freq
    cos, sin = jnp.cos(ang), jnp.sin(ang)
    s = positions.shape[0]
    half = MLA_ROPE // 2
    z = lambda n: jnp.zeros((s, n), F32)
    ct = jnp.concatenate([jnp.ones((s, MLA_NOPE), F32), cos, cos, z(LANES - MLA_QK)], axis=1)
    s1 = jnp.concatenate([z(MLA_NOPE), -sin, z(half), z(LANES - MLA_QK)], axis=1)
    s2 = jnp.concatenate([z(MLA_NOPE), z(half), sin, z(LANES - MLA_QK)], axis=1)
    return ct, s1, s2


def _layer_weights(full, small, l, part):
    w = {}
    row = lambda n: small[n][l][None, :]
    stacked = lambda g: g.reshape((N_CHIPS * g.shape[1], g.shape[2]))
    if part in ('ffn1', 'ffn2'):
        w[part + '_w_in'] = full[part + '_w_in']
        w[part + '_w_out'] = stacked(full[part + '_w_out'])
        w[part + '_norm'] = row(part + '_norm')
        return w
    w['w_out'] = stacked(full['w_out'])
    fl = {n: _join(n, full[n]) for n in ('w_in', 'mla_w_uq', 'mla_w_ukv', 'w_branch', 'ssd_conv_w')}
    w['w_in_main'], w['w_in_tail'] = _w_in_pad(fl['w_in'])
    w['wuq'] = _heads_pad(fl['mla_w_uq'], MLA_QK, 1)
    ukv = fl['mla_w_ukv'].reshape(MLA_KV_RANK, MLA_HEADS, MLA_NOPE + MLA_V)
    zero = jnp.zeros((MLA_KV_RANK, MLA_HEADS, LANES - MLA_NOPE), ukv.dtype)
    wk = jnp.concatenate([ukv[:, :, :MLA_NOPE], zero], axis=2).reshape(MLA_KV_RANK, HP)
    wv = jnp.concatenate([ukv[:, :, MLA_NOPE:], zero], axis=2).reshape(MLA_KV_RANK, HP)
    w['wkv'] = jnp.concatenate([wk, wv], axis=1)
    wb = fl['w_branch']
    w['wb0'] = wb[0]
    w['wb1'] = _heads_pad(wb[1], MLA_V, 0)
    w['wb2'] = _heads_pad(wb[2], SSD_HEAD_DIM, 0)
    w['conv_w'] = _xbc_pad(fl['ssd_conv_w'].astype(F32))
    for n in ('mix_norm', 'gm_v_norm', 'mla_q_norm', 'mla_kv_norm'):
        w[n] = row(n)
    w['gm_w_s'] = small['gm_w_s'][l]
    w['gm_b_full'] = jnp.broadcast_to(small['gm_b_s'][l][:, :, None], (GM_GROUPS, CHUNK, LANES))
    w['gq'] = _lane_place(row('mla_q_gain'), 0)
    w['gk'] = _lane_place(row('mla_k_gain'), 0)
    w['conv_b'] = _xbc_pad(row('ssd_conv_b'))
    w['dt_bias'] = _lane_place(row('ssd_dt_bias'), 0)
    w['a_log'] = _lane_place(row('ssd_a_log'), 0)
    w['d_vec'] = jnp.repeat(small['ssd_d'][l], LANES)[None, :]
    w['ssd_norm'] = _heads_pad(row('ssd_norm'), SSD_HEAD_DIM, 1)
    return w


_MIXER_SMALL = ['mla_w_uq', 'mla_w_ukv', 'ssd_conv_w', 'w_branch', 'w_out']
_MIXER_SMALL_G = [n for n in _MIXER_SMALL if n != 'ssd_conv_w']
GATHER_HOSTS = {'attn': ['ffn1_w_in', 'ffn2_w_in'], 'scan': ['ffn1_w_out', 'ffn2_w_out'], 'merge': _MIXER_SMALL, 'ffn2_in': ['w_in']}
FIRST_NOW = ['ffn1_w_in', 'ffn1_w_out']
FIRST_HOSTS = {'ffn1_in': ['w_in'], 'ffn1_out': _MIXER_SMALL, 'proj': ['ffn2_w_in', 'ffn2_w_out']}
PAIR_HOSTS = {'ffn2_dwout': ['ffn1_w_in', 'ffn2_w_in'], 'ffn2_dwin': ['ffn1_w_out', 'w_in', 'ffn2_w_out'] + _MIXER_SMALL_G}
REDUCE_HOSTS = {'dattn_q': ['ffn1_w_out', 'w_in', 'ffn2_w_out'], 'dattn_kv': ['ffn1_w_in', 'ffn2_w_in'], 'dmla_pre': _MIXER_SMALL_G}
LAST_EARLY = ['w_in', 'ffn2_w_in', 'ffn2_w_out'] + _MIXER_SMALL_G
LAST_HOSTS = {'ffn1_dact': ['w_in'], 'ffn1_dwin': ['ffn2_w_in'], 'ffn1_dx': ['ffn2_w_out'] + _MIXER_SMALL_G}
LAST_LATE = ['ffn1_w_in', 'ffn1_w_out']


def _ffn_fwd(x, norm, w4, w_out, tag, sides=None):
    sides = sides or {}
    carried = {}
    (h, gate, up, act), carried[f"{tag}_in"] = _ffn_in(x, norm, w4, f"{tag}_in", sides.get(f"{tag}_in"))
    y, carried[f"{tag}_out"] = _ffn_out(act, w_out, x, f"{tag}_out", sides.get(f"{tag}_out"))
    return y, (x, h, gate, up, act), carried


def _ffn_bwd(dy, saved, norm, w4, w_out, tag, sides=None, after_dwout=None):
    sides = dict(sides or {})
    carried = {}
    x, h, gate, up, act = saved
    dw_out, carried[f"{tag}_dwout"] = _ffn_dwout(act, dy, f"{tag}_dwout", sides.get(f"{tag}_dwout"))
    if after_dwout is not None:
        sides.update(after_dwout(carried[f"{tag}_dwout"]))
    da, carried[f"{tag}_dact"] = _ffn_dact(dy, w_out, gate, up, f"{tag}_dact", sides.get(f"{tag}_dact"))
    dw_in, carried[f"{tag}_dwin"] = _ffn_dwin(h, da, f"{tag}_dwin", sides.get(f"{tag}_dwin"))
    (dx, dnorm), carried[f"{tag}_dx"] = _ffn_dx(da, w4, x, norm, dy, f"{tag}_dx", sides.get(f"{tag}_dx"))
    return dx, dnorm, dw_in, dw_out.reshape((N_CHIPS, 2 * FC // N_CHIPS, D_MODEL)), carried


def _mixer_fwd(x, w, tabs, tag, sides=None):
    sides = sides or {}
    carried = {}
    h = _rmsnorm_fwd(x, w['mix_norm'], f"{tag}_norm")
    if sides.get('proj') is None:
        pm = _matmul(h, w['w_in_main'], out_dtype=BF16, name=f"{tag}_proj_main")
    else:
        pm, carried['proj'] = _matmul(h, w['w_in_main'], out_dtype=BF16, name=f"{tag}_proj_main", side=sides['proj'])
    pt = _matmul(h, w['w_in_tail'], name=f"{tag}_proj_tail")
    ya = _gmlp_fwd(pm, w['gm_v_norm'], w['gm_w_s'], w['gm_b_full'], f"{tag}_gmlp")
    q, k, v = _mla_pre_fwd(pt, tabs, w['mla_q_norm'], w['mla_kv_norm'], w['wuq'], w['wkv'], w['gq'], w['gk'], f"{tag}_mla_pre")
    (o, lse), carried['attn'] = _attn_fwd(q, k, v, f"{tag}_attn", sides.get('attn'))
    xs = _conv_fwd(pm, C_XS, HP, w['conv_w'][:, :HP], w['conv_b'][:, :HP], f"{tag}_conv_x")
    bc = _conv_fwd(pt, T_BC, BCW, w['conv_w'][:, HP:], w['conv_b'][:, HP:], f"{tag}_conv_bc")
    dtb, dab = _dt_fwd(pt, w['dt_bias'], w['a_log'], f"{tag}_dt")
    (ys, s_in), carried['scan'] = _scan_fwd(xs, bc, dtb, dab, f"{tag}_scan", sides.get('scan'))
    yc = _ssd_post_fwd(ys, xs, pm, w['d_vec'], w['ssd_norm'], f"{tag}_ssd_post")
    (mg, y), carried['merge'] = _merge_fwd(pm, ya, o, yc, w['wb0'], w['wb1'], w['wb2'], w['w_out'], x, f"{tag}_merge",
                                           sides.get('merge'))
    return y, (x, h, pm, pt, ya, q, k, v, o, lse, xs, bc, dtb, dab, ys, s_in, yc, mg), carried


def _pair_sums(pending, got):
    return {n: _pair_add(pending[n], got[n], f"pair_add_{n}") for n in got}


def _chip_sums(sums, arrived):
    return {n: _chip_add(sums[n][1], arrived[n], f"chip_add_{n}") for n in arrived}


def _reduce_to_chip(pending):
    names = list(pending)
    got = _run_exchange(_pair_exchange([pending[n] for n in names]), "pair_exchange")
    sums = _pair_sums(pending, dict(zip(names, got)))
    arrived = _run_exchange(_chip_exchange([sums[n][0] for n in names]), "chip_exchange")
    return _chip_sums(sums, dict(zip(names, arrived)))


def _mixer_bwd(dy, saved, w, tabs, tag, sides=None):
    sides = sides or {}
    carried = {}
    x, h, pm, pt, ya, q, k, v, o, lse, xs, bc, dtb, dab, ys, s_in, yc, mg = saved
    g = {}
    g['w_out'] = _matmul(mg, dy, ta=True, name=f"{tag}_dwout").reshape((N_CHIPS, D_MODEL // N_CHIPS, D_MODEL))
    d0, d1, d2, dgates, dya, do, dyc = _merge_bwd(pm, ya, o, yc, w['wb0'], w['wb1'], w['wb2'], w['w_out'], dy, f"{tag}_dmerge")
    dwb0 = _matmul(ya, d0, ta=True, name=f"{tag}_dwb0")
    dwb1 = _matmul(o, d1, ta=True, name=f"{tag}_dwb1")
    dwb2 = _matmul(yc, d2, ta=True, name=f"{tag}_dwb2")
    g['w_branch'] = _split('w_branch', jnp.stack([dwb0, _heads_unpad(dwb1, MLA_V, 0), _heads_unpad(dwb2, SSD_HEAD_DIM, 0)]))
    duv, g['gm_v_norm'], g['gm_w_s'], db = _gmlp_bwd(pm, w['gm_v_norm'], w['gm_w_s'], w['gm_b_full'], dya, f"{tag}_dgmlp")
    g['gm_b_s'] = db.T
    (dq, delta), carried['dattn_q'] = _attn_bwd_dq(q, k, v, o, lse, do, f"{tag}_dattn_q", sides.get('dattn_q'))
    (dk, dv), carried['dattn_kv'] = _attn_bwd_dkv(q, k, v, lse, delta, do, f"{tag}_dattn_kv", sides.get('dattn_kv'))
    (dcq, dckv, dkr, dwuq, dwkv, g['mla_q_norm'], g['mla_kv_norm'], dgq, dgk), carried['dmla_pre'] = _mla_pre_bwd(
        pt, tabs, w['mla_q_norm'], w['mla_kv_norm'], w['wuq'], w['wkv'], w['gq'], w['gk'], dq, dk, dv, f"{tag}_dmla_pre",
        sides.get('dmla_pre'))
    g['mla_w_uq'] = _split('mla_w_uq', _heads_unpad(dwuq, MLA_QK, 1))
    dwk = dwkv[:, :HP].reshape(MLA_KV_RANK, MLA_HEADS, LANES)[:, :, :MLA_NOPE]
    dwv = dwkv[:, HP:].reshape(MLA_KV_RANK, MLA_HEADS, LANES)[:, :, :MLA_V]
    g['mla_w_ukv'] = _split('mla_w_ukv', jnp.concatenate([dwk, dwv], axis=2).reshape(MLA_KV_RANK, MLA_HEADS * (MLA_NOPE + MLA_V)))
    g['mla_q_gain'], g['mla_k_gain'] = dgq[:, :MLA_QK], dgk[:, :MLA_QK]
    dys, dz, dssd_norm, dd = _ssd_post_bwd(ys, xs, pm, w['d_vec'], w['ssd_norm'], dyc, f"{tag}_dssd_post")
    g['ssd_norm'] = _heads_unpad(dssd_norm, SSD_HEAD_DIM, 1)
    g['ssd_d'] = jnp.sum(dd.reshape(SSD_HEADS, LANES), axis=1)[None, :]
    dxs, dbm, dcm, dda, ddtx = _scan_bwd(xs, bc, dtb, dab, s_in, dys, w['d_vec'], f"{tag}_dscan")
    dxs16, dcw_x, dcb_x = _conv_bwd(pm, C_XS, HP, w['conv_w'][:, :HP], w['conv_b'][:, :HP], dxs, f"{tag}_dconv_x")
    dbc16, dcw_bc, dcb_bc = _conv_bwd(pt, T_BC, BCW, w['conv_w'][:, HP:], w['conv_b'][:, HP:],
                                      jnp.concatenate([dbm, dcm], axis=1), f"{tag}_dconv_bc")
    g['ssd_conv_w'] = _xbc_unpad(jnp.concatenate([dcw_x, dcw_bc], axis=1))
    g['ssd_conv_b'] = _xbc_unpad(jnp.concatenate([dcb_x, dcb_bc], axis=1))
    ddt, dbias, dalog = _dt_bwd(pt, w['dt_bias'], w['a_log'], dda, ddtx, f"{tag}_ddt")
    g['ssd_dt_bias'], g['ssd_a_log'] = dbias[:, :SSD_HEADS], dalog[:, :SSD_HEADS]
    s = x.shape[0]
    dpm = jnp.concatenate([duv, dz, dxs16, dgates], axis=1)
    dpt = jnp.concatenate([dbc16, dckv, dcq, dkr, ddt, jnp.zeros((s, PW_TAIL - T_DT - LANES), BF16)], axis=1)
    g['w_in'] = _split('w_in', _w_in_unpad(_matmul(h, dpm, ta=True, name=f"{tag}_dwin_main"),
                                           _matmul(h, dpt, ta=True, name=f"{tag}_dwin_tail")))
    dh = _matmul(dpt, w['w_in_tail'], tb=True, name=f"{tag}_dh_tail")
    dh = _matmul(dpm, w['w_in_main'], tb=True, res=dh, name=f"{tag}_dh_main")
    dx, g['mix_norm'] = _rmsnorm_bwd(x, w['mix_norm'], dh, dy, f"{tag}_dnorm")
    return dx, g, carried


_CONV_ROWS = 32


def _rows_cols(a, lead):
    return a.reshape(a.shape[:lead] + (int(np.prod(a.shape[lead:-1])), a.shape[-1]))


def _shard_views(wts):
    views = []
    for n in SHARDED_ORDER:
        a = _rows_cols(wts[n].astype(BF16), 1)
        if n == 'ssd_conv_w':
            a = jnp.pad(a, ((0, 0), (0, _CONV_ROWS - a.shape[1]), (0, 0)))
        views.append(a)
    return views


def _gathered(names, arrays):
    out = {}
    for n, a in zip(names, arrays):
        shp = _shard_shape(n)
        if n == 'ssd_conv_w':
            a = a[:, :shp[0]]
        out[n] = a.reshape((N_CHIPS,) + shp)
    return out


def _local_step(x, positions, target, weights, small, distributed=True):
    tabs = _rope_tables(positions)
    views = dict(zip(SHARDED_ORDER, weights)) if distributed else None
    plan = [{} for _ in range(DEPTH)]
    if distributed:
        for l in range(DEPTH - 1):
            plan[l].update({host: (names, l + 1) for host, names in GATHER_HOSTS.items()})
        plan[0].update({host: (names, 0) for host, names in FIRST_HOSTS.items()})
        have = [dict() for _ in range(DEPTH)]
        have[0].update(_gathered(FIRST_NOW, _run_exchange(_gather_exchange([views[n] for n in FIRST_NOW], 0), "gather_first")))
    else:
        have = weights

    def absorb(l, carried):
        for host, arrays in carried.items():
            if host in plan[l]:
                names, layer = plan[l][host]
                have[layer].update(_gathered(names, arrays))

    ws, saved = [], []
    for l in range(DEPTH):
        sides = {host: _gather_exchange([views[n] for n in names], layer) for host, (names, layer) in plan[l].items()}
        w = _layer_weights(have[l], small, l, 'ffn1')
        x, s1, carried = _ffn_fwd(x, w['ffn1_norm'], w['ffn1_w_in'], w['ffn1_w_out'], "ffn1", sides)
        absorb(l, carried)
        w.update(_layer_weights(have[l], small, l, 'mixer'))
        x, s2, carried = _mixer_fwd(x, w, tabs, "mix", sides)
        absorb(l, carried)
        w.update(_layer_weights(have[l], small, l, 'ffn2'))
        x, s3, carried = _ffn_fwd(x, w['ffn2_norm'], w['ffn2_w_in'], w['ffn2_w_out'], "ffn2", sides)
        absorb(l, carried)
        ws.append(w)
        saved.append((s1, s2, s3))
    dy, sq = _loss_head(x, target, "loss_head")
    loss = 0.5 * jnp.sum(sq) / D_MODEL
    grads, reduced, pending = [None] * DEPTH, [dict() for _ in range(DEPTH)], None

    def chip_sides(sums, hosts):
        return {host: _chip_exchange([sums[n][0] for n in names]) for host, names in hosts.items()}

    def arrivals(carried, hosts):
        return {n: a for host, names in hosts.items() for n, a in zip(names, carried[host])}

    for l in reversed(range(DEPTH)):
        w = ws[l]
        s1, s2, s3 = saved[l]
        sides = {host: _pair_exchange([pending[n] for n in names]) for host, names in PAIR_HOSTS.items()} if pending else {}
        dy, dn2, dwi2, dwo2, carried = _ffn_bwd(dy, s3, w['ffn2_norm'], w['ffn2_w_in'], w['ffn2_w_out'], "ffn2", sides)
        sides = {}
        if pending:
            sums = _pair_sums(pending, arrivals(carried, PAIR_HOSTS))
            sides = chip_sides(sums, REDUCE_HOSTS)
        dy, g, carried = _mixer_bwd(dy, s2, w, tabs, "mix", sides)
        if pending:
            reduced[l + 1] = _chip_sums(sums, arrivals(carried, REDUCE_HOSTS))
        g.update(ffn2_norm=dn2, ffn2_w_in=dwi2, ffn2_w_out=dwo2)
        last = distributed and l == 0
        if last:
            early = {n: _rows_cols(g[n], 1) for n in LAST_EARLY}
            after = {}

            def after_dwout(got):
                after['sums'] = _pair_sums(early, dict(zip(LAST_EARLY, got)))
                return chip_sides(after['sums'], LAST_HOSTS)

            dy, dn1, dwi1, dwo1, carried = _ffn_bwd(dy, s1, w['ffn1_norm'], w['ffn1_w_in'], w['ffn1_w_out'], "ffn1",
                                                    {'ffn1_dwout': _pair_exchange([early[n] for n in LAST_EARLY])}, after_dwout)
            reduced[0].update(_chip_sums(after['sums'], arrivals(carried, LAST_HOSTS)))
        else:
            dy, dn1, dwi1, dwo1, _ = _ffn_bwd(dy, s1, w['ffn1_norm'], w['ffn1_w_in'], w['ffn1_w_out'], "ffn1")
        g.update(ffn1_norm=dn1, ffn1_w_in=dwi1, ffn1_w_out=dwo1)
        grads[l] = g
        if distributed:
            pending = {n: _rows_cols(g[n], 1) for n in REDUCED}
    if distributed:
        reduced[0].update(_reduce_to_chip({n: pending[n] for n in LAST_LATE}))
    return loss, dy, grads, reduced


SMALL_PACK = SMALL_ORDER + ['ssd_conv_w']


def _pack_small(per_layer_rows, tail=None):
    parts = [per_layer_rows[l][n].reshape(-1).astype(F32) for l in range(DEPTH) for n in SMALL_PACK]
    if tail is not None:
        parts.append(tail.reshape(1))
    flat = jnp.concatenate(parts)
    rows = -(-flat.shape[0] // LANES)
    rows = -(-rows // 8) * 8
    return jnp.pad(flat, (0, rows * LANES - flat.shape[0])).reshape(rows, LANES)


def _unpack_small(buf, shapes):
    flat = buf.reshape(-1)
    off = 0
    out = {n: [] for n in SMALL_PACK}
    for l in range(DEPTH):
        for n in SMALL_PACK:
            size = int(np.prod(shapes[n]))
            out[n].append(flat[off:off + size].reshape(shapes[n]))
            off += size
    return {n: jnp.stack(v) for n, v in out.items()}


def kernel(x, positions, ffn1_norm, ffn1_w_in, ffn1_w_out, mix_norm, w_in, gm_v_norm, gm_w_s, gm_b_s, mla_q_norm, mla_kv_norm, mla_w_uq, mla_w_ukv, mla_q_gain, mla_k_gain, ssd_conv_w, ssd_conv_b, ssd_dt_bias, ssd_a_log, ssd_d, ssd_norm, w_branch, w_out, ffn2_norm, ffn2_w_in, ffn2_w_out, loss_target, m_ffn1_norm, m_ffn1_w_in, m_ffn1_w_out, m_mix_norm, m_w_in, m_gm_v_norm, m_gm_w_s, m_gm_b_s, m_mla_q_norm, m_mla_kv_norm, m_mla_w_uq, m_mla_w_ukv, m_mla_q_gain, m_mla_k_gain, m_ssd_conv_w, m_ssd_conv_b, m_ssd_dt_bias, m_ssd_a_log, m_ssd_d, m_ssd_norm, m_w_branch, m_w_out, m_ffn2_norm, m_ffn2_w_in, m_ffn2_w_out, v_ffn1_norm, v_ffn1_w_in, v_ffn1_w_out, v_mix_norm, v_w_in, v_gm_v_norm, v_gm_w_s, v_gm_b_s, v_mla_q_norm, v_mla_kv_norm, v_mla_w_uq, v_mla_w_ukv, v_mla_q_gain, v_mla_k_gain, v_ssd_conv_w, v_ssd_conv_b, v_ssd_dt_bias, v_ssd_a_log, v_ssd_d, v_ssd_norm, v_w_branch, v_w_out, v_ffn2_norm, v_ffn2_w_in, v_ffn2_w_out):
    wts = dict(zip(WEIGHTS, (ffn1_norm, ffn1_w_in, ffn1_w_out, mix_norm, w_in, gm_v_norm, gm_w_s, gm_b_s, mla_q_norm, mla_kv_norm,
                             mla_w_uq, mla_w_ukv, mla_q_gain, mla_k_gain, ssd_conv_w, ssd_conv_b, ssd_dt_bias, ssd_a_log, ssd_d,
                             ssd_norm, w_branch, w_out, ffn2_norm, ffn2_w_in, ffn2_w_out)))
    mom = dict(zip(WEIGHTS, (m_ffn1_norm, m_ffn1_w_in, m_ffn1_w_out, m_mix_norm, m_w_in, m_gm_v_norm, m_gm_w_s, m_gm_b_s, m_mla_q_norm,
                             m_mla_kv_norm, m_mla_w_uq, m_mla_w_ukv, m_mla_q_gain, m_mla_k_gain, m_ssd_conv_w, m_ssd_conv_b,
                             m_ssd_dt_bias, m_ssd_a_log, m_ssd_d, m_ssd_norm, m_w_branch, m_w_out, m_ffn2_norm, m_ffn2_w_in,
                             m_ffn2_w_out)))
    var = dict(zip(WEIGHTS, (v_ffn1_norm, v_ffn1_w_in, v_ffn1_w_out, v_mix_norm, v_w_in, v_gm_v_norm, v_gm_w_s, v_gm_b_s, v_mla_q_norm,
                             v_mla_kv_norm, v_mla_w_uq, v_mla_w_ukv, v_mla_q_gain, v_mla_k_gain, v_ssd_conv_w, v_ssd_conv_b,
                             v_ssd_dt_bias, v_ssd_a_log, v_ssd_d, v_ssd_norm, v_w_branch, v_w_out, v_ffn2_norm, v_ffn2_w_in,
                             v_ffn2_w_out)))
    cx, cy, _ = _me()
    mychip = _chip_index(cx, cy)

    small = {n: wts[n] for n in SMALL_ORDER}
    loss_part, dx, grads, reduced = _local_step(x[0], positions[0], loss_target[0], _shard_views(wts), small)
    rows_cols = _rows_cols
    halves = [jnp.stack([reduced[l][n] for l in range(DEPTH)]) for n in REDUCED]
    theirs = _run_exchange(_pair_share(halves), "pair_share")
    shapes = {n: wts[n].shape[1:] for n in SMALL_ORDER}
    shapes['ssd_conv_w'] = SHARDED['ssd_conv_w'][0]
    summed = _sum_slots(_all_exchange(_pack_small(grads, tail=loss_part)), "small_sum")
    small_g = _unpack_small(summed, shapes)
    loss = summed.reshape(-1)[DEPTH * sum(int(np.prod(shapes[n])) for n in SMALL_PACK)]
    conv_full = small_g.pop('ssd_conv_w')
    shard_cols = _shard_shape('ssd_conv_w')[1]
    small_g['ssd_conv_w'] = lax.dynamic_slice_in_dim(conv_full, mychip * shard_cols, shard_cols, axis=2)
    shapes['ssd_conv_w'] = _shard_shape('ssd_conv_w')

    grad, delta, new_m, new_v = {}, {}, {}, {}
    for n, a, b in zip(REDUCED, halves, theirs):
        shp = wts[n].shape
        outs = _adamw_sharded(rows_cols(wts[n], 1), rows_cols(mom[n], 1), rows_cols(var[n], 1), a, b, f"adamw_{n}")
        grad[n], delta[n], new_m[n], new_v[n] = [o.reshape(shp) for o in outs]
    per_layer = lambda t: [{n: t[n][l] for n in SMALL_PACK} for l in range(DEPTH)]
    d, nm, nv = _adamw(_pack_small(per_layer(wts)), _pack_small(per_layer(small_g)), _pack_small(per_layer(mom)),
                       _pack_small(per_layer(var)), "adamw_small")
    sd, snm, snv = _unpack_small(d, shapes), _unpack_small(nm, shapes), _unpack_small(nv, shapes)
    for n in SMALL_PACK:
        grad[n], delta[n], new_m[n], new_v[n] = small_g[n], sd[n], snm[n], snv[n]
    return (loss, dx[None], *[grad[n] for n in WEIGHTS], *[delta[n] for n in WEIGHTS], *[new_m[n] for n in WEIGHTS],
            *[new_v[n] for n in WEIGHTS])
```

```python
import functools
import math

import numpy as np
import jax
import jax.numpy as jnp
from jax import lax
from jax.experimental import pallas as pl
from jax.experimental.pallas import tpu as pltpu

F32, BF16 = jnp.float32, jnp.bfloat16
MESH = pl.DeviceIdType.MESH

D_MODEL, DEPTH, D_FF, EPS = 1024, 4, 2816, 1e-6
GM_WIDTH, GM_GROUPS, CHUNK = 512, 4, 128
MLA_HEADS, MLA_Q_RANK, MLA_KV_RANK, MLA_NOPE, MLA_ROPE, MLA_V = 8, 384, 256, 64, 32, 64
MLA_QK = MLA_NOPE + MLA_ROPE
ROPE_THETA = 10000.0
SSD_HEADS, SSD_HEAD_DIM, SSD_GROUPS, SSD_STATE, SSD_CONV = 8, 64, 2, 128, 4
SSD_INNER = SSD_HEADS * SSD_HEAD_DIM
IN_COLS = 6312
LANES = 128
ADAM_LR, ADAM_B1, ADAM_B2, ADAM_EPS, ADAM_WD, ADAM_STEP = 0.001, 0.9, 0.999, 1e-08, 0.01, 10

C_UV, C_Z, C_XS, C_G, PW_MAIN = 0, 1024, 2048, 3072, 6144
T_BC, T_CKV, T_CQ, T_KR, T_DT, PW_TAIL = 0, 512, 768, 1152, 1280, 1536
HP = MLA_HEADS * LANES
FC = 2 * D_FF // 4

WEIGHTS = ['ffn1_norm', 'ffn1_w_in', 'ffn1_w_out', 'mix_norm', 'w_in', 'gm_v_norm', 'gm_w_s', 'gm_b_s', 'mla_q_norm',
           'mla_kv_norm', 'mla_w_uq', 'mla_w_ukv', 'mla_q_gain', 'mla_k_gain', 'ssd_conv_w', 'ssd_conv_b', 'ssd_dt_bias',
           'ssd_a_log', 'ssd_d', 'ssd_norm', 'w_branch', 'w_out', 'ffn2_norm', 'ffn2_w_in', 'ffn2_w_out']
SHARDED = {'ffn1_w_in': ((1024, 5632), 1), 'ffn1_w_out': ((2816, 1024), 0), 'w_in': ((1024, 6312), 1),
           'mla_w_uq': ((384, 768), 1), 'mla_w_ukv': ((256, 1024), 1), 'ssd_conv_w': ((4, 1024), 1),
           'w_branch': ((3, 512, 1024), 2), 'w_out': ((1024, 1024), 0), 'ffn2_w_in': ((1024, 5632), 1),
           'ffn2_w_out': ((2816, 1024), 0)}
SHARDED_ORDER = [n for n in WEIGHTS if n in SHARDED]
SMALL_ORDER = [n for n in WEIGHTS if n not in SHARDED]
REDUCED = [n for n in SHARDED_ORDER if n != 'ssd_conv_w']
N_CHIPS = 4
HALF_L = DEPTH // 2


def _shard_shape(name):
    shape, ax = SHARDED[name]
    return tuple(d // N_CHIPS if i == ax else d for i, d in enumerate(shape))


def _pick(dim, target):
    if dim <= target:
        return dim
    t = (target // LANES) * LANES
    while t >= LANES:
        if dim % t == 0:
            return t
        t -= LANES
    return dim


def _sigmoid(x):
    return 1.0 / (1.0 + jnp.exp(-x))


def _params(*sem):
    return pltpu.CompilerParams(dimension_semantics=sem, vmem_limit_bytes=56 * 1024 * 1024)


def _matmul(a, b, *, ta=False, tb=False, out_dtype=F32, scale=1.0, res=None, name, side=None):
    if ta:
        k_dim, m_dim = a.shape
    else:
        m_dim, k_dim = a.shape
    if tb:
        n_dim, k2 = b.shape
    else:
        k2, n_dim = b.shape
    assert k_dim == k2, (a.shape, b.shape, ta, tb)
    tm, tn, tk = _pick(m_dim, 1024), _pick(n_dim, 1024), _pick(k_dim, 1024)
    nk = k_dim // tk
    dn = (((0 if ta else 1,), (1 if tb else 0,)), ((), ()))

    def body(*refs):
        if res is not None:
            a_ref, b_ref, r_ref, o_ref, acc = refs
        else:
            a_ref, b_ref, o_ref, acc = refs
        k = pl.program_id(2)

        @pl.when(k == 0)
        def _():
            acc[...] = jnp.zeros_like(acc)

        acc[...] += lax.dot_general(a_ref[...].astype(BF16), b_ref[...].astype(BF16), dn, preferred_element_type=F32)

        @pl.when(k == nk - 1)
        def _():
            r = acc[...]
            if scale != 1.0:
                r = r * scale
            if res is not None:
                r = r + r_ref[...]
            o_ref[...] = r.astype(out_dtype)

    a_spec = pl.BlockSpec((tk, tm), lambda j, i, k: (k, i)) if ta else pl.BlockSpec((tm, tk), lambda j, i, k: (i, k))
    b_spec = pl.BlockSpec((tn, tk), lambda j, i, k: (j, k)) if tb else pl.BlockSpec((tk, tn), lambda j, i, k: (k, j))
    in_specs = [a_spec, b_spec]
    args = [a, b]
    if res is not None:
        in_specs.append(pl.BlockSpec((tm, tn), lambda j, i, k: (i, j)))
        args.append(res)
    (out,), carried = _call(
        body, name=name, grid=(n_dim // tn, m_dim // tm, nk), in_specs=in_specs,
        out_specs=[pl.BlockSpec((tm, tn), lambda j, i, k: (i, j))],
        out_shape=[jax.ShapeDtypeStruct((m_dim, n_dim), out_dtype)],
        scratch_shapes=[pltpu.VMEM((tm, tn), F32)], args=args, semantics=("parallel", "parallel", "arbitrary"), side=side)
    return out if side is None else (out, carried)


def _rmsnorm_fwd(x, gain, name):
    s, d = x.shape
    tm = _pick(s, 512)

    def body(x_ref, g_ref, o_ref):
        xv = x_ref[...]
        r = lax.rsqrt(jnp.mean(xv * xv, axis=-1, keepdims=True) + EPS)
        o_ref[...] = (xv * r * g_ref[...]).astype(BF16)

    return pl.pallas_call(
        body, name=name, grid=(s // tm,),
        in_specs=[pl.BlockSpec((tm, d), lambda i: (i, 0)), pl.BlockSpec((1, d), lambda i: (0, 0))],
        out_specs=pl.BlockSpec((tm, d), lambda i: (i, 0)),
        out_shape=jax.ShapeDtypeStruct((s, d), BF16), compiler_params=_params("parallel"))(x, gain)


def _rmsnorm_bwd(x, gain, dh, dres, name):
    s, d = x.shape
    tm = _pick(s, 512)

    def body(x_ref, g_ref, dh_ref, dr_ref, dx_ref, dg_ref):
        @pl.when(pl.program_id(0) == 0)
        def _():
            dg_ref[...] = jnp.zeros_like(dg_ref)

        xv, dhv = x_ref[...], dh_ref[...]
        r = lax.rsqrt(jnp.mean(xv * xv, axis=-1, keepdims=True) + EPS)
        u = dhv * g_ref[...]
        dx_ref[...] = dr_ref[...] + r * u - xv * (r * r * r) * jnp.mean(xv * u, axis=-1, keepdims=True)
        dg_ref[...] += jnp.sum(dhv * xv * r, axis=0, keepdims=True)

    row = pl.BlockSpec((tm, d), lambda i: (i, 0))
    vec = pl.BlockSpec((1, d), lambda i: (0, 0))
    return pl.pallas_call(
        body, name=name, grid=(s // tm,), in_specs=[row, vec, row, row], out_specs=[row, vec],
        out_shape=[jax.ShapeDtypeStruct((s, d), F32), jax.ShapeDtypeStruct((1, d), F32)],
        compiler_params=_params("arbitrary"))(x, gain, dh, dres)


_NT = (((1,), (1,)), ((), ()))
_TN = (((0,), (0,)), ((), ()))


def _resident(shape):
    return pl.BlockSpec(shape, lambda *_: tuple(0 for _ in shape), pipeline_mode=pl.Buffered(1))


def _ffn_in(x, gain, w4, name, side=None):
    s, d = x.shape
    tm = _pick(s, 512)

    def body(x_ref, g_ref, w_ref, h_ref, gate_ref, up_ref, act_ref):
        xv = x_ref[...]
        r = lax.rsqrt(jnp.mean(xv * xv, axis=-1, keepdims=True) + EPS)
        h = (xv * r * g_ref[...]).astype(BF16)
        h_ref[...] = h
        for j in range(2):
            g16 = jnp.dot(h, w_ref[j], preferred_element_type=F32).astype(BF16)
            u16 = jnp.dot(h, w_ref[j + 2], preferred_element_type=F32).astype(BF16)
            gate_ref[j] = g16
            up_ref[j] = u16
            gf, uf = g16.astype(F32), u16.astype(F32)
            act_ref[j] = (gf * _sigmoid(gf) * uf).astype(BF16)

    half = pl.BlockSpec((2, tm, FC), lambda i: (0, i, 0))
    return _call(
        body, name=name, grid=(s // tm,),
        in_specs=[pl.BlockSpec((tm, d), lambda i: (i, 0)), pl.BlockSpec((1, d), lambda i: (0, 0)), _resident((4, d, FC))],
        out_specs=[pl.BlockSpec((tm, d), lambda i: (i, 0)), half, half, half],
        out_shape=[jax.ShapeDtypeStruct((s, d), BF16)] + [jax.ShapeDtypeStruct((2, s, FC), BF16)] * 3,
        scratch_shapes=[], args=(x, gain, w4), semantics=("parallel",), side=side)


def _ffn_out(act, w_out, x, name, side=None):
    s, d = x.shape
    tm = _pick(s, 512)

    def body(a_ref, w_ref, x_ref, o_ref):
        acc = jnp.dot(a_ref[0], w_ref[0:FC, :], preferred_element_type=F32)
        acc = acc + jnp.dot(a_ref[1], w_ref[FC:2 * FC, :], preferred_element_type=F32)
        o_ref[...] = x_ref[...] + 0.5 * acc

    row = pl.BlockSpec((tm, d), lambda i: (i, 0))
    (out,), carried = _call(
        body, name=name, grid=(s // tm,),
        in_specs=[pl.BlockSpec((2, tm, FC), lambda i: (0, i, 0)), _resident((2 * FC, d)), row], out_specs=[row],
        out_shape=[jax.ShapeDtypeStruct((s, d), F32)], scratch_shapes=[], args=(act, w_out, x), semantics=("parallel",),
        side=side)
    return out, carried


def _ffn_dact(dy, w_out, gate, up, name, side=None):
    s, d = dy.shape
    tm = _pick(s, 512)

    def body(dy_ref, w_ref, g_ref, u_ref, o_ref):
        dy16 = dy_ref[...].astype(BF16)
        for j in range(2):
            dact = 0.5 * lax.dot_general(dy16, w_ref[j * FC:(j + 1) * FC, :], _NT, preferred_element_type=F32)
            g, u = g_ref[j].astype(F32), u_ref[j].astype(F32)
            sg = _sigmoid(g)
            o_ref[j] = (dact * u * (sg * (1.0 + g * (1.0 - sg)))).astype(BF16)
            o_ref[j + 2] = (dact * g * sg).astype(BF16)

    half = pl.BlockSpec((2, tm, FC), lambda i: (0, i, 0))
    (out,), carried = _call(
        body, name=name, grid=(s // tm,),
        in_specs=[pl.BlockSpec((tm, d), lambda i: (i, 0)), _resident((2 * FC, d)), half, half],
        out_specs=[pl.BlockSpec((4, tm, FC), lambda i: (0, i, 0))],
        out_shape=[jax.ShapeDtypeStruct((4, s, FC), BF16)], scratch_shapes=[], args=(dy, w_out, gate, up),
        semantics=("parallel",), side=side)
    return out, carried


def _ffn_dwout(act, dy, name, side=None):
    s, d = dy.shape
    tk = _pick(s, 1024)
    nk = s // tk

    def body(a_ref, dy_ref, o_ref):
        k = pl.program_id(1)

        @pl.when(k == 0)
        def _():
            o_ref[...] = jnp.zeros_like(o_ref)

        o_ref[...] += lax.dot_general(a_ref[...], dy_ref[...].astype(BF16), _TN, preferred_element_type=F32)

        @pl.when(k == nk - 1)
        def _():
            o_ref[...] = 0.5 * o_ref[...]

    (out,), carried = _call(
        body, name=name, grid=(2, nk),
        in_specs=[pl.BlockSpec((None, tk, FC), lambda j, k: (j, k, 0)), pl.BlockSpec((tk, d), lambda j, k: (k, 0))],
        out_specs=[pl.BlockSpec((FC, d), lambda j, k: (j, 0))], out_shape=[jax.ShapeDtypeStruct((2 * FC, d), F32)],
        scratch_shapes=[], args=(act, dy), semantics=("parallel", "arbitrary"), side=side)
    return out, carried


def _ffn_dwin(h, da, name, side=None):
    s, d = h.shape
    tk = _pick(s, 1024)

    def body(h_ref, da_ref, o_ref):
        @pl.when(pl.program_id(1) == 0)
        def _():
            o_ref[...] = jnp.zeros_like(o_ref)

        o_ref[...] += lax.dot_general(h_ref[...], da_ref[...], _TN, preferred_element_type=F32)

    (out,), carried = _call(
        body, name=name, grid=(4, s // tk),
        in_specs=[pl.BlockSpec((tk, d), lambda j, k: (k, 0)), pl.BlockSpec((None, tk, FC), lambda j, k: (j, k, 0))],
        out_specs=[pl.BlockSpec((None, d, FC), lambda j, k: (j, 0, 0))], out_shape=[jax.ShapeDtypeStruct((4, d, FC), F32)],
        scratch_shapes=[], args=(h, da), semantics=("parallel", "arbitrary"), side=side)
    return out, carried


def _ffn_dx(da, w4, x, gain, dy, name, side=None):
    s, d = x.shape
    tm = _pick(s, 512)

    def body(da_ref, w_ref, x_ref, g_ref, dy_ref, dx_ref, dg_ref):
        @pl.when(pl.program_id(0) == 0)
        def _():
            dg_ref[...] = jnp.zeros_like(dg_ref)

        dh = jnp.zeros((tm, d), F32)
        for j in range(4):
            dh = dh + lax.dot_general(da_ref[j], w_ref[j], _NT, preferred_element_type=F32)
        xv = x_ref[...]
        r = lax.rsqrt(jnp.mean(xv * xv, axis=-1, keepdims=True) + EPS)
        u = dh * g_ref[...]
        dx_ref[...] = dy_ref[...] + r * u - xv * (r * r * r) * jnp.mean(xv * u, axis=-1, keepdims=True)
        dg_ref[...] += jnp.sum(dh * xv * r, axis=0, keepdims=True)

    row = pl.BlockSpec((tm, d), lambda i: (i, 0))
    vec = pl.BlockSpec((1, d), lambda i: (0, 0))
    return _call(
        body, name=name, grid=(s // tm,),
        in_specs=[pl.BlockSpec((4, tm, FC), lambda i: (0, i, 0)), _resident((4, d, FC)), row, vec, row],
        out_specs=[row, vec], out_shape=[jax.ShapeDtypeStruct((s, d), F32), jax.ShapeDtypeStruct((1, d), F32)],
        scratch_shapes=[], args=(da, w4, x, gain, dy), semantics=("arbitrary",), side=side)


_INV_SQRT2 = 0.7071067811865476
_INV_SQRT2PI = 0.3989422804014327


def _gelu(x):
    return 0.5 * x * (1.0 + lax.erf(x * _INV_SQRT2))


def _gelu_grad(x):
    return 0.5 * (1.0 + lax.erf(x * _INV_SQRT2)) + x * jnp.exp(-0.5 * x * x) * _INV_SQRT2PI


def _tril_mask():
    r = lax.broadcasted_iota(jnp.int32, (CHUNK, CHUNK), 0)
    c = lax.broadcasted_iota(jnp.int32, (CHUNK, CHUNK), 1)
    return r >= c


def _gmlp_fwd(p, v_gain, w_s, b_full, name):
    s = p.shape[0]
    tm = _pick(s, 512)
    nch = tm // CHUNK

    def body(uv_ref, g_ref, w_ref, b_ref, o_ref):
        gel = _gelu(uv_ref[...].astype(F32))
        u, v = gel[:, :GM_WIDTH], gel[:, GM_WIDTH:]
        r = lax.rsqrt(jnp.mean(v * v, axis=-1, keepdims=True) + EPS)
        vn = (v * r * g_ref[...]).astype(BF16)
        mask = _tril_mask()
        for g in range(GM_GROUPS):
            wm = jnp.where(mask, w_ref[g], 0.0).astype(BF16)
            for c in range(nch):
                rs, cs = slice(c * CHUNK, (c + 1) * CHUNK), slice(g * LANES, (g + 1) * LANES)
                sp = jnp.dot(wm, vn[rs, cs], preferred_element_type=F32) + b_ref[g]
                o_ref[rs, cs] = (u[rs, cs] * sp).astype(BF16)

    full3 = pl.BlockSpec((GM_GROUPS, CHUNK, CHUNK), lambda i: (0, 0, 0))
    return pl.pallas_call(
        body, name=name, grid=(s // tm,),
        in_specs=[pl.BlockSpec((tm, 2 * GM_WIDTH), lambda i: (i, C_UV // (2 * GM_WIDTH))),
                  pl.BlockSpec((1, GM_WIDTH), lambda i: (0, 0)), full3, full3],
        out_specs=pl.BlockSpec((tm, GM_WIDTH), lambda i: (i, 0)),
        out_shape=jax.ShapeDtypeStruct((s, GM_WIDTH), BF16), compiler_params=_params("parallel"))(p, v_gain, w_s, b_full)


def _gmlp_bwd(p, v_gain, w_s, b_full, dy, name):
    s = p.shape[0]
    tm = _pick(s, 512)
    nch = tm // CHUNK
    nsteps = s // tm

    def body(uv_ref, g_ref, w_ref, b_ref, dy_ref, duv_ref, dg_ref, dw_ref, db_ref, dvn_s, dbacc):
        step = pl.program_id(0)

        @pl.when(step == 0)
        def _():
            dg_ref[...] = jnp.zeros_like(dg_ref)
            dw_ref[...] = jnp.zeros_like(dw_ref)
            dbacc[...] = jnp.zeros_like(dbacc)

        uv = uv_ref[...].astype(F32)
        gel = _gelu(uv)
        u, v = gel[:, :GM_WIDTH], gel[:, GM_WIDTH:]
        r = lax.rsqrt(jnp.mean(v * v, axis=-1, keepdims=True) + EPS)
        gain = g_ref[...]
        vn32 = v * r * gain
        vn = vn32.astype(BF16)
        dy = dy_ref[...]
        mask = _tril_mask()
        for g in range(GM_GROUPS):
            wm = jnp.where(mask, w_ref[g], 0.0).astype(BF16)
            dwg = jnp.zeros((CHUNK, CHUNK), F32)
            dbg = jnp.zeros((CHUNK, LANES), F32)
            for c in range(nch):
                rs, cs = slice(c * CHUNK, (c + 1) * CHUNK), slice(g * LANES, (g + 1) * LANES)
                sp = jnp.dot(wm, vn[rs, cs], preferred_element_type=F32) + b_ref[g]
                dyc = dy[rs, cs]
                dsp = dyc * u[rs, cs]
                dsp16 = dsp.astype(BF16)
                duv_ref[rs, cs] = (dyc * sp * _gelu_grad(uv[rs, cs])).astype(BF16)
                dvn_s[rs, cs] = lax.dot_general(wm, dsp16, (((0,), (0,)), ((), ())), preferred_element_type=F32)
                dwg = dwg + lax.dot_general(dsp16, vn[rs, cs], (((1,), (1,)), ((), ())), preferred_element_type=F32)
                dbg = dbg + dsp
            dw_ref[g] += jnp.where(mask, dwg, 0.0)
            dbacc[:, g * LANES:(g + 1) * LANES] += dbg
        dvn = dvn_s[...]
        uu = dvn * gain
        dv = r * uu - v * (r * r * r) * jnp.mean(v * uu, axis=-1, keepdims=True)
        duv_ref[:, GM_WIDTH:] = (dv * _gelu_grad(uv[:, GM_WIDTH:])).astype(BF16)
        dg_ref[...] += jnp.sum(dvn * v * r, axis=0, keepdims=True)

        @pl.when(step == nsteps - 1)
        def _():
            for g in range(GM_GROUPS):
                db_ref[:, g:g + 1] = jnp.sum(dbacc[:, g * LANES:(g + 1) * LANES], axis=1, keepdims=True)

    full3 = pl.BlockSpec((GM_GROUPS, CHUNK, CHUNK), lambda i: (0, 0, 0))
    return pl.pallas_call(
        body, name=name, grid=(nsteps,),
        in_specs=[pl.BlockSpec((tm, 2 * GM_WIDTH), lambda i: (i, C_UV // (2 * GM_WIDTH))),
                  pl.BlockSpec((1, GM_WIDTH), lambda i: (0, 0)), full3, full3,
                  pl.BlockSpec((tm, GM_WIDTH), lambda i: (i, 0))],
        out_specs=[pl.BlockSpec((tm, 2 * GM_WIDTH), lambda i: (i, 0)), pl.BlockSpec((1, GM_WIDTH), lambda i: (0, 0)),
                   full3, pl.BlockSpec((CHUNK, GM_GROUPS), lambda i: (0, 0))],
        out_shape=[jax.ShapeDtypeStruct((s, 2 * GM_WIDTH), BF16), jax.ShapeDtypeStruct((1, GM_WIDTH), F32),
                   jax.ShapeDtypeStruct((GM_GROUPS, CHUNK, CHUNK), F32), jax.ShapeDtypeStruct((CHUNK, GM_GROUPS), F32)],
        scratch_shapes=[pltpu.VMEM((tm, GM_WIDTH), F32), pltpu.VMEM((CHUNK, GM_WIDTH), F32)],
        compiler_params=_params("arbitrary"))(p, v_gain, w_s, b_full, dy)


def _rope(x, ct, s1, s2):
    return x * ct + pltpu.roll(x, LANES - MLA_ROPE // 2, 1) * s1 + pltpu.roll(x, MLA_ROPE // 2, 1) * s2


def _rope_bwd(d, ct, s1, s2):
    return d * ct + pltpu.roll(d * s1, MLA_ROPE // 2, 1) + pltpu.roll(d * s2, LANES - MLA_ROPE // 2, 1)


def _head_norm(x, gain):
    r = lax.rsqrt(jnp.sum(x * x, axis=-1, keepdims=True) * (1.0 / MLA_QK) + EPS)
    return x * r * gain, r


def _head_norm_bwd(x, r, gain, d):
    u = d * gain
    return r * u - x * (r * r * r) * (jnp.sum(x * u, axis=-1, keepdims=True) * (1.0 / MLA_QK))


def _mla_specs(tm):
    cq = pl.BlockSpec((tm, MLA_Q_RANK), lambda i: (i, T_CQ // MLA_Q_RANK))
    ckv = pl.BlockSpec((tm, MLA_KV_RANK), lambda i: (i, T_CKV // MLA_KV_RANK))
    kr = pl.BlockSpec((tm, LANES), lambda i: (i, T_KR // LANES))
    tab = pl.BlockSpec((tm, LANES), lambda i: (i, 0))
    return cq, ckv, kr, tab


def _const(shape):
    return pl.BlockSpec(shape, lambda i: tuple(0 for _ in shape))


def _mla_pre_fwd(p, tabs, qn_g, kvn_g, wuq, wkv, gq, gk, name):
    s = p.shape[0]
    tm = _pick(s, 256)
    ct, s1, s2 = tabs

    def body(cq_ref, ckv_ref, kr_ref, ct_ref, s1_ref, s2_ref, qg_ref, kvg_ref, wuq_ref, wkv_ref, gq_ref, gk_ref,
             q_ref, k_ref, v_ref):
        cq, ckv, kr = cq_ref[...], ckv_ref[...], kr_ref[...]
        ctv, s1v, s2v = ct_ref[...], s1_ref[...], s2_ref[...]
        rq = lax.rsqrt(jnp.mean(cq * cq, axis=-1, keepdims=True) + EPS)
        q = jnp.dot((cq * rq * qg_ref[...]).astype(BF16), wuq_ref[...], preferred_element_type=F32)
        rk = lax.rsqrt(jnp.mean(ckv * ckv, axis=-1, keepdims=True) + EPS)
        kv = jnp.dot((ckv * rk * kvg_ref[...]).astype(BF16), wkv_ref[...], preferred_element_type=F32)
        v_ref[...] = kv[:, HP:].astype(BF16)
        for h in range(MLA_HEADS):
            hs = slice(h * LANES, (h + 1) * LANES)
            qh, _ = _head_norm(q[:, hs], gq_ref[...])
            q_ref[:, hs] = (_rope(qh, ctv, s1v, s2v) * _ATT_SCALE).astype(BF16)
            kh, _ = _head_norm(kv[:, hs] + kr, gk_ref[...])
            k_ref[:, hs] = _rope(kh, ctv, s1v, s2v).astype(BF16)

    cq_s, ckv_s, kr_s, tab_s = _mla_specs(tm)
    out = pl.BlockSpec((tm, HP), lambda i: (i, 0))
    return pl.pallas_call(
        body, name=name, grid=(s // tm,),
        in_specs=[cq_s, ckv_s, kr_s, tab_s, tab_s, tab_s, _const((1, MLA_Q_RANK)), _const((1, MLA_KV_RANK)),
                  _const((MLA_Q_RANK, HP)), _const((MLA_KV_RANK, 2 * HP)), _const((1, LANES)), _const((1, LANES))],
        out_specs=[out, out, out], out_shape=[jax.ShapeDtypeStruct((s, HP), BF16)] * 3,
        compiler_params=_params("parallel"))(p, p, p, ct, s1, s2, qn_g, kvn_g, wuq, wkv, gq, gk)


def _mla_pre_bwd(p, tabs, qn_g, kvn_g, wuq, wkv, gq, gk, dq, dk, dv, name, side=None):
    s = p.shape[0]
    tm = _pick(s, 256)
    ct, s1, s2 = tabs

    def body(cq_ref, ckv_ref, kr_ref, ct_ref, s1_ref, s2_ref, qg_ref, kvg_ref, wuq_ref, wkv_ref, gq_ref, gk_ref,
             dq_ref, dk_ref, dv_ref, dcq_ref, dckv_ref, dkr_ref, dwuq_ref, dwkv_ref, dqg_ref, dkvg_ref, dgq_ref, dgk_ref,
             dqp, dkvp):
        @pl.when(pl.program_id(0) == 0)
        def _():
            for ref in (dwuq_ref, dwkv_ref, dqg_ref, dkvg_ref, dgq_ref, dgk_ref):
                ref[...] = jnp.zeros_like(ref)

        cq, ckv, kr = cq_ref[...], ckv_ref[...], kr_ref[...]
        ctv, s1v, s2v = ct_ref[...], s1_ref[...], s2_ref[...]
        rq = lax.rsqrt(jnp.mean(cq * cq, axis=-1, keepdims=True) + EPS)
        qn = (cq * rq * qg_ref[...]).astype(BF16)
        q = jnp.dot(qn, wuq_ref[...], preferred_element_type=F32)
        rk = lax.rsqrt(jnp.mean(ckv * ckv, axis=-1, keepdims=True) + EPS)
        kvn = (ckv * rk * kvg_ref[...]).astype(BF16)
        kv = jnp.dot(kvn, wkv_ref[...], preferred_element_type=F32)
        gqv, gkv = gq_ref[...], gk_ref[...]
        dgq = jnp.zeros((1, LANES), F32)
        dgk = jnp.zeros((1, LANES), F32)
        dkr = jnp.zeros((tm, LANES), F32)
        for h in range(MLA_HEADS):
            hs = slice(h * LANES, (h + 1) * LANES)
            xq = q[:, hs]
            _, r = _head_norm(xq, gqv)
            d = _rope_bwd(dq_ref[:, hs], ctv, s1v, s2v)
            dgq = dgq + jnp.sum(d * xq * r, axis=0, keepdims=True)
            dqp[:, hs] = _head_norm_bwd(xq, r, gqv, d)
            xk = kv[:, hs] + kr
            _, r = _head_norm(xk, gkv)
            d = _rope_bwd(dk_ref[:, hs], ctv, s1v, s2v)
            dgk = dgk + jnp.sum(d * xk * r, axis=0, keepdims=True)
            dxk = _head_norm_bwd(xk, r, gkv, d)
            dkvp[:, hs] = dxk
            dkr = dkr + dxk
        dkvp[:, HP:] = dv_ref[...]
        dgq_ref[...] += dgq
        dgk_ref[...] += dgk
        dkr_ref[...] = dkr.astype(BF16)
        tn = (((0,), (0,)), ((), ()))
        nt = (((1,), (1,)), ((), ()))
        dq16 = dqp[...].astype(BF16)
        dwuq_ref[...] += lax.dot_general(qn, dq16, tn, preferred_element_type=F32)
        dqn = lax.dot_general(dq16, wuq_ref[...], nt, preferred_element_type=F32)
        dqg_ref[...] += jnp.sum(dqn * cq * rq, axis=0, keepdims=True)
        u = dqn * qg_ref[...]
        dcq_ref[...] = (rq * u - cq * (rq * rq * rq) * jnp.mean(cq * u, axis=-1, keepdims=True)).astype(BF16)
        dkv16 = dkvp[...].astype(BF16)
        dwkv_ref[...] += lax.dot_general(kvn, dkv16, tn, preferred_element_type=F32)
        dkvn = lax.dot_general(dkv16, wkv_ref[...], nt, preferred_element_type=F32)
        dkvg_ref[...] += jnp.sum(dkvn * ckv * rk, axis=0, keepdims=True)
        u = dkvn * kvg_ref[...]
        dckv_ref[...] = (rk * u - ckv * (rk * rk * rk) * jnp.mean(ckv * u, axis=-1, keepdims=True)).astype(BF16)

    cq_s, ckv_s, kr_s, tab_s = _mla_specs(tm)
    hd = pl.BlockSpec((tm, HP), lambda i: (i, 0))
    return _call(
        body, name=name, grid=(s // tm,),
        in_specs=[cq_s, ckv_s, kr_s, tab_s, tab_s, tab_s, _const((1, MLA_Q_RANK)), _const((1, MLA_KV_RANK)),
                  _const((MLA_Q_RANK, HP)), _const((MLA_KV_RANK, 2 * HP)), _const((1, LANES)), _const((1, LANES)),
                  hd, hd, hd],
        out_specs=[pl.BlockSpec((tm, MLA_Q_RANK), lambda i: (i, 0)), pl.BlockSpec((tm, MLA_KV_RANK), lambda i: (i, 0)),
                   pl.BlockSpec((tm, LANES), lambda i: (i, 0)), _const((MLA_Q_RANK, HP)), _const((MLA_KV_RANK, 2 * HP)),
                   _const((1, MLA_Q_RANK)), _const((1, MLA_KV_RANK)), _const((1, LANES)), _const((1, LANES))],
        out_shape=[jax.ShapeDtypeStruct((s, MLA_Q_RANK), BF16), jax.ShapeDtypeStruct((s, MLA_KV_RANK), BF16),
                   jax.ShapeDtypeStruct((s, LANES), BF16), jax.ShapeDtypeStruct((MLA_Q_RANK, HP), F32),
                   jax.ShapeDtypeStruct((MLA_KV_RANK, 2 * HP), F32), jax.ShapeDtypeStruct((1, MLA_Q_RANK), F32),
                   jax.ShapeDtypeStruct((1, MLA_KV_RANK), F32), jax.ShapeDtypeStruct((1, LANES), F32),
                   jax.ShapeDtypeStruct((1, LANES), F32)],
        scratch_shapes=[pltpu.VMEM((tm, HP), F32), pltpu.VMEM((tm, 2 * HP), F32)],
        args=(p, p, p, ct, s1, s2, qn_g, kvn_g, wuq, wkv, gq, gk, dq, dk, dv), semantics=("arbitrary",), side=side)


_ATT_SCALE = MLA_QK ** -0.5
ATT_BLOCK = 1024
_NEG = -1e30
_NT = (((1,), (1,)), ((), ()))
_TN = (((0,), (0,)), ((), ()))


def _tri_rows(step, n):
    i = step * 0
    for m in range(1, n):
        i = i + (step >= m * (m + 1) // 2).astype(jnp.int32)
    return i, step - i * (i + 1) // 2


def _tri_cols(step, n):
    j = step * 0
    for m in range(1, n):
        j = j + (step >= m * n - m * (m - 1) // 2).astype(jnp.int32)
    return j, j + step - (j * n - j * (j - 1) // 2)


def _diag_mask(t):
    return lax.broadcasted_iota(jnp.int32, (t, t), 0) <= lax.broadcasted_iota(jnp.int32, (t, t), 1)


def _attn_fwd(q, k, v, name, side=None):
    s = q.shape[0]
    t = _pick(s, ATT_BLOCK)
    n = s // t

    def body(q_ref, k_ref, v_ref, o_ref, lse_ref, m_s, l_s, acc):
        i, j = _tri_rows(pl.program_id(1), n)

        @pl.when(j == 0)
        def _():
            m_s[...] = jnp.full_like(m_s, _NEG)
            l_s[...] = jnp.zeros_like(l_s)
            acc[...] = jnp.zeros_like(acc)

        def step(diagonal):
            sc = lax.dot_general(k_ref[...], q_ref[...], _NT, preferred_element_type=F32)
            if diagonal:
                sc = jnp.where(_diag_mask(t), sc, _NEG)
            m_new = jnp.maximum(m_s[...], jnp.max(sc, axis=0, keepdims=True))
            alpha = jnp.exp(m_s[...] - m_new)
            pr = jnp.exp(sc - m_new)
            l_s[...] = alpha * l_s[...] + jnp.sum(pr, axis=0, keepdims=True)
            acc[...] = alpha * acc[...] + lax.dot_general(v_ref[...], pr.astype(BF16), _TN, preferred_element_type=F32)
            m_s[...] = m_new

        @pl.when(j < i)
        def _():
            step(False)

        @pl.when(j == i)
        def _():
            step(True)
            o_ref[...] = (acc[...] / l_s[...]).T
            lse_ref[...] = m_s[...] + jnp.log(l_s[...])

    qs = pl.BlockSpec((t, LANES), lambda h, p: (_tri_rows(p, n)[0], h))
    ks = pl.BlockSpec((t, LANES), lambda h, p: (_tri_rows(p, n)[1], h))
    return _call(
        body, name=name, grid=(MLA_HEADS, n * (n + 1) // 2), in_specs=[qs, ks, ks],
        out_specs=[qs, pl.BlockSpec((None, 1, t), lambda h, p: (h, 0, _tri_rows(p, n)[0]))],
        out_shape=[jax.ShapeDtypeStruct((s, HP), F32), jax.ShapeDtypeStruct((MLA_HEADS, 1, s), F32)],
        scratch_shapes=[pltpu.VMEM((1, t), F32), pltpu.VMEM((1, t), F32), pltpu.VMEM((LANES, t), F32)],
        args=(q, k, v), semantics=("parallel", "arbitrary"), side=side)


def _attn_bwd_dq(q, k, v, o, lse, do, name, side=None):
    s = q.shape[0]
    t = _pick(s, ATT_BLOCK)
    n = s // t

    def body(q_ref, k_ref, v_ref, o_ref, lse_ref, do_ref, dq_ref, dl_ref, acc, dl_s):
        i, j = _tri_rows(pl.program_id(1), n)

        @pl.when(j == 0)
        def _():
            acc[...] = jnp.zeros_like(acc)
            dl_s[...] = jnp.sum((do_ref[...] * o_ref[...]).T, axis=0, keepdims=True)

        def step(diagonal):
            sc = lax.dot_general(k_ref[...], q_ref[...], _NT, preferred_element_type=F32)
            if diagonal:
                sc = jnp.where(_diag_mask(t), sc, _NEG)
            pr = jnp.exp(sc - lse_ref[...])
            dp = lax.dot_general(v_ref[...], do_ref[...].astype(BF16), _NT, preferred_element_type=F32)
            ds = (pr * (dp - dl_s[...])).astype(BF16)
            acc[...] += lax.dot_general(k_ref[...], ds, _TN, preferred_element_type=F32)

        @pl.when(j < i)
        def _():
            step(False)

        @pl.when(j == i)
        def _():
            step(True)
            dq_ref[...] = (acc[...] * _ATT_SCALE).T
            dl_ref[...] = dl_s[...]

    qs = pl.BlockSpec((t, LANES), lambda h, p: (_tri_rows(p, n)[0], h))
    ks = pl.BlockSpec((t, LANES), lambda h, p: (_tri_rows(p, n)[1], h))
    ls = pl.BlockSpec((None, 1, t), lambda h, p: (h, 0, _tri_rows(p, n)[0]))
    return _call(
        body, name=name, grid=(MLA_HEADS, n * (n + 1) // 2), in_specs=[qs, ks, ks, qs, ls, qs], out_specs=[qs, ls],
        out_shape=[jax.ShapeDtypeStruct((s, HP), F32), jax.ShapeDtypeStruct((MLA_HEADS, 1, s), F32)],
        scratch_shapes=[pltpu.VMEM((LANES, t), F32), pltpu.VMEM((1, t), F32)],
        args=(q, k, v, o, lse, do), semantics=("parallel", "arbitrary"), side=side)


def _attn_bwd_dkv(q, k, v, lse, delta, do, name, side=None):
    s = q.shape[0]
    t = _pick(s, ATT_BLOCK)
    n = s // t

    def body(q_ref, k_ref, v_ref, lse_ref, dl_ref, do_ref, dk_ref, dv_ref, dk_acc, dv_acc):
        j, i = _tri_cols(pl.program_id(1), n)

        def step(diagonal):
            sc = lax.dot_general(k_ref[...], q_ref[...], _NT, preferred_element_type=F32)
            if diagonal:
                sc = jnp.where(_diag_mask(t), sc, _NEG)
            pr = jnp.exp(sc - lse_ref[...])
            do16 = do_ref[...].astype(BF16)
            dv_acc[...] += jnp.dot(pr.astype(BF16), do16, preferred_element_type=F32)
            dp = lax.dot_general(v_ref[...], do16, _NT, preferred_element_type=F32)
            ds = (pr * (dp - dl_ref[...])).astype(BF16)
            dk_acc[...] += jnp.dot(ds, q_ref[...], preferred_element_type=F32)

        @pl.when(i == j)
        def _():
            dk_acc[...] = jnp.zeros_like(dk_acc)
            dv_acc[...] = jnp.zeros_like(dv_acc)
            step(True)

        @pl.when(i > j)
        def _():
            step(False)

        @pl.when(i == n - 1)
        def _():
            dk_ref[...] = dk_acc[...]
            dv_ref[...] = dv_acc[...]

    qs = pl.BlockSpec((t, LANES), lambda h, p: (_tri_cols(p, n)[1], h))
    ks = pl.BlockSpec((t, LANES), lambda h, p: (_tri_cols(p, n)[0], h))
    ls = pl.BlockSpec((None, 1, t), lambda h, p: (h, 0, _tri_cols(p, n)[1]))
    return _call(
        body, name=name, grid=(MLA_HEADS, n * (n + 1) // 2), in_specs=[qs, ks, ks, ls, ls, qs], out_specs=[ks, ks],
        out_shape=[jax.ShapeDtypeStruct((s, HP), F32)] * 2,
        scratch_shapes=[pltpu.VMEM((t, LANES), F32), pltpu.VMEM((t, LANES), F32)],
        args=(q, k, v, lse, delta, do), semantics=("parallel", "arbitrary"), side=side)


XBC = HP + 2 * SSD_GROUPS * SSD_STATE
BCW = 2 * SSD_GROUPS * SSD_STATE


def _conv_fwd(p, col0, width, conv_w, conv_b, name):
    s = p.shape[0]
    c0, nblk = col0 // LANES, width // LANES

    def body(x_ref, w_ref, b_ref, o_ref, pad):
        pad[0:8, :] = jnp.zeros((8, LANES), F32)
        pad[8:s + 8, :] = x_ref[...].astype(F32)
        acc = jnp.broadcast_to(b_ref[...], (s, LANES))
        for t in range(SSD_CONV):
            acc = acc + pad[pl.ds(8 - (SSD_CONV - 1) + t, s), :] * w_ref[t:t + 1, :]
        o_ref[...] = acc * _sigmoid(acc)

    return pl.pallas_call(
        body, name=name, grid=(nblk,),
        in_specs=[pl.BlockSpec((s, LANES), lambda j: (0, c0 + j)), pl.BlockSpec((SSD_CONV, LANES), lambda j: (0, j)),
                  pl.BlockSpec((1, LANES), lambda j: (0, j))],
        out_specs=pl.BlockSpec((s, LANES), lambda j: (0, j)), out_shape=jax.ShapeDtypeStruct((s, width), F32),
        scratch_shapes=[pltpu.VMEM((s + 8, LANES), F32)], compiler_params=_params("parallel"))(p, conv_w, conv_b)


def _conv_bwd(p, col0, width, conv_w, conv_b, dact, name):
    s = p.shape[0]
    c0, nblk = col0 // LANES, width // LANES

    def body(x_ref, w_ref, b_ref, d_ref, dx_ref, dw_ref, db_ref, pad, padd):
        pad[0:8, :] = jnp.zeros((8, LANES), F32)
        pad[8:s + 8, :] = x_ref[...].astype(F32)
        acc = jnp.broadcast_to(b_ref[...], (s, LANES))
        for t in range(SSD_CONV):
            acc = acc + pad[pl.ds(8 - (SSD_CONV - 1) + t, s), :] * w_ref[t:t + 1, :]
        sg = _sigmoid(acc)
        dpre = d_ref[...] * (sg * (1.0 + acc * (1.0 - sg)))
        padd[0:s, :] = dpre
        padd[s:s + 8, :] = jnp.zeros((8, LANES), F32)
        dx = jnp.zeros((s, LANES), F32)
        for t in range(SSD_CONV):
            dx = dx + padd[pl.ds(SSD_CONV - 1 - t, s), :] * w_ref[t:t + 1, :]
            dw_ref[t:t + 1, :] = jnp.sum(dpre * pad[pl.ds(8 - (SSD_CONV - 1) + t, s), :], axis=0, keepdims=True)
        dx_ref[...] = dx.astype(BF16)
        db_ref[...] = jnp.sum(dpre, axis=0, keepdims=True)

    blk = pl.BlockSpec((s, LANES), lambda j: (0, j))
    return pl.pallas_call(
        body, name=name, grid=(nblk,),
        in_specs=[pl.BlockSpec((s, LANES), lambda j: (0, c0 + j)), pl.BlockSpec((SSD_CONV, LANES), lambda j: (0, j)),
                  pl.BlockSpec((1, LANES), lambda j: (0, j)), blk],
        out_specs=[blk, pl.BlockSpec((SSD_CONV, LANES), lambda j: (0, j)), pl.BlockSpec((1, LANES), lambda j: (0, j))],
        out_shape=[jax.ShapeDtypeStruct((s, width), BF16), jax.ShapeDtypeStruct((SSD_CONV, width), F32),
                   jax.ShapeDtypeStruct((1, width), F32)],
        scratch_shapes=[pltpu.VMEM((s + 8, LANES), F32), pltpu.VMEM((s + 8, LANES), F32)],
        compiler_params=_params("parallel"))(p, conv_w, conv_b, dact)


def _softplus(x):
    return jnp.maximum(x, 0.0) + jnp.log(1.0 + jnp.exp(-jnp.abs(x)))


def _dt_fwd(p, dt_bias, a_log, name):
    s = p.shape[0]
    tm = _pick(s, 512)

    def body(x_ref, b_ref, a_ref, dt_ref, da_ref):
        dtv = _softplus(x_ref[...] + b_ref[...])
        dav = dtv * (-jnp.exp(a_ref[...]))
        for h in range(SSD_HEADS):
            hs = slice(h * LANES, (h + 1) * LANES)
            dt_ref[:, hs] = jnp.broadcast_to(dtv[:, h:h + 1], (tm, LANES))
            da_ref[:, hs] = jnp.broadcast_to(dav[:, h:h + 1], (tm, LANES))

    out = pl.BlockSpec((tm, HP), lambda i: (i, 0))
    return pl.pallas_call(
        body, name=name, grid=(s // tm,),
        in_specs=[pl.BlockSpec((tm, LANES), lambda i: (i, T_DT // LANES)), _const((1, LANES)), _const((1, LANES))],
        out_specs=[out, out], out_shape=[jax.ShapeDtypeStruct((s, HP), F32)] * 2,
        compiler_params=_params("parallel"))(p, dt_bias, a_log)


def _dt_bwd(p, dt_bias, a_log, dda, ddtx, name):
    s = p.shape[0]
    tm = _pick(s, 512)

    def body(x_ref, b_ref, a_ref, dda_ref, ddtx_ref, dx_ref, db_ref, dal_ref):
        @pl.when(pl.program_id(0) == 0)
        def _():
            db_ref[...] = jnp.zeros_like(db_ref)
            dal_ref[...] = jnp.zeros_like(dal_ref)

        x = x_ref[...] + b_ref[...]
        dtv = _softplus(x)
        av = -jnp.exp(a_ref[...])
        lane = lax.broadcasted_iota(jnp.int32, (tm, LANES), 1)
        pa = jnp.zeros((tm, LANES), F32)
        px = jnp.zeros((tm, LANES), F32)
        for h in range(SSD_HEADS):
            pa = jnp.where(lane == h, dda_ref[:, h * LANES:(h + 1) * LANES], pa)
            px = jnp.where(lane == h, ddtx_ref[:, h * LANES:(h + 1) * LANES], px)
        draw = (pa * av + px) * _sigmoid(x)
        dx_ref[...] = draw.astype(BF16)
        db_ref[...] += jnp.sum(draw, axis=0, keepdims=True)
        dal_ref[...] += jnp.sum(pa * dtv, axis=0, keepdims=True) * av

    hd = pl.BlockSpec((tm, HP), lambda i: (i, 0))
    return pl.pallas_call(
        body, name=name, grid=(s // tm,),
        in_specs=[pl.BlockSpec((tm, LANES), lambda i: (i, T_DT // LANES)), _const((1, LANES)), _const((1, LANES)), hd, hd],
        out_specs=[pl.BlockSpec((tm, LANES), lambda i: (i, 0)), _const((1, LANES)), _const((1, LANES))],
        out_shape=[jax.ShapeDtypeStruct((s, LANES), BF16), jax.ShapeDtypeStruct((1, LANES), F32),
                   jax.ShapeDtypeStruct((1, LANES), F32)],
        compiler_params=_params("arbitrary"))(p, dt_bias, a_log, dda, ddtx)


def _cumsum_rows(x):
    row = lax.broadcasted_iota(jnp.int32, x.shape, 0)
    k = 1
    while k < x.shape[0]:
        x = x + jnp.where(row >= k, pltpu.roll(x, k, 0), 0.0)
        k *= 2
    return x


def _rev_cumsum_rows(x):
    n = x.shape[0]
    row = lax.broadcasted_iota(jnp.int32, x.shape, 0)
    k = 1
    while k < n:
        x = x + jnp.where(row < n - k, pltpu.roll(x, n - k, 0), 0.0)
        k *= 2
    return x


HPG = SSD_HEADS // SSD_GROUPS


def _chunk_decay(da):
    cs = _cumsum_rows(da)
    lm = jnp.exp(jnp.where(_tril_mask(), cs - cs.T, _NEG))
    return cs, lm, cs[CHUNK - 1:CHUNK, :]


def _scan_fwd(xs, bc, dtb, dab, name, side=None):
    s = xs.shape[0]
    nc = s // CHUNK

    def body(x_ref, b_ref, c_ref, dt_ref, da_ref, y_ref, sin_ref, state):
        @pl.when(pl.program_id(1) == 0)
        def _():
            state[...] = jnp.zeros_like(state)

        bv = b_ref[...]
        b16, c16 = bv.astype(BF16), c_ref[...].astype(BF16)
        g = lax.dot_general(c16, b16, _NT, preferred_element_type=F32)
        for hh in range(HPG):
            hs = slice(hh * LANES, (hh + 1) * LANES)
            st = state[hh]
            sin_ref[hh] = st
            cs, lm, cl = _chunk_decay(da_ref[:, hs])
            xd = (x_ref[:, hs] * dt_ref[:, hs]).astype(BF16)
            y = jnp.dot((g * lm).astype(BF16), xd, preferred_element_type=F32)
            y_ref[:, hs] = y + jnp.dot(c16, st.astype(BF16), preferred_element_type=F32) * jnp.exp(cs)
            bd = (bv * jnp.exp(cl - cs)).astype(BF16)
            state[hh] = jnp.exp(cl) * st + lax.dot_general(bd, xd, _TN, preferred_element_type=F32)

    gw = HPG * LANES
    hd = pl.BlockSpec((CHUNK, gw), lambda g, c: (c, g))
    return _call(
        body, name=name, grid=(SSD_GROUPS, nc),
        in_specs=[hd, pl.BlockSpec((CHUNK, LANES), lambda g, c: (c, g)),
                  pl.BlockSpec((CHUNK, LANES), lambda g, c: (c, SSD_GROUPS + g)), hd, hd],
        out_specs=[hd, pl.BlockSpec((HPG, None, SSD_STATE, LANES), lambda g, c: (g, c, 0, 0))],
        out_shape=[jax.ShapeDtypeStruct((s, HP), F32), jax.ShapeDtypeStruct((SSD_HEADS, nc, SSD_STATE, LANES), F32)],
        scratch_shapes=[pltpu.VMEM((HPG, SSD_STATE, LANES), F32)],
        args=(xs, bc, bc, dtb, dab), semantics=("parallel", "arbitrary"), side=side)


def _scan_bwd(xs, bc, dtb, dab, s_in, dy, d_vec, name):
    s = xs.shape[0]
    nc = s // CHUNK

    def body(x_ref, b_ref, c_ref, dt_ref, da_ref, sin_ref, dy_ref, dv_ref, dx_ref, db_ref, dc_ref, dda_ref, ddtx_ref, dstate):
        @pl.when(pl.program_id(1) == 0)
        def _():
            dstate[...] = jnp.zeros_like(dstate)

        bv = b_ref[...]
        b16, c16 = bv.astype(BF16), c_ref[...].astype(BF16)
        g = lax.dot_general(c16, b16, _NT, preferred_element_type=F32)
        row = lax.broadcasted_iota(jnp.int32, (CHUNK, 1), 0)
        dbm = jnp.zeros((CHUNK, SSD_STATE), F32)
        dcm = jnp.zeros((CHUNK, SSD_STATE), F32)
        for hh in range(HPG):
            hs = slice(hh * LANES, (hh + 1) * LANES)
            st, ds = sin_ref[hh], dstate[hh]
            st16, ds16 = st.astype(BF16), ds.astype(BF16)
            xv, dtv, dyv = x_ref[:, hs], dt_ref[:, hs], dy_ref[:, hs]
            cs, lm, cl = _chunk_decay(da_ref[:, hs])
            ecs, ecl = jnp.exp(cs), jnp.exp(cl)
            decay = jnp.exp(cl - cs)
            xd = (xv * dtv).astype(BF16)
            dy16 = dyv.astype(BF16)
            dye = (dyv * ecs).astype(BF16)
            yoff = jnp.dot(c16, st16, preferred_element_type=F32) * ecs
            dcs = jnp.sum(dyv * yoff, axis=-1, keepdims=True)
            dcm = dcm + lax.dot_general(dye, st16, _NT, preferred_element_type=F32)
            dstate[hh] = ecl * ds + lax.dot_general(c16, dye, _TN, preferred_element_type=F32)
            dcl = jnp.sum(jnp.sum(ds * st, axis=0, keepdims=True), axis=1, keepdims=True) * ecl[:, 0:1]
            bd32 = bv * decay
            qm = lax.dot_general(xd, ds16, _NT, preferred_element_type=F32)
            dbm = dbm + qm * decay
            w = jnp.sum(bd32 * qm, axis=-1, keepdims=True)
            dcs = dcs - w
            dcl = dcl + jnp.sum(w, axis=0, keepdims=True)
            dxd = jnp.dot(bd32.astype(BF16), ds16, preferred_element_type=F32)
            m16 = (g * lm).astype(BF16)
            dm = lax.dot_general(dy16, xd, _NT, preferred_element_type=F32)
            dxd = dxd + lax.dot_general(m16, dy16, _TN, preferred_element_type=F32)
            dg = dm * lm
            dg16 = dg.astype(BF16)
            tt = dg * g
            dcm = dcm + jnp.dot(dg16, b16, preferred_element_type=F32)
            dbm = dbm + lax.dot_general(dg16, c16, _TN, preferred_element_type=F32)
            dcs = dcs + jnp.sum(tt, axis=-1, keepdims=True) - jnp.sum(tt.T, axis=-1, keepdims=True)
            dcs = dcs + jnp.where(row == CHUNK - 1, dcl, 0.0)
            dda_ref[:, hs] = _rev_cumsum_rows(jnp.broadcast_to(dcs, (CHUNK, LANES)))
            ddtx_ref[:, hs] = jnp.broadcast_to(jnp.sum(dxd * xv, axis=-1, keepdims=True), (CHUNK, LANES))
            dx_ref[:, hs] = dxd * dtv + dyv * dv_ref[:, hs]
        db_ref[...] = dbm
        dc_ref[...] = dcm

    gw = HPG * LANES
    hd = pl.BlockSpec((CHUNK, gw), lambda g, c: (nc - 1 - c, g))
    gp = pl.BlockSpec((CHUNK, LANES), lambda g, c: (nc - 1 - c, g))
    return pl.pallas_call(
        body, name=name, grid=(SSD_GROUPS, nc),
        in_specs=[hd, gp, pl.BlockSpec((CHUNK, LANES), lambda g, c: (nc - 1 - c, SSD_GROUPS + g)), hd, hd,
                  pl.BlockSpec((HPG, None, SSD_STATE, LANES), lambda g, c: (g, nc - 1 - c, 0, 0)), hd,
                  pl.BlockSpec((1, gw), lambda g, c: (0, g))],
        out_specs=[hd, gp, gp, hd, hd],
        out_shape=[jax.ShapeDtypeStruct((s, HP), F32), jax.ShapeDtypeStruct((s, SSD_GROUPS * SSD_STATE), F32),
                   jax.ShapeDtypeStruct((s, SSD_GROUPS * SSD_STATE), F32), jax.ShapeDtypeStruct((s, HP), F32),
                   jax.ShapeDtypeStruct((s, HP), F32)],
        scratch_shapes=[pltpu.VMEM((HPG, SSD_STATE, LANES), F32)],
        compiler_params=_params("parallel", "arbitrary"))(xs, bc, bc, dtb, dab, s_in, dy, d_vec)


_GN = SSD_INNER // SSD_GROUPS
_GW = HP // SSD_GROUPS


def _ssd_post_fwd(y, xbc, p, d_vec, gain, name):
    s = y.shape[0]
    tm = _pick(s, 512)

    def body(y_ref, x_ref, z_ref, d_ref, g_ref, o_ref):
        z = z_ref[...].astype(F32)
        y2 = (y_ref[...] + x_ref[...] * d_ref[...]) * (z * _sigmoid(z))
        for g in range(SSD_GROUPS):
            gs = slice(g * _GW, (g + 1) * _GW)
            yg = y2[:, gs]
            r = lax.rsqrt(jnp.sum(yg * yg, axis=-1, keepdims=True) * (1.0 / _GN) + EPS)
            o_ref[:, gs] = (yg * r * g_ref[:, gs]).astype(BF16)

    hd = pl.BlockSpec((tm, HP), lambda i: (i, 0))
    return pl.pallas_call(
        body, name=name, grid=(s // tm,),
        in_specs=[hd, hd, pl.BlockSpec((tm, HP), lambda i: (i, C_Z // HP)), _const((1, HP)), _const((1, HP))],
        out_specs=hd, out_shape=jax.ShapeDtypeStruct((s, HP), BF16), compiler_params=_params("parallel"))(y, xbc, p, d_vec, gain)


def _ssd_post_bwd(y, xbc, p, d_vec, gain, dyn, name):
    s = y.shape[0]
    tm = _pick(s, 512)

    def body(y_ref, x_ref, z_ref, d_ref, g_ref, dn_ref, dy_ref, dz_ref, dg_ref, dd_ref):
        @pl.when(pl.program_id(0) == 0)
        def _():
            dg_ref[...] = jnp.zeros_like(dg_ref)
            dd_ref[...] = jnp.zeros_like(dd_ref)

        z, xv = z_ref[...].astype(F32), x_ref[...]
        sg = _sigmoid(z)
        sz = z * sg
        yt = y_ref[...] + xv * d_ref[...]
        y2 = yt * sz
        for g in range(SSD_GROUPS):
            gs = slice(g * _GW, (g + 1) * _GW)
            yg, dn = y2[:, gs], dn_ref[:, gs]
            r = lax.rsqrt(jnp.sum(yg * yg, axis=-1, keepdims=True) * (1.0 / _GN) + EPS)
            u = dn * g_ref[:, gs]
            dy2 = r * u - yg * (r * r * r) * (jnp.sum(yg * u, axis=-1, keepdims=True) * (1.0 / _GN))
            dg_ref[:, gs] += jnp.sum(dn * yg * r, axis=0, keepdims=True)
            dyt = dy2 * sz[:, gs]
            dy_ref[:, gs] = dyt
            dz_ref[:, gs] = (dy2 * yt[:, gs] * (sg[:, gs] * (1.0 + z[:, gs] * (1.0 - sg[:, gs])))).astype(BF16)
            dd_ref[:, gs] += jnp.sum(dyt * xv[:, gs], axis=0, keepdims=True)

    hd = pl.BlockSpec((tm, HP), lambda i: (i, 0))
    return pl.pallas_call(
        body, name=name, grid=(s // tm,),
        in_specs=[hd, hd, pl.BlockSpec((tm, HP), lambda i: (i, C_Z // HP)), _const((1, HP)), _const((1, HP)), hd],
        out_specs=[hd, hd, _const((1, HP)), _const((1, HP))],
        out_shape=[jax.ShapeDtypeStruct((s, HP), F32), jax.ShapeDtypeStruct((s, HP), BF16),
                   jax.ShapeDtypeStruct((1, HP), F32), jax.ShapeDtypeStruct((1, HP), F32)],
        compiler_params=_params("arbitrary"))(y, xbc, p, d_vec, gain, dyn)


def _merge_fwd(p, ya, o, yc, wb0, wb1, wb2, w_out, x, name, side=None):
    s = p.shape[0]
    tm = _pick(s, 512)

    def body(g_ref, ya_ref, o_ref, yc_ref, w0_ref, w1_ref, w2_ref, wo_ref, x_ref, mg_ref, y_ref):
        acc = jnp.zeros((tm, D_MODEL), F32)
        for i, (b_ref, w_ref) in enumerate(((ya_ref, w0_ref), (o_ref, w1_ref), (yc_ref, w2_ref))):
            t = jnp.dot(b_ref[...].astype(BF16), w_ref[...], preferred_element_type=F32)
            acc = acc + _sigmoid(g_ref[:, i * D_MODEL:(i + 1) * D_MODEL].astype(F32)) * t
        mg = acc.astype(BF16)
        mg_ref[...] = mg
        y_ref[...] = x_ref[...] + jnp.dot(mg, wo_ref[...], preferred_element_type=F32)

    row = pl.BlockSpec((tm, D_MODEL), lambda i: (i, 0))
    return _call(
        body, name=name, grid=(s // tm,),
        in_specs=[pl.BlockSpec((tm, 3 * D_MODEL), lambda i: (i, C_G // (3 * D_MODEL))),
                  pl.BlockSpec((tm, GM_WIDTH), lambda i: (i, 0)), row, row,
                  _resident((GM_WIDTH, D_MODEL)), _resident((HP, D_MODEL)), _resident((HP, D_MODEL)),
                  _resident((D_MODEL, D_MODEL)), row],
        out_specs=[row, row],
        out_shape=[jax.ShapeDtypeStruct((s, D_MODEL), BF16), jax.ShapeDtypeStruct((s, D_MODEL), F32)],
        scratch_shapes=[], args=(p, ya, o, yc, wb0, wb1, wb2, w_out, x), semantics=("parallel",), side=side)


def _merge_bwd(p, ya, o, yc, wb0, wb1, wb2, w_out, dy, name):
    s = p.shape[0]
    tm = _pick(s, 512)

    def body(g_ref, ya_ref, o_ref, yc_ref, w0_ref, w1_ref, w2_ref, wo_ref, dy_ref,
             d0_ref, d1_ref, d2_ref, dg_ref, dya_ref, do_ref, dyc_ref):
        dm = lax.dot_general(dy_ref[...].astype(BF16), wo_ref[...], _NT, preferred_element_type=F32)
        for i, (b_ref, w_ref, d_ref, db_ref) in enumerate(((ya_ref, w0_ref, d0_ref, dya_ref), (o_ref, w1_ref, d1_ref, do_ref),
                                                            (yc_ref, w2_ref, d2_ref, dyc_ref))):
            cs = slice(i * D_MODEL, (i + 1) * D_MODEL)
            t = jnp.dot(b_ref[...].astype(BF16), w_ref[...], preferred_element_type=F32)
            sg = _sigmoid(g_ref[:, cs].astype(F32))
            dt16 = (dm * sg).astype(BF16)
            d_ref[...] = dt16
            dg_ref[:, cs] = (dm * t * sg * (1.0 - sg)).astype(BF16)
            db_ref[...] = lax.dot_general(dt16, w_ref[...], _NT, preferred_element_type=F32)

    row = pl.BlockSpec((tm, D_MODEL), lambda i: (i, 0))
    nar = pl.BlockSpec((tm, GM_WIDTH), lambda i: (i, 0))
    wide = pl.BlockSpec((tm, 3 * D_MODEL), lambda i: (i, 0))
    return pl.pallas_call(
        body, name=name, grid=(s // tm,),
        in_specs=[pl.BlockSpec((tm, 3 * D_MODEL), lambda i: (i, C_G // (3 * D_MODEL))), nar, row, row,
                  _resident((GM_WIDTH, D_MODEL)), _resident((HP, D_MODEL)), _resident((HP, D_MODEL)),
                  _resident((D_MODEL, D_MODEL)), row],
        out_specs=[row, row, row, wide, nar, row, row],
        out_shape=[jax.ShapeDtypeStruct((s, D_MODEL), BF16)] * 3 + [jax.ShapeDtypeStruct((s, 3 * D_MODEL), BF16),
                   jax.ShapeDtypeStruct((s, GM_WIDTH), F32), jax.ShapeDtypeStruct((s, D_MODEL), F32),
                   jax.ShapeDtypeStruct((s, D_MODEL), F32)],
        compiler_params=_params("parallel"))(p, ya, o, yc, wb0, wb1, wb2, w_out, dy)


def _loss_head(y, target, name):
    s, d = y.shape
    tm = _pick(s, 512)

    def body(y_ref, t_ref, dy_ref, sq_ref):
        @pl.when(pl.program_id(0) == 0)
        def _():
            sq_ref[...] = jnp.zeros_like(sq_ref)

        e = y_ref[...] - t_ref[...]
        dy_ref[...] = e * (1.0 / d)
        sq_ref[...] += jnp.sum(e * e, axis=0, keepdims=True)

    row = pl.BlockSpec((tm, d), lambda i: (i, 0))
    return pl.pallas_call(
        body, name=name, grid=(s // tm,), in_specs=[row, row], out_specs=[row, _const((1, d))],
        out_shape=[jax.ShapeDtypeStruct((s, d), F32), jax.ShapeDtypeStruct((1, d), F32)],
        compiler_params=_params("arbitrary"))(y, target)


def _adamw(w, g, m, v, name):
    rows, cols = w.shape
    tr = rows
    for cand in (512, 256, 128, 64, 32, 16, 8):
        if rows % cand == 0 and cand * cols * 4 <= 3 * 1024 * 1024:
            tr = cand
            break

    def body(w_ref, g_ref, m_ref, v_ref, d_ref, nm_ref, nv_ref):
        d_ref[...], nm_ref[...], nv_ref[...] = _adam_update(w_ref[...], g_ref[...], m_ref[...], v_ref[...])

    blk = pl.BlockSpec((tr, cols), lambda i: (i, 0))
    return pl.pallas_call(
        body, name=name, grid=(rows // tr,), in_specs=[blk] * 4, out_specs=[blk] * 3,
        out_shape=[jax.ShapeDtypeStruct((rows, cols), F32)] * 3, compiler_params=_params("parallel"))(w, g, m, v)


def _adam_update(w, g, m, v):
    nm = ADAM_B1 * m + (1.0 - ADAM_B1) * g
    nv = ADAM_B2 * v + (1.0 - ADAM_B2) * (g * g)
    c1 = 1.0 - ADAM_B1 ** ADAM_STEP
    c2 = 1.0 - ADAM_B2 ** ADAM_STEP
    return -ADAM_LR * ((nm / c1) / (jnp.sqrt(nv / c2) + ADAM_EPS) + ADAM_WD * w), nm, nv


def _adamw_sharded(w, m, v, mine, theirs, name, side=None):
    depth, rows, cols = w.shape
    tr = _row_tile(rows // 2, cols, 1024 * 1024)
    nb = rows // 2 // tr

    def body(w_ref, m_ref, v_ref, a_ref, b_ref, g_ref, d_ref, nm_ref, nv_ref):
        c = lax.axis_index("c")
        g = jnp.where(pl.program_id(1) // nb == c, a_ref[...], b_ref[...])
        g_ref[...] = g
        d_ref[...], nm_ref[...], nv_ref[...] = _adam_update(w_ref[...], g, m_ref[...], v_ref[...])

    blk = pl.BlockSpec((None, tr, cols), lambda l, i: (l, i, 0))
    half = pl.BlockSpec((None, tr, cols), lambda l, i: (l, i % nb, 0))
    return _call(
        body, name=name, grid=(depth, rows // tr), in_specs=[blk, blk, blk, half, half], out_specs=[blk] * 4,
        out_shape=[jax.ShapeDtypeStruct((depth, rows, cols), F32)] * 4, scratch_shapes=[],
        args=(w, m, v, mine, theirs), semantics=("parallel", "parallel"), side=side)


ANY = pl.BlockSpec(memory_space=pl.ANY)


def _me():
    return lax.axis_index("x"), lax.axis_index("y"), lax.axis_index("c")


def _other_chips(x, y):
    return [(1 - x, y), (x, 1 - y), (1 - x, 1 - y)]


def _chip_index(cx, cy):
    return 2 * cx + cy


class _Exchange:
    def __init__(self, ins, out_shapes, n_sems, start, finish):
        self.ins, self.out_shapes, self.n_sems, self.start, self.finish = list(ins), list(out_shapes), n_sems, start, finish


def _sem_scratch(ex):
    return [pltpu.SemaphoreType.DMA((ex.n_sems,)), pltpu.SemaphoreType.DMA((ex.n_sems,))]


def _run_exchange(ex, name):
    n_in, n_out = len(ex.ins), len(ex.out_shapes)

    def body(*refs):
        in_refs, out_refs, (send, recv) = refs[:n_in], refs[n_in:n_in + n_out], refs[n_in + n_out:]
        ex.start(in_refs, out_refs, send, recv)
        ex.finish(in_refs, out_refs, send, recv)

    return pl.pallas_call(body, name=name, in_specs=[ANY] * n_in, out_specs=[ANY] * n_out, out_shape=ex.out_shapes,
                          scratch_shapes=_sem_scratch(ex))(*ex.ins)


def _call(body, *, name, grid, in_specs, out_specs, out_shape, scratch_shapes, args, semantics, side=None):
    if side is None:
        return pl.pallas_call(body, name=name, grid=grid, in_specs=in_specs, out_specs=out_specs, out_shape=out_shape,
                              scratch_shapes=scratch_shapes, compiler_params=_params(*semantics))(*args), []
    n_in, n_out, n_sc = len(in_specs), len(out_specs), len(scratch_shapes)
    s_in, s_out = len(side.ins), len(side.out_shapes)

    def hosted(*refs):
        pos = 0
        parts = []
        for size in (n_in, s_in, n_out, s_out, n_sc, 2):
            parts.append(refs[pos:pos + size])
            pos += size
        ins, sins, outs, souts, scratch, (send, recv) = parts
        ids = [pl.program_id(a) for a in range(len(grid))]
        first = functools.reduce(jnp.logical_and, [i == 0 for i in ids])
        last = functools.reduce(jnp.logical_and, [i == g - 1 for i, g in zip(ids, grid)])

        @pl.when(first)
        def _():
            side.start(sins, souts, send, recv)

        body(*ins, *outs, *scratch)

        @pl.when(last)
        def _():
            side.finish(sins, souts, send, recv)

    res = pl.pallas_call(
        hosted, name=name, grid=grid, in_specs=list(in_specs) + [ANY] * s_in, out_specs=list(out_specs) + [ANY] * s_out,
        out_shape=list(out_shape) + side.out_shapes, scratch_shapes=list(scratch_shapes) + _sem_scratch(side),
        compiler_params=_params(*["arbitrary"] * len(grid)))(*args, *side.ins)
    return res[:n_out], res[n_out:]


def _half(ref_rows, c):
    return pl.ds(c * (ref_rows // 2), ref_rows // 2)


def _gather_exchange(shards, layer):
    n = len(shards)
    rows = [a.shape[1] for a in shards]

    def copy(in_refs, out_refs, send, recv, t, k, chip, hc, to, from_input=False):
        dst = out_refs[t].at[chip, _half(rows[t], hc)]
        src = in_refs[t].at[layer, _half(rows[t], hc)] if from_input else dst
        return pltpu.make_async_remote_copy(src_ref=src, dst_ref=dst, send_sem=send.at[7 * t + k], recv_sem=recv.at[7 * t + k],
                                            device_id=to, device_id_type=MESH)

    def own(in_refs, out_refs, send, recv, t):
        x, y, c = _me()
        return pltpu.make_async_remote_copy(src_ref=in_refs[t].at[layer], dst_ref=out_refs[t].at[_chip_index(x, y)],
                                            send_sem=send.at[7 * t + 6], recv_sem=recv.at[7 * t + 6],
                                            device_id=(x, y, 1 - c), device_id_type=MESH)

    def start(in_refs, out_refs, send, recv):
        x, y, c = _me()
        for j, chip in enumerate(_other_chips(x, y)):
            for t in range(n):
                copy(in_refs, out_refs, send, recv, t, j, _chip_index(x, y), c, (*chip, c), from_input=True).start()
        for t in range(n):
            own(in_refs, out_refs, send, recv, t).start()

    def finish(in_refs, out_refs, send, recv):
        x, y, c = _me()
        chips = _other_chips(x, y)
        passed = []
        for t in range(n):
            own(in_refs, out_refs, send, recv, t).wait()
        for j, chip in enumerate(chips):
            for t in range(n):
                copy(in_refs, out_refs, send, recv, t, j, _chip_index(*chip), c, (x, y, c)).wait_recv()
                cp = copy(in_refs, out_refs, send, recv, t, 3 + j, _chip_index(*chip), c, (x, y, 1 - c))
                cp.start()
                passed.append(cp)
        for j, chip in enumerate(chips):
            for t in range(n):
                copy(in_refs, out_refs, send, recv, t, 3 + j, _chip_index(*chip), 1 - c, (x, y, c)).wait_recv()
                copy(in_refs, out_refs, send, recv, t, j, _chip_index(x, y), c, (*chip, c), from_input=True).wait_send()
        for cp in passed:
            cp.wait_send()

    return _Exchange(shards, [jax.ShapeDtypeStruct((N_CHIPS,) + a.shape[1:], a.dtype) for a in shards], 7 * n, start, finish)


def _pair_exchange(gs):
    n = len(gs)
    rows = [a.shape[1] for a in gs]

    def copies(in_refs, out_refs, send, recv):
        x, y, c = _me()
        return [pltpu.make_async_remote_copy(src_ref=in_refs[t].at[:, _half(rows[t], 1 - c)], dst_ref=out_refs[t],
                                             send_sem=send.at[t], recv_sem=recv.at[t], device_id=(x, y, 1 - c),
                                             device_id_type=MESH) for t in range(n)]

    def start(*refs):
        for cp in copies(*refs):
            cp.start()

    def finish(*refs):
        for cp in copies(*refs):
            cp.wait()

    return _Exchange(gs, [jax.ShapeDtypeStruct((N_CHIPS, a.shape[1] // 2, a.shape[2]), a.dtype) for a in gs], n, start, finish)


def _row_tile(rows, cols, budget=2 * 1024 * 1024):
    best = None
    for t in range(8, rows + 1, 8):
        if rows % t == 0 and t * cols * 4 <= budget:
            best = t
    return best or rows


def _pair_add(g, got, name):
    _, rows, cols = g.shape
    tr = _row_tile(rows // 2, cols)
    nb = rows // 2 // tr

    def body(lo_ref, hi_ref, r_ref, o16_ref, own_ref):
        x, y, c = _me()
        tot = jnp.where(c == 0, lo_ref[...], hi_ref[...]) + r_ref[...]
        o16_ref[...] = tot.astype(BF16)

        @pl.when(pl.program_id(1) == _chip_index(x, y))
        def _():
            own_ref[...] = tot

    blk = (None, tr, cols)
    return pl.pallas_call(
        body, name=name, grid=(nb, N_CHIPS),
        in_specs=[pl.BlockSpec(blk, lambda i, k: (k, i, 0)), pl.BlockSpec(blk, lambda i, k: (k, i + nb, 0)),
                  pl.BlockSpec(blk, lambda i, k: (k, i, 0))],
        out_specs=[pl.BlockSpec(blk, lambda i, k: (k, i, 0)), pl.BlockSpec((tr, cols), lambda i, k: (i, 0))],
        out_shape=[jax.ShapeDtypeStruct((N_CHIPS, rows // 2, cols), BF16), jax.ShapeDtypeStruct((rows // 2, cols), F32)],
        compiler_params=_params("parallel", "arbitrary"))(g, g, got)


def _chip_exchange(parts):
    n = len(parts)

    def copies(in_refs, out_refs, send, recv):
        x, y, c = _me()
        return [pltpu.make_async_remote_copy(src_ref=in_refs[t].at[_chip_index(*chip)], dst_ref=out_refs[t].at[j],
                                             send_sem=send.at[3 * t + j], recv_sem=recv.at[3 * t + j],
                                             device_id=(*chip, c), device_id_type=MESH)
                for j, chip in enumerate(_other_chips(x, y)) for t in range(n)]

    def start(*refs):
        for cp in copies(*refs):
            cp.start()

    def finish(*refs):
        for cp in copies(*refs):
            cp.wait()

    return _Exchange(parts, [jax.ShapeDtypeStruct((3,) + a.shape[1:], a.dtype) for a in parts], 3 * n, start, finish)


def _chip_add(own, got, name):
    rows, cols = own.shape
    tr = _row_tile(rows, cols, 1024 * 1024)

    def body(own_ref, got_ref, o_ref):
        acc = own_ref[...]
        for j in range(3):
            acc = acc + got_ref[j].astype(F32)
        o_ref[...] = acc

    return pl.pallas_call(
        body, name=name, grid=(rows // tr,),
        in_specs=[pl.BlockSpec((tr, cols), lambda i: (i, 0)), pl.BlockSpec((3, tr, cols), lambda i: (0, i, 0))],
        out_specs=pl.BlockSpec((tr, cols), lambda i: (i, 0)),
        out_shape=jax.ShapeDtypeStruct((rows, cols), F32), compiler_params=_params("parallel"))(own, got)


def _pair_share(halves):
    n = len(halves)

    def copies(in_refs, out_refs, send, recv):
        x, y, c = _me()
        return [pltpu.make_async_remote_copy(src_ref=in_refs[t], dst_ref=out_refs[t], send_sem=send.at[t],
                                             recv_sem=recv.at[t], device_id=(x, y, 1 - c), device_id_type=MESH)
                for t in range(n)]

    def start(*refs):
        for cp in copies(*refs):
            cp.start()

    def finish(*refs):
        for cp in copies(*refs):
            cp.wait()

    return _Exchange(halves, [jax.ShapeDtypeStruct(a.shape, a.dtype) for a in halves], n, start, finish)


N_DEV = 8


def _all_exchange(v):
    r, cols = v.shape

    def peers():
        x, y, c = _me()
        flip = lambda v, f: 1 - v if f else v
        return 4 * x + 2 * y + c, [(flip(x, fx), flip(y, fy), flip(c, fc)) for fx in (0, 1) for fy in (0, 1) for fc in (0, 1)][1:]

    def local(in_refs, out_refs, send, me):
        return pltpu.make_async_copy(in_refs[0], out_refs[0].at[me], send.at[7])

    def start(in_refs, out_refs, send, recv):
        me, others = peers()
        local(in_refs, out_refs, send, me).start()
        for j, peer in enumerate(others):
            pltpu.make_async_remote_copy(src_ref=in_refs[0], dst_ref=out_refs[0].at[me], send_sem=send.at[j],
                                         recv_sem=recv.at[j], device_id=peer, device_id_type=MESH).start()

    def finish(in_refs, out_refs, send, recv):
        me, others = peers()
        for j, (px, py, pc) in enumerate(others):
            pltpu.make_async_remote_copy(src_ref=in_refs[0], dst_ref=out_refs[0].at[4 * px + 2 * py + pc], send_sem=send.at[j],
                                         recv_sem=recv.at[j], device_id=(px, py, pc), device_id_type=MESH).wait()
        local(in_refs, out_refs, send, me).wait()

    return _Exchange([v], [jax.ShapeDtypeStruct((N_DEV, r, cols), v.dtype)], 8, start, finish)


def _sum_slots(a, name):
    n, r, cols = a.shape
    tr = _pick(r, 512) if r % 8 == 0 else r
    for cand in (512, 256, 128, 64, 32, 16, 8):
        if r % cand == 0:
            tr = cand
            break

    def body(a_ref, o_ref):
        acc = a_ref[0]
        for k in range(1, n):
            acc = acc + a_ref[k]
        o_ref[...] = acc

    return pl.pallas_call(
        body, name=name, grid=(r // tr,), in_specs=[pl.BlockSpec((n, tr, cols), lambda i: (0, i, 0))],
        out_specs=pl.BlockSpec((tr, cols), lambda i: (i, 0)), out_shape=jax.ShapeDtypeStruct((r, cols), F32),
        compiler_params=_params("parallel"))(a)


def _join(name, stacked):
    ax = SHARDED[name][1]
    return jnp.concatenate([stacked[k] for k in range(N_CHIPS)], axis=ax)


def _split(name, full):
    ax = SHARDED[name][1]
    return jnp.stack(jnp.split(full, N_CHIPS, axis=ax))


def _heads_pad(a, real, axis):
    shp = a.shape
    a = a.reshape(shp[:axis] + (MLA_HEADS, real) + shp[axis + 1:])
    pad = [(0, 0)] * a.ndim
    pad[axis + 1] = (0, LANES - real)
    a = jnp.pad(a, pad)
    return a.reshape(shp[:axis] + (HP,) + shp[axis + 1:])


def _heads_unpad(a, real, axis):
    shp = a.shape
    a = a.reshape(shp[:axis] + (MLA_HEADS, LANES) + shp[axis + 1:])
    a = lax.slice_in_dim(a, 0, real, axis=axis + 1)
    return a.reshape(shp[:axis] + (MLA_HEADS * real,) + shp[axis + 1:])


def _lane_place(a, start):
    n = a.shape[-1]
    pad = [(0, 0)] * (a.ndim - 1) + [(start, LANES - start - n)]
    return jnp.pad(a, pad)


_O_UV, _O_CQ, _O_CKV, _O_KR, _O_Z, _O_XBC, _O_DT, _O_G = 0, 1024, 1408, 1664, 1696, 2208, 3232, 3240


def _w_in_pad(w):
    sl = lambda a, b: w[:, a:b]
    xs = _heads_pad(sl(_O_XBC, _O_XBC + SSD_INNER), SSD_HEAD_DIM, 1)
    bc = sl(_O_XBC + SSD_INNER, _O_DT)
    main = jnp.concatenate([sl(_O_UV, _O_CQ), _heads_pad(sl(_O_Z, _O_XBC), SSD_HEAD_DIM, 1), xs, sl(_O_G, IN_COLS)], axis=1)
    tail = jnp.concatenate([bc, sl(_O_CKV, _O_KR), sl(_O_CQ, _O_CKV), _lane_place(sl(_O_KR, _O_Z), MLA_NOPE),
                            _lane_place(sl(_O_DT, _O_G), 0), jnp.zeros((w.shape[0], PW_TAIL - T_DT - LANES), w.dtype)], axis=1)
    return main, tail


def _w_in_unpad(gm, gt):
    m = lambda a, n: gm[:, a:a + n]
    t = lambda a, n: gt[:, a:a + n]
    parts = [m(C_UV, 1024), t(T_CQ, MLA_Q_RANK), t(T_CKV, MLA_KV_RANK), t(T_KR + MLA_NOPE, MLA_ROPE),
             _heads_unpad(m(C_Z, HP), SSD_HEAD_DIM, 1), _heads_unpad(m(C_XS, HP), SSD_HEAD_DIM, 1), t(T_BC, BCW),
             t(T_DT, SSD_HEADS), m(C_G, 3 * D_MODEL)]
    return jnp.concatenate(parts, axis=1)


def _xbc_pad(a):
    return jnp.concatenate([_heads_pad(a[..., :SSD_INNER], SSD_HEAD_DIM, a.ndim - 1), a[..., SSD_INNER:]], axis=-1)


def _xbc_unpad(a):
    return jnp.concatenate([_heads_unpad(a[..., :HP], SSD_HEAD_DIM, a.ndim - 1), a[..., HP:]], axis=-1)


def _rope_tables(positions):
    inv_freq = 1.0 / (ROPE_THETA ** (jnp.arange(0, MLA_ROPE, 2, dtype=F32) / MLA_ROPE))
    ang = positions.astype(F32)[:, None] * inv_freq
    cos, sin = jnp.cos(ang), jnp.sin(ang)
    s = positions.shape[0]
    half = MLA_ROPE // 2
    z = lambda n: jnp.zeros((s, n), F32)
    ct = jnp.concatenate([jnp.ones((s, MLA_NOPE), F32), cos, cos, z(LANES - MLA_QK)], axis=1)
    s1 = jnp.concatenate([z(MLA_NOPE), -sin, z(half), z(LANES - MLA_QK)], axis=1)
    s2 = jnp.concatenate([z(MLA_NOPE), z(half), sin, z(LANES - MLA_QK)], axis=1)
    return ct, s1, s2


def _layer_weights(full, small, l, part):
    w = {}
    row = lambda n: small[n][l][None, :]
    stacked = lambda g: g.reshape((N_CHIPS * g.shape[1], g.shape[2]))
    if part in ('ffn1', 'ffn2'):
        w[part + '_w_in'] = full[part + '_w_in']
        w[part + '_w_out'] = stacked(full[part + '_w_out'])
        w[part + '_norm'] = row(part + '_norm')
        return w
    w['w_out'] = stacked(full['w_out'])
    fl = {n: _join(n, full[n]) for n in ('w_in', 'mla_w_uq', 'mla_w_ukv', 'w_branch', 'ssd_conv_w')}
    w['w_in_main'], w['w_in_tail'] = _w_in_pad(fl['w_in'])
    w['wuq'] = _heads_pad(fl['mla_w_uq'], MLA_QK, 1)
    ukv = fl['mla_w_ukv'].reshape(MLA_KV_RANK, MLA_HEADS, MLA_NOPE + MLA_V)
    zero = jnp.zeros((MLA_KV_RANK, MLA_HEADS, LANES - MLA_NOPE), ukv.dtype)
    wk = jnp.concatenate([ukv[:, :, :MLA_NOPE], zero], axis=2).reshape(MLA_KV_RANK, HP)
    wv = jnp.concatenate([ukv[:, :, MLA_NOPE:], zero], axis=2).reshape(MLA_KV_RANK, HP)
    w['wkv'] = jnp.concatenate([wk, wv], axis=1)
    wb = fl['w_branch']
    w['wb0'] = wb[0]
    w['wb1'] = _heads_pad(wb[1], MLA_V, 0)
    w['wb2'] = _heads_pad(wb[2], SSD_HEAD_DIM, 0)
    w['conv_w'] = _xbc_pad(fl['ssd_conv_w'].astype(F32))
    for n in ('mix_norm', 'gm_v_norm', 'mla_q_norm', 'mla_kv_norm'):
        w[n] = row(n)
    w['gm_w_s'] = small['gm_w_s'][l]
    w['gm_b_full'] = jnp.broadcast_to(small['gm_b_s'][l][:, :, None], (GM_GROUPS, CHUNK, LANES))
    w['gq'] = _lane_place(row('mla_q_gain'), 0)
    w['gk'] = _lane_place(row('mla_k_gain'), 0)
    w['conv_b'] = _xbc_pad(row('ssd_conv_b'))
    w['dt_bias'] = _lane_place(row('ssd_dt_bias'), 0)
    w['a_log'] = _lane_place(row('ssd_a_log'), 0)
    w['d_vec'] = jnp.repeat(small['ssd_d'][l], LANES)[None, :]
    w['ssd_norm'] = _heads_pad(row('ssd_norm'), SSD_HEAD_DIM, 1)
    return w


_MIXER_SMALL = ['mla_w_uq', 'mla_w_ukv', 'ssd_conv_w', 'w_branch', 'w_out']
_MIXER_SMALL_G = [n for n in _MIXER_SMALL if n != 'ssd_conv_w']
GATHER_HOSTS = {'attn': ['ffn1_w_in', 'ffn2_w_in'], 'scan': ['ffn1_w_out', 'ffn2_w_out'], 'merge': _MIXER_SMALL, 'ffn2_in': ['w_in']}
FIRST_NOW = ['ffn1_w_in', 'ffn1_w_out']
FIRST_HOSTS = {'ffn1_in': ['w_in'], 'ffn1_out': _MIXER_SMALL, 'proj': ['ffn2_w_in', 'ffn2_w_out']}
PAIR_HOSTS = {'ffn2_dwout': ['ffn1_w_in', 'ffn2_w_in'], 'ffn2_dwin': ['ffn1_w_out', 'w_in', 'ffn2_w_out'] + _MIXER_SMALL_G}
REDUCE_HOSTS = {'dattn_q': ['ffn1_w_out', 'w_in', 'ffn2_w_out'], 'dattn_kv': ['ffn1_w_in', 'ffn2_w_in'], 'dmla_pre': _MIXER_SMALL_G}
LAST_EARLY = ['w_in', 'ffn2_w_in', 'ffn2_w_out'] + _MIXER_SMALL_G
LAST_HOSTS = {'ffn1_dact': ['w_in'], 'ffn1_dwin': ['ffn2_w_in'], 'ffn1_dx': ['ffn2_w_out'] + _MIXER_SMALL_G}
LAST_LATE = ['ffn1_w_in', 'ffn1_w_out']


def _ffn_fwd(x, norm, w4, w_out, tag, sides=None):
    sides = sides or {}
    carried = {}
    (h, gate, up, act), carried[f"{tag}_in"] = _ffn_in(x, norm, w4, f"{tag}_in", sides.get(f"{tag}_in"))
    y, carried[f"{tag}_out"] = _ffn_out(act, w_out, x, f"{tag}_out", sides.get(f"{tag}_out"))
    return y, (x, h, gate, up, act), carried


def _ffn_bwd(dy, saved, norm, w4, w_out, tag, sides=None, after_dwout=None):
    sides = dict(sides or {})
    carried = {}
    x, h, gate, up, act = saved
    dw_out, carried[f"{tag}_dwout"] = _ffn_dwout(act, dy, f"{tag}_dwout", sides.get(f"{tag}_dwout"))
    if after_dwout is not None:
        sides.update(after_dwout(carried[f"{tag}_dwout"]))
    da, carried[f"{tag}_dact"] = _ffn_dact(dy, w_out, gate, up, f"{tag}_dact", sides.get(f"{tag}_dact"))
    dw_in, carried[f"{tag}_dwin"] = _ffn_dwin(h, da, f"{tag}_dwin", sides.get(f"{tag}_dwin"))
    (dx, dnorm), carried[f"{tag}_dx"] = _ffn_dx(da, w4, x, norm, dy, f"{tag}_dx", sides.get(f"{tag}_dx"))
    return dx, dnorm, dw_in, dw_out.reshape((N_CHIPS, 2 * FC // N_CHIPS, D_MODEL)), carried


def _mixer_fwd(x, w, tabs, tag, sides=None):
    sides = sides or {}
    carried = {}
    h = _rmsnorm_fwd(x, w['mix_norm'], f"{tag}_norm")
    if sides.get('proj') is None:
        pm = _matmul(h, w['w_in_main'], out_dtype=BF16, name=f"{tag}_proj_main")
    else:
        pm, carried['proj'] = _matmul(h, w['w_in_main'], out_dtype=BF16, name=f"{tag}_proj_main", side=sides['proj'])
    pt = _matmul(h, w['w_in_tail'], name=f"{tag}_proj_tail")
    ya = _gmlp_fwd(pm, w['gm_v_norm'], w['gm_w_s'], w['gm_b_full'], f"{tag}_gmlp")
    q, k, v = _mla_pre_fwd(pt, tabs, w['mla_q_norm'], w['mla_kv_norm'], w['wuq'], w['wkv'], w['gq'], w['gk'], f"{tag}_mla_pre")
    (o, lse), carried['attn'] = _attn_fwd(q, k, v, f"{tag}_attn", sides.get('attn'))
    xs = _conv_fwd(pm, C_XS, HP, w['conv_w'][:, :HP], w['conv_b'][:, :HP], f"{tag}_conv_x")
    bc = _conv_fwd(pt, T_BC, BCW, w['conv_w'][:, HP:], w['conv_b'][:, HP:], f"{tag}_conv_bc")
    dtb, dab = _dt_fwd(pt, w['dt_bias'], w['a_log'], f"{tag}_dt")
    (ys, s_in), carried['scan'] = _scan_fwd(xs, bc, dtb, dab, f"{tag}_scan", sides.get('scan'))
    yc = _ssd_post_fwd(ys, xs, pm, w['d_vec'], w['ssd_norm'], f"{tag}_ssd_post")
    (mg, y), carried['merge'] = _merge_fwd(pm, ya, o, yc, w['wb0'], w['wb1'], w['wb2'], w['w_out'], x, f"{tag}_merge",
                                           sides.get('merge'))
    return y, (x, h, pm, pt, ya, q, k, v, o, lse, xs, bc, dtb, dab, ys, s_in, yc, mg), carried


def _pair_sums(pending, got):
    return {n: _pair_add(pending[n], got[n], f"pair_add_{n}") for n in got}


def _chip_sums(sums, arrived):
    return {n: _chip_add(sums[n][1], arrived[n], f"chip_add_{n}") for n in arrived}


def _reduce_to_chip(pending):
    names = list(pending)
    got = _run_exchange(_pair_exchange([pending[n] for n in names]), "pair_exchange")
    sums = _pair_sums(pending, dict(zip(names, got)))
    arrived = _run_exchange(_chip_exchange([sums[n][0] for n in names]), "chip_exchange")
    return _chip_sums(sums, dict(zip(names, arrived)))


def _mixer_bwd(dy, saved, w, tabs, tag, sides=None):
    sides = sides or {}
    carried = {}
    x, h, pm, pt, ya, q, k, v, o, lse, xs, bc, dtb, dab, ys, s_in, yc, mg = saved
    g = {}
    g['w_out'] = _matmul(mg, dy, ta=True, name=f"{tag}_dwout").reshape((N_CHIPS, D_MODEL // N_CHIPS, D_MODEL))
    d0, d1, d2, dgates, dya, do, dyc = _merge_bwd(pm, ya, o, yc, w['wb0'], w['wb1'], w['wb2'], w['w_out'], dy, f"{tag}_dmerge")
    dwb0 = _matmul(ya, d0, ta=True, name=f"{tag}_dwb0")
    dwb1 = _matmul(o, d1, ta=True, name=f"{tag}_dwb1")
    dwb2 = _matmul(yc, d2, ta=True, name=f"{tag}_dwb2")
    g['w_branch'] = _split('w_branch', jnp.stack([dwb0, _heads_unpad(dwb1, MLA_V, 0), _heads_unpad(dwb2, SSD_HEAD_DIM, 0)]))
    duv, g['gm_v_norm'], g['gm_w_s'], db = _gmlp_bwd(pm, w['gm_v_norm'], w['gm_w_s'], w['gm_b_full'], dya, f"{tag}_dgmlp")
    g['gm_b_s'] = db.T
    (dq, delta), carried['dattn_q'] = _attn_bwd_dq(q, k, v, o, lse, do, f"{tag}_dattn_q", sides.get('dattn_q'))
    (dk, dv), carried['dattn_kv'] = _attn_bwd_dkv(q, k, v, lse, delta, do, f"{tag}_dattn_kv", sides.get('dattn_kv'))
    (dcq, dckv, dkr, dwuq, dwkv, g['mla_q_norm'], g['mla_kv_norm'], dgq, dgk), carried['dmla_pre'] = _mla_pre_bwd(
        pt, tabs, w['mla_q_norm'], w['mla_kv_norm'], w['wuq'], w['wkv'], w['gq'], w['gk'], dq, dk, dv, f"{tag}_dmla_pre",
        sides.get('dmla_pre'))
    g['mla_w_uq'] = _split('mla_w_uq', _heads_unpad(dwuq, MLA_QK, 1))
    dwk = dwkv[:, :HP].reshape(MLA_KV_RANK, MLA_HEADS, LANES)[:, :, :MLA_NOPE]
    dwv = dwkv[:, HP:].reshape(MLA_KV_RANK, MLA_HEADS, LANES)[:, :, :MLA_V]
    g['mla_w_ukv'] = _split('mla_w_ukv', jnp.concatenate([dwk, dwv], axis=2).reshape(MLA_KV_RANK, MLA_HEADS * (MLA_NOPE + MLA_V)))
    g['mla_q_gain'], g['mla_k_gain'] = dgq[:, :MLA_QK], dgk[:, :MLA_QK]
    dys, dz, dssd_norm, dd = _ssd_post_bwd(ys, xs, pm, w['d_vec'], w['ssd_norm'], dyc, f"{tag}_dssd_post")
    g['ssd_norm'] = _heads_unpad(dssd_norm, SSD_HEAD_DIM, 1)
    g['ssd_d'] = jnp.sum(dd.reshape(SSD_HEADS, LANES), axis=1)[None, :]
    dxs, dbm, dcm, dda, ddtx = _scan_bwd(xs, bc, dtb, dab, s_in, dys, w['d_vec'], f"{tag}_dscan")
    dxs16, dcw_x, dcb_x = _conv_bwd(pm, C_XS, HP, w['conv_w'][:, :HP], w['conv_b'][:, :HP], dxs, f"{tag}_dconv_x")
    dbc16, dcw_bc, dcb_bc = _conv_bwd(pt, T_BC, BCW, w['conv_w'][:, HP:], w['conv_b'][:, HP:],
                                      jnp.concatenate([dbm, dcm], axis=1), f"{tag}_dconv_bc")
    g['ssd_conv_w'] = _xbc_unpad(jnp.concatenate([dcw_x, dcw_bc], axis=1))
    g['ssd_conv_b'] = _xbc_unpad(jnp.concatenate([dcb_x, dcb_bc], axis=1))
    ddt, dbias, dalog = _dt_bwd(pt, w['dt_bias'], w['a_log'], dda, ddtx, f"{tag}_ddt")
    g['ssd_dt_bias'], g['ssd_a_log'] = dbias[:, :SSD_HEADS], dalog[:, :SSD_HEADS]
    s = x.shape[0]
    dpm = jnp.concatenate([duv, dz, dxs16, dgates], axis=1)
    dpt = jnp.concatenate([dbc16, dckv, dcq, dkr, ddt, jnp.zeros((s, PW_TAIL - T_DT - LANES), BF16)], axis=1)
    g['w_in'] = _split('w_in', _w_in_unpad(_matmul(h, dpm, ta=True, name=f"{tag}_dwin_main"),
                                           _matmul(h, dpt, ta=True, name=f"{tag}_dwin_tail")))
    dh = _matmul(dpt, w['w_in_tail'], tb=True, name=f"{tag}_dh_tail")
    dh = _matmul(dpm, w['w_in_main'], tb=True, res=dh, name=f"{tag}_dh_main")
    dx, g['mix_norm'] = _rmsnorm_bwd(x, w['mix_norm'], dh, dy, f"{tag}_dnorm")
    return dx, g, carried


_CONV_ROWS = 32


def _rows_cols(a, lead):
    return a.reshape(a.shape[:lead] + (int(np.prod(a.shape[lead:-1])), a.shape[-1]))


def _shard_views(wts):
    views = []
    for n in SHARDED_ORDER:
        a = _rows_cols(wts[n].astype(BF16), 1)
        if n == 'ssd_conv_w':
            a = jnp.pad(a, ((0, 0), (0, _CONV_ROWS - a.shape[1]), (0, 0)))
        views.append(a)
    return views


def _gathered(names, arrays):
    out = {}
    for n, a in zip(names, arrays):
        shp = _shard_shape(n)
        if n == 'ssd_conv_w':
            a = a[:, :shp[0]]
        out[n] = a.reshape((N_CHIPS,) + shp)
    return out


def _local_step(x, positions, target, weights, small, distributed=True):
    tabs = _rope_tables(positions)
    views = dict(zip(SHARDED_ORDER, weights)) if distributed else None
    plan = [{} for _ in range(DEPTH)]
    if distributed:
        for l in range(DEPTH - 1):
            plan[l].update({host: (names, l + 1) for host, names in GATHER_HOSTS.items()})
        plan[0].update({host: (names, 0) for host, names in FIRST_HOSTS.items()})
        have = [dict() for _ in range(DEPTH)]
        have[0].update(_gathered(FIRST_NOW, _run_exchange(_gather_exchange([views[n] for n in FIRST_NOW], 0), "gather_first")))
    else:
        have = weights

    def absorb(l, carried):
        for host, arrays in carried.items():
            if host in plan[l]:
                names, layer = plan[l][host]
                have[layer].update(_gathered(names, arrays))

    ws, saved = [], []
    for l in range(DEPTH):
        sides = {host: _gather_exchange([views[n] for n in names], layer) for host, (names, layer) in plan[l].items()}
        w = _layer_weights(have[l], small, l, 'ffn1')
        x, s1, carried = _ffn_fwd(x, w['ffn1_norm'], w['ffn1_w_in'], w['ffn1_w_out'], "ffn1", sides)
        absorb(l, carried)
        w.update(_layer_weights(have[l], small, l, 'mixer'))
        x, s2, carried = _mixer_fwd(x, w, tabs, "mix", sides)
        absorb(l, carried)
        w.update(_layer_weights(have[l], small, l, 'ffn2'))
        x, s3, carried = _ffn_fwd(x, w['ffn2_norm'], w['ffn2_w_in'], w['ffn2_w_out'], "ffn2", sides)
        absorb(l, carried)
        ws.append(w)
        saved.append((s1, s2, s3))
    dy, sq = _loss_head(x, target, "loss_head")
    loss = 0.5 * jnp.sum(sq) / D_MODEL
    grads, reduced, pending = [None] * DEPTH, [dict() for _ in range(DEPTH)], None

    def chip_sides(sums, hosts):
        return {host: _chip_exchange([sums[n][0] for n in names]) for host, names in hosts.items()}

    def arrivals(carried, hosts):
        return {n: a for host, names in hosts.items() for n, a in zip(names, carried[host])}

    for l in reversed(range(DEPTH)):
        w = ws[l]
        s1, s2, s3 = saved[l]
        sides = {host: _pair_exchange([pending[n] for n in names]) for host, names in PAIR_HOSTS.items()} if pending else {}
        dy, dn2, dwi2, dwo2, carried = _ffn_bwd(dy, s3, w['ffn2_norm'], w['ffn2_w_in'], w['ffn2_w_out'], "ffn2", sides)
        sides = {}
        if pending:
            sums = _pair_sums(pending, arrivals(carried, PAIR_HOSTS))
            sides = chip_sides(sums, REDUCE_HOSTS)
        dy, g, carried = _mixer_bwd(dy, s2, w, tabs, "mix", sides)
        if pending:
            reduced[l + 1] = _chip_sums(sums, arrivals(carried, REDUCE_HOSTS))
        g.update(ffn2_norm=dn2, ffn2_w_in=dwi2, ffn2_w_out=dwo2)
        last = distributed and l == 0
        if last:
            early = {n: _rows_cols(g[n], 1) for n in LAST_EARLY}
            after = {}

            def after_dwout(got):
                after['sums'] = _pair_sums(early, dict(zip(LAST_EARLY, got)))
                return chip_sides(after['sums'], LAST_HOSTS)

            dy, dn1, dwi1, dwo1, carried = _ffn_bwd(dy, s1, w['ffn1_norm'], w['ffn1_w_in'], w['ffn1_w_out'], "ffn1",
                                                    {'ffn1_dwout': _pair_exchange([early[n] for n in LAST_EARLY])}, after_dwout)
            reduced[0].update(_chip_sums(after['sums'], arrivals(carried, LAST_HOSTS)))
        else:
            dy, dn1, dwi1, dwo1, _ = _ffn_bwd(dy, s1, w['ffn1_norm'], w['ffn1_w_in'], w['ffn1_w_out'], "ffn1")
        g.update(ffn1_norm=dn1, ffn1_w_in=dwi1, ffn1_w_out=dwo1)
        grads[l] = g
        if distributed:
            pending = {n: _rows_cols(g[n], 1) for n in REDUCED}
    if distributed:
        reduced[0].update(_reduce_to_chip({n: pending[n] for n in LAST_LATE}))
    return loss, dy, grads, reduced


SMALL_PACK = SMALL_ORDER + ['ssd_conv_w']


def _pack_small(per_layer_rows, tail=None):
    parts = [per_layer_rows[l][n].reshape(-1).astype(F32) for l in range(DEPTH) for n in SMALL_PACK]
    if tail is not None:
        parts.append(tail.reshape(1))
    flat = jnp.concatenate(parts)
    rows = -(-flat.shape[0] // LANES)
    rows = -(-rows // 8) * 8
    return jnp.pad(flat, (0, rows * LANES - flat.shape[0])).reshape(rows, LANES)


def _unpack_small(buf, shapes):
    flat = buf.reshape(-1)
    off = 0
    out = {n: [] for n in SMALL_PACK}
    for l in range(DEPTH):
        for n in SMALL_PACK:
            size = int(np.prod(shapes[n]))
            out[n].append(flat[off:off + size].reshape(shapes[n]))
            off += size
    return {n: jnp.stack(v) for n, v in out.items()}


def kernel(x, positions, ffn1_norm, ffn1_w_in, ffn1_w_out, mix_norm, w_in, gm_v_norm, gm_w_s, gm_b_s, mla_q_norm, mla_kv_norm, mla_w_uq, mla_w_ukv, mla_q_gain, mla_k_gain, ssd_conv_w, ssd_conv_b, ssd_dt_bias, ssd_a_log, ssd_d, ssd_norm, w_branch, w_out, ffn2_norm, ffn2_w_in, ffn2_w_out, loss_target, m_ffn1_norm, m_ffn1_w_in, m_ffn1_w_out, m_mix_norm, m_w_in, m_gm_v_norm, m_gm_w_s, m_gm_b_s, m_mla_q_norm, m_mla_kv_norm, m_mla_w_uq, m_mla_w_ukv, m_mla_q_gain, m_mla_k_gain, m_ssd_conv_w, m_ssd_conv_b, m_ssd_dt_bias, m_ssd_a_log, m_ssd_d, m_ssd_norm, m_w_branch, m_w_out, m_ffn2_norm, m_ffn2_w_in, m_ffn2_w_out, v_ffn1_norm, v_ffn1_w_in, v_ffn1_w_out, v_mix_norm, v_w_in, v_gm_v_norm, v_gm_w_s, v_gm_b_s, v_mla_q_norm, v_mla_kv_norm, v_mla_w_uq, v_mla_w_ukv, v_mla_q_gain, v_mla_k_gain, v_ssd_conv_w, v_ssd_conv_b, v_ssd_dt_bias, v_ssd_a_log, v_ssd_d, v_ssd_norm, v_w_branch, v_w_out, v_ffn2_norm, v_ffn2_w_in, v_ffn2_w_out):
    wts = dict(zip(WEIGHTS, (ffn1_norm, ffn1_w_in, ffn1_w_out, mix_norm, w_in, gm_v_norm, gm_w_s, gm_b_s, mla_q_norm, mla_kv_norm,
                             mla_w_uq, mla_w_ukv, mla_q_gain, mla_k_gain, ssd_conv_w, ssd_conv_b, ssd_dt_bias, ssd_a_log, ssd_d,
                             ssd_norm, w_branch, w_out, ffn2_norm, ffn2_w_in, ffn2_w_out)))
    mom = dict(zip(WEIGHTS, (m_ffn1_norm, m_ffn1_w_in, m_ffn1_w_out, m_mix_norm, m_w_in, m_gm_v_norm, m_gm_w_s, m_gm_b_s, m_mla_q_norm,
                             m_mla_kv_norm, m_mla_w_uq, m_mla_w_ukv, m_mla_q_gain, m_mla_k_gain, m_ssd_conv_w, m_ssd_conv_b,
                             m_ssd_dt_bias, m_ssd_a_log, m_ssd_d, m_ssd_norm, m_w_branch, m_w_out, m_ffn2_norm, m_ffn2_w_in,
                             m_ffn2_w_out)))
    var = dict(zip(WEIGHTS, (v_ffn1_norm, v_ffn1_w_in, v_ffn1_w_out, v_mix_norm, v_w_in, v_gm_v_norm, v_gm_w_s, v_gm_b_s, v_mla_q_norm,
                             v_mla_kv_norm, v_mla_w_uq, v_mla_w_ukv, v_mla_q_gain, v_mla_k_gain, v_ssd_conv_w, v_ssd_conv_b,
                             v_ssd_dt_bias, v_ssd_a_log, v_ssd_d, v_ssd_norm, v_w_branch, v_w_out, v_ffn2_norm, v_ffn2_w_in,
                             v_ffn2_w_out)))
    cx, cy, _ = _me()
    mychip = _chip_index(cx, cy)

    small = {n: wts[n] for n in SMALL_ORDER}
    loss_part, dx, grads, reduced = _local_step(x[0], positions[0], loss_target[0], _shard_views(wts), small)
    rows_cols = _rows_cols
    halves = [jnp.stack([reduced[l][n] for l in range(DEPTH)]) for n in REDUCED]
    theirs = _run_exchange(_pair_share(halves), "pair_share")
    grad, delta, new_m, new_v = {}, {}, {}, {}
    everyone = _all_exchange(_pack_small(grads, tail=loss_part))
    for n, a, b in zip(REDUCED, halves, theirs):
        shp = wts[n].shape
        outs, carried = _adamw_sharded(rows_cols(wts[n], 1), rows_cols(mom[n], 1), rows_cols(var[n], 1), a, b, f"adamw_{n}",
                                       everyone if n == REDUCED[0] else None)
        if n == REDUCED[0]:
            partials = carried[0]
        grad[n], delta[n], new_m[n], new_v[n] = [o.reshape(shp) for o in outs]
    shapes = {n: wts[n].shape[1:] for n in SMALL_ORDER}
    shapes['ssd_conv_w'] = SHARDED['ssd_conv_w'][0]
    summed = _sum_slots(partials, "small_sum")
    small_g = _unpack_small(summed, shapes)
    loss = summed.reshape(-1)[DEPTH * sum(int(np.prod(shapes[n])) for n in SMALL_PACK)]
    conv_full = small_g.pop('ssd_conv_w')
    shard_cols = _shard_shape('ssd_conv_w')[1]
    small_g['ssd_conv_w'] = lax.dynamic_slice_in_dim(conv_full, mychip * shard_cols, shard_cols, axis=2)
    shapes['ssd_conv_w'] = _shard_shape('ssd_conv_w')

    per_layer = lambda t: [{n: t[n][l] for n in SMALL_PACK} for l in range(DEPTH)]
    d, nm, nv = _adamw(_pack_small(per_layer(wts)), _pack_small(per_layer(small_g)), _pack_small(per_layer(mom)),
                       _pack_small(per_layer(var)), "adamw_small")
    sd, snm, snv = _unpack_small(d, shapes), _unpack_small(nm, shapes), _unpack_small(nv, shapes)
    for n in SMALL_PACK:
        grad[n], delta[n], new_m[n], new_v[n] = small_g[n], sd[n], snm[n], snv[n]
    return (loss, dx[None], *[grad[n] for n in WEIGHTS], *[delta[n] for n in WEIGHTS], *[new_m[n] for n in WEIGHTS],
            *[new_v[n] for n in WEIGHTS])
```

```python
import functools
import math

import numpy as np
import jax
import jax.numpy as jnp
from jax import lax
from jax.experimental import pallas as pl
from jax.experimental.pallas import tpu as pltpu

F32, BF16 = jnp.float32, jnp.bfloat16
MESH = pl.DeviceIdType.MESH

D_MODEL, DEPTH, D_FF, EPS = 1024, 4, 2816, 1e-6
GM_WIDTH, GM_GROUPS, CHUNK = 512, 4, 128
MLA_HEADS, MLA_Q_RANK, MLA_KV_RANK, MLA_NOPE, MLA_ROPE, MLA_V = 8, 384, 256, 64, 32, 64
MLA_QK = MLA_NOPE + MLA_ROPE
ROPE_THETA = 10000.0
SSD_HEADS, SSD_HEAD_DIM, SSD_GROUPS, SSD_STATE, SSD_CONV = 8, 64, 2, 128, 4
SSD_INNER = SSD_HEADS * SSD_HEAD_DIM
IN_COLS = 6312
LANES = 128
ADAM_LR, ADAM_B1, ADAM_B2, ADAM_EPS, ADAM_WD, ADAM_STEP = 0.001, 0.9, 0.999, 1e-08, 0.01, 10

C_UV, C_Z, C_XS, C_G, PW_MAIN = 0, 1024, 2048, 3072, 6144
T_BC, T_CKV, T_CQ, T_KR, T_DT, PW_TAIL = 0, 512, 768, 1152, 1280, 1536
HP = MLA_HEADS * LANES
FC = 2 * D_FF // 4

WEIGHTS = ['ffn1_norm', 'ffn1_w_in', 'ffn1_w_out', 'mix_norm', 'w_in', 'gm_v_norm', 'gm_w_s', 'gm_b_s', 'mla_q_norm',
           'mla_kv_norm', 'mla_w_uq', 'mla_w_ukv', 'mla_q_gain', 'mla_k_gain', 'ssd_conv_w', 'ssd_conv_b', 'ssd_dt_bias',
           'ssd_a_log', 'ssd_d', 'ssd_norm', 'w_branch', 'w_out', 'ffn2_norm', 'ffn2_w_in', 'ffn2_w_out']
SHARDED = {'ffn1_w_in': ((1024, 5632), 1), 'ffn1_w_out': ((2816, 1024), 0), 'w_in': ((1024, 6312), 1),
           'mla_w_uq': ((384, 768), 1), 'mla_w_ukv': ((256, 1024), 1), 'ssd_conv_w': ((4, 1024), 1),
           'w_branch': ((3, 512, 1024), 2), 'w_out': ((1024, 1024), 0), 'ffn2_w_in': ((1024, 5632), 1),
           'ffn2_w_out': ((2816, 1024), 0)}
SHARDED_ORDER = [n for n in WEIGHTS if n in SHARDED]
SMALL_ORDER = [n for n in WEIGHTS if n not in SHARDED]
REDUCED = [n for n in SHARDED_ORDER if n != 'ssd_conv_w']
N_CHIPS = 4
HALF_L = DEPTH // 2


def _shard_shape(name):
    shape, ax = SHARDED[name]
    return tuple(d // N_CHIPS if i == ax else d for i, d in enumerate(shape))


def _pick(dim, target):
    if dim <= target:
        return dim
    t = (target // LANES) * LANES
    while t >= LANES:
        if dim % t == 0:
            return t
        t -= LANES
    return dim


def _sigmoid(x):
    return 1.0 / (1.0 + jnp.exp(-x))


def _params(*sem):
    return pltpu.CompilerParams(dimension_semantics=sem, vmem_limit_bytes=56 * 1024 * 1024)


def _matmul(a, b, *, ta=False, tb=False, out_dtype=F32, scale=1.0, res=None, name, side=None):
    if ta:
        k_dim, m_dim = a.shape
    else:
        m_dim, k_dim = a.shape
    if tb:
        n_dim, k2 = b.shape
    else:
        k2, n_dim = b.shape
    assert k_dim == k2, (a.shape, b.shape, ta, tb)
    tm, tn, tk = _pick(m_dim, 1024), _pick(n_dim, 1024), _pick(k_dim, 1024)
    nk = k_dim // tk
    dn = (((0 if ta else 1,), (1 if tb else 0,)), ((), ()))

    def body(*refs):
        if res is not None:
            a_ref, b_ref, r_ref, o_ref, acc = refs
        else:
            a_ref, b_ref, o_ref, acc = refs
        k = pl.program_id(2)

        @pl.when(k == 0)
        def _():
            acc[...] = jnp.zeros_like(acc)

        acc[...] += lax.dot_general(a_ref[...].astype(BF16), b_ref[...].astype(BF16), dn, preferred_element_type=F32)

        @pl.when(k == nk - 1)
        def _():
            r = acc[...]
            if scale != 1.0:
                r = r * scale
            if res is not None:
                r = r + r_ref[...]
            o_ref[...] = r.astype(out_dtype)

    a_spec = pl.BlockSpec((tk, tm), lambda j, i, k: (k, i)) if ta else pl.BlockSpec((tm, tk), lambda j, i, k: (i, k))
    b_spec = pl.BlockSpec((tn, tk), lambda j, i, k: (j, k)) if tb else pl.BlockSpec((tk, tn), lambda j, i, k: (k, j))
    in_specs = [a_spec, b_spec]
    args = [a, b]
    if res is not None:
        in_specs.append(pl.BlockSpec((tm, tn), lambda j, i, k: (i, j)))
        args.append(res)
    (out,), carried = _call(
        body, name=name, grid=(n_dim // tn, m_dim // tm, nk), in_specs=in_specs,
        out_specs=[pl.BlockSpec((tm, tn), lambda j, i, k: (i, j))],
        out_shape=[jax.ShapeDtypeStruct((m_dim, n_dim), out_dtype)],
        scratch_shapes=[pltpu.VMEM((tm, tn), F32)], args=args, semantics=("parallel", "parallel", "arbitrary"), side=side)
    return out if side is None else (out, carried)


def _rmsnorm_fwd(x, gain, name):
    s, d = x.shape
    tm = _pick(s, 512)

    def body(x_ref, g_ref, o_ref):
        xv = x_ref[...]
        r = lax.rsqrt(jnp.mean(xv * xv, axis=-1, keepdims=True) + EPS)
        o_ref[...] = (xv * r * g_ref[...]).astype(BF16)

    return pl.pallas_call(
        body, name=name, grid=(s // tm,),
        in_specs=[pl.BlockSpec((tm, d), lambda i: (i, 0)), pl.BlockSpec((1, d), lambda i: (0, 0))],
        out_specs=pl.BlockSpec((tm, d), lambda i: (i, 0)),
        out_shape=jax.ShapeDtypeStruct((s, d), BF16), compiler_params=_params("parallel"))(x, gain)


def _rmsnorm_bwd(x, gain, dh, dres, name):
    s, d = x.shape
    tm = _pick(s, 512)

    def body(x_ref, g_ref, dh_ref, dr_ref, dx_ref, dg_ref):
        @pl.when(pl.program_id(0) == 0)
        def _():
            dg_ref[...] = jnp.zeros_like(dg_ref)

        xv, dhv = x_ref[...], dh_ref[...]
        r = lax.rsqrt(jnp.mean(xv * xv, axis=-1, keepdims=True) + EPS)
        u = dhv * g_ref[...]
        dx_ref[...] = dr_ref[...] + r * u - xv * (r * r * r) * jnp.mean(xv * u, axis=-1, keepdims=True)
        dg_ref[...] += jnp.sum(dhv * xv * r, axis=0, keepdims=True)

    row = pl.BlockSpec((tm, d), lambda i: (i, 0))
    vec = pl.BlockSpec((1, d), lambda i: (0, 0))
    return pl.pallas_call(
        body, name=name, grid=(s // tm,), in_specs=[row, vec, row, row], out_specs=[row, vec],
        out_shape=[jax.ShapeDtypeStruct((s, d), F32), jax.ShapeDtypeStruct((1, d), F32)],
        compiler_params=_params("arbitrary"))(x, gain, dh, dres)


_NT = (((1,), (1,)), ((), ()))
_TN = (((0,), (0,)), ((), ()))


def _resident(shape):
    return pl.BlockSpec(shape, lambda *_: tuple(0 for _ in shape), pipeline_mode=pl.Buffered(1))


def _ffn_in(x, gain, w4, name, side=None):
    s, d = x.shape
    tm = _pick(s, 512)

    def body(x_ref, g_ref, w_ref, h_ref, gate_ref, up_ref, act_ref):
        xv = x_ref[...]
        r = lax.rsqrt(jnp.mean(xv * xv, axis=-1, keepdims=True) + EPS)
        h = (xv * r * g_ref[...]).astype(BF16)
        h_ref[...] = h
        for j in range(2):
            g16 = jnp.dot(h, w_ref[j], preferred_element_type=F32).astype(BF16)
            u16 = jnp.dot(h, w_ref[j + 2], preferred_element_type=F32).astype(BF16)
            gate_ref[j] = g16
            up_ref[j] = u16
            gf, uf = g16.astype(F32), u16.astype(F32)
            act_ref[j] = (gf * _sigmoid(gf) * uf).astype(BF16)

    half = pl.BlockSpec((2, tm, FC), lambda i: (0, i, 0))
    return _call(
        body, name=name, grid=(s // tm,),
        in_specs=[pl.BlockSpec((tm, d), lambda i: (i, 0)), pl.BlockSpec((1, d), lambda i: (0, 0)), _resident((4, d, FC))],
        out_specs=[pl.BlockSpec((tm, d), lambda i: (i, 0)), half, half, half],
        out_shape=[jax.ShapeDtypeStruct((s, d), BF16)] + [jax.ShapeDtypeStruct((2, s, FC), BF16)] * 3,
        scratch_shapes=[], args=(x, gain, w4), semantics=("parallel",), side=side)


def _ffn_out(act, w_out, x, name, side=None):
    s, d = x.shape
    tm = _pick(s, 512)

    def body(a_ref, w_ref, x_ref, o_ref):
        acc = jnp.dot(a_ref[0], w_ref[0:FC, :], preferred_element_type=F32)
        acc = acc + jnp.dot(a_ref[1], w_ref[FC:2 * FC, :], preferred_element_type=F32)
        o_ref[...] = x_ref[...] + 0.5 * acc

    row = pl.BlockSpec((tm, d), lambda i: (i, 0))
    (out,), carried = _call(
        body, name=name, grid=(s // tm,),
        in_specs=[pl.BlockSpec((2, tm, FC), lambda i: (0, i, 0)), _resident((2 * FC, d)), row], out_specs=[row],
        out_shape=[jax.ShapeDtypeStruct((s, d), F32)], scratch_shapes=[], args=(act, w_out, x), semantics=("parallel",),
        side=side)
    return out, carried


def _ffn_dact(dy, w_out, gate, up, name, side=None):
    s, d = dy.shape
    tm = _pick(s, 512)

    def body(dy_ref, w_ref, g_ref, u_ref, o_ref):
        dy16 = dy_ref[...].astype(BF16)
        for j in range(2):
            dact = 0.5 * lax.dot_general(dy16, w_ref[j * FC:(j + 1) * FC, :], _NT, preferred_element_type=F32)
            g, u = g_ref[j].astype(F32), u_ref[j].astype(F32)
            sg = _sigmoid(g)
            o_ref[j] = (dact * u * (sg * (1.0 + g * (1.0 - sg)))).astype(BF16)
            o_ref[j + 2] = (dact * g * sg).astype(BF16)

    half = pl.BlockSpec((2, tm, FC), lambda i: (0, i, 0))
    (out,), carried = _call(
        body, name=name, grid=(s // tm,),
        in_specs=[pl.BlockSpec((tm, d), lambda i: (i, 0)), _resident((2 * FC, d)), half, half],
        out_specs=[pl.BlockSpec((4, tm, FC), lambda i: (0, i, 0))],
        out_shape=[jax.ShapeDtypeStruct((4, s, FC), BF16)], scratch_shapes=[], args=(dy, w_out, gate, up),
        semantics=("parallel",), side=side)
    return out, carried


def _ffn_dwout(act, dy, name, side=None):
    s, d = dy.shape
    tk = _pick(s, 1024)
    nk = s // tk

    def body(a_ref, dy_ref, o_ref):
        k = pl.program_id(1)

        @pl.when(k == 0)
        def _():
            o_ref[...] = jnp.zeros_like(o_ref)

        o_ref[...] += lax.dot_general(a_ref[...], dy_ref[...].astype(BF16), _TN, preferred_element_type=F32)

        @pl.when(k == nk - 1)
        def _():
            o_ref[...] = 0.5 * o_ref[...]

    (out,), carried = _call(
        body, name=name, grid=(2, nk),
        in_specs=[pl.BlockSpec((None, tk, FC), lambda j, k: (j, k, 0)), pl.BlockSpec((tk, d), lambda j, k: (k, 0))],
        out_specs=[pl.BlockSpec((FC, d), lambda j, k: (j, 0))], out_shape=[jax.ShapeDtypeStruct((2 * FC, d), F32)],
        scratch_shapes=[], args=(act, dy), semantics=("parallel", "arbitrary"), side=side)
    return out, carried


def _ffn_dwin(h, da, name, side=None):
    s, d = h.shape
    tk = _pick(s, 1024)

    def body(h_ref, da_ref, o_ref):
        @pl.when(pl.program_id(1) == 0)
        def _():
            o_ref[...] = jnp.zeros_like(o_ref)

        o_ref[...] += lax.dot_general(h_ref[...], da_ref[...], _TN, preferred_element_type=F32)

    (out,), carried = _call(
        body, name=name, grid=(4, s // tk),
        in_specs=[pl.BlockSpec((tk, d), lambda j, k: (k, 0)), pl.BlockSpec((None, tk, FC), lambda j, k: (j, k, 0))],
        out_specs=[pl.BlockSpec((None, d, FC), lambda j, k: (j, 0, 0))], out_shape=[jax.ShapeDtypeStruct((4, d, FC), F32)],
        scratch_shapes=[], args=(h, da), semantics=("parallel", "arbitrary"), side=side)
    return out, carried


def _ffn_dx(da, w4, x, gain, dy, name, side=None):
    s, d = x.shape
    tm = _pick(s, 512)

    def body(da_ref, w_ref, x_ref, g_ref, dy_ref, dx_ref, dg_ref):
        @pl.when(pl.program_id(0) == 0)
        def _():
            dg_ref[...] = jnp.zeros_like(dg_ref)

        dh = jnp.zeros((tm, d), F32)
        for j in range(4):
            dh = dh + lax.dot_general(da_ref[j], w_ref[j], _NT, preferred_element_type=F32)
        xv = x_ref[...]
        r = lax.rsqrt(jnp.mean(xv * xv, axis=-1, keepdims=True) + EPS)
        u = dh * g_ref[...]
        dx_ref[...] = dy_ref[...] + r * u - xv * (r * r * r) * jnp.mean(xv * u, axis=-1, keepdims=True)
        dg_ref[...] += jnp.sum(dh * xv * r, axis=0, keepdims=True)

    row = pl.BlockSpec((tm, d), lambda i: (i, 0))
    vec = pl.BlockSpec((1, d), lambda i: (0, 0))
    return _call(
        body, name=name, grid=(s // tm,),
        in_specs=[pl.BlockSpec((4, tm, FC), lambda i: (0, i, 0)), _resident((4, d, FC)), row, vec, row],
        out_specs=[row, vec], out_shape=[jax.ShapeDtypeStruct((s, d), F32), jax.ShapeDtypeStruct((1, d), F32)],
        scratch_shapes=[], args=(da, w4, x, gain, dy), semantics=("arbitrary",), side=side)


_INV_SQRT2 = 0.7071067811865476
_INV_SQRT2PI = 0.3989422804014327


def _gelu(x):
    return 0.5 * x * (1.0 + lax.erf(x * _INV_SQRT2))


def _gelu_grad(x):
    return 0.5 * (1.0 + lax.erf(x * _INV_SQRT2)) + x * jnp.exp(-0.5 * x * x) * _INV_SQRT2PI


def _tril_mask():
    r = lax.broadcasted_iota(jnp.int32, (CHUNK, CHUNK), 0)
    c = lax.broadcasted_iota(jnp.int32, (CHUNK, CHUNK), 1)
    return r >= c


def _gmlp_fwd(p, v_gain, w_s, b_full, name):
    s = p.shape[0]
    tm = _pick(s, 512)
    nch = tm // CHUNK

    def body(uv_ref, g_ref, w_ref, b_ref, o_ref):
        gel = _gelu(uv_ref[...].astype(F32))
        u, v = gel[:, :GM_WIDTH], gel[:, GM_WIDTH:]
        r = lax.rsqrt(jnp.mean(v * v, axis=-1, keepdims=True) + EPS)
        vn = (v * r * g_ref[...]).astype(BF16)
        mask = _tril_mask()
        for g in range(GM_GROUPS):
            wm = jnp.where(mask, w_ref[g], 0.0).astype(BF16)
            for c in range(nch):
                rs, cs = slice(c * CHUNK, (c + 1) * CHUNK), slice(g * LANES, (g + 1) * LANES)
                sp = jnp.dot(wm, vn[rs, cs], preferred_element_type=F32) + b_ref[g]
                o_ref[rs, cs] = (u[rs, cs] * sp).astype(BF16)

    full3 = pl.BlockSpec((GM_GROUPS, CHUNK, CHUNK), lambda i: (0, 0, 0))
    return pl.pallas_call(
        body, name=name, grid=(s // tm,),
        in_specs=[pl.BlockSpec((tm, 2 * GM_WIDTH), lambda i: (i, C_UV // (2 * GM_WIDTH))),
                  pl.BlockSpec((1, GM_WIDTH), lambda i: (0, 0)), full3, full3],
        out_specs=pl.BlockSpec((tm, GM_WIDTH), lambda i: (i, 0)),
        out_shape=jax.ShapeDtypeStruct((s, GM_WIDTH), BF16), compiler_params=_params("parallel"))(p, v_gain, w_s, b_full)


def _gmlp_bwd(p, v_gain, w_s, b_full, dy, name):
    s = p.shape[0]
    tm = _pick(s, 512)
    nch = tm // CHUNK
    nsteps = s // tm

    def body(uv_ref, g_ref, w_ref, b_ref, dy_ref, duv_ref, dg_ref, dw_ref, db_ref, dvn_s, dbacc):
        step = pl.program_id(0)

        @pl.when(step == 0)
        def _():
            dg_ref[...] = jnp.zeros_like(dg_ref)
            dw_ref[...] = jnp.zeros_like(dw_ref)
            dbacc[...] = jnp.zeros_like(dbacc)

        uv = uv_ref[...].astype(F32)
        gel = _gelu(uv)
        u, v = gel[:, :GM_WIDTH], gel[:, GM_WIDTH:]
        r = lax.rsqrt(jnp.mean(v * v, axis=-1, keepdims=True) + EPS)
        gain = g_ref[...]
        vn32 = v * r * gain
        vn = vn32.astype(BF16)
        dy = dy_ref[...]
        mask = _tril_mask()
        for g in range(GM_GROUPS):
            wm = jnp.where(mask, w_ref[g], 0.0).astype(BF16)
            dwg = jnp.zeros((CHUNK, CHUNK), F32)
            dbg = jnp.zeros((CHUNK, LANES), F32)
            for c in range(nch):
                rs, cs = slice(c * CHUNK, (c + 1) * CHUNK), slice(g * LANES, (g + 1) * LANES)
                sp = jnp.dot(wm, vn[rs, cs], preferred_element_type=F32) + b_ref[g]
                dyc = dy[rs, cs]
                dsp = dyc * u[rs, cs]
                dsp16 = dsp.astype(BF16)
                duv_ref[rs, cs] = (dyc * sp * _gelu_grad(uv[rs, cs])).astype(BF16)
                dvn_s[rs, cs] = lax.dot_general(wm, dsp16, (((0,), (0,)), ((), ())), preferred_element_type=F32)
                dwg = dwg + lax.dot_general(dsp16, vn[rs, cs], (((1,), (1,)), ((), ())), preferred_element_type=F32)
                dbg = dbg + dsp
            dw_ref[g] += jnp.where(mask, dwg, 0.0)
            dbacc[:, g * LANES:(g + 1) * LANES] += dbg
        dvn = dvn_s[...]
        uu = dvn * gain
        dv = r * uu - v * (r * r * r) * jnp.mean(v * uu, axis=-1, keepdims=True)
        duv_ref[:, GM_WIDTH:] = (dv * _gelu_grad(uv[:, GM_WIDTH:])).astype(BF16)
        dg_ref[...] += jnp.sum(dvn * v * r, axis=0, keepdims=True)

        @pl.when(step == nsteps - 1)
        def _():
            for g in range(GM_GROUPS):
                db_ref[:, g:g + 1] = jnp.sum(dbacc[:, g * LANES:(g + 1) * LANES], axis=1, keepdims=True)

    full3 = pl.BlockSpec((GM_GROUPS, CHUNK, CHUNK), lambda i: (0, 0, 0))
    return pl.pallas_call(
        body, name=name, grid=(nsteps,),
        in_specs=[pl.BlockSpec((tm, 2 * GM_WIDTH), lambda i: (i, C_UV // (2 * GM_WIDTH))),
                  pl.BlockSpec((1, GM_WIDTH), lambda i: (0, 0)), full3, full3,
                  pl.BlockSpec((tm, GM_WIDTH), lambda i: (i, 0))],
        out_specs=[pl.BlockSpec((tm, 2 * GM_WIDTH), lambda i: (i, 0)), pl.BlockSpec((1, GM_WIDTH), lambda i: (0, 0)),
                   full3, pl.BlockSpec((CHUNK, GM_GROUPS), lambda i: (0, 0))],
        out_shape=[jax.ShapeDtypeStruct((s, 2 * GM_WIDTH), BF16), jax.ShapeDtypeStruct((1, GM_WIDTH), F32),
                   jax.ShapeDtypeStruct((GM_GROUPS, CHUNK, CHUNK), F32), jax.ShapeDtypeStruct((CHUNK, GM_GROUPS), F32)],
        scratch_shapes=[pltpu.VMEM((tm, GM_WIDTH), F32), pltpu.VMEM((CHUNK, GM_WIDTH), F32)],
        compiler_params=_params("arbitrary"))(p, v_gain, w_s, b_full, dy)


def _rope(x, ct, s1, s2):
    return x * ct + pltpu.roll(x, LANES - MLA_ROPE // 2, 1) * s1 + pltpu.roll(x, MLA_ROPE // 2, 1) * s2


def _rope_bwd(d, ct, s1, s2):
    return d * ct + pltpu.roll(d * s1, MLA_ROPE // 2, 1) + pltpu.roll(d * s2, LANES - MLA_ROPE // 2, 1)


def _head_norm(x, gain):
    r = lax.rsqrt(jnp.sum(x * x, axis=-1, keepdims=True) * (1.0 / MLA_QK) + EPS)
    return x * r * gain, r


def _head_norm_bwd(x, r, gain, d):
    u = d * gain
    return r * u - x * (r * r * r) * (jnp.sum(x * u, axis=-1, keepdims=True) * (1.0 / MLA_QK))


def _mla_specs(tm):
    cq = pl.BlockSpec((tm, MLA_Q_RANK), lambda i: (i, T_CQ // MLA_Q_RANK))
    ckv = pl.BlockSpec((tm, MLA_KV_RANK), lambda i: (i, T_CKV // MLA_KV_RANK))
    kr = pl.BlockSpec((tm, LANES), lambda i: (i, T_KR // LANES))
    tab = pl.BlockSpec((tm, LANES), lambda i: (i, 0))
    return cq, ckv, kr, tab


def _const(shape):
    return pl.BlockSpec(shape, lambda i: tuple(0 for _ in shape))


def _mla_pre_fwd(p, tabs, qn_g, kvn_g, wuq, wkv, gq, gk, name):
    s = p.shape[0]
    tm = _pick(s, 256)
    ct, s1, s2 = tabs

    def body(cq_ref, ckv_ref, kr_ref, ct_ref, s1_ref, s2_ref, qg_ref, kvg_ref, wuq_ref, wkv_ref, gq_ref, gk_ref,
             q_ref, k_ref, v_ref):
        cq, ckv, kr = cq_ref[...], ckv_ref[...], kr_ref[...]
        ctv, s1v, s2v = ct_ref[...], s1_ref[...], s2_ref[...]
        rq = lax.rsqrt(jnp.mean(cq * cq, axis=-1, keepdims=True) + EPS)
        q = jnp.dot((cq * rq * qg_ref[...]).astype(BF16), wuq_ref[...], preferred_element_type=F32)
        rk = lax.rsqrt(jnp.mean(ckv * ckv, axis=-1, keepdims=True) + EPS)
        kv = jnp.dot((ckv * rk * kvg_ref[...]).astype(BF16), wkv_ref[...], preferred_element_type=F32)
        v_ref[...] = kv[:, HP:].astype(BF16)
        for h in range(MLA_HEADS):
            hs = slice(h * LANES, (h + 1) * LANES)
            qh, _ = _head_norm(q[:, hs], gq_ref[...])
            q_ref[:, hs] = (_rope(qh, ctv, s1v, s2v) * _Q_SCALE).astype(BF16)
            kh, _ = _head_norm(kv[:, hs] + kr, gk_ref[...])
            k_ref[:, hs] = _rope(kh, ctv, s1v, s2v).astype(BF16)

    cq_s, ckv_s, kr_s, tab_s = _mla_specs(tm)
    out = pl.BlockSpec((tm, HP), lambda i: (i, 0))
    return pl.pallas_call(
        body, name=name, grid=(s // tm,),
        in_specs=[cq_s, ckv_s, kr_s, tab_s, tab_s, tab_s, _const((1, MLA_Q_RANK)), _const((1, MLA_KV_RANK)),
                  _const((MLA_Q_RANK, HP)), _const((MLA_KV_RANK, 2 * HP)), _const((1, LANES)), _const((1, LANES))],
        out_specs=[out, out, out], out_shape=[jax.ShapeDtypeStruct((s, HP), BF16)] * 3,
        compiler_params=_params("parallel"))(p, p, p, ct, s1, s2, qn_g, kvn_g, wuq, wkv, gq, gk)


def _mla_pre_bwd(p, tabs, qn_g, kvn_g, wuq, wkv, gq, gk, dq, dk, dv, name, side=None):
    s = p.shape[0]
    tm = _pick(s, 256)
    ct, s1, s2 = tabs

    def body(cq_ref, ckv_ref, kr_ref, ct_ref, s1_ref, s2_ref, qg_ref, kvg_ref, wuq_ref, wkv_ref, gq_ref, gk_ref,
             dq_ref, dk_ref, dv_ref, dcq_ref, dckv_ref, dkr_ref, dwuq_ref, dwkv_ref, dqg_ref, dkvg_ref, dgq_ref, dgk_ref,
             dqp, dkvp):
        @pl.when(pl.program_id(0) == 0)
        def _():
            for ref in (dwuq_ref, dwkv_ref, dqg_ref, dkvg_ref, dgq_ref, dgk_ref):
                ref[...] = jnp.zeros_like(ref)

        cq, ckv, kr = cq_ref[...], ckv_ref[...], kr_ref[...]
        ctv, s1v, s2v = ct_ref[...], s1_ref[...], s2_ref[...]
        rq = lax.rsqrt(jnp.mean(cq * cq, axis=-1, keepdims=True) + EPS)
        qn = (cq * rq * qg_ref[...]).astype(BF16)
        q = jnp.dot(qn, wuq_ref[...], preferred_element_type=F32)
        rk = lax.rsqrt(jnp.mean(ckv * ckv, axis=-1, keepdims=True) + EPS)
        kvn = (ckv * rk * kvg_ref[...]).astype(BF16)
        kv = jnp.dot(kvn, wkv_ref[...], preferred_element_type=F32)
        gqv, gkv = gq_ref[...], gk_ref[...]
        dgq = jnp.zeros((1, LANES), F32)
        dgk = jnp.zeros((1, LANES), F32)
        dkr = jnp.zeros((tm, LANES), F32)
        for h in range(MLA_HEADS):
            hs = slice(h * LANES, (h + 1) * LANES)
            xq = q[:, hs]
            _, r = _head_norm(xq, gqv)
            d = _rope_bwd(dq_ref[:, hs], ctv, s1v, s2v)
            dgq = dgq + jnp.sum(d * xq * r, axis=0, keepdims=True)
            dqp[:, hs] = _head_norm_bwd(xq, r, gqv, d)
            xk = kv[:, hs] + kr
            _, r = _head_norm(xk, gkv)
            d = _rope_bwd(dk_ref[:, hs], ctv, s1v, s2v)
            dgk = dgk + jnp.sum(d * xk * r, axis=0, keepdims=True)
            dxk = _head_norm_bwd(xk, r, gkv, d)
            dkvp[:, hs] = dxk
            dkr = dkr + dxk
        dkvp[:, HP:] = dv_ref[...]
        dgq_ref[...] += dgq
        dgk_ref[...] += dgk
        dkr_ref[...] = dkr.astype(BF16)
        tn = (((0,), (0,)), ((), ()))
        nt = (((1,), (1,)), ((), ()))
        dq16 = dqp[...].astype(BF16)
        dwuq_ref[...] += lax.dot_general(qn, dq16, tn, preferred_element_type=F32)
        dqn = lax.dot_general(dq16, wuq_ref[...], nt, preferred_element_type=F32)
        dqg_ref[...] += jnp.sum(dqn * cq * rq, axis=0, keepdims=True)
        u = dqn * qg_ref[...]
        dcq_ref[...] = (rq * u - cq * (rq * rq * rq) * jnp.mean(cq * u, axis=-1, keepdims=True)).astype(BF16)
        dkv16 = dkvp[...].astype(BF16)
        dwkv_ref[...] += lax.dot_general(kvn, dkv16, tn, preferred_element_type=F32)
        dkvn = lax.dot_general(dkv16, wkv_ref[...], nt, preferred_element_type=F32)
        dkvg_ref[...] += jnp.sum(dkvn * ckv * rk, axis=0, keepdims=True)
        u = dkvn * kvg_ref[...]
        dckv_ref[...] = (rk * u - ckv * (rk * rk * rk) * jnp.mean(ckv * u, axis=-1, keepdims=True)).astype(BF16)

    cq_s, ckv_s, kr_s, tab_s = _mla_specs(tm)
    hd = pl.BlockSpec((tm, HP), lambda i: (i, 0))
    return _call(
        body, name=name, grid=(s // tm,),
        in_specs=[cq_s, ckv_s, kr_s, tab_s, tab_s, tab_s, _const((1, MLA_Q_RANK)), _const((1, MLA_KV_RANK)),
                  _const((MLA_Q_RANK, HP)), _const((MLA_KV_RANK, 2 * HP)), _const((1, LANES)), _const((1, LANES)),
                  hd, hd, hd],
        out_specs=[pl.BlockSpec((tm, MLA_Q_RANK), lambda i: (i, 0)), pl.BlockSpec((tm, MLA_KV_RANK), lambda i: (i, 0)),
                   pl.BlockSpec((tm, LANES), lambda i: (i, 0)), _const((MLA_Q_RANK, HP)), _const((MLA_KV_RANK, 2 * HP)),
                   _const((1, MLA_Q_RANK)), _const((1, MLA_KV_RANK)), _const((1, LANES)), _const((1, LANES))],
        out_shape=[jax.ShapeDtypeStruct((s, MLA_Q_RANK), BF16), jax.ShapeDtypeStruct((s, MLA_KV_RANK), BF16),
                   jax.ShapeDtypeStruct((s, LANES), BF16), jax.ShapeDtypeStruct((MLA_Q_RANK, HP), F32),
                   jax.ShapeDtypeStruct((MLA_KV_RANK, 2 * HP), F32), jax.ShapeDtypeStruct((1, MLA_Q_RANK), F32),
                   jax.ShapeDtypeStruct((1, MLA_KV_RANK), F32), jax.ShapeDtypeStruct((1, LANES), F32),
                   jax.ShapeDtypeStruct((1, LANES), F32)],
        scratch_shapes=[pltpu.VMEM((tm, HP), F32), pltpu.VMEM((tm, 2 * HP), F32)],
        args=(p, p, p, ct, s1, s2, qn_g, kvn_g, wuq, wkv, gq, gk, dq, dk, dv), semantics=("arbitrary",), side=side)


_ATT_SCALE = MLA_QK ** -0.5
_LOG2E = 1.4426950408889634
_Q_SCALE = _ATT_SCALE * _LOG2E
ATT_BLOCK = 1024
_NEG = -1e30
_NT = (((1,), (1,)), ((), ()))
_TN = (((0,), (0,)), ((), ()))


def _tri_rows(step, n):
    i = step * 0
    for m in range(1, n):
        i = i + (step >= m * (m + 1) // 2).astype(jnp.int32)
    return i, step - i * (i + 1) // 2


def _tri_cols(step, n):
    j = step * 0
    for m in range(1, n):
        j = j + (step >= m * n - m * (m - 1) // 2).astype(jnp.int32)
    return j, j + step - (j * n - j * (j - 1) // 2)


def _diag_mask(t):
    return lax.broadcasted_iota(jnp.int32, (t, t), 0) <= lax.broadcasted_iota(jnp.int32, (t, t), 1)


def _attn_fwd(q, k, v, name, side=None):
    s = q.shape[0]
    t = _pick(s, ATT_BLOCK)
    n = s // t

    def body(q_ref, k_ref, v_ref, o_ref, lse_ref, m_s, l_s, acc):
        i, j = _tri_rows(pl.program_id(1), n)

        @pl.when(j == 0)
        def _():
            m_s[...] = jnp.full_like(m_s, _NEG)
            l_s[...] = jnp.zeros_like(l_s)
            acc[...] = jnp.zeros_like(acc)

        def step(diagonal):
            sc = lax.dot_general(k_ref[...], q_ref[...], _NT, preferred_element_type=F32)
            if diagonal:
                sc = jnp.where(_diag_mask(t), sc, _NEG)
            m_new = jnp.maximum(m_s[...], jnp.max(sc, axis=0, keepdims=True))
            alpha = jnp.exp2(m_s[...] - m_new)
            pr = jnp.exp2(sc - m_new)
            l_s[...] = alpha * l_s[...] + jnp.sum(pr, axis=0, keepdims=True)
            acc[...] = alpha * acc[...] + lax.dot_general(v_ref[...], pr.astype(BF16), _TN, preferred_element_type=F32)
            m_s[...] = m_new

        @pl.when(j < i)
        def _():
            step(False)

        @pl.when(j == i)
        def _():
            step(True)
            o_ref[...] = (acc[...] / l_s[...]).T
            lse_ref[...] = m_s[...] + jnp.log2(l_s[...])

    qs = pl.BlockSpec((t, LANES), lambda h, p: (_tri_rows(p, n)[0], h))
    ks = pl.BlockSpec((t, LANES), lambda h, p: (_tri_rows(p, n)[1], h))
    return _call(
        body, name=name, grid=(MLA_HEADS, n * (n + 1) // 2), in_specs=[qs, ks, ks],
        out_specs=[qs, pl.BlockSpec((None, 1, t), lambda h, p: (h, 0, _tri_rows(p, n)[0]))],
        out_shape=[jax.ShapeDtypeStruct((s, HP), F32), jax.ShapeDtypeStruct((MLA_HEADS, 1, s), F32)],
        scratch_shapes=[pltpu.VMEM((1, t), F32), pltpu.VMEM((1, t), F32), pltpu.VMEM((LANES, t), F32)],
        args=(q, k, v), semantics=("parallel", "arbitrary"), side=side)


def _attn_bwd_dq(q, k, v, o, lse, do, name, side=None):
    s = q.shape[0]
    t = _pick(s, ATT_BLOCK)
    n = s // t

    def body(q_ref, k_ref, v_ref, o_ref, lse_ref, do_ref, dq_ref, dl_ref, acc, dl_s):
        i, j = _tri_rows(pl.program_id(1), n)

        @pl.when(j == 0)
        def _():
            acc[...] = jnp.zeros_like(acc)
            dl_s[...] = jnp.sum((do_ref[...] * o_ref[...]).T, axis=0, keepdims=True)

        def step(diagonal):
            sc = lax.dot_general(k_ref[...], q_ref[...], _NT, preferred_element_type=F32)
            if diagonal:
                sc = jnp.where(_diag_mask(t), sc, _NEG)
            pr = jnp.exp2(sc - lse_ref[...])
            dp = lax.dot_general(v_ref[...], do_ref[...].astype(BF16), _NT, preferred_element_type=F32)
            ds = (pr * (dp - dl_s[...])).astype(BF16)
            acc[...] += lax.dot_general(k_ref[...], ds, _TN, preferred_element_type=F32)

        @pl.when(j < i)
        def _():
            step(False)

        @pl.when(j == i)
        def _():
            step(True)
            dq_ref[...] = (acc[...] * _ATT_SCALE).T
            dl_ref[...] = dl_s[...]

    qs = pl.BlockSpec((t, LANES), lambda h, p: (_tri_rows(p, n)[0], h))
    ks = pl.BlockSpec((t, LANES), lambda h, p: (_tri_rows(p, n)[1], h))
    ls = pl.BlockSpec((None, 1, t), lambda h, p: (h, 0, _tri_rows(p, n)[0]))
    return _call(
        body, name=name, grid=(MLA_HEADS, n * (n + 1) // 2), in_specs=[qs, ks, ks, qs, ls, qs], out_specs=[qs, ls],
        out_shape=[jax.ShapeDtypeStruct((s, HP), F32), jax.ShapeDtypeStruct((MLA_HEADS, 1, s), F32)],
        scratch_shapes=[pltpu.VMEM((LANES, t), F32), pltpu.VMEM((1, t), F32)],
        args=(q, k, v, o, lse, do), semantics=("parallel", "arbitrary"), side=side)


def _attn_bwd_dkv(q, k, v, lse, delta, do, name, side=None):
    s = q.shape[0]
    t = _pick(s, ATT_BLOCK)
    n = s // t

    def body(q_ref, k_ref, v_ref, lse_ref, dl_ref, do_ref, dk_ref, dv_ref, dk_acc, dv_acc):
        j, i = _tri_cols(pl.program_id(1), n)

        def step(diagonal):
            sc = lax.dot_general(k_ref[...], q_ref[...], _NT, preferred_element_type=F32)
            if diagonal:
                sc = jnp.where(_diag_mask(t), sc, _NEG)
            pr = jnp.exp2(sc - lse_ref[...])
            do16 = do_ref[...].astype(BF16)
            dv_acc[...] += jnp.dot(pr.astype(BF16), do16, preferred_element_type=F32)
            dp = lax.dot_general(v_ref[...], do16, _NT, preferred_element_type=F32)
            ds = (pr * (dp - dl_ref[...])).astype(BF16)
            dk_acc[...] += jnp.dot(ds, q_ref[...], preferred_element_type=F32)

        @pl.when(i == j)
        def _():
            dk_acc[...] = jnp.zeros_like(dk_acc)
            dv_acc[...] = jnp.zeros_like(dv_acc)
            step(True)

        @pl.when(i > j)
        def _():
            step(False)

        @pl.when(i == n - 1)
        def _():
            dk_ref[...] = dk_acc[...] * (1.0 / _LOG2E)
            dv_ref[...] = dv_acc[...]

    qs = pl.BlockSpec((t, LANES), lambda h, p: (_tri_cols(p, n)[1], h))
    ks = pl.BlockSpec((t, LANES), lambda h, p: (_tri_cols(p, n)[0], h))
    ls = pl.BlockSpec((None, 1, t), lambda h, p: (h, 0, _tri_cols(p, n)[1]))
    return _call(
        body, name=name, grid=(MLA_HEADS, n * (n + 1) // 2), in_specs=[qs, ks, ks, ls, ls, qs], out_specs=[ks, ks],
        out_shape=[jax.ShapeDtypeStruct((s, HP), F32)] * 2,
        scratch_shapes=[pltpu.VMEM((t, LANES), F32), pltpu.VMEM((t, LANES), F32)],
        args=(q, k, v, lse, delta, do), semantics=("parallel", "arbitrary"), side=side)


XBC = HP + 2 * SSD_GROUPS * SSD_STATE
BCW = 2 * SSD_GROUPS * SSD_STATE


def _conv_fwd(p, col0, width, conv_w, conv_b, name):
    s = p.shape[0]
    c0, nblk = col0 // LANES, width // LANES

    def body(x_ref, w_ref, b_ref, o_ref, pad):
        pad[0:8, :] = jnp.zeros((8, LANES), F32)
        pad[8:s + 8, :] = x_ref[...].astype(F32)
        acc = jnp.broadcast_to(b_ref[...], (s, LANES))
        for t in range(SSD_CONV):
            acc = acc + pad[pl.ds(8 - (SSD_CONV - 1) + t, s), :] * w_ref[t:t + 1, :]
        o_ref[...] = acc * _sigmoid(acc)

    return pl.pallas_call(
        body, name=name, grid=(nblk,),
        in_specs=[pl.BlockSpec((s, LANES), lambda j: (0, c0 + j)), pl.BlockSpec((SSD_CONV, LANES), lambda j: (0, j)),
                  pl.BlockSpec((1, LANES), lambda j: (0, j))],
        out_specs=pl.BlockSpec((s, LANES), lambda j: (0, j)), out_shape=jax.ShapeDtypeStruct((s, width), F32),
        scratch_shapes=[pltpu.VMEM((s + 8, LANES), F32)], compiler_params=_params("parallel"))(p, conv_w, conv_b)


def _conv_bwd(p, col0, width, conv_w, conv_b, dact, name):
    s = p.shape[0]
    c0, nblk = col0 // LANES, width // LANES

    def body(x_ref, w_ref, b_ref, d_ref, dx_ref, dw_ref, db_ref, pad, padd):
        pad[0:8, :] = jnp.zeros((8, LANES), F32)
        pad[8:s + 8, :] = x_ref[...].astype(F32)
        acc = jnp.broadcast_to(b_ref[...], (s, LANES))
        for t in range(SSD_CONV):
            acc = acc + pad[pl.ds(8 - (SSD_CONV - 1) + t, s), :] * w_ref[t:t + 1, :]
        sg = _sigmoid(acc)
        dpre = d_ref[...] * (sg * (1.0 + acc * (1.0 - sg)))
        padd[0:s, :] = dpre
        padd[s:s + 8, :] = jnp.zeros((8, LANES), F32)
        dx = jnp.zeros((s, LANES), F32)
        for t in range(SSD_CONV):
            dx = dx + padd[pl.ds(SSD_CONV - 1 - t, s), :] * w_ref[t:t + 1, :]
            dw_ref[t:t + 1, :] = jnp.sum(dpre * pad[pl.ds(8 - (SSD_CONV - 1) + t, s), :], axis=0, keepdims=True)
        dx_ref[...] = dx.astype(BF16)
        db_ref[...] = jnp.sum(dpre, axis=0, keepdims=True)

    blk = pl.BlockSpec((s, LANES), lambda j: (0, j))
    return pl.pallas_call(
        body, name=name, grid=(nblk,),
        in_specs=[pl.BlockSpec((s, LANES), lambda j: (0, c0 + j)), pl.BlockSpec((SSD_CONV, LANES), lambda j: (0, j)),
                  pl.BlockSpec((1, LANES), lambda j: (0, j)), blk],
        out_specs=[blk, pl.BlockSpec((SSD_CONV, LANES), lambda j: (0, j)), pl.BlockSpec((1, LANES), lambda j: (0, j))],
        out_shape=[jax.ShapeDtypeStruct((s, width), BF16), jax.ShapeDtypeStruct((SSD_CONV, width), F32),
                   jax.ShapeDtypeStruct((1, width), F32)],
        scratch_shapes=[pltpu.VMEM((s + 8, LANES), F32), pltpu.VMEM((s + 8, LANES), F32)],
        compiler_params=_params("parallel"))(p, conv_w, conv_b, dact)


def _softplus(x):
    return jnp.maximum(x, 0.0) + jnp.log(1.0 + jnp.exp(-jnp.abs(x)))


def _dt_fwd(p, dt_bias, a_log, name):
    s = p.shape[0]
    tm = _pick(s, 512)

    def body(x_ref, b_ref, a_ref, dt_ref, da_ref):
        dtv = _softplus(x_ref[...] + b_ref[...])
        dav = dtv * (-jnp.exp(a_ref[...]))
        for h in range(SSD_HEADS):
            hs = slice(h * LANES, (h + 1) * LANES)
            dt_ref[:, hs] = jnp.broadcast_to(dtv[:, h:h + 1], (tm, LANES))
            da_ref[:, hs] = jnp.broadcast_to(dav[:, h:h + 1], (tm, LANES))

    out = pl.BlockSpec((tm, HP), lambda i: (i, 0))
    return pl.pallas_call(
        body, name=name, grid=(s // tm,),
        in_specs=[pl.BlockSpec((tm, LANES), lambda i: (i, T_DT // LANES)), _const((1, LANES)), _const((1, LANES))],
        out_specs=[out, out], out_shape=[jax.ShapeDtypeStruct((s, HP), F32)] * 2,
        compiler_params=_params("parallel"))(p, dt_bias, a_log)


def _dt_bwd(p, dt_bias, a_log, dda, ddtx, name):
    s = p.shape[0]
    tm = _pick(s, 512)

    def body(x_ref, b_ref, a_ref, dda_ref, ddtx_ref, dx_ref, db_ref, dal_ref):
        @pl.when(pl.program_id(0) == 0)
        def _():
            db_ref[...] = jnp.zeros_like(db_ref)
            dal_ref[...] = jnp.zeros_like(dal_ref)

        x = x_ref[...] + b_ref[...]
        dtv = _softplus(x)
        av = -jnp.exp(a_ref[...])
        lane = lax.broadcasted_iota(jnp.int32, (tm, LANES), 1)
        pa = jnp.zeros((tm, LANES), F32)
        px = jnp.zeros((tm, LANES), F32)
        for h in range(SSD_HEADS):
            pa = jnp.where(lane == h, dda_ref[:, h * LANES:(h + 1) * LANES], pa)
            px = jnp.where(lane == h, ddtx_ref[:, h * LANES:(h + 1) * LANES], px)
        draw = (pa * av + px) * _sigmoid(x)
        dx_ref[...] = draw.astype(BF16)
        db_ref[...] += jnp.sum(draw, axis=0, keepdims=True)
        dal_ref[...] += jnp.sum(pa * dtv, axis=0, keepdims=True) * av

    hd = pl.BlockSpec((tm, HP), lambda i: (i, 0))
    return pl.pallas_call(
        body, name=name, grid=(s // tm,),
        in_specs=[pl.BlockSpec((tm, LANES), lambda i: (i, T_DT // LANES)), _const((1, LANES)), _const((1, LANES)), hd, hd],
        out_specs=[pl.BlockSpec((tm, LANES), lambda i: (i, 0)), _const((1, LANES)), _const((1, LANES))],
        out_shape=[jax.ShapeDtypeStruct((s, LANES), BF16), jax.ShapeDtypeStruct((1, LANES), F32),
                   jax.ShapeDtypeStruct((1, LANES), F32)],
        compiler_params=_params("arbitrary"))(p, dt_bias, a_log, dda, ddtx)


def _cumsum_rows(x):
    row = lax.broadcasted_iota(jnp.int32, x.shape, 0)
    k = 1
    while k < x.shape[0]:
        x = x + jnp.where(row >= k, pltpu.roll(x, k, 0), 0.0)
        k *= 2
    return x


def _rev_cumsum_rows(x):
    n = x.shape[0]
    row = lax.broadcasted_iota(jnp.int32, x.shape, 0)
    k = 1
    while k < n:
        x = x + jnp.where(row < n - k, pltpu.roll(x, n - k, 0), 0.0)
        k *= 2
    return x


HPG = SSD_HEADS // SSD_GROUPS


def _chunk_decay(da):
    cs = _cumsum_rows(da)
    lm = jnp.exp(jnp.where(_tril_mask(), cs - cs.T, _NEG))
    return cs, lm, cs[CHUNK - 1:CHUNK, :]


def _scan_fwd(xs, bc, dtb, dab, name, side=None):
    s = xs.shape[0]
    nc = s // CHUNK

    def body(x_ref, b_ref, c_ref, dt_ref, da_ref, y_ref, sin_ref, state):
        @pl.when(pl.program_id(1) == 0)
        def _():
            state[...] = jnp.zeros_like(state)

        bv = b_ref[...]
        b16, c16 = bv.astype(BF16), c_ref[...].astype(BF16)
        g = lax.dot_general(c16, b16, _NT, preferred_element_type=F32)
        for hh in range(HPG):
            hs = slice(hh * LANES, (hh + 1) * LANES)
            st = state[hh]
            sin_ref[hh] = st
            cs, lm, cl = _chunk_decay(da_ref[:, hs])
            xd = (x_ref[:, hs] * dt_ref[:, hs]).astype(BF16)
            y = jnp.dot((g * lm).astype(BF16), xd, preferred_element_type=F32)
            y_ref[:, hs] = y + jnp.dot(c16, st.astype(BF16), preferred_element_type=F32) * jnp.exp(cs)
            bd = (bv * jnp.exp(cl - cs)).astype(BF16)
            state[hh] = jnp.exp(cl) * st + lax.dot_general(bd, xd, _TN, preferred_element_type=F32)

    gw = HPG * LANES
    hd = pl.BlockSpec((CHUNK, gw), lambda g, c: (c, g))
    return _call(
        body, name=name, grid=(SSD_GROUPS, nc),
        in_specs=[hd, pl.BlockSpec((CHUNK, LANES), lambda g, c: (c, g)),
                  pl.BlockSpec((CHUNK, LANES), lambda g, c: (c, SSD_GROUPS + g)), hd, hd],
        out_specs=[hd, pl.BlockSpec((HPG, None, SSD_STATE, LANES), lambda g, c: (g, c, 0, 0))],
        out_shape=[jax.ShapeDtypeStruct((s, HP), F32), jax.ShapeDtypeStruct((SSD_HEADS, nc, SSD_STATE, LANES), F32)],
        scratch_shapes=[pltpu.VMEM((HPG, SSD_STATE, LANES), F32)],
        args=(xs, bc, bc, dtb, dab), semantics=("parallel", "arbitrary"), side=side)


def _scan_bwd(xs, bc, dtb, dab, s_in, dy, d_vec, name):
    s = xs.shape[0]
    nc = s // CHUNK

    def body(x_ref, b_ref, c_ref, dt_ref, da_ref, sin_ref, dy_ref, dv_ref, dx_ref, db_ref, dc_ref, dda_ref, ddtx_ref, dstate):
        @pl.when(pl.program_id(1) == 0)
        def _():
            dstate[...] = jnp.zeros_like(dstate)

        bv = b_ref[...]
        b16, c16 = bv.astype(BF16), c_ref[...].astype(BF16)
        g = lax.dot_general(c16, b16, _NT, preferred_element_type=F32)
        row = lax.broadcasted_iota(jnp.int32, (CHUNK, 1), 0)
        dbm = jnp.zeros((CHUNK, SSD_STATE), F32)
        dcm = jnp.zeros((CHUNK, SSD_STATE), F32)
        for hh in range(HPG):
            hs = slice(hh * LANES, (hh + 1) * LANES)
            st, ds = sin_ref[hh], dstate[hh]
            st16, ds16 = st.astype(BF16), ds.astype(BF16)
            xv, dtv, dyv = x_ref[:, hs], dt_ref[:, hs], dy_ref[:, hs]
            cs, lm, cl = _chunk_decay(da_ref[:, hs])
            ecs, ecl = jnp.exp(cs), jnp.exp(cl)
            decay = jnp.exp(cl - cs)
            xd = (xv * dtv).astype(BF16)
            dy16 = dyv.astype(BF16)
            dye = (dyv * ecs).astype(BF16)
            yoff = jnp.dot(c16, st16, preferred_element_type=F32) * ecs
            dcs = jnp.sum(dyv * yoff, axis=-1, keepdims=True)
            dcm = dcm + lax.dot_general(dye, st16, _NT, preferred_element_type=F32)
            dstate[hh] = ecl * ds + lax.dot_general(c16, dye, _TN, preferred_element_type=F32)
            dcl = jnp.sum(jnp.sum(ds * st, axis=0, keepdims=True), axis=1, keepdims=True) * ecl[:, 0:1]
            bd32 = bv * decay
            qm = lax.dot_general(xd, ds16, _NT, preferred_element_type=F32)
            dbm = dbm + qm * decay
            w = jnp.sum(bd32 * qm, axis=-1, keepdims=True)
            dcs = dcs - w
            dcl = dcl + jnp.sum(w, axis=0, keepdims=True)
            dxd = jnp.dot(bd32.astype(BF16), ds16, preferred_element_type=F32)
            m16 = (g * lm).astype(BF16)
            dm = lax.dot_general(dy16, xd, _NT, preferred_element_type=F32)
            dxd = dxd + lax.dot_general(m16, dy16, _TN, preferred_element_type=F32)
            dg = dm * lm
            dg16 = dg.astype(BF16)
            tt = dg * g
            dcm = dcm + jnp.dot(dg16, b16, preferred_element_type=F32)
            dbm = dbm + lax.dot_general(dg16, c16, _TN, preferred_element_type=F32)
            dcs = dcs + jnp.sum(tt, axis=-1, keepdims=True) - jnp.sum(tt.T, axis=-1, keepdims=True)
            dcs = dcs + jnp.where(row == CHUNK - 1, dcl, 0.0)
            dda_ref[:, hs] = _rev_cumsum_rows(jnp.broadcast_to(dcs, (CHUNK, LANES)))
            ddtx_ref[:, hs] = jnp.broadcast_to(jnp.sum(dxd * xv, axis=-1, keepdims=True), (CHUNK, LANES))
            dx_ref[:, hs] = dxd * dtv + dyv * dv_ref[:, hs]
        db_ref[...] = dbm
        dc_ref[...] = dcm

    gw = HPG * LANES
    hd = pl.BlockSpec((CHUNK, gw), lambda g, c: (nc - 1 - c, g))
    gp = pl.BlockSpec((CHUNK, LANES), lambda g, c: (nc - 1 - c, g))
    return pl.pallas_call(
        body, name=name, grid=(SSD_GROUPS, nc),
        in_specs=[hd, gp, pl.BlockSpec((CHUNK, LANES), lambda g, c: (nc - 1 - c, SSD_GROUPS + g)), hd, hd,
                  pl.BlockSpec((HPG, None, SSD_STATE, LANES), lambda g, c: (g, nc - 1 - c, 0, 0)), hd,
                  pl.BlockSpec((1, gw), lambda g, c: (0, g))],
        out_specs=[hd, gp, gp, hd, hd],
        out_shape=[jax.ShapeDtypeStruct((s, HP), F32), jax.ShapeDtypeStruct((s, SSD_GROUPS * SSD_STATE), F32),
                   jax.ShapeDtypeStruct((s, SSD_GROUPS * SSD_STATE), F32), jax.ShapeDtypeStruct((s, HP), F32),
                   jax.ShapeDtypeStruct((s, HP), F32)],
        scratch_shapes=[pltpu.VMEM((HPG, SSD_STATE, LANES), F32)],
        compiler_params=_params("parallel", "arbitrary"))(xs, bc, bc, dtb, dab, s_in, dy, d_vec)


_GN = SSD_INNER // SSD_GROUPS
_GW = HP // SSD_GROUPS


def _ssd_post_fwd(y, xbc, p, d_vec, gain, name):
    s = y.shape[0]
    tm = _pick(s, 512)

    def body(y_ref, x_ref, z_ref, d_ref, g_ref, o_ref):
        z = z_ref[...].astype(F32)
        y2 = (y_ref[...] + x_ref[...] * d_ref[...]) * (z * _sigmoid(z))
        for g in range(SSD_GROUPS):
            gs = slice(g * _GW, (g + 1) * _GW)
            yg = y2[:, gs]
            r = lax.rsqrt(jnp.sum(yg * yg, axis=-1, keepdims=True) * (1.0 / _GN) + EPS)
            o_ref[:, gs] = (yg * r * g_ref[:, gs]).astype(BF16)

    hd = pl.BlockSpec((tm, HP), lambda i: (i, 0))
    return pl.pallas_call(
        body, name=name, grid=(s // tm,),
        in_specs=[hd, hd, pl.BlockSpec((tm, HP), lambda i: (i, C_Z // HP)), _const((1, HP)), _const((1, HP))],
        out_specs=hd, out_shape=jax.ShapeDtypeStruct((s, HP), BF16), compiler_params=_params("parallel"))(y, xbc, p, d_vec, gain)


def _ssd_post_bwd(y, xbc, p, d_vec, gain, dyn, name):
    s = y.shape[0]
    tm = _pick(s, 512)

    def body(y_ref, x_ref, z_ref, d_ref, g_ref, dn_ref, dy_ref, dz_ref, dg_ref, dd_ref):
        @pl.when(pl.program_id(0) == 0)
        def _():
            dg_ref[...] = jnp.zeros_like(dg_ref)
            dd_ref[...] = jnp.zeros_like(dd_ref)

        z, xv = z_ref[...].astype(F32), x_ref[...]
        sg = _sigmoid(z)
        sz = z * sg
        yt = y_ref[...] + xv * d_ref[...]
        y2 = yt * sz
        for g in range(SSD_GROUPS):
            gs = slice(g * _GW, (g + 1) * _GW)
            yg, dn = y2[:, gs], dn_ref[:, gs]
            r = lax.rsqrt(jnp.sum(yg * yg, axis=-1, keepdims=True) * (1.0 / _GN) + EPS)
            u = dn * g_ref[:, gs]
            dy2 = r * u - yg * (r * r * r) * (jnp.sum(yg * u, axis=-1, keepdims=True) * (1.0 / _GN))
            dg_ref[:, gs] += jnp.sum(dn * yg * r, axis=0, keepdims=True)
            dyt = dy2 * sz[:, gs]
            dy_ref[:, gs] = dyt
            dz_ref[:, gs] = (dy2 * yt[:, gs] * (sg[:, gs] * (1.0 + z[:, gs] * (1.0 - sg[:, gs])))).astype(BF16)
            dd_ref[:, gs] += jnp.sum(dyt * xv[:, gs], axis=0, keepdims=True)

    hd = pl.BlockSpec((tm, HP), lambda i: (i, 0))
    return pl.pallas_call(
        body, name=name, grid=(s // tm,),
        in_specs=[hd, hd, pl.BlockSpec((tm, HP), lambda i: (i, C_Z // HP)), _const((1, HP)), _const((1, HP)), hd],
        out_specs=[hd, hd, _const((1, HP)), _const((1, HP))],
        out_shape=[jax.ShapeDtypeStruct((s, HP), F32), jax.ShapeDtypeStruct((s, HP), BF16),
                   jax.ShapeDtypeStruct((1, HP), F32), jax.ShapeDtypeStruct((1, HP), F32)],
        compiler_params=_params("arbitrary"))(y, xbc, p, d_vec, gain, dyn)


def _merge_fwd(p, ya, o, yc, wb0, wb1, wb2, w_out, x, name, side=None):
    s = p.shape[0]
    tm = _pick(s, 512)

    def body(g_ref, ya_ref, o_ref, yc_ref, w0_ref, w1_ref, w2_ref, wo_ref, x_ref, mg_ref, y_ref):
        acc = jnp.zeros((tm, D_MODEL), F32)
        for i, (b_ref, w_ref) in enumerate(((ya_ref, w0_ref), (o_ref, w1_ref), (yc_ref, w2_ref))):
            t = jnp.dot(b_ref[...].astype(BF16), w_ref[...], preferred_element_type=F32)
            acc = acc + _sigmoid(g_ref[:, i * D_MODEL:(i + 1) * D_MODEL].astype(F32)) * t
        mg = acc.astype(BF16)
        mg_ref[...] = mg
        y_ref[...] = x_ref[...] + jnp.dot(mg, wo_ref[...], preferred_element_type=F32)

    row = pl.BlockSpec((tm, D_MODEL), lambda i: (i, 0))
    return _call(
        body, name=name, grid=(s // tm,),
        in_specs=[pl.BlockSpec((tm, 3 * D_MODEL), lambda i: (i, C_G // (3 * D_MODEL))),
                  pl.BlockSpec((tm, GM_WIDTH), lambda i: (i, 0)), row, row,
                  _resident((GM_WIDTH, D_MODEL)), _resident((HP, D_MODEL)), _resident((HP, D_MODEL)),
                  _resident((D_MODEL, D_MODEL)), row],
        out_specs=[row, row],
        out_shape=[jax.ShapeDtypeStruct((s, D_MODEL), BF16), jax.ShapeDtypeStruct((s, D_MODEL), F32)],
        scratch_shapes=[], args=(p, ya, o, yc, wb0, wb1, wb2, w_out, x), semantics=("parallel",), side=side)


def _merge_bwd(p, ya, o, yc, wb0, wb1, wb2, w_out, dy, name):
    s = p.shape[0]
    tm = _pick(s, 512)

    def body(g_ref, ya_ref, o_ref, yc_ref, w0_ref, w1_ref, w2_ref, wo_ref, dy_ref,
             d0_ref, d1_ref, d2_ref, dg_ref, dya_ref, do_ref, dyc_ref):
        dm = lax.dot_general(dy_ref[...].astype(BF16), wo_ref[...], _NT, preferred_element_type=F32)
        for i, (b_ref, w_ref, d_ref, db_ref) in enumerate(((ya_ref, w0_ref, d0_ref, dya_ref), (o_ref, w1_ref, d1_ref, do_ref),
                                                            (yc_ref, w2_ref, d2_ref, dyc_ref))):
            cs = slice(i * D_MODEL, (i + 1) * D_MODEL)
            t = jnp.dot(b_ref[...].astype(BF16), w_ref[...], preferred_element_type=F32)
            sg = _sigmoid(g_ref[:, cs].astype(F32))
            dt16 = (dm * sg).astype(BF16)
            d_ref[...] = dt16
            dg_ref[:, cs] = (dm * t * sg * (1.0 - sg)).astype(BF16)
            db_ref[...] = lax.dot_general(dt16, w_ref[...], _NT, preferred_element_type=F32)

    row = pl.BlockSpec((tm, D_MODEL), lambda i: (i, 0))
    nar = pl.BlockSpec((tm, GM_WIDTH), lambda i: (i, 0))
    wide = pl.BlockSpec((tm, 3 * D_MODEL), lambda i: (i, 0))
    return pl.pallas_call(
        body, name=name, grid=(s // tm,),
        in_specs=[pl.BlockSpec((tm, 3 * D_MODEL), lambda i: (i, C_G // (3 * D_MODEL))), nar, row, row,
                  _resident((GM_WIDTH, D_MODEL)), _resident((HP, D_MODEL)), _resident((HP, D_MODEL)),
                  _resident((D_MODEL, D_MODEL)), row],
        out_specs=[row, row, row, wide, nar, row, row],
        out_shape=[jax.ShapeDtypeStruct((s, D_MODEL), BF16)] * 3 + [jax.ShapeDtypeStruct((s, 3 * D_MODEL), BF16),
                   jax.ShapeDtypeStruct((s, GM_WIDTH), F32), jax.ShapeDtypeStruct((s, D_MODEL), F32),
                   jax.ShapeDtypeStruct((s, D_MODEL), F32)],
        compiler_params=_params("parallel"))(p, ya, o, yc, wb0, wb1, wb2, w_out, dy)


def _loss_head(y, target, name):
    s, d = y.shape
    tm = _pick(s, 512)

    def body(y_ref, t_ref, dy_ref, sq_ref):
        @pl.when(pl.program_id(0) == 0)
        def _():
            sq_ref[...] = jnp.zeros_like(sq_ref)

        e = y_ref[...] - t_ref[...]
        dy_ref[...] = e * (1.0 / d)
        sq_ref[...] += jnp.sum(e * e, axis=0, keepdims=True)

    row = pl.BlockSpec((tm, d), lambda i: (i, 0))
    return pl.pallas_call(
        body, name=name, grid=(s // tm,), in_specs=[row, row], out_specs=[row, _const((1, d))],
        out_shape=[jax.ShapeDtypeStruct((s, d), F32), jax.ShapeDtypeStruct((1, d), F32)],
        compiler_params=_params("arbitrary"))(y, target)


def _adamw(w, g, m, v, name):
    rows, cols = w.shape
    tr = rows
    for cand in (512, 256, 128, 64, 32, 16, 8):
        if rows % cand == 0 and cand * cols * 4 <= 3 * 1024 * 1024:
            tr = cand
            break

    def body(w_ref, g_ref, m_ref, v_ref, d_ref, nm_ref, nv_ref):
        d_ref[...], nm_ref[...], nv_ref[...] = _adam_update(w_ref[...], g_ref[...], m_ref[...], v_ref[...])

    blk = pl.BlockSpec((tr, cols), lambda i: (i, 0))
    return pl.pallas_call(
        body, name=name, grid=(rows // tr,), in_specs=[blk] * 4, out_specs=[blk] * 3,
        out_shape=[jax.ShapeDtypeStruct((rows, cols), F32)] * 3, compiler_params=_params("parallel"))(w, g, m, v)


def _adam_update(w, g, m, v):
    nm = ADAM_B1 * m + (1.0 - ADAM_B1) * g
    nv = ADAM_B2 * v + (1.0 - ADAM_B2) * (g * g)
    c1 = 1.0 - ADAM_B1 ** ADAM_STEP
    c2 = 1.0 - ADAM_B2 ** ADAM_STEP
    return -ADAM_LR * ((nm / c1) / (jnp.sqrt(nv / c2) + ADAM_EPS) + ADAM_WD * w), nm, nv


def _adamw_sharded(w, m, v, mine, theirs, name, side=None):
    depth, rows, cols = w.shape
    tr = _row_tile(rows // 2, cols, 1024 * 1024)
    nb = rows // 2 // tr

    def body(w_ref, m_ref, v_ref, a_ref, b_ref, g_ref, d_ref, nm_ref, nv_ref):
        c = lax.axis_index("c")
        g = jnp.where(pl.program_id(1) // nb == c, a_ref[...], b_ref[...])
        g_ref[...] = g
        d_ref[...], nm_ref[...], nv_ref[...] = _adam_update(w_ref[...], g, m_ref[...], v_ref[...])

    blk = pl.BlockSpec((None, tr, cols), lambda l, i: (l, i, 0))
    half = pl.BlockSpec((None, tr, cols), lambda l, i: (l, i % nb, 0))
    return _call(
        body, name=name, grid=(depth, rows // tr), in_specs=[blk, blk, blk, half, half], out_specs=[blk] * 4,
        out_shape=[jax.ShapeDtypeStruct((depth, rows, cols), F32)] * 4, scratch_shapes=[],
        args=(w, m, v, mine, theirs), semantics=("parallel", "parallel"), side=side)


ANY = pl.BlockSpec(memory_space=pl.ANY)


def _me():
    return lax.axis_index("x"), lax.axis_index("y"), lax.axis_index("c")


def _other_chips(x, y):
    return [(1 - x, y), (x, 1 - y), (1 - x, 1 - y)]


def _chip_index(cx, cy):
    return 2 * cx + cy


class _Exchange:
    def __init__(self, ins, out_shapes, n_sems, start, finish):
        self.ins, self.out_shapes, self.n_sems, self.start, self.finish = list(ins), list(out_shapes), n_sems, start, finish


def _sem_scratch(ex):
    return [pltpu.SemaphoreType.DMA((ex.n_sems,)), pltpu.SemaphoreType.DMA((ex.n_sems,))]


def _run_exchange(ex, name):
    n_in, n_out = len(ex.ins), len(ex.out_shapes)

    def body(*refs):
        in_refs, out_refs, (send, recv) = refs[:n_in], refs[n_in:n_in + n_out], refs[n_in + n_out:]
        ex.start(in_refs, out_refs, send, recv)
        ex.finish(in_refs, out_refs, send, recv)

    return pl.pallas_call(body, name=name, in_specs=[ANY] * n_in, out_specs=[ANY] * n_out, out_shape=ex.out_shapes,
                          scratch_shapes=_sem_scratch(ex))(*ex.ins)


def _call(body, *, name, grid, in_specs, out_specs, out_shape, scratch_shapes, args, semantics, side=None):
    if side is None:
        return pl.pallas_call(body, name=name, grid=grid, in_specs=in_specs, out_specs=out_specs, out_shape=out_shape,
                              scratch_shapes=scratch_shapes, compiler_params=_params(*semantics))(*args), []
    n_in, n_out, n_sc = len(in_specs), len(out_specs), len(scratch_shapes)
    s_in, s_out = len(side.ins), len(side.out_shapes)

    def hosted(*refs):
        pos = 0
        parts = []
        for size in (n_in, s_in, n_out, s_out, n_sc, 2):
            parts.append(refs[pos:pos + size])
            pos += size
        ins, sins, outs, souts, scratch, (send, recv) = parts
        ids = [pl.program_id(a) for a in range(len(grid))]
        first = functools.reduce(jnp.logical_and, [i == 0 for i in ids])
        last = functools.reduce(jnp.logical_and, [i == g - 1 for i, g in zip(ids, grid)])

        @pl.when(first)
        def _():
            side.start(sins, souts, send, recv)

        body(*ins, *outs, *scratch)

        @pl.when(last)
        def _():
            side.finish(sins, souts, send, recv)

    res = pl.pallas_call(
        hosted, name=name, grid=grid, in_specs=list(in_specs) + [ANY] * s_in, out_specs=list(out_specs) + [ANY] * s_out,
        out_shape=list(out_shape) + side.out_shapes, scratch_shapes=list(scratch_shapes) + _sem_scratch(side),
        compiler_params=_params(*["arbitrary"] * len(grid)))(*args, *side.ins)
    return res[:n_out], res[n_out:]


def _half(ref_rows, c):
    return pl.ds(c * (ref_rows // 2), ref_rows // 2)


def _gather_exchange(shards, layer):
    n = len(shards)
    rows = [a.shape[1] for a in shards]

    def copy(in_refs, out_refs, send, recv, t, k, chip, hc, to, from_input=False):
        dst = out_refs[t].at[chip, _half(rows[t], hc)]
        src = in_refs[t].at[layer, _half(rows[t], hc)] if from_input else dst
        return pltpu.make_async_remote_copy(src_ref=src, dst_ref=dst, send_sem=send.at[7 * t + k], recv_sem=recv.at[7 * t + k],
                                            device_id=to, device_id_type=MESH)

    def own(in_refs, out_refs, send, recv, t):
        x, y, c = _me()
        return pltpu.make_async_remote_copy(src_ref=in_refs[t].at[layer], dst_ref=out_refs[t].at[_chip_index(x, y)],
                                            send_sem=send.at[7 * t + 6], recv_sem=recv.at[7 * t + 6],
                                            device_id=(x, y, 1 - c), device_id_type=MESH)

    def start(in_refs, out_refs, send, recv):
        x, y, c = _me()
        for j, chip in enumerate(_other_chips(x, y)):
            for t in range(n):
                copy(in_refs, out_refs, send, recv, t, j, _chip_index(x, y), c, (*chip, c), from_input=True).start()
        for t in range(n):
            own(in_refs, out_refs, send, recv, t).start()

    def finish(in_refs, out_refs, send, recv):
        x, y, c = _me()
        chips = _other_chips(x, y)
        passed = []
        for t in range(n):
            own(in_refs, out_refs, send, recv, t).wait()
        for j, chip in enumerate(chips):
            for t in range(n):
                copy(in_refs, out_refs, send, recv, t, j, _chip_index(*chip), c, (x, y, c)).wait_recv()
                cp = copy(in_refs, out_refs, send, recv, t, 3 + j, _chip_index(*chip), c, (x, y, 1 - c))
                cp.start()
                passed.append(cp)
        for j, chip in enumerate(chips):
            for t in range(n):
                copy(in_refs, out_refs, send, recv, t, 3 + j, _chip_index(*chip), 1 - c, (x, y, c)).wait_recv()
                copy(in_refs, out_refs, send, recv, t, j, _chip_index(x, y), c, (*chip, c), from_input=True).wait_send()
        for cp in passed:
            cp.wait_send()

    return _Exchange(shards, [jax.ShapeDtypeStruct((N_CHIPS,) + a.shape[1:], a.dtype) for a in shards], 7 * n, start, finish)


def _pair_exchange(gs):
    n = len(gs)
    rows = [a.shape[1] for a in gs]

    def copies(in_refs, out_refs, send, recv):
        x, y, c = _me()
        return [pltpu.make_async_remote_copy(src_ref=in_refs[t].at[:, _half(rows[t], 1 - c)], dst_ref=out_refs[t],
                                             send_sem=send.at[t], recv_sem=recv.at[t], device_id=(x, y, 1 - c),
                                             device_id_type=MESH) for t in range(n)]

    def start(*refs):
        for cp in copies(*refs):
            cp.start()

    def finish(*refs):
        for cp in copies(*refs):
            cp.wait()

    return _Exchange(gs, [jax.ShapeDtypeStruct((N_CHIPS, a.shape[1] // 2, a.shape[2]), a.dtype) for a in gs], n, start, finish)


def _row_tile(rows, cols, budget=2 * 1024 * 1024):
    best = None
    for t in range(8, rows + 1, 8):
        if rows % t == 0 and t * cols * 4 <= budget:
            best = t
    return best or rows


def _pair_add(g, got, name):
    _, rows, cols = g.shape
    tr = _row_tile(rows // 2, cols)
    nb = rows // 2 // tr

    def body(lo_ref, hi_ref, r_ref, o16_ref, own_ref):
        x, y, c = _me()
        tot = jnp.where(c == 0, lo_ref[...], hi_ref[...]) + r_ref[...]
        o16_ref[...] = tot.astype(BF16)

        @pl.when(pl.program_id(1) == _chip_index(x, y))
        def _():
            own_ref[...] = tot

    blk = (None, tr, cols)
    return pl.pallas_call(
        body, name=name, grid=(nb, N_CHIPS),
        in_specs=[pl.BlockSpec(blk, lambda i, k: (k, i, 0)), pl.BlockSpec(blk, lambda i, k: (k, i + nb, 0)),
                  pl.BlockSpec(blk, lambda i, k: (k, i, 0))],
        out_specs=[pl.BlockSpec(blk, lambda i, k: (k, i, 0)), pl.BlockSpec((tr, cols), lambda i, k: (i, 0))],
        out_shape=[jax.ShapeDtypeStruct((N_CHIPS, rows // 2, cols), BF16), jax.ShapeDtypeStruct((rows // 2, cols), F32)],
        compiler_params=_params("parallel", "arbitrary"))(g, g, got)


def _chip_exchange(parts):
    n = len(parts)

    def copies(in_refs, out_refs, send, recv):
        x, y, c = _me()
        return [pltpu.make_async_remote_copy(src_ref=in_refs[t].at[_chip_index(*chip)], dst_ref=out_refs[t].at[j],
                                             send_sem=send.at[3 * t + j], recv_sem=recv.at[3 * t + j],
                                             device_id=(*chip, c), device_id_type=MESH)
                for j, chip in enumerate(_other_chips(x, y)) for t in range(n)]

    def start(*refs):
        for cp in copies(*refs):
            cp.start()

    def finish(*refs):
        for cp in copies(*refs):
            cp.wait()

    return _Exchange(parts, [jax.ShapeDtypeStruct((3,) + a.shape[1:], a.dtype) for a in parts], 3 * n, start, finish)


def _chip_add(own, got, name, layer, into=None):
    rows, cols = own.shape
    tr = _row_tile(rows, cols, 1024 * 1024)

    def body(own_ref, got_ref, *rest):
        acc = own_ref[...]
        for j in range(3):
            acc = acc + got_ref[j].astype(F32)
        rest[-1][...] = acc

    in_specs = [pl.BlockSpec((tr, cols), lambda i: (i, 0)), pl.BlockSpec((3, tr, cols), lambda i: (0, i, 0))]
    args, alias = [own, got], {}
    if into is not None:
        in_specs.append(ANY)
        args.append(into)
        alias = {2: 0}
    return pl.pallas_call(
        body, name=name, grid=(rows // tr,), in_specs=in_specs,
        out_specs=pl.BlockSpec((None, tr, cols), lambda i: (layer, i, 0)),
        out_shape=jax.ShapeDtypeStruct((DEPTH, rows, cols), F32), input_output_aliases=alias,
        compiler_params=_params("parallel"))(*args)


def _pair_share(halves):
    n = len(halves)

    def copies(in_refs, out_refs, send, recv):
        x, y, c = _me()
        return [pltpu.make_async_remote_copy(src_ref=in_refs[t], dst_ref=out_refs[t], send_sem=send.at[t],
                                             recv_sem=recv.at[t], device_id=(x, y, 1 - c), device_id_type=MESH)
                for t in range(n)]

    def start(*refs):
        for cp in copies(*refs):
            cp.start()

    def finish(*refs):
        for cp in copies(*refs):
            cp.wait()

    return _Exchange(halves, [jax.ShapeDtypeStruct(a.shape, a.dtype) for a in halves], n, start, finish)


N_DEV = 8


def _all_exchange(v):
    r, cols = v.shape

    def peers():
        x, y, c = _me()
        flip = lambda v, f: 1 - v if f else v
        return 4 * x + 2 * y + c, [(flip(x, fx), flip(y, fy), flip(c, fc)) for fx in (0, 1) for fy in (0, 1) for fc in (0, 1)][1:]

    def local(in_refs, out_refs, send, me):
        return pltpu.make_async_copy(in_refs[0], out_refs[0].at[me], send.at[7])

    def start(in_refs, out_refs, send, recv):
        me, others = peers()
        local(in_refs, out_refs, send, me).start()
        for j, peer in enumerate(others):
            pltpu.make_async_remote_copy(src_ref=in_refs[0], dst_ref=out_refs[0].at[me], send_sem=send.at[j],
                                         recv_sem=recv.at[j], device_id=peer, device_id_type=MESH).start()

    def finish(in_refs, out_refs, send, recv):
        me, others = peers()
        for j, (px, py, pc) in enumerate(others):
            pltpu.make_async_remote_copy(src_ref=in_refs[0], dst_ref=out_refs[0].at[4 * px + 2 * py + pc], send_sem=send.at[j],
                                         recv_sem=recv.at[j], device_id=(px, py, pc), device_id_type=MESH).wait()
        local(in_refs, out_refs, send, me).wait()

    return _Exchange([v], [jax.ShapeDtypeStruct((N_DEV, r, cols), v.dtype)], 8, start, finish)


def _sum_slots(a, name):
    n, r, cols = a.shape
    tr = _pick(r, 512) if r % 8 == 0 else r
    for cand in (512, 256, 128, 64, 32, 16, 8):
        if r % cand == 0:
            tr = cand
            break

    def body(a_ref, o_ref):
        acc = a_ref[0]
        for k in range(1, n):
            acc = acc + a_ref[k]
        o_ref[...] = acc

    return pl.pallas_call(
        body, name=name, grid=(r // tr,), in_specs=[pl.BlockSpec((n, tr, cols), lambda i: (0, i, 0))],
        out_specs=pl.BlockSpec((tr, cols), lambda i: (i, 0)), out_shape=jax.ShapeDtypeStruct((r, cols), F32),
        compiler_params=_params("parallel"))(a)


def _join(name, stacked):
    ax = SHARDED[name][1]
    return jnp.concatenate([stacked[k] for k in range(N_CHIPS)], axis=ax)


def _split(name, full):
    ax = SHARDED[name][1]
    return jnp.stack(jnp.split(full, N_CHIPS, axis=ax))


def _heads_pad(a, real, axis):
    shp = a.shape
    a = a.reshape(shp[:axis] + (MLA_HEADS, real) + shp[axis + 1:])
    pad = [(0, 0)] * a.ndim
    pad[axis + 1] = (0, LANES - real)
    a = jnp.pad(a, pad)
    return a.reshape(shp[:axis] + (HP,) + shp[axis + 1:])


def _heads_unpad(a, real, axis):
    shp = a.shape
    a = a.reshape(shp[:axis] + (MLA_HEADS, LANES) + shp[axis + 1:])
    a = lax.slice_in_dim(a, 0, real, axis=axis + 1)
    return a.reshape(shp[:axis] + (MLA_HEADS * real,) + shp[axis + 1:])


def _lane_place(a, start):
    n = a.shape[-1]
    pad = [(0, 0)] * (a.ndim - 1) + [(start, LANES - start - n)]
    return jnp.pad(a, pad)


_O_UV, _O_CQ, _O_CKV, _O_KR, _O_Z, _O_XBC, _O_DT, _O_G = 0, 1024, 1408, 1664, 1696, 2208, 3232, 3240


def _w_in_pad(w):
    sl = lambda a, b: w[:, a:b]
    xs = _heads_pad(sl(_O_XBC, _O_XBC + SSD_INNER), SSD_HEAD_DIM, 1)
    bc = sl(_O_XBC + SSD_INNER, _O_DT)
    main = jnp.concatenate([sl(_O_UV, _O_CQ), _heads_pad(sl(_O_Z, _O_XBC), SSD_HEAD_DIM, 1), xs, sl(_O_G, IN_COLS)], axis=1)
    tail = jnp.concatenate([bc, sl(_O_CKV, _O_KR), sl(_O_CQ, _O_CKV), _lane_place(sl(_O_KR, _O_Z), MLA_NOPE),
                            _lane_place(sl(_O_DT, _O_G), 0), jnp.zeros((w.shape[0], PW_TAIL - T_DT - LANES), w.dtype)], axis=1)
    return main, tail


def _w_in_unpad(gm, gt):
    m = lambda a, n: gm[:, a:a + n]
    t = lambda a, n: gt[:, a:a + n]
    parts = [m(C_UV, 1024), t(T_CQ, MLA_Q_RANK), t(T_CKV, MLA_KV_RANK), t(T_KR + MLA_NOPE, MLA_ROPE),
             _heads_unpad(m(C_Z, HP), SSD_HEAD_DIM, 1), _heads_unpad(m(C_XS, HP), SSD_HEAD_DIM, 1), t(T_BC, BCW),
             t(T_DT, SSD_HEADS), m(C_G, 3 * D_MODEL)]
    return jnp.concatenate(parts, axis=1)


def _xbc_pad(a):
    return jnp.concatenate([_heads_pad(a[..., :SSD_INNER], SSD_HEAD_DIM, a.ndim - 1), a[..., SSD_INNER:]], axis=-1)


def _xbc_unpad(a):
    return jnp.concatenate([_heads_unpad(a[..., :HP], SSD_HEAD_DIM, a.ndim - 1), a[..., HP:]], axis=-1)


def _rope_tables(positions):
    inv_freq = 1.0 / (ROPE_THETA ** (jnp.arange(0, MLA_ROPE, 2, dtype=F32) / MLA_ROPE))
    ang = positions.astype(F32)[:, None] * inv_freq
    cos, sin = jnp.cos(ang), jnp.sin(ang)
    s = positions.shape[0]
    half = MLA_ROPE // 2
    z = lambda n: jnp.zeros((s, n), F32)
    ct = jnp.concatenate([jnp.ones((s, MLA_NOPE), F32), cos, cos, z(LANES - MLA_QK)], axis=1)
    s1 = jnp.concatenate([z(MLA_NOPE), -sin, z(half), z(LANES - MLA_QK)], axis=1)
    s2 = jnp.concatenate([z(MLA_NOPE), z(half), sin, z(LANES - MLA_QK)], axis=1)
    return ct, s1, s2


def _layer_weights(full, small, l, part):
    w = {}
    row = lambda n: small[n][l][None, :]
    stacked = lambda g: g.reshape((N_CHIPS * g.shape[1], g.shape[2]))
    if part in ('ffn1', 'ffn2'):
        w[part + '_w_in'] = full[part + '_w_in']
        w[part + '_w_out'] = stacked(full[part + '_w_out'])
        w[part + '_norm'] = row(part + '_norm')
        return w
    w['w_out'] = stacked(full['w_out'])
    fl = {n: _join(n, full[n]) for n in ('w_in', 'mla_w_uq', 'mla_w_ukv', 'w_branch', 'ssd_conv_w')}
    w['w_in_main'], w['w_in_tail'] = _w_in_pad(fl['w_in'])
    w['wuq'] = _heads_pad(fl['mla_w_uq'], MLA_QK, 1)
    ukv = fl['mla_w_ukv'].reshape(MLA_KV_RANK, MLA_HEADS, MLA_NOPE + MLA_V)
    zero = jnp.zeros((MLA_KV_RANK, MLA_HEADS, LANES - MLA_NOPE), ukv.dtype)
    wk = jnp.concatenate([ukv[:, :, :MLA_NOPE], zero], axis=2).reshape(MLA_KV_RANK, HP)
    wv = jnp.concatenate([ukv[:, :, MLA_NOPE:], zero], axis=2).reshape(MLA_KV_RANK, HP)
    w['wkv'] = jnp.concatenate([wk, wv], axis=1)
    wb = fl['w_branch']
    w['wb0'] = wb[0]
    w['wb1'] = _heads_pad(wb[1], MLA_V, 0)
    w['wb2'] = _heads_pad(wb[2], SSD_HEAD_DIM, 0)
    w['conv_w'] = _xbc_pad(fl['ssd_conv_w'].astype(F32))
    for n in ('mix_norm', 'gm_v_norm', 'mla_q_norm', 'mla_kv_norm'):
        w[n] = row(n)
    w['gm_w_s'] = small['gm_w_s'][l]
    w['gm_b_full'] = jnp.broadcast_to(small['gm_b_s'][l][:, :, None], (GM_GROUPS, CHUNK, LANES))
    w['gq'] = _lane_place(row('mla_q_gain'), 0)
    w['gk'] = _lane_place(row('mla_k_gain'), 0)
    w['conv_b'] = _xbc_pad(row('ssd_conv_b'))
    w['dt_bias'] = _lane_place(row('ssd_dt_bias'), 0)
    w['a_log'] = _lane_place(row('ssd_a_log'), 0)
    w['d_vec'] = jnp.repeat(small['ssd_d'][l], LANES)[None, :]
    w['ssd_norm'] = _heads_pad(row('ssd_norm'), SSD_HEAD_DIM, 1)
    return w


_MIXER_SMALL = ['mla_w_uq', 'mla_w_ukv', 'ssd_conv_w', 'w_branch', 'w_out']
_MIXER_SMALL_G = [n for n in _MIXER_SMALL if n != 'ssd_conv_w']
GATHER_HOSTS = {'attn': ['ffn1_w_in', 'ffn2_w_in'], 'scan': ['ffn1_w_out', 'ffn2_w_out'], 'merge': _MIXER_SMALL, 'ffn2_in': ['w_in']}
GATHER_HOSTS_LATER = {'ffn1_in': ['ffn1_w_out'], 'proj': ['w_in'], 'attn': ['ffn1_w_in', 'ffn2_w_in'], 'scan': ['ffn2_w_out'],
                      'merge': _MIXER_SMALL}
FIRST_NOW = ['ffn1_w_in', 'ffn1_w_out']
FIRST_HOSTS = {'ffn1_in': ['w_in'], 'ffn1_out': _MIXER_SMALL, 'proj': ['ffn2_w_in', 'ffn2_w_out']}
PAIR_HOSTS = {'ffn2_dwout': ['ffn1_w_in', 'ffn2_w_in'], 'ffn2_dwin': ['ffn1_w_out', 'w_in', 'ffn2_w_out'] + _MIXER_SMALL_G}
REDUCE_HOSTS = {'dattn_q': ['ffn1_w_out', 'w_in', 'ffn2_w_out'], 'dattn_kv': ['ffn1_w_in', 'ffn2_w_in'], 'dmla_pre': _MIXER_SMALL_G}
LAST_EARLY = ['w_in', 'ffn2_w_in', 'ffn2_w_out'] + _MIXER_SMALL_G
LAST_HOSTS = {'ffn1_dact': ['w_in'], 'ffn1_dwin': ['ffn2_w_in'], 'ffn1_dx': ['ffn2_w_out'] + _MIXER_SMALL_G}
LAST_LATE = ['ffn1_w_in', 'ffn1_w_out']


def _ffn_fwd(x, norm, w4, w_out, tag, sides=None):
    sides = sides or {}
    carried = {}
    (h, gate, up, act), carried[f"{tag}_in"] = _ffn_in(x, norm, w4, f"{tag}_in", sides.get(f"{tag}_in"))
    y, carried[f"{tag}_out"] = _ffn_out(act, w_out, x, f"{tag}_out", sides.get(f"{tag}_out"))
    return y, (x, h, gate, up, act), carried


def _ffn_bwd(dy, saved, norm, w4, w_out, tag, sides=None, after_dwout=None):
    sides = dict(sides or {})
    carried = {}
    x, h, gate, up, act = saved
    dw_out, carried[f"{tag}_dwout"] = _ffn_dwout(act, dy, f"{tag}_dwout", sides.get(f"{tag}_dwout"))
    if after_dwout is not None:
        sides.update(after_dwout(carried[f"{tag}_dwout"]))
    da, carried[f"{tag}_dact"] = _ffn_dact(dy, w_out, gate, up, f"{tag}_dact", sides.get(f"{tag}_dact"))
    dw_in, carried[f"{tag}_dwin"] = _ffn_dwin(h, da, f"{tag}_dwin", sides.get(f"{tag}_dwin"))
    (dx, dnorm), carried[f"{tag}_dx"] = _ffn_dx(da, w4, x, norm, dy, f"{tag}_dx", sides.get(f"{tag}_dx"))
    return dx, dnorm, dw_in, dw_out.reshape((N_CHIPS, 2 * FC // N_CHIPS, D_MODEL)), carried


def _mixer_fwd(x, w, tabs, tag, sides=None):
    sides = sides or {}
    carried = {}
    h = _rmsnorm_fwd(x, w['mix_norm'], f"{tag}_norm")
    if sides.get('proj') is None:
        pm = _matmul(h, w['w_in_main'], out_dtype=BF16, name=f"{tag}_proj_main")
    else:
        pm, carried['proj'] = _matmul(h, w['w_in_main'], out_dtype=BF16, name=f"{tag}_proj_main", side=sides['proj'])
    pt = _matmul(h, w['w_in_tail'], name=f"{tag}_proj_tail")
    ya = _gmlp_fwd(pm, w['gm_v_norm'], w['gm_w_s'], w['gm_b_full'], f"{tag}_gmlp")
    q, k, v = _mla_pre_fwd(pt, tabs, w['mla_q_norm'], w['mla_kv_norm'], w['wuq'], w['wkv'], w['gq'], w['gk'], f"{tag}_mla_pre")
    (o, lse), carried['attn'] = _attn_fwd(q, k, v, f"{tag}_attn", sides.get('attn'))
    xs = _conv_fwd(pm, C_XS, HP, w['conv_w'][:, :HP], w['conv_b'][:, :HP], f"{tag}_conv_x")
    bc = _conv_fwd(pt, T_BC, BCW, w['conv_w'][:, HP:], w['conv_b'][:, HP:], f"{tag}_conv_bc")
    dtb, dab = _dt_fwd(pt, w['dt_bias'], w['a_log'], f"{tag}_dt")
    (ys, s_in), carried['scan'] = _scan_fwd(xs, bc, dtb, dab, f"{tag}_scan", sides.get('scan'))
    yc = _ssd_post_fwd(ys, xs, pm, w['d_vec'], w['ssd_norm'], f"{tag}_ssd_post")
    (mg, y), carried['merge'] = _merge_fwd(pm, ya, o, yc, w['wb0'], w['wb1'], w['wb2'], w['w_out'], x, f"{tag}_merge",
                                           sides.get('merge'))
    return y, (x, h, pm, pt, ya, q, k, v, o, lse, xs, bc, dtb, dab, ys, s_in, yc, mg), carried


def _pair_sums(pending, got):
    return {n: _pair_add(pending[n], got[n], f"pair_add_{n}") for n in got}


def _chip_sums(sums, arrived, layer, stacked):
    for n in arrived:
        stacked[n] = _chip_add(sums[n][1], arrived[n], f"chip_add_{n}", layer, stacked.get(n))


def _reduce_to_chip(pending, layer, stacked):
    names = list(pending)
    got = _run_exchange(_pair_exchange([pending[n] for n in names]), "pair_exchange")
    sums = _pair_sums(pending, dict(zip(names, got)))
    arrived = _run_exchange(_chip_exchange([sums[n][0] for n in names]), "chip_exchange")
    _chip_sums(sums, dict(zip(names, arrived)), layer, stacked)


def _mixer_bwd(dy, saved, w, tabs, tag, sides=None):
    sides = sides or {}
    carried = {}
    x, h, pm, pt, ya, q, k, v, o, lse, xs, bc, dtb, dab, ys, s_in, yc, mg = saved
    g = {}
    g['w_out'] = _matmul(mg, dy, ta=True, name=f"{tag}_dwout").reshape((N_CHIPS, D_MODEL // N_CHIPS, D_MODEL))
    d0, d1, d2, dgates, dya, do, dyc = _merge_bwd(pm, ya, o, yc, w['wb0'], w['wb1'], w['wb2'], w['w_out'], dy, f"{tag}_dmerge")
    dwb0 = _matmul(ya, d0, ta=True, name=f"{tag}_dwb0")
    dwb1 = _matmul(o, d1, ta=True, name=f"{tag}_dwb1")
    dwb2 = _matmul(yc, d2, ta=True, name=f"{tag}_dwb2")
    g['w_branch'] = _split('w_branch', jnp.stack([dwb0, _heads_unpad(dwb1, MLA_V, 0), _heads_unpad(dwb2, SSD_HEAD_DIM, 0)]))
    duv, g['gm_v_norm'], g['gm_w_s'], db = _gmlp_bwd(pm, w['gm_v_norm'], w['gm_w_s'], w['gm_b_full'], dya, f"{tag}_dgmlp")
    g['gm_b_s'] = db.T
    (dq, delta), carried['dattn_q'] = _attn_bwd_dq(q, k, v, o, lse, do, f"{tag}_dattn_q", sides.get('dattn_q'))
    (dk, dv), carried['dattn_kv'] = _attn_bwd_dkv(q, k, v, lse, delta, do, f"{tag}_dattn_kv", sides.get('dattn_kv'))
    (dcq, dckv, dkr, dwuq, dwkv, g['mla_q_norm'], g['mla_kv_norm'], dgq, dgk), carried['dmla_pre'] = _mla_pre_bwd(
        pt, tabs, w['mla_q_norm'], w['mla_kv_norm'], w['wuq'], w['wkv'], w['gq'], w['gk'], dq, dk, dv, f"{tag}_dmla_pre",
        sides.get('dmla_pre'))
    g['mla_w_uq'] = _split('mla_w_uq', _heads_unpad(dwuq, MLA_QK, 1))
    dwk = dwkv[:, :HP].reshape(MLA_KV_RANK, MLA_HEADS, LANES)[:, :, :MLA_NOPE]
    dwv = dwkv[:, HP:].reshape(MLA_KV_RANK, MLA_HEADS, LANES)[:, :, :MLA_V]
    g['mla_w_ukv'] = _split('mla_w_ukv', jnp.concatenate([dwk, dwv], axis=2).reshape(MLA_KV_RANK, MLA_HEADS * (MLA_NOPE + MLA_V)))
    g['mla_q_gain'], g['mla_k_gain'] = dgq[:, :MLA_QK], dgk[:, :MLA_QK]
    dys, dz, dssd_norm, dd = _ssd_post_bwd(ys, xs, pm, w['d_vec'], w['ssd_norm'], dyc, f"{tag}_dssd_post")
    g['ssd_norm'] = _heads_unpad(dssd_norm, SSD_HEAD_DIM, 1)
    g['ssd_d'] = jnp.sum(dd.reshape(SSD_HEADS, LANES), axis=1)[None, :]
    dxs, dbm, dcm, dda, ddtx = _scan_bwd(xs, bc, dtb, dab, s_in, dys, w['d_vec'], f"{tag}_dscan")
    dxs16, dcw_x, dcb_x = _conv_bwd(pm, C_XS, HP, w['conv_w'][:, :HP], w['conv_b'][:, :HP], dxs, f"{tag}_dconv_x")
    dbc16, dcw_bc, dcb_bc = _conv_bwd(pt, T_BC, BCW, w['conv_w'][:, HP:], w['conv_b'][:, HP:],
                                      jnp.concatenate([dbm, dcm], axis=1), f"{tag}_dconv_bc")
    g['ssd_conv_w'] = _xbc_unpad(jnp.concatenate([dcw_x, dcw_bc], axis=1))
    g['ssd_conv_b'] = _xbc_unpad(jnp.concatenate([dcb_x, dcb_bc], axis=1))
    ddt, dbias, dalog = _dt_bwd(pt, w['dt_bias'], w['a_log'], dda, ddtx, f"{tag}_ddt")
    g['ssd_dt_bias'], g['ssd_a_log'] = dbias[:, :SSD_HEADS], dalog[:, :SSD_HEADS]
    s = x.shape[0]
    dpm = jnp.concatenate([duv, dz, dxs16, dgates], axis=1)
    dpt = jnp.concatenate([dbc16, dckv, dcq, dkr, ddt, jnp.zeros((s, PW_TAIL - T_DT - LANES), BF16)], axis=1)
    g['w_in'] = _split('w_in', _w_in_unpad(_matmul(h, dpm, ta=True, name=f"{tag}_dwin_main"),
                                           _matmul(h, dpt, ta=True, name=f"{tag}_dwin_tail")))
    dh = _matmul(dpt, w['w_in_tail'], tb=True, name=f"{tag}_dh_tail")
    dh = _matmul(dpm, w['w_in_main'], tb=True, res=dh, name=f"{tag}_dh_main")
    dx, g['mix_norm'] = _rmsnorm_bwd(x, w['mix_norm'], dh, dy, f"{tag}_dnorm")
    return dx, g, carried


_CONV_ROWS = 32


def _rows_cols(a, lead):
    return a.reshape(a.shape[:lead] + (int(np.prod(a.shape[lead:-1])), a.shape[-1]))


def _shard_views(wts):
    views = []
    for n in SHARDED_ORDER:
        a = _rows_cols(wts[n].astype(BF16), 1)
        if n == 'ssd_conv_w':
            a = jnp.pad(a, ((0, 0), (0, _CONV_ROWS - a.shape[1]), (0, 0)))
        views.append(a)
    return views


def _gathered(names, arrays):
    out = {}
    for n, a in zip(names, arrays):
        shp = _shard_shape(n)
        if n == 'ssd_conv_w':
            a = a[:, :shp[0]]
        out[n] = a.reshape((N_CHIPS,) + shp)
    return out


def _local_step(x, positions, target, weights, small, distributed=True):
    tabs = _rope_tables(positions)
    views = dict(zip(SHARDED_ORDER, weights)) if distributed else None
    plan = [{} for _ in range(DEPTH)]
    if distributed:
        for l in range(DEPTH - 1):
            plan[l].update({host: (names, l + 1) for host, names in (GATHER_HOSTS if l == 0 else GATHER_HOSTS_LATER).items()})
        plan[0].update({host: (names, 0) for host, names in FIRST_HOSTS.items()})
        have = [dict() for _ in range(DEPTH)]
        have[0].update(_gathered(FIRST_NOW, _run_exchange(_gather_exchange([views[n] for n in FIRST_NOW], 0), "gather_first")))
    else:
        have = weights

    def absorb(l, carried):
        for host, arrays in carried.items():
            if host in plan[l]:
                names, layer = plan[l][host]
                have[layer].update(_gathered(names, arrays))

    ws, saved = [], []
    for l in range(DEPTH):
        sides = {host: _gather_exchange([views[n] for n in names], layer) for host, (names, layer) in plan[l].items()}
        w = _layer_weights(have[l], small, l, 'ffn1')
        x, s1, carried = _ffn_fwd(x, w['ffn1_norm'], w['ffn1_w_in'], w['ffn1_w_out'], "ffn1", sides)
        absorb(l, carried)
        w.update(_layer_weights(have[l], small, l, 'mixer'))
        x, s2, carried = _mixer_fwd(x, w, tabs, "mix", sides)
        absorb(l, carried)
        w.update(_layer_weights(have[l], small, l, 'ffn2'))
        x, s3, carried = _ffn_fwd(x, w['ffn2_norm'], w['ffn2_w_in'], w['ffn2_w_out'], "ffn2", sides)
        absorb(l, carried)
        ws.append(w)
        saved.append((s1, s2, s3))
    dy, sq = _loss_head(x, target, "loss_head")
    loss = 0.5 * jnp.sum(sq) / D_MODEL
    grads, reduced, pending = [None] * DEPTH, {}, None

    def chip_sides(sums, hosts):
        return {host: _chip_exchange([sums[n][0] for n in names]) for host, names in hosts.items()}

    def arrivals(carried, hosts):
        return {n: a for host, names in hosts.items() for n, a in zip(names, carried[host])}

    for l in reversed(range(DEPTH)):
        w = ws[l]
        s1, s2, s3 = saved[l]
        sides = {host: _pair_exchange([pending[n] for n in names]) for host, names in PAIR_HOSTS.items()} if pending else {}
        dy, dn2, dwi2, dwo2, carried = _ffn_bwd(dy, s3, w['ffn2_norm'], w['ffn2_w_in'], w['ffn2_w_out'], "ffn2", sides)
        sides = {}
        if pending:
            sums = _pair_sums(pending, arrivals(carried, PAIR_HOSTS))
            sides = chip_sides(sums, REDUCE_HOSTS)
        dy, g, carried = _mixer_bwd(dy, s2, w, tabs, "mix", sides)
        if pending:
            _chip_sums(sums, arrivals(carried, REDUCE_HOSTS), l + 1, reduced)
        g.update(ffn2_norm=dn2, ffn2_w_in=dwi2, ffn2_w_out=dwo2)
        last = distributed and l == 0
        if last:
            early = {n: _rows_cols(g[n], 1) for n in LAST_EARLY}
            after = {}

            def after_dwout(got):
                after['sums'] = _pair_sums(early, dict(zip(LAST_EARLY, got)))
                return chip_sides(after['sums'], LAST_HOSTS)

            dy, dn1, dwi1, dwo1, carried = _ffn_bwd(dy, s1, w['ffn1_norm'], w['ffn1_w_in'], w['ffn1_w_out'], "ffn1",
                                                    {'ffn1_dwout': _pair_exchange([early[n] for n in LAST_EARLY])}, after_dwout)
            _chip_sums(after['sums'], arrivals(carried, LAST_HOSTS), 0, reduced)
        else:
            dy, dn1, dwi1, dwo1, _ = _ffn_bwd(dy, s1, w['ffn1_norm'], w['ffn1_w_in'], w['ffn1_w_out'], "ffn1")
        g.update(ffn1_norm=dn1, ffn1_w_in=dwi1, ffn1_w_out=dwo1)
        grads[l] = g
        if distributed:
            pending = {n: _rows_cols(g[n], 1) for n in REDUCED}
    if distributed:
        _reduce_to_chip({n: pending[n] for n in LAST_LATE}, 0, reduced)
    return loss, dy, grads, reduced


SMALL_PACK = SMALL_ORDER + ['ssd_conv_w']


def _pack_small(per_layer_rows, tail=None):
    parts = [per_layer_rows[l][n].reshape(-1).astype(F32) for l in range(DEPTH) for n in SMALL_PACK]
    if tail is not None:
        parts.append(tail.reshape(1))
    flat = jnp.concatenate(parts)
    rows = -(-flat.shape[0] // LANES)
    rows = -(-rows // 8) * 8
    return jnp.pad(flat, (0, rows * LANES - flat.shape[0])).reshape(rows, LANES)


def _unpack_small(buf, shapes):
    flat = buf.reshape(-1)
    off = 0
    out = {n: [] for n in SMALL_PACK}
    for l in range(DEPTH):
        for n in SMALL_PACK:
            size = int(np.prod(shapes[n]))
            out[n].append(flat[off:off + size].reshape(shapes[n]))
            off += size
    return {n: jnp.stack(v) for n, v in out.items()}


def kernel(x, positions, ffn1_norm, ffn1_w_in, ffn1_w_out, mix_norm, w_in, gm_v_norm, gm_w_s, gm_b_s, mla_q_norm, mla_kv_norm, mla_w_uq, mla_w_ukv, mla_q_gain, mla_k_gain, ssd_conv_w, ssd_conv_b, ssd_dt_bias, ssd_a_log, ssd_d, ssd_norm, w_branch, w_out, ffn2_norm, ffn2_w_in, ffn2_w_out, loss_target, m_ffn1_norm, m_ffn1_w_in, m_ffn1_w_out, m_mix_norm, m_w_in, m_gm_v_norm, m_gm_w_s, m_gm_b_s, m_mla_q_norm, m_mla_kv_norm, m_mla_w_uq, m_mla_w_ukv, m_mla_q_gain, m_mla_k_gain, m_ssd_conv_w, m_ssd_conv_b, m_ssd_dt_bias, m_ssd_a_log, m_ssd_d, m_ssd_norm, m_w_branch, m_w_out, m_ffn2_norm, m_ffn2_w_in, m_ffn2_w_out, v_ffn1_norm, v_ffn1_w_in, v_ffn1_w_out, v_mix_norm, v_w_in, v_gm_v_norm, v_gm_w_s, v_gm_b_s, v_mla_q_norm, v_mla_kv_norm, v_mla_w_uq, v_mla_w_ukv, v_mla_q_gain, v_mla_k_gain, v_ssd_conv_w, v_ssd_conv_b, v_ssd_dt_bias, v_ssd_a_log, v_ssd_d, v_ssd_norm, v_w_branch, v_w_out, v_ffn2_norm, v_ffn2_w_in, v_ffn2_w_out):
    wts = dict(zip(WEIGHTS, (ffn1_norm, ffn1_w_in, ffn1_w_out, mix_norm, w_in, gm_v_norm, gm_w_s, gm_b_s, mla_q_norm, mla_kv_norm,
                             mla_w_uq, mla_w_ukv, mla_q_gain, mla_k_gain, ssd_conv_w, ssd_conv_b, ssd_dt_bias, ssd_a_log, ssd_d,
                             ssd_norm, w_branch, w_out, ffn2_norm, ffn2_w_in, ffn2_w_out)))
    mom = dict(zip(WEIGHTS, (m_ffn1_norm, m_ffn1_w_in, m_ffn1_w_out, m_mix_norm, m_w_in, m_gm_v_norm, m_gm_w_s, m_gm_b_s, m_mla_q_norm,
                             m_mla_kv_norm, m_mla_w_uq, m_mla_w_ukv, m_mla_q_gain, m_mla_k_gain, m_ssd_conv_w, m_ssd_conv_b,
                             m_ssd_dt_bias, m_ssd_a_log, m_ssd_d, m_ssd_norm, m_w_branch, m_w_out, m_ffn2_norm, m_ffn2_w_in,
                             m_ffn2_w_out)))
    var = dict(zip(WEIGHTS, (v_ffn1_norm, v_ffn1_w_in, v_ffn1_w_out, v_mix_norm, v_w_in, v_gm_v_norm, v_gm_w_s, v_gm_b_s, v_mla_q_norm,
                             v_mla_kv_norm, v_mla_w_uq, v_mla_w_ukv, v_mla_q_gain, v_mla_k_gain, v_ssd_conv_w, v_ssd_conv_b,
                             v_ssd_dt_bias, v_ssd_a_log, v_ssd_d, v_ssd_norm, v_w_branch, v_w_out, v_ffn2_norm, v_ffn2_w_in,
                             v_ffn2_w_out)))
    cx, cy, _ = _me()
    mychip = _chip_index(cx, cy)

    small = {n: wts[n] for n in SMALL_ORDER}
    loss_part, dx, grads, reduced = _local_step(x[0], positions[0], loss_target[0], _shard_views(wts), small)
    rows_cols = _rows_cols
    halves = [reduced[n] for n in REDUCED]
    theirs = _run_exchange(_pair_share(halves), "pair_share")
    grad, delta, new_m, new_v = {}, {}, {}, {}
    everyone = _all_exchange(_pack_small(grads, tail=loss_part))
    for n, a, b in zip(REDUCED, halves, theirs):
        shp = wts[n].shape
        outs, carried = _adamw_sharded(rows_cols(wts[n], 1), rows_cols(mom[n], 1), rows_cols(var[n], 1), a, b, f"adamw_{n}",
                                       everyone if n == REDUCED[0] else None)
        if n == REDUCED[0]:
            partials = carried[0]
        grad[n], delta[n], new_m[n], new_v[n] = [o.reshape(shp) for o in outs]
    shapes = {n: wts[n].shape[1:] for n in SMALL_ORDER}
    shapes['ssd_conv_w'] = SHARDED['ssd_conv_w'][0]
    summed = _sum_slots(partials, "small_sum")
    small_g = _unpack_small(summed, shapes)
    loss = summed.reshape(-1)[DEPTH * sum(int(np.prod(shapes[n])) for n in SMALL_PACK)]
    conv_full = small_g.pop('ssd_conv_w')
    shard_cols = _shard_shape('ssd_conv_w')[1]
    small_g['ssd_conv_w'] = lax.dynamic_slice_in_dim(conv_full, mychip * shard_cols, shard_cols, axis=2)
    shapes['ssd_conv_w'] = _shard_shape('ssd_conv_w')

    per_layer = lambda t: [{n: t[n][l] for n in SMALL_PACK} for l in range(DEPTH)]
    d, nm, nv = _adamw(_pack_small(per_layer(wts)), _pack_small(per_layer(small_g)), _pack_small(per_layer(mom)),
                       _pack_small(per_layer(var)), "adamw_small")
    sd, snm, snv = _unpack_small(d, shapes), _unpack_small(nm, shapes), _unpack_small(nv, shapes)
    for n in SMALL_PACK:
        grad[n], delta[n], new_m[n], new_v[n] = small_g[n], sd[n], snm[n], snv[n]
    return (loss, dx[None], *[grad[n] for n in WEIGHTS], *[delta[n] for n in WEIGHTS], *[new_m[n] for n in WEIGHTS],
            *[new_v[n] for n in WEIGHTS])
```

```python
import functools
import math

import numpy as np
import jax
import jax.numpy as jnp
from jax import lax
from jax.experimental import pallas as pl
from jax.experimental.pallas import tpu as pltpu

F32, BF16 = jnp.float32, jnp.bfloat16
MESH = pl.DeviceIdType.MESH

D_MODEL, DEPTH, D_FF, EPS = 1024, 4, 2816, 1e-6
GM_WIDTH, GM_GROUPS, CHUNK = 512, 4, 128
MLA_HEADS, MLA_Q_RANK, MLA_KV_RANK, MLA_NOPE, MLA_ROPE, MLA_V = 8, 384, 256, 64, 32, 64
MLA_QK = MLA_NOPE + MLA_ROPE
ROPE_THETA = 10000.0
SSD_HEADS, SSD_HEAD_DIM, SSD_GROUPS, SSD_STATE, SSD_CONV = 8, 64, 2, 128, 4
SSD_INNER = SSD_HEADS * SSD_HEAD_DIM
IN_COLS = 6312
LANES = 128
ADAM_LR, ADAM_B1, ADAM_B2, ADAM_EPS, ADAM_WD, ADAM_STEP = 0.001, 0.9, 0.999, 1e-08, 0.01, 10

C_UV, C_Z, C_XS, C_G, PW_MAIN = 0, 1024, 2048, 3072, 6144
T_BC, T_CKV, T_CQ, T_KR, T_DT, PW_TAIL = 0, 512, 768, 1152, 1280, 1536
HP = MLA_HEADS * LANES
FC = 2 * D_FF // 4

WEIGHTS = ['ffn1_norm', 'ffn1_w_in', 'ffn1_w_out', 'mix_norm', 'w_in', 'gm_v_norm', 'gm_w_s', 'gm_b_s', 'mla_q_norm',
           'mla_kv_norm', 'mla_w_uq', 'mla_w_ukv', 'mla_q_gain', 'mla_k_gain', 'ssd_conv_w', 'ssd_conv_b', 'ssd_dt_bias',
           'ssd_a_log', 'ssd_d', 'ssd_norm', 'w_branch', 'w_out', 'ffn2_norm', 'ffn2_w_in', 'ffn2_w_out']
SHARDED = {'ffn1_w_in': ((1024, 5632), 1), 'ffn1_w_out': ((2816, 1024), 0), 'w_in': ((1024, 6312), 1),
           'mla_w_uq': ((384, 768), 1), 'mla_w_ukv': ((256, 1024), 1), 'ssd_conv_w': ((4, 1024), 1),
           'w_branch': ((3, 512, 1024), 2), 'w_out': ((1024, 1024), 0), 'ffn2_w_in': ((1024, 5632), 1),
           'ffn2_w_out': ((2816, 1024), 0)}
SHARDED_ORDER = [n for n in WEIGHTS if n in SHARDED]
SMALL_ORDER = [n for n in WEIGHTS if n not in SHARDED]
REDUCED = [n for n in SHARDED_ORDER if n != 'ssd_conv_w']
N_CHIPS = 4
HALF_L = DEPTH // 2


def _shard_shape(name):
    shape, ax = SHARDED[name]
    return tuple(d // N_CHIPS if i == ax else d for i, d in enumerate(shape))


def _pick(dim, target):
    if dim <= target:
        return dim
    t = (target // LANES) * LANES
    while t >= LANES:
        if dim % t == 0:
            return t
        t -= LANES
    return dim


def _sigmoid(x):
    return 1.0 / (1.0 + jnp.exp(-x))


def _params(*sem):
    return pltpu.CompilerParams(dimension_semantics=sem, vmem_limit_bytes=56 * 1024 * 1024)


def _matmul(a, b, *, ta=False, tb=False, out_dtype=F32, scale=1.0, res=None, name, side=None):
    if ta:
        k_dim, m_dim = a.shape
    else:
        m_dim, k_dim = a.shape
    if tb:
        n_dim, k2 = b.shape
    else:
        k2, n_dim = b.shape
    assert k_dim == k2, (a.shape, b.shape, ta, tb)
    tm, tn, tk = _pick(m_dim, 1024), _pick(n_dim, 1024), _pick(k_dim, 1024)
    nk = k_dim // tk
    dn = (((0 if ta else 1,), (1 if tb else 0,)), ((), ()))

    def body(*refs):
        if res is not None:
            a_ref, b_ref, r_ref, o_ref, acc = refs
        else:
            a_ref, b_ref, o_ref, acc = refs
        k = pl.program_id(2)

        @pl.when(k == 0)
        def _():
            acc[...] = jnp.zeros_like(acc)

        acc[...] += lax.dot_general(a_ref[...].astype(BF16), b_ref[...].astype(BF16), dn, preferred_element_type=F32)

        @pl.when(k == nk - 1)
        def _():
            r = acc[...]
            if scale != 1.0:
                r = r * scale
            if res is not None:
                r = r + r_ref[...]
            o_ref[...] = r.astype(out_dtype)

    a_spec = pl.BlockSpec((tk, tm), lambda j, i, k: (k, i)) if ta else pl.BlockSpec((tm, tk), lambda j, i, k: (i, k))
    b_spec = pl.BlockSpec((tn, tk), lambda j, i, k: (j, k)) if tb else pl.BlockSpec((tk, tn), lambda j, i, k: (k, j))
    in_specs = [a_spec, b_spec]
    args = [a, b]
    if res is not None:
        in_specs.append(pl.BlockSpec((tm, tn), lambda j, i, k: (i, j)))
        args.append(res)
    (out,), carried = _call(
        body, name=name, grid=(n_dim // tn, m_dim // tm, nk), in_specs=in_specs,
        out_specs=[pl.BlockSpec((tm, tn), lambda j, i, k: (i, j))],
        out_shape=[jax.ShapeDtypeStruct((m_dim, n_dim), out_dtype)],
        scratch_shapes=[pltpu.VMEM((tm, tn), F32)], args=args, semantics=("parallel", "parallel", "arbitrary"), side=side)
    return out if side is None else (out, carried)


def _rmsnorm_fwd(x, gain, name):
    s, d = x.shape
    tm = _pick(s, 512)

    def body(x_ref, g_ref, o_ref):
        xv = x_ref[...]
        r = lax.rsqrt(jnp.mean(xv * xv, axis=-1, keepdims=True) + EPS)
        o_ref[...] = (xv * r * g_ref[...]).astype(BF16)

    return pl.pallas_call(
        body, name=name, grid=(s // tm,),
        in_specs=[pl.BlockSpec((tm, d), lambda i: (i, 0)), pl.BlockSpec((1, d), lambda i: (0, 0))],
        out_specs=pl.BlockSpec((tm, d), lambda i: (i, 0)),
        out_shape=jax.ShapeDtypeStruct((s, d), BF16), compiler_params=_params("parallel"))(x, gain)


def _rmsnorm_bwd(x, gain, dh, dres, name):
    s, d = x.shape
    tm = _pick(s, 512)

    def body(x_ref, g_ref, dh_ref, dr_ref, dx_ref, dg_ref):
        @pl.when(pl.program_id(0) == 0)
        def _():
            dg_ref[...] = jnp.zeros_like(dg_ref)

        xv, dhv = x_ref[...], dh_ref[...]
        r = lax.rsqrt(jnp.mean(xv * xv, axis=-1, keepdims=True) + EPS)
        u = dhv * g_ref[...]
        dx_ref[...] = dr_ref[...] + r * u - xv * (r * r * r) * jnp.mean(xv * u, axis=-1, keepdims=True)
        dg_ref[...] += jnp.sum(dhv * xv * r, axis=0, keepdims=True)

    row = pl.BlockSpec((tm, d), lambda i: (i, 0))
    vec = pl.BlockSpec((1, d), lambda i: (0, 0))
    return pl.pallas_call(
        body, name=name, grid=(s // tm,), in_specs=[row, vec, row, row], out_specs=[row, vec],
        out_shape=[jax.ShapeDtypeStruct((s, d), F32), jax.ShapeDtypeStruct((1, d), F32)],
        compiler_params=_params("arbitrary"))(x, gain, dh, dres)


_NT = (((1,), (1,)), ((), ()))
_TN = (((0,), (0,)), ((), ()))


def _resident(shape):
    return pl.BlockSpec(shape, lambda *_: tuple(0 for _ in shape), pipeline_mode=pl.Buffered(1))


def _ffn_in(x, gain, w4, name, side=None):
    s, d = x.shape
    tm = _pick(s, 512)

    def body(x_ref, g_ref, w_ref, h_ref, gate_ref, up_ref, act_ref):
        xv = x_ref[...]
        r = lax.rsqrt(jnp.mean(xv * xv, axis=-1, keepdims=True) + EPS)
        h = (xv * r * g_ref[...]).astype(BF16)
        h_ref[...] = h
        for j in range(2):
            g16 = jnp.dot(h, w_ref[j], preferred_element_type=F32).astype(BF16)
            u16 = jnp.dot(h, w_ref[j + 2], preferred_element_type=F32).astype(BF16)
            gate_ref[j] = g16
            up_ref[j] = u16
            gf, uf = g16.astype(F32), u16.astype(F32)
            act_ref[j] = (gf * _sigmoid(gf) * uf).astype(BF16)

    half = pl.BlockSpec((2, tm, FC), lambda i: (0, i, 0))
    return _call(
        body, name=name, grid=(s // tm,),
        in_specs=[pl.BlockSpec((tm, d), lambda i: (i, 0)), pl.BlockSpec((1, d), lambda i: (0, 0)), _resident((4, d, FC))],
        out_specs=[pl.BlockSpec((tm, d), lambda i: (i, 0)), half, half, half],
        out_shape=[jax.ShapeDtypeStruct((s, d), BF16)] + [jax.ShapeDtypeStruct((2, s, FC), BF16)] * 3,
        scratch_shapes=[], args=(x, gain, w4), semantics=("parallel",), side=side)


def _ffn_out(act, w_out, x, name, side=None):
    s, d = x.shape
    tm = _pick(s, 512)

    def body(a_ref, w_ref, x_ref, o_ref):
        acc = jnp.dot(a_ref[0], w_ref[0:FC, :], preferred_element_type=F32)
        acc = acc + jnp.dot(a_ref[1], w_ref[FC:2 * FC, :], preferred_element_type=F32)
        o_ref[...] = x_ref[...] + 0.5 * acc

    row = pl.BlockSpec((tm, d), lambda i: (i, 0))
    (out,), carried = _call(
        body, name=name, grid=(s // tm,),
        in_specs=[pl.BlockSpec((2, tm, FC), lambda i: (0, i, 0)), _resident((2 * FC, d)), row], out_specs=[row],
        out_shape=[jax.ShapeDtypeStruct((s, d), F32)], scratch_shapes=[], args=(act, w_out, x), semantics=("parallel",),
        side=side)
    return out, carried


def _ffn_dact(dy, w_out, gate, up, name, side=None):
    s, d = dy.shape
    tm = _pick(s, 512)

    def body(dy_ref, w_ref, g_ref, u_ref, o_ref):
        dy16 = dy_ref[...].astype(BF16)
        for j in range(2):
            dact = 0.5 * lax.dot_general(dy16, w_ref[j * FC:(j + 1) * FC, :], _NT, preferred_element_type=F32)
            g, u = g_ref[j].astype(F32), u_ref[j].astype(F32)
            sg = _sigmoid(g)
            o_ref[j] = (dact * u * (sg * (1.0 + g * (1.0 - sg)))).astype(BF16)
            o_ref[j + 2] = (dact * g * sg).astype(BF16)

    half = pl.BlockSpec((2, tm, FC), lambda i: (0, i, 0))
    (out,), carried = _call(
        body, name=name, grid=(s // tm,),
        in_specs=[pl.BlockSpec((tm, d), lambda i: (i, 0)), _resident((2 * FC, d)), half, half],
        out_specs=[pl.BlockSpec((4, tm, FC), lambda i: (0, i, 0))],
        out_shape=[jax.ShapeDtypeStruct((4, s, FC), BF16)], scratch_shapes=[], args=(dy, w_out, gate, up),
        semantics=("parallel",), side=side)
    return out, carried


def _ffn_dwout(act, dy, name, side=None):
    s, d = dy.shape
    tk = _pick(s, 1024)
    nk = s // tk

    def body(a_ref, dy_ref, o_ref):
        k = pl.program_id(1)

        @pl.when(k == 0)
        def _():
            o_ref[...] = jnp.zeros_like(o_ref)

        o_ref[...] += lax.dot_general(a_ref[...], dy_ref[...].astype(BF16), _TN, preferred_element_type=F32)

        @pl.when(k == nk - 1)
        def _():
            o_ref[...] = 0.5 * o_ref[...]

    (out,), carried = _call(
        body, name=name, grid=(2, nk),
        in_specs=[pl.BlockSpec((None, tk, FC), lambda j, k: (j, k, 0)), pl.BlockSpec((tk, d), lambda j, k: (k, 0))],
        out_specs=[pl.BlockSpec((FC, d), lambda j, k: (j, 0))], out_shape=[jax.ShapeDtypeStruct((2 * FC, d), F32)],
        scratch_shapes=[], args=(act, dy), semantics=("parallel", "arbitrary"), side=side)
    return out, carried


def _ffn_dwin(h, da, name, side=None):
    s, d = h.shape
    tk = _pick(s, 1024)

    def body(h_ref, da_ref, o_ref):
        @pl.when(pl.program_id(1) == 0)
        def _():
            o_ref[...] = jnp.zeros_like(o_ref)

        o_ref[...] += lax.dot_general(h_ref[...], da_ref[...], _TN, preferred_element_type=F32)

    (out,), carried = _call(
        body, name=name, grid=(4, s // tk),
        in_specs=[pl.BlockSpec((tk, d), lambda j, k: (k, 0)), pl.BlockSpec((None, tk, FC), lambda j, k: (j, k, 0))],
        out_specs=[pl.BlockSpec((None, d, FC), lambda j, k: (j, 0, 0))], out_shape=[jax.ShapeDtypeStruct((4, d, FC), F32)],
        scratch_shapes=[], args=(h, da), semantics=("parallel", "arbitrary"), side=side)
    return out, carried


def _ffn_dx(da, w4, x, gain, dy, name, side=None):
    s, d = x.shape
    tm = _pick(s, 512)

    def body(da_ref, w_ref, x_ref, g_ref, dy_ref, dx_ref, dg_ref):
        @pl.when(pl.program_id(0) == 0)
        def _():
            dg_ref[...] = jnp.zeros_like(dg_ref)

        dh = jnp.zeros((tm, d), F32)
        for j in range(4):
            dh = dh + lax.dot_general(da_ref[j], w_ref[j], _NT, preferred_element_type=F32)
        xv = x_ref[...]
        r = lax.rsqrt(jnp.mean(xv * xv, axis=-1, keepdims=True) + EPS)
        u = dh * g_ref[...]
        dx_ref[...] = dy_ref[...] + r * u - xv * (r * r * r) * jnp.mean(xv * u, axis=-1, keepdims=True)
        dg_ref[...] += jnp.sum(dh * xv * r, axis=0, keepdims=True)

    row = pl.BlockSpec((tm, d), lambda i: (i, 0))
    vec = pl.BlockSpec((1, d), lambda i: (0, 0))
    return _call(
        body, name=name, grid=(s // tm,),
        in_specs=[pl.BlockSpec((4, tm, FC), lambda i: (0, i, 0)), _resident((4, d, FC)), row, vec, row],
        out_specs=[row, vec], out_shape=[jax.ShapeDtypeStruct((s, d), F32), jax.ShapeDtypeStruct((1, d), F32)],
        scratch_shapes=[], args=(da, w4, x, gain, dy), semantics=("arbitrary",), side=side)


_INV_SQRT2 = 0.7071067811865476
_INV_SQRT2PI = 0.3989422804014327


def _gelu(x):
    return 0.5 * x * (1.0 + lax.erf(x * _INV_SQRT2))


def _gelu_grad(x):
    return 0.5 * (1.0 + lax.erf(x * _INV_SQRT2)) + x * jnp.exp(-0.5 * x * x) * _INV_SQRT2PI


def _tril_mask():
    r = lax.broadcasted_iota(jnp.int32, (CHUNK, CHUNK), 0)
    c = lax.broadcasted_iota(jnp.int32, (CHUNK, CHUNK), 1)
    return r >= c


def _gmlp_fwd(p, v_gain, w_s, b_full, name):
    s = p.shape[0]
    tm = _pick(s, 512)
    nch = tm // CHUNK

    def body(uv_ref, g_ref, w_ref, b_ref, o_ref):
        gel = _gelu(uv_ref[...].astype(F32))
        u, v = gel[:, :GM_WIDTH], gel[:, GM_WIDTH:]
        r = lax.rsqrt(jnp.mean(v * v, axis=-1, keepdims=True) + EPS)
        vn = (v * r * g_ref[...]).astype(BF16)
        mask = _tril_mask()
        for g in range(GM_GROUPS):
            wm = jnp.where(mask, w_ref[g], 0.0).astype(BF16)
            for c in range(nch):
                rs, cs = slice(c * CHUNK, (c + 1) * CHUNK), slice(g * LANES, (g + 1) * LANES)
                sp = jnp.dot(wm, vn[rs, cs], preferred_element_type=F32) + b_ref[g]
                o_ref[rs, cs] = (u[rs, cs] * sp).astype(BF16)

    full3 = pl.BlockSpec((GM_GROUPS, CHUNK, CHUNK), lambda i: (0, 0, 0))
    return pl.pallas_call(
        body, name=name, grid=(s // tm,),
        in_specs=[pl.BlockSpec((tm, 2 * GM_WIDTH), lambda i: (i, C_UV // (2 * GM_WIDTH))),
                  pl.BlockSpec((1, GM_WIDTH), lambda i: (0, 0)), full3, full3],
        out_specs=pl.BlockSpec((tm, GM_WIDTH), lambda i: (i, 0)),
        out_shape=jax.ShapeDtypeStruct((s, GM_WIDTH), BF16), compiler_params=_params("parallel"))(p, v_gain, w_s, b_full)


def _gmlp_bwd(p, v_gain, w_s, b_full, dy, name):
    s = p.shape[0]
    tm = _pick(s, 512)
    nch = tm // CHUNK
    nsteps = s // tm

    def body(uv_ref, g_ref, w_ref, b_ref, dy_ref, duv_ref, dg_ref, dw_ref, db_ref, dvn_s, dbacc):
        step = pl.program_id(0)

        @pl.when(step == 0)
        def _():
            dg_ref[...] = jnp.zeros_like(dg_ref)
            dw_ref[...] = jnp.zeros_like(dw_ref)
            dbacc[...] = jnp.zeros_like(dbacc)

        uv = uv_ref[...].astype(F32)
        gel = _gelu(uv)
        u, v = gel[:, :GM_WIDTH], gel[:, GM_WIDTH:]
        r = lax.rsqrt(jnp.mean(v * v, axis=-1, keepdims=True) + EPS)
        gain = g_ref[...]
        vn32 = v * r * gain
        vn = vn32.astype(BF16)
        dy = dy_ref[...].astype(F32)
        mask = _tril_mask()
        for g in range(GM_GROUPS):
            wm = jnp.where(mask, w_ref[g], 0.0).astype(BF16)
            dwg = jnp.zeros((CHUNK, CHUNK), F32)
            dbg = jnp.zeros((CHUNK, LANES), F32)
            for c in range(nch):
                rs, cs = slice(c * CHUNK, (c + 1) * CHUNK), slice(g * LANES, (g + 1) * LANES)
                sp = jnp.dot(wm, vn[rs, cs], preferred_element_type=F32) + b_ref[g]
                dyc = dy[rs, cs]
                dsp = dyc * u[rs, cs]
                dsp16 = dsp.astype(BF16)
                duv_ref[rs, cs] = (dyc * sp * _gelu_grad(uv[rs, cs])).astype(BF16)
                dvn_s[rs, cs] = lax.dot_general(wm, dsp16, (((0,), (0,)), ((), ())), preferred_element_type=F32)
                dwg = dwg + lax.dot_general(dsp16, vn[rs, cs], (((1,), (1,)), ((), ())), preferred_element_type=F32)
                dbg = dbg + dsp
            dw_ref[g] += jnp.where(mask, dwg, 0.0)
            dbacc[:, g * LANES:(g + 1) * LANES] += dbg
        dvn = dvn_s[...]
        uu = dvn * gain
        dv = r * uu - v * (r * r * r) * jnp.mean(v * uu, axis=-1, keepdims=True)
        duv_ref[:, GM_WIDTH:] = (dv * _gelu_grad(uv[:, GM_WIDTH:])).astype(BF16)
        dg_ref[...] += jnp.sum(dvn * v * r, axis=0, keepdims=True)

        @pl.when(step == nsteps - 1)
        def _():
            for g in range(GM_GROUPS):
                db_ref[:, g:g + 1] = jnp.sum(dbacc[:, g * LANES:(g + 1) * LANES], axis=1, keepdims=True)

    full3 = pl.BlockSpec((GM_GROUPS, CHUNK, CHUNK), lambda i: (0, 0, 0))
    return pl.pallas_call(
        body, name=name, grid=(nsteps,),
        in_specs=[pl.BlockSpec((tm, 2 * GM_WIDTH), lambda i: (i, C_UV // (2 * GM_WIDTH))),
                  pl.BlockSpec((1, GM_WIDTH), lambda i: (0, 0)), full3, full3,
                  pl.BlockSpec((tm, GM_WIDTH), lambda i: (i, 0))],
        out_specs=[pl.BlockSpec((tm, 2 * GM_WIDTH), lambda i: (i, 0)), pl.BlockSpec((1, GM_WIDTH), lambda i: (0, 0)),
                   full3, pl.BlockSpec((CHUNK, GM_GROUPS), lambda i: (0, 0))],
        out_shape=[jax.ShapeDtypeStruct((s, 2 * GM_WIDTH), BF16), jax.ShapeDtypeStruct((1, GM_WIDTH), F32),
                   jax.ShapeDtypeStruct((GM_GROUPS, CHUNK, CHUNK), F32), jax.ShapeDtypeStruct((CHUNK, GM_GROUPS), F32)],
        scratch_shapes=[pltpu.VMEM((tm, GM_WIDTH), F32), pltpu.VMEM((CHUNK, GM_WIDTH), F32)],
        compiler_params=_params("arbitrary"))(p, v_gain, w_s, b_full, dy)


def _rope(x, ct, s1, s2):
    return x * ct + pltpu.roll(x, LANES - MLA_ROPE // 2, 1) * s1 + pltpu.roll(x, MLA_ROPE // 2, 1) * s2


def _rope_bwd(d, ct, s1, s2):
    return d * ct + pltpu.roll(d * s1, MLA_ROPE // 2, 1) + pltpu.roll(d * s2, LANES - MLA_ROPE // 2, 1)


def _head_norm(x, gain):
    r = lax.rsqrt(jnp.sum(x * x, axis=-1, keepdims=True) * (1.0 / MLA_QK) + EPS)
    return x * r * gain, r


def _head_norm_bwd(x, r, gain, d):
    u = d * gain
    return r * u - x * (r * r * r) * (jnp.sum(x * u, axis=-1, keepdims=True) * (1.0 / MLA_QK))


def _mla_specs(tm):
    cq = pl.BlockSpec((tm, MLA_Q_RANK), lambda i: (i, T_CQ // MLA_Q_RANK))
    ckv = pl.BlockSpec((tm, MLA_KV_RANK), lambda i: (i, T_CKV // MLA_KV_RANK))
    kr = pl.BlockSpec((tm, LANES), lambda i: (i, T_KR // LANES))
    tab = pl.BlockSpec((tm, LANES), lambda i: (i, 0))
    return cq, ckv, kr, tab


def _const(shape):
    return pl.BlockSpec(shape, lambda i: tuple(0 for _ in shape))


def _mla_pre_fwd(p, tabs, qn_g, kvn_g, wuq, wkv, gq, gk, name):
    s = p.shape[0]
    tm = _pick(s, 256)
    ct, s1, s2 = tabs

    def body(cq_ref, ckv_ref, kr_ref, ct_ref, s1_ref, s2_ref, qg_ref, kvg_ref, wuq_ref, wkv_ref, gq_ref, gk_ref,
             q_ref, k_ref, v_ref):
        cq, ckv, kr = cq_ref[...], ckv_ref[...], kr_ref[...]
        ctv, s1v, s2v = ct_ref[...], s1_ref[...], s2_ref[...]
        rq = lax.rsqrt(jnp.mean(cq * cq, axis=-1, keepdims=True) + EPS)
        q = jnp.dot((cq * rq * qg_ref[...]).astype(BF16), wuq_ref[...], preferred_element_type=F32)
        rk = lax.rsqrt(jnp.mean(ckv * ckv, axis=-1, keepdims=True) + EPS)
        kv = jnp.dot((ckv * rk * kvg_ref[...]).astype(BF16), wkv_ref[...], preferred_element_type=F32)
        v_ref[...] = kv[:, HP:].astype(BF16)
        for h in range(MLA_HEADS):
            hs = slice(h * LANES, (h + 1) * LANES)
            qh, _ = _head_norm(q[:, hs], gq_ref[...])
            q_ref[:, hs] = (_rope(qh, ctv, s1v, s2v) * _Q_SCALE).astype(BF16)
            kh, _ = _head_norm(kv[:, hs] + kr, gk_ref[...])
            k_ref[:, hs] = _rope(kh, ctv, s1v, s2v).astype(BF16)

    cq_s, ckv_s, kr_s, tab_s = _mla_specs(tm)
    out = pl.BlockSpec((tm, HP), lambda i: (i, 0))
    return pl.pallas_call(
        body, name=name, grid=(s // tm,),
        in_specs=[cq_s, ckv_s, kr_s, tab_s, tab_s, tab_s, _const((1, MLA_Q_RANK)), _const((1, MLA_KV_RANK)),
                  _const((MLA_Q_RANK, HP)), _const((MLA_KV_RANK, 2 * HP)), _const((1, LANES)), _const((1, LANES))],
        out_specs=[out, out, out], out_shape=[jax.ShapeDtypeStruct((s, HP), BF16)] * 3,
        compiler_params=_params("parallel"))(p, p, p, ct, s1, s2, qn_g, kvn_g, wuq, wkv, gq, gk)


def _mla_pre_bwd(p, tabs, qn_g, kvn_g, wuq, wkv, gq, gk, dq, dk, dv, name, side=None):
    s = p.shape[0]
    tm = _pick(s, 256)
    ct, s1, s2 = tabs

    def body(cq_ref, ckv_ref, kr_ref, ct_ref, s1_ref, s2_ref, qg_ref, kvg_ref, wuq_ref, wkv_ref, gq_ref, gk_ref,
             dq_ref, dk_ref, dv_ref, dcq_ref, dckv_ref, dkr_ref, dwuq_ref, dwkv_ref, dqg_ref, dkvg_ref, dgq_ref, dgk_ref,
             dqp, dkvp):
        @pl.when(pl.program_id(0) == 0)
        def _():
            for ref in (dwuq_ref, dwkv_ref, dqg_ref, dkvg_ref, dgq_ref, dgk_ref):
                ref[...] = jnp.zeros_like(ref)

        cq, ckv, kr = cq_ref[...], ckv_ref[...], kr_ref[...]
        ctv, s1v, s2v = ct_ref[...], s1_ref[...], s2_ref[...]
        rq = lax.rsqrt(jnp.mean(cq * cq, axis=-1, keepdims=True) + EPS)
        qn = (cq * rq * qg_ref[...]).astype(BF16)
        q = jnp.dot(qn, wuq_ref[...], preferred_element_type=F32)
        rk = lax.rsqrt(jnp.mean(ckv * ckv, axis=-1, keepdims=True) + EPS)
        kvn = (ckv * rk * kvg_ref[...]).astype(BF16)
        kv = jnp.dot(kvn, wkv_ref[...], preferred_element_type=F32)
        gqv, gkv = gq_ref[...], gk_ref[...]
        dgq = jnp.zeros((1, LANES), F32)
        dgk = jnp.zeros((1, LANES), F32)
        dkr = jnp.zeros((tm, LANES), F32)
        for h in range(MLA_HEADS):
            hs = slice(h * LANES, (h + 1) * LANES)
            xq = q[:, hs]
            _, r = _head_norm(xq, gqv)
            d = _rope_bwd(dq_ref[:, hs].astype(F32), ctv, s1v, s2v)
            dgq = dgq + jnp.sum(d * xq * r, axis=0, keepdims=True)
            dqp[:, hs] = _head_norm_bwd(xq, r, gqv, d)
            xk = kv[:, hs] + kr
            _, r = _head_norm(xk, gkv)
            d = _rope_bwd(dk_ref[:, hs].astype(F32), ctv, s1v, s2v)
            dgk = dgk + jnp.sum(d * xk * r, axis=0, keepdims=True)
            dxk = _head_norm_bwd(xk, r, gkv, d)
            dkvp[:, hs] = dxk
            dkr = dkr + dxk
        dkvp[:, HP:] = dv_ref[...].astype(F32)
        dgq_ref[...] += dgq
        dgk_ref[...] += dgk
        dkr_ref[...] = dkr.astype(BF16)
        tn = (((0,), (0,)), ((), ()))
        nt = (((1,), (1,)), ((), ()))
        dq16 = dqp[...].astype(BF16)
        dwuq_ref[...] += lax.dot_general(qn, dq16, tn, preferred_element_type=F32)
        dqn = lax.dot_general(dq16, wuq_ref[...], nt, preferred_element_type=F32)
        dqg_ref[...] += jnp.sum(dqn * cq * rq, axis=0, keepdims=True)
        u = dqn * qg_ref[...]
        dcq_ref[...] = (rq * u - cq * (rq * rq * rq) * jnp.mean(cq * u, axis=-1, keepdims=True)).astype(BF16)
        dkv16 = dkvp[...].astype(BF16)
        dwkv_ref[...] += lax.dot_general(kvn, dkv16, tn, preferred_element_type=F32)
        dkvn = lax.dot_general(dkv16, wkv_ref[...], nt, preferred_element_type=F32)
        dkvg_ref[...] += jnp.sum(dkvn * ckv * rk, axis=0, keepdims=True)
        u = dkvn * kvg_ref[...]
        dckv_ref[...] = (rk * u - ckv * (rk * rk * rk) * jnp.mean(ckv * u, axis=-1, keepdims=True)).astype(BF16)

    cq_s, ckv_s, kr_s, tab_s = _mla_specs(tm)
    hd = pl.BlockSpec((tm, HP), lambda i: (i, 0))
    return _call(
        body, name=name, grid=(s // tm,),
        in_specs=[cq_s, ckv_s, kr_s, tab_s, tab_s, tab_s, _const((1, MLA_Q_RANK)), _const((1, MLA_KV_RANK)),
                  _const((MLA_Q_RANK, HP)), _const((MLA_KV_RANK, 2 * HP)), _const((1, LANES)), _const((1, LANES)),
                  hd, hd, hd],
        out_specs=[pl.BlockSpec((tm, MLA_Q_RANK), lambda i: (i, 0)), pl.BlockSpec((tm, MLA_KV_RANK), lambda i: (i, 0)),
                   pl.BlockSpec((tm, LANES), lambda i: (i, 0)), _const((MLA_Q_RANK, HP)), _const((MLA_KV_RANK, 2 * HP)),
                   _const((1, MLA_Q_RANK)), _const((1, MLA_KV_RANK)), _const((1, LANES)), _const((1, LANES))],
        out_shape=[jax.ShapeDtypeStruct((s, MLA_Q_RANK), BF16), jax.ShapeDtypeStruct((s, MLA_KV_RANK), BF16),
                   jax.ShapeDtypeStruct((s, LANES), BF16), jax.ShapeDtypeStruct((MLA_Q_RANK, HP), F32),
                   jax.ShapeDtypeStruct((MLA_KV_RANK, 2 * HP), F32), jax.ShapeDtypeStruct((1, MLA_Q_RANK), F32),
                   jax.ShapeDtypeStruct((1, MLA_KV_RANK), F32), jax.ShapeDtypeStruct((1, LANES), F32),
                   jax.ShapeDtypeStruct((1, LANES), F32)],
        scratch_shapes=[pltpu.VMEM((tm, HP), F32), pltpu.VMEM((tm, 2 * HP), F32)],
        args=(p, p, p, ct, s1, s2, qn_g, kvn_g, wuq, wkv, gq, gk, dq, dk, dv), semantics=("arbitrary",), side=side)


_ATT_SCALE = MLA_QK ** -0.5
_LOG2E = 1.4426950408889634
_Q_SCALE = _ATT_SCALE * _LOG2E
ATT_BLOCK = 1024
_NEG = -1e30
_NT = (((1,), (1,)), ((), ()))
_TN = (((0,), (0,)), ((), ()))


def _tri_rows(step, n):
    i = step * 0
    for m in range(1, n):
        i = i + (step >= m * (m + 1) // 2).astype(jnp.int32)
    return i, step - i * (i + 1) // 2


def _tri_cols(step, n):
    j = step * 0
    for m in range(1, n):
        j = j + (step >= m * n - m * (m - 1) // 2).astype(jnp.int32)
    return j, j + step - (j * n - j * (j - 1) // 2)


def _diag_mask(t):
    return lax.broadcasted_iota(jnp.int32, (t, t), 0) <= lax.broadcasted_iota(jnp.int32, (t, t), 1)


def _attn_fwd(q, k, v, name, side=None):
    s = q.shape[0]
    t = _pick(s, ATT_BLOCK)
    n = s // t

    def body(q_ref, k_ref, v_ref, o_ref, lse_ref, m_s, l_s, acc):
        i, j = _tri_rows(pl.program_id(1), n)

        @pl.when(j == 0)
        def _():
            m_s[...] = jnp.full_like(m_s, _NEG)
            l_s[...] = jnp.zeros_like(l_s)
            acc[...] = jnp.zeros_like(acc)

        def step(diagonal):
            sc = lax.dot_general(k_ref[...], q_ref[...], _NT, preferred_element_type=F32)
            if diagonal:
                sc = jnp.where(_diag_mask(t), sc, _NEG)
            m_new = jnp.maximum(m_s[...], jnp.max(sc, axis=0, keepdims=True))
            alpha = jnp.exp2(m_s[...] - m_new)
            pr = jnp.exp2(sc - m_new)
            l_s[...] = alpha * l_s[...] + jnp.sum(pr, axis=0, keepdims=True)
            acc[...] = alpha * acc[...] + lax.dot_general(v_ref[...], pr.astype(BF16), _TN, preferred_element_type=F32)
            m_s[...] = m_new

        @pl.when(j < i)
        def _():
            step(False)

        @pl.when(j == i)
        def _():
            step(True)
            o_ref[...] = (acc[...] / l_s[...]).T.astype(BF16)
            lse_ref[...] = m_s[...] + jnp.log2(l_s[...])

    qs = pl.BlockSpec((t, LANES), lambda h, p: (_tri_rows(p, n)[0], h))
    ks = pl.BlockSpec((t, LANES), lambda h, p: (_tri_rows(p, n)[1], h))
    return _call(
        body, name=name, grid=(MLA_HEADS, n * (n + 1) // 2), in_specs=[qs, ks, ks],
        out_specs=[qs, pl.BlockSpec((None, 1, t), lambda h, p: (h, 0, _tri_rows(p, n)[0]))],
        out_shape=[jax.ShapeDtypeStruct((s, HP), BF16), jax.ShapeDtypeStruct((MLA_HEADS, 1, s), F32)],
        scratch_shapes=[pltpu.VMEM((1, t), F32), pltpu.VMEM((1, t), F32), pltpu.VMEM((LANES, t), F32)],
        args=(q, k, v), semantics=("parallel", "arbitrary"), side=side)


def _attn_bwd_dq(q, k, v, o, lse, do, name, side=None):
    s = q.shape[0]
    t = _pick(s, ATT_BLOCK)
    n = s // t

    def body(q_ref, k_ref, v_ref, o_ref, lse_ref, do_ref, dq_ref, dl_ref, acc, dl_s):
        i, j = _tri_rows(pl.program_id(1), n)

        @pl.when(j == 0)
        def _():
            acc[...] = jnp.zeros_like(acc)
            dl_s[...] = jnp.sum((do_ref[...].astype(F32) * o_ref[...].astype(F32)).T, axis=0, keepdims=True)

        def step(diagonal):
            sc = lax.dot_general(k_ref[...], q_ref[...], _NT, preferred_element_type=F32)
            if diagonal:
                sc = jnp.where(_diag_mask(t), sc, _NEG)
            pr = jnp.exp2(sc - lse_ref[...])
            dp = lax.dot_general(v_ref[...], do_ref[...].astype(BF16), _NT, preferred_element_type=F32)
            ds = (pr * (dp - dl_s[...])).astype(BF16)
            acc[...] += lax.dot_general(k_ref[...], ds, _TN, preferred_element_type=F32)

        @pl.when(j < i)
        def _():
            step(False)

        @pl.when(j == i)
        def _():
            step(True)
            dq_ref[...] = (acc[...] * _ATT_SCALE).T.astype(BF16)
            dl_ref[...] = dl_s[...]

    qs = pl.BlockSpec((t, LANES), lambda h, p: (_tri_rows(p, n)[0], h))
    ks = pl.BlockSpec((t, LANES), lambda h, p: (_tri_rows(p, n)[1], h))
    ls = pl.BlockSpec((None, 1, t), lambda h, p: (h, 0, _tri_rows(p, n)[0]))
    return _call(
        body, name=name, grid=(MLA_HEADS, n * (n + 1) // 2), in_specs=[qs, ks, ks, qs, ls, qs], out_specs=[qs, ls],
        out_shape=[jax.ShapeDtypeStruct((s, HP), BF16), jax.ShapeDtypeStruct((MLA_HEADS, 1, s), F32)],
        scratch_shapes=[pltpu.VMEM((LANES, t), F32), pltpu.VMEM((1, t), F32)],
        args=(q, k, v, o, lse, do), semantics=("parallel", "arbitrary"), side=side)


def _attn_bwd_dkv(q, k, v, lse, delta, do, name, side=None):
    s = q.shape[0]
    t = _pick(s, ATT_BLOCK)
    n = s // t

    def body(q_ref, k_ref, v_ref, lse_ref, dl_ref, do_ref, dk_ref, dv_ref, dk_acc, dv_acc):
        j, i = _tri_cols(pl.program_id(1), n)

        def step(diagonal):
            sc = lax.dot_general(k_ref[...], q_ref[...], _NT, preferred_element_type=F32)
            if diagonal:
                sc = jnp.where(_diag_mask(t), sc, _NEG)
            pr = jnp.exp2(sc - lse_ref[...])
            do16 = do_ref[...].astype(BF16)
            dv_acc[...] += jnp.dot(pr.astype(BF16), do16, preferred_element_type=F32)
            dp = lax.dot_general(v_ref[...], do16, _NT, preferred_element_type=F32)
            ds = (pr * (dp - dl_ref[...])).astype(BF16)
            dk_acc[...] += jnp.dot(ds, q_ref[...], preferred_element_type=F32)

        @pl.when(i == j)
        def _():
            dk_acc[...] = jnp.zeros_like(dk_acc)
            dv_acc[...] = jnp.zeros_like(dv_acc)
            step(True)

        @pl.when(i > j)
        def _():
            step(False)

        @pl.when(i == n - 1)
        def _():
            dk_ref[...] = (dk_acc[...] * (1.0 / _LOG2E)).astype(BF16)
            dv_ref[...] = dv_acc[...].astype(BF16)

    qs = pl.BlockSpec((t, LANES), lambda h, p: (_tri_cols(p, n)[1], h))
    ks = pl.BlockSpec((t, LANES), lambda h, p: (_tri_cols(p, n)[0], h))
    ls = pl.BlockSpec((None, 1, t), lambda h, p: (h, 0, _tri_cols(p, n)[1]))
    return _call(
        body, name=name, grid=(MLA_HEADS, n * (n + 1) // 2), in_specs=[qs, ks, ks, ls, ls, qs], out_specs=[ks, ks],
        out_shape=[jax.ShapeDtypeStruct((s, HP), BF16)] * 2,
        scratch_shapes=[pltpu.VMEM((t, LANES), F32), pltpu.VMEM((t, LANES), F32)],
        args=(q, k, v, lse, delta, do), semantics=("parallel", "arbitrary"), side=side)


XBC = HP + 2 * SSD_GROUPS * SSD_STATE
BCW = 2 * SSD_GROUPS * SSD_STATE


def _conv_fwd(p, col0, width, conv_w, conv_b, name):
    s = p.shape[0]
    c0, nblk = col0 // LANES, width // LANES

    def body(x_ref, w_ref, b_ref, o_ref, pad):
        pad[0:8, :] = jnp.zeros((8, LANES), F32)
        pad[8:s + 8, :] = x_ref[...].astype(F32)
        acc = jnp.broadcast_to(b_ref[...], (s, LANES))
        for t in range(SSD_CONV):
            acc = acc + pad[pl.ds(8 - (SSD_CONV - 1) + t, s), :] * w_ref[t:t + 1, :]
        o_ref[...] = acc * _sigmoid(acc)

    return pl.pallas_call(
        body, name=name, grid=(nblk,),
        in_specs=[pl.BlockSpec((s, LANES), lambda j: (0, c0 + j)), pl.BlockSpec((SSD_CONV, LANES), lambda j: (0, j)),
                  pl.BlockSpec((1, LANES), lambda j: (0, j))],
        out_specs=pl.BlockSpec((s, LANES), lambda j: (0, j)), out_shape=jax.ShapeDtypeStruct((s, width), F32),
        scratch_shapes=[pltpu.VMEM((s + 8, LANES), F32)], compiler_params=_params("parallel"))(p, conv_w, conv_b)


def _conv_bwd(p, col0, width, conv_w, conv_b, dact, name):
    s = p.shape[0]
    c0, nblk = col0 // LANES, width // LANES

    def body(x_ref, w_ref, b_ref, d_ref, dx_ref, dw_ref, db_ref, pad, padd):
        pad[0:8, :] = jnp.zeros((8, LANES), F32)
        pad[8:s + 8, :] = x_ref[...].astype(F32)
        acc = jnp.broadcast_to(b_ref[...], (s, LANES))
        for t in range(SSD_CONV):
            acc = acc + pad[pl.ds(8 - (SSD_CONV - 1) + t, s), :] * w_ref[t:t + 1, :]
        sg = _sigmoid(acc)
        dpre = d_ref[...] * (sg * (1.0 + acc * (1.0 - sg)))
        padd[0:s, :] = dpre
        padd[s:s + 8, :] = jnp.zeros((8, LANES), F32)
        dx = jnp.zeros((s, LANES), F32)
        for t in range(SSD_CONV):
            dx = dx + padd[pl.ds(SSD_CONV - 1 - t, s), :] * w_ref[t:t + 1, :]
            dw_ref[t:t + 1, :] = jnp.sum(dpre * pad[pl.ds(8 - (SSD_CONV - 1) + t, s), :], axis=0, keepdims=True)
        dx_ref[...] = dx.astype(BF16)
        db_ref[...] = jnp.sum(dpre, axis=0, keepdims=True)

    blk = pl.BlockSpec((s, LANES), lambda j: (0, j))
    return pl.pallas_call(
        body, name=name, grid=(nblk,),
        in_specs=[pl.BlockSpec((s, LANES), lambda j: (0, c0 + j)), pl.BlockSpec((SSD_CONV, LANES), lambda j: (0, j)),
                  pl.BlockSpec((1, LANES), lambda j: (0, j)), blk],
        out_specs=[blk, pl.BlockSpec((SSD_CONV, LANES), lambda j: (0, j)), pl.BlockSpec((1, LANES), lambda j: (0, j))],
        out_shape=[jax.ShapeDtypeStruct((s, width), BF16), jax.ShapeDtypeStruct((SSD_CONV, width), F32),
                   jax.ShapeDtypeStruct((1, width), F32)],
        scratch_shapes=[pltpu.VMEM((s + 8, LANES), F32), pltpu.VMEM((s + 8, LANES), F32)],
        compiler_params=_params("parallel"))(p, conv_w, conv_b, dact)


def _softplus(x):
    return jnp.maximum(x, 0.0) + jnp.log(1.0 + jnp.exp(-jnp.abs(x)))


def _dt_fwd(p, dt_bias, a_log, name):
    s = p.shape[0]
    tm = _pick(s, 512)

    def body(x_ref, b_ref, a_ref, dt_ref, da_ref):
        dtv = _softplus(x_ref[...] + b_ref[...])
        dav = dtv * (-jnp.exp(a_ref[...]))
        for h in range(SSD_HEADS):
            hs = slice(h * LANES, (h + 1) * LANES)
            dt_ref[:, hs] = jnp.broadcast_to(dtv[:, h:h + 1], (tm, LANES))
            da_ref[:, hs] = jnp.broadcast_to(dav[:, h:h + 1], (tm, LANES))

    out = pl.BlockSpec((tm, HP), lambda i: (i, 0))
    return pl.pallas_call(
        body, name=name, grid=(s // tm,),
        in_specs=[pl.BlockSpec((tm, LANES), lambda i: (i, T_DT // LANES)), _const((1, LANES)), _const((1, LANES))],
        out_specs=[out, out], out_shape=[jax.ShapeDtypeStruct((s, HP), F32)] * 2,
        compiler_params=_params("parallel"))(p, dt_bias, a_log)


def _dt_bwd(p, dt_bias, a_log, dda, ddtx, name):
    s = p.shape[0]
    tm = _pick(s, 512)

    def body(x_ref, b_ref, a_ref, dda_ref, ddtx_ref, dx_ref, db_ref, dal_ref):
        @pl.when(pl.program_id(0) == 0)
        def _():
            db_ref[...] = jnp.zeros_like(db_ref)
            dal_ref[...] = jnp.zeros_like(dal_ref)

        x = x_ref[...] + b_ref[...]
        dtv = _softplus(x)
        av = -jnp.exp(a_ref[...])
        lane = lax.broadcasted_iota(jnp.int32, (tm, LANES), 1)
        pa = jnp.zeros((tm, LANES), F32)
        px = jnp.zeros((tm, LANES), F32)
        for h in range(SSD_HEADS):
            pa = jnp.where(lane == h, dda_ref[:, h * LANES:(h + 1) * LANES], pa)
            px = jnp.where(lane == h, ddtx_ref[:, h * LANES:(h + 1) * LANES], px)
        draw = (pa * av + px) * _sigmoid(x)
        dx_ref[...] = draw.astype(BF16)
        db_ref[...] += jnp.sum(draw, axis=0, keepdims=True)
        dal_ref[...] += jnp.sum(pa * dtv, axis=0, keepdims=True) * av

    hd = pl.BlockSpec((tm, HP), lambda i: (i, 0))
    return pl.pallas_call(
        body, name=name, grid=(s // tm,),
        in_specs=[pl.BlockSpec((tm, LANES), lambda i: (i, T_DT // LANES)), _const((1, LANES)), _const((1, LANES)), hd, hd],
        out_specs=[pl.BlockSpec((tm, LANES), lambda i: (i, 0)), _const((1, LANES)), _const((1, LANES))],
        out_shape=[jax.ShapeDtypeStruct((s, LANES), BF16), jax.ShapeDtypeStruct((1, LANES), F32),
                   jax.ShapeDtypeStruct((1, LANES), F32)],
        compiler_params=_params("arbitrary"))(p, dt_bias, a_log, dda, ddtx)


def _cumsum_rows(x):
    row = lax.broadcasted_iota(jnp.int32, x.shape, 0)
    k = 1
    while k < x.shape[0]:
        x = x + jnp.where(row >= k, pltpu.roll(x, k, 0), 0.0)
        k *= 2
    return x


def _rev_cumsum_rows(x):
    n = x.shape[0]
    row = lax.broadcasted_iota(jnp.int32, x.shape, 0)
    k = 1
    while k < n:
        x = x + jnp.where(row < n - k, pltpu.roll(x, n - k, 0), 0.0)
        k *= 2
    return x


HPG = SSD_HEADS // SSD_GROUPS


def _chunk_decay(da):
    cs = _cumsum_rows(da)
    lm = jnp.exp(jnp.where(_tril_mask(), cs - cs.T, _NEG))
    return cs, lm, cs[CHUNK - 1:CHUNK, :]


def _scan_fwd(xs, bc, dtb, dab, name, side=None):
    s = xs.shape[0]
    nc = s // CHUNK

    def body(x_ref, b_ref, c_ref, dt_ref, da_ref, y_ref, sin_ref, state):
        @pl.when(pl.program_id(1) == 0)
        def _():
            state[...] = jnp.zeros_like(state)

        bv = b_ref[...]
        b16, c16 = bv.astype(BF16), c_ref[...].astype(BF16)
        g = lax.dot_general(c16, b16, _NT, preferred_element_type=F32)
        for hh in range(HPG):
            hs = slice(hh * LANES, (hh + 1) * LANES)
            st = state[hh]
            sin_ref[hh] = st
            cs, lm, cl = _chunk_decay(da_ref[:, hs])
            xd = (x_ref[:, hs] * dt_ref[:, hs]).astype(BF16)
            y = jnp.dot((g * lm).astype(BF16), xd, preferred_element_type=F32)
            y_ref[:, hs] = y + jnp.dot(c16, st.astype(BF16), preferred_element_type=F32) * jnp.exp(cs)
            bd = (bv * jnp.exp(cl - cs)).astype(BF16)
            state[hh] = jnp.exp(cl) * st + lax.dot_general(bd, xd, _TN, preferred_element_type=F32)

    gw = HPG * LANES
    hd = pl.BlockSpec((CHUNK, gw), lambda g, c: (c, g))
    return _call(
        body, name=name, grid=(SSD_GROUPS, nc),
        in_specs=[hd, pl.BlockSpec((CHUNK, LANES), lambda g, c: (c, g)),
                  pl.BlockSpec((CHUNK, LANES), lambda g, c: (c, SSD_GROUPS + g)), hd, hd],
        out_specs=[hd, pl.BlockSpec((HPG, None, SSD_STATE, LANES), lambda g, c: (g, c, 0, 0))],
        out_shape=[jax.ShapeDtypeStruct((s, HP), F32), jax.ShapeDtypeStruct((SSD_HEADS, nc, SSD_STATE, LANES), F32)],
        scratch_shapes=[pltpu.VMEM((HPG, SSD_STATE, LANES), F32)],
        args=(xs, bc, bc, dtb, dab), semantics=("parallel", "arbitrary"), side=side)


def _scan_bwd(xs, bc, dtb, dab, s_in, dy, d_vec, name):
    s = xs.shape[0]
    nc = s // CHUNK

    def body(x_ref, b_ref, c_ref, dt_ref, da_ref, sin_ref, dy_ref, dv_ref, dx_ref, db_ref, dc_ref, dda_ref, ddtx_ref, dstate):
        @pl.when(pl.program_id(1) == 0)
        def _():
            dstate[...] = jnp.zeros_like(dstate)

        bv = b_ref[...]
        b16, c16 = bv.astype(BF16), c_ref[...].astype(BF16)
        g = lax.dot_general(c16, b16, _NT, preferred_element_type=F32)
        row = lax.broadcasted_iota(jnp.int32, (CHUNK, 1), 0)
        dbm = jnp.zeros((CHUNK, SSD_STATE), F32)
        dcm = jnp.zeros((CHUNK, SSD_STATE), F32)
        for hh in range(HPG):
            hs = slice(hh * LANES, (hh + 1) * LANES)
            st, ds = sin_ref[hh], dstate[hh]
            st16, ds16 = st.astype(BF16), ds.astype(BF16)
            xv, dtv, dyv = x_ref[:, hs], dt_ref[:, hs], dy_ref[:, hs]
            cs, lm, cl = _chunk_decay(da_ref[:, hs])
            ecs, ecl = jnp.exp(cs), jnp.exp(cl)
            decay = jnp.exp(cl - cs)
            xd = (xv * dtv).astype(BF16)
            dy16 = dyv.astype(BF16)
            dye = (dyv * ecs).astype(BF16)
            yoff = jnp.dot(c16, st16, preferred_element_type=F32) * ecs
            dcs = jnp.sum(dyv * yoff, axis=-1, keepdims=True)
            dcm = dcm + lax.dot_general(dye, st16, _NT, preferred_element_type=F32)
            dstate[hh] = ecl * ds + lax.dot_general(c16, dye, _TN, preferred_element_type=F32)
            dcl = jnp.sum(jnp.sum(ds * st, axis=0, keepdims=True), axis=1, keepdims=True) * ecl[:, 0:1]
            bd32 = bv * decay
            qm = lax.dot_general(xd, ds16, _NT, preferred_element_type=F32)
            dbm = dbm + qm * decay
            w = jnp.sum(bd32 * qm, axis=-1, keepdims=True)
            dcs = dcs - w
            dcl = dcl + jnp.sum(w, axis=0, keepdims=True)
            dxd = jnp.dot(bd32.astype(BF16), ds16, preferred_element_type=F32)
            m16 = (g * lm).astype(BF16)
            dm = lax.dot_general(dy16, xd, _NT, preferred_element_type=F32)
            dxd = dxd + lax.dot_general(m16, dy16, _TN, preferred_element_type=F32)
            dg = dm * lm
            dg16 = dg.astype(BF16)
            tt = dg * g
            dcm = dcm + jnp.dot(dg16, b16, preferred_element_type=F32)
            dbm = dbm + lax.dot_general(dg16, c16, _TN, preferred_element_type=F32)
            dcs = dcs + jnp.sum(tt, axis=-1, keepdims=True) - jnp.sum(tt.T, axis=-1, keepdims=True)
            dcs = dcs + jnp.where(row == CHUNK - 1, dcl, 0.0)
            dda_ref[:, hs] = _rev_cumsum_rows(jnp.broadcast_to(dcs, (CHUNK, LANES)))
            ddtx_ref[:, hs] = jnp.broadcast_to(jnp.sum(dxd * xv, axis=-1, keepdims=True), (CHUNK, LANES))
            dx_ref[:, hs] = dxd * dtv + dyv * dv_ref[:, hs]
        db_ref[...] = dbm
        dc_ref[...] = dcm

    gw = HPG * LANES
    hd = pl.BlockSpec((CHUNK, gw), lambda g, c: (nc - 1 - c, g))
    gp = pl.BlockSpec((CHUNK, LANES), lambda g, c: (nc - 1 - c, g))
    return pl.pallas_call(
        body, name=name, grid=(SSD_GROUPS, nc),
        in_specs=[hd, gp, pl.BlockSpec((CHUNK, LANES), lambda g, c: (nc - 1 - c, SSD_GROUPS + g)), hd, hd,
                  pl.BlockSpec((HPG, None, SSD_STATE, LANES), lambda g, c: (g, nc - 1 - c, 0, 0)), hd,
                  pl.BlockSpec((1, gw), lambda g, c: (0, g))],
        out_specs=[hd, gp, gp, hd, hd],
        out_shape=[jax.ShapeDtypeStruct((s, HP), F32), jax.ShapeDtypeStruct((s, SSD_GROUPS * SSD_STATE), F32),
                   jax.ShapeDtypeStruct((s, SSD_GROUPS * SSD_STATE), F32), jax.ShapeDtypeStruct((s, HP), F32),
                   jax.ShapeDtypeStruct((s, HP), F32)],
        scratch_shapes=[pltpu.VMEM((HPG, SSD_STATE, LANES), F32)],
        compiler_params=_params("parallel", "arbitrary"))(xs, bc, bc, dtb, dab, s_in, dy, d_vec)


_GN = SSD_INNER // SSD_GROUPS
_GW = HP // SSD_GROUPS


def _ssd_post_fwd(y, xbc, p, d_vec, gain, name):
    s = y.shape[0]
    tm = _pick(s, 512)

    def body(y_ref, x_ref, z_ref, d_ref, g_ref, o_ref):
        z = z_ref[...].astype(F32)
        y2 = (y_ref[...] + x_ref[...] * d_ref[...]) * (z * _sigmoid(z))
        for g in range(SSD_GROUPS):
            gs = slice(g * _GW, (g + 1) * _GW)
            yg = y2[:, gs]
            r = lax.rsqrt(jnp.sum(yg * yg, axis=-1, keepdims=True) * (1.0 / _GN) + EPS)
            o_ref[:, gs] = (yg * r * g_ref[:, gs]).astype(BF16)

    hd = pl.BlockSpec((tm, HP), lambda i: (i, 0))
    return pl.pallas_call(
        body, name=name, grid=(s // tm,),
        in_specs=[hd, hd, pl.BlockSpec((tm, HP), lambda i: (i, C_Z // HP)), _const((1, HP)), _const((1, HP))],
        out_specs=hd, out_shape=jax.ShapeDtypeStruct((s, HP), BF16), compiler_params=_params("parallel"))(y, xbc, p, d_vec, gain)


def _ssd_post_bwd(y, xbc, p, d_vec, gain, dyn, name):
    s = y.shape[0]
    tm = _pick(s, 512)

    def body(y_ref, x_ref, z_ref, d_ref, g_ref, dn_ref, dy_ref, dz_ref, dg_ref, dd_ref):
        @pl.when(pl.program_id(0) == 0)
        def _():
            dg_ref[...] = jnp.zeros_like(dg_ref)
            dd_ref[...] = jnp.zeros_like(dd_ref)

        z, xv = z_ref[...].astype(F32), x_ref[...]
        sg = _sigmoid(z)
        sz = z * sg
        yt = y_ref[...] + xv * d_ref[...]
        y2 = yt * sz
        for g in range(SSD_GROUPS):
            gs = slice(g * _GW, (g + 1) * _GW)
            yg, dn = y2[:, gs], dn_ref[:, gs].astype(F32)
            r = lax.rsqrt(jnp.sum(yg * yg, axis=-1, keepdims=True) * (1.0 / _GN) + EPS)
            u = dn * g_ref[:, gs]
            dy2 = r * u - yg * (r * r * r) * (jnp.sum(yg * u, axis=-1, keepdims=True) * (1.0 / _GN))
            dg_ref[:, gs] += jnp.sum(dn * yg * r, axis=0, keepdims=True)
            dyt = dy2 * sz[:, gs]
            dy_ref[:, gs] = dyt
            dz_ref[:, gs] = (dy2 * yt[:, gs] * (sg[:, gs] * (1.0 + z[:, gs] * (1.0 - sg[:, gs])))).astype(BF16)
            dd_ref[:, gs] += jnp.sum(dyt * xv[:, gs], axis=0, keepdims=True)

    hd = pl.BlockSpec((tm, HP), lambda i: (i, 0))
    return pl.pallas_call(
        body, name=name, grid=(s // tm,),
        in_specs=[hd, hd, pl.BlockSpec((tm, HP), lambda i: (i, C_Z // HP)), _const((1, HP)), _const((1, HP)), hd],
        out_specs=[hd, hd, _const((1, HP)), _const((1, HP))],
        out_shape=[jax.ShapeDtypeStruct((s, HP), F32), jax.ShapeDtypeStruct((s, HP), BF16),
                   jax.ShapeDtypeStruct((1, HP), F32), jax.ShapeDtypeStruct((1, HP), F32)],
        compiler_params=_params("arbitrary"))(y, xbc, p, d_vec, gain, dyn)


def _merge_fwd(p, ya, o, yc, wb0, wb1, wb2, w_out, x, name, side=None):
    s = p.shape[0]
    tm = _pick(s, 512)

    def body(g_ref, ya_ref, o_ref, yc_ref, w0_ref, w1_ref, w2_ref, wo_ref, x_ref, mg_ref, y_ref):
        acc = jnp.zeros((tm, D_MODEL), F32)
        for i, (b_ref, w_ref) in enumerate(((ya_ref, w0_ref), (o_ref, w1_ref), (yc_ref, w2_ref))):
            t = jnp.dot(b_ref[...].astype(BF16), w_ref[...], preferred_element_type=F32)
            acc = acc + _sigmoid(g_ref[:, i * D_MODEL:(i + 1) * D_MODEL].astype(F32)) * t
        mg = acc.astype(BF16)
        mg_ref[...] = mg
        y_ref[...] = x_ref[...] + jnp.dot(mg, wo_ref[...], preferred_element_type=F32)

    row = pl.BlockSpec((tm, D_MODEL), lambda i: (i, 0))
    return _call(
        body, name=name, grid=(s // tm,),
        in_specs=[pl.BlockSpec((tm, 3 * D_MODEL), lambda i: (i, C_G // (3 * D_MODEL))),
                  pl.BlockSpec((tm, GM_WIDTH), lambda i: (i, 0)), row, row,
                  _resident((GM_WIDTH, D_MODEL)), _resident((HP, D_MODEL)), _resident((HP, D_MODEL)),
                  _resident((D_MODEL, D_MODEL)), row],
        out_specs=[row, row],
        out_shape=[jax.ShapeDtypeStruct((s, D_MODEL), BF16), jax.ShapeDtypeStruct((s, D_MODEL), F32)],
        scratch_shapes=[], args=(p, ya, o, yc, wb0, wb1, wb2, w_out, x), semantics=("parallel",), side=side)


def _merge_bwd(p, ya, o, yc, wb0, wb1, wb2, w_out, dy, name):
    s = p.shape[0]
    tm = _pick(s, 512)

    def body(g_ref, ya_ref, o_ref, yc_ref, w0_ref, w1_ref, w2_ref, wo_ref, dy_ref,
             d0_ref, d1_ref, d2_ref, dg_ref, dya_ref, do_ref, dyc_ref):
        dm = lax.dot_general(dy_ref[...].astype(BF16), wo_ref[...], _NT, preferred_element_type=F32)
        for i, (b_ref, w_ref, d_ref, db_ref) in enumerate(((ya_ref, w0_ref, d0_ref, dya_ref), (o_ref, w1_ref, d1_ref, do_ref),
                                                            (yc_ref, w2_ref, d2_ref, dyc_ref))):
            cs = slice(i * D_MODEL, (i + 1) * D_MODEL)
            t = jnp.dot(b_ref[...].astype(BF16), w_ref[...], preferred_element_type=F32)
            sg = _sigmoid(g_ref[:, cs].astype(F32))
            dt16 = (dm * sg).astype(BF16)
            d_ref[...] = dt16
            dg_ref[:, cs] = (dm * t * sg * (1.0 - sg)).astype(BF16)
            db_ref[...] = lax.dot_general(dt16, w_ref[...], _NT, preferred_element_type=F32).astype(db_ref.dtype)

    row = pl.BlockSpec((tm, D_MODEL), lambda i: (i, 0))
    nar = pl.BlockSpec((tm, GM_WIDTH), lambda i: (i, 0))
    wide = pl.BlockSpec((tm, 3 * D_MODEL), lambda i: (i, 0))
    return pl.pallas_call(
        body, name=name, grid=(s // tm,),
        in_specs=[pl.BlockSpec((tm, 3 * D_MODEL), lambda i: (i, C_G // (3 * D_MODEL))), nar, row, row,
                  _resident((GM_WIDTH, D_MODEL)), _resident((HP, D_MODEL)), _resident((HP, D_MODEL)),
                  _resident((D_MODEL, D_MODEL)), row],
        out_specs=[row, row, row, wide, nar, row, row],
        out_shape=[jax.ShapeDtypeStruct((s, D_MODEL), BF16)] * 3 + [jax.ShapeDtypeStruct((s, 3 * D_MODEL), BF16),
                   jax.ShapeDtypeStruct((s, GM_WIDTH), BF16), jax.ShapeDtypeStruct((s, D_MODEL), BF16),
                   jax.ShapeDtypeStruct((s, D_MODEL), F32)],
        compiler_params=_params("parallel"))(p, ya, o, yc, wb0, wb1, wb2, w_out, dy)


def _loss_head(y, target, name):
    s, d = y.shape
    tm = _pick(s, 512)

    def body(y_ref, t_ref, dy_ref, sq_ref):
        @pl.when(pl.program_id(0) == 0)
        def _():
            sq_ref[...] = jnp.zeros_like(sq_ref)

        e = y_ref[...] - t_ref[...]
        dy_ref[...] = e * (1.0 / d)
        sq_ref[...] += jnp.sum(e * e, axis=0, keepdims=True)

    row = pl.BlockSpec((tm, d), lambda i: (i, 0))
    return pl.pallas_call(
        body, name=name, grid=(s // tm,), in_specs=[row, row], out_specs=[row, _const((1, d))],
        out_shape=[jax.ShapeDtypeStruct((s, d), F32), jax.ShapeDtypeStruct((1, d), F32)],
        compiler_params=_params("arbitrary"))(y, target)


def _adamw(w, g, m, v, name):
    rows, cols = w.shape
    tr = rows
    for cand in (512, 256, 128, 64, 32, 16, 8):
        if rows % cand == 0 and cand * cols * 4 <= 3 * 1024 * 1024:
            tr = cand
            break

    def body(w_ref, g_ref, m_ref, v_ref, d_ref, nm_ref, nv_ref):
        d_ref[...], nm_ref[...], nv_ref[...] = _adam_update(w_ref[...], g_ref[...], m_ref[...], v_ref[...])

    blk = pl.BlockSpec((tr, cols), lambda i: (i, 0))
    return pl.pallas_call(
        body, name=name, grid=(rows // tr,), in_specs=[blk] * 4, out_specs=[blk] * 3,
        out_shape=[jax.ShapeDtypeStruct((rows, cols), F32)] * 3, compiler_params=_params("parallel"))(w, g, m, v)


def _adam_update(w, g, m, v):
    nm = ADAM_B1 * m + (1.0 - ADAM_B1) * g
    nv = ADAM_B2 * v + (1.0 - ADAM_B2) * (g * g)
    c1 = 1.0 - ADAM_B1 ** ADAM_STEP
    c2 = 1.0 - ADAM_B2 ** ADAM_STEP
    return -ADAM_LR * ((nm / c1) / (jnp.sqrt(nv / c2) + ADAM_EPS) + ADAM_WD * w), nm, nv


def _adamw_sharded(w, m, v, mine, theirs, name, side=None):
    depth, rows, cols = w.shape
    tr = _row_tile(rows // 2, cols, 1024 * 1024)
    nb = rows // 2 // tr

    def body(w_ref, m_ref, v_ref, a_ref, b_ref, g_ref, d_ref, nm_ref, nv_ref):
        c = lax.axis_index("c")
        g = jnp.where(pl.program_id(1) // nb == c, a_ref[...], b_ref[...])
        g_ref[...] = g
        d_ref[...], nm_ref[...], nv_ref[...] = _adam_update(w_ref[...], g, m_ref[...], v_ref[...])

    blk = pl.BlockSpec((None, tr, cols), lambda l, i: (l, i, 0))
    half = pl.BlockSpec((None, tr, cols), lambda l, i: (l, i % nb, 0))
    return _call(
        body, name=name, grid=(depth, rows // tr), in_specs=[blk, blk, blk, half, half], out_specs=[blk] * 4,
        out_shape=[jax.ShapeDtypeStruct((depth, rows, cols), F32)] * 4, scratch_shapes=[],
        args=(w, m, v, mine, theirs), semantics=("parallel", "parallel"), side=side)


ANY = pl.BlockSpec(memory_space=pl.ANY)


def _me():
    return lax.axis_index("x"), lax.axis_index("y"), lax.axis_index("c")


def _other_chips(x, y):
    return [(1 - x, y), (x, 1 - y), (1 - x, 1 - y)]


def _chip_index(cx, cy):
    return 2 * cx + cy


class _Exchange:
    def __init__(self, ins, out_shapes, n_sems, start, finish):
        self.ins, self.out_shapes, self.n_sems, self.start, self.finish = list(ins), list(out_shapes), n_sems, start, finish


def _sem_scratch(ex):
    return [pltpu.SemaphoreType.DMA((ex.n_sems,)), pltpu.SemaphoreType.DMA((ex.n_sems,))]


def _run_exchange(ex, name):
    n_in, n_out = len(ex.ins), len(ex.out_shapes)

    def body(*refs):
        in_refs, out_refs, (send, recv) = refs[:n_in], refs[n_in:n_in + n_out], refs[n_in + n_out:]
        ex.start(in_refs, out_refs, send, recv)
        ex.finish(in_refs, out_refs, send, recv)

    return pl.pallas_call(body, name=name, in_specs=[ANY] * n_in, out_specs=[ANY] * n_out, out_shape=ex.out_shapes,
                          scratch_shapes=_sem_scratch(ex))(*ex.ins)


def _call(body, *, name, grid, in_specs, out_specs, out_shape, scratch_shapes, args, semantics, side=None):
    if side is None:
        return pl.pallas_call(body, name=name, grid=grid, in_specs=in_specs, out_specs=out_specs, out_shape=out_shape,
                              scratch_shapes=scratch_shapes, compiler_params=_params(*semantics))(*args), []
    n_in, n_out, n_sc = len(in_specs), len(out_specs), len(scratch_shapes)
    s_in, s_out = len(side.ins), len(side.out_shapes)

    def hosted(*refs):
        pos = 0
        parts = []
        for size in (n_in, s_in, n_out, s_out, n_sc, 2):
            parts.append(refs[pos:pos + size])
            pos += size
        ins, sins, outs, souts, scratch, (send, recv) = parts
        ids = [pl.program_id(a) for a in range(len(grid))]
        first = functools.reduce(jnp.logical_and, [i == 0 for i in ids])
        last = functools.reduce(jnp.logical_and, [i == g - 1 for i, g in zip(ids, grid)])

        @pl.when(first)
        def _():
            side.start(sins, souts, send, recv)

        body(*ins, *outs, *scratch)

        @pl.when(last)
        def _():
            side.finish(sins, souts, send, recv)

    res = pl.pallas_call(
        hosted, name=name, grid=grid, in_specs=list(in_specs) + [ANY] * s_in, out_specs=list(out_specs) + [ANY] * s_out,
        out_shape=list(out_shape) + side.out_shapes, scratch_shapes=list(scratch_shapes) + _sem_scratch(side),
        compiler_params=_params(*["arbitrary"] * len(grid)))(*args, *side.ins)
    return res[:n_out], res[n_out:]


def _half(ref_rows, c):
    return pl.ds(c * (ref_rows // 2), ref_rows // 2)


def _gather_exchange(shards, layer):
    n = len(shards)
    rows = [a.shape[1] for a in shards]

    def copy(in_refs, out_refs, send, recv, t, k, chip, hc, to, from_input=False):
        dst = out_refs[t].at[chip, _half(rows[t], hc)]
        src = in_refs[t].at[layer, _half(rows[t], hc)] if from_input else dst
        return pltpu.make_async_remote_copy(src_ref=src, dst_ref=dst, send_sem=send.at[7 * t + k], recv_sem=recv.at[7 * t + k],
                                            device_id=to, device_id_type=MESH)

    def own(in_refs, out_refs, send, recv, t):
        x, y, c = _me()
        return pltpu.make_async_remote_copy(src_ref=in_refs[t].at[layer], dst_ref=out_refs[t].at[_chip_index(x, y)],
                                            send_sem=send.at[7 * t + 6], recv_sem=recv.at[7 * t + 6],
                                            device_id=(x, y, 1 - c), device_id_type=MESH)

    def start(in_refs, out_refs, send, recv):
        x, y, c = _me()
        for j, chip in enumerate(_other_chips(x, y)):
            for t in range(n):
                copy(in_refs, out_refs, send, recv, t, j, _chip_index(x, y), c, (*chip, c), from_input=True).start()
        for t in range(n):
            own(in_refs, out_refs, send, recv, t).start()

    def finish(in_refs, out_refs, send, recv):
        x, y, c = _me()
        chips = _other_chips(x, y)
        passed = []
        for t in range(n):
            own(in_refs, out_refs, send, recv, t).wait()
        for j, chip in enumerate(chips):
            for t in range(n):
                copy(in_refs, out_refs, send, recv, t, j, _chip_index(*chip), c, (x, y, c)).wait_recv()
                cp = copy(in_refs, out_refs, send, recv, t, 3 + j, _chip_index(*chip), c, (x, y, 1 - c))
                cp.start()
                passed.append(cp)
        for j, chip in enumerate(chips):
            for t in range(n):
                copy(in_refs, out_refs, send, recv, t, 3 + j, _chip_index(*chip), 1 - c, (x, y, c)).wait_recv()
                copy(in_refs, out_refs, send, recv, t, j, _chip_index(x, y), c, (*chip, c), from_input=True).wait_send()
        for cp in passed:
            cp.wait_send()

    return _Exchange(shards, [jax.ShapeDtypeStruct((N_CHIPS,) + a.shape[1:], a.dtype) for a in shards], 7 * n, start, finish)


def _pair_exchange(gs):
    n = len(gs)
    rows = [a.shape[1] for a in gs]

    def copies(in_refs, out_refs, send, recv):
        x, y, c = _me()
        return [pltpu.make_async_remote_copy(src_ref=in_refs[t].at[:, _half(rows[t], 1 - c)], dst_ref=out_refs[t],
                                             send_sem=send.at[t], recv_sem=recv.at[t], device_id=(x, y, 1 - c),
                                             device_id_type=MESH) for t in range(n)]

    def start(*refs):
        for cp in copies(*refs):
            cp.start()

    def finish(*refs):
        for cp in copies(*refs):
            cp.wait()

    return _Exchange(gs, [jax.ShapeDtypeStruct((N_CHIPS, a.shape[1] // 2, a.shape[2]), a.dtype) for a in gs], n, start, finish)


def _row_tile(rows, cols, budget=2 * 1024 * 1024):
    best = None
    for t in range(8, rows + 1, 8):
        if rows % t == 0 and t * cols * 4 <= budget:
            best = t
    return best or rows


def _pair_add(g, got, name):
    _, rows, cols = g.shape
    tr = _row_tile(rows // 2, cols)
    nb = rows // 2 // tr

    def body(lo_ref, hi_ref, r_ref, o16_ref, own_ref):
        x, y, c = _me()
        tot = jnp.where(c == 0, lo_ref[...], hi_ref[...]) + r_ref[...]
        o16_ref[...] = tot.astype(BF16)

        @pl.when(pl.program_id(1) == _chip_index(x, y))
        def _():
            own_ref[...] = tot

    blk = (None, tr, cols)
    return pl.pallas_call(
        body, name=name, grid=(nb, N_CHIPS),
        in_specs=[pl.BlockSpec(blk, lambda i, k: (k, i, 0)), pl.BlockSpec(blk, lambda i, k: (k, i + nb, 0)),
                  pl.BlockSpec(blk, lambda i, k: (k, i, 0))],
        out_specs=[pl.BlockSpec(blk, lambda i, k: (k, i, 0)), pl.BlockSpec((tr, cols), lambda i, k: (i, 0))],
        out_shape=[jax.ShapeDtypeStruct((N_CHIPS, rows // 2, cols), BF16), jax.ShapeDtypeStruct((rows // 2, cols), F32)],
        compiler_params=_params("parallel", "arbitrary"))(g, g, got)


def _chip_exchange(parts):
    n = len(parts)

    def copies(in_refs, out_refs, send, recv):
        x, y, c = _me()
        return [pltpu.make_async_remote_copy(src_ref=in_refs[t].at[_chip_index(*chip)], dst_ref=out_refs[t].at[j],
                                             send_sem=send.at[3 * t + j], recv_sem=recv.at[3 * t + j],
                                             device_id=(*chip, c), device_id_type=MESH)
                for j, chip in enumerate(_other_chips(x, y)) for t in range(n)]

    def start(*refs):
        for cp in copies(*refs):
            cp.start()

    def finish(*refs):
        for cp in copies(*refs):
            cp.wait()

    return _Exchange(parts, [jax.ShapeDtypeStruct((3,) + a.shape[1:], a.dtype) for a in parts], 3 * n, start, finish)


def _chip_add(own, got, name, layer, into=None):
    rows, cols = own.shape
    tr = _row_tile(rows, cols, 1024 * 1024)

    def body(own_ref, got_ref, *rest):
        acc = own_ref[...]
        for j in range(3):
            acc = acc + got_ref[j].astype(F32)
        rest[-1][...] = acc

    in_specs = [pl.BlockSpec((tr, cols), lambda i: (i, 0)), pl.BlockSpec((3, tr, cols), lambda i: (0, i, 0))]
    args, alias = [own, got], {}
    if into is not None:
        in_specs.append(ANY)
        args.append(into)
        alias = {2: 0}
    return pl.pallas_call(
        body, name=name, grid=(rows // tr,), in_specs=in_specs,
        out_specs=pl.BlockSpec((None, tr, cols), lambda i: (layer, i, 0)),
        out_shape=jax.ShapeDtypeStruct((DEPTH, rows, cols), F32), input_output_aliases=alias,
        compiler_params=_params("parallel"))(*args)


def _pair_share(halves):
    n = len(halves)

    def copies(in_refs, out_refs, send, recv):
        x, y, c = _me()
        return [pltpu.make_async_remote_copy(src_ref=in_refs[t], dst_ref=out_refs[t], send_sem=send.at[t],
                                             recv_sem=recv.at[t], device_id=(x, y, 1 - c), device_id_type=MESH)
                for t in range(n)]

    def start(*refs):
        for cp in copies(*refs):
            cp.start()

    def finish(*refs):
        for cp in copies(*refs):
            cp.wait()

    return _Exchange(halves, [jax.ShapeDtypeStruct(a.shape, a.dtype) for a in halves], n, start, finish)


N_DEV = 8


def _all_exchange(v):
    r, cols = v.shape

    def peers():
        x, y, c = _me()
        flip = lambda v, f: 1 - v if f else v
        return 4 * x + 2 * y + c, [(flip(x, fx), flip(y, fy), flip(c, fc)) for fx in (0, 1) for fy in (0, 1) for fc in (0, 1)][1:]

    def local(in_refs, out_refs, send, me):
        return pltpu.make_async_copy(in_refs[0], out_refs[0].at[me], send.at[7])

    def start(in_refs, out_refs, send, recv):
        me, others = peers()
        local(in_refs, out_refs, send, me).start()
        for j, peer in enumerate(others):
            pltpu.make_async_remote_copy(src_ref=in_refs[0], dst_ref=out_refs[0].at[me], send_sem=send.at[j],
                                         recv_sem=recv.at[j], device_id=peer, device_id_type=MESH).start()

    def finish(in_refs, out_refs, send, recv):
        me, others = peers()
        for j, (px, py, pc) in enumerate(others):
            pltpu.make_async_remote_copy(src_ref=in_refs[0], dst_ref=out_refs[0].at[4 * px + 2 * py + pc], send_sem=send.at[j],
                                         recv_sem=recv.at[j], device_id=(px, py, pc), device_id_type=MESH).wait()
        local(in_refs, out_refs, send, me).wait()

    return _Exchange([v], [jax.ShapeDtypeStruct((N_DEV, r, cols), v.dtype)], 8, start, finish)


def _sum_slots(a, name):
    n, r, cols = a.shape
    tr = _pick(r, 512) if r % 8 == 0 else r
    for cand in (512, 256, 128, 64, 32, 16, 8):
        if r % cand == 0:
            tr = cand
            break

    def body(a_ref, o_ref):
        acc = a_ref[0]
        for k in range(1, n):
            acc = acc + a_ref[k]
        o_ref[...] = acc

    return pl.pallas_call(
        body, name=name, grid=(r // tr,), in_specs=[pl.BlockSpec((n, tr, cols), lambda i: (0, i, 0))],
        out_specs=pl.BlockSpec((tr, cols), lambda i: (i, 0)), out_shape=jax.ShapeDtypeStruct((r, cols), F32),
        compiler_params=_params("parallel"))(a)


def _join(name, stacked):
    ax = SHARDED[name][1]
    return jnp.concatenate([stacked[k] for k in range(N_CHIPS)], axis=ax)


def _split(name, full):
    ax = SHARDED[name][1]
    return jnp.stack(jnp.split(full, N_CHIPS, axis=ax))


def _heads_pad(a, real, axis):
    shp = a.shape
    a = a.reshape(shp[:axis] + (MLA_HEADS, real) + shp[axis + 1:])
    pad = [(0, 0)] * a.ndim
    pad[axis + 1] = (0, LANES - real)
    a = jnp.pad(a, pad)
    return a.reshape(shp[:axis] + (HP,) + shp[axis + 1:])


def _heads_unpad(a, real, axis):
    shp = a.shape
    a = a.reshape(shp[:axis] + (MLA_HEADS, LANES) + shp[axis + 1:])
    a = lax.slice_in_dim(a, 0, real, axis=axis + 1)
    return a.reshape(shp[:axis] + (MLA_HEADS * real,) + shp[axis + 1:])


def _lane_place(a, start):
    n = a.shape[-1]
    pad = [(0, 0)] * (a.ndim - 1) + [(start, LANES - start - n)]
    return jnp.pad(a, pad)


_O_UV, _O_CQ, _O_CKV, _O_KR, _O_Z, _O_XBC, _O_DT, _O_G = 0, 1024, 1408, 1664, 1696, 2208, 3232, 3240


def _w_in_pad(w):
    sl = lambda a, b: w[:, a:b]
    xs = _heads_pad(sl(_O_XBC, _O_XBC + SSD_INNER), SSD_HEAD_DIM, 1)
    bc = sl(_O_XBC + SSD_INNER, _O_DT)
    main = jnp.concatenate([sl(_O_UV, _O_CQ), _heads_pad(sl(_O_Z, _O_XBC), SSD_HEAD_DIM, 1), xs, sl(_O_G, IN_COLS)], axis=1)
    tail = jnp.concatenate([bc, sl(_O_CKV, _O_KR), sl(_O_CQ, _O_CKV), _lane_place(sl(_O_KR, _O_Z), MLA_NOPE),
                            _lane_place(sl(_O_DT, _O_G), 0), jnp.zeros((w.shape[0], PW_TAIL - T_DT - LANES), w.dtype)], axis=1)
    return main, tail


def _w_in_unpad(gm, gt):
    m = lambda a, n: gm[:, a:a + n]
    t = lambda a, n: gt[:, a:a + n]
    parts = [m(C_UV, 1024), t(T_CQ, MLA_Q_RANK), t(T_CKV, MLA_KV_RANK), t(T_KR + MLA_NOPE, MLA_ROPE),
             _heads_unpad(m(C_Z, HP), SSD_HEAD_DIM, 1), _heads_unpad(m(C_XS, HP), SSD_HEAD_DIM, 1), t(T_BC, BCW),
             t(T_DT, SSD_HEADS), m(C_G, 3 * D_MODEL)]
    return jnp.concatenate(parts, axis=1)


def _xbc_pad(a):
    return jnp.concatenate([_heads_pad(a[..., :SSD_INNER], SSD_HEAD_DIM, a.ndim - 1), a[..., SSD_INNER:]], axis=-1)


def _xbc_unpad(a):
    return jnp.concatenate([_heads_unpad(a[..., :HP], SSD_HEAD_DIM, a.ndim - 1), a[..., HP:]], axis=-1)


def _rope_tables(positions):
    inv_freq = 1.0 / (ROPE_THETA ** (jnp.arange(0, MLA_ROPE, 2, dtype=F32) / MLA_ROPE))
    ang = positions.astype(F32)[:, None] * inv_freq
    cos, sin = jnp.cos(ang), jnp.sin(ang)
    s = positions.shape[0]
    half = MLA_ROPE // 2
    z = lambda n: jnp.zeros((s, n), F32)
    ct = jnp.concatenate([jnp.ones((s, MLA_NOPE), F32), cos, cos, z(LANES - MLA_QK)], axis=1)
    s1 = jnp.concatenate([z(MLA_NOPE), -sin, z(half), z(LANES - MLA_QK)], axis=1)
    s2 = jnp.concatenate([z(MLA_NOPE), z(half), sin, z(LANES - MLA_QK)], axis=1)
    return ct, s1, s2


def _layer_weights(full, small, l, part):
    w = {}
    row = lambda n: small[n][l][None, :]
    stacked = lambda g: g.reshape((N_CHIPS * g.shape[1], g.shape[2]))
    if part in ('ffn1', 'ffn2'):
        w[part + '_w_in'] = full[part + '_w_in']
        w[part + '_w_out'] = stacked(full[part + '_w_out'])
        w[part + '_norm'] = row(part + '_norm')
        return w
    w['w_out'] = stacked(full['w_out'])
    fl = {n: _join(n, full[n]) for n in ('w_in', 'mla_w_uq', 'mla_w_ukv', 'w_branch', 'ssd_conv_w')}
    w['w_in_main'], w['w_in_tail'] = _w_in_pad(fl['w_in'])
    w['wuq'] = _heads_pad(fl['mla_w_uq'], MLA_QK, 1)
    ukv = fl['mla_w_ukv'].reshape(MLA_KV_RANK, MLA_HEADS, MLA_NOPE + MLA_V)
    zero = jnp.zeros((MLA_KV_RANK, MLA_HEADS, LANES - MLA_NOPE), ukv.dtype)
    wk = jnp.concatenate([ukv[:, :, :MLA_NOPE], zero], axis=2).reshape(MLA_KV_RANK, HP)
    wv = jnp.concatenate([ukv[:, :, MLA_NOPE:], zero], axis=2).reshape(MLA_KV_RANK, HP)
    w['wkv'] = jnp.concatenate([wk, wv], axis=1)
    wb = fl['w_branch']
    w['wb0'] = wb[0]
    w['wb1'] = _heads_pad(wb[1], MLA_V, 0)
    w['wb2'] = _heads_pad(wb[2], SSD_HEAD_DIM, 0)
    w['conv_w'] = _xbc_pad(fl['ssd_conv_w'].astype(F32))
    for n in ('mix_norm', 'gm_v_norm', 'mla_q_norm', 'mla_kv_norm'):
        w[n] = row(n)
    w['gm_w_s'] = small['gm_w_s'][l]
    w['gm_b_full'] = jnp.broadcast_to(small['gm_b_s'][l][:, :, None], (GM_GROUPS, CHUNK, LANES))
    w['gq'] = _lane_place(row('mla_q_gain'), 0)
    w['gk'] = _lane_place(row('mla_k_gain'), 0)
    w['conv_b'] = _xbc_pad(row('ssd_conv_b'))
    w['dt_bias'] = _lane_place(row('ssd_dt_bias'), 0)
    w['a_log'] = _lane_place(row('ssd_a_log'), 0)
    w['d_vec'] = jnp.repeat(small['ssd_d'][l], LANES)[None, :]
    w['ssd_norm'] = _heads_pad(row('ssd_norm'), SSD_HEAD_DIM, 1)
    return w


_MIXER_SMALL = ['mla_w_uq', 'mla_w_ukv', 'ssd_conv_w', 'w_branch', 'w_out']
_MIXER_SMALL_G = [n for n in _MIXER_SMALL if n != 'ssd_conv_w']
GATHER_HOSTS = {'attn': ['ffn1_w_in', 'ffn2_w_in'], 'scan': ['ffn1_w_out', 'ffn2_w_out'], 'merge': _MIXER_SMALL, 'ffn2_in': ['w_in']}
GATHER_HOSTS_LATER = {'ffn1_in': ['ffn1_w_out'], 'proj': ['w_in'], 'attn': ['ffn1_w_in', 'ffn2_w_in'], 'scan': ['ffn2_w_out'],
                      'merge': _MIXER_SMALL}
FIRST_NOW = ['ffn1_w_in', 'ffn1_w_out']
FIRST_HOSTS = {'ffn1_in': ['w_in'], 'ffn1_out': _MIXER_SMALL, 'proj': ['ffn2_w_in', 'ffn2_w_out']}
PAIR_HOSTS = {'ffn2_dwout': ['ffn1_w_in', 'ffn2_w_in'], 'ffn2_dwin': ['ffn1_w_out', 'w_in', 'ffn2_w_out'] + _MIXER_SMALL_G}
REDUCE_HOSTS = {'dattn_q': ['ffn1_w_out', 'w_in', 'ffn2_w_out'], 'dattn_kv': ['ffn1_w_in', 'ffn2_w_in'], 'dmla_pre': _MIXER_SMALL_G}
LAST_EARLY = ['w_in', 'ffn2_w_in', 'ffn2_w_out'] + _MIXER_SMALL_G
LAST_HOSTS = {'ffn1_dact': ['w_in'], 'ffn1_dwin': ['ffn2_w_in'], 'ffn1_dx': ['ffn2_w_out'] + _MIXER_SMALL_G}
LAST_LATE = ['ffn1_w_in', 'ffn1_w_out']


def _ffn_fwd(x, norm, w4, w_out, tag, sides=None):
    sides = sides or {}
    carried = {}
    (h, gate, up, act), carried[f"{tag}_in"] = _ffn_in(x, norm, w4, f"{tag}_in", sides.get(f"{tag}_in"))
    y, carried[f"{tag}_out"] = _ffn_out(act, w_out, x, f"{tag}_out", sides.get(f"{tag}_out"))
    return y, (x, h, gate, up, act), carried


def _ffn_bwd(dy, saved, norm, w4, w_out, tag, sides=None, after_dwout=None):
    sides = dict(sides or {})
    carried = {}
    x, h, gate, up, act = saved
    dw_out, carried[f"{tag}_dwout"] = _ffn_dwout(act, dy, f"{tag}_dwout", sides.get(f"{tag}_dwout"))
    if after_dwout is not None:
        sides.update(after_dwout(carried[f"{tag}_dwout"]))
    da, carried[f"{tag}_dact"] = _ffn_dact(dy, w_out, gate, up, f"{tag}_dact", sides.get(f"{tag}_dact"))
    dw_in, carried[f"{tag}_dwin"] = _ffn_dwin(h, da, f"{tag}_dwin", sides.get(f"{tag}_dwin"))
    (dx, dnorm), carried[f"{tag}_dx"] = _ffn_dx(da, w4, x, norm, dy, f"{tag}_dx", sides.get(f"{tag}_dx"))
    return dx, dnorm, dw_in, dw_out.reshape((N_CHIPS, 2 * FC // N_CHIPS, D_MODEL)), carried


def _mixer_fwd(x, w, tabs, tag, sides=None):
    sides = sides or {}
    carried = {}
    h = _rmsnorm_fwd(x, w['mix_norm'], f"{tag}_norm")
    if sides.get('proj') is None:
        pm = _matmul(h, w['w_in_main'], out_dtype=BF16, name=f"{tag}_proj_main")
    else:
        pm, carried['proj'] = _matmul(h, w['w_in_main'], out_dtype=BF16, name=f"{tag}_proj_main", side=sides['proj'])
    pt = _matmul(h, w['w_in_tail'], name=f"{tag}_proj_tail")
    ya = _gmlp_fwd(pm, w['gm_v_norm'], w['gm_w_s'], w['gm_b_full'], f"{tag}_gmlp")
    q, k, v = _mla_pre_fwd(pt, tabs, w['mla_q_norm'], w['mla_kv_norm'], w['wuq'], w['wkv'], w['gq'], w['gk'], f"{tag}_mla_pre")
    (o, lse), carried['attn'] = _attn_fwd(q, k, v, f"{tag}_attn", sides.get('attn'))
    xs = _conv_fwd(pm, C_XS, HP, w['conv_w'][:, :HP], w['conv_b'][:, :HP], f"{tag}_conv_x")
    bc = _conv_fwd(pt, T_BC, BCW, w['conv_w'][:, HP:], w['conv_b'][:, HP:], f"{tag}_conv_bc")
    dtb, dab = _dt_fwd(pt, w['dt_bias'], w['a_log'], f"{tag}_dt")
    (ys, s_in), carried['scan'] = _scan_fwd(xs, bc, dtb, dab, f"{tag}_scan", sides.get('scan'))
    yc = _ssd_post_fwd(ys, xs, pm, w['d_vec'], w['ssd_norm'], f"{tag}_ssd_post")
    (mg, y), carried['merge'] = _merge_fwd(pm, ya, o, yc, w['wb0'], w['wb1'], w['wb2'], w['w_out'], x, f"{tag}_merge",
                                           sides.get('merge'))
    return y, (x, h, pm, pt, ya, q, k, v, o, lse, xs, bc, dtb, dab, ys, s_in, yc, mg), carried


def _pair_sums(pending, got):
    return {n: _pair_add(pending[n], got[n], f"pair_add_{n}") for n in got}


def _chip_sums(sums, arrived, layer, stacked):
    for n in arrived:
        stacked[n] = _chip_add(sums[n][1], arrived[n], f"chip_add_{n}", layer, stacked.get(n))


def _reduce_to_chip(pending, layer, stacked):
    names = list(pending)
    got = _run_exchange(_pair_exchange([pending[n] for n in names]), "pair_exchange")
    sums = _pair_sums(pending, dict(zip(names, got)))
    arrived = _run_exchange(_chip_exchange([sums[n][0] for n in names]), "chip_exchange")
    _chip_sums(sums, dict(zip(names, arrived)), layer, stacked)


def _mixer_bwd(dy, saved, w, tabs, tag, sides=None):
    sides = sides or {}
    carried = {}
    x, h, pm, pt, ya, q, k, v, o, lse, xs, bc, dtb, dab, ys, s_in, yc, mg = saved
    g = {}
    g['w_out'] = _matmul(mg, dy, ta=True, name=f"{tag}_dwout").reshape((N_CHIPS, D_MODEL // N_CHIPS, D_MODEL))
    d0, d1, d2, dgates, dya, do, dyc = _merge_bwd(pm, ya, o, yc, w['wb0'], w['wb1'], w['wb2'], w['w_out'], dy, f"{tag}_dmerge")
    dwb0 = _matmul(ya, d0, ta=True, name=f"{tag}_dwb0")
    dwb1 = _matmul(o, d1, ta=True, name=f"{tag}_dwb1")
    dwb2 = _matmul(yc, d2, ta=True, name=f"{tag}_dwb2")
    g['w_branch'] = _split('w_branch', jnp.stack([dwb0, _heads_unpad(dwb1, MLA_V, 0), _heads_unpad(dwb2, SSD_HEAD_DIM, 0)]))
    duv, g['gm_v_norm'], g['gm_w_s'], db = _gmlp_bwd(pm, w['gm_v_norm'], w['gm_w_s'], w['gm_b_full'], dya, f"{tag}_dgmlp")
    g['gm_b_s'] = db.T
    (dq, delta), carried['dattn_q'] = _attn_bwd_dq(q, k, v, o, lse, do, f"{tag}_dattn_q", sides.get('dattn_q'))
    (dk, dv), carried['dattn_kv'] = _attn_bwd_dkv(q, k, v, lse, delta, do, f"{tag}_dattn_kv", sides.get('dattn_kv'))
    (dcq, dckv, dkr, dwuq, dwkv, g['mla_q_norm'], g['mla_kv_norm'], dgq, dgk), carried['dmla_pre'] = _mla_pre_bwd(
        pt, tabs, w['mla_q_norm'], w['mla_kv_norm'], w['wuq'], w['wkv'], w['gq'], w['gk'], dq, dk, dv, f"{tag}_dmla_pre",
        sides.get('dmla_pre'))
    g['mla_w_uq'] = _split('mla_w_uq', _heads_unpad(dwuq, MLA_QK, 1))
    dwk = dwkv[:, :HP].reshape(MLA_KV_RANK, MLA_HEADS, LANES)[:, :, :MLA_NOPE]
    dwv = dwkv[:, HP:].reshape(MLA_KV_RANK, MLA_HEADS, LANES)[:, :, :MLA_V]
    g['mla_w_ukv'] = _split('mla_w_ukv', jnp.concatenate([dwk, dwv], axis=2).reshape(MLA_KV_RANK, MLA_HEADS * (MLA_NOPE + MLA_V)))
    g['mla_q_gain'], g['mla_k_gain'] = dgq[:, :MLA_QK], dgk[:, :MLA_QK]
    dys, dz, dssd_norm, dd = _ssd_post_bwd(ys, xs, pm, w['d_vec'], w['ssd_norm'], dyc, f"{tag}_dssd_post")
    g['ssd_norm'] = _heads_unpad(dssd_norm, SSD_HEAD_DIM, 1)
    g['ssd_d'] = jnp.sum(dd.reshape(SSD_HEADS, LANES), axis=1)[None, :]
    dxs, dbm, dcm, dda, ddtx = _scan_bwd(xs, bc, dtb, dab, s_in, dys, w['d_vec'], f"{tag}_dscan")
    dxs16, dcw_x, dcb_x = _conv_bwd(pm, C_XS, HP, w['conv_w'][:, :HP], w['conv_b'][:, :HP], dxs, f"{tag}_dconv_x")
    dbc16, dcw_bc, dcb_bc = _conv_bwd(pt, T_BC, BCW, w['conv_w'][:, HP:], w['conv_b'][:, HP:],
                                      jnp.concatenate([dbm, dcm], axis=1), f"{tag}_dconv_bc")
    g['ssd_conv_w'] = _xbc_unpad(jnp.concatenate([dcw_x, dcw_bc], axis=1))
    g['ssd_conv_b'] = _xbc_unpad(jnp.concatenate([dcb_x, dcb_bc], axis=1))
    ddt, dbias, dalog = _dt_bwd(pt, w['dt_bias'], w['a_log'], dda, ddtx, f"{tag}_ddt")
    g['ssd_dt_bias'], g['ssd_a_log'] = dbias[:, :SSD_HEADS], dalog[:, :SSD_HEADS]
    s = x.shape[0]
    dpm = jnp.concatenate([duv, dz, dxs16, dgates], axis=1)
    dpt = jnp.concatenate([dbc16, dckv, dcq, dkr, ddt, jnp.zeros((s, PW_TAIL - T_DT - LANES), BF16)], axis=1)
    g['w_in'] = _split('w_in', _w_in_unpad(_matmul(h, dpm, ta=True, name=f"{tag}_dwin_main"),
                                           _matmul(h, dpt, ta=True, name=f"{tag}_dwin_tail")))
    dh = _matmul(dpt, w['w_in_tail'], tb=True, name=f"{tag}_dh_tail")
    dh = _matmul(dpm, w['w_in_main'], tb=True, res=dh, name=f"{tag}_dh_main")
    dx, g['mix_norm'] = _rmsnorm_bwd(x, w['mix_norm'], dh, dy, f"{tag}_dnorm")
    return dx, g, carried


_CONV_ROWS = 32


def _rows_cols(a, lead):
    return a.reshape(a.shape[:lead] + (int(np.prod(a.shape[lead:-1])), a.shape[-1]))


def _shard_views(wts):
    views = []
    for n in SHARDED_ORDER:
        a = _rows_cols(wts[n].astype(BF16), 1)
        if n == 'ssd_conv_w':
            a = jnp.pad(a, ((0, 0), (0, _CONV_ROWS - a.shape[1]), (0, 0)))
        views.append(a)
    return views


def _gathered(names, arrays):
    out = {}
    for n, a in zip(names, arrays):
        shp = _shard_shape(n)
        if n == 'ssd_conv_w':
            a = a[:, :shp[0]]
        out[n] = a.reshape((N_CHIPS,) + shp)
    return out


def _local_step(x, positions, target, weights, small, distributed=True):
    tabs = _rope_tables(positions)
    views = dict(zip(SHARDED_ORDER, weights)) if distributed else None
    plan = [{} for _ in range(DEPTH)]
    if distributed:
        for l in range(DEPTH - 1):
            plan[l].update({host: (names, l + 1) for host, names in (GATHER_HOSTS if l == 0 else GATHER_HOSTS_LATER).items()})
        plan[0].update({host: (names, 0) for host, names in FIRST_HOSTS.items()})
        have = [dict() for _ in range(DEPTH)]
        have[0].update(_gathered(FIRST_NOW, _run_exchange(_gather_exchange([views[n] for n in FIRST_NOW], 0), "gather_first")))
    else:
        have = weights

    def absorb(l, carried):
        for host, arrays in carried.items():
            if host in plan[l]:
                names, layer = plan[l][host]
                have[layer].update(_gathered(names, arrays))

    ws, saved = [], []
    for l in range(DEPTH):
        sides = {host: _gather_exchange([views[n] for n in names], layer) for host, (names, layer) in plan[l].items()}
        w = _layer_weights(have[l], small, l, 'ffn1')
        x, s1, carried = _ffn_fwd(x, w['ffn1_norm'], w['ffn1_w_in'], w['ffn1_w_out'], "ffn1", sides)
        absorb(l, carried)
        w.update(_layer_weights(have[l], small, l, 'mixer'))
        x, s2, carried = _mixer_fwd(x, w, tabs, "mix", sides)
        absorb(l, carried)
        w.update(_layer_weights(have[l], small, l, 'ffn2'))
        x, s3, carried = _ffn_fwd(x, w['ffn2_norm'], w['ffn2_w_in'], w['ffn2_w_out'], "ffn2", sides)
        absorb(l, carried)
        ws.append(w)
        saved.append((s1, s2, s3))
    dy, sq = _loss_head(x, target, "loss_head")
    loss = 0.5 * jnp.sum(sq) / D_MODEL
    grads, reduced, pending = [None] * DEPTH, {}, None

    def chip_sides(sums, hosts):
        return {host: _chip_exchange([sums[n][0] for n in names]) for host, names in hosts.items()}

    def arrivals(carried, hosts):
        return {n: a for host, names in hosts.items() for n, a in zip(names, carried[host])}

    for l in reversed(range(DEPTH)):
        w = ws[l]
        s1, s2, s3 = saved[l]
        sides = {host: _pair_exchange([pending[n] for n in names]) for host, names in PAIR_HOSTS.items()} if pending else {}
        dy, dn2, dwi2, dwo2, carried = _ffn_bwd(dy, s3, w['ffn2_norm'], w['ffn2_w_in'], w['ffn2_w_out'], "ffn2", sides)
        sides = {}
        if pending:
            sums = _pair_sums(pending, arrivals(carried, PAIR_HOSTS))
            sides = chip_sides(sums, REDUCE_HOSTS)
        dy, g, carried = _mixer_bwd(dy, s2, w, tabs, "mix", sides)
        if pending:
            _chip_sums(sums, arrivals(carried, REDUCE_HOSTS), l + 1, reduced)
        g.update(ffn2_norm=dn2, ffn2_w_in=dwi2, ffn2_w_out=dwo2)
        last = distributed and l == 0
        if last:
            early = {n: _rows_cols(g[n], 1) for n in LAST_EARLY}
            after = {}

            def after_dwout(got):
                after['sums'] = _pair_sums(early, dict(zip(LAST_EARLY, got)))
                return chip_sides(after['sums'], LAST_HOSTS)

            dy, dn1, dwi1, dwo1, carried = _ffn_bwd(dy, s1, w['ffn1_norm'], w['ffn1_w_in'], w['ffn1_w_out'], "ffn1",
                                                    {'ffn1_dwout': _pair_exchange([early[n] for n in LAST_EARLY])}, after_dwout)
            _chip_sums(after['sums'], arrivals(carried, LAST_HOSTS), 0, reduced)
        else:
            dy, dn1, dwi1, dwo1, _ = _ffn_bwd(dy, s1, w['ffn1_norm'], w['ffn1_w_in'], w['ffn1_w_out'], "ffn1")
        g.update(ffn1_norm=dn1, ffn1_w_in=dwi1, ffn1_w_out=dwo1)
        grads[l] = g
        if distributed:
            pending = {n: _rows_cols(g[n], 1) for n in REDUCED}
    if distributed:
        _reduce_to_chip({n: pending[n] for n in LAST_LATE}, 0, reduced)
    return loss, dy, grads, reduced


SMALL_PACK = SMALL_ORDER + ['ssd_conv_w']


def _pack_small(per_layer_rows, tail=None):
    parts = [per_layer_rows[l][n].reshape(-1).astype(F32) for l in range(DEPTH) for n in SMALL_PACK]
    if tail is not None:
        parts.append(tail.reshape(1))
    flat = jnp.concatenate(parts)
    rows = -(-flat.shape[0] // LANES)
    rows = -(-rows // 8) * 8
    return jnp.pad(flat, (0, rows * LANES - flat.shape[0])).reshape(rows, LANES)


def _unpack_small(buf, shapes):
    flat = buf.reshape(-1)
    off = 0
    out = {n: [] for n in SMALL_PACK}
    for l in range(DEPTH):
        for n in SMALL_PACK:
            size = int(np.prod(shapes[n]))
            out[n].append(flat[off:off + size].reshape(shapes[n]))
            off += size
    return {n: jnp.stack(v) for n, v in out.items()}


def kernel(x, positions, ffn1_norm, ffn1_w_in, ffn1_w_out, mix_norm, w_in, gm_v_norm, gm_w_s, gm_b_s, mla_q_norm, mla_kv_norm, mla_w_uq, mla_w_ukv, mla_q_gain, mla_k_gain, ssd_conv_w, ssd_conv_b, ssd_dt_bias, ssd_a_log, ssd_d, ssd_norm, w_branch, w_out, ffn2_norm, ffn2_w_in, ffn2_w_out, loss_target, m_ffn1_norm, m_ffn1_w_in, m_ffn1_w_out, m_mix_norm, m_w_in, m_gm_v_norm, m_gm_w_s, m_gm_b_s, m_mla_q_norm, m_mla_kv_norm, m_mla_w_uq, m_mla_w_ukv, m_mla_q_gain, m_mla_k_gain, m_ssd_conv_w, m_ssd_conv_b, m_ssd_dt_bias, m_ssd_a_log, m_ssd_d, m_ssd_norm, m_w_branch, m_w_out, m_ffn2_norm, m_ffn2_w_in, m_ffn2_w_out, v_ffn1_norm, v_ffn1_w_in, v_ffn1_w_out, v_mix_norm, v_w_in, v_gm_v_norm, v_gm_w_s, v_gm_b_s, v_mla_q_norm, v_mla_kv_norm, v_mla_w_uq, v_mla_w_ukv, v_mla_q_gain, v_mla_k_gain, v_ssd_conv_w, v_ssd_conv_b, v_ssd_dt_bias, v_ssd_a_log, v_ssd_d, v_ssd_norm, v_w_branch, v_w_out, v_ffn2_norm, v_ffn2_w_in, v_ffn2_w_out):
    wts = dict(zip(WEIGHTS, (ffn1_norm, ffn1_w_in, ffn1_w_out, mix_norm, w_in, gm_v_norm, gm_w_s, gm_b_s, mla_q_norm, mla_kv_norm,
                             mla_w_uq, mla_w_ukv, mla_q_gain, mla_k_gain, ssd_conv_w, ssd_conv_b, ssd_dt_bias, ssd_a_log, ssd_d,
                             ssd_norm, w_branch, w_out, ffn2_norm, ffn2_w_in, ffn2_w_out)))
    mom = dict(zip(WEIGHTS, (m_ffn1_norm, m_ffn1_w_in, m_ffn1_w_out, m_mix_norm, m_w_in, m_gm_v_norm, m_gm_w_s, m_gm_b_s, m_mla_q_norm,
                             m_mla_kv_norm, m_mla_w_uq, m_mla_w_ukv, m_mla_q_gain, m_mla_k_gain, m_ssd_conv_w, m_ssd_conv_b,
                             m_ssd_dt_bias, m_ssd_a_log, m_ssd_d, m_ssd_norm, m_w_branch, m_w_out, m_ffn2_norm, m_ffn2_w_in,
                             m_ffn2_w_out)))
    var = dict(zip(WEIGHTS, (v_ffn1_norm, v_ffn1_w_in, v_ffn1_w_out, v_mix_norm, v_w_in, v_gm_v_norm, v_gm_w_s, v_gm_b_s, v_mla_q_norm,
                             v_mla_kv_norm, v_mla_w_uq, v_mla_w_ukv, v_mla_q_gain, v_mla_k_gain, v_ssd_conv_w, v_ssd_conv_b,
                             v_ssd_dt_bias, v_ssd_a_log, v_ssd_d, v_ssd_norm, v_w_branch, v_w_out, v_ffn2_norm, v_ffn2_w_in,
                             v_ffn2_w_out)))
    cx, cy, _ = _me()
    mychip = _chip_index(cx, cy)

    small = {n: wts[n] for n in SMALL_ORDER}
    loss_part, dx, grads, reduced = _local_step(x[0], positions[0], loss_target[0], _shard_views(wts), small)
    rows_cols = _rows_cols
    halves = [reduced[n] for n in REDUCED]
    theirs = _run_exchange(_pair_share(halves), "pair_share")
    grad, delta, new_m, new_v = {}, {}, {}, {}
    everyone = _all_exchange(_pack_small(grads, tail=loss_part))
    for n, a, b in zip(REDUCED, halves, theirs):
        shp = wts[n].shape
        outs, carried = _adamw_sharded(rows_cols(wts[n], 1), rows_cols(mom[n], 1), rows_cols(var[n], 1), a, b, f"adamw_{n}",
                                       everyone if n == REDUCED[0] else None)
        if n == REDUCED[0]:
            partials = carried[0]
        grad[n], delta[n], new_m[n], new_v[n] = [o.reshape(shp) for o in outs]
    shapes = {n: wts[n].shape[1:] for n in SMALL_ORDER}
    shapes['ssd_conv_w'] = SHARDED['ssd_conv_w'][0]
    summed = _sum_slots(partials, "small_sum")
    small_g = _unpack_small(summed, shapes)
    loss = summed.reshape(-1)[DEPTH * sum(int(np.prod(shapes[n])) for n in SMALL_PACK)]
    conv_full = small_g.pop('ssd_conv_w')
    shard_cols = _shard_shape('ssd_conv_w')[1]
    small_g['ssd_conv_w'] = lax.dynamic_slice_in_dim(conv_full, mychip * shard_cols, shard_cols, axis=2)
    shapes['ssd_conv_w'] = _shard_shape('ssd_conv_w')

    per_layer = lambda t: [{n: t[n][l] for n in SMALL_PACK} for l in range(DEPTH)]
    d, nm, nv = _adamw(_pack_small(per_layer(wts)), _pack_small(per_layer(small_g)), _pack_small(per_layer(mom)),
                       _pack_small(per_layer(var)), "adamw_small")
    sd, snm, snv = _unpack_small(d, shapes), _unpack_small(nm, shapes), _unpack_small(nv, shapes)
    for n in SMALL_PACK:
        grad[n], delta[n], new_m[n], new_v[n] = small_g[n], sd[n], snm[n], snv[n]
    return (loss, dx[None], *[grad[n] for n in WEIGHTS], *[delta[n] for n in WEIGHTS], *[new_m[n] for n in WEIGHTS],
            *[new_v[n] for n in WEIGHTS])
```

```python
import functools
import math

import numpy as np
import jax
import jax.numpy as jnp
from jax import lax
from jax.experimental import pallas as pl
from jax.experimental.pallas import tpu as pltpu

F32, BF16 = jnp.float32, jnp.bfloat16
MESH = pl.DeviceIdType.MESH

D_MODEL, DEPTH, D_FF, EPS = 1024, 4, 2816, 1e-6
GM_WIDTH, GM_GROUPS, CHUNK = 512, 4, 128
MLA_HEADS, MLA_Q_RANK, MLA_KV_RANK, MLA_NOPE, MLA_ROPE, MLA_V = 8, 384, 256, 64, 32, 64
MLA_QK = MLA_NOPE + MLA_ROPE
ROPE_THETA = 10000.0
SSD_HEADS, SSD_HEAD_DIM, SSD_GROUPS, SSD_STATE, SSD_CONV = 8, 64, 2, 128, 4
SSD_INNER = SSD_HEADS * SSD_HEAD_DIM
IN_COLS = 6312
LANES = 128
ADAM_LR, ADAM_B1, ADAM_B2, ADAM_EPS, ADAM_WD, ADAM_STEP = 0.001, 0.9, 0.999, 1e-08, 0.01, 10

C_UV, C_Z, C_XS, C_G, PW_MAIN = 0, 1024, 2048, 3072, 6144
T_BC, T_CKV, T_CQ, T_KR, T_DT, PW_TAIL = 0, 512, 768, 1152, 1280, 1536
HP = MLA_HEADS * LANES
FC = 2 * D_FF // 4

WEIGHTS = ['ffn1_norm', 'ffn1_w_in', 'ffn1_w_out', 'mix_norm', 'w_in', 'gm_v_norm', 'gm_w_s', 'gm_b_s', 'mla_q_norm',
           'mla_kv_norm', 'mla_w_uq', 'mla_w_ukv', 'mla_q_gain', 'mla_k_gain', 'ssd_conv_w', 'ssd_conv_b', 'ssd_dt_bias',
           'ssd_a_log', 'ssd_d', 'ssd_norm', 'w_branch', 'w_out', 'ffn2_norm', 'ffn2_w_in', 'ffn2_w_out']
SHARDED = {'ffn1_w_in': ((1024, 5632), 1), 'ffn1_w_out': ((2816, 1024), 0), 'w_in': ((1024, 6312), 1),
           'mla_w_uq': ((384, 768), 1), 'mla_w_ukv': ((256, 1024), 1), 'ssd_conv_w': ((4, 1024), 1),
           'w_branch': ((3, 512, 1024), 2), 'w_out': ((1024, 1024), 0), 'ffn2_w_in': ((1024, 5632), 1),
           'ffn2_w_out': ((2816, 1024), 0)}
SHARDED_ORDER = [n for n in WEIGHTS if n in SHARDED]
SMALL_ORDER = [n for n in WEIGHTS if n not in SHARDED]
REDUCED = [n for n in SHARDED_ORDER if n != 'ssd_conv_w']
N_CHIPS = 4
HALF_L = DEPTH // 2


def _shard_shape(name):
    shape, ax = SHARDED[name]
    return tuple(d // N_CHIPS if i == ax else d for i, d in enumerate(shape))


def _pick(dim, target):
    if dim <= target:
        return dim
    t = (target // LANES) * LANES
    while t >= LANES:
        if dim % t == 0:
            return t
        t -= LANES
    return dim


def _sigmoid(x):
    return 1.0 / (1.0 + jnp.exp(-x))


def _params(*sem):
    return pltpu.CompilerParams(dimension_semantics=sem, vmem_limit_bytes=56 * 1024 * 1024)


def _matmul(a, b, *, ta=False, tb=False, out_dtype=F32, scale=1.0, res=None, name, side=None):
    if ta:
        k_dim, m_dim = a.shape
    else:
        m_dim, k_dim = a.shape
    if tb:
        n_dim, k2 = b.shape
    else:
        k2, n_dim = b.shape
    assert k_dim == k2, (a.shape, b.shape, ta, tb)
    tm, tn, tk = _pick(m_dim, 1024), _pick(n_dim, 1024), _pick(k_dim, 1024)
    nk = k_dim // tk
    dn = (((0 if ta else 1,), (1 if tb else 0,)), ((), ()))

    def body(*refs):
        if res is not None:
            a_ref, b_ref, r_ref, o_ref, acc = refs
        else:
            a_ref, b_ref, o_ref, acc = refs
        k = pl.program_id(2)

        @pl.when(k == 0)
        def _():
            acc[...] = jnp.zeros_like(acc)

        acc[...] += lax.dot_general(a_ref[...].astype(BF16), b_ref[...].astype(BF16), dn, preferred_element_type=F32)

        @pl.when(k == nk - 1)
        def _():
            r = acc[...]
            if scale != 1.0:
                r = r * scale
            if res is not None:
                r = r + r_ref[...]
            o_ref[...] = r.astype(out_dtype)

    a_spec = pl.BlockSpec((tk, tm), lambda j, i, k: (k, i)) if ta else pl.BlockSpec((tm, tk), lambda j, i, k: (i, k))
    b_spec = pl.BlockSpec((tn, tk), lambda j, i, k: (j, k)) if tb else pl.BlockSpec((tk, tn), lambda j, i, k: (k, j))
    in_specs = [a_spec, b_spec]
    args = [a, b]
    if res is not None:
        in_specs.append(pl.BlockSpec((tm, tn), lambda j, i, k: (i, j)))
        args.append(res)
    (out,), carried = _call(
        body, name=name, grid=(n_dim // tn, m_dim // tm, nk), in_specs=in_specs,
        out_specs=[pl.BlockSpec((tm, tn), lambda j, i, k: (i, j))],
        out_shape=[jax.ShapeDtypeStruct((m_dim, n_dim), out_dtype)],
        scratch_shapes=[pltpu.VMEM((tm, tn), F32)], args=args, semantics=("parallel", "parallel", "arbitrary"), side=side)
    return out if side is None else (out, carried)


def _rmsnorm_fwd(x, gain, name):
    s, d = x.shape
    tm = _pick(s, 512)

    def body(x_ref, g_ref, o_ref):
        xv = x_ref[...]
        r = lax.rsqrt(jnp.mean(xv * xv, axis=-1, keepdims=True) + EPS)
        o_ref[...] = (xv * r * g_ref[...]).astype(BF16)

    return pl.pallas_call(
        body, name=name, grid=(s // tm,),
        in_specs=[pl.BlockSpec((tm, d), lambda i: (i, 0)), pl.BlockSpec((1, d), lambda i: (0, 0))],
        out_specs=pl.BlockSpec((tm, d), lambda i: (i, 0)),
        out_shape=jax.ShapeDtypeStruct((s, d), BF16), compiler_params=_params("parallel"))(x, gain)


def _rmsnorm_bwd(x, gain, dh, dres, name):
    s, d = x.shape
    tm = _pick(s, 512)

    def body(x_ref, g_ref, dh_ref, dr_ref, dx_ref, dg_ref):
        @pl.when(pl.program_id(0) == 0)
        def _():
            dg_ref[...] = jnp.zeros_like(dg_ref)

        xv, dhv = x_ref[...], dh_ref[...]
        r = lax.rsqrt(jnp.mean(xv * xv, axis=-1, keepdims=True) + EPS)
        u = dhv * g_ref[...]
        dx_ref[...] = dr_ref[...] + r * u - xv * (r * r * r) * jnp.mean(xv * u, axis=-1, keepdims=True)
        dg_ref[...] += jnp.sum(dhv * xv * r, axis=0, keepdims=True)

    row = pl.BlockSpec((tm, d), lambda i: (i, 0))
    vec = pl.BlockSpec((1, d), lambda i: (0, 0))
    return pl.pallas_call(
        body, name=name, grid=(s // tm,), in_specs=[row, vec, row, row], out_specs=[row, vec],
        out_shape=[jax.ShapeDtypeStruct((s, d), F32), jax.ShapeDtypeStruct((1, d), F32)],
        compiler_params=_params("arbitrary"))(x, gain, dh, dres)


_NT = (((1,), (1,)), ((), ()))
_TN = (((0,), (0,)), ((), ()))


def _resident(shape):
    return pl.BlockSpec(shape, lambda *_: tuple(0 for _ in shape), pipeline_mode=pl.Buffered(1))


def _ffn_in(x, gain, w4, name, side=None):
    s, d = x.shape
    tm = _pick(s, 512)

    def body(x_ref, g_ref, w_ref, h_ref, gate_ref, up_ref, act_ref):
        xv = x_ref[...]
        r = lax.rsqrt(jnp.mean(xv * xv, axis=-1, keepdims=True) + EPS)
        h = (xv * r * g_ref[...]).astype(BF16)
        h_ref[...] = h
        for j in range(2):
            g16 = jnp.dot(h, w_ref[j], preferred_element_type=F32).astype(BF16)
            u16 = jnp.dot(h, w_ref[j + 2], preferred_element_type=F32).astype(BF16)
            gate_ref[j] = g16
            up_ref[j] = u16
            gf, uf = g16.astype(F32), u16.astype(F32)
            act_ref[j] = (gf * _sigmoid(gf) * uf).astype(BF16)

    half = pl.BlockSpec((2, tm, FC), lambda i: (0, i, 0))
    return _call(
        body, name=name, grid=(s // tm,),
        in_specs=[pl.BlockSpec((tm, d), lambda i: (i, 0)), pl.BlockSpec((1, d), lambda i: (0, 0)), _resident((4, d, FC))],
        out_specs=[pl.BlockSpec((tm, d), lambda i: (i, 0)), half, half, half],
        out_shape=[jax.ShapeDtypeStruct((s, d), BF16)] + [jax.ShapeDtypeStruct((2, s, FC), BF16)] * 3,
        scratch_shapes=[], args=(x, gain, w4), semantics=("parallel",), side=side)


def _ffn_out(act, w_out, x, name, side=None):
    s, d = x.shape
    tm = _pick(s, 512)

    def body(a_ref, w_ref, x_ref, o_ref):
        acc = jnp.dot(a_ref[0], w_ref[0:FC, :], preferred_element_type=F32)
        acc = acc + jnp.dot(a_ref[1], w_ref[FC:2 * FC, :], preferred_element_type=F32)
        o_ref[...] = x_ref[...] + 0.5 * acc

    row = pl.BlockSpec((tm, d), lambda i: (i, 0))
    (out,), carried = _call(
        body, name=name, grid=(s // tm,),
        in_specs=[pl.BlockSpec((2, tm, FC), lambda i: (0, i, 0)), _resident((2 * FC, d)), row], out_specs=[row],
        out_shape=[jax.ShapeDtypeStruct((s, d), F32)], scratch_shapes=[], args=(act, w_out, x), semantics=("parallel",),
        side=side)
    return out, carried


def _ffn_dact(dy, w_out, gate, up, name, side=None):
    s, d = dy.shape
    tm = _pick(s, 512)

    def body(dy_ref, w_ref, g_ref, u_ref, o_ref):
        dy16 = dy_ref[...].astype(BF16)
        for j in range(2):
            dact = 0.5 * lax.dot_general(dy16, w_ref[j * FC:(j + 1) * FC, :], _NT, preferred_element_type=F32)
            g, u = g_ref[j].astype(F32), u_ref[j].astype(F32)
            sg = _sigmoid(g)
            o_ref[j] = (dact * u * (sg * (1.0 + g * (1.0 - sg)))).astype(BF16)
            o_ref[j + 2] = (dact * g * sg).astype(BF16)

    half = pl.BlockSpec((2, tm, FC), lambda i: (0, i, 0))
    (out,), carried = _call(
        body, name=name, grid=(s // tm,),
        in_specs=[pl.BlockSpec((tm, d), lambda i: (i, 0)), _resident((2 * FC, d)), half, half],
        out_specs=[pl.BlockSpec((4, tm, FC), lambda i: (0, i, 0))],
        out_shape=[jax.ShapeDtypeStruct((4, s, FC), BF16)], scratch_shapes=[], args=(dy, w_out, gate, up),
        semantics=("parallel",), side=side)
    return out, carried


def _ffn_dwout(act, dy, name, side=None):
    s, d = dy.shape
    tk = _pick(s, 1024)
    nk = s // tk

    def body(a_ref, dy_ref, o_ref):
        k = pl.program_id(1)

        @pl.when(k == 0)
        def _():
            o_ref[...] = jnp.zeros_like(o_ref)

        o_ref[...] += lax.dot_general(a_ref[...], dy_ref[...].astype(BF16), _TN, preferred_element_type=F32)

        @pl.when(k == nk - 1)
        def _():
            o_ref[...] = 0.5 * o_ref[...]

    (out,), carried = _call(
        body, name=name, grid=(2, nk),
        in_specs=[pl.BlockSpec((None, tk, FC), lambda j, k: (j, k, 0)), pl.BlockSpec((tk, d), lambda j, k: (k, 0))],
        out_specs=[pl.BlockSpec((FC, d), lambda j, k: (j, 0))], out_shape=[jax.ShapeDtypeStruct((2 * FC, d), F32)],
        scratch_shapes=[], args=(act, dy), semantics=("parallel", "arbitrary"), side=side)
    return out, carried


def _ffn_dwin(h, da, name, side=None):
    s, d = h.shape
    tk = _pick(s, 1024)

    def body(h_ref, da_ref, o_ref):
        @pl.when(pl.program_id(1) == 0)
        def _():
            o_ref[...] = jnp.zeros_like(o_ref)

        o_ref[...] += lax.dot_general(h_ref[...], da_ref[...], _TN, preferred_element_type=F32)

    (out,), carried = _call(
        body, name=name, grid=(4, s // tk),
        in_specs=[pl.BlockSpec((tk, d), lambda j, k: (k, 0)), pl.BlockSpec((None, tk, FC), lambda j, k: (j, k, 0))],
        out_specs=[pl.BlockSpec((None, d, FC), lambda j, k: (j, 0, 0))], out_shape=[jax.ShapeDtypeStruct((4, d, FC), F32)],
        scratch_shapes=[], args=(h, da), semantics=("parallel", "arbitrary"), side=side)
    return out, carried


def _ffn_dx(da, w4, x, gain, dy, name, side=None):
    s, d = x.shape
    tm = _pick(s, 512)

    def body(da_ref, w_ref, x_ref, g_ref, dy_ref, dx_ref, dg_ref):
        @pl.when(pl.program_id(0) == 0)
        def _():
            dg_ref[...] = jnp.zeros_like(dg_ref)

        dh = jnp.zeros((tm, d), F32)
        for j in range(4):
            dh = dh + lax.dot_general(da_ref[j], w_ref[j], _NT, preferred_element_type=F32)
        xv = x_ref[...]
        r = lax.rsqrt(jnp.mean(xv * xv, axis=-1, keepdims=True) + EPS)
        u = dh * g_ref[...]
        dx_ref[...] = dy_ref[...] + r * u - xv * (r * r * r) * jnp.mean(xv * u, axis=-1, keepdims=True)
        dg_ref[...] += jnp.sum(dh * xv * r, axis=0, keepdims=True)

    row = pl.BlockSpec((tm, d), lambda i: (i, 0))
    vec = pl.BlockSpec((1, d), lambda i: (0, 0))
    return _call(
        body, name=name, grid=(s // tm,),
        in_specs=[pl.BlockSpec((4, tm, FC), lambda i: (0, i, 0)), _resident((4, d, FC)), row, vec, row],
        out_specs=[row, vec], out_shape=[jax.ShapeDtypeStruct((s, d), F32), jax.ShapeDtypeStruct((1, d), F32)],
        scratch_shapes=[], args=(da, w4, x, gain, dy), semantics=("arbitrary",), side=side)


_INV_SQRT2 = 0.7071067811865476
_INV_SQRT2PI = 0.3989422804014327


def _gelu(x):
    return 0.5 * x * (1.0 + lax.erf(x * _INV_SQRT2))


def _gelu_grad(x):
    return 0.5 * (1.0 + lax.erf(x * _INV_SQRT2)) + x * jnp.exp(-0.5 * x * x) * _INV_SQRT2PI


def _tril_mask():
    r = lax.broadcasted_iota(jnp.int32, (CHUNK, CHUNK), 0)
    c = lax.broadcasted_iota(jnp.int32, (CHUNK, CHUNK), 1)
    return r >= c


def _gmlp_fwd(p, v_gain, w_s, b_full, name):
    s = p.shape[0]
    tm = _pick(s, 512)
    nch = tm // CHUNK

    def body(uv_ref, g_ref, w_ref, b_ref, o_ref):
        gel = _gelu(uv_ref[...].astype(F32))
        u, v = gel[:, :GM_WIDTH], gel[:, GM_WIDTH:]
        r = lax.rsqrt(jnp.mean(v * v, axis=-1, keepdims=True) + EPS)
        vn = (v * r * g_ref[...]).astype(BF16)
        mask = _tril_mask()
        for g in range(GM_GROUPS):
            wm = jnp.where(mask, w_ref[g], 0.0).astype(BF16)
            for c in range(nch):
                rs, cs = slice(c * CHUNK, (c + 1) * CHUNK), slice(g * LANES, (g + 1) * LANES)
                sp = jnp.dot(wm, vn[rs, cs], preferred_element_type=F32) + b_ref[g]
                o_ref[rs, cs] = (u[rs, cs] * sp).astype(BF16)

    full3 = pl.BlockSpec((GM_GROUPS, CHUNK, CHUNK), lambda i: (0, 0, 0))
    return pl.pallas_call(
        body, name=name, grid=(s // tm,),
        in_specs=[pl.BlockSpec((tm, 2 * GM_WIDTH), lambda i: (i, C_UV // (2 * GM_WIDTH))),
                  pl.BlockSpec((1, GM_WIDTH), lambda i: (0, 0)), full3, full3],
        out_specs=pl.BlockSpec((tm, GM_WIDTH), lambda i: (i, 0)),
        out_shape=jax.ShapeDtypeStruct((s, GM_WIDTH), BF16), compiler_params=_params("parallel"))(p, v_gain, w_s, b_full)


def _gmlp_bwd(p, v_gain, w_s, b_full, dy, name):
    s = p.shape[0]
    tm = _pick(s, 512)
    nch = tm // CHUNK
    nsteps = s // tm

    def body(uv_ref, g_ref, w_ref, b_ref, dy_ref, duv_ref, dg_ref, dw_ref, db_ref, dvn_s, dbacc):
        step = pl.program_id(0)

        @pl.when(step == 0)
        def _():
            dg_ref[...] = jnp.zeros_like(dg_ref)
            dw_ref[...] = jnp.zeros_like(dw_ref)
            dbacc[...] = jnp.zeros_like(dbacc)

        uv = uv_ref[...].astype(F32)
        gel = _gelu(uv)
        u, v = gel[:, :GM_WIDTH], gel[:, GM_WIDTH:]
        r = lax.rsqrt(jnp.mean(v * v, axis=-1, keepdims=True) + EPS)
        gain = g_ref[...]
        vn32 = v * r * gain
        vn = vn32.astype(BF16)
        dy = dy_ref[...].astype(F32)
        mask = _tril_mask()
        for g in range(GM_GROUPS):
            wm = jnp.where(mask, w_ref[g], 0.0).astype(BF16)
            dwg = jnp.zeros((CHUNK, CHUNK), F32)
            dbg = jnp.zeros((CHUNK, LANES), F32)
            for c in range(nch):
                rs, cs = slice(c * CHUNK, (c + 1) * CHUNK), slice(g * LANES, (g + 1) * LANES)
                sp = jnp.dot(wm, vn[rs, cs], preferred_element_type=F32) + b_ref[g]
                dyc = dy[rs, cs]
                dsp = dyc * u[rs, cs]
                dsp16 = dsp.astype(BF16)
                duv_ref[rs, cs] = (dyc * sp * _gelu_grad(uv[rs, cs])).astype(BF16)
                dvn_s[rs, cs] = lax.dot_general(wm, dsp16, (((0,), (0,)), ((), ())), preferred_element_type=F32)
                dwg = dwg + lax.dot_general(dsp16, vn[rs, cs], (((1,), (1,)), ((), ())), preferred_element_type=F32)
                dbg = dbg + dsp
            dw_ref[g] += jnp.where(mask, dwg, 0.0)
            dbacc[:, g * LANES:(g + 1) * LANES] += dbg
        dvn = dvn_s[...]
        uu = dvn * gain
        dv = r * uu - v * (r * r * r) * jnp.mean(v * uu, axis=-1, keepdims=True)
        duv_ref[:, GM_WIDTH:] = (dv * _gelu_grad(uv[:, GM_WIDTH:])).astype(BF16)
        dg_ref[...] += jnp.sum(dvn * v * r, axis=0, keepdims=True)

        @pl.when(step == nsteps - 1)
        def _():
            for g in range(GM_GROUPS):
                db_ref[:, g:g + 1] = jnp.sum(dbacc[:, g * LANES:(g + 1) * LANES], axis=1, keepdims=True)

    full3 = pl.BlockSpec((GM_GROUPS, CHUNK, CHUNK), lambda i: (0, 0, 0))
    return pl.pallas_call(
        body, name=name, grid=(nsteps,),
        in_specs=[pl.BlockSpec((tm, 2 * GM_WIDTH), lambda i: (i, C_UV // (2 * GM_WIDTH))),
                  pl.BlockSpec((1, GM_WIDTH), lambda i: (0, 0)), full3, full3,
                  pl.BlockSpec((tm, GM_WIDTH), lambda i: (i, 0))],
        out_specs=[pl.BlockSpec((tm, 2 * GM_WIDTH), lambda i: (i, 0)), pl.BlockSpec((1, GM_WIDTH), lambda i: (0, 0)),
                   full3, pl.BlockSpec((CHUNK, GM_GROUPS), lambda i: (0, 0))],
        out_shape=[jax.ShapeDtypeStruct((s, 2 * GM_WIDTH), BF16), jax.ShapeDtypeStruct((1, GM_WIDTH), F32),
                   jax.ShapeDtypeStruct((GM_GROUPS, CHUNK, CHUNK), F32), jax.ShapeDtypeStruct((CHUNK, GM_GROUPS), F32)],
        scratch_shapes=[pltpu.VMEM((tm, GM_WIDTH), F32), pltpu.VMEM((CHUNK, GM_WIDTH), F32)],
        compiler_params=_params("arbitrary"))(p, v_gain, w_s, b_full, dy)


def _rope(x, ct, s1, s2):
    return x * ct + pltpu.roll(x, LANES - MLA_ROPE // 2, 1) * s1 + pltpu.roll(x, MLA_ROPE // 2, 1) * s2


def _rope_bwd(d, ct, s1, s2):
    return d * ct + pltpu.roll(d * s1, MLA_ROPE // 2, 1) + pltpu.roll(d * s2, LANES - MLA_ROPE // 2, 1)


def _head_norm(x, gain):
    r = lax.rsqrt(jnp.sum(x * x, axis=-1, keepdims=True) * (1.0 / MLA_QK) + EPS)
    return x * r * gain, r


def _head_norm_bwd(x, r, gain, d):
    u = d * gain
    return r * u - x * (r * r * r) * (jnp.sum(x * u, axis=-1, keepdims=True) * (1.0 / MLA_QK))


def _mla_specs(tm):
    cq = pl.BlockSpec((tm, MLA_Q_RANK), lambda i: (i, T_CQ // MLA_Q_RANK))
    ckv = pl.BlockSpec((tm, MLA_KV_RANK), lambda i: (i, T_CKV // MLA_KV_RANK))
    kr = pl.BlockSpec((tm, LANES), lambda i: (i, T_KR // LANES))
    tab = pl.BlockSpec((tm, LANES), lambda i: (i, 0))
    return cq, ckv, kr, tab


def _const(shape):
    return pl.BlockSpec(shape, lambda i: tuple(0 for _ in shape))


def _mla_pre_fwd(p, tabs, qn_g, kvn_g, wuq, wkv, gq, gk, name):
    s = p.shape[0]
    tm = _pick(s, 256)
    ct, s1, s2 = tabs

    def body(cq_ref, ckv_ref, kr_ref, ct_ref, s1_ref, s2_ref, qg_ref, kvg_ref, wuq_ref, wkv_ref, gq_ref, gk_ref,
             q_ref, k_ref, v_ref):
        cq, ckv, kr = cq_ref[...], ckv_ref[...], kr_ref[...]
        ctv, s1v, s2v = ct_ref[...], s1_ref[...], s2_ref[...]
        rq = lax.rsqrt(jnp.mean(cq * cq, axis=-1, keepdims=True) + EPS)
        q = jnp.dot((cq * rq * qg_ref[...]).astype(BF16), wuq_ref[...], preferred_element_type=F32)
        rk = lax.rsqrt(jnp.mean(ckv * ckv, axis=-1, keepdims=True) + EPS)
        kv = jnp.dot((ckv * rk * kvg_ref[...]).astype(BF16), wkv_ref[...], preferred_element_type=F32)
        v_ref[...] = kv[:, HP:].astype(BF16)
        for h in range(MLA_HEADS):
            hs = slice(h * LANES, (h + 1) * LANES)
            qh, _ = _head_norm(q[:, hs], gq_ref[...])
            q_ref[:, hs] = (_rope(qh, ctv, s1v, s2v) * _Q_SCALE).astype(BF16)
            kh, _ = _head_norm(kv[:, hs] + kr, gk_ref[...])
            k_ref[:, hs] = _rope(kh, ctv, s1v, s2v).astype(BF16)

    cq_s, ckv_s, kr_s, tab_s = _mla_specs(tm)
    out = pl.BlockSpec((tm, HP), lambda i: (i, 0))
    return pl.pallas_call(
        body, name=name, grid=(s // tm,),
        in_specs=[cq_s, ckv_s, kr_s, tab_s, tab_s, tab_s, _const((1, MLA_Q_RANK)), _const((1, MLA_KV_RANK)),
                  _const((MLA_Q_RANK, HP)), _const((MLA_KV_RANK, 2 * HP)), _const((1, LANES)), _const((1, LANES))],
        out_specs=[out, out, out], out_shape=[jax.ShapeDtypeStruct((s, HP), BF16)] * 3,
        compiler_params=_params("parallel"))(p, p, p, ct, s1, s2, qn_g, kvn_g, wuq, wkv, gq, gk)


def _mla_pre_bwd(p, tabs, qn_g, kvn_g, wuq, wkv, gq, gk, dq, dk, dv, name, side=None):
    s = p.shape[0]
    tm = _pick(s, 256)
    ct, s1, s2 = tabs

    def body(cq_ref, ckv_ref, kr_ref, ct_ref, s1_ref, s2_ref, qg_ref, kvg_ref, wuq_ref, wkv_ref, gq_ref, gk_ref,
             dq_ref, dk_ref, dv_ref, dcq_ref, dckv_ref, dkr_ref, dwuq_ref, dwkv_ref, dqg_ref, dkvg_ref, dgq_ref, dgk_ref,
             dqp, dkvp):
        @pl.when(pl.program_id(0) == 0)
        def _():
            for ref in (dwuq_ref, dwkv_ref, dqg_ref, dkvg_ref, dgq_ref, dgk_ref):
                ref[...] = jnp.zeros_like(ref)

        cq, ckv, kr = cq_ref[...], ckv_ref[...], kr_ref[...]
        ctv, s1v, s2v = ct_ref[...], s1_ref[...], s2_ref[...]
        rq = lax.rsqrt(jnp.mean(cq * cq, axis=-1, keepdims=True) + EPS)
        qn = (cq * rq * qg_ref[...]).astype(BF16)
        q = jnp.dot(qn, wuq_ref[...], preferred_element_type=F32)
        rk = lax.rsqrt(jnp.mean(ckv * ckv, axis=-1, keepdims=True) + EPS)
        kvn = (ckv * rk * kvg_ref[...]).astype(BF16)
        kv = jnp.dot(kvn, wkv_ref[...], preferred_element_type=F32)
        gqv, gkv = gq_ref[...], gk_ref[...]
        dgq = jnp.zeros((1, LANES), F32)
        dgk = jnp.zeros((1, LANES), F32)
        dkr = jnp.zeros((tm, LANES), F32)
        for h in range(MLA_HEADS):
            hs = slice(h * LANES, (h + 1) * LANES)
            xq = q[:, hs]
            _, r = _head_norm(xq, gqv)
            d = _rope_bwd(dq_ref[:, hs].astype(F32), ctv, s1v, s2v)
            dgq = dgq + jnp.sum(d * xq * r, axis=0, keepdims=True)
            dqp[:, hs] = _head_norm_bwd(xq, r, gqv, d)
            xk = kv[:, hs] + kr
            _, r = _head_norm(xk, gkv)
            d = _rope_bwd(dk_ref[:, hs].astype(F32), ctv, s1v, s2v)
            dgk = dgk + jnp.sum(d * xk * r, axis=0, keepdims=True)
            dxk = _head_norm_bwd(xk, r, gkv, d)
            dkvp[:, hs] = dxk
            dkr = dkr + dxk
        dkvp[:, HP:] = dv_ref[...].astype(F32)
        dgq_ref[...] += dgq
        dgk_ref[...] += dgk
        dkr_ref[...] = dkr.astype(BF16)
        tn = (((0,), (0,)), ((), ()))
        nt = (((1,), (1,)), ((), ()))
        dq16 = dqp[...].astype(BF16)
        dwuq_ref[...] += lax.dot_general(qn, dq16, tn, preferred_element_type=F32)
        dqn = lax.dot_general(dq16, wuq_ref[...], nt, preferred_element_type=F32)
        dqg_ref[...] += jnp.sum(dqn * cq * rq, axis=0, keepdims=True)
        u = dqn * qg_ref[...]
        dcq_ref[...] = (rq * u - cq * (rq * rq * rq) * jnp.mean(cq * u, axis=-1, keepdims=True)).astype(BF16)
        dkv16 = dkvp[...].astype(BF16)
        dwkv_ref[...] += lax.dot_general(kvn, dkv16, tn, preferred_element_type=F32)
        dkvn = lax.dot_general(dkv16, wkv_ref[...], nt, preferred_element_type=F32)
        dkvg_ref[...] += jnp.sum(dkvn * ckv * rk, axis=0, keepdims=True)
        u = dkvn * kvg_ref[...]
        dckv_ref[...] = (rk * u - ckv * (rk * rk * rk) * jnp.mean(ckv * u, axis=-1, keepdims=True)).astype(BF16)

    cq_s, ckv_s, kr_s, tab_s = _mla_specs(tm)
    hd = pl.BlockSpec((tm, HP), lambda i: (i, 0))
    return _call(
        body, name=name, grid=(s // tm,),
        in_specs=[cq_s, ckv_s, kr_s, tab_s, tab_s, tab_s, _const((1, MLA_Q_RANK)), _const((1, MLA_KV_RANK)),
                  _const((MLA_Q_RANK, HP)), _const((MLA_KV_RANK, 2 * HP)), _const((1, LANES)), _const((1, LANES)),
                  hd, hd, hd],
        out_specs=[pl.BlockSpec((tm, MLA_Q_RANK), lambda i: (i, 0)), pl.BlockSpec((tm, MLA_KV_RANK), lambda i: (i, 0)),
                   pl.BlockSpec((tm, LANES), lambda i: (i, 0)), _const((MLA_Q_RANK, HP)), _const((MLA_KV_RANK, 2 * HP)),
                   _const((1, MLA_Q_RANK)), _const((1, MLA_KV_RANK)), _const((1, LANES)), _const((1, LANES))],
        out_shape=[jax.ShapeDtypeStruct((s, MLA_Q_RANK), BF16), jax.ShapeDtypeStruct((s, MLA_KV_RANK), BF16),
                   jax.ShapeDtypeStruct((s, LANES), BF16), jax.ShapeDtypeStruct((MLA_Q_RANK, HP), F32),
                   jax.ShapeDtypeStruct((MLA_KV_RANK, 2 * HP), F32), jax.ShapeDtypeStruct((1, MLA_Q_RANK), F32),
                   jax.ShapeDtypeStruct((1, MLA_KV_RANK), F32), jax.ShapeDtypeStruct((1, LANES), F32),
                   jax.ShapeDtypeStruct((1, LANES), F32)],
        scratch_shapes=[pltpu.VMEM((tm, HP), F32), pltpu.VMEM((tm, 2 * HP), F32)],
        args=(p, p, p, ct, s1, s2, qn_g, kvn_g, wuq, wkv, gq, gk, dq, dk, dv), semantics=("arbitrary",), side=side)


_ATT_SCALE = MLA_QK ** -0.5
_LOG2E = 1.4426950408889634
_Q_SCALE = _ATT_SCALE * _LOG2E
ATT_BLOCK = 1024
_NEG = -1e30
_NT = (((1,), (1,)), ((), ()))
_TN = (((0,), (0,)), ((), ()))


def _tri_rows(step, n):
    i = step * 0
    for m in range(1, n):
        i = i + (step >= m * (m + 1) // 2).astype(jnp.int32)
    return i, step - i * (i + 1) // 2


def _tri_cols(step, n):
    j = step * 0
    for m in range(1, n):
        j = j + (step >= m * n - m * (m - 1) // 2).astype(jnp.int32)
    return j, j + step - (j * n - j * (j - 1) // 2)


def _diag_mask(t):
    return lax.broadcasted_iota(jnp.int32, (t, t), 0) <= lax.broadcasted_iota(jnp.int32, (t, t), 1)


def _attn_fwd(q, k, v, name, side=None):
    s = q.shape[0]
    t = _pick(s, ATT_BLOCK)
    n = s // t

    def body(q_ref, k_ref, v_ref, o_ref, lse_ref, m_s, l_s, acc):
        i, j = _tri_rows(pl.program_id(1), n)

        @pl.when(j == 0)
        def _():
            m_s[...] = jnp.full_like(m_s, _NEG)
            l_s[...] = jnp.zeros_like(l_s)
            acc[...] = jnp.zeros_like(acc)

        def step(diagonal):
            sc = lax.dot_general(k_ref[...], q_ref[...], _NT, preferred_element_type=F32)
            if diagonal:
                sc = jnp.where(_diag_mask(t), sc, _NEG)
            m_new = jnp.maximum(m_s[...], jnp.max(sc, axis=0, keepdims=True))
            alpha = jnp.exp2(m_s[...] - m_new)
            pr = jnp.exp2(sc - m_new)
            l_s[...] = alpha * l_s[...] + jnp.sum(pr, axis=0, keepdims=True)
            acc[...] = alpha * acc[...] + lax.dot_general(v_ref[...], pr.astype(BF16), _TN, preferred_element_type=F32)
            m_s[...] = m_new

        @pl.when(j < i)
        def _():
            step(False)

        @pl.when(j == i)
        def _():
            step(True)
            o_ref[...] = (acc[...] / l_s[...]).T.astype(BF16)
            lse_ref[...] = m_s[...] + jnp.log2(l_s[...])

    qs = pl.BlockSpec((t, LANES), lambda h, p: (_tri_rows(p, n)[0], h))
    ks = pl.BlockSpec((t, LANES), lambda h, p: (_tri_rows(p, n)[1], h))
    return _call(
        body, name=name, grid=(MLA_HEADS, n * (n + 1) // 2), in_specs=[qs, ks, ks],
        out_specs=[qs, pl.BlockSpec((None, 1, t), lambda h, p: (h, 0, _tri_rows(p, n)[0]))],
        out_shape=[jax.ShapeDtypeStruct((s, HP), BF16), jax.ShapeDtypeStruct((MLA_HEADS, 1, s), F32)],
        scratch_shapes=[pltpu.VMEM((1, t), F32), pltpu.VMEM((1, t), F32), pltpu.VMEM((LANES, t), F32)],
        args=(q, k, v), semantics=("parallel", "arbitrary"), side=side)


def _attn_bwd_dq(q, k, v, o, lse, do, name, side=None):
    s = q.shape[0]
    t = _pick(s, ATT_BLOCK)
    n = s // t

    def body(q_ref, k_ref, v_ref, o_ref, lse_ref, do_ref, dq_ref, dl_ref, acc, dl_s):
        i, j = _tri_rows(pl.program_id(1), n)

        @pl.when(j == 0)
        def _():
            acc[...] = jnp.zeros_like(acc)
            dl_s[...] = jnp.sum((do_ref[...].astype(F32) * o_ref[...].astype(F32)).T, axis=0, keepdims=True)

        def step(diagonal):
            sc = lax.dot_general(k_ref[...], q_ref[...], _NT, preferred_element_type=F32)
            if diagonal:
                sc = jnp.where(_diag_mask(t), sc, _NEG)
            pr = jnp.exp2(sc - lse_ref[...])
            dp = lax.dot_general(v_ref[...], do_ref[...].astype(BF16), _NT, preferred_element_type=F32)
            ds = (pr * (dp - dl_s[...])).astype(BF16)
            acc[...] += lax.dot_general(k_ref[...], ds, _TN, preferred_element_type=F32)

        @pl.when(j < i)
        def _():
            step(False)

        @pl.when(j == i)
        def _():
            step(True)
            dq_ref[...] = (acc[...] * _ATT_SCALE).T.astype(BF16)
            dl_ref[...] = dl_s[...]

    qs = pl.BlockSpec((t, LANES), lambda h, p: (_tri_rows(p, n)[0], h))
    ks = pl.BlockSpec((t, LANES), lambda h, p: (_tri_rows(p, n)[1], h))
    ls = pl.BlockSpec((None, 1, t), lambda h, p: (h, 0, _tri_rows(p, n)[0]))
    return _call(
        body, name=name, grid=(MLA_HEADS, n * (n + 1) // 2), in_specs=[qs, ks, ks, qs, ls, qs], out_specs=[qs, ls],
        out_shape=[jax.ShapeDtypeStruct((s, HP), BF16), jax.ShapeDtypeStruct((MLA_HEADS, 1, s), F32)],
        scratch_shapes=[pltpu.VMEM((LANES, t), F32), pltpu.VMEM((1, t), F32)],
        args=(q, k, v, o, lse, do), semantics=("parallel", "arbitrary"), side=side)


def _attn_bwd_dkv(q, k, v, lse, delta, do, name, side=None):
    s = q.shape[0]
    t = _pick(s, ATT_BLOCK)
    n = s // t

    def body(q_ref, k_ref, v_ref, lse_ref, dl_ref, do_ref, dk_ref, dv_ref, dk_acc, dv_acc):
        j, i = _tri_cols(pl.program_id(1), n)

        def step(diagonal):
            sc = lax.dot_general(k_ref[...], q_ref[...], _NT, preferred_element_type=F32)
            if diagonal:
                sc = jnp.where(_diag_mask(t), sc, _NEG)
            pr = jnp.exp2(sc - lse_ref[...])
            do16 = do_ref[...].astype(BF16)
            dv_acc[...] += jnp.dot(pr.astype(BF16), do16, preferred_element_type=F32)
            dp = lax.dot_general(v_ref[...], do16, _NT, preferred_element_type=F32)
            ds = (pr * (dp - dl_ref[...])).astype(BF16)
            dk_acc[...] += jnp.dot(ds, q_ref[...], preferred_element_type=F32)

        @pl.when(i == j)
        def _():
            dk_acc[...] = jnp.zeros_like(dk_acc)
            dv_acc[...] = jnp.zeros_like(dv_acc)
            step(True)

        @pl.when(i > j)
        def _():
            step(False)

        @pl.when(i == n - 1)
        def _():
            dk_ref[...] = (dk_acc[...] * (1.0 / _LOG2E)).astype(BF16)
            dv_ref[...] = dv_acc[...].astype(BF16)

    qs = pl.BlockSpec((t, LANES), lambda h, p: (_tri_cols(p, n)[1], h))
    ks = pl.BlockSpec((t, LANES), lambda h, p: (_tri_cols(p, n)[0], h))
    ls = pl.BlockSpec((None, 1, t), lambda h, p: (h, 0, _tri_cols(p, n)[1]))
    return _call(
        body, name=name, grid=(MLA_HEADS, n * (n + 1) // 2), in_specs=[qs, ks, ks, ls, ls, qs], out_specs=[ks, ks],
        out_shape=[jax.ShapeDtypeStruct((s, HP), BF16)] * 2,
        scratch_shapes=[pltpu.VMEM((t, LANES), F32), pltpu.VMEM((t, LANES), F32)],
        args=(q, k, v, lse, delta, do), semantics=("parallel", "arbitrary"), side=side)


XBC = HP + 2 * SSD_GROUPS * SSD_STATE
BCW = 2 * SSD_GROUPS * SSD_STATE


def _conv_fwd(p, col0, width, conv_w, conv_b, name):
    s = p.shape[0]
    c0, nblk = col0 // LANES, width // LANES

    def body(x_ref, w_ref, b_ref, o_ref, pad):
        pad[0:8, :] = jnp.zeros((8, LANES), F32)
        pad[8:s + 8, :] = x_ref[...].astype(F32)
        acc = jnp.broadcast_to(b_ref[...], (s, LANES))
        for t in range(SSD_CONV):
            acc = acc + pad[pl.ds(8 - (SSD_CONV - 1) + t, s), :] * w_ref[t:t + 1, :]
        o_ref[...] = acc * _sigmoid(acc)

    return pl.pallas_call(
        body, name=name, grid=(nblk,),
        in_specs=[pl.BlockSpec((s, LANES), lambda j: (0, c0 + j)), pl.BlockSpec((SSD_CONV, LANES), lambda j: (0, j)),
                  pl.BlockSpec((1, LANES), lambda j: (0, j))],
        out_specs=pl.BlockSpec((s, LANES), lambda j: (0, j)), out_shape=jax.ShapeDtypeStruct((s, width), F32),
        scratch_shapes=[pltpu.VMEM((s + 8, LANES), F32)], compiler_params=_params("parallel"))(p, conv_w, conv_b)


def _conv_bwd(p, col0, width, conv_w, conv_b, dact, name):
    s = p.shape[0]
    c0, nblk = col0 // LANES, width // LANES

    def body(x_ref, w_ref, b_ref, d_ref, dx_ref, dw_ref, db_ref, pad, padd):
        pad[0:8, :] = jnp.zeros((8, LANES), F32)
        pad[8:s + 8, :] = x_ref[...].astype(F32)
        acc = jnp.broadcast_to(b_ref[...], (s, LANES))
        for t in range(SSD_CONV):
            acc = acc + pad[pl.ds(8 - (SSD_CONV - 1) + t, s), :] * w_ref[t:t + 1, :]
        sg = _sigmoid(acc)
        dpre = d_ref[...] * (sg * (1.0 + acc * (1.0 - sg)))
        padd[0:s, :] = dpre
        padd[s:s + 8, :] = jnp.zeros((8, LANES), F32)
        dx = jnp.zeros((s, LANES), F32)
        for t in range(SSD_CONV):
            dx = dx + padd[pl.ds(SSD_CONV - 1 - t, s), :] * w_ref[t:t + 1, :]
            dw_ref[t:t + 1, :] = jnp.sum(dpre * pad[pl.ds(8 - (SSD_CONV - 1) + t, s), :], axis=0, keepdims=True)
        dx_ref[...] = dx.astype(BF16)
        db_ref[...] = jnp.sum(dpre, axis=0, keepdims=True)

    blk = pl.BlockSpec((s, LANES), lambda j: (0, j))
    return pl.pallas_call(
        body, name=name, grid=(nblk,),
        in_specs=[pl.BlockSpec((s, LANES), lambda j: (0, c0 + j)), pl.BlockSpec((SSD_CONV, LANES), lambda j: (0, j)),
                  pl.BlockSpec((1, LANES), lambda j: (0, j)), blk],
        out_specs=[blk, pl.BlockSpec((SSD_CONV, LANES), lambda j: (0, j)), pl.BlockSpec((1, LANES), lambda j: (0, j))],
        out_shape=[jax.ShapeDtypeStruct((s, width), BF16), jax.ShapeDtypeStruct((SSD_CONV, width), F32),
                   jax.ShapeDtypeStruct((1, width), F32)],
        scratch_shapes=[pltpu.VMEM((s + 8, LANES), F32), pltpu.VMEM((s + 8, LANES), F32)],
        compiler_params=_params("parallel"))(p, conv_w, conv_b, dact)


def _softplus(x):
    return jnp.maximum(x, 0.0) + jnp.log(1.0 + jnp.exp(-jnp.abs(x)))


def _dt_fwd(p, dt_bias, a_log, name):
    s = p.shape[0]
    tm = _pick(s, 512)

    def body(x_ref, b_ref, a_ref, dt_ref, da_ref):
        dtv = _softplus(x_ref[...] + b_ref[...])
        dav = dtv * (-jnp.exp(a_ref[...]))
        for h in range(SSD_HEADS):
            hs = slice(h * LANES, (h + 1) * LANES)
            dt_ref[:, hs] = jnp.broadcast_to(dtv[:, h:h + 1], (tm, LANES))
            da_ref[:, hs] = jnp.broadcast_to(dav[:, h:h + 1], (tm, LANES))

    out = pl.BlockSpec((tm, HP), lambda i: (i, 0))
    return pl.pallas_call(
        body, name=name, grid=(s // tm,),
        in_specs=[pl.BlockSpec((tm, LANES), lambda i: (i, T_DT // LANES)), _const((1, LANES)), _const((1, LANES))],
        out_specs=[out, out], out_shape=[jax.ShapeDtypeStruct((s, HP), F32)] * 2,
        compiler_params=_params("parallel"))(p, dt_bias, a_log)


def _dt_bwd(p, dt_bias, a_log, dda, ddtx, name):
    s = p.shape[0]
    tm = _pick(s, 512)

    def body(x_ref, b_ref, a_ref, dda_ref, ddtx_ref, dx_ref, db_ref, dal_ref):
        @pl.when(pl.program_id(0) == 0)
        def _():
            db_ref[...] = jnp.zeros_like(db_ref)
            dal_ref[...] = jnp.zeros_like(dal_ref)

        x = x_ref[...] + b_ref[...]
        dtv = _softplus(x)
        av = -jnp.exp(a_ref[...])
        lane = lax.broadcasted_iota(jnp.int32, (tm, LANES), 1)
        pa = jnp.zeros((tm, LANES), F32)
        px = jnp.zeros((tm, LANES), F32)
        for h in range(SSD_HEADS):
            pa = jnp.where(lane == h, dda_ref[:, h * LANES:(h + 1) * LANES], pa)
            px = jnp.where(lane == h, ddtx_ref[:, h * LANES:(h + 1) * LANES], px)
        draw = (pa * av + px) * _sigmoid(x)
        dx_ref[...] = draw.astype(BF16)
        db_ref[...] += jnp.sum(draw, axis=0, keepdims=True)
        dal_ref[...] += jnp.sum(pa * dtv, axis=0, keepdims=True) * av

    hd = pl.BlockSpec((tm, HP), lambda i: (i, 0))
    return pl.pallas_call(
        body, name=name, grid=(s // tm,),
        in_specs=[pl.BlockSpec((tm, LANES), lambda i: (i, T_DT // LANES)), _const((1, LANES)), _const((1, LANES)), hd, hd],
        out_specs=[pl.BlockSpec((tm, LANES), lambda i: (i, 0)), _const((1, LANES)), _const((1, LANES))],
        out_shape=[jax.ShapeDtypeStruct((s, LANES), BF16), jax.ShapeDtypeStruct((1, LANES), F32),
                   jax.ShapeDtypeStruct((1, LANES), F32)],
        compiler_params=_params("arbitrary"))(p, dt_bias, a_log, dda, ddtx)


def _cumsum_rows(x):
    row = lax.broadcasted_iota(jnp.int32, x.shape, 0)
    k = 1
    while k < x.shape[0]:
        x = x + jnp.where(row >= k, pltpu.roll(x, k, 0), 0.0)
        k *= 2
    return x


def _rev_cumsum_rows(x):
    n = x.shape[0]
    row = lax.broadcasted_iota(jnp.int32, x.shape, 0)
    k = 1
    while k < n:
        x = x + jnp.where(row < n - k, pltpu.roll(x, n - k, 0), 0.0)
        k *= 2
    return x


HPG = SSD_HEADS // SSD_GROUPS


def _chunk_decay(da):
    cs = _cumsum_rows(da)
    lm = jnp.exp(jnp.where(_tril_mask(), cs - cs.T, _NEG))
    return cs, lm, cs[CHUNK - 1:CHUNK, :]


def _scan_fwd(xs, bc, dtb, dab, name, side=None):
    s = xs.shape[0]
    nc = s // CHUNK

    def body(x_ref, b_ref, c_ref, dt_ref, da_ref, y_ref, sin_ref, state):
        @pl.when(pl.program_id(1) == 0)
        def _():
            state[...] = jnp.zeros_like(state)

        bv = b_ref[...]
        b16, c16 = bv.astype(BF16), c_ref[...].astype(BF16)
        g = lax.dot_general(c16, b16, _NT, preferred_element_type=F32)
        for hh in range(HPG):
            hs = slice(hh * LANES, (hh + 1) * LANES)
            st = state[hh]
            sin_ref[hh] = st
            cs, lm, cl = _chunk_decay(da_ref[:, hs])
            xd = (x_ref[:, hs] * dt_ref[:, hs]).astype(BF16)
            y = jnp.dot((g * lm).astype(BF16), xd, preferred_element_type=F32)
            y_ref[:, hs] = y + jnp.dot(c16, st.astype(BF16), preferred_element_type=F32) * jnp.exp(cs)
            bd = (bv * jnp.exp(cl - cs)).astype(BF16)
            state[hh] = jnp.exp(cl) * st + lax.dot_general(bd, xd, _TN, preferred_element_type=F32)

    gw = HPG * LANES
    hd = pl.BlockSpec((CHUNK, gw), lambda g, c: (c, g))
    return _call(
        body, name=name, grid=(SSD_GROUPS, nc),
        in_specs=[hd, pl.BlockSpec((CHUNK, LANES), lambda g, c: (c, g)),
                  pl.BlockSpec((CHUNK, LANES), lambda g, c: (c, SSD_GROUPS + g)), hd, hd],
        out_specs=[hd, pl.BlockSpec((HPG, None, SSD_STATE, LANES), lambda g, c: (g, c, 0, 0))],
        out_shape=[jax.ShapeDtypeStruct((s, HP), F32), jax.ShapeDtypeStruct((SSD_HEADS, nc, SSD_STATE, LANES), F32)],
        scratch_shapes=[pltpu.VMEM((HPG, SSD_STATE, LANES), F32)],
        args=(xs, bc, bc, dtb, dab), semantics=("parallel", "arbitrary"), side=side)


def _scan_bwd(xs, bc, dtb, dab, s_in, dy, d_vec, name):
    s = xs.shape[0]
    nc = s // CHUNK

    def body(x_ref, b_ref, c_ref, dt_ref, da_ref, sin_ref, dy_ref, dv_ref, dx_ref, db_ref, dc_ref, dda_ref, ddtx_ref, dstate):
        @pl.when(pl.program_id(1) == 0)
        def _():
            dstate[...] = jnp.zeros_like(dstate)

        bv = b_ref[...]
        b16, c16 = bv.astype(BF16), c_ref[...].astype(BF16)
        g = lax.dot_general(c16, b16, _NT, preferred_element_type=F32)
        row = lax.broadcasted_iota(jnp.int32, (CHUNK, 1), 0)
        dbm = jnp.zeros((CHUNK, SSD_STATE), F32)
        dcm = jnp.zeros((CHUNK, SSD_STATE), F32)
        for hh in range(HPG):
            hs = slice(hh * LANES, (hh + 1) * LANES)
            st, ds = sin_ref[hh], dstate[hh]
            st16, ds16 = st.astype(BF16), ds.astype(BF16)
            xv, dtv, dyv = x_ref[:, hs], dt_ref[:, hs], dy_ref[:, hs]
            cs, lm, cl = _chunk_decay(da_ref[:, hs])
            ecs, ecl = jnp.exp(cs), jnp.exp(cl)
            decay = jnp.exp(cl - cs)
            xd = (xv * dtv).astype(BF16)
            dy16 = dyv.astype(BF16)
            dye = (dyv * ecs).astype(BF16)
            yoff = jnp.dot(c16, st16, preferred_element_type=F32) * ecs
            dcs = jnp.sum(dyv * yoff, axis=-1, keepdims=True)
            dcm = dcm + lax.dot_general(dye, st16, _NT, preferred_element_type=F32)
            dstate[hh] = ecl * ds + lax.dot_general(c16, dye, _TN, preferred_element_type=F32)
            dcl = jnp.sum(jnp.sum(ds * st, axis=0, keepdims=True), axis=1, keepdims=True) * ecl[:, 0:1]
            bd32 = bv * decay
            qm = lax.dot_general(xd, ds16, _NT, preferred_element_type=F32)
            dbm = dbm + qm * decay
            w = jnp.sum(bd32 * qm, axis=-1, keepdims=True)
            dcs = dcs - w
            dcl = dcl + jnp.sum(w, axis=0, keepdims=True)
            dxd = jnp.dot(bd32.astype(BF16), ds16, preferred_element_type=F32)
            m16 = (g * lm).astype(BF16)
            dm = lax.dot_general(dy16, xd, _NT, preferred_element_type=F32)
            dxd = dxd + lax.dot_general(m16, dy16, _TN, preferred_element_type=F32)
            dg = dm * lm
            dg16 = dg.astype(BF16)
            tt = dg * g
            dcm = dcm + jnp.dot(dg16, b16, preferred_element_type=F32)
            dbm = dbm + lax.dot_general(dg16, c16, _TN, preferred_element_type=F32)
            dcs = dcs + jnp.sum(tt, axis=-1, keepdims=True) - jnp.sum(tt.T, axis=-1, keepdims=True)
            dcs = dcs + jnp.where(row == CHUNK - 1, dcl, 0.0)
            dda_ref[:, hs] = _rev_cumsum_rows(jnp.broadcast_to(dcs, (CHUNK, LANES)))
            ddtx_ref[:, hs] = jnp.broadcast_to(jnp.sum(dxd * xv, axis=-1, keepdims=True), (CHUNK, LANES))
            dx_ref[:, hs] = dxd * dtv + dyv * dv_ref[:, hs]
        db_ref[...] = dbm
        dc_ref[...] = dcm

    gw = HPG * LANES
    hd = pl.BlockSpec((CHUNK, gw), lambda g, c: (nc - 1 - c, g))
    gp = pl.BlockSpec((CHUNK, LANES), lambda g, c: (nc - 1 - c, g))
    return pl.pallas_call(
        body, name=name, grid=(SSD_GROUPS, nc),
        in_specs=[hd, gp, pl.BlockSpec((CHUNK, LANES), lambda g, c: (nc - 1 - c, SSD_GROUPS + g)), hd, hd,
                  pl.BlockSpec((HPG, None, SSD_STATE, LANES), lambda g, c: (g, nc - 1 - c, 0, 0)), hd,
                  pl.BlockSpec((1, gw), lambda g, c: (0, g))],
        out_specs=[hd, gp, gp, hd, hd],
        out_shape=[jax.ShapeDtypeStruct((s, HP), F32), jax.ShapeDtypeStruct((s, SSD_GROUPS * SSD_STATE), F32),
                   jax.ShapeDtypeStruct((s, SSD_GROUPS * SSD_STATE), F32), jax.ShapeDtypeStruct((s, HP), F32),
                   jax.ShapeDtypeStruct((s, HP), F32)],
        scratch_shapes=[pltpu.VMEM((HPG, SSD_STATE, LANES), F32)],
        compiler_params=_params("parallel", "arbitrary"))(xs, bc, bc, dtb, dab, s_in, dy, d_vec)


_GN = SSD_INNER // SSD_GROUPS
_GW = HP // SSD_GROUPS


def _ssd_post_fwd(y, xbc, p, d_vec, gain, name):
    s = y.shape[0]
    tm = _pick(s, 512)

    def body(y_ref, x_ref, z_ref, d_ref, g_ref, o_ref):
        z = z_ref[...].astype(F32)
        y2 = (y_ref[...] + x_ref[...] * d_ref[...]) * (z * _sigmoid(z))
        for g in range(SSD_GROUPS):
            gs = slice(g * _GW, (g + 1) * _GW)
            yg = y2[:, gs]
            r = lax.rsqrt(jnp.sum(yg * yg, axis=-1, keepdims=True) * (1.0 / _GN) + EPS)
            o_ref[:, gs] = (yg * r * g_ref[:, gs]).astype(BF16)

    hd = pl.BlockSpec((tm, HP), lambda i: (i, 0))
    return pl.pallas_call(
        body, name=name, grid=(s // tm,),
        in_specs=[hd, hd, pl.BlockSpec((tm, HP), lambda i: (i, C_Z // HP)), _const((1, HP)), _const((1, HP))],
        out_specs=hd, out_shape=jax.ShapeDtypeStruct((s, HP), BF16), compiler_params=_params("parallel"))(y, xbc, p, d_vec, gain)


def _ssd_post_bwd(y, xbc, p, d_vec, gain, dyn, name):
    s = y.shape[0]
    tm = _pick(s, 512)

    def body(y_ref, x_ref, z_ref, d_ref, g_ref, dn_ref, dy_ref, dz_ref, dg_ref, dd_ref):
        @pl.when(pl.program_id(0) == 0)
        def _():
            dg_ref[...] = jnp.zeros_like(dg_ref)
            dd_ref[...] = jnp.zeros_like(dd_ref)

        z, xv = z_ref[...].astype(F32), x_ref[...]
        sg = _sigmoid(z)
        sz = z * sg
        yt = y_ref[...] + xv * d_ref[...]
        y2 = yt * sz
        for g in range(SSD_GROUPS):
            gs = slice(g * _GW, (g + 1) * _GW)
            yg, dn = y2[:, gs], dn_ref[:, gs].astype(F32)
            r = lax.rsqrt(jnp.sum(yg * yg, axis=-1, keepdims=True) * (1.0 / _GN) + EPS)
            u = dn * g_ref[:, gs]
            dy2 = r * u - yg * (r * r * r) * (jnp.sum(yg * u, axis=-1, keepdims=True) * (1.0 / _GN))
            dg_ref[:, gs] += jnp.sum(dn * yg * r, axis=0, keepdims=True)
            dyt = dy2 * sz[:, gs]
            dy_ref[:, gs] = dyt
            dz_ref[:, gs] = (dy2 * yt[:, gs] * (sg[:, gs] * (1.0 + z[:, gs] * (1.0 - sg[:, gs])))).astype(BF16)
            dd_ref[:, gs] += jnp.sum(dyt * xv[:, gs], axis=0, keepdims=True)

    hd = pl.BlockSpec((tm, HP), lambda i: (i, 0))
    return pl.pallas_call(
        body, name=name, grid=(s // tm,),
        in_specs=[hd, hd, pl.BlockSpec((tm, HP), lambda i: (i, C_Z // HP)), _const((1, HP)), _const((1, HP)), hd],
        out_specs=[hd, hd, _const((1, HP)), _const((1, HP))],
        out_shape=[jax.ShapeDtypeStruct((s, HP), F32), jax.ShapeDtypeStruct((s, HP), BF16),
                   jax.ShapeDtypeStruct((1, HP), F32), jax.ShapeDtypeStruct((1, HP), F32)],
        compiler_params=_params("arbitrary"))(y, xbc, p, d_vec, gain, dyn)


def _merge_fwd(p, ya, o, yc, wb0, wb1, wb2, w_out, x, name, side=None):
    s = p.shape[0]
    tm = _pick(s, 512)

    def body(g_ref, ya_ref, o_ref, yc_ref, w0_ref, w1_ref, w2_ref, wo_ref, x_ref, mg_ref, y_ref):
        acc = jnp.zeros((tm, D_MODEL), F32)
        for i, (b_ref, w_ref) in enumerate(((ya_ref, w0_ref), (o_ref, w1_ref), (yc_ref, w2_ref))):
            t = jnp.dot(b_ref[...].astype(BF16), w_ref[...], preferred_element_type=F32)
            acc = acc + _sigmoid(g_ref[:, i * D_MODEL:(i + 1) * D_MODEL].astype(F32)) * t
        mg = acc.astype(BF16)
        mg_ref[...] = mg
        y_ref[...] = x_ref[...] + jnp.dot(mg, wo_ref[...], preferred_element_type=F32)

    row = pl.BlockSpec((tm, D_MODEL), lambda i: (i, 0))
    return _call(
        body, name=name, grid=(s // tm,),
        in_specs=[pl.BlockSpec((tm, 3 * D_MODEL), lambda i: (i, C_G // (3 * D_MODEL))),
                  pl.BlockSpec((tm, GM_WIDTH), lambda i: (i, 0)), row, row,
                  _resident((GM_WIDTH, D_MODEL)), _resident((HP, D_MODEL)), _resident((HP, D_MODEL)),
                  _resident((D_MODEL, D_MODEL)), row],
        out_specs=[row, row],
        out_shape=[jax.ShapeDtypeStruct((s, D_MODEL), BF16), jax.ShapeDtypeStruct((s, D_MODEL), F32)],
        scratch_shapes=[], args=(p, ya, o, yc, wb0, wb1, wb2, w_out, x), semantics=("parallel",), side=side)


def _merge_bwd(p, ya, o, yc, wb0, wb1, wb2, w_out, dy, name):
    s = p.shape[0]
    tm = _pick(s, 512)

    def body(g_ref, ya_ref, o_ref, yc_ref, w0_ref, w1_ref, w2_ref, wo_ref, dy_ref,
             d0_ref, d1_ref, d2_ref, dg_ref, dya_ref, do_ref, dyc_ref):
        dm = lax.dot_general(dy_ref[...].astype(BF16), wo_ref[...], _NT, preferred_element_type=F32)
        for i, (b_ref, w_ref, d_ref, db_ref) in enumerate(((ya_ref, w0_ref, d0_ref, dya_ref), (o_ref, w1_ref, d1_ref, do_ref),
                                                            (yc_ref, w2_ref, d2_ref, dyc_ref))):
            cs = slice(i * D_MODEL, (i + 1) * D_MODEL)
            t = jnp.dot(b_ref[...].astype(BF16), w_ref[...], preferred_element_type=F32)
            sg = _sigmoid(g_ref[:, cs].astype(F32))
            dt16 = (dm * sg).astype(BF16)
            d_ref[...] = dt16
            dg_ref[:, cs] = (dm * t * sg * (1.0 - sg)).astype(BF16)
            db_ref[...] = lax.dot_general(dt16, w_ref[...], _NT, preferred_element_type=F32).astype(db_ref.dtype)

    row = pl.BlockSpec((tm, D_MODEL), lambda i: (i, 0))
    nar = pl.BlockSpec((tm, GM_WIDTH), lambda i: (i, 0))
    wide = pl.BlockSpec((tm, 3 * D_MODEL), lambda i: (i, 0))
    return pl.pallas_call(
        body, name=name, grid=(s // tm,),
        in_specs=[pl.BlockSpec((tm, 3 * D_MODEL), lambda i: (i, C_G // (3 * D_MODEL))), nar, row, row,
                  _resident((GM_WIDTH, D_MODEL)), _resident((HP, D_MODEL)), _resident((HP, D_MODEL)),
                  _resident((D_MODEL, D_MODEL)), row],
        out_specs=[row, row, row, wide, nar, row, row],
        out_shape=[jax.ShapeDtypeStruct((s, D_MODEL), BF16)] * 3 + [jax.ShapeDtypeStruct((s, 3 * D_MODEL), BF16),
                   jax.ShapeDtypeStruct((s, GM_WIDTH), BF16), jax.ShapeDtypeStruct((s, D_MODEL), BF16),
                   jax.ShapeDtypeStruct((s, D_MODEL), F32)],
        compiler_params=_params("parallel"))(p, ya, o, yc, wb0, wb1, wb2, w_out, dy)


def _loss_head(y, target, name):
    s, d = y.shape
    tm = _pick(s, 512)

    def body(y_ref, t_ref, dy_ref, sq_ref):
        @pl.when(pl.program_id(0) == 0)
        def _():
            sq_ref[...] = jnp.zeros_like(sq_ref)

        e = y_ref[...] - t_ref[...]
        dy_ref[...] = e * (1.0 / d)
        sq_ref[...] += jnp.sum(e * e, axis=0, keepdims=True)

    row = pl.BlockSpec((tm, d), lambda i: (i, 0))
    return pl.pallas_call(
        body, name=name, grid=(s // tm,), in_specs=[row, row], out_specs=[row, _const((1, d))],
        out_shape=[jax.ShapeDtypeStruct((s, d), F32), jax.ShapeDtypeStruct((1, d), F32)],
        compiler_params=_params("arbitrary"))(y, target)


def _adamw(w, g, m, v, name):
    rows, cols = w.shape
    tr = rows
    for cand in (512, 256, 128, 64, 32, 16, 8):
        if rows % cand == 0 and cand * cols * 4 <= 3 * 1024 * 1024:
            tr = cand
            break

    def body(w_ref, g_ref, m_ref, v_ref, d_ref, nm_ref, nv_ref):
        d_ref[...], nm_ref[...], nv_ref[...] = _adam_update(w_ref[...], g_ref[...], m_ref[...], v_ref[...])

    blk = pl.BlockSpec((tr, cols), lambda i: (i, 0))
    return pl.pallas_call(
        body, name=name, grid=(rows // tr,), in_specs=[blk] * 4, out_specs=[blk] * 3,
        out_shape=[jax.ShapeDtypeStruct((rows, cols), F32)] * 3, compiler_params=_params("parallel"))(w, g, m, v)


def _adam_update(w, g, m, v):
    nm = ADAM_B1 * m + (1.0 - ADAM_B1) * g
    nv = ADAM_B2 * v + (1.0 - ADAM_B2) * (g * g)
    c1 = 1.0 - ADAM_B1 ** ADAM_STEP
    c2 = 1.0 - ADAM_B2 ** ADAM_STEP
    return -ADAM_LR * ((nm / c1) / (jnp.sqrt(nv / c2) + ADAM_EPS) + ADAM_WD * w), nm, nv


def _adamw_sharded(w, m, v, mine, theirs, name, side=None):
    depth, rows, cols = w.shape
    tr = _row_tile(rows // 2, cols, 1024 * 1024)
    nb = rows // 2 // tr

    def body(w_ref, m_ref, v_ref, a_ref, b_ref, g_ref, d_ref, nm_ref, nv_ref):
        c = lax.axis_index("c")
        g = jnp.where(pl.program_id(1) // nb == c, a_ref[...], b_ref[...])
        g_ref[...] = g
        d_ref[...], nm_ref[...], nv_ref[...] = _adam_update(w_ref[...], g, m_ref[...], v_ref[...])

    blk = pl.BlockSpec((None, tr, cols), lambda l, i: (l, i, 0))
    mine_s = pl.BlockSpec((None, tr, cols), lambda l, i: (l, jnp.where(i // nb == lax.axis_index("c"), i % nb, 0), 0))
    theirs_s = pl.BlockSpec((None, tr, cols), lambda l, i: (l, jnp.where(i // nb == lax.axis_index("c"), 0, i % nb), 0))
    return _call(
        body, name=name, grid=(depth, rows // tr), in_specs=[blk, blk, blk, mine_s, theirs_s], out_specs=[blk] * 4,
        out_shape=[jax.ShapeDtypeStruct((depth, rows, cols), F32)] * 4, scratch_shapes=[],
        args=(w, m, v, mine, theirs), semantics=("parallel", "parallel"), side=side)


ANY = pl.BlockSpec(memory_space=pl.ANY)


def _me():
    return lax.axis_index("x"), lax.axis_index("y"), lax.axis_index("c")


def _other_chips(x, y):
    return [(1 - x, y), (x, 1 - y), (1 - x, 1 - y)]


def _chip_index(cx, cy):
    return 2 * cx + cy


class _Exchange:
    def __init__(self, ins, out_shapes, n_sems, start, finish):
        self.ins, self.out_shapes, self.n_sems, self.start, self.finish = list(ins), list(out_shapes), n_sems, start, finish


def _sem_scratch(ex):
    return [pltpu.SemaphoreType.DMA((ex.n_sems,)), pltpu.SemaphoreType.DMA((ex.n_sems,))]


def _run_exchange(ex, name):
    n_in, n_out = len(ex.ins), len(ex.out_shapes)

    def body(*refs):
        in_refs, out_refs, (send, recv) = refs[:n_in], refs[n_in:n_in + n_out], refs[n_in + n_out:]
        ex.start(in_refs, out_refs, send, recv)
        ex.finish(in_refs, out_refs, send, recv)

    return pl.pallas_call(body, name=name, in_specs=[ANY] * n_in, out_specs=[ANY] * n_out, out_shape=ex.out_shapes,
                          scratch_shapes=_sem_scratch(ex))(*ex.ins)


def _call(body, *, name, grid, in_specs, out_specs, out_shape, scratch_shapes, args, semantics, side=None):
    if side is None:
        return pl.pallas_call(body, name=name, grid=grid, in_specs=in_specs, out_specs=out_specs, out_shape=out_shape,
                              scratch_shapes=scratch_shapes, compiler_params=_params(*semantics))(*args), []
    n_in, n_out, n_sc = len(in_specs), len(out_specs), len(scratch_shapes)
    s_in, s_out = len(side.ins), len(side.out_shapes)

    def hosted(*refs):
        pos = 0
        parts = []
        for size in (n_in, s_in, n_out, s_out, n_sc, 2):
            parts.append(refs[pos:pos + size])
            pos += size
        ins, sins, outs, souts, scratch, (send, recv) = parts
        ids = [pl.program_id(a) for a in range(len(grid))]
        first = functools.reduce(jnp.logical_and, [i == 0 for i in ids])
        last = functools.reduce(jnp.logical_and, [i == g - 1 for i, g in zip(ids, grid)])

        @pl.when(first)
        def _():
            side.start(sins, souts, send, recv)

        body(*ins, *outs, *scratch)

        @pl.when(last)
        def _():
            side.finish(sins, souts, send, recv)

    res = pl.pallas_call(
        hosted, name=name, grid=grid, in_specs=list(in_specs) + [ANY] * s_in, out_specs=list(out_specs) + [ANY] * s_out,
        out_shape=list(out_shape) + side.out_shapes, scratch_shapes=list(scratch_shapes) + _sem_scratch(side),
        compiler_params=_params(*["arbitrary"] * len(grid)))(*args, *side.ins)
    return res[:n_out], res[n_out:]


def _half(ref_rows, c):
    return pl.ds(c * (ref_rows // 2), ref_rows // 2)


def _gather_exchange(shards, layer):
    n = len(shards)
    rows = [a.shape[1] for a in shards]

    def copy(in_refs, out_refs, send, recv, t, k, chip, hc, to, from_input=False):
        dst = out_refs[t].at[chip, _half(rows[t], hc)]
        src = in_refs[t].at[layer, _half(rows[t], hc)] if from_input else dst
        return pltpu.make_async_remote_copy(src_ref=src, dst_ref=dst, send_sem=send.at[7 * t + k], recv_sem=recv.at[7 * t + k],
                                            device_id=to, device_id_type=MESH)

    def own(in_refs, out_refs, send, recv, t):
        x, y, c = _me()
        return pltpu.make_async_remote_copy(src_ref=in_refs[t].at[layer], dst_ref=out_refs[t].at[_chip_index(x, y)],
                                            send_sem=send.at[7 * t + 6], recv_sem=recv.at[7 * t + 6],
                                            device_id=(x, y, 1 - c), device_id_type=MESH)

    def start(in_refs, out_refs, send, recv):
        x, y, c = _me()
        for j, chip in enumerate(_other_chips(x, y)):
            for t in range(n):
                copy(in_refs, out_refs, send, recv, t, j, _chip_index(x, y), c, (*chip, c), from_input=True).start()
        for t in range(n):
            own(in_refs, out_refs, send, recv, t).start()

    def finish(in_refs, out_refs, send, recv):
        x, y, c = _me()
        chips = _other_chips(x, y)
        passed = []
        for t in range(n):
            own(in_refs, out_refs, send, recv, t).wait()
        for j, chip in enumerate(chips):
            for t in range(n):
                copy(in_refs, out_refs, send, recv, t, j, _chip_index(*chip), c, (x, y, c)).wait_recv()
                cp = copy(in_refs, out_refs, send, recv, t, 3 + j, _chip_index(*chip), c, (x, y, 1 - c))
                cp.start()
                passed.append(cp)
        for j, chip in enumerate(chips):
            for t in range(n):
                copy(in_refs, out_refs, send, recv, t, 3 + j, _chip_index(*chip), 1 - c, (x, y, c)).wait_recv()
                copy(in_refs, out_refs, send, recv, t, j, _chip_index(x, y), c, (*chip, c), from_input=True).wait_send()
        for cp in passed:
            cp.wait_send()

    return _Exchange(shards, [jax.ShapeDtypeStruct((N_CHIPS,) + a.shape[1:], a.dtype) for a in shards], 7 * n, start, finish)


def _pair_exchange(gs):
    n = len(gs)
    rows = [a.shape[1] for a in gs]

    def copies(in_refs, out_refs, send, recv):
        x, y, c = _me()
        return [pltpu.make_async_remote_copy(src_ref=in_refs[t].at[:, _half(rows[t], 1 - c)], dst_ref=out_refs[t],
                                             send_sem=send.at[t], recv_sem=recv.at[t], device_id=(x, y, 1 - c),
                                             device_id_type=MESH) for t in range(n)]

    def start(*refs):
        for cp in copies(*refs):
            cp.start()

    def finish(*refs):
        for cp in copies(*refs):
            cp.wait()

    return _Exchange(gs, [jax.ShapeDtypeStruct((N_CHIPS, a.shape[1] // 2, a.shape[2]), a.dtype) for a in gs], n, start, finish)


def _row_tile(rows, cols, budget=2 * 1024 * 1024):
    best = None
    for t in range(8, rows + 1, 8):
        if rows % t == 0 and t * cols * 4 <= budget:
            best = t
    return best or rows


def _pair_add(g, got, name):
    _, rows, cols = g.shape
    tr = _row_tile(rows // 2, cols)
    nb = rows // 2 // tr

    def body(g_ref, r_ref, o16_ref, own_ref):
        x, y, _ = _me()
        tot = g_ref[...] + r_ref[...]
        o16_ref[...] = tot.astype(BF16)

        @pl.when(pl.program_id(1) == _chip_index(x, y))
        def _():
            own_ref[...] = tot

    blk = (None, tr, cols)
    return pl.pallas_call(
        body, name=name, grid=(nb, N_CHIPS),
        in_specs=[pl.BlockSpec(blk, lambda i, k: (k, i + lax.axis_index("c") * nb, 0)),
                  pl.BlockSpec(blk, lambda i, k: (k, i, 0))],
        out_specs=[pl.BlockSpec(blk, lambda i, k: (k, i, 0)), pl.BlockSpec((tr, cols), lambda i, k: (i, 0))],
        out_shape=[jax.ShapeDtypeStruct((N_CHIPS, rows // 2, cols), BF16), jax.ShapeDtypeStruct((rows // 2, cols), F32)],
        compiler_params=_params("parallel", "arbitrary"))(g, got)


def _chip_exchange(parts):
    n = len(parts)

    def copies(in_refs, out_refs, send, recv):
        x, y, c = _me()
        return [pltpu.make_async_remote_copy(src_ref=in_refs[t].at[_chip_index(*chip)], dst_ref=out_refs[t].at[j],
                                             send_sem=send.at[3 * t + j], recv_sem=recv.at[3 * t + j],
                                             device_id=(*chip, c), device_id_type=MESH)
                for j, chip in enumerate(_other_chips(x, y)) for t in range(n)]

    def start(*refs):
        for cp in copies(*refs):
            cp.start()

    def finish(*refs):
        for cp in copies(*refs):
            cp.wait()

    return _Exchange(parts, [jax.ShapeDtypeStruct((3,) + a.shape[1:], a.dtype) for a in parts], 3 * n, start, finish)


def _chip_add(own, got, name, layer, into=None):
    rows, cols = own.shape
    tr = _row_tile(rows, cols, 1024 * 1024)

    def body(own_ref, got_ref, *rest):
        acc = own_ref[...]
        for j in range(3):
            acc = acc + got_ref[j].astype(F32)
        rest[-1][...] = acc

    in_specs = [pl.BlockSpec((tr, cols), lambda i: (i, 0)), pl.BlockSpec((3, tr, cols), lambda i: (0, i, 0))]
    args, alias = [own, got], {}
    if into is not None:
        in_specs.append(ANY)
        args.append(into)
        alias = {2: 0}
    return pl.pallas_call(
        body, name=name, grid=(rows // tr,), in_specs=in_specs,
        out_specs=pl.BlockSpec((None, tr, cols), lambda i: (layer, i, 0)),
        out_shape=jax.ShapeDtypeStruct((DEPTH, rows, cols), F32), input_output_aliases=alias,
        compiler_params=_params("parallel"))(*args)


def _pair_share(halves):
    n = len(halves)

    def copies(in_refs, out_refs, send, recv):
        x, y, c = _me()
        return [pltpu.make_async_remote_copy(src_ref=in_refs[t], dst_ref=out_refs[t], send_sem=send.at[t],
                                             recv_sem=recv.at[t], device_id=(x, y, 1 - c), device_id_type=MESH)
                for t in range(n)]

    def start(*refs):
        for cp in copies(*refs):
            cp.start()

    def finish(*refs):
        for cp in copies(*refs):
            cp.wait()

    return _Exchange(halves, [jax.ShapeDtypeStruct(a.shape, a.dtype) for a in halves], n, start, finish)


N_DEV = 8


def _all_exchange(v):
    r, cols = v.shape

    def peers():
        x, y, c = _me()
        flip = lambda v, f: 1 - v if f else v
        return 4 * x + 2 * y + c, [(flip(x, fx), flip(y, fy), flip(c, fc)) for fx in (0, 1) for fy in (0, 1) for fc in (0, 1)][1:]

    def local(in_refs, out_refs, send, me):
        return pltpu.make_async_copy(in_refs[0], out_refs[0].at[me], send.at[7])

    def start(in_refs, out_refs, send, recv):
        me, others = peers()
        local(in_refs, out_refs, send, me).start()
        for j, peer in enumerate(others):
            pltpu.make_async_remote_copy(src_ref=in_refs[0], dst_ref=out_refs[0].at[me], send_sem=send.at[j],
                                         recv_sem=recv.at[j], device_id=peer, device_id_type=MESH).start()

    def finish(in_refs, out_refs, send, recv):
        me, others = peers()
        for j, (px, py, pc) in enumerate(others):
            pltpu.make_async_remote_copy(src_ref=in_refs[0], dst_ref=out_refs[0].at[4 * px + 2 * py + pc], send_sem=send.at[j],
                                         recv_sem=recv.at[j], device_id=(px, py, pc), device_id_type=MESH).wait()
        local(in_refs, out_refs, send, me).wait()

    return _Exchange([v], [jax.ShapeDtypeStruct((N_DEV, r, cols), v.dtype)], 8, start, finish)


def _sum_slots(a, name):
    n, r, cols = a.shape
    tr = _pick(r, 512) if r % 8 == 0 else r
    for cand in (512, 256, 128, 64, 32, 16, 8):
        if r % cand == 0:
            tr = cand
            break

    def body(a_ref, o_ref):
        acc = a_ref[0]
        for k in range(1, n):
            acc = acc + a_ref[k]
        o_ref[...] = acc

    return pl.pallas_call(
        body, name=name, grid=(r // tr,), in_specs=[pl.BlockSpec((n, tr, cols), lambda i: (0, i, 0))],
        out_specs=pl.BlockSpec((tr, cols), lambda i: (i, 0)), out_shape=jax.ShapeDtypeStruct((r, cols), F32),
        compiler_params=_params("parallel"))(a)


def _join(name, stacked):
    ax = SHARDED[name][1]
    return jnp.concatenate([stacked[k] for k in range(N_CHIPS)], axis=ax)


def _split(name, full):
    ax = SHARDED[name][1]
    return jnp.stack(jnp.split(full, N_CHIPS, axis=ax))


def _heads_pad(a, real, axis):
    shp = a.shape
    a = a.reshape(shp[:axis] + (MLA_HEADS, real) + shp[axis + 1:])
    pad = [(0, 0)] * a.ndim
    pad[axis + 1] = (0, LANES - real)
    a = jnp.pad(a, pad)
    return a.reshape(shp[:axis] + (HP,) + shp[axis + 1:])


def _heads_unpad(a, real, axis):
    shp = a.shape
    a = a.reshape(shp[:axis] + (MLA_HEADS, LANES) + shp[axis + 1:])
    a = lax.slice_in_dim(a, 0, real, axis=axis + 1)
    return a.reshape(shp[:axis] + (MLA_HEADS * real,) + shp[axis + 1:])


def _lane_place(a, start):
    n = a.shape[-1]
    pad = [(0, 0)] * (a.ndim - 1) + [(start, LANES - start - n)]
    return jnp.pad(a, pad)


_O_UV, _O_CQ, _O_CKV, _O_KR, _O_Z, _O_XBC, _O_DT, _O_G = 0, 1024, 1408, 1664, 1696, 2208, 3232, 3240


def _w_in_pad(w):
    sl = lambda a, b: w[:, a:b]
    xs = _heads_pad(sl(_O_XBC, _O_XBC + SSD_INNER), SSD_HEAD_DIM, 1)
    bc = sl(_O_XBC + SSD_INNER, _O_DT)
    main = jnp.concatenate([sl(_O_UV, _O_CQ), _heads_pad(sl(_O_Z, _O_XBC), SSD_HEAD_DIM, 1), xs, sl(_O_G, IN_COLS)], axis=1)
    tail = jnp.concatenate([bc, sl(_O_CKV, _O_KR), sl(_O_CQ, _O_CKV), _lane_place(sl(_O_KR, _O_Z), MLA_NOPE),
                            _lane_place(sl(_O_DT, _O_G), 0), jnp.zeros((w.shape[0], PW_TAIL - T_DT - LANES), w.dtype)], axis=1)
    return main, tail


def _w_in_unpad(gm, gt):
    m = lambda a, n: gm[:, a:a + n]
    t = lambda a, n: gt[:, a:a + n]
    parts = [m(C_UV, 1024), t(T_CQ, MLA_Q_RANK), t(T_CKV, MLA_KV_RANK), t(T_KR + MLA_NOPE, MLA_ROPE),
             _heads_unpad(m(C_Z, HP), SSD_HEAD_DIM, 1), _heads_unpad(m(C_XS, HP), SSD_HEAD_DIM, 1), t(T_BC, BCW),
             t(T_DT, SSD_HEADS), m(C_G, 3 * D_MODEL)]
    return jnp.concatenate(parts, axis=1)


def _xbc_pad(a):
    return jnp.concatenate([_heads_pad(a[..., :SSD_INNER], SSD_HEAD_DIM, a.ndim - 1), a[..., SSD_INNER:]], axis=-1)


def _xbc_unpad(a):
    return jnp.concatenate([_heads_unpad(a[..., :HP], SSD_HEAD_DIM, a.ndim - 1), a[..., HP:]], axis=-1)


def _rope_tables(positions):
    inv_freq = 1.0 / (ROPE_THETA ** (jnp.arange(0, MLA_ROPE, 2, dtype=F32) / MLA_ROPE))
    ang = positions.astype(F32)[:, None] * inv_freq
    cos, sin = jnp.cos(ang), jnp.sin(ang)
    s = positions.shape[0]
    half = MLA_ROPE // 2
    z = lambda n: jnp.zeros((s, n), F32)
    ct = jnp.concatenate([jnp.ones((s, MLA_NOPE), F32), cos, cos, z(LANES - MLA_QK)], axis=1)
    s1 = jnp.concatenate([z(MLA_NOPE), -sin, z(half), z(LANES - MLA_QK)], axis=1)
    s2 = jnp.concatenate([z(MLA_NOPE), z(half), sin, z(LANES - MLA_QK)], axis=1)
    return ct, s1, s2


def _layer_weights(full, small, l, part):
    w = {}
    row = lambda n: small[n][l][None, :]
    stacked = lambda g: g.reshape((N_CHIPS * g.shape[1], g.shape[2]))
    if part in ('ffn1', 'ffn2'):
        w[part + '_w_in'] = full[part + '_w_in']
        w[part + '_w_out'] = stacked(full[part + '_w_out'])
        w[part + '_norm'] = row(part + '_norm')
        return w
    w['w_out'] = stacked(full['w_out'])
    fl = {n: _join(n, full[n]) for n in ('w_in', 'mla_w_uq', 'mla_w_ukv', 'w_branch', 'ssd_conv_w')}
    w['w_in_main'], w['w_in_tail'] = _w_in_pad(fl['w_in'])
    w['wuq'] = _heads_pad(fl['mla_w_uq'], MLA_QK, 1)
    ukv = fl['mla_w_ukv'].reshape(MLA_KV_RANK, MLA_HEADS, MLA_NOPE + MLA_V)
    zero = jnp.zeros((MLA_KV_RANK, MLA_HEADS, LANES - MLA_NOPE), ukv.dtype)
    wk = jnp.concatenate([ukv[:, :, :MLA_NOPE], zero], axis=2).reshape(MLA_KV_RANK, HP)
    wv = jnp.concatenate([ukv[:, :, MLA_NOPE:], zero], axis=2).reshape(MLA_KV_RANK, HP)
    w['wkv'] = jnp.concatenate([wk, wv], axis=1)
    wb = fl['w_branch']
    w['wb0'] = wb[0]
    w['wb1'] = _heads_pad(wb[1], MLA_V, 0)
    w['wb2'] = _heads_pad(wb[2], SSD_HEAD_DIM, 0)
    w['conv_w'] = _xbc_pad(fl['ssd_conv_w'].astype(F32))
    for n in ('mix_norm', 'gm_v_norm', 'mla_q_norm', 'mla_kv_norm'):
        w[n] = row(n)
    w['gm_w_s'] = small['gm_w_s'][l]
    w['gm_b_full'] = jnp.broadcast_to(small['gm_b_s'][l][:, :, None], (GM_GROUPS, CHUNK, LANES))
    w['gq'] = _lane_place(row('mla_q_gain'), 0)
    w['gk'] = _lane_place(row('mla_k_gain'), 0)
    w['conv_b'] = _xbc_pad(row('ssd_conv_b'))
    w['dt_bias'] = _lane_place(row('ssd_dt_bias'), 0)
    w['a_log'] = _lane_place(row('ssd_a_log'), 0)
    w['d_vec'] = jnp.repeat(small['ssd_d'][l], LANES)[None, :]
    w['ssd_norm'] = _heads_pad(row('ssd_norm'), SSD_HEAD_DIM, 1)
    return w


_MIXER_SMALL = ['mla_w_uq', 'mla_w_ukv', 'ssd_conv_w', 'w_branch', 'w_out']
_MIXER_SMALL_G = [n for n in _MIXER_SMALL if n != 'ssd_conv_w']
GATHER_HOSTS = {'attn': ['ffn1_w_in', 'ffn2_w_in'], 'scan': ['ffn1_w_out', 'ffn2_w_out'], 'merge': _MIXER_SMALL, 'ffn2_in': ['w_in']}
GATHER_HOSTS_LATER = {'ffn1_in': ['ffn1_w_out'], 'proj': ['w_in'], 'attn': ['ffn1_w_in', 'ffn2_w_in'], 'scan': ['ffn2_w_out'],
                      'merge': _MIXER_SMALL}
FIRST_NOW = ['ffn1_w_in', 'ffn1_w_out']
FIRST_HOSTS = {'ffn1_in': ['w_in'], 'ffn1_out': _MIXER_SMALL, 'proj': ['ffn2_w_in', 'ffn2_w_out']}
PAIR_HOSTS = {'ffn2_dwout': ['ffn1_w_in', 'ffn2_w_in'], 'ffn2_dwin': ['ffn1_w_out', 'w_in', 'ffn2_w_out'] + _MIXER_SMALL_G}
REDUCE_HOSTS = {'dattn_q': ['ffn1_w_out', 'w_in', 'ffn2_w_out'], 'dattn_kv': ['ffn1_w_in', 'ffn2_w_in'], 'dmla_pre': _MIXER_SMALL_G}
LAST_EARLY = ['w_in', 'ffn2_w_in', 'ffn2_w_out'] + _MIXER_SMALL_G
LAST_HOSTS = {'ffn1_dact': ['w_in'], 'ffn1_dwin': ['ffn2_w_in'], 'ffn1_dx': ['ffn2_w_out'] + _MIXER_SMALL_G}
LAST_LATE = ['ffn1_w_in', 'ffn1_w_out']


def _ffn_fwd(x, norm, w4, w_out, tag, sides=None):
    sides = sides or {}
    carried = {}
    (h, gate, up, act), carried[f"{tag}_in"] = _ffn_in(x, norm, w4, f"{tag}_in", sides.get(f"{tag}_in"))
    y, carried[f"{tag}_out"] = _ffn_out(act, w_out, x, f"{tag}_out", sides.get(f"{tag}_out"))
    return y, (x, h, gate, up, act), carried


def _ffn_bwd(dy, saved, norm, w4, w_out, tag, sides=None, after_dwout=None):
    sides = dict(sides or {})
    carried = {}
    x, h, gate, up, act = saved
    dw_out, carried[f"{tag}_dwout"] = _ffn_dwout(act, dy, f"{tag}_dwout", sides.get(f"{tag}_dwout"))
    if after_dwout is not None:
        sides.update(after_dwout(carried[f"{tag}_dwout"]))
    da, carried[f"{tag}_dact"] = _ffn_dact(dy, w_out, gate, up, f"{tag}_dact", sides.get(f"{tag}_dact"))
    dw_in, carried[f"{tag}_dwin"] = _ffn_dwin(h, da, f"{tag}_dwin", sides.get(f"{tag}_dwin"))
    (dx, dnorm), carried[f"{tag}_dx"] = _ffn_dx(da, w4, x, norm, dy, f"{tag}_dx", sides.get(f"{tag}_dx"))
    return dx, dnorm, dw_in, dw_out.reshape((N_CHIPS, 2 * FC // N_CHIPS, D_MODEL)), carried


def _mixer_fwd(x, w, tabs, tag, sides=None):
    sides = sides or {}
    carried = {}
    h = _rmsnorm_fwd(x, w['mix_norm'], f"{tag}_norm")
    if sides.get('proj') is None:
        pm = _matmul(h, w['w_in_main'], out_dtype=BF16, name=f"{tag}_proj_main")
    else:
        pm, carried['proj'] = _matmul(h, w['w_in_main'], out_dtype=BF16, name=f"{tag}_proj_main", side=sides['proj'])
    pt = _matmul(h, w['w_in_tail'], name=f"{tag}_proj_tail")
    ya = _gmlp_fwd(pm, w['gm_v_norm'], w['gm_w_s'], w['gm_b_full'], f"{tag}_gmlp")
    q, k, v = _mla_pre_fwd(pt, tabs, w['mla_q_norm'], w['mla_kv_norm'], w['wuq'], w['wkv'], w['gq'], w['gk'], f"{tag}_mla_pre")
    (o, lse), carried['attn'] = _attn_fwd(q, k, v, f"{tag}_attn", sides.get('attn'))
    xs = _conv_fwd(pm, C_XS, HP, w['conv_w'][:, :HP], w['conv_b'][:, :HP], f"{tag}_conv_x")
    bc = _conv_fwd(pt, T_BC, BCW, w['conv_w'][:, HP:], w['conv_b'][:, HP:], f"{tag}_conv_bc")
    dtb, dab = _dt_fwd(pt, w['dt_bias'], w['a_log'], f"{tag}_dt")
    (ys, s_in), carried['scan'] = _scan_fwd(xs, bc, dtb, dab, f"{tag}_scan", sides.get('scan'))
    yc = _ssd_post_fwd(ys, xs, pm, w['d_vec'], w['ssd_norm'], f"{tag}_ssd_post")
    (mg, y), carried['merge'] = _merge_fwd(pm, ya, o, yc, w['wb0'], w['wb1'], w['wb2'], w['w_out'], x, f"{tag}_merge",
                                           sides.get('merge'))
    return y, (x, h, pm, pt, ya, q, k, v, o, lse, xs, bc, dtb, dab, ys, s_in, yc, mg), carried


def _pair_sums(pending, got):
    return {n: _pair_add(pending[n], got[n], f"pair_add_{n}") for n in got}


def _chip_sums(sums, arrived, layer, stacked):
    for n in arrived:
        stacked[n] = _chip_add(sums[n][1], arrived[n], f"chip_add_{n}", layer, stacked.get(n))


def _reduce_to_chip(pending, layer, stacked):
    names = list(pending)
    got = _run_exchange(_pair_exchange([pending[n] for n in names]), "pair_exchange")
    sums = _pair_sums(pending, dict(zip(names, got)))
    arrived = _run_exchange(_chip_exchange([sums[n][0] for n in names]), "chip_exchange")
    _chip_sums(sums, dict(zip(names, arrived)), layer, stacked)


def _mixer_bwd(dy, saved, w, tabs, tag, sides=None):
    sides = sides or {}
    carried = {}
    x, h, pm, pt, ya, q, k, v, o, lse, xs, bc, dtb, dab, ys, s_in, yc, mg = saved
    g = {}
    g['w_out'] = _matmul(mg, dy, ta=True, name=f"{tag}_dwout").reshape((N_CHIPS, D_MODEL // N_CHIPS, D_MODEL))
    d0, d1, d2, dgates, dya, do, dyc = _merge_bwd(pm, ya, o, yc, w['wb0'], w['wb1'], w['wb2'], w['w_out'], dy, f"{tag}_dmerge")
    dwb0 = _matmul(ya, d0, ta=True, name=f"{tag}_dwb0")
    dwb1 = _matmul(o, d1, ta=True, name=f"{tag}_dwb1")
    dwb2 = _matmul(yc, d2, ta=True, name=f"{tag}_dwb2")
    g['w_branch'] = _split('w_branch', jnp.stack([dwb0, _heads_unpad(dwb1, MLA_V, 0), _heads_unpad(dwb2, SSD_HEAD_DIM, 0)]))
    duv, g['gm_v_norm'], g['gm_w_s'], db = _gmlp_bwd(pm, w['gm_v_norm'], w['gm_w_s'], w['gm_b_full'], dya, f"{tag}_dgmlp")
    g['gm_b_s'] = db.T
    (dq, delta), carried['dattn_q'] = _attn_bwd_dq(q, k, v, o, lse, do, f"{tag}_dattn_q", sides.get('dattn_q'))
    (dk, dv), carried['dattn_kv'] = _attn_bwd_dkv(q, k, v, lse, delta, do, f"{tag}_dattn_kv", sides.get('dattn_kv'))
    (dcq, dckv, dkr, dwuq, dwkv, g['mla_q_norm'], g['mla_kv_norm'], dgq, dgk), carried['dmla_pre'] = _mla_pre_bwd(
        pt, tabs, w['mla_q_norm'], w['mla_kv_norm'], w['wuq'], w['wkv'], w['gq'], w['gk'], dq, dk, dv, f"{tag}_dmla_pre",
        sides.get('dmla_pre'))
    g['mla_w_uq'] = _split('mla_w_uq', _heads_unpad(dwuq, MLA_QK, 1))
    dwk = dwkv[:, :HP].reshape(MLA_KV_RANK, MLA_HEADS, LANES)[:, :, :MLA_NOPE]
    dwv = dwkv[:, HP:].reshape(MLA_KV_RANK, MLA_HEADS, LANES)[:, :, :MLA_V]
    g['mla_w_ukv'] = _split('mla_w_ukv', jnp.concatenate([dwk, dwv], axis=2).reshape(MLA_KV_RANK, MLA_HEADS * (MLA_NOPE + MLA_V)))
    g['mla_q_gain'], g['mla_k_gain'] = dgq[:, :MLA_QK], dgk[:, :MLA_QK]
    dys, dz, dssd_norm, dd = _ssd_post_bwd(ys, xs, pm, w['d_vec'], w['ssd_norm'], dyc, f"{tag}_dssd_post")
    g['ssd_norm'] = _heads_unpad(dssd_norm, SSD_HEAD_DIM, 1)
    g['ssd_d'] = jnp.sum(dd.reshape(SSD_HEADS, LANES), axis=1)[None, :]
    dxs, dbm, dcm, dda, ddtx = _scan_bwd(xs, bc, dtb, dab, s_in, dys, w['d_vec'], f"{tag}_dscan")
    dxs16, dcw_x, dcb_x = _conv_bwd(pm, C_XS, HP, w['conv_w'][:, :HP], w['conv_b'][:, :HP], dxs, f"{tag}_dconv_x")
    dbc16, dcw_bc, dcb_bc = _conv_bwd(pt, T_BC, BCW, w['conv_w'][:, HP:], w['conv_b'][:, HP:],
                                      jnp.concatenate([dbm, dcm], axis=1), f"{tag}_dconv_bc")
    g['ssd_conv_w'] = _xbc_unpad(jnp.concatenate([dcw_x, dcw_bc], axis=1))
    g['ssd_conv_b'] = _xbc_unpad(jnp.concatenate([dcb_x, dcb_bc], axis=1))
    ddt, dbias, dalog = _dt_bwd(pt, w['dt_bias'], w['a_log'], dda, ddtx, f"{tag}_ddt")
    g['ssd_dt_bias'], g['ssd_a_log'] = dbias[:, :SSD_HEADS], dalog[:, :SSD_HEADS]
    s = x.shape[0]
    dpm = jnp.concatenate([duv, dz, dxs16, dgates], axis=1)
    dpt = jnp.concatenate([dbc16, dckv, dcq, dkr, ddt, jnp.zeros((s, PW_TAIL - T_DT - LANES), BF16)], axis=1)
    g['w_in'] = _split('w_in', _w_in_unpad(_matmul(h, dpm, ta=True, name=f"{tag}_dwin_main"),
                                           _matmul(h, dpt, ta=True, name=f"{tag}_dwin_tail")))
    dh = _matmul(dpt, w['w_in_tail'], tb=True, name=f"{tag}_dh_tail")
    dh = _matmul(dpm, w['w_in_main'], tb=True, res=dh, name=f"{tag}_dh_main")
    dx, g['mix_norm'] = _rmsnorm_bwd(x, w['mix_norm'], dh, dy, f"{tag}_dnorm")
    return dx, g, carried


_CONV_ROWS = 32


def _rows_cols(a, lead):
    return a.reshape(a.shape[:lead] + (int(np.prod(a.shape[lead:-1])), a.shape[-1]))


def _shard_views(wts):
    views = []
    for n in SHARDED_ORDER:
        a = _rows_cols(wts[n].astype(BF16), 1)
        if n == 'ssd_conv_w':
            a = jnp.pad(a, ((0, 0), (0, _CONV_ROWS - a.shape[1]), (0, 0)))
        views.append(a)
    return views


def _gathered(names, arrays):
    out = {}
    for n, a in zip(names, arrays):
        shp = _shard_shape(n)
        if n == 'ssd_conv_w':
            a = a[:, :shp[0]]
        out[n] = a.reshape((N_CHIPS,) + shp)
    return out


def _local_step(x, positions, target, weights, small, distributed=True):
    tabs = _rope_tables(positions)
    views = dict(zip(SHARDED_ORDER, weights)) if distributed else None
    plan = [{} for _ in range(DEPTH)]
    if distributed:
        for l in range(DEPTH - 1):
            plan[l].update({host: (names, l + 1) for host, names in (GATHER_HOSTS if l == 0 else GATHER_HOSTS_LATER).items()})
        plan[0].update({host: (names, 0) for host, names in FIRST_HOSTS.items()})
        have = [dict() for _ in range(DEPTH)]
        have[0].update(_gathered(FIRST_NOW, _run_exchange(_gather_exchange([views[n] for n in FIRST_NOW], 0), "gather_first")))
    else:
        have = weights

    def absorb(l, carried):
        for host, arrays in carried.items():
            if host in plan[l]:
                names, layer = plan[l][host]
                have[layer].update(_gathered(names, arrays))

    ws, saved = [], []
    for l in range(DEPTH):
        sides = {host: _gather_exchange([views[n] for n in names], layer) for host, (names, layer) in plan[l].items()}
        w = _layer_weights(have[l], small, l, 'ffn1')
        x, s1, carried = _ffn_fwd(x, w['ffn1_norm'], w['ffn1_w_in'], w['ffn1_w_out'], "ffn1", sides)
        absorb(l, carried)
        w.update(_layer_weights(have[l], small, l, 'mixer'))
        x, s2, carried = _mixer_fwd(x, w, tabs, "mix", sides)
        absorb(l, carried)
        w.update(_layer_weights(have[l], small, l, 'ffn2'))
        x, s3, carried = _ffn_fwd(x, w['ffn2_norm'], w['ffn2_w_in'], w['ffn2_w_out'], "ffn2", sides)
        absorb(l, carried)
        ws.append(w)
        saved.append((s1, s2, s3))
    dy, sq = _loss_head(x, target, "loss_head")
    loss = 0.5 * jnp.sum(sq) / D_MODEL
    grads, reduced, pending = [None] * DEPTH, {}, None

    def chip_sides(sums, hosts):
        return {host: _chip_exchange([sums[n][0] for n in names]) for host, names in hosts.items()}

    def arrivals(carried, hosts):
        return {n: a for host, names in hosts.items() for n, a in zip(names, carried[host])}

    for l in reversed(range(DEPTH)):
        w = ws[l]
        s1, s2, s3 = saved[l]
        sides = {host: _pair_exchange([pending[n] for n in names]) for host, names in PAIR_HOSTS.items()} if pending else {}
        dy, dn2, dwi2, dwo2, carried = _ffn_bwd(dy, s3, w['ffn2_norm'], w['ffn2_w_in'], w['ffn2_w_out'], "ffn2", sides)
        sides = {}
        if pending:
            sums = _pair_sums(pending, arrivals(carried, PAIR_HOSTS))
            sides = chip_sides(sums, REDUCE_HOSTS)
        dy, g, carried = _mixer_bwd(dy, s2, w, tabs, "mix", sides)
        if pending:
            _chip_sums(sums, arrivals(carried, REDUCE_HOSTS), l + 1, reduced)
        g.update(ffn2_norm=dn2, ffn2_w_in=dwi2, ffn2_w_out=dwo2)
        last = distributed and l == 0
        if last:
            early = {n: _rows_cols(g[n], 1) for n in LAST_EARLY}
            after = {}

            def after_dwout(got):
                after['sums'] = _pair_sums(early, dict(zip(LAST_EARLY, got)))
                return chip_sides(after['sums'], LAST_HOSTS)

            dy, dn1, dwi1, dwo1, carried = _ffn_bwd(dy, s1, w['ffn1_norm'], w['ffn1_w_in'], w['ffn1_w_out'], "ffn1",
                                                    {'ffn1_dwout': _pair_exchange([early[n] for n in LAST_EARLY])}, after_dwout)
            _chip_sums(after['sums'], arrivals(carried, LAST_HOSTS), 0, reduced)
        else:
            dy, dn1, dwi1, dwo1, _ = _ffn_bwd(dy, s1, w['ffn1_norm'], w['ffn1_w_in'], w['ffn1_w_out'], "ffn1")
        g.update(ffn1_norm=dn1, ffn1_w_in=dwi1, ffn1_w_out=dwo1)
        grads[l] = g
        if distributed:
            pending = {n: _rows_cols(g[n], 1) for n in REDUCED}
    if distributed:
        _reduce_to_chip({n: pending[n] for n in LAST_LATE}, 0, reduced)
    return loss, dy, grads, reduced


SMALL_PACK = SMALL_ORDER + ['ssd_conv_w']


def _pack_small(per_layer_rows, tail=None):
    parts = [per_layer_rows[l][n].reshape(-1).astype(F32) for l in range(DEPTH) for n in SMALL_PACK]
    if tail is not None:
        parts.append(tail.reshape(1))
    flat = jnp.concatenate(parts)
    rows = -(-flat.shape[0] // LANES)
    rows = -(-rows // 8) * 8
    return jnp.pad(flat, (0, rows * LANES - flat.shape[0])).reshape(rows, LANES)


def _unpack_small(buf, shapes):
    flat = buf.reshape(-1)
    off = 0
    out = {n: [] for n in SMALL_PACK}
    for l in range(DEPTH):
        for n in SMALL_PACK:
            size = int(np.prod(shapes[n]))
            out[n].append(flat[off:off + size].reshape(shapes[n]))
            off += size
    return {n: jnp.stack(v) for n, v in out.items()}


def kernel(x, positions, ffn1_norm, ffn1_w_in, ffn1_w_out, mix_norm, w_in, gm_v_norm, gm_w_s, gm_b_s, mla_q_norm, mla_kv_norm, mla_w_uq, mla_w_ukv, mla_q_gain, mla_k_gain, ssd_conv_w, ssd_conv_b, ssd_dt_bias, ssd_a_log, ssd_d, ssd_norm, w_branch, w_out, ffn2_norm, ffn2_w_in, ffn2_w_out, loss_target, m_ffn1_norm, m_ffn1_w_in, m_ffn1_w_out, m_mix_norm, m_w_in, m_gm_v_norm, m_gm_w_s, m_gm_b_s, m_mla_q_norm, m_mla_kv_norm, m_mla_w_uq, m_mla_w_ukv, m_mla_q_gain, m_mla_k_gain, m_ssd_conv_w, m_ssd_conv_b, m_ssd_dt_bias, m_ssd_a_log, m_ssd_d, m_ssd_norm, m_w_branch, m_w_out, m_ffn2_norm, m_ffn2_w_in, m_ffn2_w_out, v_ffn1_norm, v_ffn1_w_in, v_ffn1_w_out, v_mix_norm, v_w_in, v_gm_v_norm, v_gm_w_s, v_gm_b_s, v_mla_q_norm, v_mla_kv_norm, v_mla_w_uq, v_mla_w_ukv, v_mla_q_gain, v_mla_k_gain, v_ssd_conv_w, v_ssd_conv_b, v_ssd_dt_bias, v_ssd_a_log, v_ssd_d, v_ssd_norm, v_w_branch, v_w_out, v_ffn2_norm, v_ffn2_w_in, v_ffn2_w_out):
    wts = dict(zip(WEIGHTS, (ffn1_norm, ffn1_w_in, ffn1_w_out, mix_norm, w_in, gm_v_norm, gm_w_s, gm_b_s, mla_q_norm, mla_kv_norm,
                             mla_w_uq, mla_w_ukv, mla_q_gain, mla_k_gain, ssd_conv_w, ssd_conv_b, ssd_dt_bias, ssd_a_log, ssd_d,
                             ssd_norm, w_branch, w_out, ffn2_norm, ffn2_w_in, ffn2_w_out)))
    mom = dict(zip(WEIGHTS, (m_ffn1_norm, m_ffn1_w_in, m_ffn1_w_out, m_mix_norm, m_w_in, m_gm_v_norm, m_gm_w_s, m_gm_b_s, m_mla_q_norm,
                             m_mla_kv_norm, m_mla_w_uq, m_mla_w_ukv, m_mla_q_gain, m_mla_k_gain, m_ssd_conv_w, m_ssd_conv_b,
                             m_ssd_dt_bias, m_ssd_a_log, m_ssd_d, m_ssd_norm, m_w_branch, m_w_out, m_ffn2_norm, m_ffn2_w_in,
                             m_ffn2_w_out)))
    var = dict(zip(WEIGHTS, (v_ffn1_norm, v_ffn1_w_in, v_ffn1_w_out, v_mix_norm, v_w_in, v_gm_v_norm, v_gm_w_s, v_gm_b_s, v_mla_q_norm,
                             v_mla_kv_norm, v_mla_w_uq, v_mla_w_ukv, v_mla_q_gain, v_mla_k_gain, v_ssd_conv_w, v_ssd_conv_b,
                             v_ssd_dt_bias, v_ssd_a_log, v_ssd_d, v_ssd_norm, v_w_branch, v_w_out, v_ffn2_norm, v_ffn2_w_in,
                             v_ffn2_w_out)))
    cx, cy, _ = _me()
    mychip = _chip_index(cx, cy)

    small = {n: wts[n] for n in SMALL_ORDER}
    loss_part, dx, grads, reduced = _local_step(x[0], positions[0], loss_target[0], _shard_views(wts), small)
    rows_cols = _rows_cols
    halves = [reduced[n] for n in REDUCED]
    theirs = _run_exchange(_pair_share(halves), "pair_share")
    grad, delta, new_m, new_v = {}, {}, {}, {}
    everyone = _all_exchange(_pack_small(grads, tail=loss_part))
    for n, a, b in zip(REDUCED, halves, theirs):
        shp = wts[n].shape
        outs, carried = _adamw_sharded(rows_cols(wts[n], 1), rows_cols(mom[n], 1), rows_cols(var[n], 1), a, b, f"adamw_{n}",
                                       everyone if n == REDUCED[0] else None)
        if n == REDUCED[0]:
            partials = carried[0]
        grad[n], delta[n], new_m[n], new_v[n] = [o.reshape(shp) for o in outs]
    shapes = {n: wts[n].shape[1:] for n in SMALL_ORDER}
    shapes['ssd_conv_w'] = SHARDED['ssd_conv_w'][0]
    summed = _sum_slots(partials, "small_sum")
    small_g = _unpack_small(summed, shapes)
    loss = summed.reshape(-1)[DEPTH * sum(int(np.prod(shapes[n])) for n in SMALL_PACK)]
    conv_full = small_g.pop('ssd_conv_w')
    shard_cols = _shard_shape('ssd_conv_w')[1]
    small_g['ssd_conv_w'] = lax.dynamic_slice_in_dim(conv_full, mychip * shard_cols, shard_cols, axis=2)
    shapes['ssd_conv_w'] = _shard_shape('ssd_conv_w')

    per_layer = lambda t: [{n: t[n][l] for n in SMALL_PACK} for l in range(DEPTH)]
    d, nm, nv = _adamw(_pack_small(per_layer(wts)), _pack_small(per_layer(small_g)), _pack_small(per_layer(mom)),
                       _pack_small(per_layer(var)), "adamw_small")
    sd, snm, snv = _unpack_small(d, shapes), _unpack_small(nm, shapes), _unpack_small(nv, shapes)
    for n in SMALL_PACK:
        grad[n], delta[n], new_m[n], new_v[n] = small_g[n], sd[n], snm[n], snv[n]
    return (loss, dx[None], *[grad[n] for n in WEIGHTS], *[delta[n] for n in WEIGHTS], *[new_m[n] for n in WEIGHTS],
            *[new_v[n] for n in WEIGHTS])
```

```python
import functools
import math

import numpy as np
import jax
import jax.numpy as jnp
from jax import lax
from jax.experimental import pallas as pl
from jax.experimental.pallas import tpu as pltpu

F32, BF16 = jnp.float32, jnp.bfloat16
MESH = pl.DeviceIdType.MESH

D_MODEL, DEPTH, D_FF, EPS = 1024, 4, 2816, 1e-6
GM_WIDTH, GM_GROUPS, CHUNK = 512, 4, 128
MLA_HEADS, MLA_Q_RANK, MLA_KV_RANK, MLA_NOPE, MLA_ROPE, MLA_V = 8, 384, 256, 64, 32, 64
MLA_QK = MLA_NOPE + MLA_ROPE
ROPE_THETA = 10000.0
SSD_HEADS, SSD_HEAD_DIM, SSD_GROUPS, SSD_STATE, SSD_CONV = 8, 64, 2, 128, 4
SSD_INNER = SSD_HEADS * SSD_HEAD_DIM
IN_COLS = 6312
LANES = 128
ADAM_LR, ADAM_B1, ADAM_B2, ADAM_EPS, ADAM_WD, ADAM_STEP = 0.001, 0.9, 0.999, 1e-08, 0.01, 10

C_UV, C_Z, C_XS, C_G, PW_MAIN = 0, 1024, 2048, 3072, 6144
T_BC, T_CKV, T_CQ, T_KR, T_DT, PW_TAIL = 0, 512, 768, 1152, 1280, 1536
HP = MLA_HEADS * LANES
FC = 2 * D_FF // 4

WEIGHTS = ['ffn1_norm', 'ffn1_w_in', 'ffn1_w_out', 'mix_norm', 'w_in', 'gm_v_norm', 'gm_w_s', 'gm_b_s', 'mla_q_norm',
           'mla_kv_norm', 'mla_w_uq', 'mla_w_ukv', 'mla_q_gain', 'mla_k_gain', 'ssd_conv_w', 'ssd_conv_b', 'ssd_dt_bias',
           'ssd_a_log', 'ssd_d', 'ssd_norm', 'w_branch', 'w_out', 'ffn2_norm', 'ffn2_w_in', 'ffn2_w_out']
SHARDED = {'ffn1_w_in': ((1024, 5632), 1), 'ffn1_w_out': ((2816, 1024), 0), 'w_in': ((1024, 6312), 1),
           'mla_w_uq': ((384, 768), 1), 'mla_w_ukv': ((256, 1024), 1), 'ssd_conv_w': ((4, 1024), 1),
           'w_branch': ((3, 512, 1024), 2), 'w_out': ((1024, 1024), 0), 'ffn2_w_in': ((1024, 5632), 1),
           'ffn2_w_out': ((2816, 1024), 0)}
SHARDED_ORDER = [n for n in WEIGHTS if n in SHARDED]
SMALL_ORDER = [n for n in WEIGHTS if n not in SHARDED]
REDUCED = [n for n in SHARDED_ORDER if n != 'ssd_conv_w']
N_CHIPS = 4
HALF_L = DEPTH // 2


def _shard_shape(name):
    shape, ax = SHARDED[name]
    return tuple(d // N_CHIPS if i == ax else d for i, d in enumerate(shape))


def _pick(dim, target):
    if dim <= target:
        return dim
    t = (target // LANES) * LANES
    while t >= LANES:
        if dim % t == 0:
            return t
        t -= LANES
    return dim


def _sigmoid(x):
    return 1.0 / (1.0 + jnp.exp(-x))


def _params(*sem):
    return pltpu.CompilerParams(dimension_semantics=sem, vmem_limit_bytes=56 * 1024 * 1024)


def _matmul(a, b, *, ta=False, tb=False, out_dtype=F32, scale=1.0, res=None, name, side=None):
    if ta:
        k_dim, m_dim = a.shape
    else:
        m_dim, k_dim = a.shape
    if tb:
        n_dim, k2 = b.shape
    else:
        k2, n_dim = b.shape
    assert k_dim == k2, (a.shape, b.shape, ta, tb)
    tm, tn, tk = _pick(m_dim, 1024), _pick(n_dim, 1024), _pick(k_dim, 1024)
    nk = k_dim // tk
    dn = (((0 if ta else 1,), (1 if tb else 0,)), ((), ()))

    def body(*refs):
        if res is not None:
            a_ref, b_ref, r_ref, o_ref, acc = refs
        else:
            a_ref, b_ref, o_ref, acc = refs
        k = pl.program_id(2)

        @pl.when(k == 0)
        def _():
            acc[...] = jnp.zeros_like(acc)

        acc[...] += lax.dot_general(a_ref[...].astype(BF16), b_ref[...].astype(BF16), dn, preferred_element_type=F32)

        @pl.when(k == nk - 1)
        def _():
            r = acc[...]
            if scale != 1.0:
                r = r * scale
            if res is not None:
                r = r + r_ref[...]
            o_ref[...] = r.astype(out_dtype)

    a_spec = pl.BlockSpec((tk, tm), lambda j, i, k: (k, i)) if ta else pl.BlockSpec((tm, tk), lambda j, i, k: (i, k))
    b_spec = pl.BlockSpec((tn, tk), lambda j, i, k: (j, k)) if tb else pl.BlockSpec((tk, tn), lambda j, i, k: (k, j))
    in_specs = [a_spec, b_spec]
    args = [a, b]
    if res is not None:
        in_specs.append(pl.BlockSpec((tm, tn), lambda j, i, k: (i, j)))
        args.append(res)
    (out,), carried = _call(
        body, name=name, grid=(n_dim // tn, m_dim // tm, nk), in_specs=in_specs,
        out_specs=[pl.BlockSpec((tm, tn), lambda j, i, k: (i, j))],
        out_shape=[jax.ShapeDtypeStruct((m_dim, n_dim), out_dtype)],
        scratch_shapes=[pltpu.VMEM((tm, tn), F32)], args=args, semantics=("parallel", "parallel", "arbitrary"), side=side)
    return out if side is None else (out, carried)


_NT = (((1,), (1,)), ((), ()))
_TN = (((0,), (0,)), ((), ()))


def _resident(shape):
    return pl.BlockSpec(shape, lambda *_: tuple(0 for _ in shape), pipeline_mode=pl.Buffered(1))


def _ffn_in(x, gain, w4, name, side=None):
    s, d = x.shape
    tm = _pick(s, 512)

    def body(x_ref, g_ref, w_ref, h_ref, gate_ref, up_ref, act_ref):
        xv = x_ref[...]
        r = lax.rsqrt(jnp.mean(xv * xv, axis=-1, keepdims=True) + EPS)
        h = (xv * r * g_ref[...]).astype(BF16)
        h_ref[...] = h
        for j in range(2):
            g16 = jnp.dot(h, w_ref[j], preferred_element_type=F32).astype(BF16)
            u16 = jnp.dot(h, w_ref[j + 2], preferred_element_type=F32).astype(BF16)
            gate_ref[j] = g16
            up_ref[j] = u16
            gf, uf = g16.astype(F32), u16.astype(F32)
            act_ref[j] = (gf * _sigmoid(gf) * uf).astype(BF16)

    half = pl.BlockSpec((2, tm, FC), lambda i: (0, i, 0))
    return _call(
        body, name=name, grid=(s // tm,),
        in_specs=[pl.BlockSpec((tm, d), lambda i: (i, 0)), pl.BlockSpec((1, d), lambda i: (0, 0)), _resident((4, d, FC))],
        out_specs=[pl.BlockSpec((tm, d), lambda i: (i, 0)), half, half, half],
        out_shape=[jax.ShapeDtypeStruct((s, d), BF16)] + [jax.ShapeDtypeStruct((2, s, FC), BF16)] * 3,
        scratch_shapes=[], args=(x, gain, w4), semantics=("parallel",), side=side)


def _ffn_out(act, w_out, x, name, side=None):
    s, d = x.shape
    tm = _pick(s, 512)

    def body(a_ref, w_ref, x_ref, o_ref):
        acc = jnp.dot(a_ref[0], w_ref[0:FC, :], preferred_element_type=F32)
        acc = acc + jnp.dot(a_ref[1], w_ref[FC:2 * FC, :], preferred_element_type=F32)
        o_ref[...] = x_ref[...] + 0.5 * acc

    row = pl.BlockSpec((tm, d), lambda i: (i, 0))
    (out,), carried = _call(
        body, name=name, grid=(s // tm,),
        in_specs=[pl.BlockSpec((2, tm, FC), lambda i: (0, i, 0)), _resident((2 * FC, d)), row], out_specs=[row],
        out_shape=[jax.ShapeDtypeStruct((s, d), F32)], scratch_shapes=[], args=(act, w_out, x), semantics=("parallel",),
        side=side)
    return out, carried


def _ffn_dact(dy, w_out, gate, up, name, side=None):
    s, d = dy.shape
    tm = _pick(s, 512)

    def body(dy_ref, w_ref, g_ref, u_ref, o_ref):
        dy16 = dy_ref[...].astype(BF16)
        for j in range(2):
            dact = 0.5 * lax.dot_general(dy16, w_ref[j * FC:(j + 1) * FC, :], _NT, preferred_element_type=F32)
            g, u = g_ref[j].astype(F32), u_ref[j].astype(F32)
            sg = _sigmoid(g)
            o_ref[j] = (dact * u * (sg * (1.0 + g * (1.0 - sg)))).astype(BF16)
            o_ref[j + 2] = (dact * g * sg).astype(BF16)

    half = pl.BlockSpec((2, tm, FC), lambda i: (0, i, 0))
    (out,), carried = _call(
        body, name=name, grid=(s // tm,),
        in_specs=[pl.BlockSpec((tm, d), lambda i: (i, 0)), _resident((2 * FC, d)), half, half],
        out_specs=[pl.BlockSpec((4, tm, FC), lambda i: (0, i, 0))],
        out_shape=[jax.ShapeDtypeStruct((4, s, FC), BF16)], scratch_shapes=[], args=(dy, w_out, gate, up),
        semantics=("parallel",), side=side)
    return out, carried


def _ffn_dwout(act, dy, name, side=None):
    s, d = dy.shape
    tk = _pick(s, 1024)
    nk = s // tk

    def body(a_ref, dy_ref, o_ref):
        k = pl.program_id(1)

        @pl.when(k == 0)
        def _():
            o_ref[...] = jnp.zeros_like(o_ref)

        o_ref[...] += lax.dot_general(a_ref[...], dy_ref[...].astype(BF16), _TN, preferred_element_type=F32)

        @pl.when(k == nk - 1)
        def _():
            o_ref[...] = 0.5 * o_ref[...]

    (out,), carried = _call(
        body, name=name, grid=(2, nk),
        in_specs=[pl.BlockSpec((None, tk, FC), lambda j, k: (j, k, 0)), pl.BlockSpec((tk, d), lambda j, k: (k, 0))],
        out_specs=[pl.BlockSpec((FC, d), lambda j, k: (j, 0))], out_shape=[jax.ShapeDtypeStruct((2 * FC, d), F32)],
        scratch_shapes=[], args=(act, dy), semantics=("parallel", "arbitrary"), side=side)
    return out, carried


def _ffn_dwin(h, da, name, side=None):
    s, d = h.shape
    tk = _pick(s, 1024)

    def body(h_ref, da_ref, o_ref):
        @pl.when(pl.program_id(1) == 0)
        def _():
            o_ref[...] = jnp.zeros_like(o_ref)

        o_ref[...] += lax.dot_general(h_ref[...], da_ref[...], _TN, preferred_element_type=F32)

    (out,), carried = _call(
        body, name=name, grid=(4, s // tk),
        in_specs=[pl.BlockSpec((tk, d), lambda j, k: (k, 0)), pl.BlockSpec((None, tk, FC), lambda j, k: (j, k, 0))],
        out_specs=[pl.BlockSpec((None, d, FC), lambda j, k: (j, 0, 0))], out_shape=[jax.ShapeDtypeStruct((4, d, FC), F32)],
        scratch_shapes=[], args=(h, da), semantics=("parallel", "arbitrary"), side=side)
    return out, carried


def _ffn_dx(da, w4, x, gain, dy, name, side=None):
    s, d = x.shape
    tm = _pick(s, 512)

    def body(da_ref, w_ref, x_ref, g_ref, dy_ref, dx_ref, dg_ref):
        @pl.when(pl.program_id(0) == 0)
        def _():
            dg_ref[...] = jnp.zeros_like(dg_ref)

        dh = jnp.zeros((tm, d), F32)
        for j in range(4):
            dh = dh + lax.dot_general(da_ref[j], w_ref[j], _NT, preferred_element_type=F32)
        xv = x_ref[...]
        r = lax.rsqrt(jnp.mean(xv * xv, axis=-1, keepdims=True) + EPS)
        u = dh * g_ref[...]
        dx_ref[...] = dy_ref[...] + r * u - xv * (r * r * r) * jnp.mean(xv * u, axis=-1, keepdims=True)
        dg_ref[...] += jnp.sum(dh * xv * r, axis=0, keepdims=True)

    row = pl.BlockSpec((tm, d), lambda i: (i, 0))
    vec = pl.BlockSpec((1, d), lambda i: (0, 0))
    return _call(
        body, name=name, grid=(s // tm,),
        in_specs=[pl.BlockSpec((4, tm, FC), lambda i: (0, i, 0)), _resident((4, d, FC)), row, vec, row],
        out_specs=[row, vec], out_shape=[jax.ShapeDtypeStruct((s, d), F32), jax.ShapeDtypeStruct((1, d), F32)],
        scratch_shapes=[], args=(da, w4, x, gain, dy), semantics=("arbitrary",), side=side)


def _mixer_proj(x, gain, w_main, w_tail, name, side=None):
    s, d = x.shape
    tm = _pick(s, 512)

    def body(x_ref, g_ref, wm_ref, wt_ref, h_ref, pm_ref, pt_ref):
        xv = x_ref[...]
        r = lax.rsqrt(jnp.mean(xv * xv, axis=-1, keepdims=True) + EPS)
        h = (xv * r * g_ref[...]).astype(BF16)
        h_ref[...] = h
        pm_ref[...] = jnp.dot(h, wm_ref[...], preferred_element_type=F32).astype(BF16)
        pt_ref[...] = jnp.dot(h, wt_ref[...], preferred_element_type=F32)

    row = lambda width: pl.BlockSpec((tm, width), lambda i: (i, 0))
    return _call(
        body, name=name, grid=(s // tm,),
        in_specs=[row(d), pl.BlockSpec((1, d), lambda i: (0, 0)), _resident((d, PW_MAIN)), _resident((d, PW_TAIL))],
        out_specs=[row(d), row(PW_MAIN), row(PW_TAIL)],
        out_shape=[jax.ShapeDtypeStruct((s, d), BF16), jax.ShapeDtypeStruct((s, PW_MAIN), BF16),
                   jax.ShapeDtypeStruct((s, PW_TAIL), F32)],
        scratch_shapes=[], args=(x, gain, w_main, w_tail), semantics=("parallel",), side=side)


def _mixer_dx(dpm, dpt, w_main, w_tail, x, gain, dy, name):
    s, d = x.shape
    tm = _pick(s, 512)

    def body(dpm_ref, dpt_ref, wm_ref, wt_ref, x_ref, g_ref, dy_ref, dx_ref, dg_ref):
        @pl.when(pl.program_id(0) == 0)
        def _():
            dg_ref[...] = jnp.zeros_like(dg_ref)

        dh = lax.dot_general(dpm_ref[...], wm_ref[...], _NT, preferred_element_type=F32)
        dh = dh + lax.dot_general(dpt_ref[...], wt_ref[...], _NT, preferred_element_type=F32)
        xv = x_ref[...]
        r = lax.rsqrt(jnp.mean(xv * xv, axis=-1, keepdims=True) + EPS)
        u = dh * g_ref[...]
        dx_ref[...] = dy_ref[...] + r * u - xv * (r * r * r) * jnp.mean(xv * u, axis=-1, keepdims=True)
        dg_ref[...] += jnp.sum(dh * xv * r, axis=0, keepdims=True)

    row = lambda width: pl.BlockSpec((tm, width), lambda i: (i, 0))
    vec = pl.BlockSpec((1, d), lambda i: (0, 0))
    return pl.pallas_call(
        body, name=name, grid=(s // tm,),
        in_specs=[row(PW_MAIN), row(PW_TAIL), _resident((d, PW_MAIN)), _resident((d, PW_TAIL)), row(d), vec, row(d)],
        out_specs=[row(d), vec], out_shape=[jax.ShapeDtypeStruct((s, d), F32), jax.ShapeDtypeStruct((1, d), F32)],
        compiler_params=_params("arbitrary"))(dpm, dpt, w_main, w_tail, x, gain, dy)


_INV_SQRT2 = 0.7071067811865476
_INV_SQRT2PI = 0.3989422804014327


def _gelu(x):
    return 0.5 * x * (1.0 + lax.erf(x * _INV_SQRT2))


def _gelu_grad(x):
    return 0.5 * (1.0 + lax.erf(x * _INV_SQRT2)) + x * jnp.exp(-0.5 * x * x) * _INV_SQRT2PI


def _tril_mask():
    r = lax.broadcasted_iota(jnp.int32, (CHUNK, CHUNK), 0)
    c = lax.broadcasted_iota(jnp.int32, (CHUNK, CHUNK), 1)
    return r >= c


def _gmlp_fwd(p, v_gain, w_s, b_full, name):
    s = p.shape[0]
    tm = _pick(s, 512)
    nch = tm // CHUNK

    def body(uv_ref, g_ref, w_ref, b_ref, o_ref):
        gel = _gelu(uv_ref[...].astype(F32))
        u, v = gel[:, :GM_WIDTH], gel[:, GM_WIDTH:]
        r = lax.rsqrt(jnp.mean(v * v, axis=-1, keepdims=True) + EPS)
        vn = (v * r * g_ref[...]).astype(BF16)
        mask = _tril_mask()
        for g in range(GM_GROUPS):
            wm = jnp.where(mask, w_ref[g], 0.0).astype(BF16)
            for c in range(nch):
                rs, cs = slice(c * CHUNK, (c + 1) * CHUNK), slice(g * LANES, (g + 1) * LANES)
                sp = jnp.dot(wm, vn[rs, cs], preferred_element_type=F32) + b_ref[g]
                o_ref[rs, cs] = (u[rs, cs] * sp).astype(BF16)

    full3 = pl.BlockSpec((GM_GROUPS, CHUNK, CHUNK), lambda i: (0, 0, 0))
    return pl.pallas_call(
        body, name=name, grid=(s // tm,),
        in_specs=[pl.BlockSpec((tm, 2 * GM_WIDTH), lambda i: (i, C_UV // (2 * GM_WIDTH))),
                  pl.BlockSpec((1, GM_WIDTH), lambda i: (0, 0)), full3, full3],
        out_specs=pl.BlockSpec((tm, GM_WIDTH), lambda i: (i, 0)),
        out_shape=jax.ShapeDtypeStruct((s, GM_WIDTH), BF16), compiler_params=_params("parallel"))(p, v_gain, w_s, b_full)


def _gmlp_bwd(p, v_gain, w_s, b_full, dy, name):
    s = p.shape[0]
    tm = _pick(s, 512)
    nch = tm // CHUNK
    nsteps = s // tm

    def body(uv_ref, g_ref, w_ref, b_ref, dy_ref, duv_ref, dg_ref, dw_ref, db_ref, dvn_s, dbacc):
        step = pl.program_id(0)

        @pl.when(step == 0)
        def _():
            dg_ref[...] = jnp.zeros_like(dg_ref)
            dw_ref[...] = jnp.zeros_like(dw_ref)
            dbacc[...] = jnp.zeros_like(dbacc)

        uv = uv_ref[...].astype(F32)
        gel = _gelu(uv)
        u, v = gel[:, :GM_WIDTH], gel[:, GM_WIDTH:]
        r = lax.rsqrt(jnp.mean(v * v, axis=-1, keepdims=True) + EPS)
        gain = g_ref[...]
        vn32 = v * r * gain
        vn = vn32.astype(BF16)
        dy = dy_ref[...].astype(F32)
        mask = _tril_mask()
        for g in range(GM_GROUPS):
            wm = jnp.where(mask, w_ref[g], 0.0).astype(BF16)
            dwg = jnp.zeros((CHUNK, CHUNK), F32)
            dbg = jnp.zeros((CHUNK, LANES), F32)
            for c in range(nch):
                rs, cs = slice(c * CHUNK, (c + 1) * CHUNK), slice(g * LANES, (g + 1) * LANES)
                sp = jnp.dot(wm, vn[rs, cs], preferred_element_type=F32) + b_ref[g]
                dyc = dy[rs, cs]
                dsp = dyc * u[rs, cs]
                dsp16 = dsp.astype(BF16)
                duv_ref[rs, cs] = (dyc * sp * _gelu_grad(uv[rs, cs])).astype(BF16)
                dvn_s[rs, cs] = lax.dot_general(wm, dsp16, (((0,), (0,)), ((), ())), preferred_element_type=F32)
                dwg = dwg + lax.dot_general(dsp16, vn[rs, cs], (((1,), (1,)), ((), ())), preferred_element_type=F32)
                dbg = dbg + dsp
            dw_ref[g] += jnp.where(mask, dwg, 0.0)
            dbacc[:, g * LANES:(g + 1) * LANES] += dbg
        dvn = dvn_s[...]
        uu = dvn * gain
        dv = r * uu - v * (r * r * r) * jnp.mean(v * uu, axis=-1, keepdims=True)
        duv_ref[:, GM_WIDTH:] = (dv * _gelu_grad(uv[:, GM_WIDTH:])).astype(BF16)
        dg_ref[...] += jnp.sum(dvn * v * r, axis=0, keepdims=True)

        @pl.when(step == nsteps - 1)
        def _():
            for g in range(GM_GROUPS):
                db_ref[:, g:g + 1] = jnp.sum(dbacc[:, g * LANES:(g + 1) * LANES], axis=1, keepdims=True)

    full3 = pl.BlockSpec((GM_GROUPS, CHUNK, CHUNK), lambda i: (0, 0, 0))
    return pl.pallas_call(
        body, name=name, grid=(nsteps,),
        in_specs=[pl.BlockSpec((tm, 2 * GM_WIDTH), lambda i: (i, C_UV // (2 * GM_WIDTH))),
                  pl.BlockSpec((1, GM_WIDTH), lambda i: (0, 0)), full3, full3,
                  pl.BlockSpec((tm, GM_WIDTH), lambda i: (i, 0))],
        out_specs=[pl.BlockSpec((tm, 2 * GM_WIDTH), lambda i: (i, 0)), pl.BlockSpec((1, GM_WIDTH), lambda i: (0, 0)),
                   full3, pl.BlockSpec((CHUNK, GM_GROUPS), lambda i: (0, 0))],
        out_shape=[jax.ShapeDtypeStruct((s, 2 * GM_WIDTH), BF16), jax.ShapeDtypeStruct((1, GM_WIDTH), F32),
                   jax.ShapeDtypeStruct((GM_GROUPS, CHUNK, CHUNK), F32), jax.ShapeDtypeStruct((CHUNK, GM_GROUPS), F32)],
        scratch_shapes=[pltpu.VMEM((tm, GM_WIDTH), F32), pltpu.VMEM((CHUNK, GM_WIDTH), F32)],
        compiler_params=_params("arbitrary"))(p, v_gain, w_s, b_full, dy)


def _rope(x, ct, s1, s2):
    return x * ct + pltpu.roll(x, LANES - MLA_ROPE // 2, 1) * s1 + pltpu.roll(x, MLA_ROPE // 2, 1) * s2


def _rope_bwd(d, ct, s1, s2):
    return d * ct + pltpu.roll(d * s1, MLA_ROPE // 2, 1) + pltpu.roll(d * s2, LANES - MLA_ROPE // 2, 1)


def _head_norm(x, gain):
    r = lax.rsqrt(jnp.sum(x * x, axis=-1, keepdims=True) * (1.0 / MLA_QK) + EPS)
    return x * r * gain, r


def _head_norm_bwd(x, r, gain, d):
    u = d * gain
    return r * u - x * (r * r * r) * (jnp.sum(x * u, axis=-1, keepdims=True) * (1.0 / MLA_QK))


def _mla_specs(tm):
    cq = pl.BlockSpec((tm, MLA_Q_RANK), lambda i: (i, T_CQ // MLA_Q_RANK))
    ckv = pl.BlockSpec((tm, MLA_KV_RANK), lambda i: (i, T_CKV // MLA_KV_RANK))
    kr = pl.BlockSpec((tm, LANES), lambda i: (i, T_KR // LANES))
    tab = pl.BlockSpec((tm, LANES), lambda i: (i, 0))
    return cq, ckv, kr, tab


def _const(shape):
    return pl.BlockSpec(shape, lambda i: tuple(0 for _ in shape))


def _mla_pre_fwd(p, tabs, qn_g, kvn_g, wuq, wkv, gq, gk, name):
    s = p.shape[0]
    tm = _pick(s, 256)
    ct, s1, s2 = tabs

    def body(cq_ref, ckv_ref, kr_ref, ct_ref, s1_ref, s2_ref, qg_ref, kvg_ref, wuq_ref, wkv_ref, gq_ref, gk_ref,
             q_ref, k_ref, v_ref):
        cq, ckv, kr = cq_ref[...], ckv_ref[...], kr_ref[...]
        ctv, s1v, s2v = ct_ref[...], s1_ref[...], s2_ref[...]
        rq = lax.rsqrt(jnp.mean(cq * cq, axis=-1, keepdims=True) + EPS)
        q = jnp.dot((cq * rq * qg_ref[...]).astype(BF16), wuq_ref[...], preferred_element_type=F32)
        rk = lax.rsqrt(jnp.mean(ckv * ckv, axis=-1, keepdims=True) + EPS)
        kv = jnp.dot((ckv * rk * kvg_ref[...]).astype(BF16), wkv_ref[...], preferred_element_type=F32)
        v_ref[...] = kv[:, HP:].astype(BF16)
        for h in range(MLA_HEADS):
            hs = slice(h * LANES, (h + 1) * LANES)
            qh, _ = _head_norm(q[:, hs], gq_ref[...])
            q_ref[:, hs] = (_rope(qh, ctv, s1v, s2v) * _Q_SCALE).astype(BF16)
            kh, _ = _head_norm(kv[:, hs] + kr, gk_ref[...])
            k_ref[:, hs] = _rope(kh, ctv, s1v, s2v).astype(BF16)

    cq_s, ckv_s, kr_s, tab_s = _mla_specs(tm)
    out = pl.BlockSpec((tm, HP), lambda i: (i, 0))
    return pl.pallas_call(
        body, name=name, grid=(s // tm,),
        in_specs=[cq_s, ckv_s, kr_s, tab_s, tab_s, tab_s, _const((1, MLA_Q_RANK)), _const((1, MLA_KV_RANK)),
                  _const((MLA_Q_RANK, HP)), _const((MLA_KV_RANK, 2 * HP)), _const((1, LANES)), _const((1, LANES))],
        out_specs=[out, out, out], out_shape=[jax.ShapeDtypeStruct((s, HP), BF16)] * 3,
        compiler_params=_params("parallel"))(p, p, p, ct, s1, s2, qn_g, kvn_g, wuq, wkv, gq, gk)


def _mla_pre_bwd(p, tabs, qn_g, kvn_g, wuq, wkv, gq, gk, dq, dk, dv, name, side=None):
    s = p.shape[0]
    tm = _pick(s, 256)
    ct, s1, s2 = tabs

    def body(cq_ref, ckv_ref, kr_ref, ct_ref, s1_ref, s2_ref, qg_ref, kvg_ref, wuq_ref, wkv_ref, gq_ref, gk_ref,
             dq_ref, dk_ref, dv_ref, dcq_ref, dckv_ref, dkr_ref, dwuq_ref, dwkv_ref, dqg_ref, dkvg_ref, dgq_ref, dgk_ref,
             dqp, dkvp):
        @pl.when(pl.program_id(0) == 0)
        def _():
            for ref in (dwuq_ref, dwkv_ref, dqg_ref, dkvg_ref, dgq_ref, dgk_ref):
                ref[...] = jnp.zeros_like(ref)

        cq, ckv, kr = cq_ref[...], ckv_ref[...], kr_ref[...]
        ctv, s1v, s2v = ct_ref[...], s1_ref[...], s2_ref[...]
        rq = lax.rsqrt(jnp.mean(cq * cq, axis=-1, keepdims=True) + EPS)
        qn = (cq * rq * qg_ref[...]).astype(BF16)
        q = jnp.dot(qn, wuq_ref[...], preferred_element_type=F32)
        rk = lax.rsqrt(jnp.mean(ckv * ckv, axis=-1, keepdims=True) + EPS)
        kvn = (ckv * rk * kvg_ref[...]).astype(BF16)
        kv = jnp.dot(kvn, wkv_ref[...], preferred_element_type=F32)
        gqv, gkv = gq_ref[...], gk_ref[...]
        dgq = jnp.zeros((1, LANES), F32)
        dgk = jnp.zeros((1, LANES), F32)
        dkr = jnp.zeros((tm, LANES), F32)
        for h in range(MLA_HEADS):
            hs = slice(h * LANES, (h + 1) * LANES)
            xq = q[:, hs]
            _, r = _head_norm(xq, gqv)
            d = _rope_bwd(dq_ref[:, hs].astype(F32), ctv, s1v, s2v)
            dgq = dgq + jnp.sum(d * xq * r, axis=0, keepdims=True)
            dqp[:, hs] = _head_norm_bwd(xq, r, gqv, d)
            xk = kv[:, hs] + kr
            _, r = _head_norm(xk, gkv)
            d = _rope_bwd(dk_ref[:, hs].astype(F32), ctv, s1v, s2v)
            dgk = dgk + jnp.sum(d * xk * r, axis=0, keepdims=True)
            dxk = _head_norm_bwd(xk, r, gkv, d)
            dkvp[:, hs] = dxk
            dkr = dkr + dxk
        dkvp[:, HP:] = dv_ref[...].astype(F32)
        dgq_ref[...] += dgq
        dgk_ref[...] += dgk
        dkr_ref[...] = dkr.astype(BF16)
        tn = (((0,), (0,)), ((), ()))
        nt = (((1,), (1,)), ((), ()))
        dq16 = dqp[...].astype(BF16)
        dwuq_ref[...] += lax.dot_general(qn, dq16, tn, preferred_element_type=F32)
        dqn = lax.dot_general(dq16, wuq_ref[...], nt, preferred_element_type=F32)
        dqg_ref[...] += jnp.sum(dqn * cq * rq, axis=0, keepdims=True)
        u = dqn * qg_ref[...]
        dcq_ref[...] = (rq * u - cq * (rq * rq * rq) * jnp.mean(cq * u, axis=-1, keepdims=True)).astype(BF16)
        dkv16 = dkvp[...].astype(BF16)
        dwkv_ref[...] += lax.dot_general(kvn, dkv16, tn, preferred_element_type=F32)
        dkvn = lax.dot_general(dkv16, wkv_ref[...], nt, preferred_element_type=F32)
        dkvg_ref[...] += jnp.sum(dkvn * ckv * rk, axis=0, keepdims=True)
        u = dkvn * kvg_ref[...]
        dckv_ref[...] = (rk * u - ckv * (rk * rk * rk) * jnp.mean(ckv * u, axis=-1, keepdims=True)).astype(BF16)

    cq_s, ckv_s, kr_s, tab_s = _mla_specs(tm)
    hd = pl.BlockSpec((tm, HP), lambda i: (i, 0))
    return _call(
        body, name=name, grid=(s // tm,),
        in_specs=[cq_s, ckv_s, kr_s, tab_s, tab_s, tab_s, _const((1, MLA_Q_RANK)), _const((1, MLA_KV_RANK)),
                  _const((MLA_Q_RANK, HP)), _const((MLA_KV_RANK, 2 * HP)), _const((1, LANES)), _const((1, LANES)),
                  hd, hd, hd],
        out_specs=[pl.BlockSpec((tm, MLA_Q_RANK), lambda i: (i, 0)), pl.BlockSpec((tm, MLA_KV_RANK), lambda i: (i, 0)),
                   pl.BlockSpec((tm, LANES), lambda i: (i, 0)), _const((MLA_Q_RANK, HP)), _const((MLA_KV_RANK, 2 * HP)),
                   _const((1, MLA_Q_RANK)), _const((1, MLA_KV_RANK)), _const((1, LANES)), _const((1, LANES))],
        out_shape=[jax.ShapeDtypeStruct((s, MLA_Q_RANK), BF16), jax.ShapeDtypeStruct((s, MLA_KV_RANK), BF16),
                   jax.ShapeDtypeStruct((s, LANES), BF16), jax.ShapeDtypeStruct((MLA_Q_RANK, HP), F32),
                   jax.ShapeDtypeStruct((MLA_KV_RANK, 2 * HP), F32), jax.ShapeDtypeStruct((1, MLA_Q_RANK), F32),
                   jax.ShapeDtypeStruct((1, MLA_KV_RANK), F32), jax.ShapeDtypeStruct((1, LANES), F32),
                   jax.ShapeDtypeStruct((1, LANES), F32)],
        scratch_shapes=[pltpu.VMEM((tm, HP), F32), pltpu.VMEM((tm, 2 * HP), F32)],
        args=(p, p, p, ct, s1, s2, qn_g, kvn_g, wuq, wkv, gq, gk, dq, dk, dv), semantics=("arbitrary",), side=side)


_ATT_SCALE = MLA_QK ** -0.5
_LOG2E = 1.4426950408889634
_Q_SCALE = _ATT_SCALE * _LOG2E
ATT_BLOCK = 1024
_NEG = -1e30
_NT = (((1,), (1,)), ((), ()))
_TN = (((0,), (0,)), ((), ()))


def _tri_rows(step, n):
    i = step * 0
    for m in range(1, n):
        i = i + (step >= m * (m + 1) // 2).astype(jnp.int32)
    return i, step - i * (i + 1) // 2


def _tri_cols(step, n):
    j = step * 0
    for m in range(1, n):
        j = j + (step >= m * n - m * (m - 1) // 2).astype(jnp.int32)
    return j, j + step - (j * n - j * (j - 1) // 2)


def _diag_mask(t):
    return lax.broadcasted_iota(jnp.int32, (t, t), 0) <= lax.broadcasted_iota(jnp.int32, (t, t), 1)


def _attn_fwd(q, k, v, name, side=None):
    s = q.shape[0]
    t = _pick(s, ATT_BLOCK)
    n = s // t

    def body(q_ref, k_ref, v_ref, o_ref, lse_ref, m_s, l_s, acc):
        i, j = _tri_rows(pl.program_id(1), n)

        @pl.when(j == 0)
        def _():
            m_s[...] = jnp.full_like(m_s, _NEG)
            l_s[...] = jnp.zeros_like(l_s)
            acc[...] = jnp.zeros_like(acc)

        def step(diagonal):
            sc = lax.dot_general(k_ref[...], q_ref[...], _NT, preferred_element_type=F32)
            if diagonal:
                sc = jnp.where(_diag_mask(t), sc, _NEG)
            m_new = jnp.maximum(m_s[...], jnp.max(sc, axis=0, keepdims=True))
            alpha = jnp.exp2(m_s[...] - m_new)
            pr = jnp.exp2(sc - m_new)
            l_s[...] = alpha * l_s[...] + jnp.sum(pr, axis=0, keepdims=True)
            acc[...] = alpha * acc[...] + lax.dot_general(v_ref[...], pr.astype(BF16), _TN, preferred_element_type=F32)
            m_s[...] = m_new

        @pl.when(j < i)
        def _():
            step(False)

        @pl.when(j == i)
        def _():
            step(True)
            o_ref[...] = (acc[...] / l_s[...]).T.astype(BF16)
            lse_ref[...] = m_s[...] + jnp.log2(l_s[...])

    qs = pl.BlockSpec((t, LANES), lambda h, p: (_tri_rows(p, n)[0], h))
    ks = pl.BlockSpec((t, LANES), lambda h, p: (_tri_rows(p, n)[1], h))
    return _call(
        body, name=name, grid=(MLA_HEADS, n * (n + 1) // 2), in_specs=[qs, ks, ks],
        out_specs=[qs, pl.BlockSpec((None, 1, t), lambda h, p: (h, 0, _tri_rows(p, n)[0]))],
        out_shape=[jax.ShapeDtypeStruct((s, HP), BF16), jax.ShapeDtypeStruct((MLA_HEADS, 1, s), F32)],
        scratch_shapes=[pltpu.VMEM((1, t), F32), pltpu.VMEM((1, t), F32), pltpu.VMEM((LANES, t), F32)],
        args=(q, k, v), semantics=("parallel", "arbitrary"), side=side)


def _attn_bwd_dq(q, k, v, o, lse, do, name, side=None):
    s = q.shape[0]
    t = _pick(s, ATT_BLOCK)
    n = s // t

    def body(q_ref, k_ref, v_ref, o_ref, lse_ref, do_ref, dq_ref, dl_ref, acc, dl_s):
        i, j = _tri_rows(pl.program_id(1), n)

        @pl.when(j == 0)
        def _():
            acc[...] = jnp.zeros_like(acc)
            dl_s[...] = jnp.sum((do_ref[...].astype(F32) * o_ref[...].astype(F32)).T, axis=0, keepdims=True)

        def step(diagonal):
            sc = lax.dot_general(k_ref[...], q_ref[...], _NT, preferred_element_type=F32)
            if diagonal:
                sc = jnp.where(_diag_mask(t), sc, _NEG)
            pr = jnp.exp2(sc - lse_ref[...])
            dp = lax.dot_general(v_ref[...], do_ref[...].astype(BF16), _NT, preferred_element_type=F32)
            ds = (pr * (dp - dl_s[...])).astype(BF16)
            acc[...] += lax.dot_general(k_ref[...], ds, _TN, preferred_element_type=F32)

        @pl.when(j < i)
        def _():
            step(False)

        @pl.when(j == i)
        def _():
            step(True)
            dq_ref[...] = (acc[...] * _ATT_SCALE).T.astype(BF16)
            dl_ref[...] = dl_s[...]

    qs = pl.BlockSpec((t, LANES), lambda h, p: (_tri_rows(p, n)[0], h))
    ks = pl.BlockSpec((t, LANES), lambda h, p: (_tri_rows(p, n)[1], h))
    ls = pl.BlockSpec((None, 1, t), lambda h, p: (h, 0, _tri_rows(p, n)[0]))
    return _call(
        body, name=name, grid=(MLA_HEADS, n * (n + 1) // 2), in_specs=[qs, ks, ks, qs, ls, qs], out_specs=[qs, ls],
        out_shape=[jax.ShapeDtypeStruct((s, HP), BF16), jax.ShapeDtypeStruct((MLA_HEADS, 1, s), F32)],
        scratch_shapes=[pltpu.VMEM((LANES, t), F32), pltpu.VMEM((1, t), F32)],
        args=(q, k, v, o, lse, do), semantics=("parallel", "arbitrary"), side=side)


def _attn_bwd_dkv(q, k, v, lse, delta, do, name, side=None):
    s = q.shape[0]
    t = _pick(s, ATT_BLOCK)
    n = s // t

    def body(q_ref, k_ref, v_ref, lse_ref, dl_ref, do_ref, dk_ref, dv_ref, dk_acc, dv_acc):
        j, i = _tri_cols(pl.program_id(1), n)

        def step(diagonal):
            sc = lax.dot_general(k_ref[...], q_ref[...], _NT, preferred_element_type=F32)
            if diagonal:
                sc = jnp.where(_diag_mask(t), sc, _NEG)
            pr = jnp.exp2(sc - lse_ref[...])
            do16 = do_ref[...].astype(BF16)
            dv_acc[...] += jnp.dot(pr.astype(BF16), do16, preferred_element_type=F32)
            dp = lax.dot_general(v_ref[...], do16, _NT, preferred_element_type=F32)
            ds = (pr * (dp - dl_ref[...])).astype(BF16)
            dk_acc[...] += jnp.dot(ds, q_ref[...], preferred_element_type=F32)

        @pl.when(i == j)
        def _():
            dk_acc[...] = jnp.zeros_like(dk_acc)
            dv_acc[...] = jnp.zeros_like(dv_acc)
            step(True)

        @pl.when(i > j)
        def _():
            step(False)

        @pl.when(i == n - 1)
        def _():
            dk_ref[...] = (dk_acc[...] * (1.0 / _LOG2E)).astype(BF16)
            dv_ref[...] = dv_acc[...].astype(BF16)

    qs = pl.BlockSpec((t, LANES), lambda h, p: (_tri_cols(p, n)[1], h))
    ks = pl.BlockSpec((t, LANES), lambda h, p: (_tri_cols(p, n)[0], h))
    ls = pl.BlockSpec((None, 1, t), lambda h, p: (h, 0, _tri_cols(p, n)[1]))
    return _call(
        body, name=name, grid=(MLA_HEADS, n * (n + 1) // 2), in_specs=[qs, ks, ks, ls, ls, qs], out_specs=[ks, ks],
        out_shape=[jax.ShapeDtypeStruct((s, HP), BF16)] * 2,
        scratch_shapes=[pltpu.VMEM((t, LANES), F32), pltpu.VMEM((t, LANES), F32)],
        args=(q, k, v, lse, delta, do), semantics=("parallel", "arbitrary"), side=side)


XBC = HP + 2 * SSD_GROUPS * SSD_STATE
BCW = 2 * SSD_GROUPS * SSD_STATE


def _conv_fwd(p, col0, width, conv_w, conv_b, name):
    s = p.shape[0]
    c0, nblk = col0 // LANES, width // LANES

    def body(x_ref, w_ref, b_ref, o_ref, pad):
        pad[0:8, :] = jnp.zeros((8, LANES), F32)
        pad[8:s + 8, :] = x_ref[...].astype(F32)
        acc = jnp.broadcast_to(b_ref[...], (s, LANES))
        for t in range(SSD_CONV):
            acc = acc + pad[pl.ds(8 - (SSD_CONV - 1) + t, s), :] * w_ref[t:t + 1, :]
        o_ref[...] = acc * _sigmoid(acc)

    return pl.pallas_call(
        body, name=name, grid=(nblk,),
        in_specs=[pl.BlockSpec((s, LANES), lambda j: (0, c0 + j)), pl.BlockSpec((SSD_CONV, LANES), lambda j: (0, j)),
                  pl.BlockSpec((1, LANES), lambda j: (0, j))],
        out_specs=pl.BlockSpec((s, LANES), lambda j: (0, j)), out_shape=jax.ShapeDtypeStruct((s, width), F32),
        scratch_shapes=[pltpu.VMEM((s + 8, LANES), F32)], compiler_params=_params("parallel"))(p, conv_w, conv_b)


def _conv_bwd(p, col0, width, conv_w, conv_b, dact, name):
    s = p.shape[0]
    c0, nblk = col0 // LANES, width // LANES

    def body(x_ref, w_ref, b_ref, d_ref, dx_ref, dw_ref, db_ref, pad, padd):
        pad[0:8, :] = jnp.zeros((8, LANES), F32)
        pad[8:s + 8, :] = x_ref[...].astype(F32)
        acc = jnp.broadcast_to(b_ref[...], (s, LANES))
        for t in range(SSD_CONV):
            acc = acc + pad[pl.ds(8 - (SSD_CONV - 1) + t, s), :] * w_ref[t:t + 1, :]
        sg = _sigmoid(acc)
        dpre = d_ref[...] * (sg * (1.0 + acc * (1.0 - sg)))
        padd[0:s, :] = dpre
        padd[s:s + 8, :] = jnp.zeros((8, LANES), F32)
        dx = jnp.zeros((s, LANES), F32)
        for t in range(SSD_CONV):
            dx = dx + padd[pl.ds(SSD_CONV - 1 - t, s), :] * w_ref[t:t + 1, :]
            dw_ref[t:t + 1, :] = jnp.sum(dpre * pad[pl.ds(8 - (SSD_CONV - 1) + t, s), :], axis=0, keepdims=True)
        dx_ref[...] = dx.astype(BF16)
        db_ref[...] = jnp.sum(dpre, axis=0, keepdims=True)

    blk = pl.BlockSpec((s, LANES), lambda j: (0, j))
    return pl.pallas_call(
        body, name=name, grid=(nblk,),
        in_specs=[pl.BlockSpec((s, LANES), lambda j: (0, c0 + j)), pl.BlockSpec((SSD_CONV, LANES), lambda j: (0, j)),
                  pl.BlockSpec((1, LANES), lambda j: (0, j)), blk],
        out_specs=[blk, pl.BlockSpec((SSD_CONV, LANES), lambda j: (0, j)), pl.BlockSpec((1, LANES), lambda j: (0, j))],
        out_shape=[jax.ShapeDtypeStruct((s, width), BF16), jax.ShapeDtypeStruct((SSD_CONV, width), F32),
                   jax.ShapeDtypeStruct((1, width), F32)],
        scratch_shapes=[pltpu.VMEM((s + 8, LANES), F32), pltpu.VMEM((s + 8, LANES), F32)],
        compiler_params=_params("parallel"))(p, conv_w, conv_b, dact)


def _softplus(x):
    return jnp.maximum(x, 0.0) + jnp.log(1.0 + jnp.exp(-jnp.abs(x)))


def _dt_fwd(p, dt_bias, a_log, name):
    s = p.shape[0]
    tm = _pick(s, 512)

    def body(x_ref, b_ref, a_ref, dt_ref, da_ref):
        dtv = _softplus(x_ref[...] + b_ref[...])
        dav = dtv * (-jnp.exp(a_ref[...]))
        for h in range(SSD_HEADS):
            hs = slice(h * LANES, (h + 1) * LANES)
            dt_ref[:, hs] = jnp.broadcast_to(dtv[:, h:h + 1], (tm, LANES))
            da_ref[:, hs] = jnp.broadcast_to(dav[:, h:h + 1], (tm, LANES))

    out = pl.BlockSpec((tm, HP), lambda i: (i, 0))
    return pl.pallas_call(
        body, name=name, grid=(s // tm,),
        in_specs=[pl.BlockSpec((tm, LANES), lambda i: (i, T_DT // LANES)), _const((1, LANES)), _const((1, LANES))],
        out_specs=[out, out], out_shape=[jax.ShapeDtypeStruct((s, HP), F32)] * 2,
        compiler_params=_params("parallel"))(p, dt_bias, a_log)


def _dt_bwd(p, dt_bias, a_log, dda, ddtx, name):
    s = p.shape[0]
    tm = _pick(s, 512)

    def body(x_ref, b_ref, a_ref, dda_ref, ddtx_ref, dx_ref, db_ref, dal_ref):
        @pl.when(pl.program_id(0) == 0)
        def _():
            db_ref[...] = jnp.zeros_like(db_ref)
            dal_ref[...] = jnp.zeros_like(dal_ref)

        x = x_ref[...] + b_ref[...]
        dtv = _softplus(x)
        av = -jnp.exp(a_ref[...])
        lane = lax.broadcasted_iota(jnp.int32, (tm, LANES), 1)
        pa = jnp.zeros((tm, LANES), F32)
        px = jnp.zeros((tm, LANES), F32)
        for h in range(SSD_HEADS):
            pa = jnp.where(lane == h, dda_ref[:, h * LANES:(h + 1) * LANES], pa)
            px = jnp.where(lane == h, ddtx_ref[:, h * LANES:(h + 1) * LANES], px)
        draw = (pa * av + px) * _sigmoid(x)
        dx_ref[...] = draw.astype(BF16)
        db_ref[...] += jnp.sum(draw, axis=0, keepdims=True)
        dal_ref[...] += jnp.sum(pa * dtv, axis=0, keepdims=True) * av

    hd = pl.BlockSpec((tm, HP), lambda i: (i, 0))
    return pl.pallas_call(
        body, name=name, grid=(s // tm,),
        in_specs=[pl.BlockSpec((tm, LANES), lambda i: (i, T_DT // LANES)), _const((1, LANES)), _const((1, LANES)), hd, hd],
        out_specs=[pl.BlockSpec((tm, LANES), lambda i: (i, 0)), _const((1, LANES)), _const((1, LANES))],
        out_shape=[jax.ShapeDtypeStruct((s, LANES), BF16), jax.ShapeDtypeStruct((1, LANES), F32),
                   jax.ShapeDtypeStruct((1, LANES), F32)],
        compiler_params=_params("arbitrary"))(p, dt_bias, a_log, dda, ddtx)


def _cumsum_rows(x):
    row = lax.broadcasted_iota(jnp.int32, x.shape, 0)
    k = 1
    while k < x.shape[0]:
        x = x + jnp.where(row >= k, pltpu.roll(x, k, 0), 0.0)
        k *= 2
    return x


def _rev_cumsum_rows(x):
    n = x.shape[0]
    row = lax.broadcasted_iota(jnp.int32, x.shape, 0)
    k = 1
    while k < n:
        x = x + jnp.where(row < n - k, pltpu.roll(x, n - k, 0), 0.0)
        k *= 2
    return x


HPG = SSD_HEADS // SSD_GROUPS


def _chunk_decay(da):
    cs = _cumsum_rows(da)
    lm = jnp.exp(jnp.where(_tril_mask(), cs - cs.T, _NEG))
    return cs, lm, cs[CHUNK - 1:CHUNK, :]


def _scan_fwd(xs, bc, dtb, dab, name, side=None):
    s = xs.shape[0]
    nc = s // CHUNK

    def body(x_ref, b_ref, c_ref, dt_ref, da_ref, y_ref, sin_ref, state):
        @pl.when(pl.program_id(1) == 0)
        def _():
            state[...] = jnp.zeros_like(state)

        bv = b_ref[...]
        b16, c16 = bv.astype(BF16), c_ref[...].astype(BF16)
        g = lax.dot_general(c16, b16, _NT, preferred_element_type=F32)
        for hh in range(HPG):
            hs = slice(hh * LANES, (hh + 1) * LANES)
            st = state[hh]
            sin_ref[hh] = st
            cs, lm, cl = _chunk_decay(da_ref[:, hs])
            xd = (x_ref[:, hs] * dt_ref[:, hs]).astype(BF16)
            y = jnp.dot((g * lm).astype(BF16), xd, preferred_element_type=F32)
            y_ref[:, hs] = y + jnp.dot(c16, st.astype(BF16), preferred_element_type=F32) * jnp.exp(cs)
            bd = (bv * jnp.exp(cl - cs)).astype(BF16)
            state[hh] = jnp.exp(cl) * st + lax.dot_general(bd, xd, _TN, preferred_element_type=F32)

    gw = HPG * LANES
    hd = pl.BlockSpec((CHUNK, gw), lambda g, c: (c, g))
    return _call(
        body, name=name, grid=(SSD_GROUPS, nc),
        in_specs=[hd, pl.BlockSpec((CHUNK, LANES), lambda g, c: (c, g)),
                  pl.BlockSpec((CHUNK, LANES), lambda g, c: (c, SSD_GROUPS + g)), hd, hd],
        out_specs=[hd, pl.BlockSpec((HPG, None, SSD_STATE, LANES), lambda g, c: (g, c, 0, 0))],
        out_shape=[jax.ShapeDtypeStruct((s, HP), F32), jax.ShapeDtypeStruct((SSD_HEADS, nc, SSD_STATE, LANES), F32)],
        scratch_shapes=[pltpu.VMEM((HPG, SSD_STATE, LANES), F32)],
        args=(xs, bc, bc, dtb, dab), semantics=("parallel", "arbitrary"), side=side)


def _scan_bwd(xs, bc, dtb, dab, s_in, dy, d_vec, name):
    s = xs.shape[0]
    nc = s // CHUNK

    def body(x_ref, b_ref, c_ref, dt_ref, da_ref, sin_ref, dy_ref, dv_ref, dx_ref, db_ref, dc_ref, dda_ref, ddtx_ref, dstate):
        @pl.when(pl.program_id(1) == 0)
        def _():
            dstate[...] = jnp.zeros_like(dstate)

        bv = b_ref[...]
        b16, c16 = bv.astype(BF16), c_ref[...].astype(BF16)
        g = lax.dot_general(c16, b16, _NT, preferred_element_type=F32)
        row = lax.broadcasted_iota(jnp.int32, (CHUNK, 1), 0)
        dbm = jnp.zeros((CHUNK, SSD_STATE), F32)
        dcm = jnp.zeros((CHUNK, SSD_STATE), F32)
        for hh in range(HPG):
            hs = slice(hh * LANES, (hh + 1) * LANES)
            st, ds = sin_ref[hh], dstate[hh]
            st16, ds16 = st.astype(BF16), ds.astype(BF16)
            xv, dtv, dyv = x_ref[:, hs], dt_ref[:, hs], dy_ref[:, hs]
            cs, lm, cl = _chunk_decay(da_ref[:, hs])
            ecs, ecl = jnp.exp(cs), jnp.exp(cl)
            decay = jnp.exp(cl - cs)
            xd = (xv * dtv).astype(BF16)
            dy16 = dyv.astype(BF16)
            dye = (dyv * ecs).astype(BF16)
            yoff = jnp.dot(c16, st16, preferred_element_type=F32) * ecs
            dcs = jnp.sum(dyv * yoff, axis=-1, keepdims=True)
            dcm = dcm + lax.dot_general(dye, st16, _NT, preferred_element_type=F32)
            dstate[hh] = ecl * ds + lax.dot_general(c16, dye, _TN, preferred_element_type=F32)
            dcl = jnp.sum(jnp.sum(ds * st, axis=0, keepdims=True), axis=1, keepdims=True) * ecl[:, 0:1]
            bd32 = bv * decay
            qm = lax.dot_general(xd, ds16, _NT, preferred_element_type=F32)
            dbm = dbm + qm * decay
            w = jnp.sum(bd32 * qm, axis=-1, keepdims=True)
            dcs = dcs - w
            dcl = dcl + jnp.sum(w, axis=0, keepdims=True)
            dxd = jnp.dot(bd32.astype(BF16), ds16, preferred_element_type=F32)
            m16 = (g * lm).astype(BF16)
            dm = lax.dot_general(dy16, xd, _NT, preferred_element_type=F32)
            dxd = dxd + lax.dot_general(m16, dy16, _TN, preferred_element_type=F32)
            dg = dm * lm
            dg16 = dg.astype(BF16)
            tt = dg * g
            dcm = dcm + jnp.dot(dg16, b16, preferred_element_type=F32)
            dbm = dbm + lax.dot_general(dg16, c16, _TN, preferred_element_type=F32)
            dcs = dcs + jnp.sum(tt, axis=-1, keepdims=True) - jnp.sum(tt.T, axis=-1, keepdims=True)
            dcs = dcs + jnp.where(row == CHUNK - 1, dcl, 0.0)
            dda_ref[:, hs] = _rev_cumsum_rows(jnp.broadcast_to(dcs, (CHUNK, LANES)))
            ddtx_ref[:, hs] = jnp.broadcast_to(jnp.sum(dxd * xv, axis=-1, keepdims=True), (CHUNK, LANES))
            dx_ref[:, hs] = dxd * dtv + dyv * dv_ref[:, hs]
        db_ref[...] = dbm
        dc_ref[...] = dcm

    gw = HPG * LANES
    hd = pl.BlockSpec((CHUNK, gw), lambda g, c: (nc - 1 - c, g))
    gp = pl.BlockSpec((CHUNK, LANES), lambda g, c: (nc - 1 - c, g))
    return pl.pallas_call(
        body, name=name, grid=(SSD_GROUPS, nc),
        in_specs=[hd, gp, pl.BlockSpec((CHUNK, LANES), lambda g, c: (nc - 1 - c, SSD_GROUPS + g)), hd, hd,
                  pl.BlockSpec((HPG, None, SSD_STATE, LANES), lambda g, c: (g, nc - 1 - c, 0, 0)), hd,
                  pl.BlockSpec((1, gw), lambda g, c: (0, g))],
        out_specs=[hd, gp, gp, hd, hd],
        out_shape=[jax.ShapeDtypeStruct((s, HP), F32), jax.ShapeDtypeStruct((s, SSD_GROUPS * SSD_STATE), F32),
                   jax.ShapeDtypeStruct((s, SSD_GROUPS * SSD_STATE), F32), jax.ShapeDtypeStruct((s, HP), F32),
                   jax.ShapeDtypeStruct((s, HP), F32)],
        scratch_shapes=[pltpu.VMEM((HPG, SSD_STATE, LANES), F32)],
        compiler_params=_params("parallel", "arbitrary"))(xs, bc, bc, dtb, dab, s_in, dy, d_vec)


_GN = SSD_INNER // SSD_GROUPS
_GW = HP // SSD_GROUPS


def _ssd_post_fwd(y, xbc, p, d_vec, gain, name):
    s = y.shape[0]
    tm = _pick(s, 512)

    def body(y_ref, x_ref, z_ref, d_ref, g_ref, o_ref):
        z = z_ref[...].astype(F32)
        y2 = (y_ref[...] + x_ref[...] * d_ref[...]) * (z * _sigmoid(z))
        for g in range(SSD_GROUPS):
            gs = slice(g * _GW, (g + 1) * _GW)
            yg = y2[:, gs]
            r = lax.rsqrt(jnp.sum(yg * yg, axis=-1, keepdims=True) * (1.0 / _GN) + EPS)
            o_ref[:, gs] = (yg * r * g_ref[:, gs]).astype(BF16)

    hd = pl.BlockSpec((tm, HP), lambda i: (i, 0))
    return pl.pallas_call(
        body, name=name, grid=(s // tm,),
        in_specs=[hd, hd, pl.BlockSpec((tm, HP), lambda i: (i, C_Z // HP)), _const((1, HP)), _const((1, HP))],
        out_specs=hd, out_shape=jax.ShapeDtypeStruct((s, HP), BF16), compiler_params=_params("parallel"))(y, xbc, p, d_vec, gain)


def _ssd_post_bwd(y, xbc, p, d_vec, gain, dyn, name):
    s = y.shape[0]
    tm = _pick(s, 512)

    def body(y_ref, x_ref, z_ref, d_ref, g_ref, dn_ref, dy_ref, dz_ref, dg_ref, dd_ref):
        @pl.when(pl.program_id(0) == 0)
        def _():
            dg_ref[...] = jnp.zeros_like(dg_ref)
            dd_ref[...] = jnp.zeros_like(dd_ref)

        z, xv = z_ref[...].astype(F32), x_ref[...]
        sg = _sigmoid(z)
        sz = z * sg
        yt = y_ref[...] + xv * d_ref[...]
        y2 = yt * sz
        for g in range(SSD_GROUPS):
            gs = slice(g * _GW, (g + 1) * _GW)
            yg, dn = y2[:, gs], dn_ref[:, gs].astype(F32)
            r = lax.rsqrt(jnp.sum(yg * yg, axis=-1, keepdims=True) * (1.0 / _GN) + EPS)
            u = dn * g_ref[:, gs]
            dy2 = r * u - yg * (r * r * r) * (jnp.sum(yg * u, axis=-1, keepdims=True) * (1.0 / _GN))
            dg_ref[:, gs] += jnp.sum(dn * yg * r, axis=0, keepdims=True)
            dyt = dy2 * sz[:, gs]
            dy_ref[:, gs] = dyt
            dz_ref[:, gs] = (dy2 * yt[:, gs] * (sg[:, gs] * (1.0 + z[:, gs] * (1.0 - sg[:, gs])))).astype(BF16)
            dd_ref[:, gs] += jnp.sum(dyt * xv[:, gs], axis=0, keepdims=True)

    hd = pl.BlockSpec((tm, HP), lambda i: (i, 0))
    return pl.pallas_call(
        body, name=name, grid=(s // tm,),
        in_specs=[hd, hd, pl.BlockSpec((tm, HP), lambda i: (i, C_Z // HP)), _const((1, HP)), _const((1, HP)), hd],
        out_specs=[hd, hd, _const((1, HP)), _const((1, HP))],
        out_shape=[jax.ShapeDtypeStruct((s, HP), F32), jax.ShapeDtypeStruct((s, HP), BF16),
                   jax.ShapeDtypeStruct((1, HP), F32), jax.ShapeDtypeStruct((1, HP), F32)],
        compiler_params=_params("arbitrary"))(y, xbc, p, d_vec, gain, dyn)


def _merge_fwd(p, ya, o, yc, wb0, wb1, wb2, w_out, x, name, side=None):
    s = p.shape[0]
    tm = _pick(s, 512)

    def body(g_ref, ya_ref, o_ref, yc_ref, w0_ref, w1_ref, w2_ref, wo_ref, x_ref, mg_ref, y_ref):
        acc = jnp.zeros((tm, D_MODEL), F32)
        for i, (b_ref, w_ref) in enumerate(((ya_ref, w0_ref), (o_ref, w1_ref), (yc_ref, w2_ref))):
            t = jnp.dot(b_ref[...].astype(BF16), w_ref[...], preferred_element_type=F32)
            acc = acc + _sigmoid(g_ref[:, i * D_MODEL:(i + 1) * D_MODEL].astype(F32)) * t
        mg = acc.astype(BF16)
        mg_ref[...] = mg
        y_ref[...] = x_ref[...] + jnp.dot(mg, wo_ref[...], preferred_element_type=F32)

    row = pl.BlockSpec((tm, D_MODEL), lambda i: (i, 0))
    return _call(
        body, name=name, grid=(s // tm,),
        in_specs=[pl.BlockSpec((tm, 3 * D_MODEL), lambda i: (i, C_G // (3 * D_MODEL))),
                  pl.BlockSpec((tm, GM_WIDTH), lambda i: (i, 0)), row, row,
                  _resident((GM_WIDTH, D_MODEL)), _resident((HP, D_MODEL)), _resident((HP, D_MODEL)),
                  _resident((D_MODEL, D_MODEL)), row],
        out_specs=[row, row],
        out_shape=[jax.ShapeDtypeStruct((s, D_MODEL), BF16), jax.ShapeDtypeStruct((s, D_MODEL), F32)],
        scratch_shapes=[], args=(p, ya, o, yc, wb0, wb1, wb2, w_out, x), semantics=("parallel",), side=side)


def _merge_bwd(p, ya, o, yc, wb0, wb1, wb2, w_out, dy, name):
    s = p.shape[0]
    tm = _pick(s, 512)

    def body(g_ref, ya_ref, o_ref, yc_ref, w0_ref, w1_ref, w2_ref, wo_ref, dy_ref,
             d0_ref, d1_ref, d2_ref, dg_ref, dya_ref, do_ref, dyc_ref):
        dm = lax.dot_general(dy_ref[...].astype(BF16), wo_ref[...], _NT, preferred_element_type=F32)
        for i, (b_ref, w_ref, d_ref, db_ref) in enumerate(((ya_ref, w0_ref, d0_ref, dya_ref), (o_ref, w1_ref, d1_ref, do_ref),
                                                            (yc_ref, w2_ref, d2_ref, dyc_ref))):
            cs = slice(i * D_MODEL, (i + 1) * D_MODEL)
            t = jnp.dot(b_ref[...].astype(BF16), w_ref[...], preferred_element_type=F32)
            sg = _sigmoid(g_ref[:, cs].astype(F32))
            dt16 = (dm * sg).astype(BF16)
            d_ref[...] = dt16
            dg_ref[:, cs] = (dm * t * sg * (1.0 - sg)).astype(BF16)
            db_ref[...] = lax.dot_general(dt16, w_ref[...], _NT, preferred_element_type=F32).astype(db_ref.dtype)

    row = pl.BlockSpec((tm, D_MODEL), lambda i: (i, 0))
    nar = pl.BlockSpec((tm, GM_WIDTH), lambda i: (i, 0))
    wide = pl.BlockSpec((tm, 3 * D_MODEL), lambda i: (i, 0))
    return pl.pallas_call(
        body, name=name, grid=(s // tm,),
        in_specs=[pl.BlockSpec((tm, 3 * D_MODEL), lambda i: (i, C_G // (3 * D_MODEL))), nar, row, row,
                  _resident((GM_WIDTH, D_MODEL)), _resident((HP, D_MODEL)), _resident((HP, D_MODEL)),
                  _resident((D_MODEL, D_MODEL)), row],
        out_specs=[row, row, row, wide, nar, row, row],
        out_shape=[jax.ShapeDtypeStruct((s, D_MODEL), BF16)] * 3 + [jax.ShapeDtypeStruct((s, 3 * D_MODEL), BF16),
                   jax.ShapeDtypeStruct((s, GM_WIDTH), BF16), jax.ShapeDtypeStruct((s, D_MODEL), BF16),
                   jax.ShapeDtypeStruct((s, D_MODEL), F32)],
        compiler_params=_params("parallel"))(p, ya, o, yc, wb0, wb1, wb2, w_out, dy)


def _loss_head(y, target, name):
    s, d = y.shape
    tm = _pick(s, 512)

    def body(y_ref, t_ref, dy_ref, sq_ref):
        @pl.when(pl.program_id(0) == 0)
        def _():
            sq_ref[...] = jnp.zeros_like(sq_ref)

        e = y_ref[...] - t_ref[...]
        dy_ref[...] = e * (1.0 / d)
        sq_ref[...] += jnp.sum(e * e, axis=0, keepdims=True)

    row = pl.BlockSpec((tm, d), lambda i: (i, 0))
    return pl.pallas_call(
        body, name=name, grid=(s // tm,), in_specs=[row, row], out_specs=[row, _const((1, d))],
        out_shape=[jax.ShapeDtypeStruct((s, d), F32), jax.ShapeDtypeStruct((1, d), F32)],
        compiler_params=_params("arbitrary"))(y, target)


def _adamw(w, g, m, v, name):
    rows, cols = w.shape
    tr = rows
    for cand in (512, 256, 128, 64, 32, 16, 8):
        if rows % cand == 0 and cand * cols * 4 <= 3 * 1024 * 1024:
            tr = cand
            break

    def body(w_ref, g_ref, m_ref, v_ref, d_ref, nm_ref, nv_ref):
        d_ref[...], nm_ref[...], nv_ref[...] = _adam_update(w_ref[...], g_ref[...], m_ref[...], v_ref[...])

    blk = pl.BlockSpec((tr, cols), lambda i: (i, 0))
    return pl.pallas_call(
        body, name=name, grid=(rows // tr,), in_specs=[blk] * 4, out_specs=[blk] * 3,
        out_shape=[jax.ShapeDtypeStruct((rows, cols), F32)] * 3, compiler_params=_params("parallel"))(w, g, m, v)


def _adam_update(w, g, m, v):
    nm = ADAM_B1 * m + (1.0 - ADAM_B1) * g
    nv = ADAM_B2 * v + (1.0 - ADAM_B2) * (g * g)
    c1 = 1.0 - ADAM_B1 ** ADAM_STEP
    c2 = 1.0 - ADAM_B2 ** ADAM_STEP
    return -ADAM_LR * ((nm / c1) / (jnp.sqrt(nv / c2) + ADAM_EPS) + ADAM_WD * w), nm, nv


def _adamw_sharded(w, m, v, mine, theirs, name, side=None):
    depth, rows, cols = w.shape
    tr = _row_tile(rows // 2, cols, 1024 * 1024)
    nb = rows // 2 // tr

    def body(w_ref, m_ref, v_ref, a_ref, b_ref, g_ref, d_ref, nm_ref, nv_ref):
        c = lax.axis_index("c")
        g = jnp.where(pl.program_id(1) // nb == c, a_ref[...], b_ref[...])
        g_ref[...] = g
        d_ref[...], nm_ref[...], nv_ref[...] = _adam_update(w_ref[...], g, m_ref[...], v_ref[...])

    blk = pl.BlockSpec((None, tr, cols), lambda l, i: (l, i, 0))
    mine_s = pl.BlockSpec((None, tr, cols), lambda l, i: (l, jnp.where(i // nb == lax.axis_index("c"), i % nb, 0), 0))
    theirs_s = pl.BlockSpec((None, tr, cols), lambda l, i: (l, jnp.where(i // nb == lax.axis_index("c"), 0, i % nb), 0))
    return _call(
        body, name=name, grid=(depth, rows // tr), in_specs=[blk, blk, blk, mine_s, theirs_s], out_specs=[blk] * 4,
        out_shape=[jax.ShapeDtypeStruct((depth, rows, cols), F32)] * 4, scratch_shapes=[],
        args=(w, m, v, mine, theirs), semantics=("parallel", "parallel"), side=side)


ANY = pl.BlockSpec(memory_space=pl.ANY)


def _me():
    return lax.axis_index("x"), lax.axis_index("y"), lax.axis_index("c")


def _other_chips(x, y):
    return [(1 - x, y), (x, 1 - y), (1 - x, 1 - y)]


def _chip_index(cx, cy):
    return 2 * cx + cy


class _Exchange:
    def __init__(self, ins, out_shapes, n_sems, start, finish):
        self.ins, self.out_shapes, self.n_sems, self.start, self.finish = list(ins), list(out_shapes), n_sems, start, finish


def _sem_scratch(ex):
    return [pltpu.SemaphoreType.DMA((ex.n_sems,)), pltpu.SemaphoreType.DMA((ex.n_sems,))]


def _run_exchange(ex, name):
    n_in, n_out = len(ex.ins), len(ex.out_shapes)

    def body(*refs):
        in_refs, out_refs, (send, recv) = refs[:n_in], refs[n_in:n_in + n_out], refs[n_in + n_out:]
        ex.start(in_refs, out_refs, send, recv)
        ex.finish(in_refs, out_refs, send, recv)

    return pl.pallas_call(body, name=name, in_specs=[ANY] * n_in, out_specs=[ANY] * n_out, out_shape=ex.out_shapes,
                          scratch_shapes=_sem_scratch(ex))(*ex.ins)


def _call(body, *, name, grid, in_specs, out_specs, out_shape, scratch_shapes, args, semantics, side=None):
    if side is None:
        return pl.pallas_call(body, name=name, grid=grid, in_specs=in_specs, out_specs=out_specs, out_shape=out_shape,
                              scratch_shapes=scratch_shapes, compiler_params=_params(*semantics))(*args), []
    n_in, n_out, n_sc = len(in_specs), len(out_specs), len(scratch_shapes)
    s_in, s_out = len(side.ins), len(side.out_shapes)

    def hosted(*refs):
        pos = 0
        parts = []
        for size in (n_in, s_in, n_out, s_out, n_sc, 2):
            parts.append(refs[pos:pos + size])
            pos += size
        ins, sins, outs, souts, scratch, (send, recv) = parts
        ids = [pl.program_id(a) for a in range(len(grid))]
        first = functools.reduce(jnp.logical_and, [i == 0 for i in ids])
        last = functools.reduce(jnp.logical_and, [i == g - 1 for i, g in zip(ids, grid)])

        @pl.when(first)
        def _():
            side.start(sins, souts, send, recv)

        body(*ins, *outs, *scratch)

        @pl.when(last)
        def _():
            side.finish(sins, souts, send, recv)

    res = pl.pallas_call(
        hosted, name=name, grid=grid, in_specs=list(in_specs) + [ANY] * s_in, out_specs=list(out_specs) + [ANY] * s_out,
        out_shape=list(out_shape) + side.out_shapes, scratch_shapes=list(scratch_shapes) + _sem_scratch(side),
        compiler_params=_params(*["arbitrary"] * len(grid)))(*args, *side.ins)
    return res[:n_out], res[n_out:]


def _half(ref_rows, c):
    return pl.ds(c * (ref_rows // 2), ref_rows // 2)


def _gather_exchange(shards, layer):
    n = len(shards)
    rows = [a.shape[1] for a in shards]

    def copy(in_refs, out_refs, send, recv, t, k, chip, hc, to, from_input=False):
        dst = out_refs[t].at[chip, _half(rows[t], hc)]
        src = in_refs[t].at[layer, _half(rows[t], hc)] if from_input else dst
        return pltpu.make_async_remote_copy(src_ref=src, dst_ref=dst, send_sem=send.at[7 * t + k], recv_sem=recv.at[7 * t + k],
                                            device_id=to, device_id_type=MESH)

    def own(in_refs, out_refs, send, recv, t):
        x, y, c = _me()
        return pltpu.make_async_remote_copy(src_ref=in_refs[t].at[layer], dst_ref=out_refs[t].at[_chip_index(x, y)],
                                            send_sem=send.at[7 * t + 6], recv_sem=recv.at[7 * t + 6],
                                            device_id=(x, y, 1 - c), device_id_type=MESH)

    def start(in_refs, out_refs, send, recv):
        x, y, c = _me()
        for j, chip in enumerate(_other_chips(x, y)):
            for t in range(n):
                copy(in_refs, out_refs, send, recv, t, j, _chip_index(x, y), c, (*chip, c), from_input=True).start()
        for t in range(n):
            own(in_refs, out_refs, send, recv, t).start()

    def finish(in_refs, out_refs, send, recv):
        x, y, c = _me()
        chips = _other_chips(x, y)
        passed = []
        for t in range(n):
            own(in_refs, out_refs, send, recv, t).wait()
        for j, chip in enumerate(chips):
            for t in range(n):
                copy(in_refs, out_refs, send, recv, t, j, _chip_index(*chip), c, (x, y, c)).wait_recv()
                cp = copy(in_refs, out_refs, send, recv, t, 3 + j, _chip_index(*chip), c, (x, y, 1 - c))
                cp.start()
                passed.append(cp)
        for j, chip in enumerate(chips):
            for t in range(n):
                copy(in_refs, out_refs, send, recv, t, 3 + j, _chip_index(*chip), 1 - c, (x, y, c)).wait_recv()
                copy(in_refs, out_refs, send, recv, t, j, _chip_index(x, y), c, (*chip, c), from_input=True).wait_send()
        for cp in passed:
            cp.wait_send()

    return _Exchange(shards, [jax.ShapeDtypeStruct((N_CHIPS,) + a.shape[1:], a.dtype) for a in shards], 7 * n, start, finish)


def _pair_exchange(gs):
    n = len(gs)
    rows = [a.shape[1] for a in gs]

    def copies(in_refs, out_refs, send, recv):
        x, y, c = _me()
        return [pltpu.make_async_remote_copy(src_ref=in_refs[t].at[:, _half(rows[t], 1 - c)], dst_ref=out_refs[t],
                                             send_sem=send.at[t], recv_sem=recv.at[t], device_id=(x, y, 1 - c),
                                             device_id_type=MESH) for t in range(n)]

    def start(*refs):
        for cp in copies(*refs):
            cp.start()

    def finish(*refs):
        for cp in copies(*refs):
            cp.wait()

    return _Exchange(gs, [jax.ShapeDtypeStruct((N_CHIPS, a.shape[1] // 2, a.shape[2]), a.dtype) for a in gs], n, start, finish)


def _row_tile(rows, cols, budget=2 * 1024 * 1024):
    best = None
    for t in range(8, rows + 1, 8):
        if rows % t == 0 and t * cols * 4 <= budget:
            best = t
    return best or rows


def _pair_add(g, got, name):
    _, rows, cols = g.shape
    tr = _row_tile(rows // 2, cols)
    nb = rows // 2 // tr

    def body(g_ref, r_ref, o16_ref, own_ref):
        x, y, _ = _me()
        tot = g_ref[...] + r_ref[...]
        o16_ref[...] = tot.astype(BF16)

        @pl.when(pl.program_id(1) == _chip_index(x, y))
        def _():
            own_ref[...] = tot

    blk = (None, tr, cols)
    return pl.pallas_call(
        body, name=name, grid=(nb, N_CHIPS),
        in_specs=[pl.BlockSpec(blk, lambda i, k: (k, i + lax.axis_index("c") * nb, 0)),
                  pl.BlockSpec(blk, lambda i, k: (k, i, 0))],
        out_specs=[pl.BlockSpec(blk, lambda i, k: (k, i, 0)), pl.BlockSpec((tr, cols), lambda i, k: (i, 0))],
        out_shape=[jax.ShapeDtypeStruct((N_CHIPS, rows // 2, cols), BF16), jax.ShapeDtypeStruct((rows // 2, cols), F32)],
        compiler_params=_params("parallel", "arbitrary"))(g, got)


def _chip_exchange(parts):
    n = len(parts)

    def copies(in_refs, out_refs, send, recv):
        x, y, c = _me()
        return [pltpu.make_async_remote_copy(src_ref=in_refs[t].at[_chip_index(*chip)], dst_ref=out_refs[t].at[j],
                                             send_sem=send.at[3 * t + j], recv_sem=recv.at[3 * t + j],
                                             device_id=(*chip, c), device_id_type=MESH)
                for j, chip in enumerate(_other_chips(x, y)) for t in range(n)]

    def start(*refs):
        for cp in copies(*refs):
            cp.start()

    def finish(*refs):
        for cp in copies(*refs):
            cp.wait()

    return _Exchange(parts, [jax.ShapeDtypeStruct((3,) + a.shape[1:], a.dtype) for a in parts], 3 * n, start, finish)


def _chip_add(own, got, name, layer, into=None):
    rows, cols = own.shape
    tr = _row_tile(rows, cols, 1024 * 1024)

    def body(own_ref, got_ref, *rest):
        acc = own_ref[...]
        for j in range(3):
            acc = acc + got_ref[j].astype(F32)
        rest[-1][...] = acc

    in_specs = [pl.BlockSpec((tr, cols), lambda i: (i, 0)), pl.BlockSpec((3, tr, cols), lambda i: (0, i, 0))]
    args, alias = [own, got], {}
    if into is not None:
        in_specs.append(ANY)
        args.append(into)
        alias = {2: 0}
    return pl.pallas_call(
        body, name=name, grid=(rows // tr,), in_specs=in_specs,
        out_specs=pl.BlockSpec((None, tr, cols), lambda i: (layer, i, 0)),
        out_shape=jax.ShapeDtypeStruct((DEPTH, rows, cols), F32), input_output_aliases=alias,
        compiler_params=_params("parallel"))(*args)


def _pair_share(halves):
    n = len(halves)

    def copies(in_refs, out_refs, send, recv):
        x, y, c = _me()
        return [pltpu.make_async_remote_copy(src_ref=in_refs[t], dst_ref=out_refs[t], send_sem=send.at[t],
                                             recv_sem=recv.at[t], device_id=(x, y, 1 - c), device_id_type=MESH)
                for t in range(n)]

    def start(*refs):
        for cp in copies(*refs):
            cp.start()

    def finish(*refs):
        for cp in copies(*refs):
            cp.wait()

    return _Exchange(halves, [jax.ShapeDtypeStruct(a.shape, a.dtype) for a in halves], n, start, finish)


N_DEV = 8


def _all_exchange(v):
    r, cols = v.shape

    def peers():
        x, y, c = _me()
        flip = lambda v, f: 1 - v if f else v
        return 4 * x + 2 * y + c, [(flip(x, fx), flip(y, fy), flip(c, fc)) for fx in (0, 1) for fy in (0, 1) for fc in (0, 1)][1:]

    def local(in_refs, out_refs, send, me):
        return pltpu.make_async_copy(in_refs[0], out_refs[0].at[me], send.at[7])

    def start(in_refs, out_refs, send, recv):
        me, others = peers()
        local(in_refs, out_refs, send, me).start()
        for j, peer in enumerate(others):
            pltpu.make_async_remote_copy(src_ref=in_refs[0], dst_ref=out_refs[0].at[me], send_sem=send.at[j],
                                         recv_sem=recv.at[j], device_id=peer, device_id_type=MESH).start()

    def finish(in_refs, out_refs, send, recv):
        me, others = peers()
        for j, (px, py, pc) in enumerate(others):
            pltpu.make_async_remote_copy(src_ref=in_refs[0], dst_ref=out_refs[0].at[4 * px + 2 * py + pc], send_sem=send.at[j],
                                         recv_sem=recv.at[j], device_id=(px, py, pc), device_id_type=MESH).wait()
        local(in_refs, out_refs, send, me).wait()

    return _Exchange([v], [jax.ShapeDtypeStruct((N_DEV, r, cols), v.dtype)], 8, start, finish)


def _sum_slots(a, name):
    n, r, cols = a.shape
    tr = _pick(r, 512) if r % 8 == 0 else r
    for cand in (512, 256, 128, 64, 32, 16, 8):
        if r % cand == 0:
            tr = cand
            break

    def body(a_ref, o_ref):
        acc = a_ref[0]
        for k in range(1, n):
            acc = acc + a_ref[k]
        o_ref[...] = acc

    return pl.pallas_call(
        body, name=name, grid=(r // tr,), in_specs=[pl.BlockSpec((n, tr, cols), lambda i: (0, i, 0))],
        out_specs=pl.BlockSpec((tr, cols), lambda i: (i, 0)), out_shape=jax.ShapeDtypeStruct((r, cols), F32),
        compiler_params=_params("parallel"))(a)


def _join(name, stacked):
    ax = SHARDED[name][1]
    return jnp.concatenate([stacked[k] for k in range(N_CHIPS)], axis=ax)


def _split(name, full):
    ax = SHARDED[name][1]
    return jnp.stack(jnp.split(full, N_CHIPS, axis=ax))


def _heads_pad(a, real, axis):
    shp = a.shape
    a = a.reshape(shp[:axis] + (MLA_HEADS, real) + shp[axis + 1:])
    pad = [(0, 0)] * a.ndim
    pad[axis + 1] = (0, LANES - real)
    a = jnp.pad(a, pad)
    return a.reshape(shp[:axis] + (HP,) + shp[axis + 1:])


def _heads_unpad(a, real, axis):
    shp = a.shape
    a = a.reshape(shp[:axis] + (MLA_HEADS, LANES) + shp[axis + 1:])
    a = lax.slice_in_dim(a, 0, real, axis=axis + 1)
    return a.reshape(shp[:axis] + (MLA_HEADS * real,) + shp[axis + 1:])


def _lane_place(a, start):
    n = a.shape[-1]
    pad = [(0, 0)] * (a.ndim - 1) + [(start, LANES - start - n)]
    return jnp.pad(a, pad)


_O_UV, _O_CQ, _O_CKV, _O_KR, _O_Z, _O_XBC, _O_DT, _O_G = 0, 1024, 1408, 1664, 1696, 2208, 3232, 3240


def _w_in_pad(w):
    sl = lambda a, b: w[:, a:b]
    xs = _heads_pad(sl(_O_XBC, _O_XBC + SSD_INNER), SSD_HEAD_DIM, 1)
    bc = sl(_O_XBC + SSD_INNER, _O_DT)
    main = jnp.concatenate([sl(_O_UV, _O_CQ), _heads_pad(sl(_O_Z, _O_XBC), SSD_HEAD_DIM, 1), xs, sl(_O_G, IN_COLS)], axis=1)
    tail = jnp.concatenate([bc, sl(_O_CKV, _O_KR), sl(_O_CQ, _O_CKV), _lane_place(sl(_O_KR, _O_Z), MLA_NOPE),
                            _lane_place(sl(_O_DT, _O_G), 0), jnp.zeros((w.shape[0], PW_TAIL - T_DT - LANES), w.dtype)], axis=1)
    return main, tail


def _w_in_unpad(gm, gt):
    m = lambda a, n: gm[:, a:a + n]
    t = lambda a, n: gt[:, a:a + n]
    parts = [m(C_UV, 1024), t(T_CQ, MLA_Q_RANK), t(T_CKV, MLA_KV_RANK), t(T_KR + MLA_NOPE, MLA_ROPE),
             _heads_unpad(m(C_Z, HP), SSD_HEAD_DIM, 1), _heads_unpad(m(C_XS, HP), SSD_HEAD_DIM, 1), t(T_BC, BCW),
             t(T_DT, SSD_HEADS), m(C_G, 3 * D_MODEL)]
    return jnp.concatenate(parts, axis=1)


def _xbc_pad(a):
    return jnp.concatenate([_heads_pad(a[..., :SSD_INNER], SSD_HEAD_DIM, a.ndim - 1), a[..., SSD_INNER:]], axis=-1)


def _xbc_unpad(a):
    return jnp.concatenate([_heads_unpad(a[..., :HP], SSD_HEAD_DIM, a.ndim - 1), a[..., HP:]], axis=-1)


def _rope_tables(positions):
    inv_freq = 1.0 / (ROPE_THETA ** (jnp.arange(0, MLA_ROPE, 2, dtype=F32) / MLA_ROPE))
    ang = positions.astype(F32)[:, None] * inv_freq
    cos, sin = jnp.cos(ang), jnp.sin(ang)
    s = positions.shape[0]
    half = MLA_ROPE // 2
    z = lambda n: jnp.zeros((s, n), F32)
    ct = jnp.concatenate([jnp.ones((s, MLA_NOPE), F32), cos, cos, z(LANES - MLA_QK)], axis=1)
    s1 = jnp.concatenate([z(MLA_NOPE), -sin, z(half), z(LANES - MLA_QK)], axis=1)
    s2 = jnp.concatenate([z(MLA_NOPE), z(half), sin, z(LANES - MLA_QK)], axis=1)
    return ct, s1, s2


def _layer_weights(full, small, l, part):
    w = {}
    row = lambda n: small[n][l][None, :]
    stacked = lambda g: g.reshape((N_CHIPS * g.shape[1], g.shape[2]))
    if part in ('ffn1', 'ffn2'):
        w[part + '_w_in'] = full[part + '_w_in']
        w[part + '_w_out'] = stacked(full[part + '_w_out'])
        w[part + '_norm'] = row(part + '_norm')
        return w
    w['w_out'] = stacked(full['w_out'])
    fl = {n: _join(n, full[n]) for n in ('w_in', 'mla_w_uq', 'mla_w_ukv', 'w_branch', 'ssd_conv_w')}
    w['w_in_main'], w['w_in_tail'] = _w_in_pad(fl['w_in'])
    w['wuq'] = _heads_pad(fl['mla_w_uq'], MLA_QK, 1)
    ukv = fl['mla_w_ukv'].reshape(MLA_KV_RANK, MLA_HEADS, MLA_NOPE + MLA_V)
    zero = jnp.zeros((MLA_KV_RANK, MLA_HEADS, LANES - MLA_NOPE), ukv.dtype)
    wk = jnp.concatenate([ukv[:, :, :MLA_NOPE], zero], axis=2).reshape(MLA_KV_RANK, HP)
    wv = jnp.concatenate([ukv[:, :, MLA_NOPE:], zero], axis=2).reshape(MLA_KV_RANK, HP)
    w['wkv'] = jnp.concatenate([wk, wv], axis=1)
    wb = fl['w_branch']
    w['wb0'] = wb[0]
    w['wb1'] = _heads_pad(wb[1], MLA_V, 0)
    w['wb2'] = _heads_pad(wb[2], SSD_HEAD_DIM, 0)
    w['conv_w'] = _xbc_pad(fl['ssd_conv_w'].astype(F32))
    for n in ('mix_norm', 'gm_v_norm', 'mla_q_norm', 'mla_kv_norm'):
        w[n] = row(n)
    w['gm_w_s'] = small['gm_w_s'][l]
    w['gm_b_full'] = jnp.broadcast_to(small['gm_b_s'][l][:, :, None], (GM_GROUPS, CHUNK, LANES))
    w['gq'] = _lane_place(row('mla_q_gain'), 0)
    w['gk'] = _lane_place(row('mla_k_gain'), 0)
    w['conv_b'] = _xbc_pad(row('ssd_conv_b'))
    w['dt_bias'] = _lane_place(row('ssd_dt_bias'), 0)
    w['a_log'] = _lane_place(row('ssd_a_log'), 0)
    w['d_vec'] = jnp.repeat(small['ssd_d'][l], LANES)[None, :]
    w['ssd_norm'] = _heads_pad(row('ssd_norm'), SSD_HEAD_DIM, 1)
    return w


_MIXER_SMALL = ['mla_w_uq', 'mla_w_ukv', 'ssd_conv_w', 'w_branch', 'w_out']
_MIXER_SMALL_G = [n for n in _MIXER_SMALL if n != 'ssd_conv_w']
GATHER_HOSTS = {'attn': ['ffn1_w_in', 'ffn2_w_in'], 'scan': ['ffn1_w_out', 'ffn2_w_out'], 'merge': _MIXER_SMALL, 'ffn2_in': ['w_in']}
GATHER_HOSTS_LATER = {'ffn1_in': ['ffn1_w_out'], 'proj': ['w_in'], 'attn': ['ffn1_w_in', 'ffn2_w_in'], 'scan': ['ffn2_w_out'],
                      'merge': _MIXER_SMALL}
FIRST_NOW = ['ffn1_w_in', 'ffn1_w_out']
FIRST_HOSTS = {'ffn1_in': ['w_in'], 'ffn1_out': _MIXER_SMALL, 'proj': ['ffn2_w_in', 'ffn2_w_out']}
PAIR_HOSTS = {'ffn2_dwout': ['ffn1_w_in', 'ffn2_w_in'], 'ffn2_dwin': ['ffn1_w_out', 'w_in', 'ffn2_w_out'] + _MIXER_SMALL_G}
REDUCE_HOSTS = {'dattn_q': ['ffn1_w_out', 'w_in', 'ffn2_w_out'], 'dattn_kv': ['ffn1_w_in', 'ffn2_w_in'], 'dmla_pre': _MIXER_SMALL_G}
LAST_EARLY = ['w_in', 'ffn2_w_in', 'ffn2_w_out'] + _MIXER_SMALL_G
LAST_HOSTS = {'ffn1_dact': ['w_in'], 'ffn1_dwin': ['ffn2_w_in'], 'ffn1_dx': ['ffn2_w_out'] + _MIXER_SMALL_G}
LAST_LATE = ['ffn1_w_in', 'ffn1_w_out']


def _ffn_fwd(x, norm, w4, w_out, tag, sides=None):
    sides = sides or {}
    carried = {}
    (h, gate, up, act), carried[f"{tag}_in"] = _ffn_in(x, norm, w4, f"{tag}_in", sides.get(f"{tag}_in"))
    y, carried[f"{tag}_out"] = _ffn_out(act, w_out, x, f"{tag}_out", sides.get(f"{tag}_out"))
    return y, (x, h, gate, up, act), carried


def _ffn_bwd(dy, saved, norm, w4, w_out, tag, sides=None, after_dwout=None):
    sides = dict(sides or {})
    carried = {}
    x, h, gate, up, act = saved
    dw_out, carried[f"{tag}_dwout"] = _ffn_dwout(act, dy, f"{tag}_dwout", sides.get(f"{tag}_dwout"))
    if after_dwout is not None:
        sides.update(after_dwout(carried[f"{tag}_dwout"]))
    da, carried[f"{tag}_dact"] = _ffn_dact(dy, w_out, gate, up, f"{tag}_dact", sides.get(f"{tag}_dact"))
    dw_in, carried[f"{tag}_dwin"] = _ffn_dwin(h, da, f"{tag}_dwin", sides.get(f"{tag}_dwin"))
    (dx, dnorm), carried[f"{tag}_dx"] = _ffn_dx(da, w4, x, norm, dy, f"{tag}_dx", sides.get(f"{tag}_dx"))
    return dx, dnorm, dw_in, dw_out.reshape((N_CHIPS, 2 * FC // N_CHIPS, D_MODEL)), carried


def _mixer_fwd(x, w, tabs, tag, sides=None):
    sides = sides or {}
    carried = {}
    (h, pm, pt), carried['proj'] = _mixer_proj(x, w['mix_norm'], w['w_in_main'], w['w_in_tail'], f"{tag}_proj", sides.get('proj'))
    ya = _gmlp_fwd(pm, w['gm_v_norm'], w['gm_w_s'], w['gm_b_full'], f"{tag}_gmlp")
    q, k, v = _mla_pre_fwd(pt, tabs, w['mla_q_norm'], w['mla_kv_norm'], w['wuq'], w['wkv'], w['gq'], w['gk'], f"{tag}_mla_pre")
    (o, lse), carried['attn'] = _attn_fwd(q, k, v, f"{tag}_attn", sides.get('attn'))
    xs = _conv_fwd(pm, C_XS, HP, w['conv_w'][:, :HP], w['conv_b'][:, :HP], f"{tag}_conv_x")
    bc = _conv_fwd(pt, T_BC, BCW, w['conv_w'][:, HP:], w['conv_b'][:, HP:], f"{tag}_conv_bc")
    dtb, dab = _dt_fwd(pt, w['dt_bias'], w['a_log'], f"{tag}_dt")
    (ys, s_in), carried['scan'] = _scan_fwd(xs, bc, dtb, dab, f"{tag}_scan", sides.get('scan'))
    yc = _ssd_post_fwd(ys, xs, pm, w['d_vec'], w['ssd_norm'], f"{tag}_ssd_post")
    (mg, y), carried['merge'] = _merge_fwd(pm, ya, o, yc, w['wb0'], w['wb1'], w['wb2'], w['w_out'], x, f"{tag}_merge",
                                           sides.get('merge'))
    return y, (x, h, pm, pt, ya, q, k, v, o, lse, xs, bc, dtb, dab, ys, s_in, yc, mg), carried


def _pair_sums(pending, got):
    return {n: _pair_add(pending[n], got[n], f"pair_add_{n}") for n in got}


def _chip_sums(sums, arrived, layer, stacked):
    for n in arrived:
        stacked[n] = _chip_add(sums[n][1], arrived[n], f"chip_add_{n}", layer, stacked.get(n))


def _reduce_to_chip(pending, layer, stacked):
    names = list(pending)
    got = _run_exchange(_pair_exchange([pending[n] for n in names]), "pair_exchange")
    sums = _pair_sums(pending, dict(zip(names, got)))
    arrived = _run_exchange(_chip_exchange([sums[n][0] for n in names]), "chip_exchange")
    _chip_sums(sums, dict(zip(names, arrived)), layer, stacked)


def _mixer_bwd(dy, saved, w, tabs, tag, sides=None):
    sides = sides or {}
    carried = {}
    x, h, pm, pt, ya, q, k, v, o, lse, xs, bc, dtb, dab, ys, s_in, yc, mg = saved
    g = {}
    g['w_out'] = _matmul(mg, dy, ta=True, name=f"{tag}_dwout").reshape((N_CHIPS, D_MODEL // N_CHIPS, D_MODEL))
    d0, d1, d2, dgates, dya, do, dyc = _merge_bwd(pm, ya, o, yc, w['wb0'], w['wb1'], w['wb2'], w['w_out'], dy, f"{tag}_dmerge")
    dwb0 = _matmul(ya, d0, ta=True, name=f"{tag}_dwb0")
    dwb1 = _matmul(o, d1, ta=True, name=f"{tag}_dwb1")
    dwb2 = _matmul(yc, d2, ta=True, name=f"{tag}_dwb2")
    g['w_branch'] = _split('w_branch', jnp.stack([dwb0, _heads_unpad(dwb1, MLA_V, 0), _heads_unpad(dwb2, SSD_HEAD_DIM, 0)]))
    duv, g['gm_v_norm'], g['gm_w_s'], db = _gmlp_bwd(pm, w['gm_v_norm'], w['gm_w_s'], w['gm_b_full'], dya, f"{tag}_dgmlp")
    g['gm_b_s'] = db.T
    (dq, delta), carried['dattn_q'] = _attn_bwd_dq(q, k, v, o, lse, do, f"{tag}_dattn_q", sides.get('dattn_q'))
    (dk, dv), carried['dattn_kv'] = _attn_bwd_dkv(q, k, v, lse, delta, do, f"{tag}_dattn_kv", sides.get('dattn_kv'))
    (dcq, dckv, dkr, dwuq, dwkv, g['mla_q_norm'], g['mla_kv_norm'], dgq, dgk), carried['dmla_pre'] = _mla_pre_bwd(
        pt, tabs, w['mla_q_norm'], w['mla_kv_norm'], w['wuq'], w['wkv'], w['gq'], w['gk'], dq, dk, dv, f"{tag}_dmla_pre",
        sides.get('dmla_pre'))
    g['mla_w_uq'] = _split('mla_w_uq', _heads_unpad(dwuq, MLA_QK, 1))
    dwk = dwkv[:, :HP].reshape(MLA_KV_RANK, MLA_HEADS, LANES)[:, :, :MLA_NOPE]
    dwv = dwkv[:, HP:].reshape(MLA_KV_RANK, MLA_HEADS, LANES)[:, :, :MLA_V]
    g['mla_w_ukv'] = _split('mla_w_ukv', jnp.concatenate([dwk, dwv], axis=2).reshape(MLA_KV_RANK, MLA_HEADS * (MLA_NOPE + MLA_V)))
    g['mla_q_gain'], g['mla_k_gain'] = dgq[:, :MLA_QK], dgk[:, :MLA_QK]
    dys, dz, dssd_norm, dd = _ssd_post_bwd(ys, xs, pm, w['d_vec'], w['ssd_norm'], dyc, f"{tag}_dssd_post")
    g['ssd_norm'] = _heads_unpad(dssd_norm, SSD_HEAD_DIM, 1)
    g['ssd_d'] = jnp.sum(dd.reshape(SSD_HEADS, LANES), axis=1)[None, :]
    dxs, dbm, dcm, dda, ddtx = _scan_bwd(xs, bc, dtb, dab, s_in, dys, w['d_vec'], f"{tag}_dscan")
    dxs16, dcw_x, dcb_x = _conv_bwd(pm, C_XS, HP, w['conv_w'][:, :HP], w['conv_b'][:, :HP], dxs, f"{tag}_dconv_x")
    dbc16, dcw_bc, dcb_bc = _conv_bwd(pt, T_BC, BCW, w['conv_w'][:, HP:], w['conv_b'][:, HP:],
                                      jnp.concatenate([dbm, dcm], axis=1), f"{tag}_dconv_bc")
    g['ssd_conv_w'] = _xbc_unpad(jnp.concatenate([dcw_x, dcw_bc], axis=1))
    g['ssd_conv_b'] = _xbc_unpad(jnp.concatenate([dcb_x, dcb_bc], axis=1))
    ddt, dbias, dalog = _dt_bwd(pt, w['dt_bias'], w['a_log'], dda, ddtx, f"{tag}_ddt")
    g['ssd_dt_bias'], g['ssd_a_log'] = dbias[:, :SSD_HEADS], dalog[:, :SSD_HEADS]
    s = x.shape[0]
    dpm = jnp.concatenate([duv, dz, dxs16, dgates], axis=1)
    dpt = jnp.concatenate([dbc16, dckv, dcq, dkr, ddt, jnp.zeros((s, PW_TAIL - T_DT - LANES), BF16)], axis=1)
    g['w_in'] = _split('w_in', _w_in_unpad(_matmul(h, dpm, ta=True, name=f"{tag}_dwin_main"),
                                           _matmul(h, dpt, ta=True, name=f"{tag}_dwin_tail")))
    dx, g['mix_norm'] = _mixer_dx(dpm, dpt, w['w_in_main'], w['w_in_tail'], x, w['mix_norm'], dy, f"{tag}_dx")
    return dx, g, carried


_CONV_ROWS = 32


def _rows_cols(a, lead):
    return a.reshape(a.shape[:lead] + (int(np.prod(a.shape[lead:-1])), a.shape[-1]))


def _shard_views(wts):
    views = []
    for n in SHARDED_ORDER:
        a = _rows_cols(wts[n].astype(BF16), 1)
        if n == 'ssd_conv_w':
            a = jnp.pad(a, ((0, 0), (0, _CONV_ROWS - a.shape[1]), (0, 0)))
        views.append(a)
    return views


def _gathered(names, arrays):
    out = {}
    for n, a in zip(names, arrays):
        shp = _shard_shape(n)
        if n == 'ssd_conv_w':
            a = a[:, :shp[0]]
        out[n] = a.reshape((N_CHIPS,) + shp)
    return out


def _local_step(x, positions, target, weights, small, distributed=True):
    tabs = _rope_tables(positions)
    views = dict(zip(SHARDED_ORDER, weights)) if distributed else None
    plan = [{} for _ in range(DEPTH)]
    if distributed:
        for l in range(DEPTH - 1):
            plan[l].update({host: (names, l + 1) for host, names in (GATHER_HOSTS if l == 0 else GATHER_HOSTS_LATER).items()})
        plan[0].update({host: (names, 0) for host, names in FIRST_HOSTS.items()})
        have = [dict() for _ in range(DEPTH)]
        have[0].update(_gathered(FIRST_NOW, _run_exchange(_gather_exchange([views[n] for n in FIRST_NOW], 0), "gather_first")))
    else:
        have = weights

    def absorb(l, carried):
        for host, arrays in carried.items():
            if host in plan[l]:
                names, layer = plan[l][host]
                have[layer].update(_gathered(names, arrays))

    ws, saved = [], []
    for l in range(DEPTH):
        sides = {host: _gather_exchange([views[n] for n in names], layer) for host, (names, layer) in plan[l].items()}
        w = _layer_weights(have[l], small, l, 'ffn1')
        x, s1, carried = _ffn_fwd(x, w['ffn1_norm'], w['ffn1_w_in'], w['ffn1_w_out'], "ffn1", sides)
        absorb(l, carried)
        w.update(_layer_weights(have[l], small, l, 'mixer'))
        x, s2, carried = _mixer_fwd(x, w, tabs, "mix", sides)
        absorb(l, carried)
        w.update(_layer_weights(have[l], small, l, 'ffn2'))
        x, s3, carried = _ffn_fwd(x, w['ffn2_norm'], w['ffn2_w_in'], w['ffn2_w_out'], "ffn2", sides)
        absorb(l, carried)
        ws.append(w)
        saved.append((s1, s2, s3))
    dy, sq = _loss_head(x, target, "loss_head")
    loss = 0.5 * jnp.sum(sq) / D_MODEL
    grads, reduced, pending = [None] * DEPTH, {}, None

    def chip_sides(sums, hosts):
        return {host: _chip_exchange([sums[n][0] for n in names]) for host, names in hosts.items()}

    def arrivals(carried, hosts):
        return {n: a for host, names in hosts.items() for n, a in zip(names, carried[host])}

    for l in reversed(range(DEPTH)):
        w = ws[l]
        s1, s2, s3 = saved[l]
        sides = {host: _pair_exchange([pending[n] for n in names]) for host, names in PAIR_HOSTS.items()} if pending else {}
        dy, dn2, dwi2, dwo2, carried = _ffn_bwd(dy, s3, w['ffn2_norm'], w['ffn2_w_in'], w['ffn2_w_out'], "ffn2", sides)
        sides = {}
        if pending:
            sums = _pair_sums(pending, arrivals(carried, PAIR_HOSTS))
            sides = chip_sides(sums, REDUCE_HOSTS)
        dy, g, carried = _mixer_bwd(dy, s2, w, tabs, "mix", sides)
        if pending:
            _chip_sums(sums, arrivals(carried, REDUCE_HOSTS), l + 1, reduced)
        g.update(ffn2_norm=dn2, ffn2_w_in=dwi2, ffn2_w_out=dwo2)
        last = distributed and l == 0
        if last:
            early = {n: _rows_cols(g[n], 1) for n in LAST_EARLY}
            after = {}

            def after_dwout(got):
                after['sums'] = _pair_sums(early, dict(zip(LAST_EARLY, got)))
                return chip_sides(after['sums'], LAST_HOSTS)

            dy, dn1, dwi1, dwo1, carried = _ffn_bwd(dy, s1, w['ffn1_norm'], w['ffn1_w_in'], w['ffn1_w_out'], "ffn1",
                                                    {'ffn1_dwout': _pair_exchange([early[n] for n in LAST_EARLY])}, after_dwout)
            _chip_sums(after['sums'], arrivals(carried, LAST_HOSTS), 0, reduced)
        else:
            dy, dn1, dwi1, dwo1, _ = _ffn_bwd(dy, s1, w['ffn1_norm'], w['ffn1_w_in'], w['ffn1_w_out'], "ffn1")
        g.update(ffn1_norm=dn1, ffn1_w_in=dwi1, ffn1_w_out=dwo1)
        grads[l] = g
        if distributed:
            pending = {n: _rows_cols(g[n], 1) for n in REDUCED}
    if distributed:
        _reduce_to_chip({n: pending[n] for n in LAST_LATE}, 0, reduced)
    return loss, dy, grads, reduced


SMALL_PACK = SMALL_ORDER + ['ssd_conv_w']


def _pack_small(per_layer_rows, tail=None):
    parts = [per_layer_rows[l][n].reshape(-1).astype(F32) for l in range(DEPTH) for n in SMALL_PACK]
    if tail is not None:
        parts.append(tail.reshape(1))
    flat = jnp.concatenate(parts)
    rows = -(-flat.shape[0] // LANES)
    rows = -(-rows // 8) * 8
    return jnp.pad(flat, (0, rows * LANES - flat.shape[0])).reshape(rows, LANES)


def _unpack_small(buf, shapes):
    flat = buf.reshape(-1)
    off = 0
    out = {n: [] for n in SMALL_PACK}
    for l in range(DEPTH):
        for n in SMALL_PACK:
            size = int(np.prod(shapes[n]))
            out[n].append(flat[off:off + size].reshape(shapes[n]))
            off += size
    return {n: jnp.stack(v) for n, v in out.items()}


def kernel(x, positions, ffn1_norm, ffn1_w_in, ffn1_w_out, mix_norm, w_in, gm_v_norm, gm_w_s, gm_b_s, mla_q_norm, mla_kv_norm, mla_w_uq, mla_w_ukv, mla_q_gain, mla_k_gain, ssd_conv_w, ssd_conv_b, ssd_dt_bias, ssd_a_log, ssd_d, ssd_norm, w_branch, w_out, ffn2_norm, ffn2_w_in, ffn2_w_out, loss_target, m_ffn1_norm, m_ffn1_w_in, m_ffn1_w_out, m_mix_norm, m_w_in, m_gm_v_norm, m_gm_w_s, m_gm_b_s, m_mla_q_norm, m_mla_kv_norm, m_mla_w_uq, m_mla_w_ukv, m_mla_q_gain, m_mla_k_gain, m_ssd_conv_w, m_ssd_conv_b, m_ssd_dt_bias, m_ssd_a_log, m_ssd_d, m_ssd_norm, m_w_branch, m_w_out, m_ffn2_norm, m_ffn2_w_in, m_ffn2_w_out, v_ffn1_norm, v_ffn1_w_in, v_ffn1_w_out, v_mix_norm, v_w_in, v_gm_v_norm, v_gm_w_s, v_gm_b_s, v_mla_q_norm, v_mla_kv_norm, v_mla_w_uq, v_mla_w_ukv, v_mla_q_gain, v_mla_k_gain, v_ssd_conv_w, v_ssd_conv_b, v_ssd_dt_bias, v_ssd_a_log, v_ssd_d, v_ssd_norm, v_w_branch, v_w_out, v_ffn2_norm, v_ffn2_w_in, v_ffn2_w_out):
    wts = dict(zip(WEIGHTS, (ffn1_norm, ffn1_w_in, ffn1_w_out, mix_norm, w_in, gm_v_norm, gm_w_s, gm_b_s, mla_q_norm, mla_kv_norm,
                             mla_w_uq, mla_w_ukv, mla_q_gain, mla_k_gain, ssd_conv_w, ssd_conv_b, ssd_dt_bias, ssd_a_log, ssd_d,
                             ssd_norm, w_branch, w_out, ffn2_norm, ffn2_w_in, ffn2_w_out)))
    mom = dict(zip(WEIGHTS, (m_ffn1_norm, m_ffn1_w_in, m_ffn1_w_out, m_mix_norm, m_w_in, m_gm_v_norm, m_gm_w_s, m_gm_b_s, m_mla_q_norm,
                             m_mla_kv_norm, m_mla_w_uq, m_mla_w_ukv, m_mla_q_gain, m_mla_k_gain, m_ssd_conv_w, m_ssd_conv_b,
                             m_ssd_dt_bias, m_ssd_a_log, m_ssd_d, m_ssd_norm, m_w_branch, m_w_out, m_ffn2_norm, m_ffn2_w_in,
                             m_ffn2_w_out)))
    var = dict(zip(WEIGHTS, (v_ffn1_norm, v_ffn1_w_in, v_ffn1_w_out, v_mix_norm, v_w_in, v_gm_v_norm, v_gm_w_s, v_gm_b_s, v_mla_q_norm,
                             v_mla_kv_norm, v_mla_w_uq, v_mla_w_ukv, v_mla_q_gain, v_mla_k_gain, v_ssd_conv_w, v_ssd_conv_b,
                             v_ssd_dt_bias, v_ssd_a_log, v_ssd_d, v_ssd_norm, v_w_branch, v_w_out, v_ffn2_norm, v_ffn2_w_in,
                             v_ffn2_w_out)))
    cx, cy, _ = _me()
    mychip = _chip_index(cx, cy)

    small = {n: wts[n] for n in SMALL_ORDER}
    loss_part, dx, grads, reduced = _local_step(x[0], positions[0], loss_target[0], _shard_views(wts), small)
    rows_cols = _rows_cols
    halves = [reduced[n] for n in REDUCED]
    theirs = _run_exchange(_pair_share(halves), "pair_share")
    grad, delta, new_m, new_v = {}, {}, {}, {}
    everyone = _all_exchange(_pack_small(grads, tail=loss_part))
    for n, a, b in zip(REDUCED, halves, theirs):
        shp = wts[n].shape
        outs, carried = _adamw_sharded(rows_cols(wts[n], 1), rows_cols(mom[n], 1), rows_cols(var[n], 1), a, b, f"adamw_{n}",
                                       everyone if n == REDUCED[0] else None)
        if n == REDUCED[0]:
            partials = carried[0]
        grad[n], delta[n], new_m[n], new_v[n] = [o.reshape(shp) for o in outs]
    shapes = {n: wts[n].shape[1:] for n in SMALL_ORDER}
    shapes['ssd_conv_w'] = SHARDED['ssd_conv_w'][0]
    summed = _sum_slots(partials, "small_sum")
    small_g = _unpack_small(summed, shapes)
    loss = summed.reshape(-1)[DEPTH * sum(int(np.prod(shapes[n])) for n in SMALL_PACK)]
    conv_full = small_g.pop('ssd_conv_w')
    shard_cols = _shard_shape('ssd_conv_w')[1]
    small_g['ssd_conv_w'] = lax.dynamic_slice_in_dim(conv_full, mychip * shard_cols, shard_cols, axis=2)
    shapes['ssd_conv_w'] = _shard_shape('ssd_conv_w')

    per_layer = lambda t: [{n: t[n][l] for n in SMALL_PACK} for l in range(DEPTH)]
    d, nm, nv = _adamw(_pack_small(per_layer(wts)), _pack_small(per_layer(small_g)), _pack_small(per_layer(mom)),
                       _pack_small(per_layer(var)), "adamw_small")
    sd, snm, snv = _unpack_small(d, shapes), _unpack_small(nm, shapes), _unpack_small(nv, shapes)
    for n in SMALL_PACK:
        grad[n], delta[n], new_m[n], new_v[n] = small_g[n], sd[n], snm[n], snv[n]
    return (loss, dx[None], *[grad[n] for n in WEIGHTS], *[delta[n] for n in WEIGHTS], *[new_m[n] for n in WEIGHTS],
            *[new_v[n] for n in WEIGHTS])
```

```python
import functools

import numpy as np
import jax
import jax.numpy as jnp
from jax import lax
from jax.experimental import pallas as pl
from jax.experimental.pallas import tpu as pltpu

F32, BF16 = jnp.float32, jnp.bfloat16
MESH = pl.DeviceIdType.MESH

D_MODEL, DEPTH, D_FF, EPS = 1024, 4, 2816, 1e-6
GM_WIDTH, GM_GROUPS, CHUNK = 512, 4, 128
MLA_HEADS, MLA_Q_RANK, MLA_KV_RANK, MLA_NOPE, MLA_ROPE, MLA_V = 8, 384, 256, 64, 32, 64
MLA_QK = MLA_NOPE + MLA_ROPE
ROPE_THETA = 10000.0
SSD_HEADS, SSD_HEAD_DIM, SSD_GROUPS, SSD_STATE, SSD_CONV = 8, 64, 2, 128, 4
SSD_INNER = SSD_HEADS * SSD_HEAD_DIM
IN_COLS = 6312
LANES = 128
ADAM_LR, ADAM_B1, ADAM_B2, ADAM_EPS, ADAM_WD, ADAM_STEP = 0.001, 0.9, 0.999, 1e-08, 0.01, 10

C_UV, C_Z, C_XS, C_G, PW_MAIN = 0, 1024, 2048, 3072, 6144
T_BC, T_CKV, T_CQ, T_KR, T_DT, PW_TAIL = 0, 512, 768, 1152, 1280, 1536
HP = MLA_HEADS * LANES
FC = 2 * D_FF // 4

WEIGHTS = ['ffn1_norm', 'ffn1_w_in', 'ffn1_w_out', 'mix_norm', 'w_in', 'gm_v_norm', 'gm_w_s', 'gm_b_s', 'mla_q_norm',
           'mla_kv_norm', 'mla_w_uq', 'mla_w_ukv', 'mla_q_gain', 'mla_k_gain', 'ssd_conv_w', 'ssd_conv_b', 'ssd_dt_bias',
           'ssd_a_log', 'ssd_d', 'ssd_norm', 'w_branch', 'w_out', 'ffn2_norm', 'ffn2_w_in', 'ffn2_w_out']
SHARDED = {'ffn1_w_in': ((1024, 5632), 1), 'ffn1_w_out': ((2816, 1024), 0), 'w_in': ((1024, 6312), 1),
           'mla_w_uq': ((384, 768), 1), 'mla_w_ukv': ((256, 1024), 1), 'ssd_conv_w': ((4, 1024), 1),
           'w_branch': ((3, 512, 1024), 2), 'w_out': ((1024, 1024), 0), 'ffn2_w_in': ((1024, 5632), 1),
           'ffn2_w_out': ((2816, 1024), 0)}
SHARDED_ORDER = [n for n in WEIGHTS if n in SHARDED]
SMALL_ORDER = [n for n in WEIGHTS if n not in SHARDED]
REDUCED = [n for n in SHARDED_ORDER if n != 'ssd_conv_w']
N_CHIPS = 4


def _shard_shape(name):
    shape, ax = SHARDED[name]
    return tuple(d // N_CHIPS if i == ax else d for i, d in enumerate(shape))


def _pick(dim, target):
    if dim <= target:
        return dim
    t = (target // LANES) * LANES
    while t >= LANES:
        if dim % t == 0:
            return t
        t -= LANES
    return dim


def _sigmoid(x):
    return 1.0 / (1.0 + jnp.exp(-x))


def _params(*sem):
    return pltpu.CompilerParams(dimension_semantics=sem, vmem_limit_bytes=56 * 1024 * 1024)


def _matmul(a, b, *, ta=False, tb=False, out_dtype=F32, scale=1.0, res=None, name, side=None):
    if ta:
        k_dim, m_dim = a.shape
    else:
        m_dim, k_dim = a.shape
    if tb:
        n_dim, k2 = b.shape
    else:
        k2, n_dim = b.shape
    assert k_dim == k2, (a.shape, b.shape, ta, tb)
    tm, tn, tk = _pick(m_dim, 1024), _pick(n_dim, 1024), _pick(k_dim, 1024)
    nk = k_dim // tk
    dn = (((0 if ta else 1,), (1 if tb else 0,)), ((), ()))

    def body(*refs):
        if res is not None:
            a_ref, b_ref, r_ref, o_ref, acc = refs
        else:
            a_ref, b_ref, o_ref, acc = refs
        k = pl.program_id(2)

        @pl.when(k == 0)
        def _():
            acc[...] = jnp.zeros_like(acc)

        acc[...] += lax.dot_general(a_ref[...].astype(BF16), b_ref[...].astype(BF16), dn, preferred_element_type=F32)

        @pl.when(k == nk - 1)
        def _():
            r = acc[...]
            if scale != 1.0:
                r = r * scale
            if res is not None:
                r = r + r_ref[...]
            o_ref[...] = r.astype(out_dtype)

    a_spec = pl.BlockSpec((tk, tm), lambda j, i, k: (k, i)) if ta else pl.BlockSpec((tm, tk), lambda j, i, k: (i, k))
    b_spec = pl.BlockSpec((tn, tk), lambda j, i, k: (j, k)) if tb else pl.BlockSpec((tk, tn), lambda j, i, k: (k, j))
    in_specs = [a_spec, b_spec]
    args = [a, b]
    if res is not None:
        in_specs.append(pl.BlockSpec((tm, tn), lambda j, i, k: (i, j)))
        args.append(res)
    (out,), carried = _call(
        body, name=name, grid=(n_dim // tn, m_dim // tm, nk), in_specs=in_specs,
        out_specs=[pl.BlockSpec((tm, tn), lambda j, i, k: (i, j))],
        out_shape=[jax.ShapeDtypeStruct((m_dim, n_dim), out_dtype)],
        scratch_shapes=[pltpu.VMEM((tm, tn), F32)], args=args, semantics=("parallel", "parallel", "arbitrary"), side=side)
    return out if side is None else (out, carried)


_NT = (((1,), (1,)), ((), ()))
_TN = (((0,), (0,)), ((), ()))


def _resident(shape):
    return pl.BlockSpec(shape, lambda *_: tuple(0 for _ in shape), pipeline_mode=pl.Buffered(1))


def _ffn_in(x, gain, w4, name, side=None):
    s, d = x.shape
    tm = _pick(s, 512)

    def body(x_ref, g_ref, w_ref, h_ref, gate_ref, up_ref, act_ref):
        xv = x_ref[...]
        r = lax.rsqrt(jnp.mean(xv * xv, axis=-1, keepdims=True) + EPS)
        h = (xv * r * g_ref[...]).astype(BF16)
        h_ref[...] = h
        for j in range(2):
            g16 = jnp.dot(h, w_ref[j], preferred_element_type=F32).astype(BF16)
            u16 = jnp.dot(h, w_ref[j + 2], preferred_element_type=F32).astype(BF16)
            gate_ref[j] = g16
            up_ref[j] = u16
            gf, uf = g16.astype(F32), u16.astype(F32)
            act_ref[j] = (gf * _sigmoid(gf) * uf).astype(BF16)

    half = pl.BlockSpec((2, tm, FC), lambda i: (0, i, 0))
    return _call(
        body, name=name, grid=(s // tm,),
        in_specs=[pl.BlockSpec((tm, d), lambda i: (i, 0)), pl.BlockSpec((1, d), lambda i: (0, 0)), _resident((4, d, FC))],
        out_specs=[pl.BlockSpec((tm, d), lambda i: (i, 0)), half, half, half],
        out_shape=[jax.ShapeDtypeStruct((s, d), BF16)] + [jax.ShapeDtypeStruct((2, s, FC), BF16)] * 3,
        scratch_shapes=[], args=(x, gain, w4), semantics=("parallel",), side=side)


def _ffn_out(act, w_out, x, name, side=None):
    s, d = x.shape
    tm = _pick(s, 512)

    def body(a_ref, w_ref, x_ref, o_ref):
        acc = jnp.dot(a_ref[0], w_ref[0:FC, :], preferred_element_type=F32)
        acc = acc + jnp.dot(a_ref[1], w_ref[FC:2 * FC, :], preferred_element_type=F32)
        o_ref[...] = x_ref[...] + 0.5 * acc

    row = pl.BlockSpec((tm, d), lambda i: (i, 0))
    (out,), carried = _call(
        body, name=name, grid=(s // tm,),
        in_specs=[pl.BlockSpec((2, tm, FC), lambda i: (0, i, 0)), _resident((2 * FC, d)), row], out_specs=[row],
        out_shape=[jax.ShapeDtypeStruct((s, d), F32)], scratch_shapes=[], args=(act, w_out, x), semantics=("parallel",),
        side=side)
    return out, carried


def _ffn_dact(dy, w_out, gate, up, name, side=None):
    s, d = dy.shape
    tm = _pick(s, 512)

    def body(dy_ref, w_ref, g_ref, u_ref, o_ref):
        dy16 = dy_ref[...].astype(BF16)
        for j in range(2):
            dact = 0.5 * lax.dot_general(dy16, w_ref[j * FC:(j + 1) * FC, :], _NT, preferred_element_type=F32)
            g, u = g_ref[j].astype(F32), u_ref[j].astype(F32)
            sg = _sigmoid(g)
            o_ref[j] = (dact * u * (sg * (1.0 + g * (1.0 - sg)))).astype(BF16)
            o_ref[j + 2] = (dact * g * sg).astype(BF16)

    half = pl.BlockSpec((2, tm, FC), lambda i: (0, i, 0))
    (out,), carried = _call(
        body, name=name, grid=(s // tm,),
        in_specs=[pl.BlockSpec((tm, d), lambda i: (i, 0)), _resident((2 * FC, d)), half, half],
        out_specs=[pl.BlockSpec((4, tm, FC), lambda i: (0, i, 0))],
        out_shape=[jax.ShapeDtypeStruct((4, s, FC), BF16)], scratch_shapes=[], args=(dy, w_out, gate, up),
        semantics=("parallel",), side=side)
    return out, carried


def _ffn_dwout(act, dy, name, side=None):
    s, d = dy.shape
    tk = _pick(s, 1024)
    nk = s // tk

    def body(a_ref, dy_ref, o_ref):
        k = pl.program_id(1)

        @pl.when(k == 0)
        def _():
            o_ref[...] = jnp.zeros_like(o_ref)

        o_ref[...] += lax.dot_general(a_ref[...], dy_ref[...].astype(BF16), _TN, preferred_element_type=F32)

        @pl.when(k == nk - 1)
        def _():
            o_ref[...] = 0.5 * o_ref[...]

    (out,), carried = _call(
        body, name=name, grid=(2, nk),
        in_specs=[pl.BlockSpec((None, tk, FC), lambda j, k: (j, k, 0)), pl.BlockSpec((tk, d), lambda j, k: (k, 0))],
        out_specs=[pl.BlockSpec((FC, d), lambda j, k: (j, 0))], out_shape=[jax.ShapeDtypeStruct((2 * FC, d), F32)],
        scratch_shapes=[], args=(act, dy), semantics=("parallel", "arbitrary"), side=side)
    return out, carried


def _ffn_dwin(h, da, name, side=None):
    s, d = h.shape
    tk = _pick(s, 1024)

    def body(h_ref, da_ref, o_ref):
        @pl.when(pl.program_id(1) == 0)
        def _():
            o_ref[...] = jnp.zeros_like(o_ref)

        o_ref[...] += lax.dot_general(h_ref[...], da_ref[...], _TN, preferred_element_type=F32)

    (out,), carried = _call(
        body, name=name, grid=(4, s // tk),
        in_specs=[pl.BlockSpec((tk, d), lambda j, k: (k, 0)), pl.BlockSpec((None, tk, FC), lambda j, k: (j, k, 0))],
        out_specs=[pl.BlockSpec((None, d, FC), lambda j, k: (j, 0, 0))], out_shape=[jax.ShapeDtypeStruct((4, d, FC), F32)],
        scratch_shapes=[], args=(h, da), semantics=("parallel", "arbitrary"), side=side)
    return out, carried


def _ffn_dx(da, w4, x, gain, dy, name, side=None):
    s, d = x.shape
    tm = _pick(s, 512)

    def body(da_ref, w_ref, x_ref, g_ref, dy_ref, dx_ref, dg_ref):
        @pl.when(pl.program_id(0) == 0)
        def _():
            dg_ref[...] = jnp.zeros_like(dg_ref)

        dh = jnp.zeros((tm, d), F32)
        for j in range(4):
            dh = dh + lax.dot_general(da_ref[j], w_ref[j], _NT, preferred_element_type=F32)
        xv = x_ref[...]
        r = lax.rsqrt(jnp.mean(xv * xv, axis=-1, keepdims=True) + EPS)
        u = dh * g_ref[...]
        dx_ref[...] = dy_ref[...] + r * u - xv * (r * r * r) * jnp.mean(xv * u, axis=-1, keepdims=True)
        dg_ref[...] += jnp.sum(dh * xv * r, axis=0, keepdims=True)

    row = pl.BlockSpec((tm, d), lambda i: (i, 0))
    vec = pl.BlockSpec((1, d), lambda i: (0, 0))
    return _call(
        body, name=name, grid=(s // tm,),
        in_specs=[pl.BlockSpec((4, tm, FC), lambda i: (0, i, 0)), _resident((4, d, FC)), row, vec, row],
        out_specs=[row, vec], out_shape=[jax.ShapeDtypeStruct((s, d), F32), jax.ShapeDtypeStruct((1, d), F32)],
        scratch_shapes=[], args=(da, w4, x, gain, dy), semantics=("arbitrary",), side=side)


def _mixer_proj(x, gain, w_main, w_tail, name, side=None):
    s, d = x.shape
    tm = _pick(s, 512)

    def body(x_ref, g_ref, wm_ref, wt_ref, h_ref, pm_ref, pt_ref):
        xv = x_ref[...]
        r = lax.rsqrt(jnp.mean(xv * xv, axis=-1, keepdims=True) + EPS)
        h = (xv * r * g_ref[...]).astype(BF16)
        h_ref[...] = h
        pm_ref[...] = jnp.dot(h, wm_ref[...], preferred_element_type=F32).astype(BF16)
        pt_ref[...] = jnp.dot(h, wt_ref[...], preferred_element_type=F32)

    row = lambda width: pl.BlockSpec((tm, width), lambda i: (i, 0))
    return _call(
        body, name=name, grid=(s // tm,),
        in_specs=[row(d), pl.BlockSpec((1, d), lambda i: (0, 0)), _resident((d, PW_MAIN)), _resident((d, PW_TAIL))],
        out_specs=[row(d), row(PW_MAIN), row(PW_TAIL)],
        out_shape=[jax.ShapeDtypeStruct((s, d), BF16), jax.ShapeDtypeStruct((s, PW_MAIN), BF16),
                   jax.ShapeDtypeStruct((s, PW_TAIL), F32)],
        scratch_shapes=[], args=(x, gain, w_main, w_tail), semantics=("parallel",), side=side)


def _mixer_dx(dpm, dpt, w_main, w_tail, x, gain, dy, name):
    s, d = x.shape
    tm = _pick(s, 512)

    def body(dpm_ref, dpt_ref, wm_ref, wt_ref, x_ref, g_ref, dy_ref, dx_ref, dg_ref):
        @pl.when(pl.program_id(0) == 0)
        def _():
            dg_ref[...] = jnp.zeros_like(dg_ref)

        dh = lax.dot_general(dpm_ref[...], wm_ref[...], _NT, preferred_element_type=F32)
        dh = dh + lax.dot_general(dpt_ref[...], wt_ref[...], _NT, preferred_element_type=F32)
        xv = x_ref[...]
        r = lax.rsqrt(jnp.mean(xv * xv, axis=-1, keepdims=True) + EPS)
        u = dh * g_ref[...]
        dx_ref[...] = dy_ref[...] + r * u - xv * (r * r * r) * jnp.mean(xv * u, axis=-1, keepdims=True)
        dg_ref[...] += jnp.sum(dh * xv * r, axis=0, keepdims=True)

    row = lambda width: pl.BlockSpec((tm, width), lambda i: (i, 0))
    vec = pl.BlockSpec((1, d), lambda i: (0, 0))
    return pl.pallas_call(
        body, name=name, grid=(s // tm,),
        in_specs=[row(PW_MAIN), row(PW_TAIL), _resident((d, PW_MAIN)), _resident((d, PW_TAIL)), row(d), vec, row(d)],
        out_specs=[row(d), vec], out_shape=[jax.ShapeDtypeStruct((s, d), F32), jax.ShapeDtypeStruct((1, d), F32)],
        compiler_params=_params("arbitrary"))(dpm, dpt, w_main, w_tail, x, gain, dy)


_INV_SQRT2 = 0.7071067811865476
_INV_SQRT2PI = 0.3989422804014327


def _gelu(x):
    return 0.5 * x * (1.0 + lax.erf(x * _INV_SQRT2))


def _gelu_grad(x):
    return 0.5 * (1.0 + lax.erf(x * _INV_SQRT2)) + x * jnp.exp(-0.5 * x * x) * _INV_SQRT2PI


def _tril_mask():
    r = lax.broadcasted_iota(jnp.int32, (CHUNK, CHUNK), 0)
    c = lax.broadcasted_iota(jnp.int32, (CHUNK, CHUNK), 1)
    return r >= c


def _gmlp_fwd(p, v_gain, w_s, b_full, name):
    s = p.shape[0]
    tm = _pick(s, 512)
    nch = tm // CHUNK

    def body(uv_ref, g_ref, w_ref, b_ref, o_ref):
        gel = _gelu(uv_ref[...].astype(F32))
        u, v = gel[:, :GM_WIDTH], gel[:, GM_WIDTH:]
        r = lax.rsqrt(jnp.mean(v * v, axis=-1, keepdims=True) + EPS)
        vn = (v * r * g_ref[...]).astype(BF16)
        mask = _tril_mask()
        for g in range(GM_GROUPS):
            wm = jnp.where(mask, w_ref[g], 0.0).astype(BF16)
            for c in range(nch):
                rs, cs = slice(c * CHUNK, (c + 1) * CHUNK), slice(g * LANES, (g + 1) * LANES)
                sp = jnp.dot(wm, vn[rs, cs], preferred_element_type=F32) + b_ref[g]
                o_ref[rs, cs] = (u[rs, cs] * sp).astype(BF16)

    full3 = pl.BlockSpec((GM_GROUPS, CHUNK, CHUNK), lambda i: (0, 0, 0))
    return pl.pallas_call(
        body, name=name, grid=(s // tm,),
        in_specs=[pl.BlockSpec((tm, 2 * GM_WIDTH), lambda i: (i, C_UV // (2 * GM_WIDTH))),
                  pl.BlockSpec((1, GM_WIDTH), lambda i: (0, 0)), full3, full3],
        out_specs=pl.BlockSpec((tm, GM_WIDTH), lambda i: (i, 0)),
        out_shape=jax.ShapeDtypeStruct((s, GM_WIDTH), BF16), compiler_params=_params("parallel"))(p, v_gain, w_s, b_full)


def _gmlp_bwd(p, v_gain, w_s, b_full, dy, name):
    s = p.shape[0]
    tm = _pick(s, 512)
    nch = tm // CHUNK
    nsteps = s // tm

    def body(uv_ref, g_ref, w_ref, b_ref, dy_ref, duv_ref, dg_ref, dw_ref, db_ref, dvn_s, dbacc):
        step = pl.program_id(0)

        @pl.when(step == 0)
        def _():
            dg_ref[...] = jnp.zeros_like(dg_ref)
            dw_ref[...] = jnp.zeros_like(dw_ref)
            dbacc[...] = jnp.zeros_like(dbacc)

        uv = uv_ref[...].astype(F32)
        gel = _gelu(uv)
        u, v = gel[:, :GM_WIDTH], gel[:, GM_WIDTH:]
        r = lax.rsqrt(jnp.mean(v * v, axis=-1, keepdims=True) + EPS)
        gain = g_ref[...]
        vn32 = v * r * gain
        vn = vn32.astype(BF16)
        dy = dy_ref[...].astype(F32)
        mask = _tril_mask()
        for g in range(GM_GROUPS):
            wm = jnp.where(mask, w_ref[g], 0.0).astype(BF16)
            dwg = jnp.zeros((CHUNK, CHUNK), F32)
            dbg = jnp.zeros((CHUNK, LANES), F32)
            for c in range(nch):
                rs, cs = slice(c * CHUNK, (c + 1) * CHUNK), slice(g * LANES, (g + 1) * LANES)
                sp = jnp.dot(wm, vn[rs, cs], preferred_element_type=F32) + b_ref[g]
                dyc = dy[rs, cs]
                dsp = dyc * u[rs, cs]
                dsp16 = dsp.astype(BF16)
                duv_ref[rs, cs] = (dyc * sp * _gelu_grad(uv[rs, cs])).astype(BF16)
                dvn_s[rs, cs] = lax.dot_general(wm, dsp16, (((0,), (0,)), ((), ())), preferred_element_type=F32)
                dwg = dwg + lax.dot_general(dsp16, vn[rs, cs], (((1,), (1,)), ((), ())), preferred_element_type=F32)
                dbg = dbg + dsp
            dw_ref[g] += jnp.where(mask, dwg, 0.0)
            dbacc[:, g * LANES:(g + 1) * LANES] += dbg
        dvn = dvn_s[...]
        uu = dvn * gain
        dv = r * uu - v * (r * r * r) * jnp.mean(v * uu, axis=-1, keepdims=True)
        duv_ref[:, GM_WIDTH:] = (dv * _gelu_grad(uv[:, GM_WIDTH:])).astype(BF16)
        dg_ref[...] += jnp.sum(dvn * v * r, axis=0, keepdims=True)

        @pl.when(step == nsteps - 1)
        def _():
            for g in range(GM_GROUPS):
                db_ref[:, g:g + 1] = jnp.sum(dbacc[:, g * LANES:(g + 1) * LANES], axis=1, keepdims=True)

    full3 = pl.BlockSpec((GM_GROUPS, CHUNK, CHUNK), lambda i: (0, 0, 0))
    return pl.pallas_call(
        body, name=name, grid=(nsteps,),
        in_specs=[pl.BlockSpec((tm, 2 * GM_WIDTH), lambda i: (i, C_UV // (2 * GM_WIDTH))),
                  pl.BlockSpec((1, GM_WIDTH), lambda i: (0, 0)), full3, full3,
                  pl.BlockSpec((tm, GM_WIDTH), lambda i: (i, 0))],
        out_specs=[pl.BlockSpec((tm, 2 * GM_WIDTH), lambda i: (i, 0)), pl.BlockSpec((1, GM_WIDTH), lambda i: (0, 0)),
                   full3, pl.BlockSpec((CHUNK, GM_GROUPS), lambda i: (0, 0))],
        out_shape=[jax.ShapeDtypeStruct((s, 2 * GM_WIDTH), BF16), jax.ShapeDtypeStruct((1, GM_WIDTH), F32),
                   jax.ShapeDtypeStruct((GM_GROUPS, CHUNK, CHUNK), F32), jax.ShapeDtypeStruct((CHUNK, GM_GROUPS), F32)],
        scratch_shapes=[pltpu.VMEM((tm, GM_WIDTH), F32), pltpu.VMEM((CHUNK, GM_WIDTH), F32)],
        compiler_params=_params("arbitrary"))(p, v_gain, w_s, b_full, dy)


def _rope(x, ct, s1, s2):
    return x * ct + pltpu.roll(x, LANES - MLA_ROPE // 2, 1) * s1 + pltpu.roll(x, MLA_ROPE // 2, 1) * s2


def _rope_bwd(d, ct, s1, s2):
    return d * ct + pltpu.roll(d * s1, MLA_ROPE // 2, 1) + pltpu.roll(d * s2, LANES - MLA_ROPE // 2, 1)


def _head_norm(x, gain):
    r = lax.rsqrt(jnp.sum(x * x, axis=-1, keepdims=True) * (1.0 / MLA_QK) + EPS)
    return x * r * gain, r


def _head_norm_bwd(x, r, gain, d):
    u = d * gain
    return r * u - x * (r * r * r) * (jnp.sum(x * u, axis=-1, keepdims=True) * (1.0 / MLA_QK))


def _mla_specs(tm):
    cq = pl.BlockSpec((tm, MLA_Q_RANK), lambda i: (i, T_CQ // MLA_Q_RANK))
    ckv = pl.BlockSpec((tm, MLA_KV_RANK), lambda i: (i, T_CKV // MLA_KV_RANK))
    kr = pl.BlockSpec((tm, LANES), lambda i: (i, T_KR // LANES))
    tab = pl.BlockSpec((tm, LANES), lambda i: (i, 0))
    return cq, ckv, kr, tab


def _const(shape):
    return pl.BlockSpec(shape, lambda i: tuple(0 for _ in shape))


def _mla_pre_fwd(p, tabs, qn_g, kvn_g, wuq, wkv, gq, gk, name):
    s = p.shape[0]
    tm = _pick(s, 256)
    ct, s1, s2 = tabs

    def body(cq_ref, ckv_ref, kr_ref, ct_ref, s1_ref, s2_ref, qg_ref, kvg_ref, wuq_ref, wkv_ref, gq_ref, gk_ref,
             q_ref, k_ref, v_ref):
        cq, ckv, kr = cq_ref[...], ckv_ref[...], kr_ref[...]
        ctv, s1v, s2v = ct_ref[...], s1_ref[...], s2_ref[...]
        rq = lax.rsqrt(jnp.mean(cq * cq, axis=-1, keepdims=True) + EPS)
        q = jnp.dot((cq * rq * qg_ref[...]).astype(BF16), wuq_ref[...], preferred_element_type=F32)
        rk = lax.rsqrt(jnp.mean(ckv * ckv, axis=-1, keepdims=True) + EPS)
        kv = jnp.dot((ckv * rk * kvg_ref[...]).astype(BF16), wkv_ref[...], preferred_element_type=F32)
        v_ref[...] = kv[:, HP:].astype(BF16)
        for h in range(MLA_HEADS):
            hs = slice(h * LANES, (h + 1) * LANES)
            qh, _ = _head_norm(q[:, hs], gq_ref[...])
            q_ref[:, hs] = (_rope(qh, ctv, s1v, s2v) * _Q_SCALE).astype(BF16)
            kh, _ = _head_norm(kv[:, hs] + kr, gk_ref[...])
            k_ref[:, hs] = _rope(kh, ctv, s1v, s2v).astype(BF16)

    cq_s, ckv_s, kr_s, tab_s = _mla_specs(tm)
    out = pl.BlockSpec((tm, HP), lambda i: (i, 0))
    return pl.pallas_call(
        body, name=name, grid=(s // tm,),
        in_specs=[cq_s, ckv_s, kr_s, tab_s, tab_s, tab_s, _const((1, MLA_Q_RANK)), _const((1, MLA_KV_RANK)),
                  _const((MLA_Q_RANK, HP)), _const((MLA_KV_RANK, 2 * HP)), _const((1, LANES)), _const((1, LANES))],
        out_specs=[out, out, out], out_shape=[jax.ShapeDtypeStruct((s, HP), BF16)] * 3,
        compiler_params=_params("parallel"))(p, p, p, ct, s1, s2, qn_g, kvn_g, wuq, wkv, gq, gk)


def _mla_pre_bwd(p, tabs, qn_g, kvn_g, wuq, wkv, gq, gk, dq, dk, dv, name, side=None):
    s = p.shape[0]
    tm = _pick(s, 256)
    ct, s1, s2 = tabs

    def body(cq_ref, ckv_ref, kr_ref, ct_ref, s1_ref, s2_ref, qg_ref, kvg_ref, wuq_ref, wkv_ref, gq_ref, gk_ref,
             dq_ref, dk_ref, dv_ref, dcq_ref, dckv_ref, dkr_ref, dwuq_ref, dwkv_ref, dqg_ref, dkvg_ref, dgq_ref, dgk_ref,
             dqp, dkvp):
        @pl.when(pl.program_id(0) == 0)
        def _():
            for ref in (dwuq_ref, dwkv_ref, dqg_ref, dkvg_ref, dgq_ref, dgk_ref):
                ref[...] = jnp.zeros_like(ref)

        cq, ckv, kr = cq_ref[...], ckv_ref[...], kr_ref[...]
        ctv, s1v, s2v = ct_ref[...], s1_ref[...], s2_ref[...]
        rq = lax.rsqrt(jnp.mean(cq * cq, axis=-1, keepdims=True) + EPS)
        qn = (cq * rq * qg_ref[...]).astype(BF16)
        q = jnp.dot(qn, wuq_ref[...], preferred_element_type=F32)
        rk = lax.rsqrt(jnp.mean(ckv * ckv, axis=-1, keepdims=True) + EPS)
        kvn = (ckv * rk * kvg_ref[...]).astype(BF16)
        kv = jnp.dot(kvn, wkv_ref[...], preferred_element_type=F32)
        gqv, gkv = gq_ref[...], gk_ref[...]
        dgq = jnp.zeros((1, LANES), F32)
        dgk = jnp.zeros((1, LANES), F32)
        dkr = jnp.zeros((tm, LANES), F32)
        for h in range(MLA_HEADS):
            hs = slice(h * LANES, (h + 1) * LANES)
            xq = q[:, hs]
            _, r = _head_norm(xq, gqv)
            d = _rope_bwd(dq_ref[:, hs].astype(F32), ctv, s1v, s2v)
            dgq = dgq + jnp.sum(d * xq * r, axis=0, keepdims=True)
            dqp[:, hs] = _head_norm_bwd(xq, r, gqv, d)
            xk = kv[:, hs] + kr
            _, r = _head_norm(xk, gkv)
            d = _rope_bwd(dk_ref[:, hs].astype(F32), ctv, s1v, s2v)
            dgk = dgk + jnp.sum(d * xk * r, axis=0, keepdims=True)
            dxk = _head_norm_bwd(xk, r, gkv, d)
            dkvp[:, hs] = dxk
            dkr = dkr + dxk
        dkvp[:, HP:] = dv_ref[...].astype(F32)
        dgq_ref[...] += dgq
        dgk_ref[...] += dgk
        dkr_ref[...] = dkr.astype(BF16)
        tn = (((0,), (0,)), ((), ()))
        nt = (((1,), (1,)), ((), ()))
        dq16 = dqp[...].astype(BF16)
        dwuq_ref[...] += lax.dot_general(qn, dq16, tn, preferred_element_type=F32)
        dqn = lax.dot_general(dq16, wuq_ref[...], nt, preferred_element_type=F32)
        dqg_ref[...] += jnp.sum(dqn * cq * rq, axis=0, keepdims=True)
        u = dqn * qg_ref[...]
        dcq_ref[...] = (rq * u - cq * (rq * rq * rq) * jnp.mean(cq * u, axis=-1, keepdims=True)).astype(BF16)
        dkv16 = dkvp[...].astype(BF16)
        dwkv_ref[...] += lax.dot_general(kvn, dkv16, tn, preferred_element_type=F32)
        dkvn = lax.dot_general(dkv16, wkv_ref[...], nt, preferred_element_type=F32)
        dkvg_ref[...] += jnp.sum(dkvn * ckv * rk, axis=0, keepdims=True)
        u = dkvn * kvg_ref[...]
        dckv_ref[...] = (rk * u - ckv * (rk * rk * rk) * jnp.mean(ckv * u, axis=-1, keepdims=True)).astype(BF16)

    cq_s, ckv_s, kr_s, tab_s = _mla_specs(tm)
    hd = pl.BlockSpec((tm, HP), lambda i: (i, 0))
    return _call(
        body, name=name, grid=(s // tm,),
        in_specs=[cq_s, ckv_s, kr_s, tab_s, tab_s, tab_s, _const((1, MLA_Q_RANK)), _const((1, MLA_KV_RANK)),
                  _const((MLA_Q_RANK, HP)), _const((MLA_KV_RANK, 2 * HP)), _const((1, LANES)), _const((1, LANES)),
                  hd, hd, hd],
        out_specs=[pl.BlockSpec((tm, MLA_Q_RANK), lambda i: (i, 0)), pl.BlockSpec((tm, MLA_KV_RANK), lambda i: (i, 0)),
                   pl.BlockSpec((tm, LANES), lambda i: (i, 0)), _const((MLA_Q_RANK, HP)), _const((MLA_KV_RANK, 2 * HP)),
                   _const((1, MLA_Q_RANK)), _const((1, MLA_KV_RANK)), _const((1, LANES)), _const((1, LANES))],
        out_shape=[jax.ShapeDtypeStruct((s, MLA_Q_RANK), BF16), jax.ShapeDtypeStruct((s, MLA_KV_RANK), BF16),
                   jax.ShapeDtypeStruct((s, LANES), BF16), jax.ShapeDtypeStruct((MLA_Q_RANK, HP), F32),
                   jax.ShapeDtypeStruct((MLA_KV_RANK, 2 * HP), F32), jax.ShapeDtypeStruct((1, MLA_Q_RANK), F32),
                   jax.ShapeDtypeStruct((1, MLA_KV_RANK), F32), jax.ShapeDtypeStruct((1, LANES), F32),
                   jax.ShapeDtypeStruct((1, LANES), F32)],
        scratch_shapes=[pltpu.VMEM((tm, HP), F32), pltpu.VMEM((tm, 2 * HP), F32)],
        args=(p, p, p, ct, s1, s2, qn_g, kvn_g, wuq, wkv, gq, gk, dq, dk, dv), semantics=("arbitrary",), side=side)


_ATT_SCALE = MLA_QK ** -0.5
_LOG2E = 1.4426950408889634
_Q_SCALE = _ATT_SCALE * _LOG2E
ATT_BLOCK = 1024
_NEG = -1e30
_NT = (((1,), (1,)), ((), ()))
_TN = (((0,), (0,)), ((), ()))


def _tri_rows(step, n):
    i = step * 0
    for m in range(1, n):
        i = i + (step >= m * (m + 1) // 2).astype(jnp.int32)
    return i, step - i * (i + 1) // 2


def _tri_cols(step, n):
    j = step * 0
    for m in range(1, n):
        j = j + (step >= m * n - m * (m - 1) // 2).astype(jnp.int32)
    return j, j + step - (j * n - j * (j - 1) // 2)


def _diag_mask(t):
    return lax.broadcasted_iota(jnp.int32, (t, t), 0) <= lax.broadcasted_iota(jnp.int32, (t, t), 1)


def _attn_fwd(q, k, v, name, side=None):
    s = q.shape[0]
    t = _pick(s, ATT_BLOCK)
    n = s // t

    def body(q_ref, k_ref, v_ref, o_ref, lse_ref, m_s, l_s, acc):
        i, j = _tri_rows(pl.program_id(1), n)

        @pl.when(j == 0)
        def _():
            m_s[...] = jnp.full_like(m_s, _NEG)
            l_s[...] = jnp.zeros_like(l_s)
            acc[...] = jnp.zeros_like(acc)

        def step(diagonal):
            sc = lax.dot_general(k_ref[...], q_ref[...], _NT, preferred_element_type=F32)
            if diagonal:
                sc = jnp.where(_diag_mask(t), sc, _NEG)
            m_new = jnp.maximum(m_s[...], jnp.max(sc, axis=0, keepdims=True))
            alpha = jnp.exp2(m_s[...] - m_new)
            pr = jnp.exp2(sc - m_new)
            l_s[...] = alpha * l_s[...] + jnp.sum(pr, axis=0, keepdims=True)
            acc[...] = alpha * acc[...] + lax.dot_general(v_ref[...], pr.astype(BF16), _TN, preferred_element_type=F32)
            m_s[...] = m_new

        @pl.when(j < i)
        def _():
            step(False)

        @pl.when(j == i)
        def _():
            step(True)
            o_ref[...] = (acc[...] / l_s[...]).T.astype(BF16)
            lse_ref[...] = m_s[...] + jnp.log2(l_s[...])

    qs = pl.BlockSpec((t, LANES), lambda h, p: (_tri_rows(p, n)[0], h))
    ks = pl.BlockSpec((t, LANES), lambda h, p: (_tri_rows(p, n)[1], h))
    return _call(
        body, name=name, grid=(MLA_HEADS, n * (n + 1) // 2), in_specs=[qs, ks, ks],
        out_specs=[qs, pl.BlockSpec((None, 1, t), lambda h, p: (h, 0, _tri_rows(p, n)[0]))],
        out_shape=[jax.ShapeDtypeStruct((s, HP), BF16), jax.ShapeDtypeStruct((MLA_HEADS, 1, s), F32)],
        scratch_shapes=[pltpu.VMEM((1, t), F32), pltpu.VMEM((1, t), F32), pltpu.VMEM((LANES, t), F32)],
        args=(q, k, v), semantics=("parallel", "arbitrary"), side=side)


def _attn_bwd_dq(q, k, v, o, lse, do, name, side=None):
    s = q.shape[0]
    t = _pick(s, ATT_BLOCK)
    n = s // t

    def body(q_ref, k_ref, v_ref, o_ref, lse_ref, do_ref, dq_ref, dl_ref, acc, dl_s):
        i, j = _tri_rows(pl.program_id(1), n)

        @pl.when(j == 0)
        def _():
            acc[...] = jnp.zeros_like(acc)
            dl_s[...] = jnp.sum((do_ref[...].astype(F32) * o_ref[...].astype(F32)).T, axis=0, keepdims=True)

        def step(diagonal):
            sc = lax.dot_general(k_ref[...], q_ref[...], _NT, preferred_element_type=F32)
            if diagonal:
                sc = jnp.where(_diag_mask(t), sc, _NEG)
            pr = jnp.exp2(sc - lse_ref[...])
            dp = lax.dot_general(v_ref[...], do_ref[...].astype(BF16), _NT, preferred_element_type=F32)
            ds = (pr * (dp - dl_s[...])).astype(BF16)
            acc[...] += lax.dot_general(k_ref[...], ds, _TN, preferred_element_type=F32)

        @pl.when(j < i)
        def _():
            step(False)

        @pl.when(j == i)
        def _():
            step(True)
            dq_ref[...] = (acc[...] * _ATT_SCALE).T.astype(BF16)
            dl_ref[...] = dl_s[...]

    qs = pl.BlockSpec((t, LANES), lambda h, p: (_tri_rows(p, n)[0], h))
    ks = pl.BlockSpec((t, LANES), lambda h, p: (_tri_rows(p, n)[1], h))
    ls = pl.BlockSpec((None, 1, t), lambda h, p: (h, 0, _tri_rows(p, n)[0]))
    return _call(
        body, name=name, grid=(MLA_HEADS, n * (n + 1) // 2), in_specs=[qs, ks, ks, qs, ls, qs], out_specs=[qs, ls],
        out_shape=[jax.ShapeDtypeStruct((s, HP), BF16), jax.ShapeDtypeStruct((MLA_HEADS, 1, s), F32)],
        scratch_shapes=[pltpu.VMEM((LANES, t), F32), pltpu.VMEM((1, t), F32)],
        args=(q, k, v, o, lse, do), semantics=("parallel", "arbitrary"), side=side)


def _attn_bwd_dkv(q, k, v, lse, delta, do, name, side=None):
    s = q.shape[0]
    t = _pick(s, ATT_BLOCK)
    n = s // t

    def body(q_ref, k_ref, v_ref, lse_ref, dl_ref, do_ref, dk_ref, dv_ref, dk_acc, dv_acc):
        j, i = _tri_cols(pl.program_id(1), n)

        def step(diagonal):
            sc = lax.dot_general(k_ref[...], q_ref[...], _NT, preferred_element_type=F32)
            if diagonal:
                sc = jnp.where(_diag_mask(t), sc, _NEG)
            pr = jnp.exp2(sc - lse_ref[...])
            do16 = do_ref[...].astype(BF16)
            dv_acc[...] += jnp.dot(pr.astype(BF16), do16, preferred_element_type=F32)
            dp = lax.dot_general(v_ref[...], do16, _NT, preferred_element_type=F32)
            ds = (pr * (dp - dl_ref[...])).astype(BF16)
            dk_acc[...] += jnp.dot(ds, q_ref[...], preferred_element_type=F32)

        @pl.when(i == j)
        def _():
            dk_acc[...] = jnp.zeros_like(dk_acc)
            dv_acc[...] = jnp.zeros_like(dv_acc)
            step(True)

        @pl.when(i > j)
        def _():
            step(False)

        @pl.when(i == n - 1)
        def _():
            dk_ref[...] = (dk_acc[...] * (1.0 / _LOG2E)).astype(BF16)
            dv_ref[...] = dv_acc[...].astype(BF16)

    qs = pl.BlockSpec((t, LANES), lambda h, p: (_tri_cols(p, n)[1], h))
    ks = pl.BlockSpec((t, LANES), lambda h, p: (_tri_cols(p, n)[0], h))
    ls = pl.BlockSpec((None, 1, t), lambda h, p: (h, 0, _tri_cols(p, n)[1]))
    return _call(
        body, name=name, grid=(MLA_HEADS, n * (n + 1) // 2), in_specs=[qs, ks, ks, ls, ls, qs], out_specs=[ks, ks],
        out_shape=[jax.ShapeDtypeStruct((s, HP), BF16)] * 2,
        scratch_shapes=[pltpu.VMEM((t, LANES), F32), pltpu.VMEM((t, LANES), F32)],
        args=(q, k, v, lse, delta, do), semantics=("parallel", "arbitrary"), side=side)


XBC = HP + 2 * SSD_GROUPS * SSD_STATE
BCW = 2 * SSD_GROUPS * SSD_STATE


def _conv_fwd(p, col0, width, conv_w, conv_b, name):
    s = p.shape[0]
    c0, nblk = col0 // LANES, width // LANES

    def body(x_ref, w_ref, b_ref, o_ref, pad):
        pad[0:8, :] = jnp.zeros((8, LANES), F32)
        pad[8:s + 8, :] = x_ref[...].astype(F32)
        acc = jnp.broadcast_to(b_ref[...], (s, LANES))
        for t in range(SSD_CONV):
            acc = acc + pad[pl.ds(8 - (SSD_CONV - 1) + t, s), :] * w_ref[t:t + 1, :]
        o_ref[...] = acc * _sigmoid(acc)

    return pl.pallas_call(
        body, name=name, grid=(nblk,),
        in_specs=[pl.BlockSpec((s, LANES), lambda j: (0, c0 + j)), pl.BlockSpec((SSD_CONV, LANES), lambda j: (0, j)),
                  pl.BlockSpec((1, LANES), lambda j: (0, j))],
        out_specs=pl.BlockSpec((s, LANES), lambda j: (0, j)), out_shape=jax.ShapeDtypeStruct((s, width), F32),
        scratch_shapes=[pltpu.VMEM((s + 8, LANES), F32)], compiler_params=_params("parallel"))(p, conv_w, conv_b)


def _conv_bwd(p, col0, width, conv_w, conv_b, dact, name):
    s = p.shape[0]
    c0, nblk = col0 // LANES, width // LANES

    def body(x_ref, w_ref, b_ref, d_ref, dx_ref, dw_ref, db_ref, pad, padd):
        pad[0:8, :] = jnp.zeros((8, LANES), F32)
        pad[8:s + 8, :] = x_ref[...].astype(F32)
        acc = jnp.broadcast_to(b_ref[...], (s, LANES))
        for t in range(SSD_CONV):
            acc = acc + pad[pl.ds(8 - (SSD_CONV - 1) + t, s), :] * w_ref[t:t + 1, :]
        sg = _sigmoid(acc)
        dpre = d_ref[...] * (sg * (1.0 + acc * (1.0 - sg)))
        padd[0:s, :] = dpre
        padd[s:s + 8, :] = jnp.zeros((8, LANES), F32)
        dx = jnp.zeros((s, LANES), F32)
        for t in range(SSD_CONV):
            dx = dx + padd[pl.ds(SSD_CONV - 1 - t, s), :] * w_ref[t:t + 1, :]
            dw_ref[t:t + 1, :] = jnp.sum(dpre * pad[pl.ds(8 - (SSD_CONV - 1) + t, s), :], axis=0, keepdims=True)
        dx_ref[...] = dx.astype(BF16)
        db_ref[...] = jnp.sum(dpre, axis=0, keepdims=True)

    blk = pl.BlockSpec((s, LANES), lambda j: (0, j))
    return pl.pallas_call(
        body, name=name, grid=(nblk,),
        in_specs=[pl.BlockSpec((s, LANES), lambda j: (0, c0 + j)), pl.BlockSpec((SSD_CONV, LANES), lambda j: (0, j)),
                  pl.BlockSpec((1, LANES), lambda j: (0, j)), blk],
        out_specs=[blk, pl.BlockSpec((SSD_CONV, LANES), lambda j: (0, j)), pl.BlockSpec((1, LANES), lambda j: (0, j))],
        out_shape=[jax.ShapeDtypeStruct((s, width), BF16), jax.ShapeDtypeStruct((SSD_CONV, width), F32),
                   jax.ShapeDtypeStruct((1, width), F32)],
        scratch_shapes=[pltpu.VMEM((s + 8, LANES), F32), pltpu.VMEM((s + 8, LANES), F32)],
        compiler_params=_params("parallel"))(p, conv_w, conv_b, dact)


def _softplus(x):
    return jnp.maximum(x, 0.0) + jnp.log(1.0 + jnp.exp(-jnp.abs(x)))


def _dt_fwd(p, dt_bias, a_log, name):
    s = p.shape[0]
    tm = _pick(s, 512)

    def body(x_ref, b_ref, a_ref, dt_ref, da_ref):
        dtv = _softplus(x_ref[...] + b_ref[...])
        dav = dtv * (-jnp.exp(a_ref[...]))
        for h in range(SSD_HEADS):
            hs = slice(h * LANES, (h + 1) * LANES)
            dt_ref[:, hs] = jnp.broadcast_to(dtv[:, h:h + 1], (tm, LANES))
            da_ref[:, hs] = jnp.broadcast_to(dav[:, h:h + 1], (tm, LANES))

    out = pl.BlockSpec((tm, HP), lambda i: (i, 0))
    return pl.pallas_call(
        body, name=name, grid=(s // tm,),
        in_specs=[pl.BlockSpec((tm, LANES), lambda i: (i, T_DT // LANES)), _const((1, LANES)), _const((1, LANES))],
        out_specs=[out, out], out_shape=[jax.ShapeDtypeStruct((s, HP), F32)] * 2,
        compiler_params=_params("parallel"))(p, dt_bias, a_log)


def _dt_bwd(p, dt_bias, a_log, dda, ddtx, name):
    s = p.shape[0]
    tm = _pick(s, 512)

    def body(x_ref, b_ref, a_ref, dda_ref, ddtx_ref, dx_ref, db_ref, dal_ref):
        @pl.when(pl.program_id(0) == 0)
        def _():
            db_ref[...] = jnp.zeros_like(db_ref)
            dal_ref[...] = jnp.zeros_like(dal_ref)

        x = x_ref[...] + b_ref[...]
        dtv = _softplus(x)
        av = -jnp.exp(a_ref[...])
        lane = lax.broadcasted_iota(jnp.int32, (tm, LANES), 1)
        pa = jnp.zeros((tm, LANES), F32)
        px = jnp.zeros((tm, LANES), F32)
        for h in range(SSD_HEADS):
            pa = jnp.where(lane == h, dda_ref[:, h * LANES:(h + 1) * LANES], pa)
            px = jnp.where(lane == h, ddtx_ref[:, h * LANES:(h + 1) * LANES], px)
        draw = (pa * av + px) * _sigmoid(x)
        dx_ref[...] = draw.astype(BF16)
        db_ref[...] += jnp.sum(draw, axis=0, keepdims=True)
        dal_ref[...] += jnp.sum(pa * dtv, axis=0, keepdims=True) * av

    hd = pl.BlockSpec((tm, HP), lambda i: (i, 0))
    return pl.pallas_call(
        body, name=name, grid=(s // tm,),
        in_specs=[pl.BlockSpec((tm, LANES), lambda i: (i, T_DT // LANES)), _const((1, LANES)), _const((1, LANES)), hd, hd],
        out_specs=[pl.BlockSpec((tm, LANES), lambda i: (i, 0)), _const((1, LANES)), _const((1, LANES))],
        out_shape=[jax.ShapeDtypeStruct((s, LANES), BF16), jax.ShapeDtypeStruct((1, LANES), F32),
                   jax.ShapeDtypeStruct((1, LANES), F32)],
        compiler_params=_params("arbitrary"))(p, dt_bias, a_log, dda, ddtx)


def _cumsum_rows(x):
    row = lax.broadcasted_iota(jnp.int32, x.shape, 0)
    k = 1
    while k < x.shape[0]:
        x = x + jnp.where(row >= k, pltpu.roll(x, k, 0), 0.0)
        k *= 2
    return x


def _rev_cumsum_rows(x):
    n = x.shape[0]
    row = lax.broadcasted_iota(jnp.int32, x.shape, 0)
    k = 1
    while k < n:
        x = x + jnp.where(row < n - k, pltpu.roll(x, n - k, 0), 0.0)
        k *= 2
    return x


HPG = SSD_HEADS // SSD_GROUPS


def _chunk_decay(da):
    cs = _cumsum_rows(da)
    lm = jnp.exp(jnp.where(_tril_mask(), cs - cs.T, _NEG))
    return cs, lm, cs[CHUNK - 1:CHUNK, :]


def _scan_fwd(xs, bc, dtb, dab, name, side=None):
    s = xs.shape[0]
    nc = s // CHUNK

    def body(x_ref, b_ref, c_ref, dt_ref, da_ref, y_ref, sin_ref, state):
        @pl.when(pl.program_id(1) == 0)
        def _():
            state[...] = jnp.zeros_like(state)

        bv = b_ref[...]
        b16, c16 = bv.astype(BF16), c_ref[...].astype(BF16)
        g = lax.dot_general(c16, b16, _NT, preferred_element_type=F32)
        for hh in range(HPG):
            hs = slice(hh * LANES, (hh + 1) * LANES)
            st = state[hh]
            sin_ref[hh] = st
            cs, lm, cl = _chunk_decay(da_ref[:, hs])
            xd = (x_ref[:, hs] * dt_ref[:, hs]).astype(BF16)
            y = jnp.dot((g * lm).astype(BF16), xd, preferred_element_type=F32)
            y_ref[:, hs] = y + jnp.dot(c16, st.astype(BF16), preferred_element_type=F32) * jnp.exp(cs)
            bd = (bv * jnp.exp(cl - cs)).astype(BF16)
            state[hh] = jnp.exp(cl) * st + lax.dot_general(bd, xd, _TN, preferred_element_type=F32)

    gw = HPG * LANES
    hd = pl.BlockSpec((CHUNK, gw), lambda g, c: (c, g))
    return _call(
        body, name=name, grid=(SSD_GROUPS, nc),
        in_specs=[hd, pl.BlockSpec((CHUNK, LANES), lambda g, c: (c, g)),
                  pl.BlockSpec((CHUNK, LANES), lambda g, c: (c, SSD_GROUPS + g)), hd, hd],
        out_specs=[hd, pl.BlockSpec((HPG, None, SSD_STATE, LANES), lambda g, c: (g, c, 0, 0))],
        out_shape=[jax.ShapeDtypeStruct((s, HP), F32), jax.ShapeDtypeStruct((SSD_HEADS, nc, SSD_STATE, LANES), F32)],
        scratch_shapes=[pltpu.VMEM((HPG, SSD_STATE, LANES), F32)],
        args=(xs, bc, bc, dtb, dab), semantics=("parallel", "arbitrary"), side=side)


def _scan_bwd(xs, bc, dtb, dab, s_in, dy, d_vec, name):
    s = xs.shape[0]
    nc = s // CHUNK

    def body(x_ref, b_ref, c_ref, dt_ref, da_ref, sin_ref, dy_ref, dv_ref, dx_ref, db_ref, dc_ref, dda_ref, ddtx_ref, dstate):
        @pl.when(pl.program_id(1) == 0)
        def _():
            dstate[...] = jnp.zeros_like(dstate)

        bv = b_ref[...]
        b16, c16 = bv.astype(BF16), c_ref[...].astype(BF16)
        g = lax.dot_general(c16, b16, _NT, preferred_element_type=F32)
        row = lax.broadcasted_iota(jnp.int32, (CHUNK, 1), 0)
        dbm = jnp.zeros((CHUNK, SSD_STATE), F32)
        dcm = jnp.zeros((CHUNK, SSD_STATE), F32)
        for hh in range(HPG):
            hs = slice(hh * LANES, (hh + 1) * LANES)
            st, ds = sin_ref[hh], dstate[hh]
            st16, ds16 = st.astype(BF16), ds.astype(BF16)
            xv, dtv, dyv = x_ref[:, hs], dt_ref[:, hs], dy_ref[:, hs]
            cs, lm, cl = _chunk_decay(da_ref[:, hs])
            ecs, ecl = jnp.exp(cs), jnp.exp(cl)
            decay = jnp.exp(cl - cs)
            xd = (xv * dtv).astype(BF16)
            dy16 = dyv.astype(BF16)
            dye = (dyv * ecs).astype(BF16)
            yoff = jnp.dot(c16, st16, preferred_element_type=F32) * ecs
            dcs = jnp.sum(dyv * yoff, axis=-1, keepdims=True)
            dcm = dcm + lax.dot_general(dye, st16, _NT, preferred_element_type=F32)
            dstate[hh] = ecl * ds + lax.dot_general(c16, dye, _TN, preferred_element_type=F32)
            dcl = jnp.sum(jnp.sum(ds * st, axis=0, keepdims=True), axis=1, keepdims=True) * ecl[:, 0:1]
            bd32 = bv * decay
            qm = lax.dot_general(xd, ds16, _NT, preferred_element_type=F32)
            dbm = dbm + qm * decay
            w = jnp.sum(bd32 * qm, axis=-1, keepdims=True)
            dcs = dcs - w
            dcl = dcl + jnp.sum(w, axis=0, keepdims=True)
            dxd = jnp.dot(bd32.astype(BF16), ds16, preferred_element_type=F32)
            m16 = (g * lm).astype(BF16)
            dm = lax.dot_general(dy16, xd, _NT, preferred_element_type=F32)
            dxd = dxd + lax.dot_general(m16, dy16, _TN, preferred_element_type=F32)
            dg = dm * lm
            dg16 = dg.astype(BF16)
            tt = dg * g
            dcm = dcm + jnp.dot(dg16, b16, preferred_element_type=F32)
            dbm = dbm + lax.dot_general(dg16, c16, _TN, preferred_element_type=F32)
            dcs = dcs + jnp.sum(tt, axis=-1, keepdims=True) - jnp.sum(tt.T, axis=-1, keepdims=True)
            dcs = dcs + jnp.where(row == CHUNK - 1, dcl, 0.0)
            dda_ref[:, hs] = _rev_cumsum_rows(jnp.broadcast_to(dcs, (CHUNK, LANES)))
            ddtx_ref[:, hs] = jnp.broadcast_to(jnp.sum(dxd * xv, axis=-1, keepdims=True), (CHUNK, LANES))
            dx_ref[:, hs] = dxd * dtv + dyv * dv_ref[:, hs]
        db_ref[...] = dbm
        dc_ref[...] = dcm

    gw = HPG * LANES
    hd = pl.BlockSpec((CHUNK, gw), lambda g, c: (nc - 1 - c, g))
    gp = pl.BlockSpec((CHUNK, LANES), lambda g, c: (nc - 1 - c, g))
    return pl.pallas_call(
        body, name=name, grid=(SSD_GROUPS, nc),
        in_specs=[hd, gp, pl.BlockSpec((CHUNK, LANES), lambda g, c: (nc - 1 - c, SSD_GROUPS + g)), hd, hd,
                  pl.BlockSpec((HPG, None, SSD_STATE, LANES), lambda g, c: (g, nc - 1 - c, 0, 0)), hd,
                  pl.BlockSpec((1, gw), lambda g, c: (0, g))],
        out_specs=[hd, gp, gp, hd, hd],
        out_shape=[jax.ShapeDtypeStruct((s, HP), F32), jax.ShapeDtypeStruct((s, SSD_GROUPS * SSD_STATE), F32),
                   jax.ShapeDtypeStruct((s, SSD_GROUPS * SSD_STATE), F32), jax.ShapeDtypeStruct((s, HP), F32),
                   jax.ShapeDtypeStruct((s, HP), F32)],
        scratch_shapes=[pltpu.VMEM((HPG, SSD_STATE, LANES), F32)],
        compiler_params=_params("parallel", "arbitrary"))(xs, bc, bc, dtb, dab, s_in, dy, d_vec)


_GN = SSD_INNER // SSD_GROUPS
_GW = HP // SSD_GROUPS


def _ssd_post_fwd(y, xbc, p, d_vec, gain, name):
    s = y.shape[0]
    tm = _pick(s, 512)

    def body(y_ref, x_ref, z_ref, d_ref, g_ref, o_ref):
        z = z_ref[...].astype(F32)
        y2 = (y_ref[...] + x_ref[...] * d_ref[...]) * (z * _sigmoid(z))
        for g in range(SSD_GROUPS):
            gs = slice(g * _GW, (g + 1) * _GW)
            yg = y2[:, gs]
            r = lax.rsqrt(jnp.sum(yg * yg, axis=-1, keepdims=True) * (1.0 / _GN) + EPS)
            o_ref[:, gs] = (yg * r * g_ref[:, gs]).astype(BF16)

    hd = pl.BlockSpec((tm, HP), lambda i: (i, 0))
    return pl.pallas_call(
        body, name=name, grid=(s // tm,),
        in_specs=[hd, hd, pl.BlockSpec((tm, HP), lambda i: (i, C_Z // HP)), _const((1, HP)), _const((1, HP))],
        out_specs=hd, out_shape=jax.ShapeDtypeStruct((s, HP), BF16), compiler_params=_params("parallel"))(y, xbc, p, d_vec, gain)


def _ssd_post_bwd(y, xbc, p, d_vec, gain, dyn, name):
    s = y.shape[0]
    tm = _pick(s, 512)

    def body(y_ref, x_ref, z_ref, d_ref, g_ref, dn_ref, dy_ref, dz_ref, dg_ref, dd_ref):
        @pl.when(pl.program_id(0) == 0)
        def _():
            dg_ref[...] = jnp.zeros_like(dg_ref)
            dd_ref[...] = jnp.zeros_like(dd_ref)

        z, xv = z_ref[...].astype(F32), x_ref[...]
        sg = _sigmoid(z)
        sz = z * sg
        yt = y_ref[...] + xv * d_ref[...]
        y2 = yt * sz
        for g in range(SSD_GROUPS):
            gs = slice(g * _GW, (g + 1) * _GW)
            yg, dn = y2[:, gs], dn_ref[:, gs].astype(F32)
            r = lax.rsqrt(jnp.sum(yg * yg, axis=-1, keepdims=True) * (1.0 / _GN) + EPS)
            u = dn * g_ref[:, gs]
            dy2 = r * u - yg * (r * r * r) * (jnp.sum(yg * u, axis=-1, keepdims=True) * (1.0 / _GN))
            dg_ref[:, gs] += jnp.sum(dn * yg * r, axis=0, keepdims=True)
            dyt = dy2 * sz[:, gs]
            dy_ref[:, gs] = dyt
            dz_ref[:, gs] = (dy2 * yt[:, gs] * (sg[:, gs] * (1.0 + z[:, gs] * (1.0 - sg[:, gs])))).astype(BF16)
            dd_ref[:, gs] += jnp.sum(dyt * xv[:, gs], axis=0, keepdims=True)

    hd = pl.BlockSpec((tm, HP), lambda i: (i, 0))
    return pl.pallas_call(
        body, name=name, grid=(s // tm,),
        in_specs=[hd, hd, pl.BlockSpec((tm, HP), lambda i: (i, C_Z // HP)), _const((1, HP)), _const((1, HP)), hd],
        out_specs=[hd, hd, _const((1, HP)), _const((1, HP))],
        out_shape=[jax.ShapeDtypeStruct((s, HP), F32), jax.ShapeDtypeStruct((s, HP), BF16),
                   jax.ShapeDtypeStruct((1, HP), F32), jax.ShapeDtypeStruct((1, HP), F32)],
        compiler_params=_params("arbitrary"))(y, xbc, p, d_vec, gain, dyn)


def _merge_fwd(p, ya, o, yc, wb0, wb1, wb2, w_out, x, name, side=None):
    s = p.shape[0]
    tm = _pick(s, 512)

    def body(g_ref, ya_ref, o_ref, yc_ref, w0_ref, w1_ref, w2_ref, wo_ref, x_ref, mg_ref, y_ref):
        acc = jnp.zeros((tm, D_MODEL), F32)
        for i, (b_ref, w_ref) in enumerate(((ya_ref, w0_ref), (o_ref, w1_ref), (yc_ref, w2_ref))):
            t = jnp.dot(b_ref[...].astype(BF16), w_ref[...], preferred_element_type=F32)
            acc = acc + _sigmoid(g_ref[:, i * D_MODEL:(i + 1) * D_MODEL].astype(F32)) * t
        mg = acc.astype(BF16)
        mg_ref[...] = mg
        y_ref[...] = x_ref[...] + jnp.dot(mg, wo_ref[...], preferred_element_type=F32)

    row = pl.BlockSpec((tm, D_MODEL), lambda i: (i, 0))
    return _call(
        body, name=name, grid=(s // tm,),
        in_specs=[pl.BlockSpec((tm, 3 * D_MODEL), lambda i: (i, C_G // (3 * D_MODEL))),
                  pl.BlockSpec((tm, GM_WIDTH), lambda i: (i, 0)), row, row,
                  _resident((GM_WIDTH, D_MODEL)), _resident((HP, D_MODEL)), _resident((HP, D_MODEL)),
                  _resident((D_MODEL, D_MODEL)), row],
        out_specs=[row, row],
        out_shape=[jax.ShapeDtypeStruct((s, D_MODEL), BF16), jax.ShapeDtypeStruct((s, D_MODEL), F32)],
        scratch_shapes=[], args=(p, ya, o, yc, wb0, wb1, wb2, w_out, x), semantics=("parallel",), side=side)


def _merge_bwd(p, ya, o, yc, wb0, wb1, wb2, w_out, dy, name):
    s = p.shape[0]
    tm = _pick(s, 512)

    def body(g_ref, ya_ref, o_ref, yc_ref, w0_ref, w1_ref, w2_ref, wo_ref, dy_ref,
             d0_ref, d1_ref, d2_ref, dg_ref, dya_ref, do_ref, dyc_ref):
        dm = lax.dot_general(dy_ref[...].astype(BF16), wo_ref[...], _NT, preferred_element_type=F32)
        for i, (b_ref, w_ref, d_ref, db_ref) in enumerate(((ya_ref, w0_ref, d0_ref, dya_ref), (o_ref, w1_ref, d1_ref, do_ref),
                                                            (yc_ref, w2_ref, d2_ref, dyc_ref))):
            cs = slice(i * D_MODEL, (i + 1) * D_MODEL)
            t = jnp.dot(b_ref[...].astype(BF16), w_ref[...], preferred_element_type=F32)
            sg = _sigmoid(g_ref[:, cs].astype(F32))
            dt16 = (dm * sg).astype(BF16)
            d_ref[...] = dt16
            dg_ref[:, cs] = (dm * t * sg * (1.0 - sg)).astype(BF16)
            db_ref[...] = lax.dot_general(dt16, w_ref[...], _NT, preferred_element_type=F32).astype(db_ref.dtype)

    row = pl.BlockSpec((tm, D_MODEL), lambda i: (i, 0))
    nar = pl.BlockSpec((tm, GM_WIDTH), lambda i: (i, 0))
    wide = pl.BlockSpec((tm, 3 * D_MODEL), lambda i: (i, 0))
    return pl.pallas_call(
        body, name=name, grid=(s // tm,),
        in_specs=[pl.BlockSpec((tm, 3 * D_MODEL), lambda i: (i, C_G // (3 * D_MODEL))), nar, row, row,
                  _resident((GM_WIDTH, D_MODEL)), _resident((HP, D_MODEL)), _resident((HP, D_MODEL)),
                  _resident((D_MODEL, D_MODEL)), row],
        out_specs=[row, row, row, wide, nar, row, row],
        out_shape=[jax.ShapeDtypeStruct((s, D_MODEL), BF16)] * 3 + [jax.ShapeDtypeStruct((s, 3 * D_MODEL), BF16),
                   jax.ShapeDtypeStruct((s, GM_WIDTH), BF16), jax.ShapeDtypeStruct((s, D_MODEL), BF16),
                   jax.ShapeDtypeStruct((s, D_MODEL), F32)],
        compiler_params=_params("parallel"))(p, ya, o, yc, wb0, wb1, wb2, w_out, dy)


def _loss_head(y, target, name):
    s, d = y.shape
    tm = _pick(s, 512)

    def body(y_ref, t_ref, dy_ref, sq_ref):
        @pl.when(pl.program_id(0) == 0)
        def _():
            sq_ref[...] = jnp.zeros_like(sq_ref)

        e = y_ref[...] - t_ref[...]
        dy_ref[...] = e * (1.0 / d)
        sq_ref[...] += jnp.sum(e * e, axis=0, keepdims=True)

    row = pl.BlockSpec((tm, d), lambda i: (i, 0))
    return pl.pallas_call(
        body, name=name, grid=(s // tm,), in_specs=[row, row], out_specs=[row, _const((1, d))],
        out_shape=[jax.ShapeDtypeStruct((s, d), F32), jax.ShapeDtypeStruct((1, d), F32)],
        compiler_params=_params("arbitrary"))(y, target)


def _adamw(w, g, m, v, name):
    rows, cols = w.shape
    tr = rows
    for cand in (512, 256, 128, 64, 32, 16, 8):
        if rows % cand == 0 and cand * cols * 4 <= 3 * 1024 * 1024:
            tr = cand
            break

    def body(w_ref, g_ref, m_ref, v_ref, d_ref, nm_ref, nv_ref):
        d_ref[...], nm_ref[...], nv_ref[...] = _adam_update(w_ref[...], g_ref[...], m_ref[...], v_ref[...])

    blk = pl.BlockSpec((tr, cols), lambda i: (i, 0))
    return pl.pallas_call(
        body, name=name, grid=(rows // tr,), in_specs=[blk] * 4, out_specs=[blk] * 3,
        out_shape=[jax.ShapeDtypeStruct((rows, cols), F32)] * 3, compiler_params=_params("parallel"))(w, g, m, v)


def _adam_update(w, g, m, v):
    nm = ADAM_B1 * m + (1.0 - ADAM_B1) * g
    nv = ADAM_B2 * v + (1.0 - ADAM_B2) * (g * g)
    c1 = 1.0 - ADAM_B1 ** ADAM_STEP
    c2 = 1.0 - ADAM_B2 ** ADAM_STEP
    return -ADAM_LR * ((nm / c1) / (jnp.sqrt(nv / c2) + ADAM_EPS) + ADAM_WD * w), nm, nv


def _adamw_sharded(w, m, v, mine, theirs, name, side=None):
    depth, rows, cols = w.shape
    tr = _row_tile(rows // 2, cols, 1024 * 1024)
    nb = rows // 2 // tr

    def body(w_ref, m_ref, v_ref, a_ref, b_ref, g_ref, d_ref, nm_ref, nv_ref):
        c = lax.axis_index("c")
        g = jnp.where(pl.program_id(1) // nb == c, a_ref[...], b_ref[...])
        g_ref[...] = g
        d_ref[...], nm_ref[...], nv_ref[...] = _adam_update(w_ref[...], g, m_ref[...], v_ref[...])

    blk = pl.BlockSpec((None, tr, cols), lambda l, i: (l, i, 0))
    mine_s = pl.BlockSpec((None, tr, cols), lambda l, i: (l, jnp.where(i // nb == lax.axis_index("c"), i % nb, 0), 0))
    theirs_s = pl.BlockSpec((None, tr, cols), lambda l, i: (l, jnp.where(i // nb == lax.axis_index("c"), 0, i % nb), 0))
    return _call(
        body, name=name, grid=(depth, rows // tr), in_specs=[blk, blk, blk, mine_s, theirs_s], out_specs=[blk] * 4,
        out_shape=[jax.ShapeDtypeStruct((depth, rows, cols), F32)] * 4, scratch_shapes=[],
        args=(w, m, v, mine, theirs), semantics=("parallel", "parallel"), side=side)


ANY = pl.BlockSpec(memory_space=pl.ANY)


def _me():
    return lax.axis_index("x"), lax.axis_index("y"), lax.axis_index("c")


def _other_chips(x, y):
    return [(1 - x, y), (x, 1 - y), (1 - x, 1 - y)]


def _chip_index(cx, cy):
    return 2 * cx + cy


class _Exchange:
    def __init__(self, ins, out_shapes, n_sems, start, finish):
        self.ins, self.out_shapes, self.n_sems, self.start, self.finish = list(ins), list(out_shapes), n_sems, start, finish


def _sem_scratch(ex):
    return [pltpu.SemaphoreType.DMA((ex.n_sems,)), pltpu.SemaphoreType.DMA((ex.n_sems,))]


def _run_exchange(ex, name):
    n_in, n_out = len(ex.ins), len(ex.out_shapes)

    def body(*refs):
        in_refs, out_refs, (send, recv) = refs[:n_in], refs[n_in:n_in + n_out], refs[n_in + n_out:]
        ex.start(in_refs, out_refs, send, recv)
        ex.finish(in_refs, out_refs, send, recv)

    return pl.pallas_call(body, name=name, in_specs=[ANY] * n_in, out_specs=[ANY] * n_out, out_shape=ex.out_shapes,
                          scratch_shapes=_sem_scratch(ex))(*ex.ins)


def _call(body, *, name, grid, in_specs, out_specs, out_shape, scratch_shapes, args, semantics, side=None):
    if side is None:
        return pl.pallas_call(body, name=name, grid=grid, in_specs=in_specs, out_specs=out_specs, out_shape=out_shape,
                              scratch_shapes=scratch_shapes, compiler_params=_params(*semantics))(*args), []
    n_in, n_out, n_sc = len(in_specs), len(out_specs), len(scratch_shapes)
    s_in, s_out = len(side.ins), len(side.out_shapes)

    def hosted(*refs):
        pos = 0
        parts = []
        for size in (n_in, s_in, n_out, s_out, n_sc, 2):
            parts.append(refs[pos:pos + size])
            pos += size
        ins, sins, outs, souts, scratch, (send, recv) = parts
        ids = [pl.program_id(a) for a in range(len(grid))]
        first = functools.reduce(jnp.logical_and, [i == 0 for i in ids])
        last = functools.reduce(jnp.logical_and, [i == g - 1 for i, g in zip(ids, grid)])

        @pl.when(first)
        def _():
            side.start(sins, souts, send, recv)

        body(*ins, *outs, *scratch)

        @pl.when(last)
        def _():
            side.finish(sins, souts, send, recv)

    res = pl.pallas_call(
        hosted, name=name, grid=grid, in_specs=list(in_specs) + [ANY] * s_in, out_specs=list(out_specs) + [ANY] * s_out,
        out_shape=list(out_shape) + side.out_shapes, scratch_shapes=list(scratch_shapes) + _sem_scratch(side),
        compiler_params=_params(*["arbitrary"] * len(grid)))(*args, *side.ins)
    return res[:n_out], res[n_out:]


def _half(ref_rows, c):
    return pl.ds(c * (ref_rows // 2), ref_rows // 2)


def _gather_exchange(shards, layer):
    n = len(shards)
    rows = [a.shape[1] for a in shards]

    def copy(in_refs, out_refs, send, recv, t, k, chip, hc, to, from_input=False):
        dst = out_refs[t].at[chip, _half(rows[t], hc)]
        src = in_refs[t].at[layer, _half(rows[t], hc)] if from_input else dst
        return pltpu.make_async_remote_copy(src_ref=src, dst_ref=dst, send_sem=send.at[7 * t + k], recv_sem=recv.at[7 * t + k],
                                            device_id=to, device_id_type=MESH)

    def own(in_refs, out_refs, send, recv, t):
        x, y, c = _me()
        return pltpu.make_async_remote_copy(src_ref=in_refs[t].at[layer], dst_ref=out_refs[t].at[_chip_index(x, y)],
                                            send_sem=send.at[7 * t + 6], recv_sem=recv.at[7 * t + 6],
                                            device_id=(x, y, 1 - c), device_id_type=MESH)

    def start(in_refs, out_refs, send, recv):
        x, y, c = _me()
        for j, chip in enumerate(_other_chips(x, y)):
            for t in range(n):
                copy(in_refs, out_refs, send, recv, t, j, _chip_index(x, y), c, (*chip, c), from_input=True).start()
        for t in range(n):
            own(in_refs, out_refs, send, recv, t).start()

    def finish(in_refs, out_refs, send, recv):
        x, y, c = _me()
        chips = _other_chips(x, y)
        passed = []
        for t in range(n):
            own(in_refs, out_refs, send, recv, t).wait()
        for j, chip in enumerate(chips):
            for t in range(n):
                copy(in_refs, out_refs, send, recv, t, j, _chip_index(*chip), c, (x, y, c)).wait_recv()
                cp = copy(in_refs, out_refs, send, recv, t, 3 + j, _chip_index(*chip), c, (x, y, 1 - c))
                cp.start()
                passed.append(cp)
        for j, chip in enumerate(chips):
            for t in range(n):
                copy(in_refs, out_refs, send, recv, t, 3 + j, _chip_index(*chip), 1 - c, (x, y, c)).wait_recv()
                copy(in_refs, out_refs, send, recv, t, j, _chip_index(x, y), c, (*chip, c), from_input=True).wait_send()
        for cp in passed:
            cp.wait_send()

    return _Exchange(shards, [jax.ShapeDtypeStruct((N_CHIPS,) + a.shape[1:], a.dtype) for a in shards], 7 * n, start, finish)


def _pair_exchange(gs):
    n = len(gs)
    rows = [a.shape[1] for a in gs]

    def copies(in_refs, out_refs, send, recv):
        x, y, c = _me()
        return [pltpu.make_async_remote_copy(src_ref=in_refs[t].at[:, _half(rows[t], 1 - c)], dst_ref=out_refs[t],
                                             send_sem=send.at[t], recv_sem=recv.at[t], device_id=(x, y, 1 - c),
                                             device_id_type=MESH) for t in range(n)]

    def start(*refs):
        for cp in copies(*refs):
            cp.start()

    def finish(*refs):
        for cp in copies(*refs):
            cp.wait()

    return _Exchange(gs, [jax.ShapeDtypeStruct((N_CHIPS, a.shape[1] // 2, a.shape[2]), a.dtype) for a in gs], n, start, finish)


def _row_tile(rows, cols, budget=2 * 1024 * 1024):
    best = None
    for t in range(8, rows + 1, 8):
        if rows % t == 0 and t * cols * 4 <= budget:
            best = t
    return best or rows


def _pair_add(g, got, name):
    _, rows, cols = g.shape
    tr = _row_tile(rows // 2, cols)
    nb = rows // 2 // tr

    def body(g_ref, r_ref, o16_ref, own_ref):
        x, y, _ = _me()
        tot = g_ref[...] + r_ref[...]
        o16_ref[...] = tot.astype(BF16)

        @pl.when(pl.program_id(1) == _chip_index(x, y))
        def _():
            own_ref[...] = tot

    blk = (None, tr, cols)
    return pl.pallas_call(
        body, name=name, grid=(nb, N_CHIPS),
        in_specs=[pl.BlockSpec(blk, lambda i, k: (k, i + lax.axis_index("c") * nb, 0)),
                  pl.BlockSpec(blk, lambda i, k: (k, i, 0))],
        out_specs=[pl.BlockSpec(blk, lambda i, k: (k, i, 0)), pl.BlockSpec((tr, cols), lambda i, k: (i, 0))],
        out_shape=[jax.ShapeDtypeStruct((N_CHIPS, rows // 2, cols), BF16), jax.ShapeDtypeStruct((rows // 2, cols), F32)],
        compiler_params=_params("parallel", "arbitrary"))(g, got)


def _chip_exchange(parts):
    n = len(parts)

    def copies(in_refs, out_refs, send, recv):
        x, y, c = _me()
        return [pltpu.make_async_remote_copy(src_ref=in_refs[t].at[_chip_index(*chip)], dst_ref=out_refs[t].at[j],
                                             send_sem=send.at[3 * t + j], recv_sem=recv.at[3 * t + j],
                                             device_id=(*chip, c), device_id_type=MESH)
                for j, chip in enumerate(_other_chips(x, y)) for t in range(n)]

    def start(*refs):
        for cp in copies(*refs):
            cp.start()

    def finish(*refs):
        for cp in copies(*refs):
            cp.wait()

    return _Exchange(parts, [jax.ShapeDtypeStruct((3,) + a.shape[1:], a.dtype) for a in parts], 3 * n, start, finish)


def _chip_add(own, got, name, layer, into=None):
    rows, cols = own.shape
    tr = _row_tile(rows, cols, 1024 * 1024)

    def body(own_ref, got_ref, *rest):
        acc = own_ref[...]
        for j in range(3):
            acc = acc + got_ref[j].astype(F32)
        rest[-1][...] = acc

    in_specs = [pl.BlockSpec((tr, cols), lambda i: (i, 0)), pl.BlockSpec((3, tr, cols), lambda i: (0, i, 0))]
    args, alias = [own, got], {}
    if into is not None:
        in_specs.append(ANY)
        args.append(into)
        alias = {2: 0}
    return pl.pallas_call(
        body, name=name, grid=(rows // tr,), in_specs=in_specs,
        out_specs=pl.BlockSpec((None, tr, cols), lambda i: (layer, i, 0)),
        out_shape=jax.ShapeDtypeStruct((DEPTH, rows, cols), F32), input_output_aliases=alias,
        compiler_params=_params("parallel"))(*args)


def _pair_share(halves):
    n = len(halves)

    def copies(in_refs, out_refs, send, recv):
        x, y, c = _me()
        return [pltpu.make_async_remote_copy(src_ref=in_refs[t], dst_ref=out_refs[t], send_sem=send.at[t],
                                             recv_sem=recv.at[t], device_id=(x, y, 1 - c), device_id_type=MESH)
                for t in range(n)]

    def start(*refs):
        for cp in copies(*refs):
            cp.start()

    def finish(*refs):
        for cp in copies(*refs):
            cp.wait()

    return _Exchange(halves, [jax.ShapeDtypeStruct(a.shape, a.dtype) for a in halves], n, start, finish)


N_DEV = 8


def _all_exchange(v):
    r, cols = v.shape

    def peers():
        x, y, c = _me()
        flip = lambda v, f: 1 - v if f else v
        return 4 * x + 2 * y + c, [(flip(x, fx), flip(y, fy), flip(c, fc)) for fx in (0, 1) for fy in (0, 1) for fc in (0, 1)][1:]

    def local(in_refs, out_refs, send, me):
        return pltpu.make_async_copy(in_refs[0], out_refs[0].at[me], send.at[7])

    def start(in_refs, out_refs, send, recv):
        me, others = peers()
        local(in_refs, out_refs, send, me).start()
        for j, peer in enumerate(others):
            pltpu.make_async_remote_copy(src_ref=in_refs[0], dst_ref=out_refs[0].at[me], send_sem=send.at[j],
                                         recv_sem=recv.at[j], device_id=peer, device_id_type=MESH).start()

    def finish(in_refs, out_refs, send, recv):
        me, others = peers()
        for j, (px, py, pc) in enumerate(others):
            pltpu.make_async_remote_copy(src_ref=in_refs[0], dst_ref=out_refs[0].at[4 * px + 2 * py + pc], send_sem=send.at[j],
                                         recv_sem=recv.at[j], device_id=(px, py, pc), device_id_type=MESH).wait()
        local(in_refs, out_refs, send, me).wait()

    return _Exchange([v], [jax.ShapeDtypeStruct((N_DEV, r, cols), v.dtype)], 8, start, finish)


def _sum_slots(a, name):
    n, r, cols = a.shape
    tr = _pick(r, 512) if r % 8 == 0 else r
    for cand in (512, 256, 128, 64, 32, 16, 8):
        if r % cand == 0:
            tr = cand
            break

    def body(a_ref, o_ref):
        acc = a_ref[0]
        for k in range(1, n):
            acc = acc + a_ref[k]
        o_ref[...] = acc

    return pl.pallas_call(
        body, name=name, grid=(r // tr,), in_specs=[pl.BlockSpec((n, tr, cols), lambda i: (0, i, 0))],
        out_specs=pl.BlockSpec((tr, cols), lambda i: (i, 0)), out_shape=jax.ShapeDtypeStruct((r, cols), F32),
        compiler_params=_params("parallel"))(a)


def _join(name, stacked):
    ax = SHARDED[name][1]
    return jnp.concatenate([stacked[k] for k in range(N_CHIPS)], axis=ax)


def _split(name, full):
    ax = SHARDED[name][1]
    return jnp.stack(jnp.split(full, N_CHIPS, axis=ax))


def _heads_pad(a, real, axis):
    shp = a.shape
    a = a.reshape(shp[:axis] + (MLA_HEADS, real) + shp[axis + 1:])
    pad = [(0, 0)] * a.ndim
    pad[axis + 1] = (0, LANES - real)
    a = jnp.pad(a, pad)
    return a.reshape(shp[:axis] + (HP,) + shp[axis + 1:])


def _heads_unpad(a, real, axis):
    shp = a.shape
    a = a.reshape(shp[:axis] + (MLA_HEADS, LANES) + shp[axis + 1:])
    a = lax.slice_in_dim(a, 0, real, axis=axis + 1)
    return a.reshape(shp[:axis] + (MLA_HEADS * real,) + shp[axis + 1:])


def _lane_place(a, start):
    n = a.shape[-1]
    pad = [(0, 0)] * (a.ndim - 1) + [(start, LANES - start - n)]
    return jnp.pad(a, pad)


_O_UV, _O_CQ, _O_CKV, _O_KR, _O_Z, _O_XBC, _O_DT, _O_G = 0, 1024, 1408, 1664, 1696, 2208, 3232, 3240


def _w_in_pad(w):
    sl = lambda a, b: w[:, a:b]
    xs = _heads_pad(sl(_O_XBC, _O_XBC + SSD_INNER), SSD_HEAD_DIM, 1)
    bc = sl(_O_XBC + SSD_INNER, _O_DT)
    main = jnp.concatenate([sl(_O_UV, _O_CQ), _heads_pad(sl(_O_Z, _O_XBC), SSD_HEAD_DIM, 1), xs, sl(_O_G, IN_COLS)], axis=1)
    tail = jnp.concatenate([bc, sl(_O_CKV, _O_KR), sl(_O_CQ, _O_CKV), _lane_place(sl(_O_KR, _O_Z), MLA_NOPE),
                            _lane_place(sl(_O_DT, _O_G), 0), jnp.zeros((w.shape[0], PW_TAIL - T_DT - LANES), w.dtype)], axis=1)
    return main, tail


def _w_in_unpad(gm, gt):
    m = lambda a, n: gm[:, a:a + n]
    t = lambda a, n: gt[:, a:a + n]
    parts = [m(C_UV, 1024), t(T_CQ, MLA_Q_RANK), t(T_CKV, MLA_KV_RANK), t(T_KR + MLA_NOPE, MLA_ROPE),
             _heads_unpad(m(C_Z, HP), SSD_HEAD_DIM, 1), _heads_unpad(m(C_XS, HP), SSD_HEAD_DIM, 1), t(T_BC, BCW),
             t(T_DT, SSD_HEADS), m(C_G, 3 * D_MODEL)]
    return jnp.concatenate(parts, axis=1)


def _xbc_pad(a):
    return jnp.concatenate([_heads_pad(a[..., :SSD_INNER], SSD_HEAD_DIM, a.ndim - 1), a[..., SSD_INNER:]], axis=-1)


def _xbc_unpad(a):
    return jnp.concatenate([_heads_unpad(a[..., :HP], SSD_HEAD_DIM, a.ndim - 1), a[..., HP:]], axis=-1)


def _rope_tables(positions):
    inv_freq = 1.0 / (ROPE_THETA ** (jnp.arange(0, MLA_ROPE, 2, dtype=F32) / MLA_ROPE))
    ang = positions.astype(F32)[:, None] * inv_freq
    cos, sin = jnp.cos(ang), jnp.sin(ang)
    s = positions.shape[0]
    half = MLA_ROPE // 2
    z = lambda n: jnp.zeros((s, n), F32)
    ct = jnp.concatenate([jnp.ones((s, MLA_NOPE), F32), cos, cos, z(LANES - MLA_QK)], axis=1)
    s1 = jnp.concatenate([z(MLA_NOPE), -sin, z(half), z(LANES - MLA_QK)], axis=1)
    s2 = jnp.concatenate([z(MLA_NOPE), z(half), sin, z(LANES - MLA_QK)], axis=1)
    return ct, s1, s2


def _layer_weights(full, small, l, part):
    w = {}
    row = lambda n: small[n][l][None, :]
    stacked = lambda g: g.reshape((N_CHIPS * g.shape[1], g.shape[2]))
    if part in ('ffn1', 'ffn2'):
        w[part + '_w_in'] = full[part + '_w_in']
        w[part + '_w_out'] = stacked(full[part + '_w_out'])
        w[part + '_norm'] = row(part + '_norm')
        return w
    w['w_out'] = stacked(full['w_out'])
    fl = {n: _join(n, full[n]) for n in ('w_in', 'mla_w_uq', 'mla_w_ukv', 'w_branch', 'ssd_conv_w')}
    w['w_in_main'], w['w_in_tail'] = _w_in_pad(fl['w_in'])
    w['wuq'] = _heads_pad(fl['mla_w_uq'], MLA_QK, 1)
    ukv = fl['mla_w_ukv'].reshape(MLA_KV_RANK, MLA_HEADS, MLA_NOPE + MLA_V)
    zero = jnp.zeros((MLA_KV_RANK, MLA_HEADS, LANES - MLA_NOPE), ukv.dtype)
    wk = jnp.concatenate([ukv[:, :, :MLA_NOPE], zero], axis=2).reshape(MLA_KV_RANK, HP)
    wv = jnp.concatenate([ukv[:, :, MLA_NOPE:], zero], axis=2).reshape(MLA_KV_RANK, HP)
    w['wkv'] = jnp.concatenate([wk, wv], axis=1)
    wb = fl['w_branch']
    w['wb0'] = wb[0]
    w['wb1'] = _heads_pad(wb[1], MLA_V, 0)
    w['wb2'] = _heads_pad(wb[2], SSD_HEAD_DIM, 0)
    w['conv_w'] = _xbc_pad(fl['ssd_conv_w'].astype(F32))
    for n in ('mix_norm', 'gm_v_norm', 'mla_q_norm', 'mla_kv_norm'):
        w[n] = row(n)
    w['gm_w_s'] = small['gm_w_s'][l]
    w['gm_b_full'] = jnp.broadcast_to(small['gm_b_s'][l][:, :, None], (GM_GROUPS, CHUNK, LANES))
    w['gq'] = _lane_place(row('mla_q_gain'), 0)
    w['gk'] = _lane_place(row('mla_k_gain'), 0)
    w['conv_b'] = _xbc_pad(row('ssd_conv_b'))
    w['dt_bias'] = _lane_place(row('ssd_dt_bias'), 0)
    w['a_log'] = _lane_place(row('ssd_a_log'), 0)
    w['d_vec'] = jnp.repeat(small['ssd_d'][l], LANES)[None, :]
    w['ssd_norm'] = _heads_pad(row('ssd_norm'), SSD_HEAD_DIM, 1)
    return w


_MIXER_SMALL = ['mla_w_uq', 'mla_w_ukv', 'ssd_conv_w', 'w_branch', 'w_out']
_MIXER_SMALL_G = [n for n in _MIXER_SMALL if n != 'ssd_conv_w']
GATHER_HOSTS = {'attn': ['ffn1_w_in', 'ffn2_w_in'], 'scan': ['ffn1_w_out', 'ffn2_w_out'], 'merge': _MIXER_SMALL, 'ffn2_in': ['w_in']}
GATHER_HOSTS_LATER = {'ffn1_in': ['ffn1_w_out'], 'proj': ['w_in'], 'attn': ['ffn1_w_in', 'ffn2_w_in'], 'scan': ['ffn2_w_out'],
                      'merge': _MIXER_SMALL}
FIRST_NOW = ['ffn1_w_in', 'ffn1_w_out']
FIRST_HOSTS = {'ffn1_in': ['w_in'], 'ffn1_out': _MIXER_SMALL, 'proj': ['ffn2_w_in', 'ffn2_w_out']}
PAIR_HOSTS = {'ffn2_dact': ['ffn1_w_in', 'ffn2_w_in'], 'ffn2_dwin': ['ffn1_w_out', 'w_in', 'ffn2_w_out'] + _MIXER_SMALL_G}
REDUCE_HOSTS = {'dattn_q': ['ffn1_w_out', 'w_in', 'ffn2_w_out'], 'dattn_kv': ['ffn1_w_in', 'ffn2_w_in'], 'dmla_pre': _MIXER_SMALL_G}
LAST_EARLY = ['w_in', 'ffn2_w_in', 'ffn2_w_out'] + _MIXER_SMALL_G
LAST_HOSTS = {'ffn1_dact': ['w_in'], 'ffn1_dwin': ['ffn2_w_in'], 'ffn1_dx': ['ffn2_w_out'] + _MIXER_SMALL_G}
LAST_LATE = ['ffn1_w_in', 'ffn1_w_out']


def _ffn_fwd(x, norm, w4, w_out, tag, sides=None):
    sides = sides or {}
    carried = {}
    (h, gate, up, act), carried[f"{tag}_in"] = _ffn_in(x, norm, w4, f"{tag}_in", sides.get(f"{tag}_in"))
    y, carried[f"{tag}_out"] = _ffn_out(act, w_out, x, f"{tag}_out", sides.get(f"{tag}_out"))
    return y, (x, h, gate, up, act), carried


def _ffn_bwd(dy, saved, norm, w4, w_out, tag, sides=None, after_dwout=None):
    sides = dict(sides or {})
    carried = {}
    x, h, gate, up, act = saved
    dw_out, carried[f"{tag}_dwout"] = _ffn_dwout(act, dy, f"{tag}_dwout", sides.get(f"{tag}_dwout"))
    if after_dwout is not None:
        sides.update(after_dwout(carried[f"{tag}_dwout"]))
    da, carried[f"{tag}_dact"] = _ffn_dact(dy, w_out, gate, up, f"{tag}_dact", sides.get(f"{tag}_dact"))
    dw_in, carried[f"{tag}_dwin"] = _ffn_dwin(h, da, f"{tag}_dwin", sides.get(f"{tag}_dwin"))
    (dx, dnorm), carried[f"{tag}_dx"] = _ffn_dx(da, w4, x, norm, dy, f"{tag}_dx", sides.get(f"{tag}_dx"))
    return dx, dnorm, dw_in, dw_out.reshape((N_CHIPS, 2 * FC // N_CHIPS, D_MODEL)), carried


def _mixer_fwd(x, w, tabs, tag, sides=None):
    sides = sides or {}
    carried = {}
    (h, pm, pt), carried['proj'] = _mixer_proj(x, w['mix_norm'], w['w_in_main'], w['w_in_tail'], f"{tag}_proj", sides.get('proj'))
    ya = _gmlp_fwd(pm, w['gm_v_norm'], w['gm_w_s'], w['gm_b_full'], f"{tag}_gmlp")
    q, k, v = _mla_pre_fwd(pt, tabs, w['mla_q_norm'], w['mla_kv_norm'], w['wuq'], w['wkv'], w['gq'], w['gk'], f"{tag}_mla_pre")
    (o, lse), carried['attn'] = _attn_fwd(q, k, v, f"{tag}_attn", sides.get('attn'))
    xs = _conv_fwd(pm, C_XS, HP, w['conv_w'][:, :HP], w['conv_b'][:, :HP], f"{tag}_conv_x")
    bc = _conv_fwd(pt, T_BC, BCW, w['conv_w'][:, HP:], w['conv_b'][:, HP:], f"{tag}_conv_bc")
    dtb, dab = _dt_fwd(pt, w['dt_bias'], w['a_log'], f"{tag}_dt")
    (ys, s_in), carried['scan'] = _scan_fwd(xs, bc, dtb, dab, f"{tag}_scan", sides.get('scan'))
    yc = _ssd_post_fwd(ys, xs, pm, w['d_vec'], w['ssd_norm'], f"{tag}_ssd_post")
    (mg, y), carried['merge'] = _merge_fwd(pm, ya, o, yc, w['wb0'], w['wb1'], w['wb2'], w['w_out'], x, f"{tag}_merge",
                                           sides.get('merge'))
    return y, (x, h, pm, pt, ya, q, k, v, o, lse, xs, bc, dtb, dab, ys, s_in, yc, mg), carried


def _pair_sums(pending, got):
    return {n: _pair_add(pending[n], got[n], f"pair_add_{n}") for n in got}


def _chip_sums(sums, arrived, layer, stacked):
    for n in arrived:
        stacked[n] = _chip_add(sums[n][1], arrived[n], f"chip_add_{n}", layer, stacked.get(n))


def _reduce_to_chip(pending, layer, stacked):
    names = list(pending)
    got = _run_exchange(_pair_exchange([pending[n] for n in names]), "pair_exchange")
    sums = _pair_sums(pending, dict(zip(names, got)))
    arrived = _run_exchange(_chip_exchange([sums[n][0] for n in names]), "chip_exchange")
    _chip_sums(sums, dict(zip(names, arrived)), layer, stacked)


def _mixer_bwd(dy, saved, w, tabs, tag, sides=None):
    sides = sides or {}
    carried = {}
    x, h, pm, pt, ya, q, k, v, o, lse, xs, bc, dtb, dab, ys, s_in, yc, mg = saved
    g = {}
    g['w_out'] = _matmul(mg, dy, ta=True, name=f"{tag}_dwout").reshape((N_CHIPS, D_MODEL // N_CHIPS, D_MODEL))
    d0, d1, d2, dgates, dya, do, dyc = _merge_bwd(pm, ya, o, yc, w['wb0'], w['wb1'], w['wb2'], w['w_out'], dy, f"{tag}_dmerge")
    dwb0 = _matmul(ya, d0, ta=True, name=f"{tag}_dwb0")
    dwb1 = _matmul(o, d1, ta=True, name=f"{tag}_dwb1")
    dwb2 = _matmul(yc, d2, ta=True, name=f"{tag}_dwb2")
    g['w_branch'] = _split('w_branch', jnp.stack([dwb0, _heads_unpad(dwb1, MLA_V, 0), _heads_unpad(dwb2, SSD_HEAD_DIM, 0)]))
    duv, g['gm_v_norm'], g['gm_w_s'], db = _gmlp_bwd(pm, w['gm_v_norm'], w['gm_w_s'], w['gm_b_full'], dya, f"{tag}_dgmlp")
    g['gm_b_s'] = db.T
    (dq, delta), carried['dattn_q'] = _attn_bwd_dq(q, k, v, o, lse, do, f"{tag}_dattn_q", sides.get('dattn_q'))
    (dk, dv), carried['dattn_kv'] = _attn_bwd_dkv(q, k, v, lse, delta, do, f"{tag}_dattn_kv", sides.get('dattn_kv'))
    (dcq, dckv, dkr, dwuq, dwkv, g['mla_q_norm'], g['mla_kv_norm'], dgq, dgk), carried['dmla_pre'] = _mla_pre_bwd(
        pt, tabs, w['mla_q_norm'], w['mla_kv_norm'], w['wuq'], w['wkv'], w['gq'], w['gk'], dq, dk, dv, f"{tag}_dmla_pre",
        sides.get('dmla_pre'))
    g['mla_w_uq'] = _split('mla_w_uq', _heads_unpad(dwuq, MLA_QK, 1))
    dwk = dwkv[:, :HP].reshape(MLA_KV_RANK, MLA_HEADS, LANES)[:, :, :MLA_NOPE]
    dwv = dwkv[:, HP:].reshape(MLA_KV_RANK, MLA_HEADS, LANES)[:, :, :MLA_V]
    g['mla_w_ukv'] = _split('mla_w_ukv', jnp.concatenate([dwk, dwv], axis=2).reshape(MLA_KV_RANK, MLA_HEADS * (MLA_NOPE + MLA_V)))
    g['mla_q_gain'], g['mla_k_gain'] = dgq[:, :MLA_QK], dgk[:, :MLA_QK]
    dys, dz, dssd_norm, dd = _ssd_post_bwd(ys, xs, pm, w['d_vec'], w['ssd_norm'], dyc, f"{tag}_dssd_post")
    g['ssd_norm'] = _heads_unpad(dssd_norm, SSD_HEAD_DIM, 1)
    g['ssd_d'] = jnp.sum(dd.reshape(SSD_HEADS, LANES), axis=1)[None, :]
    dxs, dbm, dcm, dda, ddtx = _scan_bwd(xs, bc, dtb, dab, s_in, dys, w['d_vec'], f"{tag}_dscan")
    dxs16, dcw_x, dcb_x = _conv_bwd(pm, C_XS, HP, w['conv_w'][:, :HP], w['conv_b'][:, :HP], dxs, f"{tag}_dconv_x")
    dbc16, dcw_bc, dcb_bc = _conv_bwd(pt, T_BC, BCW, w['conv_w'][:, HP:], w['conv_b'][:, HP:],
                                      jnp.concatenate([dbm, dcm], axis=1), f"{tag}_dconv_bc")
    g['ssd_conv_w'] = _xbc_unpad(jnp.concatenate([dcw_x, dcw_bc], axis=1))
    g['ssd_conv_b'] = _xbc_unpad(jnp.concatenate([dcb_x, dcb_bc], axis=1))
    ddt, dbias, dalog = _dt_bwd(pt, w['dt_bias'], w['a_log'], dda, ddtx, f"{tag}_ddt")
    g['ssd_dt_bias'], g['ssd_a_log'] = dbias[:, :SSD_HEADS], dalog[:, :SSD_HEADS]
    s = x.shape[0]
    dpm = jnp.concatenate([duv, dz, dxs16, dgates], axis=1)
    dpt = jnp.concatenate([dbc16, dckv, dcq, dkr, ddt, jnp.zeros((s, PW_TAIL - T_DT - LANES), BF16)], axis=1)
    g['w_in'] = _split('w_in', _w_in_unpad(_matmul(h, dpm, ta=True, name=f"{tag}_dwin_main"),
                                           _matmul(h, dpt, ta=True, name=f"{tag}_dwin_tail")))
    dx, g['mix_norm'] = _mixer_dx(dpm, dpt, w['w_in_main'], w['w_in_tail'], x, w['mix_norm'], dy, f"{tag}_dx")
    return dx, g, carried


_CONV_ROWS = 32


def _rows_cols(a, lead):
    return a.reshape(a.shape[:lead] + (int(np.prod(a.shape[lead:-1])), a.shape[-1]))


def _shard_views(wts):
    views = []
    for n in SHARDED_ORDER:
        a = _rows_cols(wts[n].astype(BF16), 1)
        if n == 'ssd_conv_w':
            a = jnp.pad(a, ((0, 0), (0, _CONV_ROWS - a.shape[1]), (0, 0)))
        views.append(a)
    return views


def _gathered(names, arrays):
    out = {}
    for n, a in zip(names, arrays):
        shp = _shard_shape(n)
        if n == 'ssd_conv_w':
            a = a[:, :shp[0]]
        out[n] = a.reshape((N_CHIPS,) + shp)
    return out


def _local_step(x, positions, target, weights, small, distributed=True):
    tabs = _rope_tables(positions)
    views = dict(zip(SHARDED_ORDER, weights)) if distributed else None
    plan = [{} for _ in range(DEPTH)]
    if distributed:
        for l in range(DEPTH - 1):
            plan[l].update({host: (names, l + 1) for host, names in (GATHER_HOSTS if l == 0 else GATHER_HOSTS_LATER).items()})
        plan[0].update({host: (names, 0) for host, names in FIRST_HOSTS.items()})
        have = [dict() for _ in range(DEPTH)]
        have[0].update(_gathered(FIRST_NOW, _run_exchange(_gather_exchange([views[n] for n in FIRST_NOW], 0), "gather_first")))
    else:
        have = weights

    def absorb(l, carried):
        for host, arrays in carried.items():
            if host in plan[l]:
                names, layer = plan[l][host]
                have[layer].update(_gathered(names, arrays))

    ws, saved = [], []
    for l in range(DEPTH):
        sides = {host: _gather_exchange([views[n] for n in names], layer) for host, (names, layer) in plan[l].items()}
        w = _layer_weights(have[l], small, l, 'ffn1')
        x, s1, carried = _ffn_fwd(x, w['ffn1_norm'], w['ffn1_w_in'], w['ffn1_w_out'], "ffn1", sides)
        absorb(l, carried)
        w.update(_layer_weights(have[l], small, l, 'mixer'))
        x, s2, carried = _mixer_fwd(x, w, tabs, "mix", sides)
        absorb(l, carried)
        w.update(_layer_weights(have[l], small, l, 'ffn2'))
        x, s3, carried = _ffn_fwd(x, w['ffn2_norm'], w['ffn2_w_in'], w['ffn2_w_out'], "ffn2", sides)
        absorb(l, carried)
        ws.append(w)
        saved.append((s1, s2, s3))
    dy, sq = _loss_head(x, target, "loss_head")
    loss = 0.5 * jnp.sum(sq) / D_MODEL
    grads, reduced, pending = [None] * DEPTH, {}, None

    def chip_sides(sums, hosts):
        return {host: _chip_exchange([sums[n][0] for n in names]) for host, names in hosts.items()}

    def arrivals(carried, hosts):
        return {n: a for host, names in hosts.items() for n, a in zip(names, carried[host])}

    for l in reversed(range(DEPTH)):
        w = ws[l]
        s1, s2, s3 = saved[l]
        sides = {host: _pair_exchange([pending[n] for n in names]) for host, names in PAIR_HOSTS.items()} if pending else {}
        dy, dn2, dwi2, dwo2, carried = _ffn_bwd(dy, s3, w['ffn2_norm'], w['ffn2_w_in'], w['ffn2_w_out'], "ffn2", sides)
        sides = {}
        if pending:
            sums = _pair_sums(pending, arrivals(carried, PAIR_HOSTS))
            sides = chip_sides(sums, REDUCE_HOSTS)
        dy, g, carried = _mixer_bwd(dy, s2, w, tabs, "mix", sides)
        if pending:
            _chip_sums(sums, arrivals(carried, REDUCE_HOSTS), l + 1, reduced)
        g.update(ffn2_norm=dn2, ffn2_w_in=dwi2, ffn2_w_out=dwo2)
        last = distributed and l == 0
        if last:
            early = {n: _rows_cols(g[n], 1) for n in LAST_EARLY}
            after = {}

            def after_dwout(got):
                after['sums'] = _pair_sums(early, dict(zip(LAST_EARLY, got)))
                return chip_sides(after['sums'], LAST_HOSTS)

            dy, dn1, dwi1, dwo1, carried = _ffn_bwd(dy, s1, w['ffn1_norm'], w['ffn1_w_in'], w['ffn1_w_out'], "ffn1",
                                                    {'ffn1_dwout': _pair_exchange([early[n] for n in LAST_EARLY])}, after_dwout)
            _chip_sums(after['sums'], arrivals(carried, LAST_HOSTS), 0, reduced)
        else:
            dy, dn1, dwi1, dwo1, _ = _ffn_bwd(dy, s1, w['ffn1_norm'], w['ffn1_w_in'], w['ffn1_w_out'], "ffn1")
        g.update(ffn1_norm=dn1, ffn1_w_in=dwi1, ffn1_w_out=dwo1)
        grads[l] = g
        if distributed:
            pending = {n: _rows_cols(g[n], 1) for n in REDUCED}
    if distributed:
        _reduce_to_chip({n: pending[n] for n in LAST_LATE}, 0, reduced)
    return loss, dy, grads, reduced


SMALL_PACK = SMALL_ORDER + ['ssd_conv_w']


def _pack_small(per_layer_rows, tail=None):
    parts = [per_layer_rows[l][n].reshape(-1).astype(F32) for l in range(DEPTH) for n in SMALL_PACK]
    if tail is not None:
        parts.append(tail.reshape(1))
    flat = jnp.concatenate(parts)
    rows = -(-flat.shape[0] // LANES)
    rows = -(-rows // 8) * 8
    return jnp.pad(flat, (0, rows * LANES - flat.shape[0])).reshape(rows, LANES)


def _unpack_small(buf, shapes):
    flat = buf.reshape(-1)
    off = 0
    out = {n: [] for n in SMALL_PACK}
    for l in range(DEPTH):
        for n in SMALL_PACK:
            size = int(np.prod(shapes[n]))
            out[n].append(flat[off:off + size].reshape(shapes[n]))
            off += size
    return {n: jnp.stack(v) for n, v in out.items()}


def kernel(x, positions, ffn1_norm, ffn1_w_in, ffn1_w_out, mix_norm, w_in, gm_v_norm, gm_w_s, gm_b_s, mla_q_norm, mla_kv_norm, mla_w_uq, mla_w_ukv, mla_q_gain, mla_k_gain, ssd_conv_w, ssd_conv_b, ssd_dt_bias, ssd_a_log, ssd_d, ssd_norm, w_branch, w_out, ffn2_norm, ffn2_w_in, ffn2_w_out, loss_target, m_ffn1_norm, m_ffn1_w_in, m_ffn1_w_out, m_mix_norm, m_w_in, m_gm_v_norm, m_gm_w_s, m_gm_b_s, m_mla_q_norm, m_mla_kv_norm, m_mla_w_uq, m_mla_w_ukv, m_mla_q_gain, m_mla_k_gain, m_ssd_conv_w, m_ssd_conv_b, m_ssd_dt_bias, m_ssd_a_log, m_ssd_d, m_ssd_norm, m_w_branch, m_w_out, m_ffn2_norm, m_ffn2_w_in, m_ffn2_w_out, v_ffn1_norm, v_ffn1_w_in, v_ffn1_w_out, v_mix_norm, v_w_in, v_gm_v_norm, v_gm_w_s, v_gm_b_s, v_mla_q_norm, v_mla_kv_norm, v_mla_w_uq, v_mla_w_ukv, v_mla_q_gain, v_mla_k_gain, v_ssd_conv_w, v_ssd_conv_b, v_ssd_dt_bias, v_ssd_a_log, v_ssd_d, v_ssd_norm, v_w_branch, v_w_out, v_ffn2_norm, v_ffn2_w_in, v_ffn2_w_out):
    wts = dict(zip(WEIGHTS, (ffn1_norm, ffn1_w_in, ffn1_w_out, mix_norm, w_in, gm_v_norm, gm_w_s, gm_b_s, mla_q_norm, mla_kv_norm,
                             mla_w_uq, mla_w_ukv, mla_q_gain, mla_k_gain, ssd_conv_w, ssd_conv_b, ssd_dt_bias, ssd_a_log, ssd_d,
                             ssd_norm, w_branch, w_out, ffn2_norm, ffn2_w_in, ffn2_w_out)))
    mom = dict(zip(WEIGHTS, (m_ffn1_norm, m_ffn1_w_in, m_ffn1_w_out, m_mix_norm, m_w_in, m_gm_v_norm, m_gm_w_s, m_gm_b_s, m_mla_q_norm,
                             m_mla_kv_norm, m_mla_w_uq, m_mla_w_ukv, m_mla_q_gain, m_mla_k_gain, m_ssd_conv_w, m_ssd_conv_b,
                             m_ssd_dt_bias, m_ssd_a_log, m_ssd_d, m_ssd_norm, m_w_branch, m_w_out, m_ffn2_norm, m_ffn2_w_in,
                             m_ffn2_w_out)))
    var = dict(zip(WEIGHTS, (v_ffn1_norm, v_ffn1_w_in, v_ffn1_w_out, v_mix_norm, v_w_in, v_gm_v_norm, v_gm_w_s, v_gm_b_s, v_mla_q_norm,
                             v_mla_kv_norm, v_mla_w_uq, v_mla_w_ukv, v_mla_q_gain, v_mla_k_gain, v_ssd_conv_w, v_ssd_conv_b,
                             v_ssd_dt_bias, v_ssd_a_log, v_ssd_d, v_ssd_norm, v_w_branch, v_w_out, v_ffn2_norm, v_ffn2_w_in,
                             v_ffn2_w_out)))
    cx, cy, _ = _me()
    mychip = _chip_index(cx, cy)

    small = {n: wts[n] for n in SMALL_ORDER}
    loss_part, dx, grads, reduced = _local_step(x[0], positions[0], loss_target[0], _shard_views(wts), small)
    rows_cols = _rows_cols
    halves = [reduced[n] for n in REDUCED]
    theirs = _run_exchange(_pair_share(halves), "pair_share")
    grad, delta, new_m, new_v = {}, {}, {}, {}
    everyone = _all_exchange(_pack_small(grads, tail=loss_part))
    for n, a, b in zip(REDUCED, halves, theirs):
        shp = wts[n].shape
        outs, carried = _adamw_sharded(rows_cols(wts[n], 1), rows_cols(mom[n], 1), rows_cols(var[n], 1), a, b, f"adamw_{n}",
                                       everyone if n == REDUCED[0] else None)
        if n == REDUCED[0]:
            partials = carried[0]
        grad[n], delta[n], new_m[n], new_v[n] = [o.reshape(shp) for o in outs]
    shapes = {n: wts[n].shape[1:] for n in SMALL_ORDER}
    shapes['ssd_conv_w'] = SHARDED['ssd_conv_w'][0]
    summed = _sum_slots(partials, "small_sum")
    small_g = _unpack_small(summed, shapes)
    loss = summed.reshape(-1)[DEPTH * sum(int(np.prod(shapes[n])) for n in SMALL_PACK)]
    conv_full = small_g.pop('ssd_conv_w')
    shard_cols = _shard_shape('ssd_conv_w')[1]
    small_g['ssd_conv_w'] = lax.dynamic_slice_in_dim(conv_full, mychip * shard_cols, shard_cols, axis=2)
    shapes['ssd_conv_w'] = _shard_shape('ssd_conv_w')

    per_layer = lambda t: [{n: t[n][l] for n in SMALL_PACK} for l in range(DEPTH)]
    d, nm, nv = _adamw(_pack_small(per_layer(wts)), _pack_small(per_layer(small_g)), _pack_small(per_layer(mom)),
                       _pack_small(per_layer(var)), "adamw_small")
    sd, snm, snv = _unpack_small(d, shapes), _unpack_small(nm, shapes), _unpack_small(nv, shapes)
    for n in SMALL_PACK:
        grad[n], delta[n], new_m[n], new_v[n] = small_g[n], sd[n], snm[n], snv[n]
    return (loss, dx[None], *[grad[n] for n in WEIGHTS], *[delta[n] for n in WEIGHTS], *[new_m[n] for n in WEIGHTS],
            *[new_v[n] for n in WEIGHTS])
```

```python
import functools

import numpy as np
import jax
import jax.numpy as jnp
from jax import lax
from jax.experimental import pallas as pl
from jax.experimental.pallas import tpu as pltpu

F32, BF16 = jnp.float32, jnp.bfloat16
MESH = pl.DeviceIdType.MESH

D_MODEL, DEPTH, D_FF, EPS = 1024, 4, 2816, 1e-6
GM_WIDTH, GM_GROUPS, CHUNK = 512, 4, 128
MLA_HEADS, MLA_Q_RANK, MLA_KV_RANK, MLA_NOPE, MLA_ROPE, MLA_V = 8, 384, 256, 64, 32, 64
MLA_QK = MLA_NOPE + MLA_ROPE
ROPE_THETA = 10000.0
SSD_HEADS, SSD_HEAD_DIM, SSD_GROUPS, SSD_STATE, SSD_CONV = 8, 64, 2, 128, 4
SSD_INNER = SSD_HEADS * SSD_HEAD_DIM
IN_COLS = 6312
LANES = 128
ADAM_LR, ADAM_B1, ADAM_B2, ADAM_EPS, ADAM_WD, ADAM_STEP = 0.001, 0.9, 0.999, 1e-08, 0.01, 10

C_UV, C_Z, C_XS, C_G, PW_MAIN = 0, 1024, 2048, 3072, 6144
T_BC, T_CKV, T_CQ, T_KR, T_DT, PW_TAIL = 0, 512, 768, 1152, 1280, 1536
HP = MLA_HEADS * LANES
FC = 2 * D_FF // 4

WEIGHTS = ['ffn1_norm', 'ffn1_w_in', 'ffn1_w_out', 'mix_norm', 'w_in', 'gm_v_norm', 'gm_w_s', 'gm_b_s', 'mla_q_norm',
           'mla_kv_norm', 'mla_w_uq', 'mla_w_ukv', 'mla_q_gain', 'mla_k_gain', 'ssd_conv_w', 'ssd_conv_b', 'ssd_dt_bias',
           'ssd_a_log', 'ssd_d', 'ssd_norm', 'w_branch', 'w_out', 'ffn2_norm', 'ffn2_w_in', 'ffn2_w_out']
SHARDED = {'ffn1_w_in': ((1024, 5632), 1), 'ffn1_w_out': ((2816, 1024), 0), 'w_in': ((1024, 6312), 1),
           'mla_w_uq': ((384, 768), 1), 'mla_w_ukv': ((256, 1024), 1), 'ssd_conv_w': ((4, 1024), 1),
           'w_branch': ((3, 512, 1024), 2), 'w_out': ((1024, 1024), 0), 'ffn2_w_in': ((1024, 5632), 1),
           'ffn2_w_out': ((2816, 1024), 0)}
SHARDED_ORDER = [n for n in WEIGHTS if n in SHARDED]
SMALL_ORDER = [n for n in WEIGHTS if n not in SHARDED]
REDUCED = [n for n in SHARDED_ORDER if n != 'ssd_conv_w']
N_CHIPS = 4


def _shard_shape(name):
    shape, ax = SHARDED[name]
    return tuple(d // N_CHIPS if i == ax else d for i, d in enumerate(shape))


def _pick(dim, target):
    if dim <= target:
        return dim
    t = (target // LANES) * LANES
    while t >= LANES:
        if dim % t == 0:
            return t
        t -= LANES
    return dim


def _sigmoid(x):
    return 1.0 / (1.0 + jnp.exp(-x))


def _params(*sem):
    return pltpu.CompilerParams(dimension_semantics=sem, vmem_limit_bytes=56 * 1024 * 1024)


def _matmul(a, b, *, ta=False, tb=False, out_dtype=F32, scale=1.0, res=None, name, side=None):
    if ta:
        k_dim, m_dim = a.shape
    else:
        m_dim, k_dim = a.shape
    if tb:
        n_dim, k2 = b.shape
    else:
        k2, n_dim = b.shape
    assert k_dim == k2, (a.shape, b.shape, ta, tb)
    tm, tn, tk = _pick(m_dim, 1024), _pick(n_dim, 1024), _pick(k_dim, 1024)
    nk = k_dim // tk
    dn = (((0 if ta else 1,), (1 if tb else 0,)), ((), ()))

    def body(*refs):
        if res is not None:
            a_ref, b_ref, r_ref, o_ref, acc = refs
        else:
            a_ref, b_ref, o_ref, acc = refs
        k = pl.program_id(2)

        @pl.when(k == 0)
        def _():
            acc[...] = jnp.zeros_like(acc)

        acc[...] += lax.dot_general(a_ref[...].astype(BF16), b_ref[...].astype(BF16), dn, preferred_element_type=F32)

        @pl.when(k == nk - 1)
        def _():
            r = acc[...]
            if scale != 1.0:
                r = r * scale
            if res is not None:
                r = r + r_ref[...]
            o_ref[...] = r.astype(out_dtype)

    a_spec = pl.BlockSpec((tk, tm), lambda j, i, k: (k, i)) if ta else pl.BlockSpec((tm, tk), lambda j, i, k: (i, k))
    b_spec = pl.BlockSpec((tn, tk), lambda j, i, k: (j, k)) if tb else pl.BlockSpec((tk, tn), lambda j, i, k: (k, j))
    in_specs = [a_spec, b_spec]
    args = [a, b]
    if res is not None:
        in_specs.append(pl.BlockSpec((tm, tn), lambda j, i, k: (i, j)))
        args.append(res)
    (out,), carried = _call(
        body, name=name, grid=(n_dim // tn, m_dim // tm, nk), in_specs=in_specs,
        out_specs=[pl.BlockSpec((tm, tn), lambda j, i, k: (i, j))],
        out_shape=[jax.ShapeDtypeStruct((m_dim, n_dim), out_dtype)],
        scratch_shapes=[pltpu.VMEM((tm, tn), F32)], args=args, semantics=("parallel", "parallel", "arbitrary"), side=side)
    return out if side is None else (out, carried)


_NT = (((1,), (1,)), ((), ()))
_TN = (((0,), (0,)), ((), ()))


def _resident(shape):
    return pl.BlockSpec(shape, lambda *_: tuple(0 for _ in shape), pipeline_mode=pl.Buffered(1))


def _ffn_in(x, gain, w4, name, side=None):
    s, d = x.shape
    tm = _pick(s, 512)

    def body(x_ref, g_ref, w_ref, h_ref, gate_ref, up_ref, act_ref):
        xv = x_ref[...]
        r = lax.rsqrt(jnp.mean(xv * xv, axis=-1, keepdims=True) + EPS)
        h = (xv * r * g_ref[...]).astype(BF16)
        h_ref[...] = h
        for j in range(2):
            g16 = jnp.dot(h, w_ref[j], preferred_element_type=F32).astype(BF16)
            u16 = jnp.dot(h, w_ref[j + 2], preferred_element_type=F32).astype(BF16)
            gate_ref[j] = g16
            up_ref[j] = u16
            gf, uf = g16.astype(F32), u16.astype(F32)
            act_ref[j] = (gf * _sigmoid(gf) * uf).astype(BF16)

    half = pl.BlockSpec((2, tm, FC), lambda i: (0, i, 0))
    return _call(
        body, name=name, grid=(s // tm,),
        in_specs=[pl.BlockSpec((tm, d), lambda i: (i, 0)), pl.BlockSpec((1, d), lambda i: (0, 0)), _resident((4, d, FC))],
        out_specs=[pl.BlockSpec((tm, d), lambda i: (i, 0)), half, half, half],
        out_shape=[jax.ShapeDtypeStruct((s, d), BF16)] + [jax.ShapeDtypeStruct((2, s, FC), BF16)] * 3,
        scratch_shapes=[], args=(x, gain, w4), semantics=("parallel",), side=side)


def _ffn_out(act, w_out, x, name, side=None):
    s, d = x.shape
    tm = _pick(s, 512)

    def body(a_ref, w_ref, x_ref, o_ref):
        acc = jnp.dot(a_ref[0], w_ref[0:FC, :], preferred_element_type=F32)
        acc = acc + jnp.dot(a_ref[1], w_ref[FC:2 * FC, :], preferred_element_type=F32)
        o_ref[...] = x_ref[...] + 0.5 * acc

    row = pl.BlockSpec((tm, d), lambda i: (i, 0))
    (out,), carried = _call(
        body, name=name, grid=(s // tm,),
        in_specs=[pl.BlockSpec((2, tm, FC), lambda i: (0, i, 0)), _resident((2 * FC, d)), row], out_specs=[row],
        out_shape=[jax.ShapeDtypeStruct((s, d), F32)], scratch_shapes=[], args=(act, w_out, x), semantics=("parallel",),
        side=side)
    return out, carried


def _ffn_dact(dy, w_out, gate, up, name, side=None):
    s, d = dy.shape
    tm = _pick(s, 512)

    def body(dy_ref, w_ref, g_ref, u_ref, o_ref):
        dy16 = dy_ref[...].astype(BF16)
        for j in range(2):
            dact = 0.5 * lax.dot_general(dy16, w_ref[j * FC:(j + 1) * FC, :], _NT, preferred_element_type=F32)
            g, u = g_ref[j].astype(F32), u_ref[j].astype(F32)
            sg = _sigmoid(g)
            o_ref[j] = (dact * u * (sg * (1.0 + g * (1.0 - sg)))).astype(BF16)
            o_ref[j + 2] = (dact * g * sg).astype(BF16)

    half = pl.BlockSpec((2, tm, FC), lambda i: (0, i, 0))
    (out,), carried = _call(
        body, name=name, grid=(s // tm,),
        in_specs=[pl.BlockSpec((tm, d), lambda i: (i, 0)), _resident((2 * FC, d)), half, half],
        out_specs=[pl.BlockSpec((4, tm, FC), lambda i: (0, i, 0))],
        out_shape=[jax.ShapeDtypeStruct((4, s, FC), BF16)], scratch_shapes=[], args=(dy, w_out, gate, up),
        semantics=("parallel",), side=side)
    return out, carried


def _ffn_dwout(act, dy, name, side=None):
    s, d = dy.shape
    tk = _pick(s, 1024)
    nk = s // tk

    def body(a_ref, dy_ref, o_ref):
        k = pl.program_id(1)

        @pl.when(k == 0)
        def _():
            o_ref[...] = jnp.zeros_like(o_ref)

        o_ref[...] += lax.dot_general(a_ref[...], dy_ref[...].astype(BF16), _TN, preferred_element_type=F32)

        @pl.when(k == nk - 1)
        def _():
            o_ref[...] = 0.5 * o_ref[...]

    (out,), carried = _call(
        body, name=name, grid=(2, nk),
        in_specs=[pl.BlockSpec((None, tk, FC), lambda j, k: (j, k, 0)), pl.BlockSpec((tk, d), lambda j, k: (k, 0))],
        out_specs=[pl.BlockSpec((FC, d), lambda j, k: (j, 0))], out_shape=[jax.ShapeDtypeStruct((2 * FC, d), F32)],
        scratch_shapes=[], args=(act, dy), semantics=("parallel", "arbitrary"), side=side)
    return out, carried


def _ffn_dwin(h, da, name, side=None):
    s, d = h.shape
    tk = _pick(s, 1024)

    def body(h_ref, da_ref, o_ref):
        @pl.when(pl.program_id(1) == 0)
        def _():
            o_ref[...] = jnp.zeros_like(o_ref)

        o_ref[...] += lax.dot_general(h_ref[...], da_ref[...], _TN, preferred_element_type=F32)

    (out,), carried = _call(
        body, name=name, grid=(4, s // tk),
        in_specs=[pl.BlockSpec((tk, d), lambda j, k: (k, 0)), pl.BlockSpec((None, tk, FC), lambda j, k: (j, k, 0))],
        out_specs=[pl.BlockSpec((None, d, FC), lambda j, k: (j, 0, 0))], out_shape=[jax.ShapeDtypeStruct((4, d, FC), F32)],
        scratch_shapes=[], args=(h, da), semantics=("parallel", "arbitrary"), side=side)
    return out, carried


def _ffn_dx(da, w4, x, gain, dy, name, side=None):
    s, d = x.shape
    tm = _pick(s, 512)

    def body(da_ref, w_ref, x_ref, g_ref, dy_ref, dx_ref, dg_ref):
        @pl.when(pl.program_id(0) == 0)
        def _():
            dg_ref[...] = jnp.zeros_like(dg_ref)

        dh = jnp.zeros((tm, d), F32)
        for j in range(4):
            dh = dh + lax.dot_general(da_ref[j], w_ref[j], _NT, preferred_element_type=F32)
        xv = x_ref[...]
        r = lax.rsqrt(jnp.mean(xv * xv, axis=-1, keepdims=True) + EPS)
        u = dh * g_ref[...]
        dx_ref[...] = dy_ref[...] + r * u - xv * (r * r * r) * jnp.mean(xv * u, axis=-1, keepdims=True)
        dg_ref[...] += jnp.sum(dh * xv * r, axis=0, keepdims=True)

    row = pl.BlockSpec((tm, d), lambda i: (i, 0))
    vec = pl.BlockSpec((1, d), lambda i: (0, 0))
    return _call(
        body, name=name, grid=(s // tm,),
        in_specs=[pl.BlockSpec((4, tm, FC), lambda i: (0, i, 0)), _resident((4, d, FC)), row, vec, row],
        out_specs=[row, vec], out_shape=[jax.ShapeDtypeStruct((s, d), F32), jax.ShapeDtypeStruct((1, d), F32)],
        scratch_shapes=[], args=(da, w4, x, gain, dy), semantics=("arbitrary",), side=side)


def _mixer_proj(x, gain, w_main, w_tail, name, side=None):
    s, d = x.shape
    tm = _pick(s, 512)

    def body(x_ref, g_ref, wm_ref, wt_ref, h_ref, pm_ref, pt_ref):
        xv = x_ref[...]
        r = lax.rsqrt(jnp.mean(xv * xv, axis=-1, keepdims=True) + EPS)
        h = (xv * r * g_ref[...]).astype(BF16)
        h_ref[...] = h
        pm_ref[...] = jnp.dot(h, wm_ref[...], preferred_element_type=F32).astype(BF16)
        pt_ref[...] = jnp.dot(h, wt_ref[...], preferred_element_type=F32)

    row = lambda width: pl.BlockSpec((tm, width), lambda i: (i, 0))
    return _call(
        body, name=name, grid=(s // tm,),
        in_specs=[row(d), pl.BlockSpec((1, d), lambda i: (0, 0)), _resident((d, PW_MAIN)), _resident((d, PW_TAIL))],
        out_specs=[row(d), row(PW_MAIN), row(PW_TAIL)],
        out_shape=[jax.ShapeDtypeStruct((s, d), BF16), jax.ShapeDtypeStruct((s, PW_MAIN), BF16),
                   jax.ShapeDtypeStruct((s, PW_TAIL), F32)],
        scratch_shapes=[], args=(x, gain, w_main, w_tail), semantics=("parallel",), side=side)


def _mixer_dx(dpm, dpt, w_main, w_tail, x, gain, dy, name):
    s, d = x.shape
    tm = _pick(s, 512)

    def body(dpm_ref, dpt_ref, wm_ref, wt_ref, x_ref, g_ref, dy_ref, dx_ref, dg_ref):
        @pl.when(pl.program_id(0) == 0)
        def _():
            dg_ref[...] = jnp.zeros_like(dg_ref)

        dh = lax.dot_general(dpm_ref[...], wm_ref[...], _NT, preferred_element_type=F32)
        dh = dh + lax.dot_general(dpt_ref[...], wt_ref[...], _NT, preferred_element_type=F32)
        xv = x_ref[...]
        r = lax.rsqrt(jnp.mean(xv * xv, axis=-1, keepdims=True) + EPS)
        u = dh * g_ref[...]
        dx_ref[...] = dy_ref[...] + r * u - xv * (r * r * r) * jnp.mean(xv * u, axis=-1, keepdims=True)
        dg_ref[...] += jnp.sum(dh * xv * r, axis=0, keepdims=True)

    row = lambda width: pl.BlockSpec((tm, width), lambda i: (i, 0))
    vec = pl.BlockSpec((1, d), lambda i: (0, 0))
    return pl.pallas_call(
        body, name=name, grid=(s // tm,),
        in_specs=[row(PW_MAIN), row(PW_TAIL), _resident((d, PW_MAIN)), _resident((d, PW_TAIL)), row(d), vec, row(d)],
        out_specs=[row(d), vec], out_shape=[jax.ShapeDtypeStruct((s, d), F32), jax.ShapeDtypeStruct((1, d), F32)],
        compiler_params=_params("arbitrary"))(dpm, dpt, w_main, w_tail, x, gain, dy)


_INV_SQRT2 = 0.7071067811865476
_INV_SQRT2PI = 0.3989422804014327


def _gelu(x):
    return 0.5 * x * (1.0 + lax.erf(x * _INV_SQRT2))


def _gelu_grad(x):
    return 0.5 * (1.0 + lax.erf(x * _INV_SQRT2)) + x * jnp.exp(-0.5 * x * x) * _INV_SQRT2PI


def _tril_mask():
    r = lax.broadcasted_iota(jnp.int32, (CHUNK, CHUNK), 0)
    c = lax.broadcasted_iota(jnp.int32, (CHUNK, CHUNK), 1)
    return r >= c


def _gmlp_fwd(p, v_gain, w_s, b_full, name):
    s = p.shape[0]
    tm = _pick(s, 512)
    nch = tm // CHUNK

    def body(uv_ref, g_ref, w_ref, b_ref, o_ref):
        gel = _gelu(uv_ref[...].astype(F32))
        u, v = gel[:, :GM_WIDTH], gel[:, GM_WIDTH:]
        r = lax.rsqrt(jnp.mean(v * v, axis=-1, keepdims=True) + EPS)
        vn = (v * r * g_ref[...]).astype(BF16)
        mask = _tril_mask()
        for g in range(GM_GROUPS):
            wm = jnp.where(mask, w_ref[g], 0.0).astype(BF16)
            for c in range(nch):
                rs, cs = slice(c * CHUNK, (c + 1) * CHUNK), slice(g * LANES, (g + 1) * LANES)
                sp = jnp.dot(wm, vn[rs, cs], preferred_element_type=F32) + b_ref[g]
                o_ref[rs, cs] = (u[rs, cs] * sp).astype(BF16)

    full3 = pl.BlockSpec((GM_GROUPS, CHUNK, CHUNK), lambda i: (0, 0, 0))
    return pl.pallas_call(
        body, name=name, grid=(s // tm,),
        in_specs=[pl.BlockSpec((tm, 2 * GM_WIDTH), lambda i: (i, C_UV // (2 * GM_WIDTH))),
                  pl.BlockSpec((1, GM_WIDTH), lambda i: (0, 0)), full3, full3],
        out_specs=pl.BlockSpec((tm, GM_WIDTH), lambda i: (i, 0)),
        out_shape=jax.ShapeDtypeStruct((s, GM_WIDTH), BF16), compiler_params=_params("parallel"))(p, v_gain, w_s, b_full)


def _gmlp_bwd(p, v_gain, w_s, b_full, dy, name):
    s = p.shape[0]
    tm = _pick(s, 512)
    nch = tm // CHUNK
    nsteps = s // tm

    def body(uv_ref, g_ref, w_ref, b_ref, dy_ref, duv_ref, dg_ref, dw_ref, db_ref, dvn_s, dbacc):
        step = pl.program_id(0)

        @pl.when(step == 0)
        def _():
            dg_ref[...] = jnp.zeros_like(dg_ref)
            dw_ref[...] = jnp.zeros_like(dw_ref)
            dbacc[...] = jnp.zeros_like(dbacc)

        uv = uv_ref[...].astype(F32)
        gel = _gelu(uv)
        u, v = gel[:, :GM_WIDTH], gel[:, GM_WIDTH:]
        r = lax.rsqrt(jnp.mean(v * v, axis=-1, keepdims=True) + EPS)
        gain = g_ref[...]
        vn32 = v * r * gain
        vn = vn32.astype(BF16)
        dy = dy_ref[...].astype(F32)
        mask = _tril_mask()
        for g in range(GM_GROUPS):
            wm = jnp.where(mask, w_ref[g], 0.0).astype(BF16)
            dwg = jnp.zeros((CHUNK, CHUNK), F32)
            dbg = jnp.zeros((CHUNK, LANES), F32)
            for c in range(nch):
                rs, cs = slice(c * CHUNK, (c + 1) * CHUNK), slice(g * LANES, (g + 1) * LANES)
                sp = jnp.dot(wm, vn[rs, cs], preferred_element_type=F32) + b_ref[g]
                dyc = dy[rs, cs]
                dsp = dyc * u[rs, cs]
                dsp16 = dsp.astype(BF16)
                duv_ref[rs, cs] = (dyc * sp * _gelu_grad(uv[rs, cs])).astype(BF16)
                dvn_s[rs, cs] = lax.dot_general(wm, dsp16, (((0,), (0,)), ((), ())), preferred_element_type=F32)
                dwg = dwg + lax.dot_general(dsp16, vn[rs, cs], (((1,), (1,)), ((), ())), preferred_element_type=F32)
                dbg = dbg + dsp
            dw_ref[g] += jnp.where(mask, dwg, 0.0)
            dbacc[:, g * LANES:(g + 1) * LANES] += dbg
        dvn = dvn_s[...]
        uu = dvn * gain
        dv = r * uu - v * (r * r * r) * jnp.mean(v * uu, axis=-1, keepdims=True)
        duv_ref[:, GM_WIDTH:] = (dv * _gelu_grad(uv[:, GM_WIDTH:])).astype(BF16)
        dg_ref[...] += jnp.sum(dvn * v * r, axis=0, keepdims=True)

        @pl.when(step == nsteps - 1)
        def _():
            for g in range(GM_GROUPS):
                db_ref[:, g:g + 1] = jnp.sum(dbacc[:, g * LANES:(g + 1) * LANES], axis=1, keepdims=True)

    full3 = pl.BlockSpec((GM_GROUPS, CHUNK, CHUNK), lambda i: (0, 0, 0))
    return pl.pallas_call(
        body, name=name, grid=(nsteps,),
        in_specs=[pl.BlockSpec((tm, 2 * GM_WIDTH), lambda i: (i, C_UV // (2 * GM_WIDTH))),
                  pl.BlockSpec((1, GM_WIDTH), lambda i: (0, 0)), full3, full3,
                  pl.BlockSpec((tm, GM_WIDTH), lambda i: (i, 0))],
        out_specs=[pl.BlockSpec((tm, 2 * GM_WIDTH), lambda i: (i, 0)), pl.BlockSpec((1, GM_WIDTH), lambda i: (0, 0)),
                   full3, pl.BlockSpec((CHUNK, GM_GROUPS), lambda i: (0, 0))],
        out_shape=[jax.ShapeDtypeStruct((s, 2 * GM_WIDTH), BF16), jax.ShapeDtypeStruct((1, GM_WIDTH), F32),
                   jax.ShapeDtypeStruct((GM_GROUPS, CHUNK, CHUNK), F32), jax.ShapeDtypeStruct((CHUNK, GM_GROUPS), F32)],
        scratch_shapes=[pltpu.VMEM((tm, GM_WIDTH), F32), pltpu.VMEM((CHUNK, GM_WIDTH), F32)],
        compiler_params=_params("arbitrary"))(p, v_gain, w_s, b_full, dy)


def _rope(x, ct, s1, s2):
    return x * ct + pltpu.roll(x, LANES - MLA_ROPE // 2, 1) * s1 + pltpu.roll(x, MLA_ROPE // 2, 1) * s2


def _rope_bwd(d, ct, s1, s2):
    return d * ct + pltpu.roll(d * s1, MLA_ROPE // 2, 1) + pltpu.roll(d * s2, LANES - MLA_ROPE // 2, 1)


def _head_norm(x, gain):
    r = lax.rsqrt(jnp.sum(x * x, axis=-1, keepdims=True) * (1.0 / MLA_QK) + EPS)
    return x * r * gain, r


def _head_norm_bwd(x, r, gain, d):
    u = d * gain
    return r * u - x * (r * r * r) * (jnp.sum(x * u, axis=-1, keepdims=True) * (1.0 / MLA_QK))


def _mla_specs(tm):
    cq = pl.BlockSpec((tm, MLA_Q_RANK), lambda i: (i, T_CQ // MLA_Q_RANK))
    ckv = pl.BlockSpec((tm, MLA_KV_RANK), lambda i: (i, T_CKV // MLA_KV_RANK))
    kr = pl.BlockSpec((tm, LANES), lambda i: (i, T_KR // LANES))
    tab = pl.BlockSpec((tm, LANES), lambda i: (i, 0))
    return cq, ckv, kr, tab


def _const(shape):
    return pl.BlockSpec(shape, lambda i: tuple(0 for _ in shape))


def _mla_pre_fwd(p, tabs, qn_g, kvn_g, wuq, wkv, gq, gk, name):
    s = p.shape[0]
    tm = _pick(s, 256)
    ct, s1, s2 = tabs

    def body(cq_ref, ckv_ref, kr_ref, ct_ref, s1_ref, s2_ref, qg_ref, kvg_ref, wuq_ref, wkv_ref, gq_ref, gk_ref,
             q_ref, k_ref, v_ref):
        cq, ckv, kr = cq_ref[...], ckv_ref[...], kr_ref[...]
        ctv, s1v, s2v = ct_ref[...], s1_ref[...], s2_ref[...]
        rq = lax.rsqrt(jnp.mean(cq * cq, axis=-1, keepdims=True) + EPS)
        q = jnp.dot((cq * rq * qg_ref[...]).astype(BF16), wuq_ref[...], preferred_element_type=F32)
        rk = lax.rsqrt(jnp.mean(ckv * ckv, axis=-1, keepdims=True) + EPS)
        kv = jnp.dot((ckv * rk * kvg_ref[...]).astype(BF16), wkv_ref[...], preferred_element_type=F32)
        v_ref[...] = kv[:, HP:].astype(BF16)
        for h in range(MLA_HEADS):
            hs = slice(h * LANES, (h + 1) * LANES)
            qh, _ = _head_norm(q[:, hs], gq_ref[...])
            q_ref[:, hs] = (_rope(qh, ctv, s1v, s2v) * _Q_SCALE).astype(BF16)
            kh, _ = _head_norm(kv[:, hs] + kr, gk_ref[...])
            k_ref[:, hs] = _rope(kh, ctv, s1v, s2v).astype(BF16)

    cq_s, ckv_s, kr_s, tab_s = _mla_specs(tm)
    out = pl.BlockSpec((tm, HP), lambda i: (i, 0))
    return pl.pallas_call(
        body, name=name, grid=(s // tm,),
        in_specs=[cq_s, ckv_s, kr_s, tab_s, tab_s, tab_s, _const((1, MLA_Q_RANK)), _const((1, MLA_KV_RANK)),
                  _const((MLA_Q_RANK, HP)), _const((MLA_KV_RANK, 2 * HP)), _const((1, LANES)), _const((1, LANES))],
        out_specs=[out, out, out], out_shape=[jax.ShapeDtypeStruct((s, HP), BF16)] * 3,
        compiler_params=_params("parallel"))(p, p, p, ct, s1, s2, qn_g, kvn_g, wuq, wkv, gq, gk)


def _mla_pre_bwd(p, tabs, qn_g, kvn_g, wuq, wkv, gq, gk, dq, dk, dv, name, side=None):
    s = p.shape[0]
    tm = _pick(s, 256)
    ct, s1, s2 = tabs

    def body(cq_ref, ckv_ref, kr_ref, ct_ref, s1_ref, s2_ref, qg_ref, kvg_ref, wuq_ref, wkv_ref, gq_ref, gk_ref,
             dq_ref, dk_ref, dv_ref, dcq_ref, dckv_ref, dkr_ref, dwuq_ref, dwkv_ref, dqg_ref, dkvg_ref, dgq_ref, dgk_ref,
             dqp, dkvp):
        @pl.when(pl.program_id(0) == 0)
        def _():
            for ref in (dwuq_ref, dwkv_ref, dqg_ref, dkvg_ref, dgq_ref, dgk_ref):
                ref[...] = jnp.zeros_like(ref)

        cq, ckv, kr = cq_ref[...], ckv_ref[...], kr_ref[...]
        ctv, s1v, s2v = ct_ref[...], s1_ref[...], s2_ref[...]
        rq = lax.rsqrt(jnp.mean(cq * cq, axis=-1, keepdims=True) + EPS)
        qn = (cq * rq * qg_ref[...]).astype(BF16)
        q = jnp.dot(qn, wuq_ref[...], preferred_element_type=F32)
        rk = lax.rsqrt(jnp.mean(ckv * ckv, axis=-1, keepdims=True) + EPS)
        kvn = (ckv * rk * kvg_ref[...]).astype(BF16)
        kv = jnp.dot(kvn, wkv_ref[...], preferred_element_type=F32)
        gqv, gkv = gq_ref[...], gk_ref[...]
        dgq = jnp.zeros((1, LANES), F32)
        dgk = jnp.zeros((1, LANES), F32)
        dkr = jnp.zeros((tm, LANES), F32)
        for h in range(MLA_HEADS):
            hs = slice(h * LANES, (h + 1) * LANES)
            xq = q[:, hs]
            _, r = _head_norm(xq, gqv)
            d = _rope_bwd(dq_ref[:, hs].astype(F32), ctv, s1v, s2v)
            dgq = dgq + jnp.sum(d * xq * r, axis=0, keepdims=True)
            dqp[:, hs] = _head_norm_bwd(xq, r, gqv, d)
            xk = kv[:, hs] + kr
            _, r = _head_norm(xk, gkv)
            d = _rope_bwd(dk_ref[:, hs].astype(F32), ctv, s1v, s2v)
            dgk = dgk + jnp.sum(d * xk * r, axis=0, keepdims=True)
            dxk = _head_norm_bwd(xk, r, gkv, d)
            dkvp[:, hs] = dxk
            dkr = dkr + dxk
        dkvp[:, HP:] = dv_ref[...].astype(F32)
        dgq_ref[...] += dgq
        dgk_ref[...] += dgk
        dkr_ref[...] = dkr.astype(BF16)
        tn = (((0,), (0,)), ((), ()))
        nt = (((1,), (1,)), ((), ()))
        dq16 = dqp[...].astype(BF16)
        dwuq_ref[...] += lax.dot_general(qn, dq16, tn, preferred_element_type=F32)
        dqn = lax.dot_general(dq16, wuq_ref[...], nt, preferred_element_type=F32)
        dqg_ref[...] += jnp.sum(dqn * cq * rq, axis=0, keepdims=True)
        u = dqn * qg_ref[...]
        dcq_ref[...] = (rq * u - cq * (rq * rq * rq) * jnp.mean(cq * u, axis=-1, keepdims=True)).astype(BF16)
        dkv16 = dkvp[...].astype(BF16)
        dwkv_ref[...] += lax.dot_general(kvn, dkv16, tn, preferred_element_type=F32)
        dkvn = lax.dot_general(dkv16, wkv_ref[...], nt, preferred_element_type=F32)
        dkvg_ref[...] += jnp.sum(dkvn * ckv * rk, axis=0, keepdims=True)
        u = dkvn * kvg_ref[...]
        dckv_ref[...] = (rk * u - ckv * (rk * rk * rk) * jnp.mean(ckv * u, axis=-1, keepdims=True)).astype(BF16)

    cq_s, ckv_s, kr_s, tab_s = _mla_specs(tm)
    hd = pl.BlockSpec((tm, HP), lambda i: (i, 0))
    return _call(
        body, name=name, grid=(s // tm,),
        in_specs=[cq_s, ckv_s, kr_s, tab_s, tab_s, tab_s, _const((1, MLA_Q_RANK)), _const((1, MLA_KV_RANK)),
                  _const((MLA_Q_RANK, HP)), _const((MLA_KV_RANK, 2 * HP)), _const((1, LANES)), _const((1, LANES)),
                  hd, hd, hd],
        out_specs=[pl.BlockSpec((tm, MLA_Q_RANK), lambda i: (i, 0)), pl.BlockSpec((tm, MLA_KV_RANK), lambda i: (i, 0)),
                   pl.BlockSpec((tm, LANES), lambda i: (i, 0)), _const((MLA_Q_RANK, HP)), _const((MLA_KV_RANK, 2 * HP)),
                   _const((1, MLA_Q_RANK)), _const((1, MLA_KV_RANK)), _const((1, LANES)), _const((1, LANES))],
        out_shape=[jax.ShapeDtypeStruct((s, MLA_Q_RANK), BF16), jax.ShapeDtypeStruct((s, MLA_KV_RANK), BF16),
                   jax.ShapeDtypeStruct((s, LANES), BF16), jax.ShapeDtypeStruct((MLA_Q_RANK, HP), F32),
                   jax.ShapeDtypeStruct((MLA_KV_RANK, 2 * HP), F32), jax.ShapeDtypeStruct((1, MLA_Q_RANK), F32),
                   jax.ShapeDtypeStruct((1, MLA_KV_RANK), F32), jax.ShapeDtypeStruct((1, LANES), F32),
                   jax.ShapeDtypeStruct((1, LANES), F32)],
        scratch_shapes=[pltpu.VMEM((tm, HP), F32), pltpu.VMEM((tm, 2 * HP), F32)],
        args=(p, p, p, ct, s1, s2, qn_g, kvn_g, wuq, wkv, gq, gk, dq, dk, dv), semantics=("arbitrary",), side=side)


_ATT_SCALE = MLA_QK ** -0.5
_LOG2E = 1.4426950408889634
_Q_SCALE = _ATT_SCALE * _LOG2E
ATT_BLOCK = 1024
_NEG = -1e30
_NT = (((1,), (1,)), ((), ()))
_TN = (((0,), (0,)), ((), ()))


def _tri_rows(step, n):
    i = step * 0
    for m in range(1, n):
        i = i + (step >= m * (m + 1) // 2).astype(jnp.int32)
    return i, step - i * (i + 1) // 2


def _tri_cols(step, n):
    j = step * 0
    for m in range(1, n):
        j = j + (step >= m * n - m * (m - 1) // 2).astype(jnp.int32)
    return j, j + step - (j * n - j * (j - 1) // 2)


def _diag_mask(t):
    return lax.broadcasted_iota(jnp.int32, (t, t), 0) <= lax.broadcasted_iota(jnp.int32, (t, t), 1)


def _attn_fwd(q, k, v, name, side=None):
    s = q.shape[0]
    t = _pick(s, ATT_BLOCK)
    n = s // t

    def body(q_ref, k_ref, v_ref, o_ref, lse_ref, m_s, l_s, acc):
        i, j = _tri_rows(pl.program_id(1), n)

        @pl.when(j == 0)
        def _():
            m_s[...] = jnp.full_like(m_s, _NEG)
            l_s[...] = jnp.zeros_like(l_s)
            acc[...] = jnp.zeros_like(acc)

        def step(diagonal):
            sc = lax.dot_general(k_ref[...], q_ref[...], _NT, preferred_element_type=F32)
            if diagonal:
                sc = jnp.where(_diag_mask(t), sc, _NEG)
            m_new = jnp.maximum(m_s[...], jnp.max(sc, axis=0, keepdims=True))
            alpha = jnp.exp2(m_s[...] - m_new)
            pr = jnp.exp2(sc - m_new)
            l_s[...] = alpha * l_s[...] + jnp.sum(pr, axis=0, keepdims=True)
            acc[...] = alpha * acc[...] + lax.dot_general(v_ref[...], pr.astype(BF16), _TN, preferred_element_type=F32)
            m_s[...] = m_new

        @pl.when(j < i)
        def _():
            step(False)

        @pl.when(j == i)
        def _():
            step(True)
            o_ref[...] = (acc[...] / l_s[...]).T.astype(BF16)
            lse_ref[...] = m_s[...] + jnp.log2(l_s[...])

    qs = pl.BlockSpec((t, LANES), lambda h, p: (_tri_rows(p, n)[0], h))
    ks = pl.BlockSpec((t, LANES), lambda h, p: (_tri_rows(p, n)[1], h))
    return _call(
        body, name=name, grid=(MLA_HEADS, n * (n + 1) // 2), in_specs=[qs, ks, ks],
        out_specs=[qs, pl.BlockSpec((None, 1, t), lambda h, p: (h, 0, _tri_rows(p, n)[0]))],
        out_shape=[jax.ShapeDtypeStruct((s, HP), BF16), jax.ShapeDtypeStruct((MLA_HEADS, 1, s), F32)],
        scratch_shapes=[pltpu.VMEM((1, t), F32), pltpu.VMEM((1, t), F32), pltpu.VMEM((LANES, t), F32)],
        args=(q, k, v), semantics=("parallel", "arbitrary"), side=side)


def _attn_bwd_dq(q, k, v, o, lse, do, name, side=None):
    s = q.shape[0]
    t = _pick(s, ATT_BLOCK)
    n = s // t

    def body(q_ref, k_ref, v_ref, o_ref, lse_ref, do_ref, dq_ref, dl_ref, acc, dl_s):
        i, j = _tri_rows(pl.program_id(1), n)

        @pl.when(j == 0)
        def _():
            acc[...] = jnp.zeros_like(acc)
            dl_s[...] = jnp.sum((do_ref[...].astype(F32) * o_ref[...].astype(F32)).T, axis=0, keepdims=True)

        def step(diagonal):
            sc = lax.dot_general(k_ref[...], q_ref[...], _NT, preferred_element_type=F32)
            if diagonal:
                sc = jnp.where(_diag_mask(t), sc, _NEG)
            pr = jnp.exp2(sc - lse_ref[...])
            dp = lax.dot_general(v_ref[...], do_ref[...].astype(BF16), _NT, preferred_element_type=F32)
            ds = (pr * (dp - dl_s[...])).astype(BF16)
            acc[...] += lax.dot_general(k_ref[...], ds, _TN, preferred_element_type=F32)

        @pl.when(j < i)
        def _():
            step(False)

        @pl.when(j == i)
        def _():
            step(True)
            dq_ref[...] = (acc[...] * _ATT_SCALE).T.astype(BF16)
            dl_ref[...] = dl_s[...]

    qs = pl.BlockSpec((t, LANES), lambda h, p: (_tri_rows(p, n)[0], h))
    ks = pl.BlockSpec((t, LANES), lambda h, p: (_tri_rows(p, n)[1], h))
    ls = pl.BlockSpec((None, 1, t), lambda h, p: (h, 0, _tri_rows(p, n)[0]))
    return _call(
        body, name=name, grid=(MLA_HEADS, n * (n + 1) // 2), in_specs=[qs, ks, ks, qs, ls, qs], out_specs=[qs, ls],
        out_shape=[jax.ShapeDtypeStruct((s, HP), BF16), jax.ShapeDtypeStruct((MLA_HEADS, 1, s), F32)],
        scratch_shapes=[pltpu.VMEM((LANES, t), F32), pltpu.VMEM((1, t), F32)],
        args=(q, k, v, o, lse, do), semantics=("parallel", "arbitrary"), side=side)


def _attn_bwd_dkv(q, k, v, lse, delta, do, name, side=None):
    s = q.shape[0]
    t = _pick(s, ATT_BLOCK)
    n = s // t

    def body(q_ref, k_ref, v_ref, lse_ref, dl_ref, do_ref, dk_ref, dv_ref, dk_acc, dv_acc):
        j, i = _tri_cols(pl.program_id(1), n)

        def step(diagonal):
            sc = lax.dot_general(k_ref[...], q_ref[...], _NT, preferred_element_type=F32)
            if diagonal:
                sc = jnp.where(_diag_mask(t), sc, _NEG)
            pr = jnp.exp2(sc - lse_ref[...])
            do16 = do_ref[...].astype(BF16)
            dv_acc[...] += jnp.dot(pr.astype(BF16), do16, preferred_element_type=F32)
            dp = lax.dot_general(v_ref[...], do16, _NT, preferred_element_type=F32)
            ds = (pr * (dp - dl_ref[...])).astype(BF16)
            dk_acc[...] += jnp.dot(ds, q_ref[...], preferred_element_type=F32)

        @pl.when(i == j)
        def _():
            dk_acc[...] = jnp.zeros_like(dk_acc)
            dv_acc[...] = jnp.zeros_like(dv_acc)
            step(True)

        @pl.when(i > j)
        def _():
            step(False)

        @pl.when(i == n - 1)
        def _():
            dk_ref[...] = (dk_acc[...] * (1.0 / _LOG2E)).astype(BF16)
            dv_ref[...] = dv_acc[...].astype(BF16)

    qs = pl.BlockSpec((t, LANES), lambda h, p: (_tri_cols(p, n)[1], h))
    ks = pl.BlockSpec((t, LANES), lambda h, p: (_tri_cols(p, n)[0], h))
    ls = pl.BlockSpec((None, 1, t), lambda h, p: (h, 0, _tri_cols(p, n)[1]))
    return _call(
        body, name=name, grid=(MLA_HEADS, n * (n + 1) // 2), in_specs=[qs, ks, ks, ls, ls, qs], out_specs=[ks, ks],
        out_shape=[jax.ShapeDtypeStruct((s, HP), BF16)] * 2,
        scratch_shapes=[pltpu.VMEM((t, LANES), F32), pltpu.VMEM((t, LANES), F32)],
        args=(q, k, v, lse, delta, do), semantics=("parallel", "arbitrary"), side=side)


XBC = HP + 2 * SSD_GROUPS * SSD_STATE
BCW = 2 * SSD_GROUPS * SSD_STATE


def _conv_fwd(p, col0, width, conv_w, conv_b, name):
    s = p.shape[0]
    c0, nblk = col0 // LANES, width // LANES

    def body(x_ref, w_ref, b_ref, o_ref, pad):
        pad[0:8, :] = jnp.zeros((8, LANES), F32)
        pad[8:s + 8, :] = x_ref[...].astype(F32)
        acc = jnp.broadcast_to(b_ref[...], (s, LANES))
        for t in range(SSD_CONV):
            acc = acc + pad[pl.ds(8 - (SSD_CONV - 1) + t, s), :] * w_ref[t:t + 1, :]
        o_ref[...] = acc * _sigmoid(acc)

    return pl.pallas_call(
        body, name=name, grid=(nblk,),
        in_specs=[pl.BlockSpec((s, LANES), lambda j: (0, c0 + j)), pl.BlockSpec((SSD_CONV, LANES), lambda j: (0, j)),
                  pl.BlockSpec((1, LANES), lambda j: (0, j))],
        out_specs=pl.BlockSpec((s, LANES), lambda j: (0, j)), out_shape=jax.ShapeDtypeStruct((s, width), F32),
        scratch_shapes=[pltpu.VMEM((s + 8, LANES), F32)], compiler_params=_params("parallel"))(p, conv_w, conv_b)


def _conv_bwd(p, col0, width, conv_w, conv_b, dact, name):
    s = p.shape[0]
    c0, nblk = col0 // LANES, width // LANES

    def body(x_ref, w_ref, b_ref, d_ref, dx_ref, dw_ref, db_ref, pad, padd):
        pad[0:8, :] = jnp.zeros((8, LANES), F32)
        pad[8:s + 8, :] = x_ref[...].astype(F32)
        acc = jnp.broadcast_to(b_ref[...], (s, LANES))
        for t in range(SSD_CONV):
            acc = acc + pad[pl.ds(8 - (SSD_CONV - 1) + t, s), :] * w_ref[t:t + 1, :]
        sg = _sigmoid(acc)
        dpre = d_ref[...] * (sg * (1.0 + acc * (1.0 - sg)))
        padd[0:s, :] = dpre
        padd[s:s + 8, :] = jnp.zeros((8, LANES), F32)
        dx = jnp.zeros((s, LANES), F32)
        for t in range(SSD_CONV):
            dx = dx + padd[pl.ds(SSD_CONV - 1 - t, s), :] * w_ref[t:t + 1, :]
            dw_ref[t:t + 1, :] = jnp.sum(dpre * pad[pl.ds(8 - (SSD_CONV - 1) + t, s), :], axis=0, keepdims=True)
        dx_ref[...] = dx.astype(BF16)
        db_ref[...] = jnp.sum(dpre, axis=0, keepdims=True)

    blk = pl.BlockSpec((s, LANES), lambda j: (0, j))
    return pl.pallas_call(
        body, name=name, grid=(nblk,),
        in_specs=[pl.BlockSpec((s, LANES), lambda j: (0, c0 + j)), pl.BlockSpec((SSD_CONV, LANES), lambda j: (0, j)),
                  pl.BlockSpec((1, LANES), lambda j: (0, j)), blk],
        out_specs=[blk, pl.BlockSpec((SSD_CONV, LANES), lambda j: (0, j)), pl.BlockSpec((1, LANES), lambda j: (0, j))],
        out_shape=[jax.ShapeDtypeStruct((s, width), BF16), jax.ShapeDtypeStruct((SSD_CONV, width), F32),
                   jax.ShapeDtypeStruct((1, width), F32)],
        scratch_shapes=[pltpu.VMEM((s + 8, LANES), F32), pltpu.VMEM((s + 8, LANES), F32)],
        compiler_params=_params("parallel"))(p, conv_w, conv_b, dact)


def _softplus(x):
    return jnp.maximum(x, 0.0) + jnp.log(1.0 + jnp.exp(-jnp.abs(x)))


def _dt_fwd(p, dt_bias, a_log, name):
    s = p.shape[0]
    tm = _pick(s, 512)

    def body(x_ref, b_ref, a_ref, dt_ref, da_ref):
        dtv = _softplus(x_ref[...] + b_ref[...])
        dav = dtv * (-jnp.exp(a_ref[...]))
        for h in range(SSD_HEADS):
            hs = slice(h * LANES, (h + 1) * LANES)
            dt_ref[:, hs] = jnp.broadcast_to(dtv[:, h:h + 1], (tm, LANES))
            da_ref[:, hs] = jnp.broadcast_to(dav[:, h:h + 1], (tm, LANES))

    out = pl.BlockSpec((tm, HP), lambda i: (i, 0))
    return pl.pallas_call(
        body, name=name, grid=(s // tm,),
        in_specs=[pl.BlockSpec((tm, LANES), lambda i: (i, T_DT // LANES)), _const((1, LANES)), _const((1, LANES))],
        out_specs=[out, out], out_shape=[jax.ShapeDtypeStruct((s, HP), F32)] * 2,
        compiler_params=_params("parallel"))(p, dt_bias, a_log)


def _dt_bwd(p, dt_bias, a_log, dda, ddtx, name):
    s = p.shape[0]
    tm = _pick(s, 512)

    def body(x_ref, b_ref, a_ref, dda_ref, ddtx_ref, dx_ref, db_ref, dal_ref):
        @pl.when(pl.program_id(0) == 0)
        def _():
            db_ref[...] = jnp.zeros_like(db_ref)
            dal_ref[...] = jnp.zeros_like(dal_ref)

        x = x_ref[...] + b_ref[...]
        dtv = _softplus(x)
        av = -jnp.exp(a_ref[...])
        lane = lax.broadcasted_iota(jnp.int32, (tm, LANES), 1)
        pa = jnp.zeros((tm, LANES), F32)
        px = jnp.zeros((tm, LANES), F32)
        for h in range(SSD_HEADS):
            pa = jnp.where(lane == h, dda_ref[:, h * LANES:(h + 1) * LANES], pa)
            px = jnp.where(lane == h, ddtx_ref[:, h * LANES:(h + 1) * LANES], px)
        draw = (pa * av + px) * _sigmoid(x)
        dx_ref[...] = draw.astype(BF16)
        db_ref[...] += jnp.sum(draw, axis=0, keepdims=True)
        dal_ref[...] += jnp.sum(pa * dtv, axis=0, keepdims=True) * av

    hd = pl.BlockSpec((tm, HP), lambda i: (i, 0))
    return pl.pallas_call(
        body, name=name, grid=(s // tm,),
        in_specs=[pl.BlockSpec((tm, LANES), lambda i: (i, T_DT // LANES)), _const((1, LANES)), _const((1, LANES)), hd, hd],
        out_specs=[pl.BlockSpec((tm, LANES), lambda i: (i, 0)), _const((1, LANES)), _const((1, LANES))],
        out_shape=[jax.ShapeDtypeStruct((s, LANES), BF16), jax.ShapeDtypeStruct((1, LANES), F32),
                   jax.ShapeDtypeStruct((1, LANES), F32)],
        compiler_params=_params("arbitrary"))(p, dt_bias, a_log, dda, ddtx)


def _cumsum_rows(x):
    row = lax.broadcasted_iota(jnp.int32, x.shape, 0)
    k = 1
    while k < x.shape[0]:
        x = x + jnp.where(row >= k, pltpu.roll(x, k, 0), 0.0)
        k *= 2
    return x


def _rev_cumsum_rows(x):
    n = x.shape[0]
    row = lax.broadcasted_iota(jnp.int32, x.shape, 0)
    k = 1
    while k < n:
        x = x + jnp.where(row < n - k, pltpu.roll(x, n - k, 0), 0.0)
        k *= 2
    return x


HPG = SSD_HEADS // SSD_GROUPS


def _chunk_decay(da):
    cs = _cumsum_rows(da)
    lm = jnp.exp(jnp.where(_tril_mask(), cs - cs.T, _NEG))
    return cs, lm, cs[CHUNK - 1:CHUNK, :]


def _scan_fwd(xs, bc, dtb, dab, name, side=None):
    s = xs.shape[0]
    nc = s // CHUNK

    def body(x_ref, b_ref, c_ref, dt_ref, da_ref, y_ref, sin_ref, state):
        @pl.when(pl.program_id(1) == 0)
        def _():
            state[...] = jnp.zeros_like(state)

        bv = b_ref[...]
        b16, c16 = bv.astype(BF16), c_ref[...].astype(BF16)
        g = lax.dot_general(c16, b16, _NT, preferred_element_type=F32)
        for hh in range(HPG):
            hs = slice(hh * LANES, (hh + 1) * LANES)
            st = state[hh]
            sin_ref[hh] = st
            cs, lm, cl = _chunk_decay(da_ref[:, hs])
            xd = (x_ref[:, hs] * dt_ref[:, hs]).astype(BF16)
            y = jnp.dot((g * lm).astype(BF16), xd, preferred_element_type=F32)
            y_ref[:, hs] = y + jnp.dot(c16, st.astype(BF16), preferred_element_type=F32) * jnp.exp(cs)
            bd = (bv * jnp.exp(cl - cs)).astype(BF16)
            state[hh] = jnp.exp(cl) * st + lax.dot_general(bd, xd, _TN, preferred_element_type=F32)

    gw = HPG * LANES
    hd = pl.BlockSpec((CHUNK, gw), lambda g, c: (c, g))
    return _call(
        body, name=name, grid=(SSD_GROUPS, nc),
        in_specs=[hd, pl.BlockSpec((CHUNK, LANES), lambda g, c: (c, g)),
                  pl.BlockSpec((CHUNK, LANES), lambda g, c: (c, SSD_GROUPS + g)), hd, hd],
        out_specs=[hd, pl.BlockSpec((HPG, None, SSD_STATE, LANES), lambda g, c: (g, c, 0, 0))],
        out_shape=[jax.ShapeDtypeStruct((s, HP), F32), jax.ShapeDtypeStruct((SSD_HEADS, nc, SSD_STATE, LANES), F32)],
        scratch_shapes=[pltpu.VMEM((HPG, SSD_STATE, LANES), F32)],
        args=(xs, bc, bc, dtb, dab), semantics=("parallel", "arbitrary"), side=side)


def _scan_bwd(xs, bc, dtb, dab, s_in, dy, d_vec, name):
    s = xs.shape[0]
    nc = s // CHUNK

    def body(x_ref, b_ref, c_ref, dt_ref, da_ref, sin_ref, dy_ref, dv_ref, dx_ref, db_ref, dc_ref, dda_ref, ddtx_ref, dstate):
        @pl.when(pl.program_id(1) == 0)
        def _():
            dstate[...] = jnp.zeros_like(dstate)

        bv = b_ref[...]
        b16, c16 = bv.astype(BF16), c_ref[...].astype(BF16)
        g = lax.dot_general(c16, b16, _NT, preferred_element_type=F32)
        row = lax.broadcasted_iota(jnp.int32, (CHUNK, 1), 0)
        dbm = jnp.zeros((CHUNK, SSD_STATE), F32)
        dcm = jnp.zeros((CHUNK, SSD_STATE), F32)
        for hh in range(HPG):
            hs = slice(hh * LANES, (hh + 1) * LANES)
            st, ds = sin_ref[hh], dstate[hh]
            st16, ds16 = st.astype(BF16), ds.astype(BF16)
            xv, dtv, dyv = x_ref[:, hs], dt_ref[:, hs], dy_ref[:, hs]
            cs, lm, cl = _chunk_decay(da_ref[:, hs])
            ecs, ecl = jnp.exp(cs), jnp.exp(cl)
            decay = jnp.exp(cl - cs)
            xd = (xv * dtv).astype(BF16)
            dy16 = dyv.astype(BF16)
            dye = (dyv * ecs).astype(BF16)
            yoff = jnp.dot(c16, st16, preferred_element_type=F32) * ecs
            dcs = jnp.sum(dyv * yoff, axis=-1, keepdims=True)
            dcm = dcm + lax.dot_general(dye, st16, _NT, preferred_element_type=F32)
            dstate[hh] = ecl * ds + lax.dot_general(c16, dye, _TN, preferred_element_type=F32)
            dcl = jnp.sum(jnp.sum(ds * st, axis=0, keepdims=True), axis=1, keepdims=True) * ecl[:, 0:1]
            bd32 = bv * decay
            qm = lax.dot_general(xd, ds16, _NT, preferred_element_type=F32)
            dbm = dbm + qm * decay
            w = jnp.sum(bd32 * qm, axis=-1, keepdims=True)
            dcs = dcs - w
            dcl = dcl + jnp.sum(w, axis=0, keepdims=True)
            dxd = jnp.dot(bd32.astype(BF16), ds16, preferred_element_type=F32)
            m16 = (g * lm).astype(BF16)
            dm = lax.dot_general(dy16, xd, _NT, preferred_element_type=F32)
            dxd = dxd + lax.dot_general(m16, dy16, _TN, preferred_element_type=F32)
            dg = dm * lm
            dg16 = dg.astype(BF16)
            tt = dg * g
            dcm = dcm + jnp.dot(dg16, b16, preferred_element_type=F32)
            dbm = dbm + lax.dot_general(dg16, c16, _TN, preferred_element_type=F32)
            dcs = dcs + jnp.sum(tt, axis=-1, keepdims=True) - jnp.sum(tt.T, axis=-1, keepdims=True)
            dcs = dcs + jnp.where(row == CHUNK - 1, dcl, 0.0)
            dda_ref[:, hs] = _rev_cumsum_rows(jnp.broadcast_to(dcs, (CHUNK, LANES)))
            ddtx_ref[:, hs] = jnp.broadcast_to(jnp.sum(dxd * xv, axis=-1, keepdims=True), (CHUNK, LANES))
            dx_ref[:, hs] = dxd * dtv + dyv * dv_ref[:, hs]
        db_ref[...] = dbm
        dc_ref[...] = dcm

    gw = HPG * LANES
    hd = pl.BlockSpec((CHUNK, gw), lambda g, c: (nc - 1 - c, g))
    gp = pl.BlockSpec((CHUNK, LANES), lambda g, c: (nc - 1 - c, g))
    return pl.pallas_call(
        body, name=name, grid=(SSD_GROUPS, nc),
        in_specs=[hd, gp, pl.BlockSpec((CHUNK, LANES), lambda g, c: (nc - 1 - c, SSD_GROUPS + g)), hd, hd,
                  pl.BlockSpec((HPG, None, SSD_STATE, LANES), lambda g, c: (g, nc - 1 - c, 0, 0)), hd,
                  pl.BlockSpec((1, gw), lambda g, c: (0, g))],
        out_specs=[hd, gp, gp, hd, hd],
        out_shape=[jax.ShapeDtypeStruct((s, HP), F32), jax.ShapeDtypeStruct((s, SSD_GROUPS * SSD_STATE), F32),
                   jax.ShapeDtypeStruct((s, SSD_GROUPS * SSD_STATE), F32), jax.ShapeDtypeStruct((s, HP), F32),
                   jax.ShapeDtypeStruct((s, HP), F32)],
        scratch_shapes=[pltpu.VMEM((HPG, SSD_STATE, LANES), F32)],
        compiler_params=_params("parallel", "arbitrary"))(xs, bc, bc, dtb, dab, s_in, dy, d_vec)


_GN = SSD_INNER // SSD_GROUPS
_GW = HP // SSD_GROUPS


def _ssd_post_fwd(y, xbc, p, d_vec, gain, name):
    s = y.shape[0]
    tm = _pick(s, 512)

    def body(y_ref, x_ref, z_ref, d_ref, g_ref, o_ref):
        z = z_ref[...].astype(F32)
        y2 = (y_ref[...] + x_ref[...] * d_ref[...]) * (z * _sigmoid(z))
        for g in range(SSD_GROUPS):
            gs = slice(g * _GW, (g + 1) * _GW)
            yg = y2[:, gs]
            r = lax.rsqrt(jnp.sum(yg * yg, axis=-1, keepdims=True) * (1.0 / _GN) + EPS)
            o_ref[:, gs] = (yg * r * g_ref[:, gs]).astype(BF16)

    hd = pl.BlockSpec((tm, HP), lambda i: (i, 0))
    return pl.pallas_call(
        body, name=name, grid=(s // tm,),
        in_specs=[hd, hd, pl.BlockSpec((tm, HP), lambda i: (i, C_Z // HP)), _const((1, HP)), _const((1, HP))],
        out_specs=hd, out_shape=jax.ShapeDtypeStruct((s, HP), BF16), compiler_params=_params("parallel"))(y, xbc, p, d_vec, gain)


def _ssd_post_bwd(y, xbc, p, d_vec, gain, dyn, name):
    s = y.shape[0]
    tm = _pick(s, 512)

    def body(y_ref, x_ref, z_ref, d_ref, g_ref, dn_ref, dy_ref, dz_ref, dg_ref, dd_ref):
        @pl.when(pl.program_id(0) == 0)
        def _():
            dg_ref[...] = jnp.zeros_like(dg_ref)
            dd_ref[...] = jnp.zeros_like(dd_ref)

        z, xv = z_ref[...].astype(F32), x_ref[...]
        sg = _sigmoid(z)
        sz = z * sg
        yt = y_ref[...] + xv * d_ref[...]
        y2 = yt * sz
        for g in range(SSD_GROUPS):
            gs = slice(g * _GW, (g + 1) * _GW)
            yg, dn = y2[:, gs], dn_ref[:, gs].astype(F32)
            r = lax.rsqrt(jnp.sum(yg * yg, axis=-1, keepdims=True) * (1.0 / _GN) + EPS)
            u = dn * g_ref[:, gs]
            dy2 = r * u - yg * (r * r * r) * (jnp.sum(yg * u, axis=-1, keepdims=True) * (1.0 / _GN))
            dg_ref[:, gs] += jnp.sum(dn * yg * r, axis=0, keepdims=True)
            dyt = dy2 * sz[:, gs]
            dy_ref[:, gs] = dyt
            dz_ref[:, gs] = (dy2 * yt[:, gs] * (sg[:, gs] * (1.0 + z[:, gs] * (1.0 - sg[:, gs])))).astype(BF16)
            dd_ref[:, gs] += jnp.sum(dyt * xv[:, gs], axis=0, keepdims=True)

    hd = pl.BlockSpec((tm, HP), lambda i: (i, 0))
    return pl.pallas_call(
        body, name=name, grid=(s // tm,),
        in_specs=[hd, hd, pl.BlockSpec((tm, HP), lambda i: (i, C_Z // HP)), _const((1, HP)), _const((1, HP)), hd],
        out_specs=[hd, hd, _const((1, HP)), _const((1, HP))],
        out_shape=[jax.ShapeDtypeStruct((s, HP), F32), jax.ShapeDtypeStruct((s, HP), BF16),
                   jax.ShapeDtypeStruct((1, HP), F32), jax.ShapeDtypeStruct((1, HP), F32)],
        compiler_params=_params("arbitrary"))(y, xbc, p, d_vec, gain, dyn)


def _merge_fwd(p, ya, o, yc, wb0, wb1, wb2, w_out, x, name, side=None):
    s = p.shape[0]
    tm = _pick(s, 512)

    def body(g_ref, ya_ref, o_ref, yc_ref, w0_ref, w1_ref, w2_ref, wo_ref, x_ref, mg_ref, y_ref):
        acc = jnp.zeros((tm, D_MODEL), F32)
        for i, (b_ref, w_ref) in enumerate(((ya_ref, w0_ref), (o_ref, w1_ref), (yc_ref, w2_ref))):
            t = jnp.dot(b_ref[...].astype(BF16), w_ref[...], preferred_element_type=F32)
            acc = acc + _sigmoid(g_ref[:, i * D_MODEL:(i + 1) * D_MODEL].astype(F32)) * t
        mg = acc.astype(BF16)
        mg_ref[...] = mg
        y_ref[...] = x_ref[...] + jnp.dot(mg, wo_ref[...], preferred_element_type=F32)

    row = pl.BlockSpec((tm, D_MODEL), lambda i: (i, 0))
    return _call(
        body, name=name, grid=(s // tm,),
        in_specs=[pl.BlockSpec((tm, 3 * D_MODEL), lambda i: (i, C_G // (3 * D_MODEL))),
                  pl.BlockSpec((tm, GM_WIDTH), lambda i: (i, 0)), row, row,
                  _resident((GM_WIDTH, D_MODEL)), _resident((HP, D_MODEL)), _resident((HP, D_MODEL)),
                  _resident((D_MODEL, D_MODEL)), row],
        out_specs=[row, row],
        out_shape=[jax.ShapeDtypeStruct((s, D_MODEL), BF16), jax.ShapeDtypeStruct((s, D_MODEL), F32)],
        scratch_shapes=[], args=(p, ya, o, yc, wb0, wb1, wb2, w_out, x), semantics=("parallel",), side=side)


def _merge_bwd(p, ya, o, yc, wb0, wb1, wb2, w_out, dy, name):
    s = p.shape[0]
    tm = _pick(s, 512)

    def body(g_ref, ya_ref, o_ref, yc_ref, w0_ref, w1_ref, w2_ref, wo_ref, dy_ref,
             d0_ref, d1_ref, d2_ref, dg_ref, dya_ref, do_ref, dyc_ref):
        dm = lax.dot_general(dy_ref[...].astype(BF16), wo_ref[...], _NT, preferred_element_type=F32)
        for i, (b_ref, w_ref, d_ref, db_ref) in enumerate(((ya_ref, w0_ref, d0_ref, dya_ref), (o_ref, w1_ref, d1_ref, do_ref),
                                                            (yc_ref, w2_ref, d2_ref, dyc_ref))):
            cs = slice(i * D_MODEL, (i + 1) * D_MODEL)
            t = jnp.dot(b_ref[...].astype(BF16), w_ref[...], preferred_element_type=F32)
            sg = _sigmoid(g_ref[:, cs].astype(F32))
            dt16 = (dm * sg).astype(BF16)
            d_ref[...] = dt16
            dg_ref[:, cs] = (dm * t * sg * (1.0 - sg)).astype(BF16)
            db_ref[...] = lax.dot_general(dt16, w_ref[...], _NT, preferred_element_type=F32).astype(db_ref.dtype)

    row = pl.BlockSpec((tm, D_MODEL), lambda i: (i, 0))
    nar = pl.BlockSpec((tm, GM_WIDTH), lambda i: (i, 0))
    wide = pl.BlockSpec((tm, 3 * D_MODEL), lambda i: (i, 0))
    return pl.pallas_call(
        body, name=name, grid=(s // tm,),
        in_specs=[pl.BlockSpec((tm, 3 * D_MODEL), lambda i: (i, C_G // (3 * D_MODEL))), nar, row, row,
                  _resident((GM_WIDTH, D_MODEL)), _resident((HP, D_MODEL)), _resident((HP, D_MODEL)),
                  _resident((D_MODEL, D_MODEL)), row],
        out_specs=[row, row, row, wide, nar, row, row],
        out_shape=[jax.ShapeDtypeStruct((s, D_MODEL), BF16)] * 3 + [jax.ShapeDtypeStruct((s, 3 * D_MODEL), BF16),
                   jax.ShapeDtypeStruct((s, GM_WIDTH), BF16), jax.ShapeDtypeStruct((s, D_MODEL), BF16),
                   jax.ShapeDtypeStruct((s, D_MODEL), F32)],
        compiler_params=_params("parallel"))(p, ya, o, yc, wb0, wb1, wb2, w_out, dy)


def _loss_head(y, target, name):
    s, d = y.shape
    tm = _pick(s, 512)

    def body(y_ref, t_ref, dy_ref, sq_ref):
        @pl.when(pl.program_id(0) == 0)
        def _():
            sq_ref[...] = jnp.zeros_like(sq_ref)

        e = y_ref[...] - t_ref[...]
        dy_ref[...] = e * (1.0 / d)
        sq_ref[...] += jnp.sum(e * e, axis=0, keepdims=True)

    row = pl.BlockSpec((tm, d), lambda i: (i, 0))
    return pl.pallas_call(
        body, name=name, grid=(s // tm,), in_specs=[row, row], out_specs=[row, _const((1, d))],
        out_shape=[jax.ShapeDtypeStruct((s, d), F32), jax.ShapeDtypeStruct((1, d), F32)],
        compiler_params=_params("arbitrary"))(y, target)


def _adamw(w, g, m, v, name):
    rows, cols = w.shape
    tr = rows
    for cand in (512, 256, 128, 64, 32, 16, 8):
        if rows % cand == 0 and cand * cols * 4 <= 3 * 1024 * 1024:
            tr = cand
            break

    def body(w_ref, g_ref, m_ref, v_ref, d_ref, nm_ref, nv_ref):
        d_ref[...], nm_ref[...], nv_ref[...] = _adam_update(w_ref[...], g_ref[...], m_ref[...], v_ref[...])

    blk = pl.BlockSpec((tr, cols), lambda i: (i, 0))
    return pl.pallas_call(
        body, name=name, grid=(rows // tr,), in_specs=[blk] * 4, out_specs=[blk] * 3,
        out_shape=[jax.ShapeDtypeStruct((rows, cols), F32)] * 3, compiler_params=_params("parallel"))(w, g, m, v)


def _adam_update(w, g, m, v):
    nm = ADAM_B1 * m + (1.0 - ADAM_B1) * g
    nv = ADAM_B2 * v + (1.0 - ADAM_B2) * (g * g)
    c1 = 1.0 - ADAM_B1 ** ADAM_STEP
    c2 = 1.0 - ADAM_B2 ** ADAM_STEP
    return -ADAM_LR * ((nm / c1) / (jnp.sqrt(nv / c2) + ADAM_EPS) + ADAM_WD * w), nm, nv


def _adamw_sharded(w, m, v, mine, theirs, name, side=None):
    depth, rows, cols = w.shape
    tr = _row_tile(rows // 2, cols, 1024 * 1024)
    nb = rows // 2 // tr

    def body(w_ref, m_ref, v_ref, a_ref, b_ref, g_ref, d_ref, nm_ref, nv_ref):
        c = lax.axis_index("c")
        g = jnp.where(pl.program_id(1) // nb == c, a_ref[...], b_ref[...])
        g_ref[...] = g
        d_ref[...], nm_ref[...], nv_ref[...] = _adam_update(w_ref[...], g, m_ref[...], v_ref[...])

    blk = pl.BlockSpec((None, tr, cols), lambda l, i: (l, i, 0))
    mine_s = pl.BlockSpec((None, tr, cols), lambda l, i: (l, jnp.where(i // nb == lax.axis_index("c"), i % nb, 0), 0))
    theirs_s = pl.BlockSpec((None, tr, cols), lambda l, i: (l, jnp.where(i // nb == lax.axis_index("c"), 0, i % nb), 0))
    return _call(
        body, name=name, grid=(depth, rows // tr), in_specs=[blk, blk, blk, mine_s, theirs_s], out_specs=[blk] * 4,
        out_shape=[jax.ShapeDtypeStruct((depth, rows, cols), F32)] * 4, scratch_shapes=[],
        args=(w, m, v, mine, theirs), semantics=("parallel", "parallel"), side=side)


ANY = pl.BlockSpec(memory_space=pl.ANY)


def _me():
    return lax.axis_index("x"), lax.axis_index("y"), lax.axis_index("c")


def _other_chips(x, y):
    return [(1 - x, y), (x, 1 - y), (1 - x, 1 - y)]


def _chip_index(cx, cy):
    return 2 * cx + cy


class _Exchange:
    def __init__(self, ins, out_shapes, n_sems, start, finish):
        self.ins, self.out_shapes, self.n_sems, self.start, self.finish = list(ins), list(out_shapes), n_sems, start, finish


def _sem_scratch(ex):
    return [pltpu.SemaphoreType.DMA((ex.n_sems,)), pltpu.SemaphoreType.DMA((ex.n_sems,))]


def _run_exchange(ex, name):
    n_in, n_out = len(ex.ins), len(ex.out_shapes)

    def body(*refs):
        in_refs, out_refs, (send, recv) = refs[:n_in], refs[n_in:n_in + n_out], refs[n_in + n_out:]
        ex.start(in_refs, out_refs, send, recv)
        ex.finish(in_refs, out_refs, send, recv)

    return pl.pallas_call(body, name=name, in_specs=[ANY] * n_in, out_specs=[ANY] * n_out, out_shape=ex.out_shapes,
                          scratch_shapes=_sem_scratch(ex))(*ex.ins)


def _call(body, *, name, grid, in_specs, out_specs, out_shape, scratch_shapes, args, semantics, side=None):
    if side is None:
        return pl.pallas_call(body, name=name, grid=grid, in_specs=in_specs, out_specs=out_specs, out_shape=out_shape,
                              scratch_shapes=scratch_shapes, compiler_params=_params(*semantics))(*args), []
    n_in, n_out, n_sc = len(in_specs), len(out_specs), len(scratch_shapes)
    s_in, s_out = len(side.ins), len(side.out_shapes)

    def hosted(*refs):
        pos = 0
        parts = []
        for size in (n_in, s_in, n_out, s_out, n_sc, 2):
            parts.append(refs[pos:pos + size])
            pos += size
        ins, sins, outs, souts, scratch, (send, recv) = parts
        ids = [pl.program_id(a) for a in range(len(grid))]
        first = functools.reduce(jnp.logical_and, [i == 0 for i in ids])
        last = functools.reduce(jnp.logical_and, [i == g - 1 for i, g in zip(ids, grid)])

        @pl.when(first)
        def _():
            side.start(sins, souts, send, recv)

        body(*ins, *outs, *scratch)

        @pl.when(last)
        def _():
            side.finish(sins, souts, send, recv)

    res = pl.pallas_call(
        hosted, name=name, grid=grid, in_specs=list(in_specs) + [ANY] * s_in, out_specs=list(out_specs) + [ANY] * s_out,
        out_shape=list(out_shape) + side.out_shapes, scratch_shapes=list(scratch_shapes) + _sem_scratch(side),
        compiler_params=_params(*["arbitrary"] * len(grid)))(*args, *side.ins)
    return res[:n_out], res[n_out:]


def _half(ref_rows, c):
    return pl.ds(c * (ref_rows // 2), ref_rows // 2)


def _gather_exchange(shards, layer):
    n = len(shards)
    rows = [a.shape[1] for a in shards]

    def copy(in_refs, out_refs, send, recv, t, k, chip, hc, to, from_input=False):
        dst = out_refs[t].at[chip, _half(rows[t], hc)]
        src = in_refs[t].at[layer, _half(rows[t], hc)] if from_input else dst
        return pltpu.make_async_remote_copy(src_ref=src, dst_ref=dst, send_sem=send.at[7 * t + k], recv_sem=recv.at[7 * t + k],
                                            device_id=to, device_id_type=MESH)

    def own(in_refs, out_refs, send, recv, t):
        x, y, c = _me()
        return pltpu.make_async_remote_copy(src_ref=in_refs[t].at[layer], dst_ref=out_refs[t].at[_chip_index(x, y)],
                                            send_sem=send.at[7 * t + 6], recv_sem=recv.at[7 * t + 6],
                                            device_id=(x, y, 1 - c), device_id_type=MESH)

    def start(in_refs, out_refs, send, recv):
        x, y, c = _me()
        for j, chip in enumerate(_other_chips(x, y)):
            for t in range(n):
                copy(in_refs, out_refs, send, recv, t, j, _chip_index(x, y), c, (*chip, c), from_input=True).start()
        for t in range(n):
            own(in_refs, out_refs, send, recv, t).start()

    def finish(in_refs, out_refs, send, recv):
        x, y, c = _me()
        chips = _other_chips(x, y)
        passed = []
        for t in range(n):
            own(in_refs, out_refs, send, recv, t).wait()
        for j, chip in enumerate(chips):
            for t in range(n):
                copy(in_refs, out_refs, send, recv, t, j, _chip_index(*chip), c, (x, y, c)).wait_recv()
                cp = copy(in_refs, out_refs, send, recv, t, 3 + j, _chip_index(*chip), c, (x, y, 1 - c))
                cp.start()
                passed.append(cp)
        for j, chip in enumerate(chips):
            for t in range(n):
                copy(in_refs, out_refs, send, recv, t, 3 + j, _chip_index(*chip), 1 - c, (x, y, c)).wait_recv()
                copy(in_refs, out_refs, send, recv, t, j, _chip_index(x, y), c, (*chip, c), from_input=True).wait_send()
        for cp in passed:
            cp.wait_send()

    return _Exchange(shards, [jax.ShapeDtypeStruct((N_CHIPS,) + a.shape[1:], a.dtype) for a in shards], 7 * n, start, finish)


def _pair_exchange(gs):
    n = len(gs)
    rows = [a.shape[1] for a in gs]

    def copies(in_refs, out_refs, send, recv):
        x, y, c = _me()
        return [pltpu.make_async_remote_copy(src_ref=in_refs[t].at[:, _half(rows[t], 1 - c)], dst_ref=out_refs[t],
                                             send_sem=send.at[t], recv_sem=recv.at[t], device_id=(x, y, 1 - c),
                                             device_id_type=MESH) for t in range(n)]

    def start(*refs):
        for cp in copies(*refs):
            cp.start()

    def finish(*refs):
        for cp in copies(*refs):
            cp.wait()

    return _Exchange(gs, [jax.ShapeDtypeStruct((N_CHIPS, a.shape[1] // 2, a.shape[2]), a.dtype) for a in gs], n, start, finish)


def _row_tile(rows, cols, budget=2 * 1024 * 1024):
    best = None
    for t in range(8, rows + 1, 8):
        if rows % t == 0 and t * cols * 4 <= budget:
            best = t
    return best or rows


def _pair_add(g, got, name):
    _, rows, cols = g.shape
    tr = _row_tile(rows // 2, cols)
    nb = rows // 2 // tr

    def body(g_ref, r_ref, o16_ref, own_ref):
        x, y, _ = _me()
        tot = g_ref[...] + r_ref[...]
        o16_ref[...] = tot.astype(BF16)

        @pl.when(pl.program_id(1) == _chip_index(x, y))
        def _():
            own_ref[...] = tot

    blk = (None, tr, cols)
    return pl.pallas_call(
        body, name=name, grid=(nb, N_CHIPS),
        in_specs=[pl.BlockSpec(blk, lambda i, k: (k, i + lax.axis_index("c") * nb, 0)),
                  pl.BlockSpec(blk, lambda i, k: (k, i, 0))],
        out_specs=[pl.BlockSpec(blk, lambda i, k: (k, i, 0)), pl.BlockSpec((tr, cols), lambda i, k: (i, 0))],
        out_shape=[jax.ShapeDtypeStruct((N_CHIPS, rows // 2, cols), BF16), jax.ShapeDtypeStruct((rows // 2, cols), F32)],
        compiler_params=_params("parallel", "arbitrary"))(g, got)


def _chip_exchange(parts):
    n = len(parts)

    def copies(in_refs, out_refs, send, recv):
        x, y, c = _me()
        return [pltpu.make_async_remote_copy(src_ref=in_refs[t].at[_chip_index(*chip)], dst_ref=out_refs[t].at[j],
                                             send_sem=send.at[3 * t + j], recv_sem=recv.at[3 * t + j],
                                             device_id=(*chip, c), device_id_type=MESH)
                for j, chip in enumerate(_other_chips(x, y)) for t in range(n)]

    def start(*refs):
        for cp in copies(*refs):
            cp.start()

    def finish(*refs):
        for cp in copies(*refs):
            cp.wait()

    return _Exchange(parts, [jax.ShapeDtypeStruct((3,) + a.shape[1:], a.dtype) for a in parts], 3 * n, start, finish)


def _chip_add(own, got, name, layer, into=None):
    rows, cols = own.shape
    tr = _row_tile(rows, cols, 1024 * 1024)

    def body(own_ref, got_ref, *rest):
        acc = own_ref[...]
        for j in range(3):
            acc = acc + got_ref[j].astype(F32)
        rest[-1][...] = acc

    in_specs = [pl.BlockSpec((tr, cols), lambda i: (i, 0)), pl.BlockSpec((3, tr, cols), lambda i: (0, i, 0))]
    args, alias = [own, got], {}
    if into is not None:
        in_specs.append(ANY)
        args.append(into)
        alias = {2: 0}
    return pl.pallas_call(
        body, name=name, grid=(rows // tr,), in_specs=in_specs,
        out_specs=pl.BlockSpec((None, tr, cols), lambda i: (layer, i, 0)),
        out_shape=jax.ShapeDtypeStruct((DEPTH, rows, cols), F32), input_output_aliases=alias,
        compiler_params=_params("parallel"))(*args)


def _pair_share(halves):
    n = len(halves)

    def copies(in_refs, out_refs, send, recv):
        x, y, c = _me()
        return [pltpu.make_async_remote_copy(src_ref=in_refs[t], dst_ref=out_refs[t], send_sem=send.at[t],
                                             recv_sem=recv.at[t], device_id=(x, y, 1 - c), device_id_type=MESH)
                for t in range(n)]

    def start(*refs):
        for cp in copies(*refs):
            cp.start()

    def finish(*refs):
        for cp in copies(*refs):
            cp.wait()

    return _Exchange(halves, [jax.ShapeDtypeStruct(a.shape, a.dtype) for a in halves], n, start, finish)


N_DEV = 8


def _all_exchange(v):
    r, cols = v.shape

    def peers():
        x, y, c = _me()
        flip = lambda v, f: 1 - v if f else v
        return 4 * x + 2 * y + c, [(flip(x, fx), flip(y, fy), flip(c, fc)) for fx in (0, 1) for fy in (0, 1) for fc in (0, 1)][1:]

    def local(in_refs, out_refs, send, me):
        return pltpu.make_async_copy(in_refs[0], out_refs[0].at[me], send.at[7])

    def start(in_refs, out_refs, send, recv):
        me, others = peers()
        local(in_refs, out_refs, send, me).start()
        for j, peer in enumerate(others):
            pltpu.make_async_remote_copy(src_ref=in_refs[0], dst_ref=out_refs[0].at[me], send_sem=send.at[j],
                                         recv_sem=recv.at[j], device_id=peer, device_id_type=MESH).start()

    def finish(in_refs, out_refs, send, recv):
        me, others = peers()
        for j, (px, py, pc) in enumerate(others):
            pltpu.make_async_remote_copy(src_ref=in_refs[0], dst_ref=out_refs[0].at[4 * px + 2 * py + pc], send_sem=send.at[j],
                                         recv_sem=recv.at[j], device_id=(px, py, pc), device_id_type=MESH).wait()
        local(in_refs, out_refs, send, me).wait()

    return _Exchange([v], [jax.ShapeDtypeStruct((N_DEV, r, cols), v.dtype)], 8, start, finish)


def _sum_slots(a, name):
    n, r, cols = a.shape
    tr = _pick(r, 512) if r % 8 == 0 else r
    for cand in (512, 256, 128, 64, 32, 16, 8):
        if r % cand == 0:
            tr = cand
            break

    def body(a_ref, o_ref):
        acc = a_ref[0]
        for k in range(1, n):
            acc = acc + a_ref[k]
        o_ref[...] = acc

    return pl.pallas_call(
        body, name=name, grid=(r // tr,), in_specs=[pl.BlockSpec((n, tr, cols), lambda i: (0, i, 0))],
        out_specs=pl.BlockSpec((tr, cols), lambda i: (i, 0)), out_shape=jax.ShapeDtypeStruct((r, cols), F32),
        compiler_params=_params("parallel"))(a)


def _join(name, stacked):
    ax = SHARDED[name][1]
    return jnp.concatenate([stacked[k] for k in range(N_CHIPS)], axis=ax)


def _split(name, full):
    ax = SHARDED[name][1]
    return jnp.stack(jnp.split(full, N_CHIPS, axis=ax))


def _heads_pad(a, real, axis):
    shp = a.shape
    a = a.reshape(shp[:axis] + (MLA_HEADS, real) + shp[axis + 1:])
    pad = [(0, 0)] * a.ndim
    pad[axis + 1] = (0, LANES - real)
    a = jnp.pad(a, pad)
    return a.reshape(shp[:axis] + (HP,) + shp[axis + 1:])


def _heads_unpad(a, real, axis):
    shp = a.shape
    a = a.reshape(shp[:axis] + (MLA_HEADS, LANES) + shp[axis + 1:])
    a = lax.slice_in_dim(a, 0, real, axis=axis + 1)
    return a.reshape(shp[:axis] + (MLA_HEADS * real,) + shp[axis + 1:])


def _lane_place(a, start):
    n = a.shape[-1]
    pad = [(0, 0)] * (a.ndim - 1) + [(start, LANES - start - n)]
    return jnp.pad(a, pad)


_O_UV, _O_CQ, _O_CKV, _O_KR, _O_Z, _O_XBC, _O_DT, _O_G = 0, 1024, 1408, 1664, 1696, 2208, 3232, 3240


def _w_in_pad(w):
    sl = lambda a, b: w[:, a:b]
    xs = _heads_pad(sl(_O_XBC, _O_XBC + SSD_INNER), SSD_HEAD_DIM, 1)
    bc = sl(_O_XBC + SSD_INNER, _O_DT)
    main = jnp.concatenate([sl(_O_UV, _O_CQ), _heads_pad(sl(_O_Z, _O_XBC), SSD_HEAD_DIM, 1), xs, sl(_O_G, IN_COLS)], axis=1)
    tail = jnp.concatenate([bc, sl(_O_CKV, _O_KR), sl(_O_CQ, _O_CKV), _lane_place(sl(_O_KR, _O_Z), MLA_NOPE),
                            _lane_place(sl(_O_DT, _O_G), 0), jnp.zeros((w.shape[0], PW_TAIL - T_DT - LANES), w.dtype)], axis=1)
    return main, tail


def _w_in_unpad(gm, gt):
    m = lambda a, n: gm[:, a:a + n]
    t = lambda a, n: gt[:, a:a + n]
    parts = [m(C_UV, 1024), t(T_CQ, MLA_Q_RANK), t(T_CKV, MLA_KV_RANK), t(T_KR + MLA_NOPE, MLA_ROPE),
             _heads_unpad(m(C_Z, HP), SSD_HEAD_DIM, 1), _heads_unpad(m(C_XS, HP), SSD_HEAD_DIM, 1), t(T_BC, BCW),
             t(T_DT, SSD_HEADS), m(C_G, 3 * D_MODEL)]
    return jnp.concatenate(parts, axis=1)


def _xbc_pad(a):
    return jnp.concatenate([_heads_pad(a[..., :SSD_INNER], SSD_HEAD_DIM, a.ndim - 1), a[..., SSD_INNER:]], axis=-1)


def _xbc_unpad(a):
    return jnp.concatenate([_heads_unpad(a[..., :HP], SSD_HEAD_DIM, a.ndim - 1), a[..., HP:]], axis=-1)


def _rope_tables(positions):
    inv_freq = 1.0 / (ROPE_THETA ** (jnp.arange(0, MLA_ROPE, 2, dtype=F32) / MLA_ROPE))
    ang = positions.astype(F32)[:, None] * inv_freq
    cos, sin = jnp.cos(ang), jnp.sin(ang)
    s = positions.shape[0]
    half = MLA_ROPE // 2
    z = lambda n: jnp.zeros((s, n), F32)
    ct = jnp.concatenate([jnp.ones((s, MLA_NOPE), F32), cos, cos, z(LANES - MLA_QK)], axis=1)
    s1 = jnp.concatenate([z(MLA_NOPE), -sin, z(half), z(LANES - MLA_QK)], axis=1)
    s2 = jnp.concatenate([z(MLA_NOPE), z(half), sin, z(LANES - MLA_QK)], axis=1)
    return ct, s1, s2


def _layer_weights(full, small, l, part):
    w = {}
    row = lambda n: small[n][l][None, :]
    stacked = lambda g: g.reshape((N_CHIPS * g.shape[1], g.shape[2]))
    if part in ('ffn1', 'ffn2'):
        w[part + '_w_in'] = full[part + '_w_in']
        w[part + '_w_out'] = stacked(full[part + '_w_out'])
        w[part + '_norm'] = row(part + '_norm')
        return w
    w['w_out'] = stacked(full['w_out'])
    fl = {n: _join(n, full[n]) for n in ('w_in', 'mla_w_uq', 'mla_w_ukv', 'w_branch', 'ssd_conv_w')}
    w['w_in_main'], w['w_in_tail'] = _w_in_pad(fl['w_in'])
    w['wuq'] = _heads_pad(fl['mla_w_uq'], MLA_QK, 1)
    ukv = fl['mla_w_ukv'].reshape(MLA_KV_RANK, MLA_HEADS, MLA_NOPE + MLA_V)
    zero = jnp.zeros((MLA_KV_RANK, MLA_HEADS, LANES - MLA_NOPE), ukv.dtype)
    wk = jnp.concatenate([ukv[:, :, :MLA_NOPE], zero], axis=2).reshape(MLA_KV_RANK, HP)
    wv = jnp.concatenate([ukv[:, :, MLA_NOPE:], zero], axis=2).reshape(MLA_KV_RANK, HP)
    w['wkv'] = jnp.concatenate([wk, wv], axis=1)
    wb = fl['w_branch']
    w['wb0'] = wb[0]
    w['wb1'] = _heads_pad(wb[1], MLA_V, 0)
    w['wb2'] = _heads_pad(wb[2], SSD_HEAD_DIM, 0)
    w['conv_w'] = _xbc_pad(fl['ssd_conv_w'].astype(F32))
    for n in ('mix_norm', 'gm_v_norm', 'mla_q_norm', 'mla_kv_norm'):
        w[n] = row(n)
    w['gm_w_s'] = small['gm_w_s'][l]
    w['gm_b_full'] = jnp.broadcast_to(small['gm_b_s'][l][:, :, None], (GM_GROUPS, CHUNK, LANES))
    w['gq'] = _lane_place(row('mla_q_gain'), 0)
    w['gk'] = _lane_place(row('mla_k_gain'), 0)
    w['conv_b'] = _xbc_pad(row('ssd_conv_b'))
    w['dt_bias'] = _lane_place(row('ssd_dt_bias'), 0)
    w['a_log'] = _lane_place(row('ssd_a_log'), 0)
    w['d_vec'] = jnp.repeat(small['ssd_d'][l], LANES)[None, :]
    w['ssd_norm'] = _heads_pad(row('ssd_norm'), SSD_HEAD_DIM, 1)
    return w


_MIXER_SMALL = ['mla_w_uq', 'mla_w_ukv', 'ssd_conv_w', 'w_branch', 'w_out']
_MIXER_SMALL_G = [n for n in _MIXER_SMALL if n != 'ssd_conv_w']
GATHER_HOSTS = {'attn': ['ffn1_w_in', 'ffn2_w_in'], 'scan': ['ffn1_w_out', 'ffn2_w_out'], 'merge': _MIXER_SMALL, 'ffn2_in': ['w_in']}
GATHER_HOSTS_LATER = {'ffn1_in': ['ffn1_w_out'], 'proj': ['w_in'], 'attn': ['ffn1_w_in', 'ffn2_w_in'], 'scan': ['ffn2_w_out'],
                      'merge': _MIXER_SMALL}
FIRST_NOW = ['ffn1_w_in', 'ffn1_w_out']
FIRST_HOSTS = {'ffn1_in': ['w_in'], 'ffn1_out': _MIXER_SMALL, 'proj': ['ffn2_w_in', 'ffn2_w_out']}
PAIR_HOSTS = {'ffn2_dact': ['ffn1_w_in', 'ffn2_w_in'], 'ffn2_dwin': ['ffn1_w_out', 'w_in', 'ffn2_w_out'] + _MIXER_SMALL_G}
REDUCE_HOSTS = {'dattn_q': ['ffn1_w_out', 'w_in', 'ffn2_w_out'], 'dattn_kv': ['ffn1_w_in', 'ffn2_w_in'], 'dmla_pre': _MIXER_SMALL_G}
LAST_EARLY = ['w_in', 'ffn2_w_in', 'ffn2_w_out'] + _MIXER_SMALL_G
LAST_HOSTS = {'ffn1_dact': ['w_in'], 'ffn1_dwin': ['ffn2_w_in'], 'ffn1_dx': ['ffn2_w_out'] + _MIXER_SMALL_G}
LAST_LATE = ['ffn1_w_in', 'ffn1_w_out']


def _ffn_fwd(x, norm, w4, w_out, tag, sides=None):
    sides = sides or {}
    carried = {}
    (h, gate, up, act), carried[f"{tag}_in"] = _ffn_in(x, norm, w4, f"{tag}_in", sides.get(f"{tag}_in"))
    y, carried[f"{tag}_out"] = _ffn_out(act, w_out, x, f"{tag}_out", sides.get(f"{tag}_out"))
    return y, (x, h, gate, up, act), carried


def _ffn_bwd(dy, saved, norm, w4, w_out, tag, sides=None, after_dwout=None):
    sides = dict(sides or {})
    carried = {}
    x, h, gate, up, act = saved
    dw_out, carried[f"{tag}_dwout"] = _ffn_dwout(act, dy, f"{tag}_dwout", sides.get(f"{tag}_dwout"))
    if after_dwout is not None:
        sides.update(after_dwout(carried[f"{tag}_dwout"]))
    da, carried[f"{tag}_dact"] = _ffn_dact(dy, w_out, gate, up, f"{tag}_dact", sides.get(f"{tag}_dact"))
    dw_in, carried[f"{tag}_dwin"] = _ffn_dwin(h, da, f"{tag}_dwin", sides.get(f"{tag}_dwin"))
    (dx, dnorm), carried[f"{tag}_dx"] = _ffn_dx(da, w4, x, norm, dy, f"{tag}_dx", sides.get(f"{tag}_dx"))
    return dx, dnorm, dw_in, dw_out.reshape((N_CHIPS, 2 * FC // N_CHIPS, D_MODEL)), carried


def _mixer_fwd(x, w, tabs, tag, sides=None):
    sides = sides or {}
    carried = {}
    (h, pm, pt), carried['proj'] = _mixer_proj(x, w['mix_norm'], w['w_in_main'], w['w_in_tail'], f"{tag}_proj", sides.get('proj'))
    ya = _gmlp_fwd(pm, w['gm_v_norm'], w['gm_w_s'], w['gm_b_full'], f"{tag}_gmlp")
    q, k, v = _mla_pre_fwd(pt, tabs, w['mla_q_norm'], w['mla_kv_norm'], w['wuq'], w['wkv'], w['gq'], w['gk'], f"{tag}_mla_pre")
    (o, lse), carried['attn'] = _attn_fwd(q, k, v, f"{tag}_attn", sides.get('attn'))
    xs = _conv_fwd(pm, C_XS, HP, w['conv_w'][:, :HP], w['conv_b'][:, :HP], f"{tag}_conv_x")
    bc = _conv_fwd(pt, T_BC, BCW, w['conv_w'][:, HP:], w['conv_b'][:, HP:], f"{tag}_conv_bc")
    dtb, dab = _dt_fwd(pt, w['dt_bias'], w['a_log'], f"{tag}_dt")
    (ys, s_in), carried['scan'] = _scan_fwd(xs, bc, dtb, dab, f"{tag}_scan", sides.get('scan'))
    yc = _ssd_post_fwd(ys, xs, pm, w['d_vec'], w['ssd_norm'], f"{tag}_ssd_post")
    (mg, y), carried['merge'] = _merge_fwd(pm, ya, o, yc, w['wb0'], w['wb1'], w['wb2'], w['w_out'], x, f"{tag}_merge",
                                           sides.get('merge'))
    return y, (x, h, pm, pt, ya, q, k, v, o, lse, xs, bc, dtb, dab, ys, s_in, yc, mg), carried


def _pair_sums(pending, got):
    return {n: _pair_add(pending[n], got[n], f"pair_add_{n}") for n in got}


def _chip_sums(sums, arrived, layer, stacked):
    for n in arrived:
        stacked[n] = _chip_add(sums[n][1], arrived[n], f"chip_add_{n}", layer, stacked.get(n))


def _reduce_to_chip(pending, layer, stacked):
    names = list(pending)
    got = _run_exchange(_pair_exchange([pending[n] for n in names]), "pair_exchange")
    sums = _pair_sums(pending, dict(zip(names, got)))
    arrived = _run_exchange(_chip_exchange([sums[n][0] for n in names]), "chip_exchange")
    _chip_sums(sums, dict(zip(names, arrived)), layer, stacked)


def _mixer_bwd(dy, saved, w, tabs, tag, sides=None):
    sides = sides or {}
    carried = {}
    x, h, pm, pt, ya, q, k, v, o, lse, xs, bc, dtb, dab, ys, s_in, yc, mg = saved
    g = {}
    g['w_out'] = _matmul(mg, dy, ta=True, name=f"{tag}_dwout").reshape((N_CHIPS, D_MODEL // N_CHIPS, D_MODEL))
    d0, d1, d2, dgates, dya, do, dyc = _merge_bwd(pm, ya, o, yc, w['wb0'], w['wb1'], w['wb2'], w['w_out'], dy, f"{tag}_dmerge")
    dwb0 = _matmul(ya, d0, ta=True, name=f"{tag}_dwb0")
    dwb1 = _matmul(o, d1, ta=True, name=f"{tag}_dwb1")
    dwb2 = _matmul(yc, d2, ta=True, name=f"{tag}_dwb2")
    g['w_branch'] = _split('w_branch', jnp.stack([dwb0, _heads_unpad(dwb1, MLA_V, 0), _heads_unpad(dwb2, SSD_HEAD_DIM, 0)]))
    duv, g['gm_v_norm'], g['gm_w_s'], db = _gmlp_bwd(pm, w['gm_v_norm'], w['gm_w_s'], w['gm_b_full'], dya, f"{tag}_dgmlp")
    g['gm_b_s'] = db.T
    (dq, delta), carried['dattn_q'] = _attn_bwd_dq(q, k, v, o, lse, do, f"{tag}_dattn_q", sides.get('dattn_q'))
    (dk, dv), carried['dattn_kv'] = _attn_bwd_dkv(q, k, v, lse, delta, do, f"{tag}_dattn_kv", sides.get('dattn_kv'))
    (dcq, dckv, dkr, dwuq, dwkv, g['mla_q_norm'], g['mla_kv_norm'], dgq, dgk), carried['dmla_pre'] = _mla_pre_bwd(
        pt, tabs, w['mla_q_norm'], w['mla_kv_norm'], w['wuq'], w['wkv'], w['gq'], w['gk'], dq, dk, dv, f"{tag}_dmla_pre",
        sides.get('dmla_pre'))
    g['mla_w_uq'] = _split('mla_w_uq', _heads_unpad(dwuq, MLA_QK, 1))
    dwk = dwkv[:, :HP].reshape(MLA_KV_RANK, MLA_HEADS, LANES)[:, :, :MLA_NOPE]
    dwv = dwkv[:, HP:].reshape(MLA_KV_RANK, MLA_HEADS, LANES)[:, :, :MLA_V]
    g['mla_w_ukv'] = _split('mla_w_ukv', jnp.concatenate([dwk, dwv], axis=2).reshape(MLA_KV_RANK, MLA_HEADS * (MLA_NOPE + MLA_V)))
    g['mla_q_gain'], g['mla_k_gain'] = dgq[:, :MLA_QK], dgk[:, :MLA_QK]
    dys, dz, dssd_norm, dd = _ssd_post_bwd(ys, xs, pm, w['d_vec'], w['ssd_norm'], dyc, f"{tag}_dssd_post")
    g['ssd_norm'] = _heads_unpad(dssd_norm, SSD_HEAD_DIM, 1)
    g['ssd_d'] = jnp.sum(dd.reshape(SSD_HEADS, LANES), axis=1)[None, :]
    dxs, dbm, dcm, dda, ddtx = _scan_bwd(xs, bc, dtb, dab, s_in, dys, w['d_vec'], f"{tag}_dscan")
    dxs16, dcw_x, dcb_x = _conv_bwd(pm, C_XS, HP, w['conv_w'][:, :HP], w['conv_b'][:, :HP], dxs, f"{tag}_dconv_x")
    dbc16, dcw_bc, dcb_bc = _conv_bwd(pt, T_BC, BCW, w['conv_w'][:, HP:], w['conv_b'][:, HP:],
                                      jnp.concatenate([dbm, dcm], axis=1), f"{tag}_dconv_bc")
    g['ssd_conv_w'] = _xbc_unpad(jnp.concatenate([dcw_x, dcw_bc], axis=1))
    g['ssd_conv_b'] = _xbc_unpad(jnp.concatenate([dcb_x, dcb_bc], axis=1))
    ddt, dbias, dalog = _dt_bwd(pt, w['dt_bias'], w['a_log'], dda, ddtx, f"{tag}_ddt")
    g['ssd_dt_bias'], g['ssd_a_log'] = dbias[:, :SSD_HEADS], dalog[:, :SSD_HEADS]
    s = x.shape[0]
    dpm = jnp.concatenate([duv, dz, dxs16, dgates], axis=1)
    dpt = jnp.concatenate([dbc16, dckv, dcq, dkr, ddt, jnp.zeros((s, PW_TAIL - T_DT - LANES), BF16)], axis=1)
    g['w_in'] = _split('w_in', _w_in_unpad(_matmul(h, dpm, ta=True, name=f"{tag}_dwin_main"),
                                           _matmul(h, dpt, ta=True, name=f"{tag}_dwin_tail")))
    dx, g['mix_norm'] = _mixer_dx(dpm, dpt, w['w_in_main'], w['w_in_tail'], x, w['mix_norm'], dy, f"{tag}_dx")
    return dx, g, carried


_CONV_ROWS = 32


def _rows_cols(a, lead):
    return a.reshape(a.shape[:lead] + (int(np.prod(a.shape[lead:-1])), a.shape[-1]))


def _shard_views(wts):
    views = []
    for n in SHARDED_ORDER:
        a = _rows_cols(wts[n].astype(BF16), 1)
        if n == 'ssd_conv_w':
            a = jnp.pad(a, ((0, 0), (0, _CONV_ROWS - a.shape[1]), (0, 0)))
        views.append(a)
    return views


def _gathered(names, arrays):
    out = {}
    for n, a in zip(names, arrays):
        shp = _shard_shape(n)
        if n == 'ssd_conv_w':
            a = a[:, :shp[0]]
        out[n] = a.reshape((N_CHIPS,) + shp)
    return out


def _local_step(x, positions, target, weights, small, distributed=True):
    tabs = _rope_tables(positions)
    views = dict(zip(SHARDED_ORDER, weights)) if distributed else None
    plan = [{} for _ in range(DEPTH)]
    if distributed:
        for l in range(DEPTH - 1):
            plan[l].update({host: (names, l + 1) for host, names in (GATHER_HOSTS if l == 0 else GATHER_HOSTS_LATER).items()})
        plan[0].update({host: (names, 0) for host, names in FIRST_HOSTS.items()})
        have = [dict() for _ in range(DEPTH)]
        have[0].update(_gathered(FIRST_NOW, _run_exchange(_gather_exchange([views[n] for n in FIRST_NOW], 0), "gather_first")))
    else:
        have = weights

    def absorb(l, carried):
        for host, arrays in carried.items():
            if host in plan[l]:
                names, layer = plan[l][host]
                have[layer].update(_gathered(names, arrays))

    ws, saved = [], []
    for l in range(DEPTH):
        sides = {host: _gather_exchange([views[n] for n in names], layer) for host, (names, layer) in plan[l].items()}
        w = _layer_weights(have[l], small, l, 'ffn1')
        x, s1, carried = _ffn_fwd(x, w['ffn1_norm'], w['ffn1_w_in'], w['ffn1_w_out'], "ffn1", sides)
        absorb(l, carried)
        w.update(_layer_weights(have[l], small, l, 'mixer'))
        x, s2, carried = _mixer_fwd(x, w, tabs, "mix", sides)
        absorb(l, carried)
        w.update(_layer_weights(have[l], small, l, 'ffn2'))
        x, s3, carried = _ffn_fwd(x, w['ffn2_norm'], w['ffn2_w_in'], w['ffn2_w_out'], "ffn2", sides)
        absorb(l, carried)
        ws.append(w)
        saved.append((s1, s2, s3))
    dy, sq = _loss_head(x, target, "loss_head")
    loss = 0.5 * jnp.sum(sq) / D_MODEL
    grads, reduced, pending = [None] * DEPTH, {}, None

    def chip_sides(sums, hosts):
        return {host: _chip_exchange([sums[n][0] for n in names]) for host, names in hosts.items()}

    def arrivals(carried, hosts):
        return {n: a for host, names in hosts.items() for n, a in zip(names, carried[host])}

    for l in reversed(range(DEPTH)):
        w = ws[l]
        s1, s2, s3 = saved[l]
        sides = {host: _pair_exchange([pending[n] for n in names]) for host, names in PAIR_HOSTS.items()} if pending else {}
        dy, dn2, dwi2, dwo2, carried = _ffn_bwd(dy, s3, w['ffn2_norm'], w['ffn2_w_in'], w['ffn2_w_out'], "ffn2", sides)
        sides = {}
        if pending:
            sums = _pair_sums(pending, arrivals(carried, PAIR_HOSTS))
            sides = chip_sides(sums, REDUCE_HOSTS)
        dy, g, carried = _mixer_bwd(dy, s2, w, tabs, "mix", sides)
        if pending:
            _chip_sums(sums, arrivals(carried, REDUCE_HOSTS), l + 1, reduced)
        g.update(ffn2_norm=dn2, ffn2_w_in=dwi2, ffn2_w_out=dwo2)
        last = distributed and l == 0
        if last:
            early = {n: _rows_cols(g[n], 1) for n in LAST_EARLY}
            after = {}

            def after_dwout(got):
                after['sums'] = _pair_sums(early, dict(zip(LAST_EARLY, got)))
                return chip_sides(after['sums'], LAST_HOSTS)

            dy, dn1, dwi1, dwo1, carried = _ffn_bwd(dy, s1, w['ffn1_norm'], w['ffn1_w_in'], w['ffn1_w_out'], "ffn1",
                                                    {'ffn1_dwout': _pair_exchange([early[n] for n in LAST_EARLY])}, after_dwout)
            _chip_sums(after['sums'], arrivals(carried, LAST_HOSTS), 0, reduced)
        else:
            dy, dn1, dwi1, dwo1, _ = _ffn_bwd(dy, s1, w['ffn1_norm'], w['ffn1_w_in'], w['ffn1_w_out'], "ffn1")
        g.update(ffn1_norm=dn1, ffn1_w_in=dwi1, ffn1_w_out=dwo1)
        grads[l] = g
        if distributed:
            pending = {n: _rows_cols(g[n], 1) for n in REDUCED}
    if distributed:
        _reduce_to_chip({n: pending[n] for n in LAST_LATE}, 0, reduced)
    return loss, dy, grads, reduced


SMALL_PACK = SMALL_ORDER + ['ssd_conv_w']
_SMALL_ROW_TILE = 256


def _pack_small(per_layer_rows, tail=None):
    parts = [per_layer_rows[l][n].reshape(-1).astype(F32) for l in range(DEPTH) for n in SMALL_PACK]
    if tail is not None:
        parts.append(tail.reshape(1))
    flat = jnp.concatenate(parts)
    rows = -(-flat.shape[0] // LANES)
    rows = -(-rows // _SMALL_ROW_TILE) * _SMALL_ROW_TILE
    return jnp.pad(flat, (0, rows * LANES - flat.shape[0])).reshape(rows, LANES)


def _unpack_small(buf, shapes):
    flat = buf.reshape(-1)
    off = 0
    out = {n: [] for n in SMALL_PACK}
    for l in range(DEPTH):
        for n in SMALL_PACK:
            size = int(np.prod(shapes[n]))
            out[n].append(flat[off:off + size].reshape(shapes[n]))
            off += size
    return {n: jnp.stack(v) for n, v in out.items()}


def kernel(x, positions, ffn1_norm, ffn1_w_in, ffn1_w_out, mix_norm, w_in, gm_v_norm, gm_w_s, gm_b_s, mla_q_norm, mla_kv_norm, mla_w_uq, mla_w_ukv, mla_q_gain, mla_k_gain, ssd_conv_w, ssd_conv_b, ssd_dt_bias, ssd_a_log, ssd_d, ssd_norm, w_branch, w_out, ffn2_norm, ffn2_w_in, ffn2_w_out, loss_target, m_ffn1_norm, m_ffn1_w_in, m_ffn1_w_out, m_mix_norm, m_w_in, m_gm_v_norm, m_gm_w_s, m_gm_b_s, m_mla_q_norm, m_mla_kv_norm, m_mla_w_uq, m_mla_w_ukv, m_mla_q_gain, m_mla_k_gain, m_ssd_conv_w, m_ssd_conv_b, m_ssd_dt_bias, m_ssd_a_log, m_ssd_d, m_ssd_norm, m_w_branch, m_w_out, m_ffn2_norm, m_ffn2_w_in, m_ffn2_w_out, v_ffn1_norm, v_ffn1_w_in, v_ffn1_w_out, v_mix_norm, v_w_in, v_gm_v_norm, v_gm_w_s, v_gm_b_s, v_mla_q_norm, v_mla_kv_norm, v_mla_w_uq, v_mla_w_ukv, v_mla_q_gain, v_mla_k_gain, v_ssd_conv_w, v_ssd_conv_b, v_ssd_dt_bias, v_ssd_a_log, v_ssd_d, v_ssd_norm, v_w_branch, v_w_out, v_ffn2_norm, v_ffn2_w_in, v_ffn2_w_out):
    wts = dict(zip(WEIGHTS, (ffn1_norm, ffn1_w_in, ffn1_w_out, mix_norm, w_in, gm_v_norm, gm_w_s, gm_b_s, mla_q_norm, mla_kv_norm,
                             mla_w_uq, mla_w_ukv, mla_q_gain, mla_k_gain, ssd_conv_w, ssd_conv_b, ssd_dt_bias, ssd_a_log, ssd_d,
                             ssd_norm, w_branch, w_out, ffn2_norm, ffn2_w_in, ffn2_w_out)))
    mom = dict(zip(WEIGHTS, (m_ffn1_norm, m_ffn1_w_in, m_ffn1_w_out, m_mix_norm, m_w_in, m_gm_v_norm, m_gm_w_s, m_gm_b_s, m_mla_q_norm,
                             m_mla_kv_norm, m_mla_w_uq, m_mla_w_ukv, m_mla_q_gain, m_mla_k_gain, m_ssd_conv_w, m_ssd_conv_b,
                             m_ssd_dt_bias, m_ssd_a_log, m_ssd_d, m_ssd_norm, m_w_branch, m_w_out, m_ffn2_norm, m_ffn2_w_in,
                             m_ffn2_w_out)))
    var = dict(zip(WEIGHTS, (v_ffn1_norm, v_ffn1_w_in, v_ffn1_w_out, v_mix_norm, v_w_in, v_gm_v_norm, v_gm_w_s, v_gm_b_s, v_mla_q_norm,
                             v_mla_kv_norm, v_mla_w_uq, v_mla_w_ukv, v_mla_q_gain, v_mla_k_gain, v_ssd_conv_w, v_ssd_conv_b,
                             v_ssd_dt_bias, v_ssd_a_log, v_ssd_d, v_ssd_norm, v_w_branch, v_w_out, v_ffn2_norm, v_ffn2_w_in,
                             v_ffn2_w_out)))
    cx, cy, _ = _me()
    mychip = _chip_index(cx, cy)

    small = {n: wts[n] for n in SMALL_ORDER}
    loss_part, dx, grads, reduced = _local_step(x[0], positions[0], loss_target[0], _shard_views(wts), small)
    rows_cols = _rows_cols
    halves = [reduced[n] for n in REDUCED]
    theirs = _run_exchange(_pair_share(halves), "pair_share")
    grad, delta, new_m, new_v = {}, {}, {}, {}
    everyone = _all_exchange(_pack_small(grads, tail=loss_part))
    for n, a, b in zip(REDUCED, halves, theirs):
        shp = wts[n].shape
        outs, carried = _adamw_sharded(rows_cols(wts[n], 1), rows_cols(mom[n], 1), rows_cols(var[n], 1), a, b, f"adamw_{n}",
                                       everyone if n == REDUCED[0] else None)
        if n == REDUCED[0]:
            partials = carried[0]
        grad[n], delta[n], new_m[n], new_v[n] = [o.reshape(shp) for o in outs]
    shapes = {n: wts[n].shape[1:] for n in SMALL_ORDER}
    shapes['ssd_conv_w'] = SHARDED['ssd_conv_w'][0]
    summed = _sum_slots(partials, "small_sum")
    small_g = _unpack_small(summed, shapes)
    loss = summed.reshape(-1)[DEPTH * sum(int(np.prod(shapes[n])) for n in SMALL_PACK)]
    conv_full = small_g.pop('ssd_conv_w')
    shard_cols = _shard_shape('ssd_conv_w')[1]
    small_g['ssd_conv_w'] = lax.dynamic_slice_in_dim(conv_full, mychip * shard_cols, shard_cols, axis=2)
    shapes['ssd_conv_w'] = _shard_shape('ssd_conv_w')

    per_layer = lambda t: [{n: t[n][l] for n in SMALL_PACK} for l in range(DEPTH)]
    d, nm, nv = _adamw(_pack_small(per_layer(wts)), _pack_small(per_layer(small_g)), _pack_small(per_layer(mom)),
                       _pack_small(per_layer(var)), "adamw_small")
    sd, snm, snv = _unpack_small(d, shapes), _unpack_small(nm, shapes), _unpack_small(nv, shapes)
    for n in SMALL_PACK:
        grad[n], delta[n], new_m[n], new_v[n] = small_g[n], sd[n], snm[n], snv[n]
    return (loss, dx[None], *[grad[n] for n in WEIGHTS], *[delta[n] for n in WEIGHTS], *[new_m[n] for n in WEIGHTS],
            *[new_v[n] for n in WEIGHTS])
```

```python
import functools

import numpy as np
import jax
import jax.numpy as jnp
from jax import lax
from jax.experimental import pallas as pl
from jax.experimental.pallas import tpu as pltpu

F32, BF16 = jnp.float32, jnp.bfloat16
MESH = pl.DeviceIdType.MESH

D_MODEL, DEPTH, D_FF, EPS = 1024, 4, 2816, 1e-6
GM_WIDTH, GM_GROUPS, CHUNK = 512, 4, 128
MLA_HEADS, MLA_Q_RANK, MLA_KV_RANK, MLA_NOPE, MLA_ROPE, MLA_V = 8, 384, 256, 64, 32, 64
MLA_QK = MLA_NOPE + MLA_ROPE
ROPE_THETA = 10000.0
SSD_HEADS, SSD_HEAD_DIM, SSD_GROUPS, SSD_STATE, SSD_CONV = 8, 64, 2, 128, 4
SSD_INNER = SSD_HEADS * SSD_HEAD_DIM
IN_COLS = 6312
LANES = 128
ADAM_LR, ADAM_B1, ADAM_B2, ADAM_EPS, ADAM_WD, ADAM_STEP = 0.001, 0.9, 0.999, 1e-08, 0.01, 10

C_UV, C_Z, C_XS, C_G, PW_MAIN = 0, 1024, 2048, 3072, 6144
T_BC, T_CKV, T_CQ, T_KR, T_DT, PW_TAIL = 0, 512, 768, 1152, 1280, 1536
HP = MLA_HEADS * LANES
FC = 2 * D_FF // 4

WEIGHTS = ['ffn1_norm', 'ffn1_w_in', 'ffn1_w_out', 'mix_norm', 'w_in', 'gm_v_norm', 'gm_w_s', 'gm_b_s', 'mla_q_norm',
           'mla_kv_norm', 'mla_w_uq', 'mla_w_ukv', 'mla_q_gain', 'mla_k_gain', 'ssd_conv_w', 'ssd_conv_b', 'ssd_dt_bias',
           'ssd_a_log', 'ssd_d', 'ssd_norm', 'w_branch', 'w_out', 'ffn2_norm', 'ffn2_w_in', 'ffn2_w_out']
SHARDED = {'ffn1_w_in': ((1024, 5632), 1), 'ffn1_w_out': ((2816, 1024), 0), 'w_in': ((1024, 6312), 1),
           'mla_w_uq': ((384, 768), 1), 'mla_w_ukv': ((256, 1024), 1), 'ssd_conv_w': ((4, 1024), 1),
           'w_branch': ((3, 512, 1024), 2), 'w_out': ((1024, 1024), 0), 'ffn2_w_in': ((1024, 5632), 1),
           'ffn2_w_out': ((2816, 1024), 0)}
SHARDED_ORDER = [n for n in WEIGHTS if n in SHARDED]
SMALL_ORDER = [n for n in WEIGHTS if n not in SHARDED]
REDUCED = [n for n in SHARDED_ORDER if n != 'ssd_conv_w']
N_CHIPS = 4


def _shard_shape(name):
    shape, ax = SHARDED[name]
    return tuple(d // N_CHIPS if i == ax else d for i, d in enumerate(shape))


def _pick(dim, target):
    if dim <= target:
        return dim
    t = (target // LANES) * LANES
    while t >= LANES:
        if dim % t == 0:
            return t
        t -= LANES
    return dim


def _sigmoid(x):
    return 1.0 / (1.0 + jnp.exp(-x))


def _params(*sem):
    return pltpu.CompilerParams(dimension_semantics=sem, vmem_limit_bytes=56 * 1024 * 1024)


def _matmul(a, b, *, ta=False, tb=False, out_dtype=F32, scale=1.0, res=None, name, side=None):
    if ta:
        k_dim, m_dim = a.shape
    else:
        m_dim, k_dim = a.shape
    if tb:
        n_dim, k2 = b.shape
    else:
        k2, n_dim = b.shape
    assert k_dim == k2, (a.shape, b.shape, ta, tb)
    tm, tn, tk = _pick(m_dim, 1024), _pick(n_dim, 1024), _pick(k_dim, 1024)
    nk = k_dim // tk
    dn = (((0 if ta else 1,), (1 if tb else 0,)), ((), ()))

    def body(*refs):
        if res is not None:
            a_ref, b_ref, r_ref, o_ref, acc = refs
        else:
            a_ref, b_ref, o_ref, acc = refs
        k = pl.program_id(2)

        @pl.when(k == 0)
        def _():
            acc[...] = jnp.zeros_like(acc)

        acc[...] += lax.dot_general(a_ref[...].astype(BF16), b_ref[...].astype(BF16), dn, preferred_element_type=F32)

        @pl.when(k == nk - 1)
        def _():
            r = acc[...]
            if scale != 1.0:
                r = r * scale
            if res is not None:
                r = r + r_ref[...]
            o_ref[...] = r.astype(out_dtype)

    a_spec = pl.BlockSpec((tk, tm), lambda j, i, k: (k, i)) if ta else pl.BlockSpec((tm, tk), lambda j, i, k: (i, k))
    b_spec = pl.BlockSpec((tn, tk), lambda j, i, k: (j, k)) if tb else pl.BlockSpec((tk, tn), lambda j, i, k: (k, j))
    in_specs = [a_spec, b_spec]
    args = [a, b]
    if res is not None:
        in_specs.append(pl.BlockSpec((tm, tn), lambda j, i, k: (i, j)))
        args.append(res)
    (out,), carried = _call(
        body, name=name, grid=(n_dim // tn, m_dim // tm, nk), in_specs=in_specs,
        out_specs=[pl.BlockSpec((tm, tn), lambda j, i, k: (i, j))],
        out_shape=[jax.ShapeDtypeStruct((m_dim, n_dim), out_dtype)],
        scratch_shapes=[pltpu.VMEM((tm, tn), F32)], args=args, semantics=("parallel", "parallel", "arbitrary"), side=side)
    return out if side is None else (out, carried)


_NT = (((1,), (1,)), ((), ()))
_TN = (((0,), (0,)), ((), ()))


def _resident(shape):
    return pl.BlockSpec(shape, lambda *_: tuple(0 for _ in shape), pipeline_mode=pl.Buffered(1))


def _ffn_in(x, gain, w4, name, side=None):
    s, d = x.shape
    tm = _pick(s, 512)

    def body(x_ref, g_ref, w_ref, h_ref, gate_ref, up_ref, act_ref):
        xv = x_ref[...]
        r = lax.rsqrt(jnp.mean(xv * xv, axis=-1, keepdims=True) + EPS)
        h = (xv * r * g_ref[...]).astype(BF16)
        h_ref[...] = h
        for j in range(2):
            g16 = jnp.dot(h, w_ref[j], preferred_element_type=F32).astype(BF16)
            u16 = jnp.dot(h, w_ref[j + 2], preferred_element_type=F32).astype(BF16)
            gate_ref[j] = g16
            up_ref[j] = u16
            gf, uf = g16.astype(F32), u16.astype(F32)
            act_ref[j] = (gf * _sigmoid(gf) * uf).astype(BF16)

    half = pl.BlockSpec((2, tm, FC), lambda i: (0, i, 0))
    return _call(
        body, name=name, grid=(s // tm,),
        in_specs=[pl.BlockSpec((tm, d), lambda i: (i, 0)), pl.BlockSpec((1, d), lambda i: (0, 0)), _resident((4, d, FC))],
        out_specs=[pl.BlockSpec((tm, d), lambda i: (i, 0)), half, half, half],
        out_shape=[jax.ShapeDtypeStruct((s, d), BF16)] + [jax.ShapeDtypeStruct((2, s, FC), BF16)] * 3,
        scratch_shapes=[], args=(x, gain, w4), semantics=("parallel",), side=side)


def _ffn_out(act, w_out, x, name, side=None):
    s, d = x.shape
    tm = _pick(s, 512)

    def body(a_ref, w_ref, x_ref, o_ref):
        acc = jnp.dot(a_ref[0], w_ref[0:FC, :], preferred_element_type=F32)
        acc = acc + jnp.dot(a_ref[1], w_ref[FC:2 * FC, :], preferred_element_type=F32)
        o_ref[...] = x_ref[...] + 0.5 * acc

    row = pl.BlockSpec((tm, d), lambda i: (i, 0))
    (out,), carried = _call(
        body, name=name, grid=(s // tm,),
        in_specs=[pl.BlockSpec((2, tm, FC), lambda i: (0, i, 0)), _resident((2 * FC, d)), row], out_specs=[row],
        out_shape=[jax.ShapeDtypeStruct((s, d), F32)], scratch_shapes=[], args=(act, w_out, x), semantics=("parallel",),
        side=side)
    return out, carried


def _ffn_dact(dy, w_out, gate, up, name, side=None):
    s, d = dy.shape
    tm = _pick(s, 512)

    def body(dy_ref, w_ref, g_ref, u_ref, o_ref):
        dy16 = dy_ref[...].astype(BF16)
        for j in range(2):
            dact = 0.5 * lax.dot_general(dy16, w_ref[j * FC:(j + 1) * FC, :], _NT, preferred_element_type=F32)
            g, u = g_ref[j].astype(F32), u_ref[j].astype(F32)
            sg = _sigmoid(g)
            o_ref[j] = (dact * u * (sg * (1.0 + g * (1.0 - sg)))).astype(BF16)
            o_ref[j + 2] = (dact * g * sg).astype(BF16)

    half = pl.BlockSpec((2, tm, FC), lambda i: (0, i, 0))
    (out,), carried = _call(
        body, name=name, grid=(s // tm,),
        in_specs=[pl.BlockSpec((tm, d), lambda i: (i, 0)), _resident((2 * FC, d)), half, half],
        out_specs=[pl.BlockSpec((4, tm, FC), lambda i: (0, i, 0))],
        out_shape=[jax.ShapeDtypeStruct((4, s, FC), BF16)], scratch_shapes=[], args=(dy, w_out, gate, up),
        semantics=("parallel",), side=side)
    return out, carried


def _ffn_dwout(act, dy, name, side=None):
    s, d = dy.shape
    tk = _pick(s, 1024)
    nk = s // tk

    def body(a_ref, dy_ref, o_ref):
        k = pl.program_id(1)

        @pl.when(k == 0)
        def _():
            o_ref[...] = jnp.zeros_like(o_ref)

        o_ref[...] += lax.dot_general(a_ref[...], dy_ref[...].astype(BF16), _TN, preferred_element_type=F32)

        @pl.when(k == nk - 1)
        def _():
            o_ref[...] = 0.5 * o_ref[...]

    (out,), carried = _call(
        body, name=name, grid=(2, nk),
        in_specs=[pl.BlockSpec((None, tk, FC), lambda j, k: (j, k, 0)), pl.BlockSpec((tk, d), lambda j, k: (k, 0))],
        out_specs=[pl.BlockSpec((FC, d), lambda j, k: (j, 0))], out_shape=[jax.ShapeDtypeStruct((2 * FC, d), F32)],
        scratch_shapes=[], args=(act, dy), semantics=("parallel", "arbitrary"), side=side)
    return out, carried


def _ffn_dwin(h, da, name, side=None):
    s, d = h.shape
    tk = _pick(s, 1024)

    def body(h_ref, da_ref, o_ref):
        @pl.when(pl.program_id(1) == 0)
        def _():
            o_ref[...] = jnp.zeros_like(o_ref)

        o_ref[...] += lax.dot_general(h_ref[...], da_ref[...], _TN, preferred_element_type=F32)

    (out,), carried = _call(
        body, name=name, grid=(4, s // tk),
        in_specs=[pl.BlockSpec((tk, d), lambda j, k: (k, 0)), pl.BlockSpec((None, tk, FC), lambda j, k: (j, k, 0))],
        out_specs=[pl.BlockSpec((None, d, FC), lambda j, k: (j, 0, 0))], out_shape=[jax.ShapeDtypeStruct((4, d, FC), F32)],
        scratch_shapes=[], args=(h, da), semantics=("parallel", "arbitrary"), side=side)
    return out, carried


def _ffn_dx(da, w4, x, gain, dy, name, side=None):
    s, d = x.shape
    tm = _pick(s, 512)

    def body(da_ref, w_ref, x_ref, g_ref, dy_ref, dx_ref, dg_ref):
        @pl.when(pl.program_id(0) == 0)
        def _():
            dg_ref[...] = jnp.zeros_like(dg_ref)

        dh = jnp.zeros((tm, d), F32)
        for j in range(4):
            dh = dh + lax.dot_general(da_ref[j], w_ref[j], _NT, preferred_element_type=F32)
        xv = x_ref[...]
        r = lax.rsqrt(jnp.mean(xv * xv, axis=-1, keepdims=True) + EPS)
        u = dh * g_ref[...]
        dx_ref[...] = dy_ref[...] + r * u - xv * (r * r * r) * jnp.mean(xv * u, axis=-1, keepdims=True)
        dg_ref[...] += jnp.sum(dh * xv * r, axis=0, keepdims=True)

    row = pl.BlockSpec((tm, d), lambda i: (i, 0))
    vec = pl.BlockSpec((1, d), lambda i: (0, 0))
    return _call(
        body, name=name, grid=(s // tm,),
        in_specs=[pl.BlockSpec((4, tm, FC), lambda i: (0, i, 0)), _resident((4, d, FC)), row, vec, row],
        out_specs=[row, vec], out_shape=[jax.ShapeDtypeStruct((s, d), F32), jax.ShapeDtypeStruct((1, d), F32)],
        scratch_shapes=[], args=(da, w4, x, gain, dy), semantics=("arbitrary",), side=side)


def _mixer_proj(x, gain, w_main, w_tail, name, side=None):
    s, d = x.shape
    tm = _pick(s, 512)

    def body(x_ref, g_ref, wm_ref, wt_ref, h_ref, pm_ref, pt_ref):
        xv = x_ref[...]
        r = lax.rsqrt(jnp.mean(xv * xv, axis=-1, keepdims=True) + EPS)
        h = (xv * r * g_ref[...]).astype(BF16)
        h_ref[...] = h
        pm_ref[...] = jnp.dot(h, wm_ref[...], preferred_element_type=F32).astype(BF16)
        pt_ref[...] = jnp.dot(h, wt_ref[...], preferred_element_type=F32)

    row = lambda width: pl.BlockSpec((tm, width), lambda i: (i, 0))
    return _call(
        body, name=name, grid=(s // tm,),
        in_specs=[row(d), pl.BlockSpec((1, d), lambda i: (0, 0)), _resident((d, PW_MAIN)), _resident((d, PW_TAIL))],
        out_specs=[row(d), row(PW_MAIN), row(PW_TAIL)],
        out_shape=[jax.ShapeDtypeStruct((s, d), BF16), jax.ShapeDtypeStruct((s, PW_MAIN), BF16),
                   jax.ShapeDtypeStruct((s, PW_TAIL), F32)],
        scratch_shapes=[], args=(x, gain, w_main, w_tail), semantics=("parallel",), side=side)


def _mixer_dx(dpm, dpt, w_main, w_tail, x, gain, dy, name):
    s, d = x.shape
    tm = _pick(s, 512)

    def body(dpm_ref, dpt_ref, wm_ref, wt_ref, x_ref, g_ref, dy_ref, dx_ref, dg_ref):
        @pl.when(pl.program_id(0) == 0)
        def _():
            dg_ref[...] = jnp.zeros_like(dg_ref)

        dh = lax.dot_general(dpm_ref[...], wm_ref[...], _NT, preferred_element_type=F32)
        dh = dh + lax.dot_general(dpt_ref[...], wt_ref[...], _NT, preferred_element_type=F32)
        xv = x_ref[...]
        r = lax.rsqrt(jnp.mean(xv * xv, axis=-1, keepdims=True) + EPS)
        u = dh * g_ref[...]
        dx_ref[...] = dy_ref[...] + r * u - xv * (r * r * r) * jnp.mean(xv * u, axis=-1, keepdims=True)
        dg_ref[...] += jnp.sum(dh * xv * r, axis=0, keepdims=True)

    row = lambda width: pl.BlockSpec((tm, width), lambda i: (i, 0))
    vec = pl.BlockSpec((1, d), lambda i: (0, 0))
    return pl.pallas_call(
        body, name=name, grid=(s // tm,),
        in_specs=[row(PW_MAIN), row(PW_TAIL), _resident((d, PW_MAIN)), _resident((d, PW_TAIL)), row(d), vec, row(d)],
        out_specs=[row(d), vec], out_shape=[jax.ShapeDtypeStruct((s, d), F32), jax.ShapeDtypeStruct((1, d), F32)],
        compiler_params=_params("arbitrary"))(dpm, dpt, w_main, w_tail, x, gain, dy)


_INV_SQRT2 = 0.7071067811865476
_INV_SQRT2PI = 0.3989422804014327


def _gelu(x):
    return 0.5 * x * (1.0 + lax.erf(x * _INV_SQRT2))


def _gelu_grad(x):
    return 0.5 * (1.0 + lax.erf(x * _INV_SQRT2)) + x * jnp.exp(-0.5 * x * x) * _INV_SQRT2PI


def _tril_mask():
    r = lax.broadcasted_iota(jnp.int32, (CHUNK, CHUNK), 0)
    c = lax.broadcasted_iota(jnp.int32, (CHUNK, CHUNK), 1)
    return r >= c


def _gmlp_fwd(p, v_gain, w_s, b_full, name):
    s = p.shape[0]
    tm = _pick(s, 512)
    nch = tm // CHUNK

    def body(uv_ref, g_ref, w_ref, b_ref, o_ref):
        gel = _gelu(uv_ref[...].astype(F32))
        u, v = gel[:, :GM_WIDTH], gel[:, GM_WIDTH:]
        r = lax.rsqrt(jnp.mean(v * v, axis=-1, keepdims=True) + EPS)
        vn = (v * r * g_ref[...]).astype(BF16)
        mask = _tril_mask()
        for g in range(GM_GROUPS):
            wm = jnp.where(mask, w_ref[g], 0.0).astype(BF16)
            for c in range(nch):
                rs, cs = slice(c * CHUNK, (c + 1) * CHUNK), slice(g * LANES, (g + 1) * LANES)
                sp = jnp.dot(wm, vn[rs, cs], preferred_element_type=F32) + b_ref[g]
                o_ref[rs, cs] = (u[rs, cs] * sp).astype(BF16)

    full3 = pl.BlockSpec((GM_GROUPS, CHUNK, CHUNK), lambda i: (0, 0, 0))
    return pl.pallas_call(
        body, name=name, grid=(s // tm,),
        in_specs=[pl.BlockSpec((tm, 2 * GM_WIDTH), lambda i: (i, C_UV // (2 * GM_WIDTH))),
                  pl.BlockSpec((1, GM_WIDTH), lambda i: (0, 0)), full3, full3],
        out_specs=pl.BlockSpec((tm, GM_WIDTH), lambda i: (i, 0)),
        out_shape=jax.ShapeDtypeStruct((s, GM_WIDTH), BF16), compiler_params=_params("parallel"))(p, v_gain, w_s, b_full)


def _gmlp_bwd(p, v_gain, w_s, b_full, dy, name):
    s = p.shape[0]
    tm = _pick(s, 512)
    nch = tm // CHUNK
    nsteps = s // tm

    def body(uv_ref, g_ref, w_ref, b_ref, dy_ref, duv_ref, dg_ref, dw_ref, db_ref, dvn_s, dbacc):
        step = pl.program_id(0)

        @pl.when(step == 0)
        def _():
            dg_ref[...] = jnp.zeros_like(dg_ref)
            dw_ref[...] = jnp.zeros_like(dw_ref)
            dbacc[...] = jnp.zeros_like(dbacc)

        uv = uv_ref[...].astype(F32)
        gel = _gelu(uv)
        u, v = gel[:, :GM_WIDTH], gel[:, GM_WIDTH:]
        r = lax.rsqrt(jnp.mean(v * v, axis=-1, keepdims=True) + EPS)
        gain = g_ref[...]
        vn32 = v * r * gain
        vn = vn32.astype(BF16)
        dy = dy_ref[...].astype(F32)
        mask = _tril_mask()
        for g in range(GM_GROUPS):
            wm = jnp.where(mask, w_ref[g], 0.0).astype(BF16)
            dwg = jnp.zeros((CHUNK, CHUNK), F32)
            dbg = jnp.zeros((CHUNK, LANES), F32)
            for c in range(nch):
                rs, cs = slice(c * CHUNK, (c + 1) * CHUNK), slice(g * LANES, (g + 1) * LANES)
                sp = jnp.dot(wm, vn[rs, cs], preferred_element_type=F32) + b_ref[g]
                dyc = dy[rs, cs]
                dsp = dyc * u[rs, cs]
                dsp16 = dsp.astype(BF16)
                duv_ref[rs, cs] = (dyc * sp * _gelu_grad(uv[rs, cs])).astype(BF16)
                dvn_s[rs, cs] = lax.dot_general(wm, dsp16, (((0,), (0,)), ((), ())), preferred_element_type=F32)
                dwg = dwg + lax.dot_general(dsp16, vn[rs, cs], (((1,), (1,)), ((), ())), preferred_element_type=F32)
                dbg = dbg + dsp
            dw_ref[g] += jnp.where(mask, dwg, 0.0)
            dbacc[:, g * LANES:(g + 1) * LANES] += dbg
        dvn = dvn_s[...]
        uu = dvn * gain
        dv = r * uu - v * (r * r * r) * jnp.mean(v * uu, axis=-1, keepdims=True)
        duv_ref[:, GM_WIDTH:] = (dv * _gelu_grad(uv[:, GM_WIDTH:])).astype(BF16)
        dg_ref[...] += jnp.sum(dvn * v * r, axis=0, keepdims=True)

        @pl.when(step == nsteps - 1)
        def _():
            for g in range(GM_GROUPS):
                db_ref[:, g:g + 1] = jnp.sum(dbacc[:, g * LANES:(g + 1) * LANES], axis=1, keepdims=True)

    full3 = pl.BlockSpec((GM_GROUPS, CHUNK, CHUNK), lambda i: (0, 0, 0))
    return pl.pallas_call(
        body, name=name, grid=(nsteps,),
        in_specs=[pl.BlockSpec((tm, 2 * GM_WIDTH), lambda i: (i, C_UV // (2 * GM_WIDTH))),
                  pl.BlockSpec((1, GM_WIDTH), lambda i: (0, 0)), full3, full3,
                  pl.BlockSpec((tm, GM_WIDTH), lambda i: (i, 0))],
        out_specs=[pl.BlockSpec((tm, 2 * GM_WIDTH), lambda i: (i, 0)), pl.BlockSpec((1, GM_WIDTH), lambda i: (0, 0)),
                   full3, pl.BlockSpec((CHUNK, GM_GROUPS), lambda i: (0, 0))],
        out_shape=[jax.ShapeDtypeStruct((s, 2 * GM_WIDTH), BF16), jax.ShapeDtypeStruct((1, GM_WIDTH), F32),
                   jax.ShapeDtypeStruct((GM_GROUPS, CHUNK, CHUNK), F32), jax.ShapeDtypeStruct((CHUNK, GM_GROUPS), F32)],
        scratch_shapes=[pltpu.VMEM((tm, GM_WIDTH), F32), pltpu.VMEM((CHUNK, GM_WIDTH), F32)],
        compiler_params=_params("arbitrary"))(p, v_gain, w_s, b_full, dy)


def _rope(x, ct, s1, s2):
    return x * ct + pltpu.roll(x, LANES - MLA_ROPE // 2, 1) * s1 + pltpu.roll(x, MLA_ROPE // 2, 1) * s2


def _rope_bwd(d, ct, s1, s2):
    return d * ct + pltpu.roll(d * s1, MLA_ROPE // 2, 1) + pltpu.roll(d * s2, LANES - MLA_ROPE // 2, 1)


def _head_norm(x, gain):
    r = lax.rsqrt(jnp.sum(x * x, axis=-1, keepdims=True) * (1.0 / MLA_QK) + EPS)
    return x * r * gain, r


def _head_norm_bwd(x, r, gain, d):
    u = d * gain
    return r * u - x * (r * r * r) * (jnp.sum(x * u, axis=-1, keepdims=True) * (1.0 / MLA_QK))


def _mla_specs(tm):
    cq = pl.BlockSpec((tm, MLA_Q_RANK), lambda i: (i, T_CQ // MLA_Q_RANK))
    ckv = pl.BlockSpec((tm, MLA_KV_RANK), lambda i: (i, T_CKV // MLA_KV_RANK))
    kr = pl.BlockSpec((tm, LANES), lambda i: (i, T_KR // LANES))
    tab = pl.BlockSpec((tm, LANES), lambda i: (i, 0))
    return cq, ckv, kr, tab


def _const(shape):
    return pl.BlockSpec(shape, lambda i: tuple(0 for _ in shape))


def _mla_pre_fwd(p, tabs, qn_g, kvn_g, wuq, wkv, gq, gk, name):
    s = p.shape[0]
    tm = _pick(s, 256)
    ct, s1, s2 = tabs

    def body(cq_ref, ckv_ref, kr_ref, ct_ref, s1_ref, s2_ref, qg_ref, kvg_ref, wuq_ref, wkv_ref, gq_ref, gk_ref,
             q_ref, k_ref, v_ref):
        cq, ckv, kr = cq_ref[...], ckv_ref[...], kr_ref[...]
        ctv, s1v, s2v = ct_ref[...], s1_ref[...], s2_ref[...]
        rq = lax.rsqrt(jnp.mean(cq * cq, axis=-1, keepdims=True) + EPS)
        q = jnp.dot((cq * rq * qg_ref[...]).astype(BF16), wuq_ref[...], preferred_element_type=F32)
        rk = lax.rsqrt(jnp.mean(ckv * ckv, axis=-1, keepdims=True) + EPS)
        kv = jnp.dot((ckv * rk * kvg_ref[...]).astype(BF16), wkv_ref[...], preferred_element_type=F32)
        v_ref[...] = kv[:, HP:].astype(BF16)
        for h in range(MLA_HEADS):
            hs = slice(h * LANES, (h + 1) * LANES)
            qh, _ = _head_norm(q[:, hs], gq_ref[...])
            q_ref[:, hs] = (_rope(qh, ctv, s1v, s2v) * _Q_SCALE).astype(BF16)
            kh, _ = _head_norm(kv[:, hs] + kr, gk_ref[...])
            k_ref[:, hs] = _rope(kh, ctv, s1v, s2v).astype(BF16)

    cq_s, ckv_s, kr_s, tab_s = _mla_specs(tm)
    out = pl.BlockSpec((tm, HP), lambda i: (i, 0))
    return pl.pallas_call(
        body, name=name, grid=(s // tm,),
        in_specs=[cq_s, ckv_s, kr_s, tab_s, tab_s, tab_s, _const((1, MLA_Q_RANK)), _const((1, MLA_KV_RANK)),
                  _const((MLA_Q_RANK, HP)), _const((MLA_KV_RANK, 2 * HP)), _const((1, LANES)), _const((1, LANES))],
        out_specs=[out, out, out], out_shape=[jax.ShapeDtypeStruct((s, HP), BF16)] * 3,
        compiler_params=_params("parallel"))(p, p, p, ct, s1, s2, qn_g, kvn_g, wuq, wkv, gq, gk)


def _mla_pre_bwd(p, tabs, qn_g, kvn_g, wuq, wkv, gq, gk, dq, dk, dv, name, side=None):
    s = p.shape[0]
    tm = _pick(s, 256)
    ct, s1, s2 = tabs

    def body(cq_ref, ckv_ref, kr_ref, ct_ref, s1_ref, s2_ref, qg_ref, kvg_ref, wuq_ref, wkv_ref, gq_ref, gk_ref,
             dq_ref, dk_ref, dv_ref, dcq_ref, dckv_ref, dkr_ref, dwuq_ref, dwkv_ref, dqg_ref, dkvg_ref, dgq_ref, dgk_ref,
             dqp, dkvp):
        @pl.when(pl.program_id(0) == 0)
        def _():
            for ref in (dwuq_ref, dwkv_ref, dqg_ref, dkvg_ref, dgq_ref, dgk_ref):
                ref[...] = jnp.zeros_like(ref)

        cq, ckv, kr = cq_ref[...], ckv_ref[...], kr_ref[...]
        ctv, s1v, s2v = ct_ref[...], s1_ref[...], s2_ref[...]
        rq = lax.rsqrt(jnp.mean(cq * cq, axis=-1, keepdims=True) + EPS)
        qn = (cq * rq * qg_ref[...]).astype(BF16)
        q = jnp.dot(qn, wuq_ref[...], preferred_element_type=F32)
        rk = lax.rsqrt(jnp.mean(ckv * ckv, axis=-1, keepdims=True) + EPS)
        kvn = (ckv * rk * kvg_ref[...]).astype(BF16)
        kv = jnp.dot(kvn, wkv_ref[...], preferred_element_type=F32)
        gqv, gkv = gq_ref[...], gk_ref[...]
        dgq = jnp.zeros((1, LANES), F32)
        dgk = jnp.zeros((1, LANES), F32)
        dkr = jnp.zeros((tm, LANES), F32)
        for h in range(MLA_HEADS):
            hs = slice(h * LANES, (h + 1) * LANES)
            xq = q[:, hs]
            _, r = _head_norm(xq, gqv)
            d = _rope_bwd(dq_ref[:, hs].astype(F32), ctv, s1v, s2v)
            dgq = dgq + jnp.sum(d * xq * r, axis=0, keepdims=True)
            dqp[:, hs] = _head_norm_bwd(xq, r, gqv, d)
            xk = kv[:, hs] + kr
            _, r = _head_norm(xk, gkv)
            d = _rope_bwd(dk_ref[:, hs].astype(F32), ctv, s1v, s2v)
            dgk = dgk + jnp.sum(d * xk * r, axis=0, keepdims=True)
            dxk = _head_norm_bwd(xk, r, gkv, d)
            dkvp[:, hs] = dxk
            dkr = dkr + dxk
        dkvp[:, HP:] = dv_ref[...].astype(F32)
        dgq_ref[...] += dgq
        dgk_ref[...] += dgk
        dkr_ref[...] = dkr.astype(BF16)
        tn = (((0,), (0,)), ((), ()))
        nt = (((1,), (1,)), ((), ()))
        dq16 = dqp[...].astype(BF16)
        dwuq_ref[...] += lax.dot_general(qn, dq16, tn, preferred_element_type=F32)
        dqn = lax.dot_general(dq16, wuq_ref[...], nt, preferred_element_type=F32)
        dqg_ref[...] += jnp.sum(dqn * cq * rq, axis=0, keepdims=True)
        u = dqn * qg_ref[...]
        dcq_ref[...] = (rq * u - cq * (rq * rq * rq) * jnp.mean(cq * u, axis=-1, keepdims=True)).astype(BF16)
        dkv16 = dkvp[...].astype(BF16)
        dwkv_ref[...] += lax.dot_general(kvn, dkv16, tn, preferred_element_type=F32)
        dkvn = lax.dot_general(dkv16, wkv_ref[...], nt, preferred_element_type=F32)
        dkvg_ref[...] += jnp.sum(dkvn * ckv * rk, axis=0, keepdims=True)
        u = dkvn * kvg_ref[...]
        dckv_ref[...] = (rk * u - ckv * (rk * rk * rk) * jnp.mean(ckv * u, axis=-1, keepdims=True)).astype(BF16)

    cq_s, ckv_s, kr_s, tab_s = _mla_specs(tm)
    hd = pl.BlockSpec((tm, HP), lambda i: (i, 0))
    return _call(
        body, name=name, grid=(s // tm,),
        in_specs=[cq_s, ckv_s, kr_s, tab_s, tab_s, tab_s, _const((1, MLA_Q_RANK)), _const((1, MLA_KV_RANK)),
                  _const((MLA_Q_RANK, HP)), _const((MLA_KV_RANK, 2 * HP)), _const((1, LANES)), _const((1, LANES)),
                  hd, hd, hd],
        out_specs=[pl.BlockSpec((tm, MLA_Q_RANK), lambda i: (i, 0)), pl.BlockSpec((tm, MLA_KV_RANK), lambda i: (i, 0)),
                   pl.BlockSpec((tm, LANES), lambda i: (i, 0)), _const((MLA_Q_RANK, HP)), _const((MLA_KV_RANK, 2 * HP)),
                   _const((1, MLA_Q_RANK)), _const((1, MLA_KV_RANK)), _const((1, LANES)), _const((1, LANES))],
        out_shape=[jax.ShapeDtypeStruct((s, MLA_Q_RANK), BF16), jax.ShapeDtypeStruct((s, MLA_KV_RANK), BF16),
                   jax.ShapeDtypeStruct((s, LANES), BF16), jax.ShapeDtypeStruct((MLA_Q_RANK, HP), F32),
                   jax.ShapeDtypeStruct((MLA_KV_RANK, 2 * HP), F32), jax.ShapeDtypeStruct((1, MLA_Q_RANK), F32),
                   jax.ShapeDtypeStruct((1, MLA_KV_RANK), F32), jax.ShapeDtypeStruct((1, LANES), F32),
                   jax.ShapeDtypeStruct((1, LANES), F32)],
        scratch_shapes=[pltpu.VMEM((tm, HP), F32), pltpu.VMEM((tm, 2 * HP), F32)],
        args=(p, p, p, ct, s1, s2, qn_g, kvn_g, wuq, wkv, gq, gk, dq, dk, dv), semantics=("arbitrary",), side=side)


_ATT_SCALE = MLA_QK ** -0.5
_LOG2E = 1.4426950408889634
_Q_SCALE = _ATT_SCALE * _LOG2E
ATT_BLOCK = 1024
_NEG = -1e30
_NT = (((1,), (1,)), ((), ()))
_TN = (((0,), (0,)), ((), ()))


def _tri_rows(step, n):
    i = step * 0
    for m in range(1, n):
        i = i + (step >= m * (m + 1) // 2).astype(jnp.int32)
    return i, step - i * (i + 1) // 2


def _tri_cols(step, n):
    j = step * 0
    for m in range(1, n):
        j = j + (step >= m * n - m * (m - 1) // 2).astype(jnp.int32)
    return j, j + step - (j * n - j * (j - 1) // 2)


def _diag_mask(t):
    return lax.broadcasted_iota(jnp.int32, (t, t), 0) <= lax.broadcasted_iota(jnp.int32, (t, t), 1)


def _attn_fwd(q, k, v, name, side=None):
    s = q.shape[0]
    t = _pick(s, ATT_BLOCK)
    n = s // t

    def body(q_ref, k_ref, v_ref, o_ref, lse_ref, m_s, l_s, acc):
        i, j = _tri_rows(pl.program_id(1), n)

        @pl.when(j == 0)
        def _():
            m_s[...] = jnp.full_like(m_s, _NEG)
            l_s[...] = jnp.zeros_like(l_s)
            acc[...] = jnp.zeros_like(acc)

        def step(diagonal):
            sc = lax.dot_general(k_ref[...], q_ref[...], _NT, preferred_element_type=F32)
            if diagonal:
                sc = jnp.where(_diag_mask(t), sc, _NEG)
            m_new = jnp.maximum(m_s[...], jnp.max(sc, axis=0, keepdims=True))
            alpha = jnp.exp2(m_s[...] - m_new)
            pr = jnp.exp2(sc - m_new)
            l_s[...] = alpha * l_s[...] + jnp.sum(pr, axis=0, keepdims=True)
            acc[...] = alpha * acc[...] + lax.dot_general(v_ref[...], pr.astype(BF16), _TN, preferred_element_type=F32)
            m_s[...] = m_new

        @pl.when(j < i)
        def _():
            step(False)

        @pl.when(j == i)
        def _():
            step(True)
            o_ref[...] = (acc[...] / l_s[...]).T.astype(BF16)
            lse_ref[...] = m_s[...] + jnp.log2(l_s[...])

    qs = pl.BlockSpec((t, LANES), lambda h, p: (_tri_rows(p, n)[0], h))
    ks = pl.BlockSpec((t, LANES), lambda h, p: (_tri_rows(p, n)[1], h))
    return _call(
        body, name=name, grid=(MLA_HEADS, n * (n + 1) // 2), in_specs=[qs, ks, ks],
        out_specs=[qs, pl.BlockSpec((None, 1, t), lambda h, p: (h, 0, _tri_rows(p, n)[0]))],
        out_shape=[jax.ShapeDtypeStruct((s, HP), BF16), jax.ShapeDtypeStruct((MLA_HEADS, 1, s), F32)],
        scratch_shapes=[pltpu.VMEM((1, t), F32), pltpu.VMEM((1, t), F32), pltpu.VMEM((LANES, t), F32)],
        args=(q, k, v), semantics=("parallel", "arbitrary"), side=side)


def _attn_bwd_dq(q, k, v, o, lse, do, name, side=None):
    s = q.shape[0]
    t = _pick(s, ATT_BLOCK)
    n = s // t

    def body(q_ref, k_ref, v_ref, o_ref, lse_ref, do_ref, dq_ref, dl_ref, acc, dl_s):
        i, j = _tri_rows(pl.program_id(1), n)

        @pl.when(j == 0)
        def _():
            acc[...] = jnp.zeros_like(acc)
            dl_s[...] = jnp.sum((do_ref[...].astype(F32) * o_ref[...].astype(F32)).T, axis=0, keepdims=True)

        def step(diagonal):
            sc = lax.dot_general(k_ref[...], q_ref[...], _NT, preferred_element_type=F32)
            if diagonal:
                sc = jnp.where(_diag_mask(t), sc, _NEG)
            pr = jnp.exp2(sc - lse_ref[...])
            dp = lax.dot_general(v_ref[...], do_ref[...].astype(BF16), _NT, preferred_element_type=F32)
            ds = (pr * (dp - dl_s[...])).astype(BF16)
            acc[...] += lax.dot_general(k_ref[...], ds, _TN, preferred_element_type=F32)

        @pl.when(j < i)
        def _():
            step(False)

        @pl.when(j == i)
        def _():
            step(True)
            dq_ref[...] = (acc[...] * _ATT_SCALE).T.astype(BF16)
            dl_ref[...] = dl_s[...]

    qs = pl.BlockSpec((t, LANES), lambda h, p: (_tri_rows(p, n)[0], h))
    ks = pl.BlockSpec((t, LANES), lambda h, p: (_tri_rows(p, n)[1], h))
    ls = pl.BlockSpec((None, 1, t), lambda h, p: (h, 0, _tri_rows(p, n)[0]))
    return _call(
        body, name=name, grid=(MLA_HEADS, n * (n + 1) // 2), in_specs=[qs, ks, ks, qs, ls, qs], out_specs=[qs, ls],
        out_shape=[jax.ShapeDtypeStruct((s, HP), BF16), jax.ShapeDtypeStruct((MLA_HEADS, 1, s), F32)],
        scratch_shapes=[pltpu.VMEM((LANES, t), F32), pltpu.VMEM((1, t), F32)],
        args=(q, k, v, o, lse, do), semantics=("parallel", "arbitrary"), side=side)


def _attn_bwd_dkv(q, k, v, lse, delta, do, name, side=None):
    s = q.shape[0]
    t = _pick(s, ATT_BLOCK)
    n = s // t

    def body(q_ref, k_ref, v_ref, lse_ref, dl_ref, do_ref, dk_ref, dv_ref, dk_acc, dv_acc):
        j, i = _tri_cols(pl.program_id(1), n)

        def step(diagonal):
            sc = lax.dot_general(k_ref[...], q_ref[...], _NT, preferred_element_type=F32)
            if diagonal:
                sc = jnp.where(_diag_mask(t), sc, _NEG)
            pr = jnp.exp2(sc - lse_ref[...])
            do16 = do_ref[...].astype(BF16)
            dv_acc[...] += jnp.dot(pr.astype(BF16), do16, preferred_element_type=F32)
            dp = lax.dot_general(v_ref[...], do16, _NT, preferred_element_type=F32)
            ds = (pr * (dp - dl_ref[...])).astype(BF16)
            dk_acc[...] += jnp.dot(ds, q_ref[...], preferred_element_type=F32)

        @pl.when(i == j)
        def _():
            dk_acc[...] = jnp.zeros_like(dk_acc)
            dv_acc[...] = jnp.zeros_like(dv_acc)
            step(True)

        @pl.when(i > j)
        def _():
            step(False)

        @pl.when(i == n - 1)
        def _():
            dk_ref[...] = (dk_acc[...] * (1.0 / _LOG2E)).astype(BF16)
            dv_ref[...] = dv_acc[...].astype(BF16)

    qs = pl.BlockSpec((t, LANES), lambda h, p: (_tri_cols(p, n)[1], h))
    ks = pl.BlockSpec((t, LANES), lambda h, p: (_tri_cols(p, n)[0], h))
    ls = pl.BlockSpec((None, 1, t), lambda h, p: (h, 0, _tri_cols(p, n)[1]))
    return _call(
        body, name=name, grid=(MLA_HEADS, n * (n + 1) // 2), in_specs=[qs, ks, ks, ls, ls, qs], out_specs=[ks, ks],
        out_shape=[jax.ShapeDtypeStruct((s, HP), BF16)] * 2,
        scratch_shapes=[pltpu.VMEM((t, LANES), F32), pltpu.VMEM((t, LANES), F32)],
        args=(q, k, v, lse, delta, do), semantics=("parallel", "arbitrary"), side=side)


XBC = HP + 2 * SSD_GROUPS * SSD_STATE
BCW = 2 * SSD_GROUPS * SSD_STATE


def _conv_fwd(p, col0, width, conv_w, conv_b, name):
    s = p.shape[0]
    c0, nblk = col0 // LANES, width // LANES

    def body(x_ref, w_ref, b_ref, o_ref, pad):
        pad[0:8, :] = jnp.zeros((8, LANES), F32)
        pad[8:s + 8, :] = x_ref[...].astype(F32)
        acc = jnp.broadcast_to(b_ref[...], (s, LANES))
        for t in range(SSD_CONV):
            acc = acc + pad[pl.ds(8 - (SSD_CONV - 1) + t, s), :] * w_ref[t:t + 1, :]
        o_ref[...] = acc * _sigmoid(acc)

    return pl.pallas_call(
        body, name=name, grid=(nblk,),
        in_specs=[pl.BlockSpec((s, LANES), lambda j: (0, c0 + j)), pl.BlockSpec((SSD_CONV, LANES), lambda j: (0, j)),
                  pl.BlockSpec((1, LANES), lambda j: (0, j))],
        out_specs=pl.BlockSpec((s, LANES), lambda j: (0, j)), out_shape=jax.ShapeDtypeStruct((s, width), F32),
        scratch_shapes=[pltpu.VMEM((s + 8, LANES), F32)], compiler_params=_params("parallel"))(p, conv_w, conv_b)


def _conv_bwd(p, col0, width, conv_w, conv_b, dact, name):
    s = p.shape[0]
    c0, nblk = col0 // LANES, width // LANES

    def body(x_ref, w_ref, b_ref, d_ref, dx_ref, dw_ref, db_ref, pad, padd):
        pad[0:8, :] = jnp.zeros((8, LANES), F32)
        pad[8:s + 8, :] = x_ref[...].astype(F32)
        acc = jnp.broadcast_to(b_ref[...], (s, LANES))
        for t in range(SSD_CONV):
            acc = acc + pad[pl.ds(8 - (SSD_CONV - 1) + t, s), :] * w_ref[t:t + 1, :]
        sg = _sigmoid(acc)
        dpre = d_ref[...] * (sg * (1.0 + acc * (1.0 - sg)))
        padd[0:s, :] = dpre
        padd[s:s + 8, :] = jnp.zeros((8, LANES), F32)
        dx = jnp.zeros((s, LANES), F32)
        for t in range(SSD_CONV):
            dx = dx + padd[pl.ds(SSD_CONV - 1 - t, s), :] * w_ref[t:t + 1, :]
            dw_ref[t:t + 1, :] = jnp.sum(dpre * pad[pl.ds(8 - (SSD_CONV - 1) + t, s), :], axis=0, keepdims=True)
        dx_ref[...] = dx.astype(BF16)
        db_ref[...] = jnp.sum(dpre, axis=0, keepdims=True)

    blk = pl.BlockSpec((s, LANES), lambda j: (0, j))
    return pl.pallas_call(
        body, name=name, grid=(nblk,),
        in_specs=[pl.BlockSpec((s, LANES), lambda j: (0, c0 + j)), pl.BlockSpec((SSD_CONV, LANES), lambda j: (0, j)),
                  pl.BlockSpec((1, LANES), lambda j: (0, j)), blk],
        out_specs=[blk, pl.BlockSpec((SSD_CONV, LANES), lambda j: (0, j)), pl.BlockSpec((1, LANES), lambda j: (0, j))],
        out_shape=[jax.ShapeDtypeStruct((s, width), BF16), jax.ShapeDtypeStruct((SSD_CONV, width), F32),
                   jax.ShapeDtypeStruct((1, width), F32)],
        scratch_shapes=[pltpu.VMEM((s + 8, LANES), F32), pltpu.VMEM((s + 8, LANES), F32)],
        compiler_params=_params("parallel"))(p, conv_w, conv_b, dact)


def _softplus(x):
    return jnp.maximum(x, 0.0) + jnp.log(1.0 + jnp.exp(-jnp.abs(x)))


def _dt_fwd(p, dt_bias, a_log, name):
    s = p.shape[0]
    tm = _pick(s, 512)

    def body(x_ref, b_ref, a_ref, dt_ref, da_ref):
        dtv = _softplus(x_ref[...] + b_ref[...])
        dav = dtv * (-jnp.exp(a_ref[...]))
        for h in range(SSD_HEADS):
            hs = slice(h * LANES, (h + 1) * LANES)
            dt_ref[:, hs] = jnp.broadcast_to(dtv[:, h:h + 1], (tm, LANES))
            da_ref[:, hs] = jnp.broadcast_to(dav[:, h:h + 1], (tm, LANES))

    out = pl.BlockSpec((tm, HP), lambda i: (i, 0))
    return pl.pallas_call(
        body, name=name, grid=(s // tm,),
        in_specs=[pl.BlockSpec((tm, LANES), lambda i: (i, T_DT // LANES)), _const((1, LANES)), _const((1, LANES))],
        out_specs=[out, out], out_shape=[jax.ShapeDtypeStruct((s, HP), F32)] * 2,
        compiler_params=_params("parallel"))(p, dt_bias, a_log)


def _dt_bwd(p, dt_bias, a_log, dda, ddtx, name):
    s = p.shape[0]
    tm = _pick(s, 512)

    def body(x_ref, b_ref, a_ref, dda_ref, ddtx_ref, dx_ref, db_ref, dal_ref):
        @pl.when(pl.program_id(0) == 0)
        def _():
            db_ref[...] = jnp.zeros_like(db_ref)
            dal_ref[...] = jnp.zeros_like(dal_ref)

        x = x_ref[...] + b_ref[...]
        dtv = _softplus(x)
        av = -jnp.exp(a_ref[...])
        lane = lax.broadcasted_iota(jnp.int32, (tm, LANES), 1)
        pa = jnp.zeros((tm, LANES), F32)
        px = jnp.zeros((tm, LANES), F32)
        for h in range(SSD_HEADS):
            pa = jnp.where(lane == h, dda_ref[:, h * LANES:(h + 1) * LANES], pa)
            px = jnp.where(lane == h, ddtx_ref[:, h * LANES:(h + 1) * LANES], px)
        draw = (pa * av + px) * _sigmoid(x)
        dx_ref[...] = draw.astype(BF16)
        db_ref[...] += jnp.sum(draw, axis=0, keepdims=True)
        dal_ref[...] += jnp.sum(pa * dtv, axis=0, keepdims=True) * av

    hd = pl.BlockSpec((tm, HP), lambda i: (i, 0))
    return pl.pallas_call(
        body, name=name, grid=(s // tm,),
        in_specs=[pl.BlockSpec((tm, LANES), lambda i: (i, T_DT // LANES)), _const((1, LANES)), _const((1, LANES)), hd, hd],
        out_specs=[pl.BlockSpec((tm, LANES), lambda i: (i, 0)), _const((1, LANES)), _const((1, LANES))],
        out_shape=[jax.ShapeDtypeStruct((s, LANES), BF16), jax.ShapeDtypeStruct((1, LANES), F32),
                   jax.ShapeDtypeStruct((1, LANES), F32)],
        compiler_params=_params("arbitrary"))(p, dt_bias, a_log, dda, ddtx)


def _cumsum_rows(x):
    row = lax.broadcasted_iota(jnp.int32, x.shape, 0)
    k = 1
    while k < x.shape[0]:
        x = x + jnp.where(row >= k, pltpu.roll(x, k, 0), 0.0)
        k *= 2
    return x


def _rev_cumsum_rows(x):
    n = x.shape[0]
    row = lax.broadcasted_iota(jnp.int32, x.shape, 0)
    k = 1
    while k < n:
        x = x + jnp.where(row < n - k, pltpu.roll(x, n - k, 0), 0.0)
        k *= 2
    return x


HPG = SSD_HEADS // SSD_GROUPS


def _chunk_decay(da):
    cs = _cumsum_rows(da)
    lm = jnp.exp(jnp.where(_tril_mask(), cs - cs.T, _NEG))
    return cs, lm, cs[CHUNK - 1:CHUNK, :]


def _scan_fwd(xs, bc, dtb, dab, name, side=None):
    s = xs.shape[0]
    nc = s // CHUNK

    def body(x_ref, b_ref, c_ref, dt_ref, da_ref, y_ref, sin_ref, state):
        @pl.when(pl.program_id(1) == 0)
        def _():
            state[...] = jnp.zeros_like(state)

        bv = b_ref[...]
        b16, c16 = bv.astype(BF16), c_ref[...].astype(BF16)
        g = lax.dot_general(c16, b16, _NT, preferred_element_type=F32)
        for hh in range(HPG):
            hs = slice(hh * LANES, (hh + 1) * LANES)
            st = state[hh]
            sin_ref[hh] = st
            cs, lm, cl = _chunk_decay(da_ref[:, hs])
            xd = (x_ref[:, hs] * dt_ref[:, hs]).astype(BF16)
            y = jnp.dot((g * lm).astype(BF16), xd, preferred_element_type=F32)
            y_ref[:, hs] = y + jnp.dot(c16, st.astype(BF16), preferred_element_type=F32) * jnp.exp(cs)
            bd = (bv * jnp.exp(cl - cs)).astype(BF16)
            state[hh] = jnp.exp(cl) * st + lax.dot_general(bd, xd, _TN, preferred_element_type=F32)

    gw = HPG * LANES
    hd = pl.BlockSpec((CHUNK, gw), lambda g, c: (c, g))
    return _call(
        body, name=name, grid=(SSD_GROUPS, nc),
        in_specs=[hd, pl.BlockSpec((CHUNK, LANES), lambda g, c: (c, g)),
                  pl.BlockSpec((CHUNK, LANES), lambda g, c: (c, SSD_GROUPS + g)), hd, hd],
        out_specs=[hd, pl.BlockSpec((HPG, None, SSD_STATE, LANES), lambda g, c: (g, c, 0, 0))],
        out_shape=[jax.ShapeDtypeStruct((s, HP), F32), jax.ShapeDtypeStruct((SSD_HEADS, nc, SSD_STATE, LANES), F32)],
        scratch_shapes=[pltpu.VMEM((HPG, SSD_STATE, LANES), F32)],
        args=(xs, bc, bc, dtb, dab), semantics=("parallel", "arbitrary"), side=side)


def _scan_bwd(xs, bc, dtb, dab, s_in, dy, d_vec, name):
    s = xs.shape[0]
    nc = s // CHUNK

    def body(x_ref, b_ref, c_ref, dt_ref, da_ref, sin_ref, dy_ref, dv_ref, dx_ref, db_ref, dc_ref, dda_ref, ddtx_ref, dstate):
        @pl.when(pl.program_id(1) == 0)
        def _():
            dstate[...] = jnp.zeros_like(dstate)

        bv = b_ref[...]
        b16, c16 = bv.astype(BF16), c_ref[...].astype(BF16)
        g = lax.dot_general(c16, b16, _NT, preferred_element_type=F32)
        row = lax.broadcasted_iota(jnp.int32, (CHUNK, 1), 0)
        dbm = jnp.zeros((CHUNK, SSD_STATE), F32)
        dcm = jnp.zeros((CHUNK, SSD_STATE), F32)
        for hh in range(HPG):
            hs = slice(hh * LANES, (hh + 1) * LANES)
            st, ds = sin_ref[hh], dstate[hh]
            st16, ds16 = st.astype(BF16), ds.astype(BF16)
            xv, dtv, dyv = x_ref[:, hs], dt_ref[:, hs], dy_ref[:, hs]
            cs, lm, cl = _chunk_decay(da_ref[:, hs])
            ecs, ecl = jnp.exp(cs), jnp.exp(cl)
            decay = jnp.exp(cl - cs)
            xd = (xv * dtv).astype(BF16)
            dy16 = dyv.astype(BF16)
            dye = (dyv * ecs).astype(BF16)
            yoff = jnp.dot(c16, st16, preferred_element_type=F32) * ecs
            dcs = jnp.sum(dyv * yoff, axis=-1, keepdims=True)
            dcm = dcm + lax.dot_general(dye, st16, _NT, preferred_element_type=F32)
            dstate[hh] = ecl * ds + lax.dot_general(c16, dye, _TN, preferred_element_type=F32)
            dcl = jnp.sum(jnp.sum(ds * st, axis=0, keepdims=True), axis=1, keepdims=True) * ecl[:, 0:1]
            bd32 = bv * decay
            qm = lax.dot_general(xd, ds16, _NT, preferred_element_type=F32)
            dbm = dbm + qm * decay
            w = jnp.sum(bd32 * qm, axis=-1, keepdims=True)
            dcs = dcs - w
            dcl = dcl + jnp.sum(w, axis=0, keepdims=True)
            dxd = jnp.dot(bd32.astype(BF16), ds16, preferred_element_type=F32)
            m16 = (g * lm).astype(BF16)
            dm = lax.dot_general(dy16, xd, _NT, preferred_element_type=F32)
            dxd = dxd + lax.dot_general(m16, dy16, _TN, preferred_element_type=F32)
            dg = dm * lm
            dg16 = dg.astype(BF16)
            tt = dg * g
            dcm = dcm + jnp.dot(dg16, b16, preferred_element_type=F32)
            dbm = dbm + lax.dot_general(dg16, c16, _TN, preferred_element_type=F32)
            dcs = dcs + jnp.sum(tt, axis=-1, keepdims=True) - jnp.sum(tt.T, axis=-1, keepdims=True)
            dcs = dcs + jnp.where(row == CHUNK - 1, dcl, 0.0)
            dda_ref[:, hs] = _rev_cumsum_rows(jnp.broadcast_to(dcs, (CHUNK, LANES)))
            ddtx_ref[:, hs] = jnp.broadcast_to(jnp.sum(dxd * xv, axis=-1, keepdims=True), (CHUNK, LANES))
            dx_ref[:, hs] = dxd * dtv + dyv * dv_ref[:, hs]
        db_ref[...] = dbm
        dc_ref[...] = dcm

    gw = HPG * LANES
    hd = pl.BlockSpec((CHUNK, gw), lambda g, c: (nc - 1 - c, g))
    gp = pl.BlockSpec((CHUNK, LANES), lambda g, c: (nc - 1 - c, g))
    return pl.pallas_call(
        body, name=name, grid=(SSD_GROUPS, nc),
        in_specs=[hd, gp, pl.BlockSpec((CHUNK, LANES), lambda g, c: (nc - 1 - c, SSD_GROUPS + g)), hd, hd,
                  pl.BlockSpec((HPG, None, SSD_STATE, LANES), lambda g, c: (g, nc - 1 - c, 0, 0)), hd,
                  pl.BlockSpec((1, gw), lambda g, c: (0, g))],
        out_specs=[hd, gp, gp, hd, hd],
        out_shape=[jax.ShapeDtypeStruct((s, HP), F32), jax.ShapeDtypeStruct((s, SSD_GROUPS * SSD_STATE), F32),
                   jax.ShapeDtypeStruct((s, SSD_GROUPS * SSD_STATE), F32), jax.ShapeDtypeStruct((s, HP), F32),
                   jax.ShapeDtypeStruct((s, HP), F32)],
        scratch_shapes=[pltpu.VMEM((HPG, SSD_STATE, LANES), F32)],
        compiler_params=_params("parallel", "arbitrary"))(xs, bc, bc, dtb, dab, s_in, dy, d_vec)


_GN = SSD_INNER // SSD_GROUPS
_GW = HP // SSD_GROUPS


def _ssd_post_fwd(y, xbc, p, d_vec, gain, name):
    s = y.shape[0]
    tm = _pick(s, 512)

    def body(y_ref, x_ref, z_ref, d_ref, g_ref, o_ref):
        z = z_ref[...].astype(F32)
        y2 = (y_ref[...] + x_ref[...] * d_ref[...]) * (z * _sigmoid(z))
        for g in range(SSD_GROUPS):
            gs = slice(g * _GW, (g + 1) * _GW)
            yg = y2[:, gs]
            r = lax.rsqrt(jnp.sum(yg * yg, axis=-1, keepdims=True) * (1.0 / _GN) + EPS)
            o_ref[:, gs] = (yg * r * g_ref[:, gs]).astype(BF16)

    hd = pl.BlockSpec((tm, HP), lambda i: (i, 0))
    return pl.pallas_call(
        body, name=name, grid=(s // tm,),
        in_specs=[hd, hd, pl.BlockSpec((tm, HP), lambda i: (i, C_Z // HP)), _const((1, HP)), _const((1, HP))],
        out_specs=hd, out_shape=jax.ShapeDtypeStruct((s, HP), BF16), compiler_params=_params("parallel"))(y, xbc, p, d_vec, gain)


def _ssd_post_bwd(y, xbc, p, d_vec, gain, dyn, name):
    s = y.shape[0]
    tm = _pick(s, 512)

    def body(y_ref, x_ref, z_ref, d_ref, g_ref, dn_ref, dy_ref, dz_ref, dg_ref, dd_ref):
        @pl.when(pl.program_id(0) == 0)
        def _():
            dg_ref[...] = jnp.zeros_like(dg_ref)
            dd_ref[...] = jnp.zeros_like(dd_ref)

        z, xv = z_ref[...].astype(F32), x_ref[...]
        sg = _sigmoid(z)
        sz = z * sg
        yt = y_ref[...] + xv * d_ref[...]
        y2 = yt * sz
        for g in range(SSD_GROUPS):
            gs = slice(g * _GW, (g + 1) * _GW)
            yg, dn = y2[:, gs], dn_ref[:, gs].astype(F32)
            r = lax.rsqrt(jnp.sum(yg * yg, axis=-1, keepdims=True) * (1.0 / _GN) + EPS)
            u = dn * g_ref[:, gs]
            dy2 = r * u - yg * (r * r * r) * (jnp.sum(yg * u, axis=-1, keepdims=True) * (1.0 / _GN))
            dg_ref[:, gs] += jnp.sum(dn * yg * r, axis=0, keepdims=True)
            dyt = dy2 * sz[:, gs]
            dy_ref[:, gs] = dyt
            dz_ref[:, gs] = (dy2 * yt[:, gs] * (sg[:, gs] * (1.0 + z[:, gs] * (1.0 - sg[:, gs])))).astype(BF16)
            dd_ref[:, gs] += jnp.sum(dyt * xv[:, gs], axis=0, keepdims=True)

    hd = pl.BlockSpec((tm, HP), lambda i: (i, 0))
    return pl.pallas_call(
        body, name=name, grid=(s // tm,),
        in_specs=[hd, hd, pl.BlockSpec((tm, HP), lambda i: (i, C_Z // HP)), _const((1, HP)), _const((1, HP)), hd],
        out_specs=[hd, hd, _const((1, HP)), _const((1, HP))],
        out_shape=[jax.ShapeDtypeStruct((s, HP), F32), jax.ShapeDtypeStruct((s, HP), BF16),
                   jax.ShapeDtypeStruct((1, HP), F32), jax.ShapeDtypeStruct((1, HP), F32)],
        compiler_params=_params("arbitrary"))(y, xbc, p, d_vec, gain, dyn)


def _merge_fwd(p, ya, o, yc, wb0, wb1, wb2, w_out, x, name, side=None):
    s = p.shape[0]
    tm = _pick(s, 512)

    def body(g_ref, ya_ref, o_ref, yc_ref, w0_ref, w1_ref, w2_ref, wo_ref, x_ref, mg_ref, y_ref):
        acc = jnp.zeros((tm, D_MODEL), F32)
        for i, (b_ref, w_ref) in enumerate(((ya_ref, w0_ref), (o_ref, w1_ref), (yc_ref, w2_ref))):
            t = jnp.dot(b_ref[...].astype(BF16), w_ref[...], preferred_element_type=F32)
            acc = acc + _sigmoid(g_ref[:, i * D_MODEL:(i + 1) * D_MODEL].astype(F32)) * t
        mg = acc.astype(BF16)
        mg_ref[...] = mg
        y_ref[...] = x_ref[...] + jnp.dot(mg, wo_ref[...], preferred_element_type=F32)

    row = pl.BlockSpec((tm, D_MODEL), lambda i: (i, 0))
    return _call(
        body, name=name, grid=(s // tm,),
        in_specs=[pl.BlockSpec((tm, 3 * D_MODEL), lambda i: (i, C_G // (3 * D_MODEL))),
                  pl.BlockSpec((tm, GM_WIDTH), lambda i: (i, 0)), row, row,
                  _resident((GM_WIDTH, D_MODEL)), _resident((HP, D_MODEL)), _resident((HP, D_MODEL)),
                  _resident((D_MODEL, D_MODEL)), row],
        out_specs=[row, row],
        out_shape=[jax.ShapeDtypeStruct((s, D_MODEL), BF16), jax.ShapeDtypeStruct((s, D_MODEL), F32)],
        scratch_shapes=[], args=(p, ya, o, yc, wb0, wb1, wb2, w_out, x), semantics=("parallel",), side=side)


def _merge_bwd(p, ya, o, yc, wb0, wb1, wb2, w_out, dy, name):
    s = p.shape[0]
    tm = _pick(s, 512)

    def body(g_ref, ya_ref, o_ref, yc_ref, w0_ref, w1_ref, w2_ref, wo_ref, dy_ref,
             d0_ref, d1_ref, d2_ref, dg_ref, dya_ref, do_ref, dyc_ref):
        dm = lax.dot_general(dy_ref[...].astype(BF16), wo_ref[...], _NT, preferred_element_type=F32)
        for i, (b_ref, w_ref, d_ref, db_ref) in enumerate(((ya_ref, w0_ref, d0_ref, dya_ref), (o_ref, w1_ref, d1_ref, do_ref),
                                                            (yc_ref, w2_ref, d2_ref, dyc_ref))):
            cs = slice(i * D_MODEL, (i + 1) * D_MODEL)
            t = jnp.dot(b_ref[...].astype(BF16), w_ref[...], preferred_element_type=F32)
            sg = _sigmoid(g_ref[:, cs].astype(F32))
            dt16 = (dm * sg).astype(BF16)
            d_ref[...] = dt16
            dg_ref[:, cs] = (dm * t * sg * (1.0 - sg)).astype(BF16)
            db_ref[...] = lax.dot_general(dt16, w_ref[...], _NT, preferred_element_type=F32).astype(db_ref.dtype)

    row = pl.BlockSpec((tm, D_MODEL), lambda i: (i, 0))
    nar = pl.BlockSpec((tm, GM_WIDTH), lambda i: (i, 0))
    wide = pl.BlockSpec((tm, 3 * D_MODEL), lambda i: (i, 0))
    return pl.pallas_call(
        body, name=name, grid=(s // tm,),
        in_specs=[pl.BlockSpec((tm, 3 * D_MODEL), lambda i: (i, C_G // (3 * D_MODEL))), nar, row, row,
                  _resident((GM_WIDTH, D_MODEL)), _resident((HP, D_MODEL)), _resident((HP, D_MODEL)),
                  _resident((D_MODEL, D_MODEL)), row],
        out_specs=[row, row, row, wide, nar, row, row],
        out_shape=[jax.ShapeDtypeStruct((s, D_MODEL), BF16)] * 3 + [jax.ShapeDtypeStruct((s, 3 * D_MODEL), BF16),
                   jax.ShapeDtypeStruct((s, GM_WIDTH), BF16), jax.ShapeDtypeStruct((s, D_MODEL), BF16),
                   jax.ShapeDtypeStruct((s, D_MODEL), F32)],
        compiler_params=_params("parallel"))(p, ya, o, yc, wb0, wb1, wb2, w_out, dy)


def _loss_head(y, target, name):
    s, d = y.shape
    tm = _pick(s, 512)

    def body(y_ref, t_ref, dy_ref, sq_ref):
        @pl.when(pl.program_id(0) == 0)
        def _():
            sq_ref[...] = jnp.zeros_like(sq_ref)

        e = y_ref[...] - t_ref[...]
        dy_ref[...] = e * (1.0 / d)
        sq_ref[...] += jnp.sum(e * e, axis=0, keepdims=True)

    row = pl.BlockSpec((tm, d), lambda i: (i, 0))
    return pl.pallas_call(
        body, name=name, grid=(s // tm,), in_specs=[row, row], out_specs=[row, _const((1, d))],
        out_shape=[jax.ShapeDtypeStruct((s, d), F32), jax.ShapeDtypeStruct((1, d), F32)],
        compiler_params=_params("arbitrary"))(y, target)


def _adamw(w, g, m, v, name):
    rows, cols = w.shape
    tr = rows
    for cand in (512, 256, 128, 64, 32, 16, 8):
        if rows % cand == 0 and cand * cols * 4 <= 3 * 1024 * 1024:
            tr = cand
            break

    def body(w_ref, g_ref, m_ref, v_ref, d_ref, nm_ref, nv_ref):
        d_ref[...], nm_ref[...], nv_ref[...] = _adam_update(w_ref[...], g_ref[...], m_ref[...], v_ref[...])

    blk = pl.BlockSpec((tr, cols), lambda i: (i, 0))
    return pl.pallas_call(
        body, name=name, grid=(rows // tr,), in_specs=[blk] * 4, out_specs=[blk] * 3,
        out_shape=[jax.ShapeDtypeStruct((rows, cols), F32)] * 3, compiler_params=_params("parallel"))(w, g, m, v)


def _adam_update(w, g, m, v):
    nm = ADAM_B1 * m + (1.0 - ADAM_B1) * g
    nv = ADAM_B2 * v + (1.0 - ADAM_B2) * (g * g)
    c1 = 1.0 - ADAM_B1 ** ADAM_STEP
    c2 = 1.0 - ADAM_B2 ** ADAM_STEP
    return -ADAM_LR * ((nm / c1) / (jnp.sqrt(nv / c2) + ADAM_EPS) + ADAM_WD * w), nm, nv


def _adamw_sharded(w, m, v, mine, theirs, name, side=None):
    depth, rows, cols = w.shape
    tr = _row_tile(rows // 2, cols, 1024 * 1024)
    nb = rows // 2 // tr

    def body(w_ref, m_ref, v_ref, a_ref, b_ref, g_ref, d_ref, nm_ref, nv_ref):
        c = lax.axis_index("c")
        g = jnp.where(pl.program_id(1) // nb == c, a_ref[...], b_ref[...])
        g_ref[...] = g
        d_ref[...], nm_ref[...], nv_ref[...] = _adam_update(w_ref[...], g, m_ref[...], v_ref[...])

    blk = pl.BlockSpec((None, tr, cols), lambda l, i: (l, i, 0))
    mine_s = pl.BlockSpec((None, tr, cols), lambda l, i: (l, jnp.where(i // nb == lax.axis_index("c"), i % nb, 0), 0))
    theirs_s = pl.BlockSpec((None, tr, cols), lambda l, i: (l, jnp.where(i // nb == lax.axis_index("c"), 0, i % nb), 0))
    return _call(
        body, name=name, grid=(depth, rows // tr), in_specs=[blk, blk, blk, mine_s, theirs_s], out_specs=[blk] * 4,
        out_shape=[jax.ShapeDtypeStruct((depth, rows, cols), F32)] * 4, scratch_shapes=[],
        args=(w, m, v, mine, theirs), semantics=("parallel", "parallel"), side=side)


ANY = pl.BlockSpec(memory_space=pl.ANY)


def _me():
    return lax.axis_index("x"), lax.axis_index("y"), lax.axis_index("c")


def _other_chips(x, y):
    return [(1 - x, y), (x, 1 - y), (1 - x, 1 - y)]


def _chip_index(cx, cy):
    return 2 * cx + cy


class _Exchange:
    def __init__(self, ins, out_shapes, n_sems, start, finish):
        self.ins, self.out_shapes, self.n_sems, self.start, self.finish = list(ins), list(out_shapes), n_sems, start, finish


def _sem_scratch(ex):
    return [pltpu.SemaphoreType.DMA((ex.n_sems,)), pltpu.SemaphoreType.DMA((ex.n_sems,))]


class _SemsFrom:
    def __init__(self, sems, first):
        self._sems, self._first = sems, first

    @property
    def at(self):
        return self

    def __getitem__(self, k):
        return self._sems.at[self._first + k]


def _together(a, b):
    ia, oa = len(a.ins), len(a.out_shapes)

    def parts(in_refs, out_refs, send, recv):
        return ((in_refs[:ia], out_refs[:oa], send, recv),
                (in_refs[ia:], out_refs[oa:], _SemsFrom(send, a.n_sems), _SemsFrom(recv, a.n_sems)))

    def start(*refs):
        pa, pb = parts(*refs)
        a.start(*pa)
        b.start(*pb)

    def finish(*refs):
        pa, pb = parts(*refs)
        a.finish(*pa)
        b.finish(*pb)

    return _Exchange(a.ins + b.ins, a.out_shapes + b.out_shapes, a.n_sems + b.n_sems, start, finish)


def _run_exchange(ex, name):
    n_in, n_out = len(ex.ins), len(ex.out_shapes)

    def body(*refs):
        in_refs, out_refs, (send, recv) = refs[:n_in], refs[n_in:n_in + n_out], refs[n_in + n_out:]
        ex.start(in_refs, out_refs, send, recv)
        ex.finish(in_refs, out_refs, send, recv)

    return pl.pallas_call(body, name=name, in_specs=[ANY] * n_in, out_specs=[ANY] * n_out, out_shape=ex.out_shapes,
                          scratch_shapes=_sem_scratch(ex))(*ex.ins)


def _call(body, *, name, grid, in_specs, out_specs, out_shape, scratch_shapes, args, semantics, side=None):
    if side is None:
        return pl.pallas_call(body, name=name, grid=grid, in_specs=in_specs, out_specs=out_specs, out_shape=out_shape,
                              scratch_shapes=scratch_shapes, compiler_params=_params(*semantics))(*args), []
    n_in, n_out, n_sc = len(in_specs), len(out_specs), len(scratch_shapes)
    s_in, s_out = len(side.ins), len(side.out_shapes)

    def hosted(*refs):
        pos = 0
        parts = []
        for size in (n_in, s_in, n_out, s_out, n_sc, 2):
            parts.append(refs[pos:pos + size])
            pos += size
        ins, sins, outs, souts, scratch, (send, recv) = parts
        ids = [pl.program_id(a) for a in range(len(grid))]
        first = functools.reduce(jnp.logical_and, [i == 0 for i in ids])
        last = functools.reduce(jnp.logical_and, [i == g - 1 for i, g in zip(ids, grid)])

        @pl.when(first)
        def _():
            side.start(sins, souts, send, recv)

        body(*ins, *outs, *scratch)

        @pl.when(last)
        def _():
            side.finish(sins, souts, send, recv)

    res = pl.pallas_call(
        hosted, name=name, grid=grid, in_specs=list(in_specs) + [ANY] * s_in, out_specs=list(out_specs) + [ANY] * s_out,
        out_shape=list(out_shape) + side.out_shapes, scratch_shapes=list(scratch_shapes) + _sem_scratch(side),
        compiler_params=_params(*["arbitrary"] * len(grid)))(*args, *side.ins)
    return res[:n_out], res[n_out:]


def _half(ref_rows, c):
    return pl.ds(c * (ref_rows // 2), ref_rows // 2)


def _gather_exchange(shards, layer):
    n = len(shards)
    rows = [a.shape[1] for a in shards]

    def copy(in_refs, out_refs, send, recv, t, k, chip, hc, to, from_input=False):
        dst = out_refs[t].at[chip, _half(rows[t], hc)]
        src = in_refs[t].at[layer, _half(rows[t], hc)] if from_input else dst
        return pltpu.make_async_remote_copy(src_ref=src, dst_ref=dst, send_sem=send.at[7 * t + k], recv_sem=recv.at[7 * t + k],
                                            device_id=to, device_id_type=MESH)

    def own(in_refs, out_refs, send, recv, t):
        x, y, c = _me()
        return pltpu.make_async_remote_copy(src_ref=in_refs[t].at[layer], dst_ref=out_refs[t].at[_chip_index(x, y)],
                                            send_sem=send.at[7 * t + 6], recv_sem=recv.at[7 * t + 6],
                                            device_id=(x, y, 1 - c), device_id_type=MESH)

    def start(in_refs, out_refs, send, recv):
        x, y, c = _me()
        for j, chip in enumerate(_other_chips(x, y)):
            for t in range(n):
                copy(in_refs, out_refs, send, recv, t, j, _chip_index(x, y), c, (*chip, c), from_input=True).start()
        for t in range(n):
            own(in_refs, out_refs, send, recv, t).start()

    def finish(in_refs, out_refs, send, recv):
        x, y, c = _me()
        chips = _other_chips(x, y)
        passed = []
        for t in range(n):
            own(in_refs, out_refs, send, recv, t).wait()
        for j, chip in enumerate(chips):
            for t in range(n):
                copy(in_refs, out_refs, send, recv, t, j, _chip_index(*chip), c, (x, y, c)).wait_recv()
                cp = copy(in_refs, out_refs, send, recv, t, 3 + j, _chip_index(*chip), c, (x, y, 1 - c))
                cp.start()
                passed.append(cp)
        for j, chip in enumerate(chips):
            for t in range(n):
                copy(in_refs, out_refs, send, recv, t, 3 + j, _chip_index(*chip), 1 - c, (x, y, c)).wait_recv()
                copy(in_refs, out_refs, send, recv, t, j, _chip_index(x, y), c, (*chip, c), from_input=True).wait_send()
        for cp in passed:
            cp.wait_send()

    return _Exchange(shards, [jax.ShapeDtypeStruct((N_CHIPS,) + a.shape[1:], a.dtype) for a in shards], 7 * n, start, finish)


def _pair_exchange(gs):
    n = len(gs)
    rows = [a.shape[1] for a in gs]

    def copies(in_refs, out_refs, send, recv):
        x, y, c = _me()
        return [pltpu.make_async_remote_copy(src_ref=in_refs[t].at[:, _half(rows[t], 1 - c)], dst_ref=out_refs[t],
                                             send_sem=send.at[t], recv_sem=recv.at[t], device_id=(x, y, 1 - c),
                                             device_id_type=MESH) for t in range(n)]

    def start(*refs):
        for cp in copies(*refs):
            cp.start()

    def finish(*refs):
        for cp in copies(*refs):
            cp.wait()

    return _Exchange(gs, [jax.ShapeDtypeStruct((N_CHIPS, a.shape[1] // 2, a.shape[2]), a.dtype) for a in gs], n, start, finish)


def _row_tile(rows, cols, budget=2 * 1024 * 1024):
    best = None
    for t in range(8, rows + 1, 8):
        if rows % t == 0 and t * cols * 4 <= budget:
            best = t
    return best or rows


def _pair_add(g, got, name):
    _, rows, cols = g.shape
    tr = _row_tile(rows // 2, cols)
    nb = rows // 2 // tr

    def body(g_ref, r_ref, o16_ref, own_ref):
        x, y, _ = _me()
        tot = g_ref[...] + r_ref[...]
        o16_ref[...] = tot.astype(BF16)

        @pl.when(pl.program_id(1) == _chip_index(x, y))
        def _():
            own_ref[...] = tot

    blk = (None, tr, cols)
    return pl.pallas_call(
        body, name=name, grid=(nb, N_CHIPS),
        in_specs=[pl.BlockSpec(blk, lambda i, k: (k, i + lax.axis_index("c") * nb, 0)),
                  pl.BlockSpec(blk, lambda i, k: (k, i, 0))],
        out_specs=[pl.BlockSpec(blk, lambda i, k: (k, i, 0)), pl.BlockSpec((tr, cols), lambda i, k: (i, 0))],
        out_shape=[jax.ShapeDtypeStruct((N_CHIPS, rows // 2, cols), BF16), jax.ShapeDtypeStruct((rows // 2, cols), F32)],
        compiler_params=_params("parallel", "arbitrary"))(g, got)


def _chip_exchange(parts):
    n = len(parts)

    def copies(in_refs, out_refs, send, recv):
        x, y, c = _me()
        return [pltpu.make_async_remote_copy(src_ref=in_refs[t].at[_chip_index(*chip)], dst_ref=out_refs[t].at[j],
                                             send_sem=send.at[3 * t + j], recv_sem=recv.at[3 * t + j],
                                             device_id=(*chip, c), device_id_type=MESH)
                for j, chip in enumerate(_other_chips(x, y)) for t in range(n)]

    def start(*refs):
        for cp in copies(*refs):
            cp.start()

    def finish(*refs):
        for cp in copies(*refs):
            cp.wait()

    return _Exchange(parts, [jax.ShapeDtypeStruct((3,) + a.shape[1:], a.dtype) for a in parts], 3 * n, start, finish)


def _chip_add(own, got, name, layer, into=None):
    rows, cols = own.shape
    tr = _row_tile(rows, cols, 1024 * 1024)

    def body(own_ref, got_ref, *rest):
        acc = own_ref[...]
        for j in range(3):
            acc = acc + got_ref[j].astype(F32)
        rest[-1][...] = acc

    in_specs = [pl.BlockSpec((tr, cols), lambda i: (i, 0)), pl.BlockSpec((3, tr, cols), lambda i: (0, i, 0))]
    args, alias = [own, got], {}
    if into is not None:
        in_specs.append(ANY)
        args.append(into)
        alias = {2: 0}
    return pl.pallas_call(
        body, name=name, grid=(rows // tr,), in_specs=in_specs,
        out_specs=pl.BlockSpec((None, tr, cols), lambda i: (layer, i, 0)),
        out_shape=jax.ShapeDtypeStruct((DEPTH, rows, cols), F32), input_output_aliases=alias,
        compiler_params=_params("parallel"))(*args)


def _pair_share(halves):
    n = len(halves)

    def copies(in_refs, out_refs, send, recv):
        x, y, c = _me()
        return [pltpu.make_async_remote_copy(src_ref=in_refs[t], dst_ref=out_refs[t], send_sem=send.at[t],
                                             recv_sem=recv.at[t], device_id=(x, y, 1 - c), device_id_type=MESH)
                for t in range(n)]

    def start(*refs):
        for cp in copies(*refs):
            cp.start()

    def finish(*refs):
        for cp in copies(*refs):
            cp.wait()

    return _Exchange(halves, [jax.ShapeDtypeStruct(a.shape, a.dtype) for a in halves], n, start, finish)


N_DEV = 8


def _all_exchange(v):
    r, cols = v.shape

    def peers():
        x, y, c = _me()
        flip = lambda v, f: 1 - v if f else v
        return 4 * x + 2 * y + c, [(flip(x, fx), flip(y, fy), flip(c, fc)) for fx in (0, 1) for fy in (0, 1) for fc in (0, 1)][1:]

    def local(in_refs, out_refs, send, me):
        return pltpu.make_async_copy(in_refs[0], out_refs[0].at[me], send.at[7])

    def start(in_refs, out_refs, send, recv):
        me, others = peers()
        local(in_refs, out_refs, send, me).start()
        for j, peer in enumerate(others):
            pltpu.make_async_remote_copy(src_ref=in_refs[0], dst_ref=out_refs[0].at[me], send_sem=send.at[j],
                                         recv_sem=recv.at[j], device_id=peer, device_id_type=MESH).start()

    def finish(in_refs, out_refs, send, recv):
        me, others = peers()
        for j, (px, py, pc) in enumerate(others):
            pltpu.make_async_remote_copy(src_ref=in_refs[0], dst_ref=out_refs[0].at[4 * px + 2 * py + pc], send_sem=send.at[j],
                                         recv_sem=recv.at[j], device_id=(px, py, pc), device_id_type=MESH).wait()
        local(in_refs, out_refs, send, me).wait()

    return _Exchange([v], [jax.ShapeDtypeStruct((N_DEV, r, cols), v.dtype)], 8, start, finish)


def _sum_slots(a, name):
    n, r, cols = a.shape
    tr = _pick(r, 512) if r % 8 == 0 else r
    for cand in (512, 256, 128, 64, 32, 16, 8):
        if r % cand == 0:
            tr = cand
            break

    def body(a_ref, o_ref):
        acc = a_ref[0]
        for k in range(1, n):
            acc = acc + a_ref[k]
        o_ref[...] = acc

    return pl.pallas_call(
        body, name=name, grid=(r // tr,), in_specs=[pl.BlockSpec((n, tr, cols), lambda i: (0, i, 0))],
        out_specs=pl.BlockSpec((tr, cols), lambda i: (i, 0)), out_shape=jax.ShapeDtypeStruct((r, cols), F32),
        compiler_params=_params("parallel"))(a)


def _join(name, stacked):
    ax = SHARDED[name][1]
    return jnp.concatenate([stacked[k] for k in range(N_CHIPS)], axis=ax)


def _split(name, full):
    ax = SHARDED[name][1]
    return jnp.stack(jnp.split(full, N_CHIPS, axis=ax))


def _heads_pad(a, real, axis):
    shp = a.shape
    a = a.reshape(shp[:axis] + (MLA_HEADS, real) + shp[axis + 1:])
    pad = [(0, 0)] * a.ndim
    pad[axis + 1] = (0, LANES - real)
    a = jnp.pad(a, pad)
    return a.reshape(shp[:axis] + (HP,) + shp[axis + 1:])


def _heads_unpad(a, real, axis):
    shp = a.shape
    a = a.reshape(shp[:axis] + (MLA_HEADS, LANES) + shp[axis + 1:])
    a = lax.slice_in_dim(a, 0, real, axis=axis + 1)
    return a.reshape(shp[:axis] + (MLA_HEADS * real,) + shp[axis + 1:])


def _lane_place(a, start):
    n = a.shape[-1]
    pad = [(0, 0)] * (a.ndim - 1) + [(start, LANES - start - n)]
    return jnp.pad(a, pad)


_O_UV, _O_CQ, _O_CKV, _O_KR, _O_Z, _O_XBC, _O_DT, _O_G = 0, 1024, 1408, 1664, 1696, 2208, 3232, 3240


def _w_in_pad(w):
    sl = lambda a, b: w[:, a:b]
    xs = _heads_pad(sl(_O_XBC, _O_XBC + SSD_INNER), SSD_HEAD_DIM, 1)
    bc = sl(_O_XBC + SSD_INNER, _O_DT)
    main = jnp.concatenate([sl(_O_UV, _O_CQ), _heads_pad(sl(_O_Z, _O_XBC), SSD_HEAD_DIM, 1), xs, sl(_O_G, IN_COLS)], axis=1)
    tail = jnp.concatenate([bc, sl(_O_CKV, _O_KR), sl(_O_CQ, _O_CKV), _lane_place(sl(_O_KR, _O_Z), MLA_NOPE),
                            _lane_place(sl(_O_DT, _O_G), 0), jnp.zeros((w.shape[0], PW_TAIL - T_DT - LANES), w.dtype)], axis=1)
    return main, tail


def _w_in_unpad(gm, gt):
    m = lambda a, n: gm[:, a:a + n]
    t = lambda a, n: gt[:, a:a + n]
    parts = [m(C_UV, 1024), t(T_CQ, MLA_Q_RANK), t(T_CKV, MLA_KV_RANK), t(T_KR + MLA_NOPE, MLA_ROPE),
             _heads_unpad(m(C_Z, HP), SSD_HEAD_DIM, 1), _heads_unpad(m(C_XS, HP), SSD_HEAD_DIM, 1), t(T_BC, BCW),
             t(T_DT, SSD_HEADS), m(C_G, 3 * D_MODEL)]
    return jnp.concatenate(parts, axis=1)


def _xbc_pad(a):
    return jnp.concatenate([_heads_pad(a[..., :SSD_INNER], SSD_HEAD_DIM, a.ndim - 1), a[..., SSD_INNER:]], axis=-1)


def _xbc_unpad(a):
    return jnp.concatenate([_heads_unpad(a[..., :HP], SSD_HEAD_DIM, a.ndim - 1), a[..., HP:]], axis=-1)


def _rope_tables(positions):
    inv_freq = 1.0 / (ROPE_THETA ** (jnp.arange(0, MLA_ROPE, 2, dtype=F32) / MLA_ROPE))
    ang = positions.astype(F32)[:, None] * inv_freq
    cos, sin = jnp.cos(ang), jnp.sin(ang)
    s = positions.shape[0]
    half = MLA_ROPE // 2
    z = lambda n: jnp.zeros((s, n), F32)
    ct = jnp.concatenate([jnp.ones((s, MLA_NOPE), F32), cos, cos, z(LANES - MLA_QK)], axis=1)
    s1 = jnp.concatenate([z(MLA_NOPE), -sin, z(half), z(LANES - MLA_QK)], axis=1)
    s2 = jnp.concatenate([z(MLA_NOPE), z(half), sin, z(LANES - MLA_QK)], axis=1)
    return ct, s1, s2


def _layer_weights(full, small, l, part):
    w = {}
    row = lambda n: small[n][l][None, :]
    stacked = lambda g: g.reshape((N_CHIPS * g.shape[1], g.shape[2]))
    if part in ('ffn1', 'ffn2'):
        w[part + '_w_in'] = full[part + '_w_in']
        w[part + '_w_out'] = stacked(full[part + '_w_out'])
        w[part + '_norm'] = row(part + '_norm')
        return w
    w['w_out'] = stacked(full['w_out'])
    fl = {n: _join(n, full[n]) for n in ('w_in', 'mla_w_uq', 'mla_w_ukv', 'w_branch', 'ssd_conv_w')}
    w['w_in_main'], w['w_in_tail'] = _w_in_pad(fl['w_in'])
    w['wuq'] = _heads_pad(fl['mla_w_uq'], MLA_QK, 1)
    ukv = fl['mla_w_ukv'].reshape(MLA_KV_RANK, MLA_HEADS, MLA_NOPE + MLA_V)
    zero = jnp.zeros((MLA_KV_RANK, MLA_HEADS, LANES - MLA_NOPE), ukv.dtype)
    wk = jnp.concatenate([ukv[:, :, :MLA_NOPE], zero], axis=2).reshape(MLA_KV_RANK, HP)
    wv = jnp.concatenate([ukv[:, :, MLA_NOPE:], zero], axis=2).reshape(MLA_KV_RANK, HP)
    w['wkv'] = jnp.concatenate([wk, wv], axis=1)
    wb = fl['w_branch']
    w['wb0'] = wb[0]
    w['wb1'] = _heads_pad(wb[1], MLA_V, 0)
    w['wb2'] = _heads_pad(wb[2], SSD_HEAD_DIM, 0)
    w['conv_w'] = _xbc_pad(fl['ssd_conv_w'].astype(F32))
    for n in ('mix_norm', 'gm_v_norm', 'mla_q_norm', 'mla_kv_norm'):
        w[n] = row(n)
    w['gm_w_s'] = small['gm_w_s'][l]
    w['gm_b_full'] = jnp.broadcast_to(small['gm_b_s'][l][:, :, None], (GM_GROUPS, CHUNK, LANES))
    w['gq'] = _lane_place(row('mla_q_gain'), 0)
    w['gk'] = _lane_place(row('mla_k_gain'), 0)
    w['conv_b'] = _xbc_pad(row('ssd_conv_b'))
    w['dt_bias'] = _lane_place(row('ssd_dt_bias'), 0)
    w['a_log'] = _lane_place(row('ssd_a_log'), 0)
    w['d_vec'] = jnp.repeat(small['ssd_d'][l], LANES)[None, :]
    w['ssd_norm'] = _heads_pad(row('ssd_norm'), SSD_HEAD_DIM, 1)
    return w


_MIXER_SMALL = ['mla_w_uq', 'mla_w_ukv', 'ssd_conv_w', 'w_branch', 'w_out']
_MIXER_SMALL_G = [n for n in _MIXER_SMALL if n != 'ssd_conv_w']
GATHER_HOSTS = {'attn': ['ffn1_w_in', 'ffn2_w_in'], 'scan': ['ffn1_w_out', 'ffn2_w_out'], 'merge': _MIXER_SMALL, 'ffn2_in': ['w_in']}
GATHER_HOSTS_LATER = {'ffn1_in': ['ffn1_w_out'], 'proj': ['w_in'], 'attn': ['ffn1_w_in', 'ffn2_w_in'], 'scan': ['ffn2_w_out'],
                      'merge': _MIXER_SMALL}
FIRST_NOW = ['ffn1_w_in', 'ffn1_w_out']
FIRST_HOSTS = {'ffn1_in': ['w_in'], 'ffn1_out': _MIXER_SMALL, 'proj': ['ffn2_w_in', 'ffn2_w_out']}
PAIR_HOSTS = {'ffn2_dact': ['ffn1_w_in', 'ffn2_w_in'], 'ffn2_dwin': ['ffn1_w_out', 'w_in', 'ffn2_w_out'] + _MIXER_SMALL_G}
REDUCE_HOSTS = {'dattn_q': ['ffn1_w_out', 'w_in', 'ffn2_w_out'], 'dattn_kv': ['ffn1_w_in', 'ffn2_w_in'], 'dmla_pre': _MIXER_SMALL_G}
LAST_EARLY = ['w_in', 'ffn2_w_in', 'ffn2_w_out'] + _MIXER_SMALL_G
LAST_HOSTS = {'ffn1_dact': ['w_in'], 'ffn1_dwin': ['ffn2_w_in'], 'ffn1_dx': ['ffn2_w_out'] + _MIXER_SMALL_G}
LAST_LATE = ['ffn1_w_in', 'ffn1_w_out']


def _ffn_fwd(x, norm, w4, w_out, tag, sides=None):
    sides = sides or {}
    carried = {}
    (h, gate, up, act), carried[f"{tag}_in"] = _ffn_in(x, norm, w4, f"{tag}_in", sides.get(f"{tag}_in"))
    y, carried[f"{tag}_out"] = _ffn_out(act, w_out, x, f"{tag}_out", sides.get(f"{tag}_out"))
    return y, (x, h, gate, up, act), carried


def _ffn_bwd(dy, saved, norm, w4, w_out, tag, sides=None, after_dwout=None):
    sides = dict(sides or {})
    carried = {}
    x, h, gate, up, act = saved
    dw_out, carried[f"{tag}_dwout"] = _ffn_dwout(act, dy, f"{tag}_dwout", sides.get(f"{tag}_dwout"))
    if after_dwout is not None:
        sides.update(after_dwout(carried[f"{tag}_dwout"]))
    da, carried[f"{tag}_dact"] = _ffn_dact(dy, w_out, gate, up, f"{tag}_dact", sides.get(f"{tag}_dact"))
    dw_in, carried[f"{tag}_dwin"] = _ffn_dwin(h, da, f"{tag}_dwin", sides.get(f"{tag}_dwin"))
    (dx, dnorm), carried[f"{tag}_dx"] = _ffn_dx(da, w4, x, norm, dy, f"{tag}_dx", sides.get(f"{tag}_dx"))
    return dx, dnorm, dw_in, dw_out.reshape((N_CHIPS, 2 * FC // N_CHIPS, D_MODEL)), carried


def _mixer_fwd(x, w, tabs, tag, sides=None):
    sides = sides or {}
    carried = {}
    (h, pm, pt), carried['proj'] = _mixer_proj(x, w['mix_norm'], w['w_in_main'], w['w_in_tail'], f"{tag}_proj", sides.get('proj'))
    ya = _gmlp_fwd(pm, w['gm_v_norm'], w['gm_w_s'], w['gm_b_full'], f"{tag}_gmlp")
    q, k, v = _mla_pre_fwd(pt, tabs, w['mla_q_norm'], w['mla_kv_norm'], w['wuq'], w['wkv'], w['gq'], w['gk'], f"{tag}_mla_pre")
    (o, lse), carried['attn'] = _attn_fwd(q, k, v, f"{tag}_attn", sides.get('attn'))
    xs = _conv_fwd(pm, C_XS, HP, w['conv_w'][:, :HP], w['conv_b'][:, :HP], f"{tag}_conv_x")
    bc = _conv_fwd(pt, T_BC, BCW, w['conv_w'][:, HP:], w['conv_b'][:, HP:], f"{tag}_conv_bc")
    dtb, dab = _dt_fwd(pt, w['dt_bias'], w['a_log'], f"{tag}_dt")
    (ys, s_in), carried['scan'] = _scan_fwd(xs, bc, dtb, dab, f"{tag}_scan", sides.get('scan'))
    yc = _ssd_post_fwd(ys, xs, pm, w['d_vec'], w['ssd_norm'], f"{tag}_ssd_post")
    (mg, y), carried['merge'] = _merge_fwd(pm, ya, o, yc, w['wb0'], w['wb1'], w['wb2'], w['w_out'], x, f"{tag}_merge",
                                           sides.get('merge'))
    return y, (x, h, pm, pt, ya, q, k, v, o, lse, xs, bc, dtb, dab, ys, s_in, yc, mg), carried


def _pair_sums(pending, got):
    return {n: _pair_add(pending[n], got[n], f"pair_add_{n}") for n in got}


def _chip_sums(sums, arrived, layer, stacked):
    for n in arrived:
        stacked[n] = _chip_add(sums[n][1], arrived[n], f"chip_add_{n}", layer, stacked.get(n))


def _reduce_to_chip(pending, layer, stacked):
    names = list(pending)
    got = _run_exchange(_pair_exchange([pending[n] for n in names]), "pair_exchange")
    sums = _pair_sums(pending, dict(zip(names, got)))
    arrived = _run_exchange(_chip_exchange([sums[n][0] for n in names]), "chip_exchange")
    _chip_sums(sums, dict(zip(names, arrived)), layer, stacked)


def _mixer_bwd(dy, saved, w, tabs, tag, sides=None):
    sides = sides or {}
    carried = {}
    x, h, pm, pt, ya, q, k, v, o, lse, xs, bc, dtb, dab, ys, s_in, yc, mg = saved
    g = {}
    g['w_out'] = _matmul(mg, dy, ta=True, name=f"{tag}_dwout").reshape((N_CHIPS, D_MODEL // N_CHIPS, D_MODEL))
    d0, d1, d2, dgates, dya, do, dyc = _merge_bwd(pm, ya, o, yc, w['wb0'], w['wb1'], w['wb2'], w['w_out'], dy, f"{tag}_dmerge")
    dwb0 = _matmul(ya, d0, ta=True, name=f"{tag}_dwb0")
    dwb1 = _matmul(o, d1, ta=True, name=f"{tag}_dwb1")
    dwb2 = _matmul(yc, d2, ta=True, name=f"{tag}_dwb2")
    g['w_branch'] = _split('w_branch', jnp.stack([dwb0, _heads_unpad(dwb1, MLA_V, 0), _heads_unpad(dwb2, SSD_HEAD_DIM, 0)]))
    duv, g['gm_v_norm'], g['gm_w_s'], db = _gmlp_bwd(pm, w['gm_v_norm'], w['gm_w_s'], w['gm_b_full'], dya, f"{tag}_dgmlp")
    g['gm_b_s'] = db.T
    (dq, delta), carried['dattn_q'] = _attn_bwd_dq(q, k, v, o, lse, do, f"{tag}_dattn_q", sides.get('dattn_q'))
    (dk, dv), carried['dattn_kv'] = _attn_bwd_dkv(q, k, v, lse, delta, do, f"{tag}_dattn_kv", sides.get('dattn_kv'))
    (dcq, dckv, dkr, dwuq, dwkv, g['mla_q_norm'], g['mla_kv_norm'], dgq, dgk), carried['dmla_pre'] = _mla_pre_bwd(
        pt, tabs, w['mla_q_norm'], w['mla_kv_norm'], w['wuq'], w['wkv'], w['gq'], w['gk'], dq, dk, dv, f"{tag}_dmla_pre",
        sides.get('dmla_pre'))
    g['mla_w_uq'] = _split('mla_w_uq', _heads_unpad(dwuq, MLA_QK, 1))
    dwk = dwkv[:, :HP].reshape(MLA_KV_RANK, MLA_HEADS, LANES)[:, :, :MLA_NOPE]
    dwv = dwkv[:, HP:].reshape(MLA_KV_RANK, MLA_HEADS, LANES)[:, :, :MLA_V]
    g['mla_w_ukv'] = _split('mla_w_ukv', jnp.concatenate([dwk, dwv], axis=2).reshape(MLA_KV_RANK, MLA_HEADS * (MLA_NOPE + MLA_V)))
    g['mla_q_gain'], g['mla_k_gain'] = dgq[:, :MLA_QK], dgk[:, :MLA_QK]
    dys, dz, dssd_norm, dd = _ssd_post_bwd(ys, xs, pm, w['d_vec'], w['ssd_norm'], dyc, f"{tag}_dssd_post")
    g['ssd_norm'] = _heads_unpad(dssd_norm, SSD_HEAD_DIM, 1)
    g['ssd_d'] = jnp.sum(dd.reshape(SSD_HEADS, LANES), axis=1)[None, :]
    dxs, dbm, dcm, dda, ddtx = _scan_bwd(xs, bc, dtb, dab, s_in, dys, w['d_vec'], f"{tag}_dscan")
    dxs16, dcw_x, dcb_x = _conv_bwd(pm, C_XS, HP, w['conv_w'][:, :HP], w['conv_b'][:, :HP], dxs, f"{tag}_dconv_x")
    dbc16, dcw_bc, dcb_bc = _conv_bwd(pt, T_BC, BCW, w['conv_w'][:, HP:], w['conv_b'][:, HP:],
                                      jnp.concatenate([dbm, dcm], axis=1), f"{tag}_dconv_bc")
    g['ssd_conv_w'] = _xbc_unpad(jnp.concatenate([dcw_x, dcw_bc], axis=1))
    g['ssd_conv_b'] = _xbc_unpad(jnp.concatenate([dcb_x, dcb_bc], axis=1))
    ddt, dbias, dalog = _dt_bwd(pt, w['dt_bias'], w['a_log'], dda, ddtx, f"{tag}_ddt")
    g['ssd_dt_bias'], g['ssd_a_log'] = dbias[:, :SSD_HEADS], dalog[:, :SSD_HEADS]
    s = x.shape[0]
    dpm = jnp.concatenate([duv, dz, dxs16, dgates], axis=1)
    dpt = jnp.concatenate([dbc16, dckv, dcq, dkr, ddt, jnp.zeros((s, PW_TAIL - T_DT - LANES), BF16)], axis=1)
    g['w_in'] = _split('w_in', _w_in_unpad(_matmul(h, dpm, ta=True, name=f"{tag}_dwin_main"),
                                           _matmul(h, dpt, ta=True, name=f"{tag}_dwin_tail")))
    dx, g['mix_norm'] = _mixer_dx(dpm, dpt, w['w_in_main'], w['w_in_tail'], x, w['mix_norm'], dy, f"{tag}_dx")
    return dx, g, carried


_CONV_ROWS = 32


def _rows_cols(a, lead):
    return a.reshape(a.shape[:lead] + (int(np.prod(a.shape[lead:-1])), a.shape[-1]))


def _shard_views(wts):
    views = []
    for n in SHARDED_ORDER:
        a = _rows_cols(wts[n].astype(BF16), 1)
        if n == 'ssd_conv_w':
            a = jnp.pad(a, ((0, 0), (0, _CONV_ROWS - a.shape[1]), (0, 0)))
        views.append(a)
    return views


def _gathered(names, arrays):
    out = {}
    for n, a in zip(names, arrays):
        shp = _shard_shape(n)
        if n == 'ssd_conv_w':
            a = a[:, :shp[0]]
        out[n] = a.reshape((N_CHIPS,) + shp)
    return out


def _local_step(x, positions, target, weights, small, distributed=True):
    tabs = _rope_tables(positions)
    views = dict(zip(SHARDED_ORDER, weights)) if distributed else None
    plan = [{} for _ in range(DEPTH)]
    if distributed:
        for l in range(DEPTH - 1):
            plan[l].update({host: (names, l + 1) for host, names in (GATHER_HOSTS if l == 0 else GATHER_HOSTS_LATER).items()})
        plan[0].update({host: (names, 0) for host, names in FIRST_HOSTS.items()})
        have = [dict() for _ in range(DEPTH)]
        have[0].update(_gathered(FIRST_NOW, _run_exchange(_gather_exchange([views[n] for n in FIRST_NOW], 0), "gather_first")))
    else:
        have = weights

    def absorb(l, carried):
        for host, arrays in carried.items():
            if host in plan[l]:
                names, layer = plan[l][host]
                have[layer].update(_gathered(names, arrays))

    ws, saved = [], []
    for l in range(DEPTH):
        sides = {host: _gather_exchange([views[n] for n in names], layer) for host, (names, layer) in plan[l].items()}
        w = _layer_weights(have[l], small, l, 'ffn1')
        x, s1, carried = _ffn_fwd(x, w['ffn1_norm'], w['ffn1_w_in'], w['ffn1_w_out'], "ffn1", sides)
        absorb(l, carried)
        w.update(_layer_weights(have[l], small, l, 'mixer'))
        x, s2, carried = _mixer_fwd(x, w, tabs, "mix", sides)
        absorb(l, carried)
        w.update(_layer_weights(have[l], small, l, 'ffn2'))
        x, s3, carried = _ffn_fwd(x, w['ffn2_norm'], w['ffn2_w_in'], w['ffn2_w_out'], "ffn2", sides)
        absorb(l, carried)
        ws.append(w)
        saved.append((s1, s2, s3))
    dy, sq = _loss_head(x, target, "loss_head")
    loss = 0.5 * jnp.sum(sq) / D_MODEL
    grads, reduced, pending = [None] * DEPTH, {}, None

    def chip_sides(sums, hosts):
        return {host: _chip_exchange([sums[n][0] for n in names]) for host, names in hosts.items()}

    def arrivals(carried, hosts):
        return {n: a for host, names in hosts.items() for n, a in zip(names, carried[host])}

    for l in reversed(range(DEPTH)):
        w = ws[l]
        s1, s2, s3 = saved[l]
        sides = {host: _pair_exchange([pending[n] for n in names]) for host, names in PAIR_HOSTS.items()} if pending else {}
        dy, dn2, dwi2, dwo2, carried = _ffn_bwd(dy, s3, w['ffn2_norm'], w['ffn2_w_in'], w['ffn2_w_out'], "ffn2", sides)
        sides = {}
        if pending:
            sums = _pair_sums(pending, arrivals(carried, PAIR_HOSTS))
            sides = chip_sides(sums, REDUCE_HOSTS)
        dy, g, carried = _mixer_bwd(dy, s2, w, tabs, "mix", sides)
        if pending:
            _chip_sums(sums, arrivals(carried, REDUCE_HOSTS), l + 1, reduced)
        g.update(ffn2_norm=dn2, ffn2_w_in=dwi2, ffn2_w_out=dwo2)
        last = distributed and l == 0
        if last:
            early = {n: _rows_cols(g[n], 1) for n in LAST_EARLY}
            after = {}

            def after_dwout(got):
                after['sums'] = _pair_sums(early, dict(zip(LAST_EARLY, got)))
                return chip_sides(after['sums'], LAST_HOSTS)

            dy, dn1, dwi1, dwo1, carried = _ffn_bwd(dy, s1, w['ffn1_norm'], w['ffn1_w_in'], w['ffn1_w_out'], "ffn1",
                                                    {'ffn1_dwout': _pair_exchange([early[n] for n in LAST_EARLY])}, after_dwout)
            _chip_sums(after['sums'], arrivals(carried, LAST_HOSTS), 0, reduced)
        else:
            dy, dn1, dwi1, dwo1, _ = _ffn_bwd(dy, s1, w['ffn1_norm'], w['ffn1_w_in'], w['ffn1_w_out'], "ffn1")
        g.update(ffn1_norm=dn1, ffn1_w_in=dwi1, ffn1_w_out=dwo1)
        grads[l] = g
        if distributed:
            pending = {n: _rows_cols(g[n], 1) for n in REDUCED}
    if distributed:
        _reduce_to_chip({n: pending[n] for n in LAST_LATE}, 0, reduced)
    return loss, dy, grads, reduced


SMALL_PACK = SMALL_ORDER + ['ssd_conv_w']
_SMALL_ROW_TILE = 256


def _pack_small(per_layer_rows, tail=None):
    parts = [per_layer_rows[l][n].reshape(-1).astype(F32) for l in range(DEPTH) for n in SMALL_PACK]
    if tail is not None:
        parts.append(tail.reshape(1))
    flat = jnp.concatenate(parts)
    rows = -(-flat.shape[0] // LANES)
    rows = -(-rows // _SMALL_ROW_TILE) * _SMALL_ROW_TILE
    return jnp.pad(flat, (0, rows * LANES - flat.shape[0])).reshape(rows, LANES)


def _unpack_small(buf, shapes):
    flat = buf.reshape(-1)
    off = 0
    out = {n: [] for n in SMALL_PACK}
    for l in range(DEPTH):
        for n in SMALL_PACK:
            size = int(np.prod(shapes[n]))
            out[n].append(flat[off:off + size].reshape(shapes[n]))
            off += size
    return {n: jnp.stack(v) for n, v in out.items()}


def kernel(x, positions, ffn1_norm, ffn1_w_in, ffn1_w_out, mix_norm, w_in, gm_v_norm, gm_w_s, gm_b_s, mla_q_norm, mla_kv_norm, mla_w_uq, mla_w_ukv, mla_q_gain, mla_k_gain, ssd_conv_w, ssd_conv_b, ssd_dt_bias, ssd_a_log, ssd_d, ssd_norm, w_branch, w_out, ffn2_norm, ffn2_w_in, ffn2_w_out, loss_target, m_ffn1_norm, m_ffn1_w_in, m_ffn1_w_out, m_mix_norm, m_w_in, m_gm_v_norm, m_gm_w_s, m_gm_b_s, m_mla_q_norm, m_mla_kv_norm, m_mla_w_uq, m_mla_w_ukv, m_mla_q_gain, m_mla_k_gain, m_ssd_conv_w, m_ssd_conv_b, m_ssd_dt_bias, m_ssd_a_log, m_ssd_d, m_ssd_norm, m_w_branch, m_w_out, m_ffn2_norm, m_ffn2_w_in, m_ffn2_w_out, v_ffn1_norm, v_ffn1_w_in, v_ffn1_w_out, v_mix_norm, v_w_in, v_gm_v_norm, v_gm_w_s, v_gm_b_s, v_mla_q_norm, v_mla_kv_norm, v_mla_w_uq, v_mla_w_ukv, v_mla_q_gain, v_mla_k_gain, v_ssd_conv_w, v_ssd_conv_b, v_ssd_dt_bias, v_ssd_a_log, v_ssd_d, v_ssd_norm, v_w_branch, v_w_out, v_ffn2_norm, v_ffn2_w_in, v_ffn2_w_out):
    wts = dict(zip(WEIGHTS, (ffn1_norm, ffn1_w_in, ffn1_w_out, mix_norm, w_in, gm_v_norm, gm_w_s, gm_b_s, mla_q_norm, mla_kv_norm,
                             mla_w_uq, mla_w_ukv, mla_q_gain, mla_k_gain, ssd_conv_w, ssd_conv_b, ssd_dt_bias, ssd_a_log, ssd_d,
                             ssd_norm, w_branch, w_out, ffn2_norm, ffn2_w_in, ffn2_w_out)))
    mom = dict(zip(WEIGHTS, (m_ffn1_norm, m_ffn1_w_in, m_ffn1_w_out, m_mix_norm, m_w_in, m_gm_v_norm, m_gm_w_s, m_gm_b_s, m_mla_q_norm,
                             m_mla_kv_norm, m_mla_w_uq, m_mla_w_ukv, m_mla_q_gain, m_mla_k_gain, m_ssd_conv_w, m_ssd_conv_b,
                             m_ssd_dt_bias, m_ssd_a_log, m_ssd_d, m_ssd_norm, m_w_branch, m_w_out, m_ffn2_norm, m_ffn2_w_in,
                             m_ffn2_w_out)))
    var = dict(zip(WEIGHTS, (v_ffn1_norm, v_ffn1_w_in, v_ffn1_w_out, v_mix_norm, v_w_in, v_gm_v_norm, v_gm_w_s, v_gm_b_s, v_mla_q_norm,
                             v_mla_kv_norm, v_mla_w_uq, v_mla_w_ukv, v_mla_q_gain, v_mla_k_gain, v_ssd_conv_w, v_ssd_conv_b,
                             v_ssd_dt_bias, v_ssd_a_log, v_ssd_d, v_ssd_norm, v_w_branch, v_w_out, v_ffn2_norm, v_ffn2_w_in,
                             v_ffn2_w_out)))
    cx, cy, _ = _me()
    mychip = _chip_index(cx, cy)

    small = {n: wts[n] for n in SMALL_ORDER}
    loss_part, dx, grads, reduced = _local_step(x[0], positions[0], loss_target[0], _shard_views(wts), small)
    rows_cols = _rows_cols
    halves = [reduced[n] for n in REDUCED]
    everyone = _all_exchange(_pack_small(grads, tail=loss_part))
    partials, *theirs = _run_exchange(_together(everyone, _pair_share(halves)), "final_exchange")
    grad, delta, new_m, new_v = {}, {}, {}, {}
    for n, a, b in zip(REDUCED, halves, theirs):
        shp = wts[n].shape
        outs, _ = _adamw_sharded(rows_cols(wts[n], 1), rows_cols(mom[n], 1), rows_cols(var[n], 1), a, b, f"adamw_{n}")
        grad[n], delta[n], new_m[n], new_v[n] = [o.reshape(shp) for o in outs]
    shapes = {n: wts[n].shape[1:] for n in SMALL_ORDER}
    shapes['ssd_conv_w'] = SHARDED['ssd_conv_w'][0]
    summed = _sum_slots(partials, "small_sum")
    small_g = _unpack_small(summed, shapes)
    loss = summed.reshape(-1)[DEPTH * sum(int(np.prod(shapes[n])) for n in SMALL_PACK)]
    conv_full = small_g.pop('ssd_conv_w')
    shard_cols = _shard_shape('ssd_conv_w')[1]
    small_g['ssd_conv_w'] = lax.dynamic_slice_in_dim(conv_full, mychip * shard_cols, shard_cols, axis=2)
    shapes['ssd_conv_w'] = _shard_shape('ssd_conv_w')

    per_layer = lambda t: [{n: t[n][l] for n in SMALL_PACK} for l in range(DEPTH)]
    d, nm, nv = _adamw(_pack_small(per_layer(wts)), _pack_small(per_layer(small_g)), _pack_small(per_layer(mom)),
                       _pack_small(per_layer(var)), "adamw_small")
    sd, snm, snv = _unpack_small(d, shapes), _unpack_small(nm, shapes), _unpack_small(nv, shapes)
    for n in SMALL_PACK:
        grad[n], delta[n], new_m[n], new_v[n] = small_g[n], sd[n], snm[n], snv[n]
    return (loss, dx[None], *[grad[n] for n in WEIGHTS], *[delta[n] for n in WEIGHTS], *[new_m[n] for n in WEIGHTS],
            *[new_v[n] for n in WEIGHTS])
```
